```python
import math
import jax, jax.numpy as jnp
from jax import lax
import numpy as np

D_MODEL = 1024
BATCH = 8
SEQ = 8192
DEPTH = 2

RMS_EPS = 1e-6
N_MIX_GROUPS = 4
GROUP_W = D_MODEL // N_MIX_GROUPS
D_MIX = N_MIX_GROUPS * GROUP_W

HGRN_HEADS = 4
HGRN_DK = GROUP_W // HGRN_HEADS
HGRN_DV = GROUP_W // HGRN_HEADS
HGRN_CHUNK = 64

S5_CH = 16
S5_GROUPS = GROUP_W // S5_CH
S5_STATE = 64
S5_DT_MIN = 1e-3
S5_DT_MAX = 1e-1

RWKV_HS = 64
RWKV_HEADS = GROUP_W // RWKV_HS
RWKV_W_LORA = 64
RWKV_A_LORA = 64
RWKV_V_LORA = 32
RWKV_G_LORA = 128
RWKV_GN_EPS = 64e-5
RWKV_SIZES = (GROUP_W, GROUP_W, GROUP_W, RWKV_W_LORA, RWKV_A_LORA, RWKV_G_LORA)
RWKV_COLS = 3 * GROUP_W + RWKV_W_LORA + RWKV_A_LORA + RWKV_G_LORA

LRU_HEADS = 4
LRU_HD = GROUP_W // LRU_HEADS
LRU_CONV = 4
LRU_C = 8.0

MOE_GROUPS = 4
MOE_PER_GROUP = 8
N_EXPERTS = MOE_GROUPS * MOE_PER_GROUP
MOE_TOPK = 2
D_EXPERT = 512

COL_SIZES = (GROUP_W, GROUP_W, GROUP_W, GROUP_W, GROUP_W, RWKV_COLS, GROUP_W, GROUP_W)
D_IN = 7 * GROUP_W + RWKV_COLS

kernel_name = "hybrid_parallel_heads_hier_moe"


def _splits(sizes):
    return np.cumsum(np.array(sizes))[:-1].tolist()


def rms_norm(x, w, eps=RMS_EPS):
    xf = x.astype(jnp.float32)
    y = xf * lax.rsqrt(jnp.mean(xf * xf, axis=-1, keepdims=True) + eps)
    return (y * w.astype(jnp.float32)).astype(x.dtype)


def _shift(p):
    return jnp.pad(p, ((0, 0), (1, 0), (0, 0)))[:, :-1]


def hgrn2_mixer(q_raw, f_raw, i_raw, g_raw, lb, norm_w):
    f32 = jnp.float32
    B, T, _ = q_raw.shape
    n_chunks = T // HGRN_CHUNK
    q = jax.nn.silu(q_raw.astype(f32))
    fx = f_raw.astype(f32)
    log_f = jnp.logaddexp(jnp.log(lb), jnp.log1p(-lb) + jax.nn.log_sigmoid(fx))
    k = (1.0 - lb) * jax.nn.sigmoid(-fx)
    v = jax.nn.silu(i_raw.astype(f32))

    def to_chunks(t, d):
        return t.reshape(B, n_chunks, HGRN_CHUNK, HGRN_HEADS, d).transpose(1, 0, 3, 2, 4)

    tri = jnp.tril(jnp.ones((HGRN_CHUNK, HGRN_CHUNK), dtype=bool))

    def chunk_step(S, inp):
        qc, lfc, kc, vc = inp
        b = jnp.cumsum(lfc, axis=2)
        rel = b[:, :, :, None, :] - b[:, :, None, :, :]
        decay = jnp.exp(jnp.where(tri[:, :, None], rel, -jnp.inf))
        scores = jnp.einsum('bhtk,bhsk,bhtsk->bhts', qc, kc, decay)
        o = (jnp.einsum('bhts,bhsv->bhtv', scores, vc)
             + jnp.einsum('bhtk,bhkv->bhtv', qc * jnp.exp(b), S))
        b_end = b[:, :, -1:, :]
        S = (jnp.exp(b_end[:, :, 0, :])[..., None] * S
             + jnp.einsum('bhsk,bhsv->bhkv', kc * jnp.exp(b_end - b), vc))
        return S, o

    S0 = jnp.zeros((B, HGRN_HEADS, HGRN_DK, HGRN_DV), f32)
    _, o = lax.scan(chunk_step, S0, (to_chunks(q, HGRN_DK), to_chunks(log_f, HGRN_DK),
                                     to_chunks(k, HGRN_DK), to_chunks(v, HGRN_DV)))
    o = o.transpose(1, 0, 3, 2, 4).reshape(B, T, HGRN_HEADS, HGRN_DV)
    o = o * lax.rsqrt(jnp.mean(o * o, axis=-1, keepdims=True) + RMS_EPS)
    o = o * norm_w.astype(f32).reshape(HGRN_HEADS, HGRN_DV)
    o = o.reshape(B, T, GROUP_W) * jax.nn.silu(g_raw.astype(f32))
    return o.astype(q_raw.dtype)


def s5_mixer(u, lam_re, lam_im, log_dt, b_re, b_im, c_re, c_im, d_skip, w_glu):
    f32 = jnp.float32
    B, T, _ = u.shape
    uf = u.astype(f32)
    ug = uf.reshape(B, T, S5_GROUPS, S5_CH)
    lr, li = lam_re.astype(f32), lam_im.astype(f32)
    dt = jnp.exp(log_dt.astype(f32))[:, None]
    mag = jnp.exp(lr * dt)
    a_re, a_im = mag * jnp.cos(li * dt), mag * jnp.sin(li * dt)
    den = lr * lr + li * li
    kap_re = ((a_re - 1.0) * lr + a_im * li) / den
    kap_im = (a_im * lr - (a_re - 1.0) * li) / den
    br, bi = b_re.astype(f32), b_im.astype(f32)
    bb_re = kap_re[..., None] * br - kap_im[..., None] * bi
    bb_im = kap_re[..., None] * bi + kap_im[..., None] * br
    bu_re = jnp.einsum('btgc,gpc->tbgp', ug, bb_re)
    bu_im = jnp.einsum('btgc,gpc->tbgp', ug, bb_im)
    a_re_t = jnp.broadcast_to(a_re, (T, 1, S5_GROUPS, S5_STATE))
    a_im_t = jnp.broadcast_to(a_im, (T, 1, S5_GROUPS, S5_STATE))

    def combine(e1, e2):
        a1r, a1i, b1r, b1i = e1
        a2r, a2i, b2r, b2i = e2
        return (a2r * a1r - a2i * a1i, a2r * a1i + a2i * a1r,
                a2r * b1r - a2i * b1i + b2r, a2r * b1i + a2i * b1r + b2i)

    _, _, x_re, x_im = lax.associative_scan(combine, (a_re_t, a_im_t, bu_re, bu_im), axis=0)
    y = (jnp.einsum('tbgp,gcp->btgc', x_re, c_re.astype(f32))
         - jnp.einsum('tbgp,gcp->btgc', x_im, c_im.astype(f32))).reshape(B, T, GROUP_W)
    y = jax.nn.gelu(y + d_skip * uf)
    za, zb = jnp.split(y @ w_glu.astype(f32), 2, axis=-1)
    return (za * jax.nn.sigmoid(zb)).astype(u.dtype)


def rwkv7_mixer(p, mu, w0, w2, a0, a2, g2, k_k, k_a, r_k, ln_w, ln_b, v_first, v_mix):
    f32 = jnp.float32
    B, T, _ = p.shape
    pf = p.astype(f32)
    pf = pf + mu * (_shift(pf) - pf)
    r, k, v, wd, ad, gd = jnp.split(pf, _splits(RWKV_SIZES), axis=-1)
    w = -jax.nn.softplus(-(w0 + jnp.tanh(wd) @ w2)) - 0.5
    decay = jnp.exp(-jnp.exp(w))
    a = jax.nn.sigmoid(a0 + ad @ a2)
    g = jax.nn.sigmoid(gd) @ g2

    def heads(t):
        return t.reshape(B, T, RWKV_HEADS, RWKV_HS)

    kk = heads(k * k_k)
    kk = kk / jnp.maximum(jnp.sqrt(jnp.sum(kk * kk, axis=-1, keepdims=True)), 1e-12)
    k = k * (1.0 + (a - 1.0) * k_a)
    if v_mix is None:
        v_first = v
    else:
        v0, v1, v2 = v_mix
        v = v + (v_first - v) * jax.nn.sigmoid(v0 + (v @ v1) @ v2)
    rh, kh, vh = heads(r), heads(k), heads(v)
    bh = kk * heads(a)

    def tm(t):
        return t.transpose(1, 0, 2, 3)

    def step(S, inp):
        r_t, w_t, k_t, v_t, kk_t, b_t = inp
        sa = jnp.einsum('bhvk,bhk->bhv', S, -kk_t)
        S = (S * w_t[:, :, None, :] + sa[..., None] * b_t[:, :, None, :]
             + v_t[..., None] * k_t[:, :, None, :])
        return S, jnp.einsum('bhvk,bhk->bhv', S, r_t)

    S0 = jnp.zeros((B, RWKV_HEADS, RWKV_HS, RWKV_HS), f32)
    _, y = lax.scan(step, S0, (tm(rh), tm(heads(decay)), tm(kh), tm(vh), tm(kk), tm(bh)))
    y = y.transpose(1, 0, 2, 3)
    mean = jnp.mean(y, axis=-1, keepdims=True)
    var = jnp.mean((y - mean) ** 2, axis=-1, keepdims=True)
    y = ((y - mean) * lax.rsqrt(var + RWKV_GN_EPS) * ln_w.reshape(RWKV_HEADS, RWKV_HS)
         + ln_b.reshape(RWKV_HEADS, RWKV_HS))
    y = y + jnp.sum(rh * kh * r_k.reshape(RWKV_HEADS, RWKV_HS), axis=-1, keepdims=True) * vh
    out = y.reshape(B, T, GROUP_W) * g
    return out.astype(p.dtype), v_first


def rglru_mixer(xg, xr, conv_w, conv_b, wa, ba, wx, bx, lam):
    f32 = jnp.float32
    B, T, _ = xr.shape
    xc = lax.conv_general_dilated(xr.astype(f32), conv_w.astype(f32)[:, None, :], window_strides=(1,),
                                  padding=[(LRU_CONV - 1, 0)], dimension_numbers=('NWC', 'WIO', 'NWC'),
                                  feature_group_count=GROUP_W) + conv_b
    xh = xc.reshape(B, T, LRU_HEADS, LRU_HD)
    r = jax.nn.sigmoid(jnp.einsum('bthi,hij->bthj', xh, wa).reshape(B, T, GROUP_W) + ba)
    gi = jax.nn.sigmoid(jnp.einsum('bthi,hij->bthj', xh, wx).reshape(B, T, GROUP_W) + bx)
    log_a = -LRU_C * r * jax.nn.softplus(-lam)
    u = jnp.sqrt(-jnp.expm1(2.0 * log_a)) * (gi * xc)

    def combine(e1, e2):
        a1, b1 = e1
        a2, b2 = e2
        return (a1 * a2, a2 * b1 + b2)

    _, h = lax.associative_scan(combine, (jnp.exp(log_a), u), axis=1)
    return (jax.nn.gelu(xg.astype(f32)) * h).astype(xg.dtype)


def hier_moe(h, coarse_w, coarse_b, fine_w, fine_b, w_gate, w_up, w_down):
    B, T, D = h.shape
    xt = h.reshape(B * T, D)
    coarse = (xt @ coarse_w + coarse_b).astype(jnp.float32)
    p_coarse = jax.nn.softmax(coarse, axis=-1)
    gsel = jnp.argmax(coarse, axis=-1)
    g_onehot = jax.nn.one_hot(gsel, MOE_GROUPS, dtype=jnp.float32)
    p_g = jnp.sum(p_coarse * g_onehot, axis=-1, keepdims=True)
    fine = (jnp.einsum('nd,gde->nge', xt, fine_w) + fine_b).astype(jnp.float32)
    fine_sel = jnp.einsum('ng,nge->ne', g_onehot, fine)
    top_v, top_i = lax.top_k(fine_sel, MOE_TOPK)
    w_top = p_g * jax.nn.softmax(top_v, axis=-1)
    expert_ids = gsel[:, None] * MOE_PER_GROUP + top_i
    combine = jnp.einsum('nk,nke->ne', w_top,
                         jax.nn.one_hot(expert_ids, N_EXPERTS, dtype=jnp.float32)).astype(xt.dtype)
    y = jnp.zeros_like(xt)
    for e in range(N_EXPERTS):
        he = jax.nn.silu(xt @ w_gate[e]) * (xt @ w_up[e])
        y = y + combine[:, e:e + 1] * (he @ w_down[e])
    return y.reshape(B, T, D)


def setup_inputs(seed: int = 0) -> dict:
    key = jax.random.key(seed)
    ks = iter(jax.random.split(key, 64))
    L, D, G = DEPTH, D_MODEL, GROUP_W

    def nrm(shape, scale):
        return scale * jax.random.normal(next(ks), shape, jnp.float32)

    def gain(shape):
        return 1.0 + nrm(shape, 0.02)

    def unif(shape, lo, hi):
        return jax.random.uniform(next(ks), shape, jnp.float32, minval=lo, maxval=hi)

    inputs = {
        "x": nrm((BATCH, SEQ, D), 1.0),
        "norm1_w": gain((L, D)),
        "w_in": nrm((L, D, D_IN), D ** -0.5),
        "hgrn_lb_logits": nrm((L, G), 0.5),
        "hgrn_norm_w": gain((L, G)),
        "s5_lambda_re": -0.5 + nrm((L, S5_GROUPS, S5_STATE), 0.01),
        "s5_lambda_im": jnp.pi * jnp.arange(S5_STATE, dtype=jnp.float32) + nrm((L, S5_GROUPS, S5_STATE), 0.01),
        "s5_log_dt": unif((L, S5_GROUPS), math.log(S5_DT_MIN), math.log(S5_DT_MAX)),
        "s5_b_re": nrm((L, S5_GROUPS, S5_STATE, S5_CH), (2 * S5_CH) ** -0.5),
        "s5_b_im": nrm((L, S5_GROUPS, S5_STATE, S5_CH), (2 * S5_CH) ** -0.5),
        "s5_c_re": nrm((L, S5_GROUPS, S5_CH, S5_STATE), S5_STATE ** -0.5),
        "s5_c_im": nrm((L, S5_GROUPS, S5_CH, S5_STATE), S5_STATE ** -0.5),
        "s5_d": nrm((L, G), 1.0),
        "s5_w_glu": nrm((L, G, 2 * G), G ** -0.5),
        "rwkv_mu": unif((L, RWKV_COLS), 0.0, 1.0),
        "rwkv_w0": unif((L, G), -6.0, 1.0),
        "rwkv_w2": nrm((L, RWKV_W_LORA, G), RWKV_W_LORA ** -0.5),
        "rwkv_a0": nrm((L, G), 0.1),
        "rwkv_a2": nrm((L, RWKV_A_LORA, G), RWKV_A_LORA ** -0.5),
        "rwkv_g2": nrm((L, RWKV_G_LORA, G), RWKV_G_LORA ** -0.5),
        "rwkv_k_k": 0.85 + nrm((L, G), 0.02),
        "rwkv_k_a": gain((L, G)),
        "rwkv_r_k": nrm((L, G), 0.1),
        "rwkv_v0": 1.0 + nrm((L - 1, G), 0.1),
        "rwkv_v1": nrm((L - 1, G, RWKV_V_LORA), G ** -0.5),
        "rwkv_v2": nrm((L - 1, RWKV_V_LORA, G), RWKV_V_LORA ** -0.5),
        "rwkv_ln_w": gain((L, G)),
        "rwkv_ln_b": nrm((L, G), 0.02),
        "lru_conv_w": nrm((L, LRU_CONV, G), LRU_CONV ** -0.5),
        "lru_conv_b": nrm((L, G), 0.02),
        "lru_wa": nrm((L, LRU_HEADS, LRU_HD, LRU_HD), LRU_HD ** -0.5),
        "lru_ba": nrm((L, G), 0.02),
        "lru_wx": nrm((L, LRU_HEADS, LRU_HD, LRU_HD), LRU_HD ** -0.5),
        "lru_bx": nrm((L, G), 0.02),
        "lru_lambda": None,
        "merge_gain": None,
    }
    a_lru = unif((L, G), 0.9, 0.999) ** (1.0 / LRU_C)
    inputs["lru_lambda"] = jnp.log(a_lru) - jnp.log1p(-a_lru)
    inputs["merge_gain"] = gain((L, D_MIX))
    inputs["w_out"] = nrm((L, D_MIX, D), D_MIX ** -0.5)
    inputs["norm2_w"] = gain((L, D))
    inputs["moe_coarse_w"] = nrm((L, D, MOE_GROUPS), D ** -0.5)
    inputs["moe_coarse_b"] = nrm((L, MOE_GROUPS), 0.01)
    inputs["moe_fine_w"] = nrm((L, MOE_GROUPS, D, MOE_PER_GROUP), D ** -0.5)
    inputs["moe_fine_b"] = nrm((L, MOE_GROUPS, MOE_PER_GROUP), 0.01)
    inputs["moe_w_gate"] = nrm((L, N_EXPERTS, D, D_EXPERT), D ** -0.5)
    inputs["moe_w_up"] = nrm((L, N_EXPERTS, D, D_EXPERT), D ** -0.5)
    inputs["moe_w_down"] = nrm((L, N_EXPERTS, D_EXPERT, D), D_EXPERT ** -0.5)
    inputs["final_norm_w"] = gain((D,))
    return inputs


def reference(x, norm1_w, w_in, hgrn_lb_logits, hgrn_norm_w, s5_lambda_re, s5_lambda_im, s5_log_dt,
              s5_b_re, s5_b_im, s5_c_re, s5_c_im, s5_d, s5_w_glu, rwkv_mu, rwkv_w0, rwkv_w2, rwkv_a0,
              rwkv_a2, rwkv_g2, rwkv_k_k, rwkv_k_a, rwkv_r_k, rwkv_v0, rwkv_v1, rwkv_v2, rwkv_ln_w,
              rwkv_ln_b, lru_conv_w, lru_conv_b, lru_wa, lru_ba, lru_wx, lru_bx, lru_lambda,
              merge_gain, w_out, norm2_w, moe_coarse_w, moe_coarse_b, moe_fine_w, moe_fine_b,
              moe_w_gate, moe_w_up, moe_w_down, final_norm_w):
    B, T, D = x.shape
    lb_all = jnp.cumsum(jax.nn.softmax(hgrn_lb_logits.astype(jnp.float32), axis=0), axis=0)
    lb_all = lb_all - lb_all[:1]
    h = x
    v_first = None
    for l in range(DEPTH):
        hn = rms_norm(h, norm1_w[l])
        proj = hn @ w_in[l]
        hq, hf, hi, hg, su, rw, lg, lx = jnp.split(proj, _splits(COL_SIZES), axis=-1)
        o_a = hgrn2_mixer(hq, hf, hi, hg, lb_all[l], hgrn_norm_w[l])
        o_b = s5_mixer(su, s5_lambda_re[l], s5_lambda_im[l], s5_log_dt[l], s5_b_re[l], s5_b_im[l],
                       s5_c_re[l], s5_c_im[l], s5_d[l], s5_w_glu[l])
        v_mix = None if l == 0 else (rwkv_v0[l - 1], rwkv_v1[l - 1], rwkv_v2[l - 1])
        o_c, v_first = rwkv7_mixer(rw, rwkv_mu[l], rwkv_w0[l], rwkv_w2[l], rwkv_a0[l], rwkv_a2[l],
                                   rwkv_g2[l], rwkv_k_k[l], rwkv_k_a[l], rwkv_r_k[l], rwkv_ln_w[l],
                                   rwkv_ln_b[l], v_first, v_mix)
        o_d = rglru_mixer(lg, lx, lru_conv_w[l], lru_conv_b[l], lru_wa[l], lru_ba[l], lru_wx[l],
                          lru_bx[l], lru_lambda[l])
        mix = jnp.concatenate([o_a, o_b, o_c, o_d], axis=-1).reshape(B, T, N_MIX_GROUPS, GROUP_W)
        mix = rms_norm(mix, merge_gain[l].reshape(N_MIX_GROUPS, GROUP_W)).reshape(B, T, D_MIX)
        h = h + mix @ w_out[l]
        h = h + hier_moe(rms_norm(h, norm2_w[l]), moe_coarse_w[l], moe_coarse_b[l], moe_fine_w[l],
                         moe_fine_b[l], moe_w_gate[l], moe_w_up[l], moe_w_down[l])
    return rms_norm(h, final_norm_w)
```

```python
import functools
import math

import jax
import jax.numpy as jnp
from jax import lax
from jax.experimental import pallas as pl
from jax.experimental.pallas import tpu as pltpu

F32 = jnp.float32
BF16 = jnp.bfloat16

D_MODEL = 1024
GROUP_W = 256
RMS_EPS = 1e-6
HEAD_W = 64
HGRN_CHUNK = 16
S5_CH = 16
S5_GROUPS = GROUP_W // S5_CH
S5_STATE = 64
S5_CHUNK = 16
RWKV_CHUNK = 64
RWKV_GN_EPS = 64e-5
RWKV_W_LORA = 64
RWKV_A_LORA = 64
RWKV_G_LORA = 128
RWKV_V_LORA = 32
LRU_CONV = 4
LRU_C = 8.0
MOE_GROUPS = 4
MOE_PER_GROUP = 8
N_EXPERTS = MOE_GROUPS * MOE_PER_GROUP
D_EXPERT = 512
ROUTER_LANES = 128
NEG_BIG = -1e30
VMEM_LIMIT = 56 * 1024 * 1024

COL_HQ, COL_HF, COL_HI, COL_HG, COL_S5, COL_R, COL_K, COL_V, COL_LORA, COL_LG, COL_LX = range(11)


def _cparams(*sem):
    return pltpu.CompilerParams(dimension_semantics=sem, vmem_limit_bytes=VMEM_LIMIT)


def _mm(a, b):
    return jnp.dot(a.astype(BF16), b.astype(BF16), preferred_element_type=F32)


def _split2(x):
    hi = x.astype(BF16)
    lo = (x - hi.astype(F32)).astype(BF16)
    return hi, lo


def _dg3(a, b, dims):
    ah, al = _split2(a)
    bh, bl = _split2(b)
    d = lambda x, y: lax.dot_general(x, y, (dims, ((), ())), preferred_element_type=F32)
    return d(ah, bh) + d(ah, bl) + d(al, bh)


_NN = ((1,), (0,))
_NT = ((1,), (1,))
_TN = ((0,), (0,))


def _exact_lhs_mm(m_bf16, x):
    h1 = x.astype(BF16)
    r1 = x - h1.astype(F32)
    h2 = r1.astype(BF16)
    h3 = (r1 - h2.astype(F32)).astype(BF16)
    d = lambda y: jnp.dot(m_bf16, y, preferred_element_type=F32)
    return d(h1) + d(h2) + d(h3)


def _head_sum(x, bd_bf16):
    h1 = x.astype(BF16)
    r1 = x - h1.astype(F32)
    h2 = r1.astype(BF16)
    h3 = (r1 - h2.astype(F32)).astype(BF16)
    d = lambda y: jnp.dot(y, bd_bf16, preferred_element_type=F32)
    return d(h1) + d(h2) + d(h3)


def _block_diag_mask(n, blk):
    r = lax.broadcasted_iota(jnp.int32, (n, n), 0) // blk
    c = lax.broadcasted_iota(jnp.int32, (n, n), 1) // blk
    return r == c


def _rms(x, w):
    return x * lax.rsqrt(jnp.mean(x * x, axis=-1, keepdims=True) + RMS_EPS) * w


def _silu(x):
    return x * jax.nn.sigmoid(x)


def _softplus(x):
    return jnp.maximum(x, 0.0) + jnp.log(1.0 + jnp.exp(-jnp.abs(x)))


def _in_proj_kernel(x_ref, nw_ref, w_ref, o_ref):
    y = _rms(x_ref[...], nw_ref[...])
    o_ref[...] = jnp.dot(y.astype(BF16), w_ref[...], preferred_element_type=F32)


def _in_proj(h2d, norm_w, w_bf16, tm):
    n, d = h2d.shape
    d_in = w_bf16.shape[1]
    return pl.pallas_call(
        _in_proj_kernel,
        grid=(n // tm,),
        in_specs=[pl.BlockSpec((tm, d), lambda i: (i, 0)),
                  pl.BlockSpec((1, d), lambda i: (0, 0)),
                  pl.BlockSpec((d, d_in), lambda i: (0, 0))],
        out_specs=pl.BlockSpec((tm, d_in), lambda i: (i, 0)),
        out_shape=jax.ShapeDtypeStruct((n, d_in), F32),
        compiler_params=_cparams("parallel"),
        name="in_proj",
    )(h2d, norm_w.reshape(1, d), w_bf16)


def _col_spec(tb, col):
    return pl.BlockSpec((None, tb, GROUP_W), lambda b, t: (b, t, col))


def _row_spec(width=GROUP_W):
    return pl.BlockSpec((1, width), lambda b, t: (0, 0))


def _full_spec(shape):
    return pl.BlockSpec(shape, lambda b, t: (0,) * len(shape))


def _hgrn_kernel(q_ref, f_ref, i_ref, g_ref, lb_ref, nw_ref, mg_ref, o_ref,
                 st_ref, q_s, k_s, v_s, lf_s, o_s):
    ch = HGRN_CHUNK
    tb = q_ref.shape[0]

    @pl.when(pl.program_id(1) == 0)
    def _():
        st_ref[...] = jnp.zeros_like(st_ref)

    lb = lb_ref[...]
    fx = f_ref[...]
    x1 = jnp.log(lb)
    x2 = jnp.log(1.0 - lb) - _softplus(-fx)
    m = jnp.maximum(x1, x2)
    lf_s[...] = m + jnp.log(jnp.exp(x1 - m) + jnp.exp(x2 - m))
    q_s[...] = _silu(q_ref[...])
    k_s[...] = (1.0 - lb) * jax.nn.sigmoid(-fx)
    v_s[...] = _silu(i_ref[...])

    bd = _block_diag_mask(GROUP_W, HEAD_W)
    bd_bf16 = bd.astype(BF16)
    tri = (lax.broadcasted_iota(jnp.int32, (ch, ch), 0)
           >= lax.broadcasted_iota(jnp.int32, (ch, ch), 1))
    tri_bf16 = tri.astype(BF16)
    tri3 = (lax.broadcasted_iota(jnp.int32, (ch, ch, 1), 0)
            >= lax.broadcasted_iota(jnp.int32, (ch, ch, 1), 1))

    def chunk(c, carry):
        sl = pl.ds(pl.multiple_of(c * ch, ch), ch)
        qc, kc, vc = q_s[sl, :], k_s[sl, :], v_s[sl, :]
        b = _exact_lhs_mm(tri_bf16, lf_s[sl, :])
        rel = b[:, None, :] - b[None, :, :]
        dec = jnp.exp(jnp.where(tri3, rel, NEG_BIG))
        p = (qc[:, None, :] * kc[None, :, :]) * dec
        sc = jnp.dot(p.reshape(ch * ch, GROUP_W).astype(BF16), bd_bf16,
                     preferred_element_type=F32).reshape(ch, ch, GROUP_W)
        o_intra = jnp.sum(sc * vc[None, :, :], axis=1)
        st = st_ref[...]
        o_inter = lax.dot_general((qc * jnp.exp(b)).astype(BF16), st.astype(BF16),
                                  (_NT, ((), ())), preferred_element_type=F32)
        b_end = b[ch - 1:ch, :]
        kh = kc * jnp.exp(b_end - b)
        upd = lax.dot_general(vc.astype(BF16), kh.astype(BF16), (_TN, ((), ())),
                              preferred_element_type=F32)
        st_ref[...] = st * jnp.exp(b_end) + jnp.where(bd, upd, 0.0)
        o_s[sl, :] = o_intra + o_inter
        return carry

    lax.fori_loop(0, tb // ch, chunk, 0)

    o = o_s[...]
    ms = _head_sum(o * o, bd_bf16) * (1.0 / HEAD_W)
    o = o * lax.rsqrt(ms + RMS_EPS) * nw_ref[...] * _silu(g_ref[...])
    o_ref[...] = _rms(o, mg_ref[...])


def _hgrn(proj3, lb, norm_w, merge_g, tb):
    bsz, t, _ = proj3.shape
    blk = pltpu.VMEM((tb, GROUP_W), F32)
    return pl.pallas_call(
        _hgrn_kernel,
        grid=(bsz, t // tb),
        in_specs=[_col_spec(tb, COL_HQ), _col_spec(tb, COL_HF), _col_spec(tb, COL_HI),
                  _col_spec(tb, COL_HG), _row_spec(), _row_spec(), _row_spec()],
        out_specs=pl.BlockSpec((None, tb, GROUP_W), lambda b, i: (b, i, 0)),
        out_shape=jax.ShapeDtypeStruct((bsz, t, GROUP_W), F32),
        scratch_shapes=[pltpu.VMEM((GROUP_W, GROUP_W), F32), blk, blk, blk, blk, blk],
        compiler_params=_cparams("parallel", "arbitrary"),
        name="hgrn2",
    )(proj3, proj3, proj3, proj3, lb.reshape(1, -1), norm_w.reshape(1, -1), merge_g.reshape(1, -1))


def _lru_kernel(xg_ref, xr_ref, cw_ref, cb_ref, wa_ref, ba_ref, wx_ref, bx_ref, lam_ref, mg_ref,
                o_ref, buf_ref, h_ref):
    tb = xr_ref.shape[0]

    @pl.when(pl.program_id(1) == 0)
    def _():
        buf_ref[0:8, :] = jnp.zeros((8, GROUP_W), F32)
        h_ref[...] = jnp.zeros_like(h_ref)

    xr = xr_ref[...]
    buf_ref[8:8 + tb, :] = xr
    xc = cb_ref[...] + jnp.zeros_like(xr)
    for j in range(LRU_CONV):
        xc = xc + cw_ref[j:j + 1, :] * buf_ref[pl.ds(8 - (LRU_CONV - 1) + j, tb), :]
    buf_ref[0:8, :] = xr[tb - 8:tb, :]

    r = jax.nn.sigmoid(jnp.dot(xc.astype(BF16), wa_ref[...], preferred_element_type=F32) + ba_ref[...])
    gi = jax.nn.sigmoid(jnp.dot(xc.astype(BF16), wx_ref[...], preferred_element_type=F32) + bx_ref[...])
    log_a = -LRU_C * r * _softplus(-lam_ref[...])
    a = jnp.exp(log_a)
    x = jnp.sqrt(1.0 - jnp.exp(2.0 * log_a)) * (gi * xc)

    rows = lax.broadcasted_iota(jnp.int32, (tb, 1), 0)
    k = 1
    while k < tb:
        keep = rows >= k
        x = x + jnp.where(keep, a * pltpu.roll(x, k, axis=0), 0.0)
        a = jnp.where(keep, a * pltpu.roll(a, k, axis=0), a)
        k *= 2
    h = x + a * h_ref[...]
    h_ref[...] = h[tb - 1:tb, :]
    o_ref[...] = _rms(jax.nn.gelu(xg_ref[...]) * h, mg_ref[...])


def _lru(proj3, conv_w, conv_b, wa_bd, ba, wx_bd, bx, lam, merge_g, tb):
    bsz, t, _ = proj3.shape
    return pl.pallas_call(
        _lru_kernel,
        grid=(bsz, t // tb),
        in_specs=[_col_spec(tb, COL_LG), _col_spec(tb, COL_LX), _full_spec((LRU_CONV, GROUP_W)),
                  _row_spec(), _full_spec((GROUP_W, GROUP_W)), _row_spec(),
                  _full_spec((GROUP_W, GROUP_W)), _row_spec(), _row_spec(), _row_spec()],
        out_specs=pl.BlockSpec((None, tb, GROUP_W), lambda b, i: (b, i, 0)),
        out_shape=jax.ShapeDtypeStruct((bsz, t, GROUP_W), F32),
        scratch_shapes=[pltpu.VMEM((tb + 8, GROUP_W), F32), pltpu.VMEM((1, GROUP_W), F32)],
        compiler_params=_cparams("parallel", "arbitrary"),
        name="rglru",
    )(proj3, proj3, conv_w, conv_b.reshape(1, -1), wa_bd, ba.reshape(1, -1), wx_bd,
      bx.reshape(1, -1), lam.reshape(1, -1), merge_g.reshape(1, -1))


def _block_diag_weight(w):
    h, n, _ = w.shape
    eye = jnp.eye(h, dtype=w.dtype)
    return jnp.einsum('hij,hg->higj', w, eye).reshape(h * n, h * n).astype(BF16)


def _s5_matrices(lam_re, lam_im, log_dt, b_re, b_im, c_re, c_im):
    L, G, P, C = S5_CHUNK, S5_GROUPS, S5_STATE, S5_CH
    lr, li = lam_re.astype(F32), lam_im.astype(F32)
    dt = jnp.exp(log_dt.astype(F32))[:, None]
    mag = jnp.exp(lr * dt)
    a_re, a_im = mag * jnp.cos(li * dt), mag * jnp.sin(li * dt)
    den = lr * lr + li * li
    kap_re = ((a_re - 1.0) * lr + a_im * li) / den
    kap_im = (a_im * lr - (a_re - 1.0) * li) / den
    br, bi = b_re.astype(F32), b_im.astype(F32)
    bb_re = kap_re[..., None] * br - kap_im[..., None] * bi
    bb_im = kap_re[..., None] * bi + kap_im[..., None] * br
    cr, ci = c_re.astype(F32), c_im.astype(F32)
    kk = jnp.arange(L + 1, dtype=F32)[:, None, None]
    pw_mag = jnp.exp(kk * (lr * dt)[None])
    pw_re = pw_mag * jnp.cos(kk * (li * dt)[None])
    pw_im = pw_mag * jnp.sin(kk * (li * dt)[None])

    pr, pi = pw_re[L - 1::-1][:L], pw_im[L - 1::-1][:L]
    w_re = jnp.einsum('tgp,gpc->gtcp', pr, bb_re) - jnp.einsum('tgp,gpc->gtcp', pi, bb_im)
    w_im = jnp.einsum('tgp,gpc->gtcp', pr, bb_im) + jnp.einsum('tgp,gpc->gtcp', pi, bb_re)
    ab_re = jnp.einsum('kgp,gpc->kgpc', pw_re[:L], bb_re) - jnp.einsum('kgp,gpc->kgpc', pw_im[:L], bb_im)
    ab_im = jnp.einsum('kgp,gpc->kgpc', pw_re[:L], bb_im) + jnp.einsum('kgp,gpc->kgpc', pw_im[:L], bb_re)
    taps = jnp.einsum('gop,kgpc->gkco', cr, ab_re) - jnp.einsum('gop,kgpc->gkco', ci, ab_im)
    lag = jnp.arange(L)[None, :] - jnp.arange(L)[:, None]
    m = jnp.where((lag >= 0)[None, :, :, None, None],
                  taps[:, jnp.clip(lag, 0, L - 1)], 0.0)
    m = m.transpose(0, 1, 3, 2, 4).reshape(G, L * C, L * C)
    qr = jnp.einsum('gop,tgp->gpto', cr, pw_re[1:]) - jnp.einsum('gop,tgp->gpto', ci, pw_im[1:])
    qi = -(jnp.einsum('gop,tgp->gpto', cr, pw_im[1:]) + jnp.einsum('gop,tgp->gpto', ci, pw_re[1:]))

    w_re = w_re.reshape(G // 2, 2, L * C, P)
    w_im = w_im.reshape(G // 2, 2, L * C, P)
    eye2 = jnp.eye(2, dtype=F32)
    wre_p = jnp.einsum('ajip,jk->ajikp', w_re, eye2).reshape(G // 2, 2 * L * C, 2 * P)
    wim_p = jnp.einsum('ajip,jk->ajikp', w_im, eye2).reshape(G // 2, 2 * L * C, 2 * P)
    w_pair = jnp.concatenate([wre_p, wim_p], axis=-1)
    m_pair = jnp.einsum('ajio,jk->ajiko', m.reshape(G // 2, 2, L * C, L * C), eye2)
    m_pair = m_pair.reshape(G // 2, 2 * L * C, 2 * L * C)
    qr_p = jnp.einsum('ajpo,jk->ajpko', qr.reshape(G // 2, 2, P, L * C), eye2)
    qi_p = jnp.einsum('ajpo,jk->ajpko', qi.reshape(G // 2, 2, P, L * C), eye2)
    q_pair = jnp.concatenate([qr_p.reshape(G // 2, 2 * P, 2 * L * C),
                              qi_p.reshape(G // 2, 2 * P, 2 * L * C)], axis=1)
    al = jnp.stack([pw_re[L].reshape(G // 2, 2 * P), pw_im[L].reshape(G // 2, 2 * P)], axis=1)
    return w_pair.astype(BF16), m_pair.astype(BF16), q_pair.astype(BF16), al


def _s5_scan_kernel(u_ref, w_ref, m_ref, q_ref, al_ref, y_ref, p_s, x_s, *, bsz):
    rows = u_ref.shape[0]
    half = 2 * S5_STATE
    u = u_ref[...]
    p_s[...] = jnp.dot(u, w_ref[...], preferred_element_type=F32)
    al_re = jnp.broadcast_to(al_ref[0:1, :], (bsz, half))
    al_im = jnp.broadcast_to(al_ref[1:2, :], (bsz, half))

    def step(c, carry):
        xr, xi = carry
        sl = pl.ds(pl.multiple_of(c * bsz, bsz), bsz)
        x_s[sl, 0:half] = xr
        x_s[sl, half:2 * half] = xi
        pr, pi = p_s[sl, 0:half], p_s[sl, half:2 * half]
        return al_re * xr - al_im * xi + pr, al_re * xi + al_im * xr + pi

    zero = jnp.zeros((bsz, half), F32)
    lax.fori_loop(0, rows // bsz, step, (zero, zero))
    y_ref[...] = (jnp.dot(u, m_ref[...], preferred_element_type=F32)
                  + jnp.dot(x_s[...].astype(BF16), q_ref[...], preferred_element_type=F32))


def _s5_scan(u_pairs, w_pair, m_pair, q_pair, al, bsz):
    npair, rows, width = u_pairs.shape
    half = 2 * S5_STATE
    spec3 = lambda a, b: pl.BlockSpec((None, a, b), lambda g: (g, 0, 0))
    return pl.pallas_call(
        functools.partial(_s5_scan_kernel, bsz=bsz),
        grid=(npair,),
        in_specs=[spec3(rows, width), spec3(width, 2 * half), spec3(width, width),
                  spec3(2 * half, width), spec3(2, half)],
        out_specs=spec3(rows, width),
        out_shape=jax.ShapeDtypeStruct((npair, rows, width), F32),
        scratch_shapes=[pltpu.VMEM((rows, 2 * half), F32), pltpu.VMEM((rows, 2 * half), F32)],
        compiler_params=_cparams("parallel"),
        name="s5_scan",
    )(u_pairs, w_pair, m_pair, q_pair, al)


def _s5_out_kernel(y_ref, u_ref, d_ref, wg_ref, mg_ref, o_ref):
    y = jax.nn.gelu(y_ref[...] + d_ref[...] * u_ref[...])
    z = jnp.dot(y.astype(BF16), wg_ref[...], preferred_element_type=F32)
    o_ref[...] = _rms(z[:, :GROUP_W] * jax.nn.sigmoid(z[:, GROUP_W:]), mg_ref[...])


def _s5_out(y3, proj3, d_skip, w_glu_bf16, merge_g, tb):
    bsz, t, _ = y3.shape
    return pl.pallas_call(
        _s5_out_kernel,
        grid=(bsz, t // tb),
        in_specs=[pl.BlockSpec((None, tb, GROUP_W), lambda b, i: (b, i, 0)), _col_spec(tb, COL_S5),
                  _row_spec(), _full_spec((GROUP_W, 2 * GROUP_W)), _row_spec()],
        out_specs=pl.BlockSpec((None, tb, GROUP_W), lambda b, i: (b, i, 0)),
        out_shape=jax.ShapeDtypeStruct((bsz, t, GROUP_W), F32),
        compiler_params=_cparams("parallel", "parallel"),
        name="s5_out",
    )(y3, proj3, d_skip.reshape(1, -1), w_glu_bf16, merge_g.reshape(1, -1))


def _s5(proj3, mats, d_skip, w_glu_bf16, merge_g, tb):
    bsz, t, _ = proj3.shape
    L, C, G = S5_CHUNK, S5_CH, S5_GROUPS
    u = proj3[:, :, COL_S5 * GROUP_W:(COL_S5 + 1) * GROUP_W].astype(BF16)
    u_pairs = (u.reshape(bsz, t // L, L, G // 2, 2, C).transpose(3, 1, 0, 4, 2, 5)
               .reshape(G // 2, (t // L) * bsz, 2 * L * C))
    y_pairs = _s5_scan(u_pairs, *mats, bsz)
    y3 = (y_pairs.reshape(G // 2, t // L, bsz, 2, L, C).transpose(2, 1, 4, 0, 3, 5)
          .reshape(bsz, t, GROUP_W))
    return _s5_out(y3, proj3, d_skip, w_glu_bf16, merge_g, tb)


def _rwkv_kernel(*refs, has_vmix):
    if has_vmix:
        (r_ref, k_ref, v_ref, lo_ref, vf_ref, mu_ref, w0_ref, w2_ref, a0_ref, a2_ref, g2_ref,
         kk_ref, ka_ref, rk_ref, lnw_ref, lnb_ref, mg_ref, v0_ref, v1_ref, v2_ref,
         o_ref, ht_ref, prev_ref, lw_s, a_s, b_s, k_s, r_s, v_s, y_s) = refs
    else:
        (r_ref, k_ref, v_ref, lo_ref, mu_ref, w0_ref, w2_ref, a0_ref, a2_ref, g2_ref,
         kk_ref, ka_ref, rk_ref, lnw_ref, lnb_ref, mg_ref,
         o_ref, vf_out_ref, ht_ref, prev_ref, lw_s, a_s, b_s, k_s, r_s, v_s, y_s) = refs
    tb = r_ref.shape[0]
    ch = RWKV_CHUNK
    nh = GROUP_W // HEAD_W

    @pl.when(pl.program_id(1) == 0)
    def _():
        ht_ref[...] = jnp.zeros_like(ht_ref)
        prev_ref[...] = jnp.zeros_like(prev_ref)

    row0 = lax.broadcasted_iota(jnp.int32, (tb, 1), 0) == 0

    def mixed(ref, j):
        p = ref[...]
        lanes = slice(j * GROUP_W, (j + 1) * GROUP_W)
        sh = jnp.where(row0, prev_ref[0:1, lanes], pltpu.roll(p, 1, axis=0))
        prev_new = p[tb - 1:tb, :]
        return p + mu_ref[0:1, lanes] * (sh - p), prev_new

    r, pr_ = mixed(r_ref, 0)
    k, pk_ = mixed(k_ref, 1)
    v, pv_ = mixed(v_ref, 2)
    lo, pl_ = mixed(lo_ref, 3)
    for j, pn in enumerate((pr_, pk_, pv_, pl_)):
        prev_ref[0:1, j * GROUP_W:(j + 1) * GROUP_W] = pn

    bd = _block_diag_mask(GROUP_W, HEAD_W)
    bd_bf16 = bd.astype(BF16)

    w_raw = -_softplus(-(w0_ref[...] + _mm(jnp.tanh(lo), w2_ref[...]))) - 0.5
    lw_s[...] = -jnp.exp(w_raw)
    alr = jax.nn.sigmoid(a0_ref[...] + _mm(lo, a2_ref[...]))
    g = _mm(jax.nn.sigmoid(lo), g2_ref[...])
    kkr = k * kk_ref[...]
    kk = kkr / jnp.maximum(jnp.sqrt(_head_sum(kkr * kkr, bd_bf16)), 1e-12)
    k = k * (1.0 + (alr - 1.0) * ka_ref[...])
    if has_vmix:
        gate = jax.nn.sigmoid(v0_ref[...] + _mm(_mm(v, v1_ref[...]), v2_ref[...]))
        v = v + (vf_ref[...] - v) * gate
    else:
        vf_out_ref[...] = v
    a_s[...] = -kk
    b_s[...] = kk * alr
    k_s[...] = k
    r_s[...] = r
    v_s[...] = v

    n4 = nh * ch
    ri = lax.broadcasted_iota(jnp.int32, (n4, n4), 0)
    ci = lax.broadcasted_iota(jnp.int32, (n4, n4), 1)
    same_head = (ri // ch) == (ci // ch)
    strict = same_head & ((ri % ch) > (ci % ch))
    incl = same_head & ((ri % ch) >= (ci % ch))
    eye = (ri == ci).astype(F32)
    hm = ((lax.broadcasted_iota(jnp.int32, (n4, GROUP_W), 0) // ch)
          == (lax.broadcasted_iota(jnp.int32, (n4, GROUP_W), 1) // HEAD_W))
    tri = (lax.broadcasted_iota(jnp.int32, (ch, ch), 0)
           >= lax.broadcasted_iota(jnp.int32, (ch, ch), 1)).astype(BF16)

    def stack(x):
        return jnp.where(hm, jnp.concatenate([x] * nh, axis=0), 0.0)

    def chunk(c, carry):
        sl = pl.ds(pl.multiple_of(c * ch, ch), ch)
        lw = lw_s[sl, :]
        cl = _exact_lhs_mm(tri, lw)
        cl_end = cl[ch - 1:ch, :]
        e_in, e_ex = jnp.exp(cl), jnp.exp(cl - lw)
        e_neg, e_end = jnp.exp(-cl), jnp.exp(cl_end - cl)
        av, bv, kv, rv, vv = a_s[sl, :], b_s[sl, :], k_s[sl, :], r_s[sl, :], v_s[sl, :]
        at4, rt4 = stack(av * e_ex), stack(rv * e_in)
        bt4 = jnp.concatenate([bv * e_neg] * nh, axis=0)
        kt4 = jnp.concatenate([kv * e_neg] * nh, axis=0)
        v4 = stack(vv)
        ar = jnp.concatenate([at4, rt4], axis=0)
        sb = _dg3(ar, bt4, _NT)
        sk = _dg3(ar, kt4, _NT)
        l_ab = jnp.where(strict, sb[:n4], 0.0)
        l_ak = jnp.where(strict, sk[:n4], 0.0)
        l_rb = jnp.where(incl, sb[n4:], 0.0)
        l_rk = jnp.where(incl, sk[n4:], 0.0)
        tinv = eye + l_ab
        pw = l_ab
        span = 2
        while span < ch:
            pw = _dg3(pw, pw, _NN)
            tinv = tinv + _dg3(tinv, pw, _NN)
            span *= 2
        ht = ht_ref[...]
        ht_b = ht.astype(BF16)
        nt = lambda x, y: lax.dot_general(x.astype(BF16), y, (_NT, ((), ())), preferred_element_type=F32)
        wmat = nt(at4, ht_b) + _mm(l_ak, v4)
        u = _mm(tinv, wmat)
        y4 = nt(rt4, ht_b) + _mm(l_rb, u) + _mm(l_rk, v4)
        y = y4[0:ch]
        for h in range(1, nh):
            y = y + y4[h * ch:(h + 1) * ch]
        y_s[sl, :] = y
        tn = lambda x, y_: lax.dot_general(x.astype(BF16), y_.astype(BF16), (_TN, ((), ())),
                                           preferred_element_type=F32)
        ht_ref[...] = (ht * jnp.exp(cl_end) + tn(u, stack(bv * e_end)) + tn(v4, stack(kv * e_end)))
        return carry

    lax.fori_loop(0, tb // ch, chunk, 0)

    y = y_s[...]
    r, k, v = r_s[...], k_s[...], v_s[...]
    mean = _head_sum(y, bd_bf16) * (1.0 / HEAD_W)
    d = y - mean
    var = _head_sum(d * d, bd_bf16) * (1.0 / HEAD_W)
    y = d * lax.rsqrt(var + RWKV_GN_EPS) * lnw_ref[...] + lnb_ref[...]
    y = y + _head_sum(r * k * rk_ref[...], bd_bf16) * v
    o_ref[...] = _rms(y * g, mg_ref[...])


def _pad_rows(w, start, total=GROUP_W):
    out = jnp.zeros((total, w.shape[1]), F32).at[start:start + w.shape[0]].set(w)
    return out.astype(BF16)


def _rwkv(proj3, v_first, lyr, p, merge_g, tb):
    bsz, t, _ = proj3.shape
    has_vmix = v_first is not None
    row = lambda x: x.reshape(1, -1)
    w2 = _pad_rows(p["rwkv_w2"][lyr], 0)
    a2 = _pad_rows(p["rwkv_a2"][lyr], RWKV_W_LORA)
    g2 = _pad_rows(p["rwkv_g2"][lyr], RWKV_W_LORA + RWKV_A_LORA)
    args = [proj3, proj3, proj3, proj3]
    specs = [_col_spec(tb, COL_R), _col_spec(tb, COL_K), _col_spec(tb, COL_V), _col_spec(tb, COL_LORA)]
    if has_vmix:
        args.append(v_first)
        specs.append(pl.BlockSpec((None, tb, GROUP_W), lambda b, i: (b, i, 0)))
    args += [row(p["rwkv_mu"][lyr]), row(p["rwkv_w0"][lyr]), w2, row(p["rwkv_a0"][lyr]), a2, g2,
             row(p["rwkv_k_k"][lyr]), row(p["rwkv_k_a"][lyr]), row(p["rwkv_r_k"][lyr]),
             row(p["rwkv_ln_w"][lyr]), row(p["rwkv_ln_b"][lyr]), row(merge_g)]
    sq = _full_spec((GROUP_W, GROUP_W))
    specs += [_row_spec(4 * GROUP_W), _row_spec(), sq, _row_spec(), sq, sq,
              _row_spec(), _row_spec(), _row_spec(), _row_spec(), _row_spec(), _row_spec()]
    if has_vmix:
        v1 = jnp.zeros((GROUP_W, 128), F32).at[:, :RWKV_V_LORA].set(p["rwkv_v1"][lyr - 1]).astype(BF16)
        v2 = jnp.zeros((128, GROUP_W), F32).at[:RWKV_V_LORA].set(p["rwkv_v2"][lyr - 1]).astype(BF16)
        args += [row(p["rwkv_v0"][lyr - 1]), v1, v2]
        specs += [_row_spec(), _full_spec((GROUP_W, 128)), _full_spec((128, GROUP_W))]
    out_blk = pl.BlockSpec((None, tb, GROUP_W), lambda b, i: (b, i, 0))
    out_sds = jax.ShapeDtypeStruct((bsz, t, GROUP_W), F32)
    blk = pltpu.VMEM((tb, GROUP_W), F32)
    res = pl.pallas_call(
        functools.partial(_rwkv_kernel, has_vmix=has_vmix),
        grid=(bsz, t // tb),
        in_specs=specs,
        out_specs=out_blk if has_vmix else [out_blk, out_blk],
        out_shape=out_sds if has_vmix else [out_sds, out_sds],
        scratch_shapes=[pltpu.VMEM((GROUP_W, GROUP_W), F32), pltpu.VMEM((1, 4 * GROUP_W), F32),
                        blk, blk, blk, blk, blk, blk, blk],
        compiler_params=_cparams("parallel", "arbitrary"),
        name="rwkv7",
    )(*args)
    if has_vmix:
        return res, v_first
    return res[0], res[1]


def _merge_router_kernel(oa_ref, ob_ref, oc_ref, od_ref, h_ref, wo_ref, nw_ref, wr_ref, br_ref,
                         h1_ref, hn_ref, cmb_ref):
    mix = jnp.concatenate([oa_ref[...], ob_ref[...], oc_ref[...], od_ref[...]], axis=1)
    h1 = h_ref[...] + jnp.dot(mix.astype(BF16), wo_ref[...], preferred_element_type=F32)
    h1_ref[...] = h1
    hn = _rms(h1, nw_ref[...])
    hn_ref[...] = hn.astype(BF16)
    logits = _dg3(hn, wr_ref[...], _NN) + br_ref[...]
    lane = lax.broadcasted_iota(jnp.int32, logits.shape, 1)
    big = jnp.int32(ROUTER_LANES)
    is_c = (lane >= N_EXPERTS) & (lane < N_EXPERTS + MOE_GROUPS)
    cm = jnp.max(jnp.where(is_c, logits, NEG_BIG), axis=-1, keepdims=True)
    gsel = jnp.min(jnp.where(is_c & (logits == cm), lane, big), axis=-1, keepdims=True) - N_EXPERTS
    p_g = 1.0 / jnp.sum(jnp.where(is_c, jnp.exp(logits - cm), 0.0), axis=-1, keepdims=True)
    lo = gsel * MOE_PER_GROUP
    in_g = (lane >= lo) & (lane < lo + MOE_PER_GROUP)
    m1 = jnp.max(jnp.where(in_g, logits, NEG_BIG), axis=-1, keepdims=True)
    i1 = jnp.min(jnp.where(in_g & (logits == m1), lane, big), axis=-1, keepdims=True)
    in_g2 = in_g & (lane != i1)
    m2 = jnp.max(jnp.where(in_g2, logits, NEG_BIG), axis=-1, keepdims=True)
    i2 = jnp.min(jnp.where(in_g2 & (logits == m2), lane, big), axis=-1, keepdims=True)
    w1 = p_g / (1.0 + jnp.exp(m2 - m1))
    w2 = p_g - w1
    cmb_ref[...] = jnp.where(lane == i1, w1, 0.0) + jnp.where(lane == i2, w2, 0.0)


def _merge_router(outs, h2d, w_out_bf16, norm_w, w_router, b_router, tm):
    n, d = h2d.shape
    grp = pl.BlockSpec((tm, GROUP_W), lambda i: (i, 0))
    full = lambda a, b: pl.BlockSpec((a, b), lambda i: (0, 0))
    rowblk = lambda w: pl.BlockSpec((tm, w), lambda i: (i, 0))
    return pl.pallas_call(
        _merge_router_kernel,
        grid=(n // tm,),
        in_specs=[grp, grp, grp, grp, rowblk(d), full(d, d), full(1, d), full(d, ROUTER_LANES),
                  full(1, ROUTER_LANES)],
        out_specs=[rowblk(d), rowblk(d), rowblk(ROUTER_LANES)],
        out_shape=[jax.ShapeDtypeStruct((n, d), F32), jax.ShapeDtypeStruct((n, d), BF16),
                   jax.ShapeDtypeStruct((n, ROUTER_LANES), F32)],
        compiler_params=_cparams("parallel"),
        name="merge_router",
    )(*[o.reshape(n, GROUP_W) for o in outs], h2d, w_out_bf16, norm_w.reshape(1, d), w_router, b_router)


def _moe_kernel(x_ref, cmb_ref, h1_ref, wg_ref, wu_ref, wd_ref, fw_ref, o_ref, *, final_norm):
    e = pl.program_id(1)

    @pl.when(e == 0)
    def _():
        o_ref[...] = h1_ref[...]

    x = x_ref[...]
    he = (_silu(jnp.dot(x, wg_ref[...], preferred_element_type=F32))
          * jnp.dot(x, wu_ref[...], preferred_element_type=F32))
    lane = lax.broadcasted_iota(jnp.int32, cmb_ref.shape, 1)
    c = jnp.sum(jnp.where(lane == e, cmb_ref[...], 0.0), axis=-1, keepdims=True)
    o_ref[...] += c * jnp.dot(he.astype(BF16), wd_ref[...], preferred_element_type=F32)

    if final_norm:
        @pl.when(e == pl.num_programs(1) - 1)
        def _():
            o_ref[...] = _rms(o_ref[...], fw_ref[...])


def _moe(hn_bf16, combine, h1, wg, wu, wd, final_w, final_norm, tm):
    n, d = h1.shape
    rowblk = lambda w: pl.BlockSpec((tm, w), lambda i, e: (i, 0))
    return pl.pallas_call(
        functools.partial(_moe_kernel, final_norm=final_norm),
        grid=(n // tm, N_EXPERTS),
        in_specs=[rowblk(d), rowblk(ROUTER_LANES), rowblk(d),
                  pl.BlockSpec((None, d, D_EXPERT), lambda i, e: (e, 0, 0)),
                  pl.BlockSpec((None, d, D_EXPERT), lambda i, e: (e, 0, 0)),
                  pl.BlockSpec((None, D_EXPERT, d), lambda i, e: (e, 0, 0)),
                  pl.BlockSpec((1, d), lambda i, e: (0, 0))],
        out_specs=rowblk(d),
        out_shape=jax.ShapeDtypeStruct((n, d), F32),
        compiler_params=_cparams("parallel", "arbitrary"),
        name="moe_experts",
    )(hn_bf16, combine, h1, wg, wu, wd, final_w.reshape(1, d))


def _pick(n, pref):
    return pref if n % pref == 0 else n


def kernel(x, norm1_w, w_in, hgrn_lb_logits, hgrn_norm_w, s5_lambda_re, s5_lambda_im, s5_log_dt, s5_b_re, s5_b_im, s5_c_re, s5_c_im, s5_d, s5_w_glu, rwkv_mu, rwkv_w0, rwkv_w2, rwkv_a0, rwkv_a2, rwkv_g2, rwkv_k_k, rwkv_k_a, rwkv_r_k, rwkv_v0, rwkv_v1, rwkv_v2, rwkv_ln_w, rwkv_ln_b, lru_conv_w, lru_conv_b, lru_wa, lru_ba, lru_wx, lru_bx, lru_lambda, merge_gain, w_out, norm2_w, moe_coarse_w, moe_coarse_b, moe_fine_w, moe_fine_b, moe_w_gate, moe_w_up, moe_w_down, final_norm_w):
    bsz, t, d = x.shape
    n = bsz * t
    depth = w_in.shape[0]
    rw = dict(rwkv_mu=rwkv_mu, rwkv_w0=rwkv_w0, rwkv_w2=rwkv_w2, rwkv_a0=rwkv_a0, rwkv_a2=rwkv_a2,
              rwkv_g2=rwkv_g2, rwkv_k_k=rwkv_k_k, rwkv_k_a=rwkv_k_a, rwkv_r_k=rwkv_r_k,
              rwkv_v0=rwkv_v0, rwkv_v1=rwkv_v1, rwkv_v2=rwkv_v2, rwkv_ln_w=rwkv_ln_w,
              rwkv_ln_b=rwkv_ln_b)
    lb_all = jnp.cumsum(jax.nn.softmax(hgrn_lb_logits.astype(F32), axis=0), axis=0)
    lb_all = lb_all - lb_all[:1]

    tm_proj = _pick(n, 512)
    tm_moe = _pick(n, 1024)
    tb_hgrn = _pick(t, 256)
    tb_lru = _pick(t, 256)
    tb_rwkv = _pick(t, 256)
    tb_s5 = _pick(t, 512)

    h = x.reshape(n, d)
    v_first = None
    for l in range(depth):
        mg = merge_gain[l].reshape(4, GROUP_W)
        proj = _in_proj(h, norm1_w[l], w_in[l].astype(BF16), tm_proj)
        proj3 = proj.reshape(bsz, t, -1)
        o_a = _hgrn(proj3, lb_all[l], hgrn_norm_w[l], mg[0], tb_hgrn)
        mats = _s5_matrices(s5_lambda_re[l], s5_lambda_im[l], s5_log_dt[l], s5_b_re[l], s5_b_im[l],
                            s5_c_re[l], s5_c_im[l])
        o_b = _s5(proj3, mats, s5_d[l], s5_w_glu[l].astype(BF16), mg[1], tb_s5)
        o_c, v_first = _rwkv(proj3, v_first, l, rw, mg[2], tb_rwkv)
        o_d = _lru(proj3, lru_conv_w[l], lru_conv_b[l], _block_diag_weight(lru_wa[l]), lru_ba[l],
                   _block_diag_weight(lru_wx[l]), lru_bx[l], lru_lambda[l], mg[3], tb_lru)
        w_router = jnp.concatenate(
            [moe_fine_w[l].transpose(1, 0, 2).reshape(d, N_EXPERTS), moe_coarse_w[l],
             jnp.zeros((d, ROUTER_LANES - N_EXPERTS - MOE_GROUPS), F32)], axis=1)
        b_router = jnp.concatenate(
            [moe_fine_b[l].reshape(N_EXPERTS), moe_coarse_b[l],
             jnp.zeros((ROUTER_LANES - N_EXPERTS - MOE_GROUPS,), F32)]).reshape(1, ROUTER_LANES)
        h1, hn, combine = _merge_router((o_a, o_b, o_c, o_d), h, w_out[l].astype(BF16), norm2_w[l],
                                        w_router, b_router, tm_proj)
        h = _moe(hn, combine, h1, moe_w_gate[l].astype(BF16), moe_w_up[l].astype(BF16),
                 moe_w_down[l].astype(BF16), final_norm_w, l == depth - 1, tm_moe)
    return h.reshape(bsz, t, d)
```

```python
import functools
import math

import jax
import jax.numpy as jnp
from jax import lax
from jax.experimental import pallas as pl
from jax.experimental.pallas import tpu as pltpu

F32 = jnp.float32
BF16 = jnp.bfloat16

D_MODEL = 1024
GROUP_W = 256
RMS_EPS = 1e-6
HEAD_W = 64
HGRN_CHUNK = 16
S5_CH = 16
S5_GROUPS = GROUP_W // S5_CH
S5_STATE = 64
S5_CHUNK = 16
RWKV_CHUNK = 64
RWKV_GN_EPS = 64e-5
RWKV_W_LORA = 64
RWKV_A_LORA = 64
RWKV_G_LORA = 128
RWKV_V_LORA = 32
LRU_CONV = 4
LRU_C = 8.0
MOE_GROUPS = 4
MOE_PER_GROUP = 8
N_EXPERTS = MOE_GROUPS * MOE_PER_GROUP
MOE_TOPK = 2
D_EXPERT = 512
ROUTER_LANES = 128
NEG_BIG = -1e30
VMEM_LIMIT = 56 * 1024 * 1024

COL_HQ, COL_HF, COL_HI, COL_HG, COL_S5, COL_R, COL_K, COL_V, COL_LORA, COL_LG, COL_LX = range(11)


def _cparams(*sem):
    return pltpu.CompilerParams(dimension_semantics=sem, vmem_limit_bytes=VMEM_LIMIT)


def _mm(a, b):
    return jnp.dot(a.astype(BF16), b.astype(BF16), preferred_element_type=F32)


def _split2(x):
    hi = x.astype(BF16)
    lo = (x - hi.astype(F32)).astype(BF16)
    return hi, lo


def _dg3(a, b, dims):
    ah, al = _split2(a)
    bh, bl = _split2(b)
    d = lambda x, y: lax.dot_general(x, y, (dims, ((), ())), preferred_element_type=F32)
    return d(ah, bh) + d(ah, bl) + d(al, bh)


_NN = ((1,), (0,))
_NT = ((1,), (1,))
_TN = ((0,), (0,))


def _exact_lhs_mm(m_bf16, x):
    h1 = x.astype(BF16)
    r1 = x - h1.astype(F32)
    h2 = r1.astype(BF16)
    h3 = (r1 - h2.astype(F32)).astype(BF16)
    d = lambda y: jnp.dot(m_bf16, y, preferred_element_type=F32)
    return d(h1) + d(h2) + d(h3)


def _head_sum(x, bd_bf16):
    h1 = x.astype(BF16)
    r1 = x - h1.astype(F32)
    h2 = r1.astype(BF16)
    h3 = (r1 - h2.astype(F32)).astype(BF16)
    d = lambda y: jnp.dot(y, bd_bf16, preferred_element_type=F32)
    return d(h1) + d(h2) + d(h3)


def _block_diag_mask(n, blk):
    r = lax.broadcasted_iota(jnp.int32, (n, n), 0) // blk
    c = lax.broadcasted_iota(jnp.int32, (n, n), 1) // blk
    return r == c


def _rms(x, w):
    return x * lax.rsqrt(jnp.mean(x * x, axis=-1, keepdims=True) + RMS_EPS) * w


def _silu(x):
    return x * jax.nn.sigmoid(x)


def _softplus(x):
    return jnp.maximum(x, 0.0) + jnp.log(1.0 + jnp.exp(-jnp.abs(x)))


def _in_proj_kernel(x_ref, nw_ref, w_ref, o_ref):
    y = _rms(x_ref[...], nw_ref[...])
    o_ref[...] = jnp.dot(y.astype(BF16), w_ref[...], preferred_element_type=F32)


def _in_proj(h2d, norm_w, w_bf16, tm):
    n, d = h2d.shape
    d_in = w_bf16.shape[1]
    return pl.pallas_call(
        _in_proj_kernel,
        grid=(n // tm,),
        in_specs=[pl.BlockSpec((tm, d), lambda i: (i, 0)),
                  pl.BlockSpec((1, d), lambda i: (0, 0)),
                  pl.BlockSpec((d, d_in), lambda i: (0, 0))],
        out_specs=pl.BlockSpec((tm, d_in), lambda i: (i, 0)),
        out_shape=jax.ShapeDtypeStruct((n, d_in), F32),
        compiler_params=_cparams("parallel"),
        name="in_proj",
    )(h2d, norm_w.reshape(1, d), w_bf16)


def _col_spec(tb, col):
    return pl.BlockSpec((None, tb, GROUP_W), lambda b, t: (b, t, col))


def _row_spec(width=GROUP_W):
    return pl.BlockSpec((1, width), lambda b, t: (0, 0))


def _full_spec(shape):
    return pl.BlockSpec(shape, lambda b, t: (0,) * len(shape))


def _hgrn_kernel(q_ref, f_ref, i_ref, g_ref, lb_ref, nw_ref, mg_ref, o_ref,
                 st_ref, q_s, k_s, v_s, lf_s, o_s):
    ch = HGRN_CHUNK
    tb = q_ref.shape[0]

    @pl.when(pl.program_id(1) == 0)
    def _():
        st_ref[...] = jnp.zeros_like(st_ref)

    lb = lb_ref[...]
    fx = f_ref[...]
    x1 = jnp.log(lb)
    x2 = jnp.log(1.0 - lb) - _softplus(-fx)
    m = jnp.maximum(x1, x2)
    lf_s[...] = m + jnp.log(jnp.exp(x1 - m) + jnp.exp(x2 - m))
    q_s[...] = _silu(q_ref[...])
    k_s[...] = (1.0 - lb) * jax.nn.sigmoid(-fx)
    v_s[...] = _silu(i_ref[...])

    bd = _block_diag_mask(GROUP_W, HEAD_W)
    bd_bf16 = bd.astype(BF16)
    tri = (lax.broadcasted_iota(jnp.int32, (ch, ch), 0)
           >= lax.broadcasted_iota(jnp.int32, (ch, ch), 1))
    tri_bf16 = tri.astype(BF16)
    tri3 = (lax.broadcasted_iota(jnp.int32, (ch, ch, 1), 0)
            >= lax.broadcasted_iota(jnp.int32, (ch, ch, 1), 1))

    def chunk(c, carry):
        sl = pl.ds(pl.multiple_of(c * ch, ch), ch)
        qc, kc, vc = q_s[sl, :], k_s[sl, :], v_s[sl, :]
        b = _exact_lhs_mm(tri_bf16, lf_s[sl, :])
        rel = b[:, None, :] - b[None, :, :]
        dec = jnp.exp(jnp.where(tri3, rel, NEG_BIG))
        p = (qc[:, None, :] * kc[None, :, :]) * dec
        sc = jnp.dot(p.reshape(ch * ch, GROUP_W).astype(BF16), bd_bf16,
                     preferred_element_type=F32).reshape(ch, ch, GROUP_W)
        o_intra = jnp.sum(sc * vc[None, :, :], axis=1)
        st = st_ref[...]
        o_inter = lax.dot_general((qc * jnp.exp(b)).astype(BF16), st.astype(BF16),
                                  (_NT, ((), ())), preferred_element_type=F32)
        b_end = b[ch - 1:ch, :]
        kh = kc * jnp.exp(b_end - b)
        upd = lax.dot_general(vc.astype(BF16), kh.astype(BF16), (_TN, ((), ())),
                              preferred_element_type=F32)
        st_ref[...] = st * jnp.exp(b_end) + jnp.where(bd, upd, 0.0)
        o_s[sl, :] = o_intra + o_inter
        return carry

    lax.fori_loop(0, tb // ch, chunk, 0)

    o = o_s[...]
    ms = _head_sum(o * o, bd_bf16) * (1.0 / HEAD_W)
    o = o * lax.rsqrt(ms + RMS_EPS) * nw_ref[...] * _silu(g_ref[...])
    o_ref[...] = _rms(o, mg_ref[...])


def _hgrn(proj3, lb, norm_w, merge_g, tb):
    bsz, t, _ = proj3.shape
    blk = pltpu.VMEM((tb, GROUP_W), F32)
    return pl.pallas_call(
        _hgrn_kernel,
        grid=(bsz, t // tb),
        in_specs=[_col_spec(tb, COL_HQ), _col_spec(tb, COL_HF), _col_spec(tb, COL_HI),
                  _col_spec(tb, COL_HG), _row_spec(), _row_spec(), _row_spec()],
        out_specs=pl.BlockSpec((None, tb, GROUP_W), lambda b, i: (b, i, 0)),
        out_shape=jax.ShapeDtypeStruct((bsz, t, GROUP_W), F32),
        scratch_shapes=[pltpu.VMEM((GROUP_W, GROUP_W), F32), blk, blk, blk, blk, blk],
        compiler_params=_cparams("parallel", "arbitrary"),
        name="hgrn2",
    )(proj3, proj3, proj3, proj3, lb.reshape(1, -1), norm_w.reshape(1, -1), merge_g.reshape(1, -1))


def _lru_kernel(xg_ref, xr_ref, cw_ref, cb_ref, wa_ref, ba_ref, wx_ref, bx_ref, lam_ref, mg_ref,
                o_ref, buf_ref, h_ref):
    tb = xr_ref.shape[0]

    @pl.when(pl.program_id(1) == 0)
    def _():
        buf_ref[0:8, :] = jnp.zeros((8, GROUP_W), F32)
        h_ref[...] = jnp.zeros_like(h_ref)

    xr = xr_ref[...]
    buf_ref[8:8 + tb, :] = xr
    xc = cb_ref[...] + jnp.zeros_like(xr)
    for j in range(LRU_CONV):
        xc = xc + cw_ref[j:j + 1, :] * buf_ref[pl.ds(8 - (LRU_CONV - 1) + j, tb), :]
    buf_ref[0:8, :] = xr[tb - 8:tb, :]

    r = jax.nn.sigmoid(jnp.dot(xc.astype(BF16), wa_ref[...], preferred_element_type=F32) + ba_ref[...])
    gi = jax.nn.sigmoid(jnp.dot(xc.astype(BF16), wx_ref[...], preferred_element_type=F32) + bx_ref[...])
    log_a = -LRU_C * r * _softplus(-lam_ref[...])
    a = jnp.exp(log_a)
    x = jnp.sqrt(1.0 - jnp.exp(2.0 * log_a)) * (gi * xc)

    rows = lax.broadcasted_iota(jnp.int32, (tb, 1), 0)
    k = 1
    while k < tb:
        keep = rows >= k
        x = x + jnp.where(keep, a * pltpu.roll(x, k, axis=0), 0.0)
        a = jnp.where(keep, a * pltpu.roll(a, k, axis=0), a)
        k *= 2
    h = x + a * h_ref[...]
    h_ref[...] = h[tb - 1:tb, :]
    o_ref[...] = _rms(jax.nn.gelu(xg_ref[...]) * h, mg_ref[...])


def _lru(proj3, conv_w, conv_b, wa_bd, ba, wx_bd, bx, lam, merge_g, tb):
    bsz, t, _ = proj3.shape
    return pl.pallas_call(
        _lru_kernel,
        grid=(bsz, t // tb),
        in_specs=[_col_spec(tb, COL_LG), _col_spec(tb, COL_LX), _full_spec((LRU_CONV, GROUP_W)),
                  _row_spec(), _full_spec((GROUP_W, GROUP_W)), _row_spec(),
                  _full_spec((GROUP_W, GROUP_W)), _row_spec(), _row_spec(), _row_spec()],
        out_specs=pl.BlockSpec((None, tb, GROUP_W), lambda b, i: (b, i, 0)),
        out_shape=jax.ShapeDtypeStruct((bsz, t, GROUP_W), F32),
        scratch_shapes=[pltpu.VMEM((tb + 8, GROUP_W), F32), pltpu.VMEM((1, GROUP_W), F32)],
        compiler_params=_cparams("parallel", "arbitrary"),
        name="rglru",
    )(proj3, proj3, conv_w, conv_b.reshape(1, -1), wa_bd, ba.reshape(1, -1), wx_bd,
      bx.reshape(1, -1), lam.reshape(1, -1), merge_g.reshape(1, -1))


def _block_diag_weight(w):
    h, n, _ = w.shape
    eye = jnp.eye(h, dtype=w.dtype)
    return jnp.einsum('hij,hg->higj', w, eye).reshape(h * n, h * n).astype(BF16)


def _s5_matrices(lam_re, lam_im, log_dt, b_re, b_im, c_re, c_im):
    L, G, P, C = S5_CHUNK, S5_GROUPS, S5_STATE, S5_CH
    lr, li = lam_re.astype(F32), lam_im.astype(F32)
    dt = jnp.exp(log_dt.astype(F32))[:, None]
    mag = jnp.exp(lr * dt)
    a_re, a_im = mag * jnp.cos(li * dt), mag * jnp.sin(li * dt)
    den = lr * lr + li * li
    kap_re = ((a_re - 1.0) * lr + a_im * li) / den
    kap_im = (a_im * lr - (a_re - 1.0) * li) / den
    br, bi = b_re.astype(F32), b_im.astype(F32)
    bb_re = kap_re[..., None] * br - kap_im[..., None] * bi
    bb_im = kap_re[..., None] * bi + kap_im[..., None] * br
    cr, ci = c_re.astype(F32), c_im.astype(F32)
    kk = jnp.arange(L + 1, dtype=F32)[:, None, None]
    pw_mag = jnp.exp(kk * (lr * dt)[None])
    pw_re = pw_mag * jnp.cos(kk * (li * dt)[None])
    pw_im = pw_mag * jnp.sin(kk * (li * dt)[None])

    pr, pi = pw_re[L - 1::-1][:L], pw_im[L - 1::-1][:L]
    w_re = jnp.einsum('tgp,gpc->gtcp', pr, bb_re) - jnp.einsum('tgp,gpc->gtcp', pi, bb_im)
    w_im = jnp.einsum('tgp,gpc->gtcp', pr, bb_im) + jnp.einsum('tgp,gpc->gtcp', pi, bb_re)
    ab_re = jnp.einsum('kgp,gpc->kgpc', pw_re[:L], bb_re) - jnp.einsum('kgp,gpc->kgpc', pw_im[:L], bb_im)
    ab_im = jnp.einsum('kgp,gpc->kgpc', pw_re[:L], bb_im) + jnp.einsum('kgp,gpc->kgpc', pw_im[:L], bb_re)
    taps = jnp.einsum('gop,kgpc->gkco', cr, ab_re) - jnp.einsum('gop,kgpc->gkco', ci, ab_im)
    lag = jnp.arange(L)[None, :] - jnp.arange(L)[:, None]
    m = jnp.where((lag >= 0)[None, :, :, None, None],
                  taps[:, jnp.clip(lag, 0, L - 1)], 0.0)
    m = m.transpose(0, 1, 3, 2, 4).reshape(G, L * C, L * C)
    qr = jnp.einsum('gop,tgp->gpto', cr, pw_re[1:]) - jnp.einsum('gop,tgp->gpto', ci, pw_im[1:])
    qi = -(jnp.einsum('gop,tgp->gpto', cr, pw_im[1:]) + jnp.einsum('gop,tgp->gpto', ci, pw_re[1:]))

    w_re = w_re.reshape(G // 2, 2, L * C, P)
    w_im = w_im.reshape(G // 2, 2, L * C, P)
    eye2 = jnp.eye(2, dtype=F32)
    wre_p = jnp.einsum('ajip,jk->ajikp', w_re, eye2).reshape(G // 2, 2 * L * C, 2 * P)
    wim_p = jnp.einsum('ajip,jk->ajikp', w_im, eye2).reshape(G // 2, 2 * L * C, 2 * P)
    w_pair = jnp.concatenate([wre_p, wim_p], axis=-1)
    m_pair = jnp.einsum('ajio,jk->ajiko', m.reshape(G // 2, 2, L * C, L * C), eye2)
    m_pair = m_pair.reshape(G // 2, 2 * L * C, 2 * L * C)
    qr_p = jnp.einsum('ajpo,jk->ajpko', qr.reshape(G // 2, 2, P, L * C), eye2)
    qi_p = jnp.einsum('ajpo,jk->ajpko', qi.reshape(G // 2, 2, P, L * C), eye2)
    q_pair = jnp.concatenate([qr_p.reshape(G // 2, 2 * P, 2 * L * C),
                              qi_p.reshape(G // 2, 2 * P, 2 * L * C)], axis=1)
    al = jnp.stack([pw_re[L].reshape(G // 2, 2 * P), pw_im[L].reshape(G // 2, 2 * P)], axis=1)
    return w_pair.astype(BF16), m_pair.astype(BF16), q_pair.astype(BF16), al


def _s5_scan_kernel(u_ref, w_ref, m_ref, q_ref, al_ref, y_ref, p_s, x_s, *, bsz):
    rows = u_ref.shape[0]
    half = 2 * S5_STATE
    u = u_ref[...]
    p_s[...] = jnp.dot(u, w_ref[...], preferred_element_type=F32)
    al_re = jnp.broadcast_to(al_ref[0:1, :], (bsz, half))
    al_im = jnp.broadcast_to(al_ref[1:2, :], (bsz, half))

    def step(c, carry):
        xr, xi = carry
        sl = pl.ds(pl.multiple_of(c * bsz, bsz), bsz)
        x_s[sl, 0:half] = xr
        x_s[sl, half:2 * half] = xi
        pr, pi = p_s[sl, 0:half], p_s[sl, half:2 * half]
        return al_re * xr - al_im * xi + pr, al_re * xi + al_im * xr + pi

    zero = jnp.zeros((bsz, half), F32)
    lax.fori_loop(0, rows // bsz, step, (zero, zero))
    y_ref[...] = (jnp.dot(u, m_ref[...], preferred_element_type=F32)
                  + jnp.dot(x_s[...].astype(BF16), q_ref[...], preferred_element_type=F32))


def _s5_scan(u_pairs, w_pair, m_pair, q_pair, al, bsz):
    npair, rows, width = u_pairs.shape
    half = 2 * S5_STATE
    spec3 = lambda a, b: pl.BlockSpec((None, a, b), lambda g: (g, 0, 0))
    return pl.pallas_call(
        functools.partial(_s5_scan_kernel, bsz=bsz),
        grid=(npair,),
        in_specs=[spec3(rows, width), spec3(width, 2 * half), spec3(width, width),
                  spec3(2 * half, width), spec3(2, half)],
        out_specs=spec3(rows, width),
        out_shape=jax.ShapeDtypeStruct((npair, rows, width), F32),
        scratch_shapes=[pltpu.VMEM((rows, 2 * half), F32), pltpu.VMEM((rows, 2 * half), F32)],
        compiler_params=_cparams("parallel"),
        name="s5_scan",
    )(u_pairs, w_pair, m_pair, q_pair, al)


def _s5_out_kernel(y_ref, u_ref, d_ref, wg_ref, mg_ref, o_ref):
    y = jax.nn.gelu(y_ref[...] + d_ref[...] * u_ref[...])
    z = jnp.dot(y.astype(BF16), wg_ref[...], preferred_element_type=F32)
    o_ref[...] = _rms(z[:, :GROUP_W] * jax.nn.sigmoid(z[:, GROUP_W:]), mg_ref[...])


def _s5_out(y3, proj3, d_skip, w_glu_bf16, merge_g, tb):
    bsz, t, _ = y3.shape
    return pl.pallas_call(
        _s5_out_kernel,
        grid=(bsz, t // tb),
        in_specs=[pl.BlockSpec((None, tb, GROUP_W), lambda b, i: (b, i, 0)), _col_spec(tb, COL_S5),
                  _row_spec(), _full_spec((GROUP_W, 2 * GROUP_W)), _row_spec()],
        out_specs=pl.BlockSpec((None, tb, GROUP_W), lambda b, i: (b, i, 0)),
        out_shape=jax.ShapeDtypeStruct((bsz, t, GROUP_W), F32),
        compiler_params=_cparams("parallel", "parallel"),
        name="s5_out",
    )(y3, proj3, d_skip.reshape(1, -1), w_glu_bf16, merge_g.reshape(1, -1))


def _s5(proj3, mats, d_skip, w_glu_bf16, merge_g, tb):
    bsz, t, _ = proj3.shape
    L, C, G = S5_CHUNK, S5_CH, S5_GROUPS
    u = proj3[:, :, COL_S5 * GROUP_W:(COL_S5 + 1) * GROUP_W].astype(BF16)
    u_pairs = (u.reshape(bsz, t // L, L, G // 2, 2, C).transpose(3, 1, 0, 4, 2, 5)
               .reshape(G // 2, (t // L) * bsz, 2 * L * C))
    y_pairs = _s5_scan(u_pairs, *mats, bsz)
    y3 = (y_pairs.reshape(G // 2, t // L, bsz, 2, L, C).transpose(2, 1, 4, 0, 3, 5)
          .reshape(bsz, t, GROUP_W))
    return _s5_out(y3, proj3, d_skip, w_glu_bf16, merge_g, tb)


def _rwkv_kernel(*refs, has_vmix):
    if has_vmix:
        (r_ref, k_ref, v_ref, lo_ref, vf_ref, mu_ref, w0_ref, w2_ref, a0_ref, a2_ref, g2_ref,
         kk_ref, ka_ref, rk_ref, lnw_ref, lnb_ref, mg_ref, v0_ref, v1_ref, v2_ref,
         o_ref, ht_ref, prev_ref, lw_s, a_s, b_s, k_s, r_s, v_s, y_s) = refs
    else:
        (r_ref, k_ref, v_ref, lo_ref, mu_ref, w0_ref, w2_ref, a0_ref, a2_ref, g2_ref,
         kk_ref, ka_ref, rk_ref, lnw_ref, lnb_ref, mg_ref,
         o_ref, vf_out_ref, ht_ref, prev_ref, lw_s, a_s, b_s, k_s, r_s, v_s, y_s) = refs
    tb = r_ref.shape[0]
    ch = RWKV_CHUNK
    nh = GROUP_W // HEAD_W

    @pl.when(pl.program_id(1) == 0)
    def _():
        ht_ref[...] = jnp.zeros_like(ht_ref)
        prev_ref[...] = jnp.zeros_like(prev_ref)

    row0 = lax.broadcasted_iota(jnp.int32, (tb, 1), 0) == 0

    def mixed(ref, j):
        p = ref[...]
        lanes = slice(j * GROUP_W, (j + 1) * GROUP_W)
        sh = jnp.where(row0, prev_ref[0:1, lanes], pltpu.roll(p, 1, axis=0))
        prev_new = p[tb - 1:tb, :]
        return p + mu_ref[0:1, lanes] * (sh - p), prev_new

    r, pr_ = mixed(r_ref, 0)
    k, pk_ = mixed(k_ref, 1)
    v, pv_ = mixed(v_ref, 2)
    lo, pl_ = mixed(lo_ref, 3)
    for j, pn in enumerate((pr_, pk_, pv_, pl_)):
        prev_ref[0:1, j * GROUP_W:(j + 1) * GROUP_W] = pn

    bd = _block_diag_mask(GROUP_W, HEAD_W)
    bd_bf16 = bd.astype(BF16)

    w_raw = -_softplus(-(w0_ref[...] + _mm(jnp.tanh(lo), w2_ref[...]))) - 0.5
    lw_s[...] = -jnp.exp(w_raw)
    alr = jax.nn.sigmoid(a0_ref[...] + _mm(lo, a2_ref[...]))
    g = _mm(jax.nn.sigmoid(lo), g2_ref[...])
    kkr = k * kk_ref[...]
    kk = kkr / jnp.maximum(jnp.sqrt(_head_sum(kkr * kkr, bd_bf16)), 1e-12)
    k = k * (1.0 + (alr - 1.0) * ka_ref[...])
    if has_vmix:
        gate = jax.nn.sigmoid(v0_ref[...] + _mm(_mm(v, v1_ref[...]), v2_ref[...]))
        v = v + (vf_ref[...] - v) * gate
    else:
        vf_out_ref[...] = v
    a_s[...] = -kk
    b_s[...] = kk * alr
    k_s[...] = k
    r_s[...] = r
    v_s[...] = v

    n4 = nh * ch
    ri = lax.broadcasted_iota(jnp.int32, (n4, n4), 0)
    ci = lax.broadcasted_iota(jnp.int32, (n4, n4), 1)
    same_head = (ri // ch) == (ci // ch)
    strict = same_head & ((ri % ch) > (ci % ch))
    incl = same_head & ((ri % ch) >= (ci % ch))
    eye = (ri == ci).astype(F32)
    hm = ((lax.broadcasted_iota(jnp.int32, (n4, GROUP_W), 0) // ch)
          == (lax.broadcasted_iota(jnp.int32, (n4, GROUP_W), 1) // HEAD_W))
    tri = (lax.broadcasted_iota(jnp.int32, (ch, ch), 0)
           >= lax.broadcasted_iota(jnp.int32, (ch, ch), 1)).astype(BF16)

    def stack(x):
        return jnp.where(hm, jnp.concatenate([x] * nh, axis=0), 0.0)

    def chunk(c, carry):
        sl = pl.ds(pl.multiple_of(c * ch, ch), ch)
        lw = lw_s[sl, :]
        cl = _exact_lhs_mm(tri, lw)
        cl_end = cl[ch - 1:ch, :]
        e_in, e_ex = jnp.exp(cl), jnp.exp(cl - lw)
        e_neg, e_end = jnp.exp(-cl), jnp.exp(cl_end - cl)
        av, bv, kv, rv, vv = a_s[sl, :], b_s[sl, :], k_s[sl, :], r_s[sl, :], v_s[sl, :]
        at4, rt4 = stack(av * e_ex), stack(rv * e_in)
        bt4 = jnp.concatenate([bv * e_neg] * nh, axis=0)
        kt4 = jnp.concatenate([kv * e_neg] * nh, axis=0)
        v4 = stack(vv)
        ar = jnp.concatenate([at4, rt4], axis=0)
        sb = _dg3(ar, bt4, _NT)
        sk = _dg3(ar, kt4, _NT)
        l_ab = jnp.where(strict, sb[:n4], 0.0)
        l_ak = jnp.where(strict, sk[:n4], 0.0)
        l_rb = jnp.where(incl, sb[n4:], 0.0)
        l_rk = jnp.where(incl, sk[n4:], 0.0)
        tinv = eye + l_ab
        pw = l_ab
        span = 2
        while span < ch:
            pw = _dg3(pw, pw, _NN)
            tinv = tinv + _dg3(tinv, pw, _NN)
            span *= 2
        ht = ht_ref[...]
        ht_b = ht.astype(BF16)
        nt = lambda x, y: lax.dot_general(x.astype(BF16), y, (_NT, ((), ())), preferred_element_type=F32)
        wmat = nt(at4, ht_b) + _mm(l_ak, v4)
        u = _mm(tinv, wmat)
        y4 = nt(rt4, ht_b) + _mm(l_rb, u) + _mm(l_rk, v4)
        y = y4[0:ch]
        for h in range(1, nh):
            y = y + y4[h * ch:(h + 1) * ch]
        y_s[sl, :] = y
        tn = lambda x, y_: lax.dot_general(x.astype(BF16), y_.astype(BF16), (_TN, ((), ())),
                                           preferred_element_type=F32)
        ht_ref[...] = (ht * jnp.exp(cl_end) + tn(u, stack(bv * e_end)) + tn(v4, stack(kv * e_end)))
        return carry

    lax.fori_loop(0, tb // ch, chunk, 0)

    y = y_s[...]
    r, k, v = r_s[...], k_s[...], v_s[...]
    mean = _head_sum(y, bd_bf16) * (1.0 / HEAD_W)
    d = y - mean
    var = _head_sum(d * d, bd_bf16) * (1.0 / HEAD_W)
    y = d * lax.rsqrt(var + RWKV_GN_EPS) * lnw_ref[...] + lnb_ref[...]
    y = y + _head_sum(r * k * rk_ref[...], bd_bf16) * v
    o_ref[...] = _rms(y * g, mg_ref[...])


def _pad_rows(w, start, total=GROUP_W):
    out = jnp.zeros((total, w.shape[1]), F32).at[start:start + w.shape[0]].set(w)
    return out.astype(BF16)


def _rwkv(proj3, v_first, lyr, p, merge_g, tb):
    bsz, t, _ = proj3.shape
    has_vmix = v_first is not None
    row = lambda x: x.reshape(1, -1)
    w2 = _pad_rows(p["rwkv_w2"][lyr], 0)
    a2 = _pad_rows(p["rwkv_a2"][lyr], RWKV_W_LORA)
    g2 = _pad_rows(p["rwkv_g2"][lyr], RWKV_W_LORA + RWKV_A_LORA)
    args = [proj3, proj3, proj3, proj3]
    specs = [_col_spec(tb, COL_R), _col_spec(tb, COL_K), _col_spec(tb, COL_V), _col_spec(tb, COL_LORA)]
    if has_vmix:
        args.append(v_first)
        specs.append(pl.BlockSpec((None, tb, GROUP_W), lambda b, i: (b, i, 0)))
    args += [row(p["rwkv_mu"][lyr]), row(p["rwkv_w0"][lyr]), w2, row(p["rwkv_a0"][lyr]), a2, g2,
             row(p["rwkv_k_k"][lyr]), row(p["rwkv_k_a"][lyr]), row(p["rwkv_r_k"][lyr]),
             row(p["rwkv_ln_w"][lyr]), row(p["rwkv_ln_b"][lyr]), row(merge_g)]
    sq = _full_spec((GROUP_W, GROUP_W))
    specs += [_row_spec(4 * GROUP_W), _row_spec(), sq, _row_spec(), sq, sq,
              _row_spec(), _row_spec(), _row_spec(), _row_spec(), _row_spec(), _row_spec()]
    if has_vmix:
        v1 = jnp.zeros((GROUP_W, 128), F32).at[:, :RWKV_V_LORA].set(p["rwkv_v1"][lyr - 1]).astype(BF16)
        v2 = jnp.zeros((128, GROUP_W), F32).at[:RWKV_V_LORA].set(p["rwkv_v2"][lyr - 1]).astype(BF16)
        args += [row(p["rwkv_v0"][lyr - 1]), v1, v2]
        specs += [_row_spec(), _full_spec((GROUP_W, 128)), _full_spec((128, GROUP_W))]
    out_blk = pl.BlockSpec((None, tb, GROUP_W), lambda b, i: (b, i, 0))
    out_sds = jax.ShapeDtypeStruct((bsz, t, GROUP_W), F32)
    blk = pltpu.VMEM((tb, GROUP_W), F32)
    res = pl.pallas_call(
        functools.partial(_rwkv_kernel, has_vmix=has_vmix),
        grid=(bsz, t // tb),
        in_specs=specs,
        out_specs=out_blk if has_vmix else [out_blk, out_blk],
        out_shape=out_sds if has_vmix else [out_sds, out_sds],
        scratch_shapes=[pltpu.VMEM((GROUP_W, GROUP_W), F32), pltpu.VMEM((1, 4 * GROUP_W), F32),
                        blk, blk, blk, blk, blk, blk, blk],
        compiler_params=_cparams("parallel", "arbitrary"),
        name="rwkv7",
    )(*args)
    if has_vmix:
        return res, v_first
    return res[0], res[1]


def _bf16_bits(x):
    u = lax.bitcast_convert_type(x, jnp.uint32)
    r = u + jnp.uint32(0x7FFF) + ((u >> 16) & jnp.uint32(1))
    return r & jnp.uint32(0xFFFF0000)


def _merge_router_kernel(oa_ref, ob_ref, oc_ref, od_ref, h_ref, wo_ref, nw_ref, wr_ref, br_ref,
                         h1_ref, hn_ref, ids_ref, wts_ref):
    mix = jnp.concatenate([oa_ref[...], ob_ref[...], oc_ref[...], od_ref[...]], axis=1)
    h1 = h_ref[...] + jnp.dot(mix.astype(BF16), wo_ref[...], preferred_element_type=F32)
    h1_ref[...] = h1
    hn = _rms(h1, nw_ref[...])
    half = hn.shape[1] // 2
    hn_ref[...] = _bf16_bits(hn[:, :half]) | (_bf16_bits(hn[:, half:]) >> 16)
    logits = _dg3(hn, wr_ref[...], _NN) + br_ref[...]
    lane = lax.broadcasted_iota(jnp.int32, logits.shape, 1)
    big = jnp.int32(ROUTER_LANES)
    is_c = (lane >= N_EXPERTS) & (lane < N_EXPERTS + MOE_GROUPS)
    cm = jnp.max(jnp.where(is_c, logits, NEG_BIG), axis=-1, keepdims=True)
    gsel = jnp.min(jnp.where(is_c & (logits == cm), lane, big), axis=-1, keepdims=True) - N_EXPERTS
    p_g = 1.0 / jnp.sum(jnp.where(is_c, jnp.exp(logits - cm), 0.0), axis=-1, keepdims=True)
    lo = gsel * MOE_PER_GROUP
    in_g = (lane >= lo) & (lane < lo + MOE_PER_GROUP)
    m1 = jnp.max(jnp.where(in_g, logits, NEG_BIG), axis=-1, keepdims=True)
    i1 = jnp.min(jnp.where(in_g & (logits == m1), lane, big), axis=-1, keepdims=True)
    in_g2 = in_g & (lane != i1)
    m2 = jnp.max(jnp.where(in_g2, logits, NEG_BIG), axis=-1, keepdims=True)
    i2 = jnp.min(jnp.where(in_g2 & (logits == m2), lane, big), axis=-1, keepdims=True)
    w1 = p_g / (1.0 + jnp.exp(m2 - m1))
    w2 = p_g - w1
    two = lax.broadcasted_iota(jnp.int32, ids_ref.shape, 1)
    ids_ref[...] = jnp.where(two == 0, i1, i2)
    wts_ref[...] = jnp.where(two == 0, w1, w2)


def _merge_router(outs, h2d, w_out_bf16, norm_w, w_router, b_router, tm):
    n, d = h2d.shape
    grp = pl.BlockSpec((tm, GROUP_W), lambda i: (i, 0))
    full = lambda a, b: pl.BlockSpec((a, b), lambda i: (0, 0))
    rowblk = lambda w: pl.BlockSpec((tm, w), lambda i: (i, 0))
    return pl.pallas_call(
        _merge_router_kernel,
        grid=(n // tm,),
        in_specs=[grp, grp, grp, grp, rowblk(d), full(d, d), full(1, d), full(d, ROUTER_LANES),
                  full(1, ROUTER_LANES)],
        out_specs=[rowblk(d), rowblk(d // 2), rowblk(MOE_TOPK), rowblk(MOE_TOPK)],
        out_shape=[jax.ShapeDtypeStruct((n, d), F32), jax.ShapeDtypeStruct((n, d // 2), jnp.uint32),
                   jax.ShapeDtypeStruct((n, MOE_TOPK), jnp.int32),
                   jax.ShapeDtypeStruct((n, MOE_TOPK), F32)],
        compiler_params=_cparams("parallel"),
        name="merge_router",
    )(*[o.reshape(n, GROUP_W) for o in outs], h2d, w_out_bf16, norm_w.reshape(1, d), w_router, b_router)


def _row_copy(src_hbm, src_row, dst_ref, dst_row, sem):
    return pltpu.make_async_copy(src_hbm.at[pl.ds(src_row, 1), :], dst_ref.at[pl.ds(dst_row, 1), :], sem)


def _gather_kernel(idx_ref, x_hbm, o_ref, sem):
    n = o_ref.shape[0]

    def issue(i, c):
        _row_copy(x_hbm, idx_ref[i], o_ref, i, sem).start()
        return c

    def drain(i, c):
        _row_copy(x_hbm, 0, o_ref, i, sem).wait()
        return c

    lax.fori_loop(0, n, issue, 0, unroll=8)
    lax.fori_loop(0, n, drain, 0, unroll=8)


def _gather_rows(x, idx, chunk):
    p = idx.shape[0]
    w = x.shape[1]
    return pl.pallas_call(
        _gather_kernel,
        grid=(p // chunk,),
        in_specs=[pl.BlockSpec((chunk,), lambda s: (s,), memory_space=pltpu.SMEM),
                  pl.BlockSpec(memory_space=pl.ANY)],
        out_specs=pl.BlockSpec((chunk, w), lambda s: (s, 0)),
        out_shape=jax.ShapeDtypeStruct((p, w), x.dtype),
        scratch_shapes=[pltpu.SemaphoreType.DMA(())],
        compiler_params=_cparams("arbitrary"),
        name="moe_gather",
    )(idx, x)


def _expert_kernel(te_ref, nt_ref, xs_ref, w_ref, wg_ref, wu_ref, wd_ref, o_ref):
    i = pl.program_id(0)

    @pl.when(i < nt_ref[0])
    def _():
        p = xs_ref[...]
        xa = lax.bitcast_convert_type(p & jnp.uint32(0xFFFF0000), F32).astype(BF16)
        xb = lax.bitcast_convert_type(p << 16, F32).astype(BF16)
        x = jnp.concatenate([xa, xb], axis=1)
        he = (_silu(jnp.dot(x, wg_ref[...], preferred_element_type=F32))
              * jnp.dot(x, wu_ref[...], preferred_element_type=F32))
        o_ref[...] = w_ref[...] * jnp.dot(he.astype(BF16), wd_ref[...], preferred_element_type=F32)

    @pl.when(i >= nt_ref[0])
    def _():
        o_ref[...] = jnp.zeros_like(o_ref)


def _experts(xs, w_pos, tile_expert, n_tiles_used, wg, wu, wd, tm):
    p, half = xs.shape
    d = 2 * half
    grid_spec = pltpu.PrefetchScalarGridSpec(
        num_scalar_prefetch=2,
        grid=(p // tm,),
        in_specs=[pl.BlockSpec((tm, half), lambda i, te, nt: (i, 0)),
                  pl.BlockSpec((tm, 1), lambda i, te, nt: (i, 0)),
                  pl.BlockSpec((None, d, D_EXPERT), lambda i, te, nt: (te[i], 0, 0)),
                  pl.BlockSpec((None, d, D_EXPERT), lambda i, te, nt: (te[i], 0, 0)),
                  pl.BlockSpec((None, D_EXPERT, d), lambda i, te, nt: (te[i], 0, 0))],
        out_specs=pl.BlockSpec((tm, d), lambda i, te, nt: (i, 0)),
    )
    return pl.pallas_call(
        _expert_kernel,
        grid_spec=grid_spec,
        out_shape=jax.ShapeDtypeStruct((p, d), F32),
        compiler_params=_cparams("arbitrary"),
        name="moe_experts",
    )(tile_expert, n_tiles_used, xs, w_pos.reshape(p, 1), wg, wu, wd)


def _combine_kernel(i0_ref, i1_ref, h1_ref, fw_ref, ys_hbm, o_ref, buf0, buf1, sem, *, final_norm):
    n = o_ref.shape[0]

    def issue(i, c):
        _row_copy(ys_hbm, i0_ref[i], buf0, i, sem).start()
        _row_copy(ys_hbm, i1_ref[i], buf1, i, sem).start()
        return c

    def drain(i, c):
        _row_copy(ys_hbm, 0, buf0, i, sem).wait()
        _row_copy(ys_hbm, 0, buf1, i, sem).wait()
        return c

    lax.fori_loop(0, n, issue, 0, unroll=8)
    lax.fori_loop(0, n, drain, 0, unroll=8)
    out = h1_ref[...] + buf0[...] + buf1[...]
    if final_norm:
        out = _rms(out, fw_ref[...])
    o_ref[...] = out


def _combine(ys, pos0, pos1, h1, final_w, final_norm, chunk):
    n, d = h1.shape
    smem = lambda: pl.BlockSpec((chunk,), lambda s: (s,), memory_space=pltpu.SMEM)
    return pl.pallas_call(
        functools.partial(_combine_kernel, final_norm=final_norm),
        grid=(n // chunk,),
        in_specs=[smem(), smem(), pl.BlockSpec((chunk, d), lambda s: (s, 0)),
                  pl.BlockSpec((1, d), lambda s: (0, 0)), pl.BlockSpec(memory_space=pl.ANY)],
        out_specs=pl.BlockSpec((chunk, d), lambda s: (s, 0)),
        out_shape=jax.ShapeDtypeStruct((n, d), F32),
        scratch_shapes=[pltpu.VMEM((chunk, d), F32), pltpu.VMEM((chunk, d), F32),
                        pltpu.SemaphoreType.DMA(())],
        compiler_params=_cparams("arbitrary"),
        name="moe_combine",
    )(pos0, pos1, h1, final_w.reshape(1, d), ys)


def _route_tables(ids, wts, tm):
    n = ids.shape[0]
    n2 = n * MOE_TOPK
    n_tiles = n2 // tm + N_EXPERTS
    e_flat = ids.reshape(n2)
    order = jnp.argsort(e_flat, stable=True).astype(jnp.int32)
    e_sorted = e_flat[order]
    bounds = jnp.searchsorted(e_sorted, jnp.arange(N_EXPERTS + 1, dtype=jnp.int32)).astype(jnp.int32)
    starts, counts = bounds[:-1], bounds[1:] - bounds[:-1]
    padded = ((counts + tm - 1) // tm) * tm
    ends_p = jnp.cumsum(padded)
    starts_p = ends_p - padded
    tile_expert = jnp.minimum(
        jnp.searchsorted(ends_p, jnp.arange(n_tiles, dtype=jnp.int32) * tm, side='right'),
        N_EXPERTS - 1).astype(jnp.int32)
    n_tiles_used = (ends_p[-1] // tm).astype(jnp.int32).reshape(1)
    pos = jnp.arange(n_tiles * tm, dtype=jnp.int32)
    e_pos = tile_expert[pos // tm]
    rank = pos - starts_p[e_pos]
    valid = (rank < counts[e_pos]) & (pos < ends_p[-1])
    src = order[jnp.clip(starts[e_pos] + rank, 0, n2 - 1)]
    src_token = jnp.where(valid, src // MOE_TOPK, 0).astype(jnp.int32)
    w_pos = jnp.where(valid, wts.reshape(n2)[src], 0.0)
    slot_of = jnp.argsort(order).astype(jnp.int32)
    e_of = e_flat
    pos_of = (starts_p[e_of] + slot_of - starts[e_of]).astype(jnp.int32).reshape(n, MOE_TOPK)
    return src_token, w_pos, tile_expert, n_tiles_used, pos_of[:, 0], pos_of[:, 1]


def _moe(hn_packed, ids, wts, h1, wg, wu, wd, final_w, final_norm, tm):
    n = h1.shape[0]
    src_token, w_pos, tile_expert, n_tiles_used, pos0, pos1 = _route_tables(ids, wts, tm)
    p = src_token.shape[0]
    xs = _gather_rows(hn_packed, src_token, _pick_div(p, 2048))
    ys = _experts(xs, w_pos, tile_expert, n_tiles_used, wg, wu, wd, tm)
    return _combine(ys, pos0, pos1, h1, final_w, final_norm, _pick_div(n, 1024))


def _pick_div(n, pref):
    while n % pref:
        pref //= 2
    return pref


def _pick(n, pref):
    return pref if n % pref == 0 else n


def kernel(x, norm1_w, w_in, hgrn_lb_logits, hgrn_norm_w, s5_lambda_re, s5_lambda_im, s5_log_dt, s5_b_re, s5_b_im, s5_c_re, s5_c_im, s5_d, s5_w_glu, rwkv_mu, rwkv_w0, rwkv_w2, rwkv_a0, rwkv_a2, rwkv_g2, rwkv_k_k, rwkv_k_a, rwkv_r_k, rwkv_v0, rwkv_v1, rwkv_v2, rwkv_ln_w, rwkv_ln_b, lru_conv_w, lru_conv_b, lru_wa, lru_ba, lru_wx, lru_bx, lru_lambda, merge_gain, w_out, norm2_w, moe_coarse_w, moe_coarse_b, moe_fine_w, moe_fine_b, moe_w_gate, moe_w_up, moe_w_down, final_norm_w):
    bsz, t, d = x.shape
    n = bsz * t
    depth = w_in.shape[0]
    rw = dict(rwkv_mu=rwkv_mu, rwkv_w0=rwkv_w0, rwkv_w2=rwkv_w2, rwkv_a0=rwkv_a0, rwkv_a2=rwkv_a2,
              rwkv_g2=rwkv_g2, rwkv_k_k=rwkv_k_k, rwkv_k_a=rwkv_k_a, rwkv_r_k=rwkv_r_k,
              rwkv_v0=rwkv_v0, rwkv_v1=rwkv_v1, rwkv_v2=rwkv_v2, rwkv_ln_w=rwkv_ln_w,
              rwkv_ln_b=rwkv_ln_b)
    lb_all = jnp.cumsum(jax.nn.softmax(hgrn_lb_logits.astype(F32), axis=0), axis=0)
    lb_all = lb_all - lb_all[:1]

    tm_proj = _pick(n, 512)
    tm_moe = 256
    tb_hgrn = _pick(t, 256)
    tb_lru = _pick(t, 256)
    tb_rwkv = _pick(t, 256)
    tb_s5 = _pick(t, 512)

    h = x.reshape(n, d)
    v_first = None
    for l in range(depth):
        mg = merge_gain[l].reshape(4, GROUP_W)
        proj = _in_proj(h, norm1_w[l], w_in[l].astype(BF16), tm_proj)
        proj3 = proj.reshape(bsz, t, -1)
        o_a = _hgrn(proj3, lb_all[l], hgrn_norm_w[l], mg[0], tb_hgrn)
        mats = _s5_matrices(s5_lambda_re[l], s5_lambda_im[l], s5_log_dt[l], s5_b_re[l], s5_b_im[l],
                            s5_c_re[l], s5_c_im[l])
        o_b = _s5(proj3, mats, s5_d[l], s5_w_glu[l].astype(BF16), mg[1], tb_s5)
        o_c, v_first = _rwkv(proj3, v_first, l, rw, mg[2], tb_rwkv)
        o_d = _lru(proj3, lru_conv_w[l], lru_conv_b[l], _block_diag_weight(lru_wa[l]), lru_ba[l],
                   _block_diag_weight(lru_wx[l]), lru_bx[l], lru_lambda[l], mg[3], tb_lru)
        w_router = jnp.concatenate(
            [moe_fine_w[l].transpose(1, 0, 2).reshape(d, N_EXPERTS), moe_coarse_w[l],
             jnp.zeros((d, ROUTER_LANES - N_EXPERTS - MOE_GROUPS), F32)], axis=1)
        b_router = jnp.concatenate(
            [moe_fine_b[l].reshape(N_EXPERTS), moe_coarse_b[l],
             jnp.zeros((ROUTER_LANES - N_EXPERTS - MOE_GROUPS,), F32)]).reshape(1, ROUTER_LANES)
        h1, hn, ids, wts = _merge_router((o_a, o_b, o_c, o_d), h, w_out[l].astype(BF16), norm2_w[l],
                                         w_router, b_router, tm_proj)
        h = _moe(hn, ids, wts, h1, moe_w_gate[l].astype(BF16), moe_w_up[l].astype(BF16),
                 moe_w_down[l].astype(BF16), final_norm_w, l == depth - 1, tm_moe)
    return h.reshape(bsz, t, d)
```

```python
import functools
import math

import jax
import jax.numpy as jnp
from jax import lax
from jax.experimental import pallas as pl
from jax.experimental.pallas import tpu as pltpu

F32 = jnp.float32
BF16 = jnp.bfloat16

D_MODEL = 1024
GROUP_W = 256
RMS_EPS = 1e-6
HEAD_W = 64
HGRN_CHUNK = 16
S5_CH = 16
S5_GROUPS = GROUP_W // S5_CH
S5_STATE = 64
S5_CHUNK = 16
RWKV_CHUNK = 64
RWKV_GN_EPS = 64e-5
RWKV_W_LORA = 64
RWKV_A_LORA = 64
RWKV_G_LORA = 128
RWKV_V_LORA = 32
LRU_CONV = 4
LRU_C = 8.0
MOE_GROUPS = 4
MOE_PER_GROUP = 8
N_EXPERTS = MOE_GROUPS * MOE_PER_GROUP
MOE_TOPK = 2
D_EXPERT = 512
ROUTER_LANES = 128
NEG_BIG = -1e30
VMEM_LIMIT = 56 * 1024 * 1024

COL_HQ, COL_HF, COL_HI, COL_HG, COL_S5, COL_R, COL_K, COL_V, COL_LORA, COL_LG, COL_LX = range(11)


def _cparams(*sem):
    return pltpu.CompilerParams(dimension_semantics=sem, vmem_limit_bytes=VMEM_LIMIT)


def _mm(a, b):
    return jnp.dot(a.astype(BF16), b.astype(BF16), preferred_element_type=F32)


def _split2(x):
    hi = x.astype(BF16)
    lo = (x - hi.astype(F32)).astype(BF16)
    return hi, lo


def _dg3(a, b, dims):
    ah, al = _split2(a)
    bh, bl = _split2(b)
    d = lambda x, y: lax.dot_general(x, y, (dims, ((), ())), preferred_element_type=F32)
    return d(ah, bh) + d(ah, bl) + d(al, bh)


_NN = ((1,), (0,))
_NT = ((1,), (1,))
_TN = ((0,), (0,))


def _exact_lhs_mm(m_bf16, x):
    h1 = x.astype(BF16)
    r1 = x - h1.astype(F32)
    h2 = r1.astype(BF16)
    h3 = (r1 - h2.astype(F32)).astype(BF16)
    d = lambda y: jnp.dot(m_bf16, y, preferred_element_type=F32)
    return d(h1) + d(h2) + d(h3)


def _head_sum(x, bd_bf16):
    h1 = x.astype(BF16)
    r1 = x - h1.astype(F32)
    h2 = r1.astype(BF16)
    h3 = (r1 - h2.astype(F32)).astype(BF16)
    d = lambda y: jnp.dot(y, bd_bf16, preferred_element_type=F32)
    return d(h1) + d(h2) + d(h3)


def _block_diag_mask(n, blk):
    r = lax.broadcasted_iota(jnp.int32, (n, n), 0) // blk
    c = lax.broadcasted_iota(jnp.int32, (n, n), 1) // blk
    return r == c


def _rms(x, w):
    return x * lax.rsqrt(jnp.mean(x * x, axis=-1, keepdims=True) + RMS_EPS) * w


def _silu(x):
    return x * jax.nn.sigmoid(x)


def _softplus(x):
    return jnp.maximum(x, 0.0) + jnp.log(1.0 + jnp.exp(-jnp.abs(x)))


def _in_proj_kernel(x_ref, nw_ref, w_ref, o_ref):
    y = _rms(x_ref[...], nw_ref[...])
    o_ref[...] = jnp.dot(y.astype(BF16), w_ref[...], preferred_element_type=F32)


def _in_proj(h2d, norm_w, w_bf16, tm):
    n, d = h2d.shape
    d_in = w_bf16.shape[1]
    return pl.pallas_call(
        _in_proj_kernel,
        grid=(n // tm,),
        in_specs=[pl.BlockSpec((tm, d), lambda i: (i, 0)),
                  pl.BlockSpec((1, d), lambda i: (0, 0)),
                  pl.BlockSpec((d, d_in), lambda i: (0, 0))],
        out_specs=pl.BlockSpec((tm, d_in), lambda i: (i, 0)),
        out_shape=jax.ShapeDtypeStruct((n, d_in), F32),
        compiler_params=_cparams("parallel"),
        name="in_proj",
    )(h2d, norm_w.reshape(1, d), w_bf16)


def _col_spec(tb, col):
    return pl.BlockSpec((None, tb, GROUP_W), lambda b, t: (b, t, col))


def _row_spec(width=GROUP_W):
    return pl.BlockSpec((1, width), lambda b, t: (0, 0))


def _full_spec(shape):
    return pl.BlockSpec(shape, lambda b, t: (0,) * len(shape))


def _hgrn_kernel(q_ref, f_ref, i_ref, g_ref, lb_ref, nw_ref, mg_ref, o_ref,
                 st_ref, q_s, k_s, v_s, lf_s, o_s):
    ch = HGRN_CHUNK
    tb = q_ref.shape[0]

    @pl.when(pl.program_id(1) == 0)
    def _():
        st_ref[...] = jnp.zeros_like(st_ref)

    lb = lb_ref[...]
    fx = f_ref[...]
    x1 = jnp.log(lb)
    x2 = jnp.log(1.0 - lb) - _softplus(-fx)
    m = jnp.maximum(x1, x2)
    lf_s[...] = m + jnp.log(jnp.exp(x1 - m) + jnp.exp(x2 - m))
    q_s[...] = _silu(q_ref[...])
    k_s[...] = (1.0 - lb) * jax.nn.sigmoid(-fx)
    v_s[...] = _silu(i_ref[...])

    bd = _block_diag_mask(GROUP_W, HEAD_W)
    bd_bf16 = bd.astype(BF16)
    tri = (lax.broadcasted_iota(jnp.int32, (ch, ch), 0)
           >= lax.broadcasted_iota(jnp.int32, (ch, ch), 1))
    tri_bf16 = tri.astype(BF16)
    tri3 = (lax.broadcasted_iota(jnp.int32, (ch, ch, 1), 0)
            >= lax.broadcasted_iota(jnp.int32, (ch, ch, 1), 1))

    def chunk(c, carry):
        sl = pl.ds(pl.multiple_of(c * ch, ch), ch)
        qc, kc, vc = q_s[sl, :], k_s[sl, :], v_s[sl, :]
        b = _exact_lhs_mm(tri_bf16, lf_s[sl, :])
        rel = b[:, None, :] - b[None, :, :]
        dec = jnp.exp(jnp.where(tri3, rel, NEG_BIG))
        p = (qc[:, None, :] * kc[None, :, :]) * dec
        sc = jnp.dot(p.reshape(ch * ch, GROUP_W).astype(BF16), bd_bf16,
                     preferred_element_type=F32).reshape(ch, ch, GROUP_W)
        o_intra = jnp.sum(sc * vc[None, :, :], axis=1)
        st = st_ref[...]
        o_inter = lax.dot_general((qc * jnp.exp(b)).astype(BF16), st.astype(BF16),
                                  (_NT, ((), ())), preferred_element_type=F32)
        b_end = b[ch - 1:ch, :]
        kh = kc * jnp.exp(b_end - b)
        upd = lax.dot_general(vc.astype(BF16), kh.astype(BF16), (_TN, ((), ())),
                              preferred_element_type=F32)
        st_ref[...] = st * jnp.exp(b_end) + jnp.where(bd, upd, 0.0)
        o_s[sl, :] = o_intra + o_inter
        return carry

    lax.fori_loop(0, tb // ch, chunk, 0)

    o = o_s[...]
    ms = _head_sum(o * o, bd_bf16) * (1.0 / HEAD_W)
    o = o * lax.rsqrt(ms + RMS_EPS) * nw_ref[...] * _silu(g_ref[...])
    o_ref[...] = _rms(o, mg_ref[...])


def _hgrn(proj3, lb, norm_w, merge_g, tb):
    bsz, t, _ = proj3.shape
    blk = pltpu.VMEM((tb, GROUP_W), F32)
    return pl.pallas_call(
        _hgrn_kernel,
        grid=(bsz, t // tb),
        in_specs=[_col_spec(tb, COL_HQ), _col_spec(tb, COL_HF), _col_spec(tb, COL_HI),
                  _col_spec(tb, COL_HG), _row_spec(), _row_spec(), _row_spec()],
        out_specs=pl.BlockSpec((None, tb, GROUP_W), lambda b, i: (b, i, 0)),
        out_shape=jax.ShapeDtypeStruct((bsz, t, GROUP_W), F32),
        scratch_shapes=[pltpu.VMEM((GROUP_W, GROUP_W), F32), blk, blk, blk, blk, blk],
        compiler_params=_cparams("parallel", "arbitrary"),
        name="hgrn2",
    )(proj3, proj3, proj3, proj3, lb.reshape(1, -1), norm_w.reshape(1, -1), merge_g.reshape(1, -1))


def _lru_kernel(xg_ref, xr_ref, cw_ref, cb_ref, wa_ref, ba_ref, wx_ref, bx_ref, lam_ref, mg_ref,
                o_ref, buf_ref, h_ref):
    tb = xr_ref.shape[0]

    @pl.when(pl.program_id(1) == 0)
    def _():
        buf_ref[0:8, :] = jnp.zeros((8, GROUP_W), F32)
        h_ref[...] = jnp.zeros_like(h_ref)

    xr = xr_ref[...]
    buf_ref[8:8 + tb, :] = xr
    xc = cb_ref[...] + jnp.zeros_like(xr)
    for j in range(LRU_CONV):
        xc = xc + cw_ref[j:j + 1, :] * buf_ref[pl.ds(8 - (LRU_CONV - 1) + j, tb), :]
    buf_ref[0:8, :] = xr[tb - 8:tb, :]

    r = jax.nn.sigmoid(jnp.dot(xc.astype(BF16), wa_ref[...], preferred_element_type=F32) + ba_ref[...])
    gi = jax.nn.sigmoid(jnp.dot(xc.astype(BF16), wx_ref[...], preferred_element_type=F32) + bx_ref[...])
    log_a = -LRU_C * r * _softplus(-lam_ref[...])
    a = jnp.exp(log_a)
    x = jnp.sqrt(1.0 - jnp.exp(2.0 * log_a)) * (gi * xc)

    rows = lax.broadcasted_iota(jnp.int32, (tb, 1), 0)
    k = 1
    while k < tb:
        keep = rows >= k
        x = x + jnp.where(keep, a * pltpu.roll(x, k, axis=0), 0.0)
        a = jnp.where(keep, a * pltpu.roll(a, k, axis=0), a)
        k *= 2
    h = x + a * h_ref[...]
    h_ref[...] = h[tb - 1:tb, :]
    o_ref[...] = _rms(jax.nn.gelu(xg_ref[...]) * h, mg_ref[...])


def _lru(proj3, conv_w, conv_b, wa_bd, ba, wx_bd, bx, lam, merge_g, tb):
    bsz, t, _ = proj3.shape
    return pl.pallas_call(
        _lru_kernel,
        grid=(bsz, t // tb),
        in_specs=[_col_spec(tb, COL_LG), _col_spec(tb, COL_LX), _full_spec((LRU_CONV, GROUP_W)),
                  _row_spec(), _full_spec((GROUP_W, GROUP_W)), _row_spec(),
                  _full_spec((GROUP_W, GROUP_W)), _row_spec(), _row_spec(), _row_spec()],
        out_specs=pl.BlockSpec((None, tb, GROUP_W), lambda b, i: (b, i, 0)),
        out_shape=jax.ShapeDtypeStruct((bsz, t, GROUP_W), F32),
        scratch_shapes=[pltpu.VMEM((tb + 8, GROUP_W), F32), pltpu.VMEM((1, GROUP_W), F32)],
        compiler_params=_cparams("parallel", "arbitrary"),
        name="rglru",
    )(proj3, proj3, conv_w, conv_b.reshape(1, -1), wa_bd, ba.reshape(1, -1), wx_bd,
      bx.reshape(1, -1), lam.reshape(1, -1), merge_g.reshape(1, -1))


def _block_diag_weight(w):
    h, n, _ = w.shape
    eye = jnp.eye(h, dtype=w.dtype)
    return jnp.einsum('hij,hg->higj', w, eye).reshape(h * n, h * n).astype(BF16)


def _s5_matrices(lam_re, lam_im, log_dt, b_re, b_im, c_re, c_im):
    L, G, P, C = S5_CHUNK, S5_GROUPS, S5_STATE, S5_CH
    lr, li = lam_re.astype(F32), lam_im.astype(F32)
    dt = jnp.exp(log_dt.astype(F32))[:, None]
    mag = jnp.exp(lr * dt)
    a_re, a_im = mag * jnp.cos(li * dt), mag * jnp.sin(li * dt)
    den = lr * lr + li * li
    kap_re = ((a_re - 1.0) * lr + a_im * li) / den
    kap_im = (a_im * lr - (a_re - 1.0) * li) / den
    br, bi = b_re.astype(F32), b_im.astype(F32)
    bb_re = kap_re[..., None] * br - kap_im[..., None] * bi
    bb_im = kap_re[..., None] * bi + kap_im[..., None] * br
    cr, ci = c_re.astype(F32), c_im.astype(F32)
    eye = jnp.eye(G, dtype=F32)
    b_cat = jnp.concatenate(
        [jnp.einsum('gpc,gh->gchp', bb_re, eye).reshape(G * C, G * P),
         jnp.einsum('gpc,gh->gchp', bb_im, eye).reshape(G * C, G * P)], axis=1)
    c_cat = jnp.concatenate(
        [jnp.einsum('gcp,gh->gphc', cr, eye).reshape(G * P, G * C),
         -jnp.einsum('gcp,gh->gphc', ci, eye).reshape(G * P, G * C)], axis=0)
    mag_l = jnp.exp(L * lr * dt)
    a_step = jnp.stack([a_re.reshape(G * P), a_im.reshape(G * P)])
    a_chunk = jnp.stack([(mag_l * jnp.cos(L * li * dt)).reshape(G * P),
                         (mag_l * jnp.sin(L * li * dt)).reshape(G * P)])
    return b_cat.astype(BF16), c_cat.astype(BF16), a_step, a_chunk


def _s5_kernel(u0_ref, u1_ref, b_ref, c_ref, a_ref, al_ref, d_ref, wg_ref, mg_ref, o_ref,
               carry_ref, x_s, st_s, y_s):
    nb, tb, lanes = u0_ref.shape
    L = S5_CHUNK
    r = tb // L
    rows = nb * r
    ns = S5_GROUPS * S5_STATE

    @pl.when(pl.program_id(0) == 0)
    def _():
        carry_ref[...] = jnp.zeros_like(carry_ref)

    a_re, a_im = a_ref[0:1, :], a_ref[1:2, :]
    al_re, al_im = al_ref[0:1, :], al_ref[1:2, :]

    def advance(tau):
        sl = pl.ds(tau, r, stride=L)
        u = jnp.concatenate([u0_ref[:, sl, :], u1_ref[:, sl, :]], axis=-1).reshape(rows, 2 * lanes)
        bu = jnp.dot(u.astype(BF16), b_ref[...], preferred_element_type=F32)
        xr, xi = x_s[:, :ns], x_s[:, ns:]
        x_s[:, :ns] = a_re * xr - a_im * xi + bu[:, :ns]
        x_s[:, ns:] = a_re * xi + a_im * xr + bu[:, ns:]

    x_s[...] = jnp.zeros_like(x_s)

    def pass1(tau, c):
        advance(tau)
        return c

    lax.fori_loop(0, L, pass1, 0)

    for b in range(nb):
        def hop(c, carry, b=b):
            xr, xi = carry
            row = pl.ds(b * r + c, 1)
            st_s[row, :] = jnp.concatenate([xr, xi], axis=1)
            p = x_s[row, :]
            return (al_re * xr - al_im * xi + p[:, :ns], al_re * xi + al_im * xr + p[:, ns:])

        xr, xi = lax.fori_loop(0, r, hop, (carry_ref[b:b + 1, :ns], carry_ref[b:b + 1, ns:]))
        carry_ref[b:b + 1, :] = jnp.concatenate([xr, xi], axis=1)

    x_s[...] = st_s[...]

    def pass2(tau, c):
        advance(tau)
        y = jnp.dot(x_s[...].astype(BF16), c_ref[...], preferred_element_type=F32)
        sl = pl.ds(tau, r, stride=L)
        y_s[0, :, sl, :] = y[:, :lanes].reshape(nb, r, lanes)
        y_s[1, :, sl, :] = y[:, lanes:].reshape(nb, r, lanes)
        return c

    lax.fori_loop(0, L, pass2, 0)

    u = jnp.concatenate([u0_ref[...], u1_ref[...]], axis=-1).reshape(nb * tb, 2 * lanes)
    y = jnp.concatenate([y_s[0], y_s[1]], axis=-1).reshape(nb * tb, 2 * lanes)
    y = jax.nn.gelu(y + d_ref[...] * u)
    z = jnp.dot(y.astype(BF16), wg_ref[...], preferred_element_type=F32)
    out = _rms(z[:, :GROUP_W] * jax.nn.sigmoid(z[:, GROUP_W:]), mg_ref[...])
    o_ref[...] = out.reshape(nb, tb, GROUP_W)


def _s5(proj3, mats, d_skip, w_glu_bf16, merge_g, tb):
    bsz, t, _ = proj3.shape
    b_cat, c_cat, a_step, a_chunk = mats
    ns = S5_GROUPS * S5_STATE
    lanes = GROUP_W // 2
    rows = bsz * (tb // S5_CHUNK)
    half = lambda j: pl.BlockSpec((bsz, tb, lanes), lambda i, j=j: (0, i, 2 * COL_S5 + j))
    full = lambda a, b: pl.BlockSpec((a, b), lambda i: (0, 0))
    return pl.pallas_call(
        _s5_kernel,
        grid=(t // tb,),
        in_specs=[half(0), half(1), full(GROUP_W, 2 * ns), full(2 * ns, GROUP_W), full(2, ns),
                  full(2, ns), full(1, GROUP_W), full(GROUP_W, 2 * GROUP_W), full(1, GROUP_W)],
        out_specs=pl.BlockSpec((bsz, tb, GROUP_W), lambda i: (0, i, 0)),
        out_shape=jax.ShapeDtypeStruct((bsz, t, GROUP_W), F32),
        scratch_shapes=[pltpu.VMEM((bsz, 2 * ns), F32), pltpu.VMEM((rows, 2 * ns), F32),
                        pltpu.VMEM((rows, 2 * ns), F32), pltpu.VMEM((2, bsz, tb, lanes), F32)],
        compiler_params=_cparams("arbitrary"),
        name="s5",
    )(proj3, proj3, b_cat, c_cat, a_step, a_chunk, d_skip.reshape(1, -1), w_glu_bf16,
      merge_g.reshape(1, -1))


def _rwkv_kernel(*refs, has_vmix):
    if has_vmix:
        (r_ref, k_ref, v_ref, lo_ref, vf_ref, mu_ref, w0_ref, w2_ref, a0_ref, a2_ref, g2_ref,
         kk_ref, ka_ref, rk_ref, lnw_ref, lnb_ref, mg_ref, v0_ref, v1_ref, v2_ref,
         o_ref, ht_ref, prev_ref, lw_s, a_s, b_s, k_s, r_s, v_s, y_s) = refs
    else:
        (r_ref, k_ref, v_ref, lo_ref, mu_ref, w0_ref, w2_ref, a0_ref, a2_ref, g2_ref,
         kk_ref, ka_ref, rk_ref, lnw_ref, lnb_ref, mg_ref,
         o_ref, vf_out_ref, ht_ref, prev_ref, lw_s, a_s, b_s, k_s, r_s, v_s, y_s) = refs
    tb = r_ref.shape[0]
    ch = RWKV_CHUNK
    nh = GROUP_W // HEAD_W

    @pl.when(pl.program_id(1) == 0)
    def _():
        ht_ref[...] = jnp.zeros_like(ht_ref)
        prev_ref[...] = jnp.zeros_like(prev_ref)

    row0 = lax.broadcasted_iota(jnp.int32, (tb, 1), 0) == 0

    def mixed(ref, j):
        p = ref[...]
        lanes = slice(j * GROUP_W, (j + 1) * GROUP_W)
        sh = jnp.where(row0, prev_ref[0:1, lanes], pltpu.roll(p, 1, axis=0))
        prev_new = p[tb - 1:tb, :]
        return p + mu_ref[0:1, lanes] * (sh - p), prev_new

    r, pr_ = mixed(r_ref, 0)
    k, pk_ = mixed(k_ref, 1)
    v, pv_ = mixed(v_ref, 2)
    lo, pl_ = mixed(lo_ref, 3)
    for j, pn in enumerate((pr_, pk_, pv_, pl_)):
        prev_ref[0:1, j * GROUP_W:(j + 1) * GROUP_W] = pn

    bd = _block_diag_mask(GROUP_W, HEAD_W)
    bd_bf16 = bd.astype(BF16)

    w_raw = -_softplus(-(w0_ref[...] + _mm(jnp.tanh(lo), w2_ref[...]))) - 0.5
    lw_s[...] = -jnp.exp(w_raw)
    alr = jax.nn.sigmoid(a0_ref[...] + _mm(lo, a2_ref[...]))
    g = _mm(jax.nn.sigmoid(lo), g2_ref[...])
    kkr = k * kk_ref[...]
    kk = kkr / jnp.maximum(jnp.sqrt(_head_sum(kkr * kkr, bd_bf16)), 1e-12)
    k = k * (1.0 + (alr - 1.0) * ka_ref[...])
    if has_vmix:
        gate = jax.nn.sigmoid(v0_ref[...] + _mm(_mm(v, v1_ref[...]), v2_ref[...]))
        v = v + (vf_ref[...] - v) * gate
    else:
        vf_out_ref[...] = v
    a_s[...] = -kk
    b_s[...] = kk * alr
    k_s[...] = k
    r_s[...] = r
    v_s[...] = v

    n4 = nh * ch
    ri = lax.broadcasted_iota(jnp.int32, (n4, n4), 0)
    ci = lax.broadcasted_iota(jnp.int32, (n4, n4), 1)
    same_head = (ri // ch) == (ci // ch)
    strict = same_head & ((ri % ch) > (ci % ch))
    incl = same_head & ((ri % ch) >= (ci % ch))
    eye = (ri == ci).astype(F32)
    hm = ((lax.broadcasted_iota(jnp.int32, (n4, GROUP_W), 0) // ch)
          == (lax.broadcasted_iota(jnp.int32, (n4, GROUP_W), 1) // HEAD_W))
    tri = (lax.broadcasted_iota(jnp.int32, (ch, ch), 0)
           >= lax.broadcasted_iota(jnp.int32, (ch, ch), 1)).astype(BF16)

    def stack(x):
        return jnp.where(hm, jnp.concatenate([x] * nh, axis=0), 0.0)

    def chunk(c, carry):
        sl = pl.ds(pl.multiple_of(c * ch, ch), ch)
        lw = lw_s[sl, :]
        cl = _exact_lhs_mm(tri, lw)
        cl_end = cl[ch - 1:ch, :]
        e_in, e_ex = jnp.exp(cl), jnp.exp(cl - lw)
        e_neg, e_end = jnp.exp(-cl), jnp.exp(cl_end - cl)
        av, bv, kv, rv, vv = a_s[sl, :], b_s[sl, :], k_s[sl, :], r_s[sl, :], v_s[sl, :]
        at4, rt4 = stack(av * e_ex), stack(rv * e_in)
        bt4 = jnp.concatenate([bv * e_neg] * nh, axis=0)
        kt4 = jnp.concatenate([kv * e_neg] * nh, axis=0)
        v4 = stack(vv)
        ar = jnp.concatenate([at4, rt4], axis=0)
        sb = _dg3(ar, bt4, _NT)
        sk = _dg3(ar, kt4, _NT)
        l_ab = jnp.where(strict, sb[:n4], 0.0)
        l_ak = jnp.where(strict, sk[:n4], 0.0)
        l_rb = jnp.where(incl, sb[n4:], 0.0)
        l_rk = jnp.where(incl, sk[n4:], 0.0)
        tinv = eye + l_ab
        pw = l_ab
        span = 2
        while span < ch:
            pw = _dg3(pw, pw, _NN)
            tinv = tinv + _dg3(tinv, pw, _NN)
            span *= 2
        ht = ht_ref[...]
        ht_b = ht.astype(BF16)
        nt = lambda x, y: lax.dot_general(x.astype(BF16), y, (_NT, ((), ())), preferred_element_type=F32)
        wmat = nt(at4, ht_b) + _mm(l_ak, v4)
        u = _mm(tinv, wmat)
        y4 = nt(rt4, ht_b) + _mm(l_rb, u) + _mm(l_rk, v4)
        y = y4[0:ch]
        for h in range(1, nh):
            y = y + y4[h * ch:(h + 1) * ch]
        y_s[sl, :] = y
        tn = lambda x, y_: lax.dot_general(x.astype(BF16), y_.astype(BF16), (_TN, ((), ())),
                                           preferred_element_type=F32)
        ht_ref[...] = (ht * jnp.exp(cl_end) + tn(u, stack(bv * e_end)) + tn(v4, stack(kv * e_end)))
        return carry

    lax.fori_loop(0, tb // ch, chunk, 0)

    y = y_s[...]
    r, k, v = r_s[...], k_s[...], v_s[...]
    mean = _head_sum(y, bd_bf16) * (1.0 / HEAD_W)
    d = y - mean
    var = _head_sum(d * d, bd_bf16) * (1.0 / HEAD_W)
    y = d * lax.rsqrt(var + RWKV_GN_EPS) * lnw_ref[...] + lnb_ref[...]
    y = y + _head_sum(r * k * rk_ref[...], bd_bf16) * v
    o_ref[...] = _rms(y * g, mg_ref[...])


def _pad_rows(w, start, total=GROUP_W):
    out = jnp.zeros((total, w.shape[1]), F32).at[start:start + w.shape[0]].set(w)
    return out.astype(BF16)


def _rwkv(proj3, v_first, lyr, p, merge_g, tb):
    bsz, t, _ = proj3.shape
    has_vmix = v_first is not None
    row = lambda x: x.reshape(1, -1)
    w2 = _pad_rows(p["rwkv_w2"][lyr], 0)
    a2 = _pad_rows(p["rwkv_a2"][lyr], RWKV_W_LORA)
    g2 = _pad_rows(p["rwkv_g2"][lyr], RWKV_W_LORA + RWKV_A_LORA)
    args = [proj3, proj3, proj3, proj3]
    specs = [_col_spec(tb, COL_R), _col_spec(tb, COL_K), _col_spec(tb, COL_V), _col_spec(tb, COL_LORA)]
    if has_vmix:
        args.append(v_first)
        specs.append(pl.BlockSpec((None, tb, GROUP_W), lambda b, i: (b, i, 0)))
    args += [row(p["rwkv_mu"][lyr]), row(p["rwkv_w0"][lyr]), w2, row(p["rwkv_a0"][lyr]), a2, g2,
             row(p["rwkv_k_k"][lyr]), row(p["rwkv_k_a"][lyr]), row(p["rwkv_r_k"][lyr]),
             row(p["rwkv_ln_w"][lyr]), row(p["rwkv_ln_b"][lyr]), row(merge_g)]
    sq = _full_spec((GROUP_W, GROUP_W))
    specs += [_row_spec(4 * GROUP_W), _row_spec(), sq, _row_spec(), sq, sq,
              _row_spec(), _row_spec(), _row_spec(), _row_spec(), _row_spec(), _row_spec()]
    if has_vmix:
        v1 = jnp.zeros((GROUP_W, 128), F32).at[:, :RWKV_V_LORA].set(p["rwkv_v1"][lyr - 1]).astype(BF16)
        v2 = jnp.zeros((128, GROUP_W), F32).at[:RWKV_V_LORA].set(p["rwkv_v2"][lyr - 1]).astype(BF16)
        args += [row(p["rwkv_v0"][lyr - 1]), v1, v2]
        specs += [_row_spec(), _full_spec((GROUP_W, 128)), _full_spec((128, GROUP_W))]
    out_blk = pl.BlockSpec((None, tb, GROUP_W), lambda b, i: (b, i, 0))
    out_sds = jax.ShapeDtypeStruct((bsz, t, GROUP_W), F32)
    blk = pltpu.VMEM((tb, GROUP_W), F32)
    res = pl.pallas_call(
        functools.partial(_rwkv_kernel, has_vmix=has_vmix),
        grid=(bsz, t // tb),
        in_specs=specs,
        out_specs=out_blk if has_vmix else [out_blk, out_blk],
        out_shape=out_sds if has_vmix else [out_sds, out_sds],
        scratch_shapes=[pltpu.VMEM((GROUP_W, GROUP_W), F32), pltpu.VMEM((1, 4 * GROUP_W), F32),
                        blk, blk, blk, blk, blk, blk, blk],
        compiler_params=_cparams("parallel", "arbitrary"),
        name="rwkv7",
    )(*args)
    if has_vmix:
        return res, v_first
    return res[0], res[1]


def _bf16_bits(x):
    u = lax.bitcast_convert_type(x, jnp.uint32)
    r = u + jnp.uint32(0x7FFF) + ((u >> 16) & jnp.uint32(1))
    return r & jnp.uint32(0xFFFF0000)


def _merge_router_kernel(oa_ref, ob_ref, oc_ref, od_ref, h_ref, wo_ref, nw_ref, wr_ref, br_ref,
                         h1_ref, hn_ref, ids_ref, wts_ref, cnt_ref):
    mix = jnp.concatenate([oa_ref[...], ob_ref[...], oc_ref[...], od_ref[...]], axis=1)
    h1 = h_ref[...] + jnp.dot(mix.astype(BF16), wo_ref[...], preferred_element_type=F32)
    h1_ref[...] = h1
    hn = _rms(h1, nw_ref[...])
    half = hn.shape[1] // 2
    hn_ref[...] = _bf16_bits(hn[:, :half]) | (_bf16_bits(hn[:, half:]) >> 16)
    logits = _dg3(hn, wr_ref[...], _NN) + br_ref[...]
    lane = lax.broadcasted_iota(jnp.int32, logits.shape, 1)
    big = jnp.int32(ROUTER_LANES)
    is_c = (lane >= N_EXPERTS) & (lane < N_EXPERTS + MOE_GROUPS)
    cm = jnp.max(jnp.where(is_c, logits, NEG_BIG), axis=-1, keepdims=True)
    gsel = jnp.min(jnp.where(is_c & (logits == cm), lane, big), axis=-1, keepdims=True) - N_EXPERTS
    p_g = 1.0 / jnp.sum(jnp.where(is_c, jnp.exp(logits - cm), 0.0), axis=-1, keepdims=True)
    lo = gsel * MOE_PER_GROUP
    in_g = (lane >= lo) & (lane < lo + MOE_PER_GROUP)
    m1 = jnp.max(jnp.where(in_g, logits, NEG_BIG), axis=-1, keepdims=True)
    i1 = jnp.min(jnp.where(in_g & (logits == m1), lane, big), axis=-1, keepdims=True)
    in_g2 = in_g & (lane != i1)
    m2 = jnp.max(jnp.where(in_g2, logits, NEG_BIG), axis=-1, keepdims=True)
    i2 = jnp.min(jnp.where(in_g2 & (logits == m2), lane, big), axis=-1, keepdims=True)
    w1 = p_g / (1.0 + jnp.exp(m2 - m1))
    w2 = p_g - w1
    two = lax.broadcasted_iota(jnp.int32, ids_ref.shape, 1)
    ids_ref[...] = jnp.where(two == 0, i1, i2)
    wts_ref[...] = jnp.where(two == 0, w1, w2)

    @pl.when(pl.program_id(0) == 0)
    def _():
        cnt_ref[...] = jnp.zeros_like(cnt_ref)

    cnt_ref[...] += jnp.sum(((lane == i1) | (lane == i2)).astype(F32), axis=0, keepdims=True)


def _merge_router(outs, h2d, w_out_bf16, norm_w, w_router, b_router, tm):
    n, d = h2d.shape
    grp = pl.BlockSpec((tm, GROUP_W), lambda i: (i, 0))
    full = lambda a, b: pl.BlockSpec((a, b), lambda i: (0, 0))
    rowblk = lambda w: pl.BlockSpec((tm, w), lambda i: (i, 0))
    return pl.pallas_call(
        _merge_router_kernel,
        grid=(n // tm,),
        in_specs=[grp, grp, grp, grp, rowblk(d), full(d, d), full(1, d), full(d, ROUTER_LANES),
                  full(1, ROUTER_LANES)],
        out_specs=[rowblk(d), rowblk(d // 2), rowblk(MOE_TOPK), rowblk(MOE_TOPK), full(1, ROUTER_LANES)],
        out_shape=[jax.ShapeDtypeStruct((n, d), F32), jax.ShapeDtypeStruct((n, d // 2), jnp.uint32),
                   jax.ShapeDtypeStruct((n, MOE_TOPK), jnp.int32),
                   jax.ShapeDtypeStruct((n, MOE_TOPK), F32),
                   jax.ShapeDtypeStruct((1, ROUTER_LANES), F32)],
        compiler_params=_cparams("arbitrary"),
        name="merge_router",
    )(*[o.reshape(n, GROUP_W) for o in outs], h2d, w_out_bf16, norm_w.reshape(1, d), w_router, b_router)


def _row_copy(src_hbm, src_row, dst_ref, dst_row, sem):
    return pltpu.make_async_copy(src_hbm.at[pl.ds(src_row, 1), :], dst_ref.at[pl.ds(dst_row, 1), :], sem)


def _position_kernel(ids_ref, base_ref, pos_ref, run_ref):
    tm = ids_ref.shape[0]

    @pl.when(pl.program_id(0) == 0)
    def _():
        run_ref[...] = jnp.zeros_like(run_ref)

    i1, i2 = ids_ref[:, 0:1], ids_ref[:, 1:2]
    lane = lax.broadcasted_iota(jnp.int32, (tm, ROUTER_LANES), 1)
    hit = ((lane == i1) | (lane == i2)).astype(BF16)
    earlier = (lax.broadcasted_iota(jnp.int32, (tm, tm), 0)
               > lax.broadcasted_iota(jnp.int32, (tm, tm), 1)).astype(BF16)
    rank = jnp.dot(earlier, hit, preferred_element_type=F32) + run_ref[...]
    where_to = rank + base_ref[...]
    p1 = jnp.sum(jnp.where(lane == i1, where_to, 0.0), axis=-1, keepdims=True)
    p2 = jnp.sum(jnp.where(lane == i2, where_to, 0.0), axis=-1, keepdims=True)
    two = lax.broadcasted_iota(jnp.int32, pos_ref.shape, 1)
    pos_ref[...] = jnp.where(two == 0, p1, p2).astype(jnp.int32)
    run_ref[...] += jnp.sum(hit.astype(F32), axis=0, keepdims=True)


def _positions(ids, group_start, tm):
    n = ids.shape[0]
    return pl.pallas_call(
        _position_kernel,
        grid=(n // tm,),
        in_specs=[pl.BlockSpec((tm, MOE_TOPK), lambda i: (i, 0)),
                  pl.BlockSpec((1, ROUTER_LANES), lambda i: (0, 0))],
        out_specs=pl.BlockSpec((tm, MOE_TOPK), lambda i: (i, 0)),
        out_shape=jax.ShapeDtypeStruct((n, MOE_TOPK), jnp.int32),
        scratch_shapes=[pltpu.VMEM((1, ROUTER_LANES), F32)],
        compiler_params=_cparams("arbitrary"),
        name="moe_positions",
    )(ids, group_start)


def _dispatch_kernel(p0_ref, p1_ref, x_ref, xs_in_hbm, xs_hbm, sem):
    del xs_in_hbm
    n = x_ref.shape[0]

    def copies(i):
        return (pltpu.make_async_copy(x_ref.at[pl.ds(i, 1), :], xs_hbm.at[pl.ds(p0_ref[i], 1), :], sem),
                pltpu.make_async_copy(x_ref.at[pl.ds(i, 1), :], xs_hbm.at[pl.ds(p1_ref[i], 1), :], sem))

    def issue(i, c):
        for cp in copies(i):
            cp.start()
        return c

    def drain(i, c):
        for cp in copies(i):
            cp.wait()
        return c

    lax.fori_loop(0, n, issue, 0, unroll=8)
    lax.fori_loop(0, n, drain, 0, unroll=8)


def _dispatch(x, pos0, pos1, n_rows, chunk):
    n, w = x.shape
    smem = lambda: pl.BlockSpec((chunk,), lambda s: (s,), memory_space=pltpu.SMEM)
    return pl.pallas_call(
        _dispatch_kernel,
        grid=(n // chunk,),
        in_specs=[smem(), smem(), pl.BlockSpec((chunk, w), lambda s: (s, 0)),
                  pl.BlockSpec(memory_space=pl.ANY)],
        out_specs=pl.BlockSpec(memory_space=pl.ANY),
        out_shape=jax.ShapeDtypeStruct((n_rows, w), x.dtype),
        input_output_aliases={3: 0},
        scratch_shapes=[pltpu.SemaphoreType.DMA(())],
        compiler_params=_cparams("arbitrary"),
        name="moe_dispatch",
    )(pos0, pos1, x, jnp.zeros((n_rows, w), x.dtype))


def _expert_kernel(te_ref, nt_ref, xs_ref, wg_ref, wu_ref, wd_ref, o_ref):
    i = pl.program_id(0)

    @pl.when(i < nt_ref[0])
    def _():
        p = xs_ref[...]
        xa = lax.bitcast_convert_type(p & jnp.uint32(0xFFFF0000), F32).astype(BF16)
        xb = lax.bitcast_convert_type(p << 16, F32).astype(BF16)
        x = jnp.concatenate([xa, xb], axis=1)
        he = (_silu(jnp.dot(x, wg_ref[...], preferred_element_type=F32))
              * jnp.dot(x, wu_ref[...], preferred_element_type=F32))
        o_ref[...] = jnp.dot(he.astype(BF16), wd_ref[...], preferred_element_type=F32)

    @pl.when(i >= nt_ref[0])
    def _():
        o_ref[...] = jnp.zeros_like(o_ref)


def _experts(xs, tile_expert, n_tiles_used, wg, wu, wd, tm):
    p, half = xs.shape
    d = 2 * half
    grid_spec = pltpu.PrefetchScalarGridSpec(
        num_scalar_prefetch=2,
        grid=(p // tm,),
        in_specs=[pl.BlockSpec((tm, half), lambda i, te, nt: (i, 0)),
                  pl.BlockSpec((None, d, D_EXPERT), lambda i, te, nt: (te[i], 0, 0)),
                  pl.BlockSpec((None, d, D_EXPERT), lambda i, te, nt: (te[i], 0, 0)),
                  pl.BlockSpec((None, D_EXPERT, d), lambda i, te, nt: (te[i], 0, 0))],
        out_specs=pl.BlockSpec((tm, d), lambda i, te, nt: (i, 0)),
    )
    return pl.pallas_call(
        _expert_kernel,
        grid_spec=grid_spec,
        out_shape=jax.ShapeDtypeStruct((p, d), F32),
        compiler_params=_cparams("arbitrary"),
        name="moe_experts",
    )(tile_expert, n_tiles_used, xs, wg, wu, wd)


def _combine_kernel(i0_ref, i1_ref, h1_ref, w_ref, fw_ref, ys_hbm, o_ref, buf0, buf1, sem, *, final_norm):
    n = o_ref.shape[0]

    def issue(i, c):
        _row_copy(ys_hbm, i0_ref[i], buf0, i, sem).start()
        _row_copy(ys_hbm, i1_ref[i], buf1, i, sem).start()
        return c

    def drain(i, c):
        _row_copy(ys_hbm, 0, buf0, i, sem).wait()
        _row_copy(ys_hbm, 0, buf1, i, sem).wait()
        return c

    lax.fori_loop(0, n, issue, 0, unroll=8)
    lax.fori_loop(0, n, drain, 0, unroll=8)
    out = h1_ref[...] + w_ref[:, 0:1] * buf0[...] + w_ref[:, 1:2] * buf1[...]
    if final_norm:
        out = _rms(out, fw_ref[...])
    o_ref[...] = out


def _combine(ys, pos0, pos1, wts, h1, final_w, final_norm, chunk):
    n, d = h1.shape
    smem = lambda: pl.BlockSpec((chunk,), lambda s: (s,), memory_space=pltpu.SMEM)
    return pl.pallas_call(
        functools.partial(_combine_kernel, final_norm=final_norm),
        grid=(n // chunk,),
        in_specs=[smem(), smem(), pl.BlockSpec((chunk, d), lambda s: (s, 0)),
                  pl.BlockSpec((chunk, MOE_TOPK), lambda s: (s, 0)),
                  pl.BlockSpec((1, d), lambda s: (0, 0)), pl.BlockSpec(memory_space=pl.ANY)],
        out_specs=pl.BlockSpec((chunk, d), lambda s: (s, 0)),
        out_shape=jax.ShapeDtypeStruct((n, d), F32),
        scratch_shapes=[pltpu.VMEM((chunk, d), F32), pltpu.VMEM((chunk, d), F32),
                        pltpu.SemaphoreType.DMA(())],
        compiler_params=_cparams("arbitrary"),
        name="moe_combine",
    )(pos0, pos1, h1, wts, final_w.reshape(1, d), ys)


def _moe(hn_packed, ids, wts, counts, h1, wg, wu, wd, final_w, final_norm, tm):
    n = h1.shape[0]
    n_tiles = (n * MOE_TOPK) // tm + N_EXPERTS
    cnt = counts[0, :N_EXPERTS].astype(jnp.int32)
    padded = ((cnt + tm - 1) // tm) * tm
    ends = jnp.cumsum(padded)
    tile_start = jnp.arange(n_tiles, dtype=jnp.int32) * tm
    tile_expert = jnp.minimum(jnp.sum(ends[None, :] <= tile_start[:, None], axis=1),
                              N_EXPERTS - 1).astype(jnp.int32)
    n_tiles_used = (ends[-1] // tm).astype(jnp.int32).reshape(1)
    group_start = jnp.zeros((1, ROUTER_LANES), F32).at[0, :N_EXPERTS].set((ends - padded).astype(F32))
    pos = _positions(ids, group_start, _pick_div(n, 512))
    pos0, pos1 = pos[:, 0], pos[:, 1]
    xs = _dispatch(hn_packed, pos0, pos1, n_tiles * tm, _pick_div(n, 1024))
    ys = _experts(xs, tile_expert, n_tiles_used, wg, wu, wd, tm)
    return _combine(ys, pos0, pos1, wts, h1, final_w, final_norm, _pick_div(n, 1024))


def _pick_div(n, pref):
    while n % pref:
        pref //= 2
    return pref


def _pick(n, pref):
    return pref if n % pref == 0 else n


def kernel(x, norm1_w, w_in, hgrn_lb_logits, hgrn_norm_w, s5_lambda_re, s5_lambda_im, s5_log_dt, s5_b_re, s5_b_im, s5_c_re, s5_c_im, s5_d, s5_w_glu, rwkv_mu, rwkv_w0, rwkv_w2, rwkv_a0, rwkv_a2, rwkv_g2, rwkv_k_k, rwkv_k_a, rwkv_r_k, rwkv_v0, rwkv_v1, rwkv_v2, rwkv_ln_w, rwkv_ln_b, lru_conv_w, lru_conv_b, lru_wa, lru_ba, lru_wx, lru_bx, lru_lambda, merge_gain, w_out, norm2_w, moe_coarse_w, moe_coarse_b, moe_fine_w, moe_fine_b, moe_w_gate, moe_w_up, moe_w_down, final_norm_w):
    bsz, t, d = x.shape
    n = bsz * t
    depth = w_in.shape[0]
    rw = dict(rwkv_mu=rwkv_mu, rwkv_w0=rwkv_w0, rwkv_w2=rwkv_w2, rwkv_a0=rwkv_a0, rwkv_a2=rwkv_a2,
              rwkv_g2=rwkv_g2, rwkv_k_k=rwkv_k_k, rwkv_k_a=rwkv_k_a, rwkv_r_k=rwkv_r_k,
              rwkv_v0=rwkv_v0, rwkv_v1=rwkv_v1, rwkv_v2=rwkv_v2, rwkv_ln_w=rwkv_ln_w,
              rwkv_ln_b=rwkv_ln_b)
    lb_all = jnp.cumsum(jax.nn.softmax(hgrn_lb_logits.astype(F32), axis=0), axis=0)
    lb_all = lb_all - lb_all[:1]

    tm_proj = _pick(n, 512)
    tm_moe = 256
    tb_hgrn = _pick(t, 256)
    tb_lru = _pick(t, 256)
    tb_rwkv = _pick(t, 256)
    tb_s5 = _pick(t, 512)

    h = x.reshape(n, d)
    v_first = None
    for l in range(depth):
        mg = merge_gain[l].reshape(4, GROUP_W)
        proj = _in_proj(h, norm1_w[l], w_in[l].astype(BF16), tm_proj)
        proj3 = proj.reshape(bsz, t, -1)
        o_a = _hgrn(proj3, lb_all[l], hgrn_norm_w[l], mg[0], tb_hgrn)
        mats = _s5_matrices(s5_lambda_re[l], s5_lambda_im[l], s5_log_dt[l], s5_b_re[l], s5_b_im[l],
                            s5_c_re[l], s5_c_im[l])
        o_b = _s5(proj3, mats, s5_d[l], s5_w_glu[l].astype(BF16), mg[1], tb_s5)
        o_c, v_first = _rwkv(proj3, v_first, l, rw, mg[2], tb_rwkv)
        o_d = _lru(proj3, lru_conv_w[l], lru_conv_b[l], _block_diag_weight(lru_wa[l]), lru_ba[l],
                   _block_diag_weight(lru_wx[l]), lru_bx[l], lru_lambda[l], mg[3], tb_lru)
        w_router = jnp.concatenate(
            [moe_fine_w[l].transpose(1, 0, 2).reshape(d, N_EXPERTS), moe_coarse_w[l],
             jnp.zeros((d, ROUTER_LANES - N_EXPERTS - MOE_GROUPS), F32)], axis=1)
        b_router = jnp.concatenate(
            [moe_fine_b[l].reshape(N_EXPERTS), moe_coarse_b[l],
             jnp.zeros((ROUTER_LANES - N_EXPERTS - MOE_GROUPS,), F32)]).reshape(1, ROUTER_LANES)
        h1, hn, ids, wts, counts = _merge_router((o_a, o_b, o_c, o_d), h, w_out[l].astype(BF16),
                                                 norm2_w[l], w_router, b_router, tm_proj)
        h = _moe(hn, ids, wts, counts, h1, moe_w_gate[l].astype(BF16), moe_w_up[l].astype(BF16),
                 moe_w_down[l].astype(BF16), final_norm_w, l == depth - 1, tm_moe)
    return h.reshape(bsz, t, d)
```

```python
import functools
import math

import jax
import jax.numpy as jnp
from jax import lax
from jax.experimental import pallas as pl
from jax.experimental.pallas import tpu as pltpu

F32 = jnp.float32
BF16 = jnp.bfloat16

D_MODEL = 1024
GROUP_W = 256
RMS_EPS = 1e-6
HEAD_W = 64
HGRN_CHUNK = 16
S5_CH = 16
S5_GROUPS = GROUP_W // S5_CH
S5_STATE = 64
S5_CHUNK = 16
RWKV_CHUNK = 64
RWKV_GN_EPS = 64e-5
RWKV_W_LORA = 64
RWKV_A_LORA = 64
RWKV_G_LORA = 128
RWKV_V_LORA = 32
LRU_CONV = 4
LRU_C = 8.0
MOE_GROUPS = 4
MOE_PER_GROUP = 8
N_EXPERTS = MOE_GROUPS * MOE_PER_GROUP
MOE_TOPK = 2
D_EXPERT = 512
ROUTER_LANES = 128
NEG_BIG = -1e30
VMEM_LIMIT = 56 * 1024 * 1024

COL_HQ, COL_HF, COL_HI, COL_HG, COL_S5, COL_R, COL_K, COL_V, COL_LORA, COL_LG, COL_LX = range(11)


def _cparams(*sem):
    return pltpu.CompilerParams(dimension_semantics=sem, vmem_limit_bytes=VMEM_LIMIT)


def _mm(a, b):
    return jnp.dot(a.astype(BF16), b.astype(BF16), preferred_element_type=F32)


def _split2(x):
    hi = x.astype(BF16)
    lo = (x - hi.astype(F32)).astype(BF16)
    return hi, lo


def _dg3(a, b, dims):
    ah, al = _split2(a)
    bh, bl = _split2(b)
    d = lambda x, y: lax.dot_general(x, y, (dims, ((), ())), preferred_element_type=F32)
    return d(ah, bh) + d(ah, bl) + d(al, bh)


_NN = ((1,), (0,))
_NT = ((1,), (1,))
_TN = ((0,), (0,))


def _exact_lhs_mm(m_bf16, x):
    h1 = x.astype(BF16)
    r1 = x - h1.astype(F32)
    h2 = r1.astype(BF16)
    h3 = (r1 - h2.astype(F32)).astype(BF16)
    d = lambda y: jnp.dot(m_bf16, y, preferred_element_type=F32)
    return d(h1) + d(h2) + d(h3)


def _head_sum(x, bd_bf16):
    h1 = x.astype(BF16)
    r1 = x - h1.astype(F32)
    h2 = r1.astype(BF16)
    h3 = (r1 - h2.astype(F32)).astype(BF16)
    d = lambda y: jnp.dot(y, bd_bf16, preferred_element_type=F32)
    return d(h1) + d(h2) + d(h3)


def _block_diag_mask(n, blk):
    r = lax.broadcasted_iota(jnp.int32, (n, n), 0) // blk
    c = lax.broadcasted_iota(jnp.int32, (n, n), 1) // blk
    return r == c


def _rms(x, w):
    return x * lax.rsqrt(jnp.mean(x * x, axis=-1, keepdims=True) + RMS_EPS) * w


def _silu(x):
    return x * jax.nn.sigmoid(x)


def _softplus(x):
    return jnp.maximum(x, 0.0) + jnp.log(1.0 + jnp.exp(-jnp.abs(x)))


def _in_proj_kernel(x_ref, nw_ref, w_ref, o_ref):
    y = _rms(x_ref[...], nw_ref[...])
    o_ref[...] = jnp.dot(y.astype(BF16), w_ref[...], preferred_element_type=F32)


def _in_proj(h2d, norm_w, w_bf16, tm):
    n, d = h2d.shape
    d_in = w_bf16.shape[1]
    return pl.pallas_call(
        _in_proj_kernel,
        grid=(n // tm,),
        in_specs=[pl.BlockSpec((tm, d), lambda i: (i, 0)),
                  pl.BlockSpec((1, d), lambda i: (0, 0)),
                  pl.BlockSpec((d, d_in), lambda i: (0, 0))],
        out_specs=pl.BlockSpec((tm, d_in), lambda i: (i, 0)),
        out_shape=jax.ShapeDtypeStruct((n, d_in), F32),
        compiler_params=_cparams("parallel"),
        name="in_proj",
    )(h2d, norm_w.reshape(1, d), w_bf16)


def _col_spec(tb, col):
    return pl.BlockSpec((None, tb, GROUP_W), lambda b, t: (b, t, col))


def _row_spec(width=GROUP_W):
    return pl.BlockSpec((1, width), lambda b, t: (0, 0))


def _full_spec(shape):
    return pl.BlockSpec(shape, lambda b, t: (0,) * len(shape))


def _hgrn_kernel(q_ref, f_ref, i_ref, g_ref, lb_ref, nw_ref, mg_ref, o_ref,
                 st_ref, q_s, k_s, v_s, lf_s, o_s):
    ch = HGRN_CHUNK
    tb = q_ref.shape[0]

    @pl.when(pl.program_id(1) == 0)
    def _():
        st_ref[...] = jnp.zeros_like(st_ref)

    lb = lb_ref[...]
    fx = f_ref[...]
    x1 = jnp.log(lb)
    x2 = jnp.log(1.0 - lb) - _softplus(-fx)
    m = jnp.maximum(x1, x2)
    lf_s[...] = m + jnp.log(jnp.exp(x1 - m) + jnp.exp(x2 - m))
    q_s[...] = _silu(q_ref[...])
    k_s[...] = (1.0 - lb) * jax.nn.sigmoid(-fx)
    v_s[...] = _silu(i_ref[...])

    bd = _block_diag_mask(GROUP_W, HEAD_W)
    bd_bf16 = bd.astype(BF16)
    tri = (lax.broadcasted_iota(jnp.int32, (ch, ch), 0)
           >= lax.broadcasted_iota(jnp.int32, (ch, ch), 1))
    tri_bf16 = tri.astype(BF16)
    tri3 = (lax.broadcasted_iota(jnp.int32, (ch, ch, 1), 0)
            >= lax.broadcasted_iota(jnp.int32, (ch, ch, 1), 1))

    st = st_ref[...]
    for c in range(tb // ch):
        sl = slice(c * ch, (c + 1) * ch)
        qc, kc, vc = q_s[sl, :], k_s[sl, :], v_s[sl, :]
        b = _exact_lhs_mm(tri_bf16, lf_s[sl, :])
        rel = b[:, None, :] - b[None, :, :]
        dec = jnp.exp(jnp.where(tri3, rel, NEG_BIG))
        p = (qc[:, None, :] * kc[None, :, :]) * dec
        sc = jnp.dot(p.reshape(ch * ch, GROUP_W).astype(BF16), bd_bf16,
                     preferred_element_type=F32).reshape(ch, ch, GROUP_W)
        o_intra = jnp.sum(sc * vc[None, :, :], axis=1)
        o_inter = lax.dot_general((qc * jnp.exp(b)).astype(BF16), st.astype(BF16),
                                  (_NT, ((), ())), preferred_element_type=F32)
        b_end = b[ch - 1:ch, :]
        kh = kc * jnp.exp(b_end - b)
        upd = lax.dot_general(vc.astype(BF16), kh.astype(BF16), (_TN, ((), ())),
                              preferred_element_type=F32)
        st = st * jnp.exp(b_end) + jnp.where(bd, upd, 0.0)
        o_s[sl, :] = o_intra + o_inter
    st_ref[...] = st

    o = o_s[...]
    ms = _head_sum(o * o, bd_bf16) * (1.0 / HEAD_W)
    o = o * lax.rsqrt(ms + RMS_EPS) * nw_ref[...] * _silu(g_ref[...])
    o_ref[...] = _rms(o, mg_ref[...])


def _hgrn(proj3, lb, norm_w, merge_g, tb):
    bsz, t, _ = proj3.shape
    blk = pltpu.VMEM((tb, GROUP_W), F32)
    return pl.pallas_call(
        _hgrn_kernel,
        grid=(bsz, t // tb),
        in_specs=[_col_spec(tb, COL_HQ), _col_spec(tb, COL_HF), _col_spec(tb, COL_HI),
                  _col_spec(tb, COL_HG), _row_spec(), _row_spec(), _row_spec()],
        out_specs=pl.BlockSpec((None, tb, GROUP_W), lambda b, i: (b, i, 0)),
        out_shape=jax.ShapeDtypeStruct((bsz, t, GROUP_W), F32),
        scratch_shapes=[pltpu.VMEM((GROUP_W, GROUP_W), F32), blk, blk, blk, blk, blk],
        compiler_params=_cparams("parallel", "arbitrary"),
        name="hgrn2",
    )(proj3, proj3, proj3, proj3, lb.reshape(1, -1), norm_w.reshape(1, -1), merge_g.reshape(1, -1))


def _lru_kernel(xg_ref, xr_ref, cw_ref, cb_ref, wa_ref, ba_ref, wx_ref, bx_ref, lam_ref, mg_ref,
                o_ref, buf_ref, h_ref):
    tb = xr_ref.shape[0]

    @pl.when(pl.program_id(1) == 0)
    def _():
        buf_ref[0:8, :] = jnp.zeros((8, GROUP_W), F32)
        h_ref[...] = jnp.zeros_like(h_ref)

    xr = xr_ref[...]
    buf_ref[8:8 + tb, :] = xr
    xc = cb_ref[...] + jnp.zeros_like(xr)
    for j in range(LRU_CONV):
        xc = xc + cw_ref[j:j + 1, :] * buf_ref[pl.ds(8 - (LRU_CONV - 1) + j, tb), :]
    buf_ref[0:8, :] = xr[tb - 8:tb, :]

    r = jax.nn.sigmoid(jnp.dot(xc.astype(BF16), wa_ref[...], preferred_element_type=F32) + ba_ref[...])
    gi = jax.nn.sigmoid(jnp.dot(xc.astype(BF16), wx_ref[...], preferred_element_type=F32) + bx_ref[...])
    log_a = -LRU_C * r * _softplus(-lam_ref[...])
    a = jnp.exp(log_a)
    x = jnp.sqrt(1.0 - jnp.exp(2.0 * log_a)) * (gi * xc)

    rows = lax.broadcasted_iota(jnp.int32, (tb, 1), 0)
    k = 1
    while k < tb:
        keep = rows >= k
        x = x + jnp.where(keep, a * pltpu.roll(x, k, axis=0), 0.0)
        a = jnp.where(keep, a * pltpu.roll(a, k, axis=0), a)
        k *= 2
    h = x + a * h_ref[...]
    h_ref[...] = h[tb - 1:tb, :]
    o_ref[...] = _rms(jax.nn.gelu(xg_ref[...]) * h, mg_ref[...])


def _lru(proj3, conv_w, conv_b, wa_bd, ba, wx_bd, bx, lam, merge_g, tb):
    bsz, t, _ = proj3.shape
    return pl.pallas_call(
        _lru_kernel,
        grid=(bsz, t // tb),
        in_specs=[_col_spec(tb, COL_LG), _col_spec(tb, COL_LX), _full_spec((LRU_CONV, GROUP_W)),
                  _row_spec(), _full_spec((GROUP_W, GROUP_W)), _row_spec(),
                  _full_spec((GROUP_W, GROUP_W)), _row_spec(), _row_spec(), _row_spec()],
        out_specs=pl.BlockSpec((None, tb, GROUP_W), lambda b, i: (b, i, 0)),
        out_shape=jax.ShapeDtypeStruct((bsz, t, GROUP_W), F32),
        scratch_shapes=[pltpu.VMEM((tb + 8, GROUP_W), F32), pltpu.VMEM((1, GROUP_W), F32)],
        compiler_params=_cparams("parallel", "arbitrary"),
        name="rglru",
    )(proj3, proj3, conv_w, conv_b.reshape(1, -1), wa_bd, ba.reshape(1, -1), wx_bd,
      bx.reshape(1, -1), lam.reshape(1, -1), merge_g.reshape(1, -1))


def _block_diag_weight(w):
    h, n, _ = w.shape
    eye = jnp.eye(h, dtype=w.dtype)
    return jnp.einsum('hij,hg->higj', w, eye).reshape(h * n, h * n).astype(BF16)


def _s5_matrices(lam_re, lam_im, log_dt, b_re, b_im, c_re, c_im):
    L, G, P, C = S5_CHUNK, S5_GROUPS, S5_STATE, S5_CH
    lr, li = lam_re.astype(F32), lam_im.astype(F32)
    dt = jnp.exp(log_dt.astype(F32))[:, None]
    mag = jnp.exp(lr * dt)
    a_re, a_im = mag * jnp.cos(li * dt), mag * jnp.sin(li * dt)
    den = lr * lr + li * li
    kap_re = ((a_re - 1.0) * lr + a_im * li) / den
    kap_im = (a_im * lr - (a_re - 1.0) * li) / den
    br, bi = b_re.astype(F32), b_im.astype(F32)
    bb_re = kap_re[..., None] * br - kap_im[..., None] * bi
    bb_im = kap_re[..., None] * bi + kap_im[..., None] * br
    cr, ci = c_re.astype(F32), c_im.astype(F32)
    eye = jnp.eye(G, dtype=F32)
    b_cat = jnp.concatenate(
        [jnp.einsum('gpc,gh->gchp', bb_re, eye).reshape(G * C, G * P),
         jnp.einsum('gpc,gh->gchp', bb_im, eye).reshape(G * C, G * P)], axis=1)
    c_cat = jnp.concatenate(
        [jnp.einsum('gcp,gh->gphc', cr, eye).reshape(G * P, G * C),
         -jnp.einsum('gcp,gh->gphc', ci, eye).reshape(G * P, G * C)], axis=0)
    mag_l = jnp.exp(L * lr * dt)
    a_step = jnp.stack([a_re.reshape(G * P), a_im.reshape(G * P)])
    a_chunk = jnp.stack([(mag_l * jnp.cos(L * li * dt)).reshape(G * P),
                         (mag_l * jnp.sin(L * li * dt)).reshape(G * P)])
    return b_cat.astype(BF16), c_cat.astype(BF16), a_step, a_chunk


def _s5_kernel(u0_ref, u1_ref, b_ref, c_ref, a_ref, al_ref, d_ref, wg_ref, mg_ref, o_ref,
               carry_ref, x_s, st_s, y_s):
    nb, tb, lanes = u0_ref.shape
    L = S5_CHUNK
    r = tb // L
    rows = nb * r
    ns = S5_GROUPS * S5_STATE

    @pl.when(pl.program_id(0) == 0)
    def _():
        carry_ref[...] = jnp.zeros_like(carry_ref)

    a_re, a_im = a_ref[0:1, :], a_ref[1:2, :]
    al_re, al_im = al_ref[0:1, :], al_ref[1:2, :]

    def advance(tau):
        sl = pl.ds(tau, r, stride=L)
        u = jnp.concatenate([u0_ref[:, sl, :], u1_ref[:, sl, :]], axis=-1).reshape(rows, 2 * lanes)
        bu = jnp.dot(u.astype(BF16), b_ref[...], preferred_element_type=F32)
        xr, xi = x_s[:, :ns], x_s[:, ns:]
        x_s[:, :ns] = a_re * xr - a_im * xi + bu[:, :ns]
        x_s[:, ns:] = a_re * xi + a_im * xr + bu[:, ns:]

    x_s[...] = jnp.zeros_like(x_s)

    def pass1(tau, c):
        advance(tau)
        return c

    lax.fori_loop(0, L, pass1, 0)

    for b in range(nb):
        def hop(c, carry, b=b):
            xr, xi = carry
            row = pl.ds(b * r + c, 1)
            st_s[row, :] = jnp.concatenate([xr, xi], axis=1)
            p = x_s[row, :]
            return (al_re * xr - al_im * xi + p[:, :ns], al_re * xi + al_im * xr + p[:, ns:])

        xr, xi = lax.fori_loop(0, r, hop, (carry_ref[b:b + 1, :ns], carry_ref[b:b + 1, ns:]))
        carry_ref[b:b + 1, :] = jnp.concatenate([xr, xi], axis=1)

    x_s[...] = st_s[...]

    def pass2(tau, c):
        advance(tau)
        y = jnp.dot(x_s[...].astype(BF16), c_ref[...], preferred_element_type=F32)
        sl = pl.ds(tau, r, stride=L)
        y_s[0, :, sl, :] = y[:, :lanes].reshape(nb, r, lanes)
        y_s[1, :, sl, :] = y[:, lanes:].reshape(nb, r, lanes)
        return c

    lax.fori_loop(0, L, pass2, 0)

    u = jnp.concatenate([u0_ref[...], u1_ref[...]], axis=-1).reshape(nb * tb, 2 * lanes)
    y = jnp.concatenate([y_s[0], y_s[1]], axis=-1).reshape(nb * tb, 2 * lanes)
    y = jax.nn.gelu(y + d_ref[...] * u)
    z = jnp.dot(y.astype(BF16), wg_ref[...], preferred_element_type=F32)
    out = _rms(z[:, :GROUP_W] * jax.nn.sigmoid(z[:, GROUP_W:]), mg_ref[...])
    o_ref[...] = out.reshape(nb, tb, GROUP_W)


def _s5(proj3, mats, d_skip, w_glu_bf16, merge_g, tb):
    bsz, t, _ = proj3.shape
    b_cat, c_cat, a_step, a_chunk = mats
    ns = S5_GROUPS * S5_STATE
    lanes = GROUP_W // 2
    rows = bsz * (tb // S5_CHUNK)
    half = lambda j: pl.BlockSpec((bsz, tb, lanes), lambda i, j=j: (0, i, 2 * COL_S5 + j))
    full = lambda a, b: pl.BlockSpec((a, b), lambda i: (0, 0))
    return pl.pallas_call(
        _s5_kernel,
        grid=(t // tb,),
        in_specs=[half(0), half(1), full(GROUP_W, 2 * ns), full(2 * ns, GROUP_W), full(2, ns),
                  full(2, ns), full(1, GROUP_W), full(GROUP_W, 2 * GROUP_W), full(1, GROUP_W)],
        out_specs=pl.BlockSpec((bsz, tb, GROUP_W), lambda i: (0, i, 0)),
        out_shape=jax.ShapeDtypeStruct((bsz, t, GROUP_W), F32),
        scratch_shapes=[pltpu.VMEM((bsz, 2 * ns), F32), pltpu.VMEM((rows, 2 * ns), F32),
                        pltpu.VMEM((rows, 2 * ns), F32), pltpu.VMEM((2, bsz, tb, lanes), F32)],
        compiler_params=_cparams("arbitrary"),
        name="s5",
    )(proj3, proj3, b_cat, c_cat, a_step, a_chunk, d_skip.reshape(1, -1), w_glu_bf16,
      merge_g.reshape(1, -1))


def _rwkv_kernel(*refs, has_vmix):
    if has_vmix:
        (r_ref, k_ref, v_ref, lo_ref, vf_ref, mu_ref, w0_ref, w2_ref, a0_ref, a2_ref, g2_ref,
         kk_ref, ka_ref, rk_ref, lnw_ref, lnb_ref, mg_ref, v0_ref, v1_ref, v2_ref,
         o_ref, h_ref, prev_ref, lw_s, a_s, b_s, k_s, r_s, v_s, y_s, m_s, n_s, p_s, z_s) = refs
    else:
        (r_ref, k_ref, v_ref, lo_ref, mu_ref, w0_ref, w2_ref, a0_ref, a2_ref, g2_ref,
         kk_ref, ka_ref, rk_ref, lnw_ref, lnb_ref, mg_ref,
         o_ref, vf_out_ref, h_ref, prev_ref, lw_s, a_s, b_s, k_s, r_s, v_s, y_s,
         m_s, n_s, p_s, z_s) = refs
    tb = r_ref.shape[0]
    ch = RWKV_CHUNK
    nh = GROUP_W // HEAD_W

    @pl.when(pl.program_id(1) == 0)
    def _():
        h_ref[...] = jnp.zeros_like(h_ref)
        prev_ref[...] = jnp.zeros_like(prev_ref)

    row0 = lax.broadcasted_iota(jnp.int32, (tb, 1), 0) == 0

    def mixed(ref, j):
        p = ref[...]
        lanes = slice(j * GROUP_W, (j + 1) * GROUP_W)
        sh = jnp.where(row0, prev_ref[0:1, lanes], pltpu.roll(p, 1, axis=0))
        prev_new = p[tb - 1:tb, :]
        return p + mu_ref[0:1, lanes] * (sh - p), prev_new

    r, pr_ = mixed(r_ref, 0)
    k, pk_ = mixed(k_ref, 1)
    v, pv_ = mixed(v_ref, 2)
    lo, pl_ = mixed(lo_ref, 3)
    for j, pn in enumerate((pr_, pk_, pv_, pl_)):
        prev_ref[0:1, j * GROUP_W:(j + 1) * GROUP_W] = pn

    bd = _block_diag_mask(GROUP_W, HEAD_W)
    bd_bf16 = bd.astype(BF16)

    w_raw = -_softplus(-(w0_ref[...] + _mm(jnp.tanh(lo), w2_ref[...]))) - 0.5
    lw_s[...] = -jnp.exp(w_raw)
    alr = jax.nn.sigmoid(a0_ref[...] + _mm(lo, a2_ref[...]))
    g = _mm(jax.nn.sigmoid(lo), g2_ref[...])
    kkr = k * kk_ref[...]
    kk = kkr / jnp.maximum(jnp.sqrt(_head_sum(kkr * kkr, bd_bf16)), 1e-12)
    k = k * (1.0 + (alr - 1.0) * ka_ref[...])
    if has_vmix:
        gate = jax.nn.sigmoid(v0_ref[...] + _mm(_mm(v, v1_ref[...]), v2_ref[...]))
        v = v + (vf_ref[...] - v) * gate
    else:
        vf_out_ref[...] = v
    a_s[...] = -kk
    b_s[...] = kk * alr
    k_s[...] = k
    r_s[...] = r
    v_s[...] = v

    n4 = nh * ch
    ri = lax.broadcasted_iota(jnp.int32, (n4, n4), 0)
    ci = lax.broadcasted_iota(jnp.int32, (n4, n4), 1)
    same_head = (ri // ch) == (ci // ch)
    strict = same_head & ((ri % ch) > (ci % ch))
    incl = same_head & ((ri % ch) >= (ci % ch))
    eye = (ri == ci).astype(F32)
    same_blk = {b: (ri // b) == (ci // b) for b in (8, 16, 32, 64)}
    hm = ((lax.broadcasted_iota(jnp.int32, (n4, GROUP_W), 0) // ch)
          == (lax.broadcasted_iota(jnp.int32, (n4, GROUP_W), 1) // HEAD_W))
    tri = (lax.broadcasted_iota(jnp.int32, (ch, ch), 0)
           >= lax.broadcasted_iota(jnp.int32, (ch, ch), 1)).astype(BF16)

    def stack(x):
        return jnp.where(hm, jnp.concatenate([x] * nh, axis=0), 0.0)

    def dot(x, y, dims=_NN):
        return lax.dot_general(x.astype(BF16), y.astype(BF16), (dims, ((), ())),
                               preferred_element_type=F32)

    chunks = range(tb // ch)
    pre = []
    for c in chunks:
        sl = slice(c * ch, (c + 1) * ch)
        lw = lw_s[sl, :]
        cl = _exact_lhs_mm(tri, lw)
        cl_end = cl[ch - 1:ch, :]
        e_in, e_ex = jnp.exp(cl), jnp.exp(cl - lw)
        e_neg, e_end = jnp.exp(-cl), jnp.exp(cl_end - cl)
        av, bv, kv, rv, vv = a_s[sl, :], b_s[sl, :], k_s[sl, :], r_s[sl, :], v_s[sl, :]
        pre.append(dict(
            at4=stack(av * e_ex), rt4=stack(rv * e_in), v4=stack(vv),
            bt4=jnp.concatenate([bv * e_neg] * nh, axis=0),
            kt4=jnp.concatenate([kv * e_neg] * nh, axis=0),
            bh4=stack(bv * e_end), kh4=stack(kv * e_end), g_end=jnp.exp(cl_end)))
    for d in pre:
        ar = jnp.concatenate([d["at4"], d["rt4"]], axis=0)
        sb = dot(ar, d["bt4"], _NT)
        sk = dot(ar, d["kt4"], _NT)
        d["l_ab"] = jnp.where(strict, sb[:n4], 0.0)
        d["l_ak"] = jnp.where(strict, sk[:n4], 0.0)
        d["l_rb"] = jnp.where(incl, sb[n4:], 0.0)
        d["l_rk"] = jnp.where(incl, sk[n4:], 0.0)
    for d in pre:
        nb8 = jnp.where(same_blk[8], d["l_ab"], 0.0)
        d["tinv"] = eye + nb8
        d["pw"] = dot(nb8, nb8)
    for d in pre:
        d["tinv"] = d["tinv"] + dot(d["tinv"], d["pw"])
        d["pw"] = dot(d["pw"], d["pw"])
    for d in pre:
        d["tinv"] = d["tinv"] + dot(d["tinv"], d["pw"])
    blk = 8
    while blk < ch:
        for d in pre:
            d["pw"] = dot(jnp.where(same_blk[2 * blk] & ~same_blk[blk], d["l_ab"], 0.0), d["tinv"])
        for d in pre:
            d["tinv"] = d["tinv"] + dot(d["tinv"], d["pw"])
        blk *= 2
    for d in pre:
        d["lakv"] = dot(d["l_ak"], d["v4"])
        d["lrkv"] = dot(d["l_rk"], d["v4"])
        d["khv"] = dot(d["kh4"], d["v4"], _TN)
    for d in pre:
        d["x12"] = dot(d["tinv"], jnp.concatenate([d["at4"], d["lakv"]], axis=1))
    for c, d in zip(chunks, pre):
        mn = dot(d["bh4"], d["x12"], _TN)
        pz = dot(d["l_rb"], d["x12"])
        m_s[c] = jnp.where(ri == ci, d["g_end"], 0.0) + mn[:, :GROUP_W]
        n_s[c] = mn[:, GROUP_W:] + d["khv"]
        p_s[c] = d["rt4"] + pz[:, :GROUP_W]
        z_s[c] = pz[:, GROUP_W:] + d["lrkv"]

    h = h_ref[...]
    for c in range(tb // ch):
        y4 = dot(p_s[c], h) + z_s[c]
        y = y4[0:ch]
        for j in range(1, nh):
            y = y + y4[j * ch:(j + 1) * ch]
        y_s[c * ch:(c + 1) * ch, :] = y
        h = _dg3(m_s[c], h, _NN) + n_s[c]
    h_ref[...] = h

    y = y_s[...]
    r, k, v = r_s[...], k_s[...], v_s[...]
    mean = _head_sum(y, bd_bf16) * (1.0 / HEAD_W)
    d = y - mean
    var = _head_sum(d * d, bd_bf16) * (1.0 / HEAD_W)
    y = d * lax.rsqrt(var + RWKV_GN_EPS) * lnw_ref[...] + lnb_ref[...]
    y = y + _head_sum(r * k * rk_ref[...], bd_bf16) * v
    o_ref[...] = _rms(y * g, mg_ref[...])


def _pad_rows(w, start, total=GROUP_W):
    out = jnp.zeros((total, w.shape[1]), F32).at[start:start + w.shape[0]].set(w)
    return out.astype(BF16)


def _rwkv(proj3, v_first, lyr, p, merge_g, tb):
    bsz, t, _ = proj3.shape
    has_vmix = v_first is not None
    row = lambda x: x.reshape(1, -1)
    w2 = _pad_rows(p["rwkv_w2"][lyr], 0)
    a2 = _pad_rows(p["rwkv_a2"][lyr], RWKV_W_LORA)
    g2 = _pad_rows(p["rwkv_g2"][lyr], RWKV_W_LORA + RWKV_A_LORA)
    args = [proj3, proj3, proj3, proj3]
    specs = [_col_spec(tb, COL_R), _col_spec(tb, COL_K), _col_spec(tb, COL_V), _col_spec(tb, COL_LORA)]
    if has_vmix:
        args.append(v_first)
        specs.append(pl.BlockSpec((None, tb, GROUP_W), lambda b, i: (b, i, 0)))
    args += [row(p["rwkv_mu"][lyr]), row(p["rwkv_w0"][lyr]), w2, row(p["rwkv_a0"][lyr]), a2, g2,
             row(p["rwkv_k_k"][lyr]), row(p["rwkv_k_a"][lyr]), row(p["rwkv_r_k"][lyr]),
             row(p["rwkv_ln_w"][lyr]), row(p["rwkv_ln_b"][lyr]), row(merge_g)]
    sq = _full_spec((GROUP_W, GROUP_W))
    specs += [_row_spec(4 * GROUP_W), _row_spec(), sq, _row_spec(), sq, sq,
              _row_spec(), _row_spec(), _row_spec(), _row_spec(), _row_spec(), _row_spec()]
    if has_vmix:
        v1 = jnp.zeros((GROUP_W, 128), F32).at[:, :RWKV_V_LORA].set(p["rwkv_v1"][lyr - 1]).astype(BF16)
        v2 = jnp.zeros((128, GROUP_W), F32).at[:RWKV_V_LORA].set(p["rwkv_v2"][lyr - 1]).astype(BF16)
        args += [row(p["rwkv_v0"][lyr - 1]), v1, v2]
        specs += [_row_spec(), _full_spec((GROUP_W, 128)), _full_spec((128, GROUP_W))]
    out_blk = pl.BlockSpec((None, tb, GROUP_W), lambda b, i: (b, i, 0))
    out_sds = jax.ShapeDtypeStruct((bsz, t, GROUP_W), F32)
    blk = pltpu.VMEM((tb, GROUP_W), F32)
    mats = pltpu.VMEM((tb // RWKV_CHUNK, GROUP_W, GROUP_W), F32)
    res = pl.pallas_call(
        functools.partial(_rwkv_kernel, has_vmix=has_vmix),
        grid=(bsz, t // tb),
        in_specs=specs,
        out_specs=out_blk if has_vmix else [out_blk, out_blk],
        out_shape=out_sds if has_vmix else [out_sds, out_sds],
        scratch_shapes=[pltpu.VMEM((GROUP_W, GROUP_W), F32), pltpu.VMEM((1, 4 * GROUP_W), F32),
                        blk, blk, blk, blk, blk, blk, blk, mats, mats, mats, mats],
        compiler_params=_cparams("parallel", "arbitrary"),
        name="rwkv7",
    )(*args)
    if has_vmix:
        return res, v_first
    return res[0], res[1]


def _bf16_bits(x):
    u = lax.bitcast_convert_type(x, jnp.uint32)
    r = u + jnp.uint32(0x7FFF) + ((u >> 16) & jnp.uint32(1))
    return r & jnp.uint32(0xFFFF0000)


def _merge_router_kernel(oa_ref, ob_ref, oc_ref, od_ref, h_ref, wo_ref, nw_ref, wr_ref, br_ref,
                         h1_ref, hn_ref, ids_ref, wts_ref, cnt_ref):
    mix = jnp.concatenate([oa_ref[...], ob_ref[...], oc_ref[...], od_ref[...]], axis=1)
    h1 = h_ref[...] + jnp.dot(mix.astype(BF16), wo_ref[...], preferred_element_type=F32)
    h1_ref[...] = h1
    hn = _rms(h1, nw_ref[...])
    half = hn.shape[1] // 2
    hn_ref[...] = _bf16_bits(hn[:, :half]) | (_bf16_bits(hn[:, half:]) >> 16)
    logits = _dg3(hn, wr_ref[...], _NN) + br_ref[...]
    lane = lax.broadcasted_iota(jnp.int32, logits.shape, 1)
    big = jnp.int32(ROUTER_LANES)
    is_c = (lane >= N_EXPERTS) & (lane < N_EXPERTS + MOE_GROUPS)
    cm = jnp.max(jnp.where(is_c, logits, NEG_BIG), axis=-1, keepdims=True)
    gsel = jnp.min(jnp.where(is_c & (logits == cm), lane, big), axis=-1, keepdims=True) - N_EXPERTS
    p_g = 1.0 / jnp.sum(jnp.where(is_c, jnp.exp(logits - cm), 0.0), axis=-1, keepdims=True)
    lo = gsel * MOE_PER_GROUP
    in_g = (lane >= lo) & (lane < lo + MOE_PER_GROUP)
    m1 = jnp.max(jnp.where(in_g, logits, NEG_BIG), axis=-1, keepdims=True)
    i1 = jnp.min(jnp.where(in_g & (logits == m1), lane, big), axis=-1, keepdims=True)
    in_g2 = in_g & (lane != i1)
    m2 = jnp.max(jnp.where(in_g2, logits, NEG_BIG), axis=-1, keepdims=True)
    i2 = jnp.min(jnp.where(in_g2 & (logits == m2), lane, big), axis=-1, keepdims=True)
    w1 = p_g / (1.0 + jnp.exp(m2 - m1))
    w2 = p_g - w1
    two = lax.broadcasted_iota(jnp.int32, ids_ref.shape, 1)
    ids_ref[...] = jnp.where(two == 0, i1, i2)
    wts_ref[...] = jnp.where(two == 0, w1, w2)

    @pl.when(pl.program_id(0) == 0)
    def _():
        cnt_ref[...] = jnp.zeros_like(cnt_ref)

    cnt_ref[...] += jnp.sum(((lane == i1) | (lane == i2)).astype(F32), axis=0, keepdims=True)


def _merge_router(outs, h2d, w_out_bf16, norm_w, w_router, b_router, tm):
    n, d = h2d.shape
    grp = pl.BlockSpec((tm, GROUP_W), lambda i: (i, 0))
    full = lambda a, b: pl.BlockSpec((a, b), lambda i: (0, 0))
    rowblk = lambda w: pl.BlockSpec((tm, w), lambda i: (i, 0))
    return pl.pallas_call(
        _merge_router_kernel,
        grid=(n // tm,),
        in_specs=[grp, grp, grp, grp, rowblk(d), full(d, d), full(1, d), full(d, ROUTER_LANES),
                  full(1, ROUTER_LANES)],
        out_specs=[rowblk(d), rowblk(d // 2), rowblk(MOE_TOPK), rowblk(MOE_TOPK), full(1, ROUTER_LANES)],
        out_shape=[jax.ShapeDtypeStruct((n, d), F32), jax.ShapeDtypeStruct((n, d // 2), jnp.uint32),
                   jax.ShapeDtypeStruct((n, MOE_TOPK), jnp.int32),
                   jax.ShapeDtypeStruct((n, MOE_TOPK), F32),
                   jax.ShapeDtypeStruct((1, ROUTER_LANES), F32)],
        compiler_params=_cparams("arbitrary"),
        name="merge_router",
    )(*[o.reshape(n, GROUP_W) for o in outs], h2d, w_out_bf16, norm_w.reshape(1, d), w_router, b_router)


def _row_copy(src_hbm, src_row, dst_ref, dst_row, sem):
    return pltpu.make_async_copy(src_hbm.at[pl.ds(src_row, 1), :], dst_ref.at[pl.ds(dst_row, 1), :], sem)


def _position_kernel(ids_ref, base_ref, pos_ref, run_ref):
    tm = ids_ref.shape[0]

    @pl.when(pl.program_id(0) == 0)
    def _():
        run_ref[...] = jnp.zeros_like(run_ref)

    i1, i2 = ids_ref[:, 0:1], ids_ref[:, 1:2]
    lane = lax.broadcasted_iota(jnp.int32, (tm, ROUTER_LANES), 1)
    hit = ((lane == i1) | (lane == i2)).astype(BF16)
    earlier = (lax.broadcasted_iota(jnp.int32, (tm, tm), 0)
               > lax.broadcasted_iota(jnp.int32, (tm, tm), 1)).astype(BF16)
    rank = jnp.dot(earlier, hit, preferred_element_type=F32) + run_ref[...]
    where_to = rank + base_ref[...]
    p1 = jnp.sum(jnp.where(lane == i1, where_to, 0.0), axis=-1, keepdims=True)
    p2 = jnp.sum(jnp.where(lane == i2, where_to, 0.0), axis=-1, keepdims=True)
    two = lax.broadcasted_iota(jnp.int32, pos_ref.shape, 1)
    pos_ref[...] = jnp.where(two == 0, p1, p2).astype(jnp.int32)
    run_ref[...] += jnp.sum(hit.astype(F32), axis=0, keepdims=True)


def _positions(ids, group_start, tm):
    n = ids.shape[0]
    return pl.pallas_call(
        _position_kernel,
        grid=(n // tm,),
        in_specs=[pl.BlockSpec((tm, MOE_TOPK), lambda i: (i, 0)),
                  pl.BlockSpec((1, ROUTER_LANES), lambda i: (0, 0))],
        out_specs=pl.BlockSpec((tm, MOE_TOPK), lambda i: (i, 0)),
        out_shape=jax.ShapeDtypeStruct((n, MOE_TOPK), jnp.int32),
        scratch_shapes=[pltpu.VMEM((1, ROUTER_LANES), F32)],
        compiler_params=_cparams("arbitrary"),
        name="moe_positions",
    )(ids, group_start)


def _dispatch_kernel(p0_ref, p1_ref, x_ref, xs_in_hbm, xs_hbm, sem):
    del xs_in_hbm
    n = x_ref.shape[0]

    def copies(i):
        return (pltpu.make_async_copy(x_ref.at[pl.ds(i, 1), :], xs_hbm.at[pl.ds(p0_ref[i], 1), :], sem),
                pltpu.make_async_copy(x_ref.at[pl.ds(i, 1), :], xs_hbm.at[pl.ds(p1_ref[i], 1), :], sem))

    def issue(i, c):
        for cp in copies(i):
            cp.start()
        return c

    def drain(i, c):
        for cp in copies(i):
            cp.wait()
        return c

    lax.fori_loop(0, n, issue, 0, unroll=8)
    lax.fori_loop(0, n, drain, 0, unroll=8)


def _dispatch(x, pos0, pos1, n_rows, chunk):
    n, w = x.shape
    smem = lambda: pl.BlockSpec((chunk,), lambda s: (s,), memory_space=pltpu.SMEM)
    return pl.pallas_call(
        _dispatch_kernel,
        grid=(n // chunk,),
        in_specs=[smem(), smem(), pl.BlockSpec((chunk, w), lambda s: (s, 0)),
                  pl.BlockSpec(memory_space=pl.ANY)],
        out_specs=pl.BlockSpec(memory_space=pl.ANY),
        out_shape=jax.ShapeDtypeStruct((n_rows, w), x.dtype),
        input_output_aliases={3: 0},
        scratch_shapes=[pltpu.SemaphoreType.DMA(())],
        compiler_params=_cparams("arbitrary"),
        name="moe_dispatch",
    )(pos0, pos1, x, jnp.zeros((n_rows, w), x.dtype))


def _expert_kernel(te_ref, nt_ref, xs_ref, wg_ref, wu_ref, wd_ref, o_ref):
    i = pl.program_id(0)

    @pl.when(i < nt_ref[0])
    def _():
        p = xs_ref[...]
        xa = lax.bitcast_convert_type(p & jnp.uint32(0xFFFF0000), F32).astype(BF16)
        xb = lax.bitcast_convert_type(p << 16, F32).astype(BF16)
        x = jnp.concatenate([xa, xb], axis=1)
        he = (_silu(jnp.dot(x, wg_ref[...], preferred_element_type=F32))
              * jnp.dot(x, wu_ref[...], preferred_element_type=F32))
        o_ref[...] = jnp.dot(he.astype(BF16), wd_ref[...], preferred_element_type=F32)

    @pl.when(i >= nt_ref[0])
    def _():
        o_ref[...] = jnp.zeros_like(o_ref)


def _experts(xs, tile_expert, n_tiles_used, wg, wu, wd, tm):
    p, half = xs.shape
    d = 2 * half
    grid_spec = pltpu.PrefetchScalarGridSpec(
        num_scalar_prefetch=2,
        grid=(p // tm,),
        in_specs=[pl.BlockSpec((tm, half), lambda i, te, nt: (i, 0)),
                  pl.BlockSpec((None, d, D_EXPERT), lambda i, te, nt: (te[i], 0, 0)),
                  pl.BlockSpec((None, d, D_EXPERT), lambda i, te, nt: (te[i], 0, 0)),
                  pl.BlockSpec((None, D_EXPERT, d), lambda i, te, nt: (te[i], 0, 0))],
        out_specs=pl.BlockSpec((tm, d), lambda i, te, nt: (i, 0)),
    )
    return pl.pallas_call(
        _expert_kernel,
        grid_spec=grid_spec,
        out_shape=jax.ShapeDtypeStruct((p, d), F32),
        compiler_params=_cparams("arbitrary"),
        name="moe_experts",
    )(tile_expert, n_tiles_used, xs, wg, wu, wd)


def _combine_kernel(i0_ref, i1_ref, h1_ref, w_ref, fw_ref, ys_hbm, o_ref, buf0, buf1, sem, *, final_norm):
    n = o_ref.shape[0]

    def issue(i, c):
        _row_copy(ys_hbm, i0_ref[i], buf0, i, sem).start()
        _row_copy(ys_hbm, i1_ref[i], buf1, i, sem).start()
        return c

    def drain(i, c):
        _row_copy(ys_hbm, 0, buf0, i, sem).wait()
        _row_copy(ys_hbm, 0, buf1, i, sem).wait()
        return c

    lax.fori_loop(0, n, issue, 0, unroll=8)
    lax.fori_loop(0, n, drain, 0, unroll=8)
    out = h1_ref[...] + w_ref[:, 0:1] * buf0[...] + w_ref[:, 1:2] * buf1[...]
    if final_norm:
        out = _rms(out, fw_ref[...])
    o_ref[...] = out


def _combine(ys, pos0, pos1, wts, h1, final_w, final_norm, chunk):
    n, d = h1.shape
    smem = lambda: pl.BlockSpec((chunk,), lambda s: (s,), memory_space=pltpu.SMEM)
    return pl.pallas_call(
        functools.partial(_combine_kernel, final_norm=final_norm),
        grid=(n // chunk,),
        in_specs=[smem(), smem(), pl.BlockSpec((chunk, d), lambda s: (s, 0)),
                  pl.BlockSpec((chunk, MOE_TOPK), lambda s: (s, 0)),
                  pl.BlockSpec((1, d), lambda s: (0, 0)), pl.BlockSpec(memory_space=pl.ANY)],
        out_specs=pl.BlockSpec((chunk, d), lambda s: (s, 0)),
        out_shape=jax.ShapeDtypeStruct((n, d), F32),
        scratch_shapes=[pltpu.VMEM((chunk, d), F32), pltpu.VMEM((chunk, d), F32),
                        pltpu.SemaphoreType.DMA(())],
        compiler_params=_cparams("arbitrary"),
        name="moe_combine",
    )(pos0, pos1, h1, wts, final_w.reshape(1, d), ys)


def _moe(hn_packed, ids, wts, counts, h1, wg, wu, wd, final_w, final_norm, tm):
    n = h1.shape[0]
    n_tiles = (n * MOE_TOPK) // tm + N_EXPERTS
    cnt = counts[0, :N_EXPERTS].astype(jnp.int32)
    padded = ((cnt + tm - 1) // tm) * tm
    ends = jnp.cumsum(padded)
    tile_start = jnp.arange(n_tiles, dtype=jnp.int32) * tm
    tile_expert = jnp.minimum(jnp.sum(ends[None, :] <= tile_start[:, None], axis=1),
                              N_EXPERTS - 1).astype(jnp.int32)
    n_tiles_used = (ends[-1] // tm).astype(jnp.int32).reshape(1)
    group_start = jnp.zeros((1, ROUTER_LANES), F32).at[0, :N_EXPERTS].set((ends - padded).astype(F32))
    pos = _positions(ids, group_start, _pick_div(n, 512))
    pos0, pos1 = pos[:, 0], pos[:, 1]
    xs = _dispatch(hn_packed, pos0, pos1, n_tiles * tm, _pick_div(n, 1024))
    ys = _experts(xs, tile_expert, n_tiles_used, wg, wu, wd, tm)
    return _combine(ys, pos0, pos1, wts, h1, final_w, final_norm, _pick_div(n, 1024))


def _pick_div(n, pref):
    while n % pref:
        pref //= 2
    return pref


def _pick(n, pref):
    return pref if n % pref == 0 else n


def kernel(x, norm1_w, w_in, hgrn_lb_logits, hgrn_norm_w, s5_lambda_re, s5_lambda_im, s5_log_dt, s5_b_re, s5_b_im, s5_c_re, s5_c_im, s5_d, s5_w_glu, rwkv_mu, rwkv_w0, rwkv_w2, rwkv_a0, rwkv_a2, rwkv_g2, rwkv_k_k, rwkv_k_a, rwkv_r_k, rwkv_v0, rwkv_v1, rwkv_v2, rwkv_ln_w, rwkv_ln_b, lru_conv_w, lru_conv_b, lru_wa, lru_ba, lru_wx, lru_bx, lru_lambda, merge_gain, w_out, norm2_w, moe_coarse_w, moe_coarse_b, moe_fine_w, moe_fine_b, moe_w_gate, moe_w_up, moe_w_down, final_norm_w):
    bsz, t, d = x.shape
    n = bsz * t
    depth = w_in.shape[0]
    rw = dict(rwkv_mu=rwkv_mu, rwkv_w0=rwkv_w0, rwkv_w2=rwkv_w2, rwkv_a0=rwkv_a0, rwkv_a2=rwkv_a2,
              rwkv_g2=rwkv_g2, rwkv_k_k=rwkv_k_k, rwkv_k_a=rwkv_k_a, rwkv_r_k=rwkv_r_k,
              rwkv_v0=rwkv_v0, rwkv_v1=rwkv_v1, rwkv_v2=rwkv_v2, rwkv_ln_w=rwkv_ln_w,
              rwkv_ln_b=rwkv_ln_b)
    lb_all = jnp.cumsum(jax.nn.softmax(hgrn_lb_logits.astype(F32), axis=0), axis=0)
    lb_all = lb_all - lb_all[:1]

    tm_proj = _pick(n, 512)
    tm_moe = 256
    tb_hgrn = _pick(t, 256)
    tb_lru = _pick(t, 256)
    tb_rwkv = _pick(t, 256)
    tb_s5 = _pick(t, 512)

    h = x.reshape(n, d)
    v_first = None
    for l in range(depth):
        mg = merge_gain[l].reshape(4, GROUP_W)
        proj = _in_proj(h, norm1_w[l], w_in[l].astype(BF16), tm_proj)
        proj3 = proj.reshape(bsz, t, -1)
        o_a = _hgrn(proj3, lb_all[l], hgrn_norm_w[l], mg[0], tb_hgrn)
        mats = _s5_matrices(s5_lambda_re[l], s5_lambda_im[l], s5_log_dt[l], s5_b_re[l], s5_b_im[l],
                            s5_c_re[l], s5_c_im[l])
        o_b = _s5(proj3, mats, s5_d[l], s5_w_glu[l].astype(BF16), mg[1], tb_s5)
        o_c, v_first = _rwkv(proj3, v_first, l, rw, mg[2], tb_rwkv)
        o_d = _lru(proj3, lru_conv_w[l], lru_conv_b[l], _block_diag_weight(lru_wa[l]), lru_ba[l],
                   _block_diag_weight(lru_wx[l]), lru_bx[l], lru_lambda[l], mg[3], tb_lru)
        w_router = jnp.concatenate(
            [moe_fine_w[l].transpose(1, 0, 2).reshape(d, N_EXPERTS), moe_coarse_w[l],
             jnp.zeros((d, ROUTER_LANES - N_EXPERTS - MOE_GROUPS), F32)], axis=1)
        b_router = jnp.concatenate(
            [moe_fine_b[l].reshape(N_EXPERTS), moe_coarse_b[l],
             jnp.zeros((ROUTER_LANES - N_EXPERTS - MOE_GROUPS,), F32)]).reshape(1, ROUTER_LANES)
        h1, hn, ids, wts, counts = _merge_router((o_a, o_b, o_c, o_d), h, w_out[l].astype(BF16),
                                                 norm2_w[l], w_router, b_router, tm_proj)
        h = _moe(hn, ids, wts, counts, h1, moe_w_gate[l].astype(BF16), moe_w_up[l].astype(BF16),
                 moe_w_down[l].astype(BF16), final_norm_w, l == depth - 1, tm_moe)
    return h.reshape(bsz, t, d)
```

```python
import functools
import math

import jax
import jax.numpy as jnp
from jax import lax
from jax.experimental import pallas as pl
from jax.experimental.pallas import tpu as pltpu

F32 = jnp.float32
BF16 = jnp.bfloat16

D_MODEL = 1024
GROUP_W = 256
RMS_EPS = 1e-6
HEAD_W = 64
HGRN_CHUNK = 16
S5_CH = 16
S5_GROUPS = GROUP_W // S5_CH
S5_STATE = 64
S5_CHUNK = 16
RWKV_CHUNK = 64
RWKV_GN_EPS = 64e-5
RWKV_W_LORA = 64
RWKV_A_LORA = 64
RWKV_G_LORA = 128
RWKV_V_LORA = 32
LRU_CONV = 4
LRU_C = 8.0
MOE_GROUPS = 4
MOE_PER_GROUP = 8
N_EXPERTS = MOE_GROUPS * MOE_PER_GROUP
MOE_TOPK = 2
D_EXPERT = 512
ROUTER_LANES = 128
NEG_BIG = -1e30
VMEM_LIMIT = 56 * 1024 * 1024

COL_HQ, COL_HF, COL_HI, COL_HG, COL_S5, COL_R, COL_K, COL_V, COL_LORA, COL_LG, COL_LX = range(11)


def _cparams(*sem):
    return pltpu.CompilerParams(dimension_semantics=sem, vmem_limit_bytes=VMEM_LIMIT)


def _mm(a, b):
    return jnp.dot(a.astype(BF16), b.astype(BF16), preferred_element_type=F32)


def _split2(x):
    hi = x.astype(BF16)
    lo = (x - hi.astype(F32)).astype(BF16)
    return hi, lo


def _dg3(a, b, dims):
    ah, al = _split2(a)
    bh, bl = _split2(b)
    d = lambda x, y: lax.dot_general(x, y, (dims, ((), ())), preferred_element_type=F32)
    return d(ah, bh) + d(ah, bl) + d(al, bh)


_NN = ((1,), (0,))
_NT = ((1,), (1,))
_TN = ((0,), (0,))


def _exact_lhs_mm(m_bf16, x):
    h1 = x.astype(BF16)
    r1 = x - h1.astype(F32)
    h2 = r1.astype(BF16)
    h3 = (r1 - h2.astype(F32)).astype(BF16)
    d = lambda y: jnp.dot(m_bf16, y, preferred_element_type=F32)
    return d(h1) + d(h2) + d(h3)


def _head_sum(x, bd_bf16):
    return jnp.dot(x.astype(BF16), bd_bf16, preferred_element_type=F32)


def _block_diag_mask(n, blk):
    r = lax.broadcasted_iota(jnp.int32, (n, n), 0) // blk
    c = lax.broadcasted_iota(jnp.int32, (n, n), 1) // blk
    return r == c


def _rms(x, w):
    return x * lax.rsqrt(jnp.mean(x * x, axis=-1, keepdims=True) + RMS_EPS) * w


def _silu(x):
    return x * jax.nn.sigmoid(x)


def _softplus(x):
    return jnp.maximum(x, 0.0) + jnp.log(1.0 + jnp.exp(-jnp.abs(x)))


def _in_proj_kernel(x_ref, nw_ref, w_ref, o_ref):
    y = _rms(x_ref[...], nw_ref[...])
    o_ref[...] = jnp.dot(y.astype(BF16), w_ref[...], preferred_element_type=F32)


def _in_proj(h2d, norm_w, w_bf16, tm):
    n, d = h2d.shape
    d_in = w_bf16.shape[1]
    return pl.pallas_call(
        _in_proj_kernel,
        grid=(n // tm,),
        in_specs=[pl.BlockSpec((tm, d), lambda i: (i, 0)),
                  pl.BlockSpec((1, d), lambda i: (0, 0)),
                  pl.BlockSpec((d, d_in), lambda i: (0, 0))],
        out_specs=pl.BlockSpec((tm, d_in), lambda i: (i, 0)),
        out_shape=jax.ShapeDtypeStruct((n, d_in), F32),
        compiler_params=_cparams("parallel"),
        name="in_proj",
    )(h2d, norm_w.reshape(1, d), w_bf16)


def _col_spec(tb, col):
    return pl.BlockSpec((None, tb, GROUP_W), lambda b, t: (b, t, col))


def _row_spec(width=GROUP_W):
    return pl.BlockSpec((1, width), lambda b, t: (0, 0))


def _full_spec(shape):
    return pl.BlockSpec(shape, lambda b, t: (0,) * len(shape))


def _hgrn_kernel(q_ref, f_ref, i_ref, g_ref, lb_ref, nw_ref, mg_ref, o_ref,
                 st_ref, q_s, k_s, v_s, lf_s, o_s):
    ch = HGRN_CHUNK
    tb = q_ref.shape[0]

    @pl.when(pl.program_id(1) == 0)
    def _():
        st_ref[...] = jnp.zeros_like(st_ref)

    lb = lb_ref[...]
    fx = f_ref[...]
    x1 = jnp.log(lb)
    x2 = jnp.log(1.0 - lb) - _softplus(-fx)
    m = jnp.maximum(x1, x2)
    lf_s[...] = m + jnp.log(jnp.exp(x1 - m) + jnp.exp(x2 - m))
    q_s[...] = _silu(q_ref[...])
    k_s[...] = (1.0 - lb) * jax.nn.sigmoid(-fx)
    v_s[...] = _silu(i_ref[...])

    bd = _block_diag_mask(GROUP_W, HEAD_W)
    bd_bf16 = bd.astype(BF16)
    tri = (lax.broadcasted_iota(jnp.int32, (ch, ch), 0)
           >= lax.broadcasted_iota(jnp.int32, (ch, ch), 1))
    tri_bf16 = tri.astype(BF16)
    tri3 = (lax.broadcasted_iota(jnp.int32, (ch, ch, 1), 0)
            <= lax.broadcasted_iota(jnp.int32, (ch, ch, 1), 1))

    st = st_ref[...]
    for c in range(tb // ch):
        sl = slice(c * ch, (c + 1) * ch)
        qc, kc, vc = q_s[sl, :], k_s[sl, :], v_s[sl, :]
        b = _exact_lhs_mm(tri_bf16, lf_s[sl, :])
        rel = b[None, :, :] - b[:, None, :]
        dec = jnp.exp(jnp.where(tri3, rel, NEG_BIG))
        p = (qc[None, :, :] * kc[:, None, :]) * dec
        sc = jnp.dot(p.reshape(ch * ch, GROUP_W).astype(BF16), bd_bf16,
                     preferred_element_type=F32).reshape(ch, ch, GROUP_W)
        o_intra = jnp.sum(sc * vc[:, None, :], axis=0)
        o_inter = lax.dot_general((qc * jnp.exp(b)).astype(BF16), st.astype(BF16),
                                  (_NT, ((), ())), preferred_element_type=F32)
        b_end = b[ch - 1:ch, :]
        kh = kc * jnp.exp(b_end - b)
        upd = lax.dot_general(vc.astype(BF16), kh.astype(BF16), (_TN, ((), ())),
                              preferred_element_type=F32)
        st = st * jnp.exp(b_end) + jnp.where(bd, upd, 0.0)
        o_s[sl, :] = o_intra + o_inter
    st_ref[...] = st

    o = o_s[...]
    ms = _head_sum(o * o, bd_bf16) * (1.0 / HEAD_W)
    o = o * lax.rsqrt(ms + RMS_EPS) * nw_ref[...] * _silu(g_ref[...])
    o_ref[...] = _rms(o, mg_ref[...])


def _hgrn(proj3, lb, norm_w, merge_g, tb):
    bsz, t, _ = proj3.shape
    blk = pltpu.VMEM((tb, GROUP_W), F32)
    return pl.pallas_call(
        _hgrn_kernel,
        grid=(bsz, t // tb),
        in_specs=[_col_spec(tb, COL_HQ), _col_spec(tb, COL_HF), _col_spec(tb, COL_HI),
                  _col_spec(tb, COL_HG), _row_spec(), _row_spec(), _row_spec()],
        out_specs=pl.BlockSpec((None, tb, GROUP_W), lambda b, i: (b, i, 0)),
        out_shape=jax.ShapeDtypeStruct((bsz, t, GROUP_W), F32),
        scratch_shapes=[pltpu.VMEM((GROUP_W, GROUP_W), F32), blk, blk, blk, blk, blk],
        compiler_params=_cparams("parallel", "arbitrary"),
        name="hgrn2",
    )(proj3, proj3, proj3, proj3, lb.reshape(1, -1), norm_w.reshape(1, -1), merge_g.reshape(1, -1))


def _lru_kernel(xg_ref, xr_ref, cw_ref, cb_ref, wa_ref, ba_ref, wx_ref, bx_ref, lam_ref, mg_ref,
                o_ref, buf_ref, h_ref):
    tb = xr_ref.shape[0]

    @pl.when(pl.program_id(1) == 0)
    def _():
        buf_ref[0:8, :] = jnp.zeros((8, GROUP_W), F32)
        h_ref[...] = jnp.zeros_like(h_ref)

    xr = xr_ref[...]
    buf_ref[8:8 + tb, :] = xr
    xc = cb_ref[...] + jnp.zeros_like(xr)
    for j in range(LRU_CONV):
        xc = xc + cw_ref[j:j + 1, :] * buf_ref[pl.ds(8 - (LRU_CONV - 1) + j, tb), :]
    buf_ref[0:8, :] = xr[tb - 8:tb, :]

    r = jax.nn.sigmoid(jnp.dot(xc.astype(BF16), wa_ref[...], preferred_element_type=F32) + ba_ref[...])
    gi = jax.nn.sigmoid(jnp.dot(xc.astype(BF16), wx_ref[...], preferred_element_type=F32) + bx_ref[...])
    log_a = -LRU_C * r * _softplus(-lam_ref[...])
    a = jnp.exp(log_a)
    x = jnp.sqrt(1.0 - jnp.exp(2.0 * log_a)) * (gi * xc)

    rows = lax.broadcasted_iota(jnp.int32, (tb, 1), 0)
    k = 1
    while k < tb:
        keep = rows >= k
        x = x + jnp.where(keep, a * pltpu.roll(x, k, axis=0), 0.0)
        a = jnp.where(keep, a * pltpu.roll(a, k, axis=0), a)
        k *= 2
    h = x + a * h_ref[...]
    h_ref[...] = h[tb - 1:tb, :]
    o_ref[...] = _rms(jax.nn.gelu(xg_ref[...]) * h, mg_ref[...])


def _lru(proj3, conv_w, conv_b, wa_bd, ba, wx_bd, bx, lam, merge_g, tb):
    bsz, t, _ = proj3.shape
    return pl.pallas_call(
        _lru_kernel,
        grid=(bsz, t // tb),
        in_specs=[_col_spec(tb, COL_LG), _col_spec(tb, COL_LX), _full_spec((LRU_CONV, GROUP_W)),
                  _row_spec(), _full_spec((GROUP_W, GROUP_W)), _row_spec(),
                  _full_spec((GROUP_W, GROUP_W)), _row_spec(), _row_spec(), _row_spec()],
        out_specs=pl.BlockSpec((None, tb, GROUP_W), lambda b, i: (b, i, 0)),
        out_shape=jax.ShapeDtypeStruct((bsz, t, GROUP_W), F32),
        scratch_shapes=[pltpu.VMEM((tb + 8, GROUP_W), F32), pltpu.VMEM((1, GROUP_W), F32)],
        compiler_params=_cparams("parallel", "arbitrary"),
        name="rglru",
    )(proj3, proj3, conv_w, conv_b.reshape(1, -1), wa_bd, ba.reshape(1, -1), wx_bd,
      bx.reshape(1, -1), lam.reshape(1, -1), merge_g.reshape(1, -1))


def _block_diag_weight(w):
    h, n, _ = w.shape
    eye = jnp.eye(h, dtype=w.dtype)
    return jnp.einsum('hij,hg->higj', w, eye).reshape(h * n, h * n).astype(BF16)


def _s5_matrices(lam_re, lam_im, log_dt, b_re, b_im, c_re, c_im):
    L, G, P, C = S5_CHUNK, S5_GROUPS, S5_STATE, S5_CH
    lr, li = lam_re.astype(F32), lam_im.astype(F32)
    dt = jnp.exp(log_dt.astype(F32))[:, None]
    mag = jnp.exp(lr * dt)
    a_re, a_im = mag * jnp.cos(li * dt), mag * jnp.sin(li * dt)
    den = lr * lr + li * li
    kap_re = ((a_re - 1.0) * lr + a_im * li) / den
    kap_im = (a_im * lr - (a_re - 1.0) * li) / den
    br, bi = b_re.astype(F32), b_im.astype(F32)
    bb_re = kap_re[..., None] * br - kap_im[..., None] * bi
    bb_im = kap_re[..., None] * bi + kap_im[..., None] * br
    cr, ci = c_re.astype(F32), c_im.astype(F32)
    eye = jnp.eye(G, dtype=F32)
    b_cat = jnp.concatenate(
        [jnp.einsum('gpc,gh->gchp', bb_re, eye).reshape(G * C, G * P),
         jnp.einsum('gpc,gh->gchp', bb_im, eye).reshape(G * C, G * P)], axis=1)
    c_cat = jnp.concatenate(
        [jnp.einsum('gcp,gh->gphc', cr, eye).reshape(G * P, G * C),
         -jnp.einsum('gcp,gh->gphc', ci, eye).reshape(G * P, G * C)], axis=0)
    mag_l = jnp.exp(L * lr * dt)
    a_step = jnp.stack([a_re.reshape(G * P), a_im.reshape(G * P)])
    a_chunk = jnp.stack([(mag_l * jnp.cos(L * li * dt)).reshape(G * P),
                         (mag_l * jnp.sin(L * li * dt)).reshape(G * P)])
    return b_cat.astype(BF16), c_cat.astype(BF16), a_step, a_chunk


def _s5_kernel(u0_ref, u1_ref, b_ref, c_ref, a_ref, al_ref, d_ref, wg_ref, mg_ref, o_ref,
               carry_ref, x_s, st_s, y_s):
    nb, tb, lanes = u0_ref.shape
    L = S5_CHUNK
    r = tb // L
    rows = nb * r
    ns = S5_GROUPS * S5_STATE

    @pl.when(pl.program_id(0) == 0)
    def _():
        carry_ref[...] = jnp.zeros_like(carry_ref)

    a_re, a_im = a_ref[0:1, :], a_ref[1:2, :]
    al_re, al_im = al_ref[0:1, :], al_ref[1:2, :]

    def advance(tau):
        sl = pl.ds(tau, r, stride=L)
        u = jnp.concatenate([u0_ref[:, sl, :], u1_ref[:, sl, :]], axis=-1).reshape(rows, 2 * lanes)
        bu = jnp.dot(u.astype(BF16), b_ref[...], preferred_element_type=F32)
        xr, xi = x_s[:, :ns], x_s[:, ns:]
        x_s[:, :ns] = a_re * xr - a_im * xi + bu[:, :ns]
        x_s[:, ns:] = a_re * xi + a_im * xr + bu[:, ns:]

    x_s[...] = jnp.zeros_like(x_s)

    def pass1(tau, c):
        advance(tau)
        return c

    lax.fori_loop(0, L, pass1, 0)

    for b in range(nb):
        def hop(c, carry, b=b):
            xr, xi = carry
            row = pl.ds(b * r + c, 1)
            st_s[row, :] = jnp.concatenate([xr, xi], axis=1)
            p = x_s[row, :]
            return (al_re * xr - al_im * xi + p[:, :ns], al_re * xi + al_im * xr + p[:, ns:])

        xr, xi = lax.fori_loop(0, r, hop, (carry_ref[b:b + 1, :ns], carry_ref[b:b + 1, ns:]))
        carry_ref[b:b + 1, :] = jnp.concatenate([xr, xi], axis=1)

    x_s[...] = st_s[...]

    def pass2(tau, c):
        advance(tau)
        y = jnp.dot(x_s[...].astype(BF16), c_ref[...], preferred_element_type=F32)
        sl = pl.ds(tau, r, stride=L)
        y_s[0, :, sl, :] = y[:, :lanes].reshape(nb, r, lanes)
        y_s[1, :, sl, :] = y[:, lanes:].reshape(nb, r, lanes)
        return c

    lax.fori_loop(0, L, pass2, 0)

    u = jnp.concatenate([u0_ref[...], u1_ref[...]], axis=-1).reshape(nb * tb, 2 * lanes)
    y = jnp.concatenate([y_s[0], y_s[1]], axis=-1).reshape(nb * tb, 2 * lanes)
    y = jax.nn.gelu(y + d_ref[...] * u)
    z = jnp.dot(y.astype(BF16), wg_ref[...], preferred_element_type=F32)
    out = _rms(z[:, :GROUP_W] * jax.nn.sigmoid(z[:, GROUP_W:]), mg_ref[...])
    o_ref[...] = out.reshape(nb, tb, GROUP_W)


def _s5(proj3, mats, d_skip, w_glu_bf16, merge_g, tb):
    bsz, t, _ = proj3.shape
    b_cat, c_cat, a_step, a_chunk = mats
    ns = S5_GROUPS * S5_STATE
    lanes = GROUP_W // 2
    rows = bsz * (tb // S5_CHUNK)
    half = lambda j: pl.BlockSpec((bsz, tb, lanes), lambda i, j=j: (0, i, 2 * COL_S5 + j))
    full = lambda a, b: pl.BlockSpec((a, b), lambda i: (0, 0))
    return pl.pallas_call(
        _s5_kernel,
        grid=(t // tb,),
        in_specs=[half(0), half(1), full(GROUP_W, 2 * ns), full(2 * ns, GROUP_W), full(2, ns),
                  full(2, ns), full(1, GROUP_W), full(GROUP_W, 2 * GROUP_W), full(1, GROUP_W)],
        out_specs=pl.BlockSpec((bsz, tb, GROUP_W), lambda i: (0, i, 0)),
        out_shape=jax.ShapeDtypeStruct((bsz, t, GROUP_W), F32),
        scratch_shapes=[pltpu.VMEM((bsz, 2 * ns), F32), pltpu.VMEM((rows, 2 * ns), F32),
                        pltpu.VMEM((rows, 2 * ns), F32), pltpu.VMEM((2, bsz, tb, lanes), F32)],
        compiler_params=_cparams("arbitrary"),
        name="s5",
    )(proj3, proj3, b_cat, c_cat, a_step, a_chunk, d_skip.reshape(1, -1), w_glu_bf16,
      merge_g.reshape(1, -1))


def _rwkv_kernel(*refs, has_vmix):
    if has_vmix:
        (r_ref, k_ref, v_ref, lo_ref, vf_ref, mu_ref, w0_ref, w2_ref, a0_ref, a2_ref, g2_ref,
         kk_ref, ka_ref, rk_ref, lnw_ref, lnb_ref, mg_ref, v0_ref, v1_ref, v2_ref,
         o_ref, h_ref, prev_ref, lw_s, a_s, b_s, k_s, r_s, v_s, y_s, m_s, n_s, p_s, z_s) = refs
    else:
        (r_ref, k_ref, v_ref, lo_ref, mu_ref, w0_ref, w2_ref, a0_ref, a2_ref, g2_ref,
         kk_ref, ka_ref, rk_ref, lnw_ref, lnb_ref, mg_ref,
         o_ref, vf_out_ref, h_ref, prev_ref, lw_s, a_s, b_s, k_s, r_s, v_s, y_s,
         m_s, n_s, p_s, z_s) = refs
    tb = r_ref.shape[0]
    ch = RWKV_CHUNK
    nh = GROUP_W // HEAD_W

    @pl.when(pl.program_id(1) == 0)
    def _():
        h_ref[...] = jnp.zeros_like(h_ref)
        prev_ref[...] = jnp.zeros_like(prev_ref)

    row0 = lax.broadcasted_iota(jnp.int32, (tb, 1), 0) == 0

    def mixed(ref, j):
        p = ref[...]
        lanes = slice(j * GROUP_W, (j + 1) * GROUP_W)
        sh = jnp.where(row0, prev_ref[0:1, lanes], pltpu.roll(p, 1, axis=0))
        prev_new = p[tb - 1:tb, :]
        return p + mu_ref[0:1, lanes] * (sh - p), prev_new

    r, pr_ = mixed(r_ref, 0)
    k, pk_ = mixed(k_ref, 1)
    v, pv_ = mixed(v_ref, 2)
    lo, pl_ = mixed(lo_ref, 3)
    for j, pn in enumerate((pr_, pk_, pv_, pl_)):
        prev_ref[0:1, j * GROUP_W:(j + 1) * GROUP_W] = pn

    bd = _block_diag_mask(GROUP_W, HEAD_W)
    bd_bf16 = bd.astype(BF16)

    w_raw = -_softplus(-(w0_ref[...] + _mm(jnp.tanh(lo), w2_ref[...]))) - 0.5
    lw_s[...] = -jnp.exp(w_raw)
    alr = jax.nn.sigmoid(a0_ref[...] + _mm(lo, a2_ref[...]))
    g = _mm(jax.nn.sigmoid(lo), g2_ref[...])
    kkr = k * kk_ref[...]
    kk = kkr / jnp.maximum(jnp.sqrt(_head_sum(kkr * kkr, bd_bf16)), 1e-12)
    k = k * (1.0 + (alr - 1.0) * ka_ref[...])
    if has_vmix:
        gate = jax.nn.sigmoid(v0_ref[...] + _mm(_mm(v, v1_ref[...]), v2_ref[...]))
        v = v + (vf_ref[...] - v) * gate
    else:
        vf_out_ref[...] = v
    a_s[...] = -kk
    b_s[...] = kk * alr
    k_s[...] = k
    r_s[...] = r
    v_s[...] = v

    n4 = nh * ch
    ri = lax.broadcasted_iota(jnp.int32, (n4, n4), 0)
    ci = lax.broadcasted_iota(jnp.int32, (n4, n4), 1)
    same_head = (ri // ch) == (ci // ch)
    strict = same_head & ((ri % ch) > (ci % ch))
    incl = same_head & ((ri % ch) >= (ci % ch))
    eye = (ri == ci).astype(F32)
    same_blk = {b: (ri // b) == (ci // b) for b in (8, 16, 32, 64)}
    hm = ((lax.broadcasted_iota(jnp.int32, (n4, GROUP_W), 0) // ch)
          == (lax.broadcasted_iota(jnp.int32, (n4, GROUP_W), 1) // HEAD_W))
    tri = (lax.broadcasted_iota(jnp.int32, (ch, ch), 0)
           >= lax.broadcasted_iota(jnp.int32, (ch, ch), 1)).astype(BF16)

    def stack(x):
        return jnp.where(hm, jnp.concatenate([x] * nh, axis=0), 0.0)

    def dot(x, y, dims=_NN):
        return lax.dot_general(x.astype(BF16), y.astype(BF16), (dims, ((), ())),
                               preferred_element_type=F32)

    chunks = range(tb // ch)
    pre = []
    for c in chunks:
        sl = slice(c * ch, (c + 1) * ch)
        lw = lw_s[sl, :]
        cl = _exact_lhs_mm(tri, lw)
        cl_end = cl[ch - 1:ch, :]
        e_in, e_ex = jnp.exp(cl), jnp.exp(cl - lw)
        e_neg, e_end = jnp.exp(-cl), jnp.exp(cl_end - cl)
        av, bv, kv, rv, vv = a_s[sl, :], b_s[sl, :], k_s[sl, :], r_s[sl, :], v_s[sl, :]
        pre.append(dict(
            at4=stack(av * e_ex), rt4=stack(rv * e_in), v4=stack(vv),
            bt4=jnp.concatenate([bv * e_neg] * nh, axis=0),
            kt4=jnp.concatenate([kv * e_neg] * nh, axis=0),
            bh4=stack(bv * e_end), kh4=stack(kv * e_end), g_end=jnp.exp(cl_end)))
    for d in pre:
        ar = jnp.concatenate([d["at4"], d["rt4"]], axis=0)
        sb = dot(ar, d["bt4"], _NT)
        sk = dot(ar, d["kt4"], _NT)
        d["l_ab"] = jnp.where(strict, sb[:n4], 0.0)
        d["l_ak"] = jnp.where(strict, sk[:n4], 0.0)
        d["l_rb"] = jnp.where(incl, sb[n4:], 0.0)
        d["l_rk"] = jnp.where(incl, sk[n4:], 0.0)
    for d in pre:
        nb8 = jnp.where(same_blk[8], d["l_ab"], 0.0)
        d["tinv"] = eye + nb8
        d["pw"] = dot(nb8, nb8)
    for d in pre:
        d["tinv"] = d["tinv"] + dot(d["tinv"], d["pw"])
        d["pw"] = dot(d["pw"], d["pw"])
    for d in pre:
        d["tinv"] = d["tinv"] + dot(d["tinv"], d["pw"])
    blk = 8
    while blk < ch:
        for d in pre:
            d["pw"] = dot(jnp.where(same_blk[2 * blk] & ~same_blk[blk], d["l_ab"], 0.0), d["tinv"])
        for d in pre:
            d["tinv"] = d["tinv"] + dot(d["tinv"], d["pw"])
        blk *= 2
    for d in pre:
        d["lakv"] = dot(d["l_ak"], d["v4"])
        d["lrkv"] = dot(d["l_rk"], d["v4"])
        d["khv"] = dot(d["kh4"], d["v4"], _TN)
    for d in pre:
        d["x12"] = dot(d["tinv"], jnp.concatenate([d["at4"], d["lakv"]], axis=1))
    for c, d in zip(chunks, pre):
        mn = dot(d["bh4"], d["x12"], _TN)
        pz = dot(d["l_rb"], d["x12"])
        m_s[c] = jnp.where(ri == ci, d["g_end"], 0.0) + mn[:, :GROUP_W]
        n_s[c] = mn[:, GROUP_W:] + d["khv"]
        p_s[c] = d["rt4"] + pz[:, :GROUP_W]
        z_s[c] = pz[:, GROUP_W:] + d["lrkv"]

    h = h_ref[...]
    for c in range(tb // ch):
        y4 = dot(p_s[c], h) + z_s[c]
        y = y4[0:ch]
        for j in range(1, nh):
            y = y + y4[j * ch:(j + 1) * ch]
        y_s[c * ch:(c + 1) * ch, :] = y
        h = _dg3(m_s[c], h, _NN) + n_s[c]
    h_ref[...] = h

    y = y_s[...]
    r, k, v = r_s[...], k_s[...], v_s[...]
    mean = _head_sum(y, bd_bf16) * (1.0 / HEAD_W)
    d = y - mean
    var = _head_sum(d * d, bd_bf16) * (1.0 / HEAD_W)
    y = d * lax.rsqrt(var + RWKV_GN_EPS) * lnw_ref[...] + lnb_ref[...]
    y = y + _head_sum(r * k * rk_ref[...], bd_bf16) * v
    o_ref[...] = _rms(y * g, mg_ref[...])


def _pad_rows(w, start, total=GROUP_W):
    out = jnp.zeros((total, w.shape[1]), F32).at[start:start + w.shape[0]].set(w)
    return out.astype(BF16)


def _rwkv(proj3, v_first, lyr, p, merge_g, tb):
    bsz, t, _ = proj3.shape
    has_vmix = v_first is not None
    row = lambda x: x.reshape(1, -1)
    w2 = _pad_rows(p["rwkv_w2"][lyr], 0)
    a2 = _pad_rows(p["rwkv_a2"][lyr], RWKV_W_LORA)
    g2 = _pad_rows(p["rwkv_g2"][lyr], RWKV_W_LORA + RWKV_A_LORA)
    args = [proj3, proj3, proj3, proj3]
    specs = [_col_spec(tb, COL_R), _col_spec(tb, COL_K), _col_spec(tb, COL_V), _col_spec(tb, COL_LORA)]
    if has_vmix:
        args.append(v_first)
        specs.append(pl.BlockSpec((None, tb, GROUP_W), lambda b, i: (b, i, 0)))
    args += [row(p["rwkv_mu"][lyr]), row(p["rwkv_w0"][lyr]), w2, row(p["rwkv_a0"][lyr]), a2, g2,
             row(p["rwkv_k_k"][lyr]), row(p["rwkv_k_a"][lyr]), row(p["rwkv_r_k"][lyr]),
             row(p["rwkv_ln_w"][lyr]), row(p["rwkv_ln_b"][lyr]), row(merge_g)]
    sq = _full_spec((GROUP_W, GROUP_W))
    specs += [_row_spec(4 * GROUP_W), _row_spec(), sq, _row_spec(), sq, sq,
              _row_spec(), _row_spec(), _row_spec(), _row_spec(), _row_spec(), _row_spec()]
    if has_vmix:
        v1 = jnp.zeros((GROUP_W, 128), F32).at[:, :RWKV_V_LORA].set(p["rwkv_v1"][lyr - 1]).astype(BF16)
        v2 = jnp.zeros((128, GROUP_W), F32).at[:RWKV_V_LORA].set(p["rwkv_v2"][lyr - 1]).astype(BF16)
        args += [row(p["rwkv_v0"][lyr - 1]), v1, v2]
        specs += [_row_spec(), _full_spec((GROUP_W, 128)), _full_spec((128, GROUP_W))]
    out_blk = pl.BlockSpec((None, tb, GROUP_W), lambda b, i: (b, i, 0))
    out_sds = jax.ShapeDtypeStruct((bsz, t, GROUP_W), F32)
    blk = pltpu.VMEM((tb, GROUP_W), F32)
    mats = pltpu.VMEM((tb // RWKV_CHUNK, GROUP_W, GROUP_W), F32)
    res = pl.pallas_call(
        functools.partial(_rwkv_kernel, has_vmix=has_vmix),
        grid=(bsz, t // tb),
        in_specs=specs,
        out_specs=out_blk if has_vmix else [out_blk, out_blk],
        out_shape=out_sds if has_vmix else [out_sds, out_sds],
        scratch_shapes=[pltpu.VMEM((GROUP_W, GROUP_W), F32), pltpu.VMEM((1, 4 * GROUP_W), F32),
                        blk, blk, blk, blk, blk, blk, blk, mats, mats, mats, mats],
        compiler_params=_cparams("parallel", "arbitrary"),
        name="rwkv7",
    )(*args)
    if has_vmix:
        return res, v_first
    return res[0], res[1]


def _bf16_bits(x):
    u = lax.bitcast_convert_type(x, jnp.uint32)
    r = u + jnp.uint32(0x7FFF) + ((u >> 16) & jnp.uint32(1))
    return r & jnp.uint32(0xFFFF0000)


def _merge_router_kernel(oa_ref, ob_ref, oc_ref, od_ref, h_ref, wo_ref, nw_ref, wr_ref, br_ref,
                         h1_ref, hn_ref, ids_ref, wts_ref, cnt_ref):
    mix = jnp.concatenate([oa_ref[...], ob_ref[...], oc_ref[...], od_ref[...]], axis=1)
    h1 = h_ref[...] + jnp.dot(mix.astype(BF16), wo_ref[...], preferred_element_type=F32)
    h1_ref[...] = h1
    hn = _rms(h1, nw_ref[...])
    half = hn.shape[1] // 2
    hn_ref[...] = _bf16_bits(hn[:, :half]) | (_bf16_bits(hn[:, half:]) >> 16)
    logits = _dg3(hn, wr_ref[...], _NN) + br_ref[...]
    lane = lax.broadcasted_iota(jnp.int32, logits.shape, 1)
    big = jnp.int32(ROUTER_LANES)
    is_c = (lane >= N_EXPERTS) & (lane < N_EXPERTS + MOE_GROUPS)
    cm = jnp.max(jnp.where(is_c, logits, NEG_BIG), axis=-1, keepdims=True)
    gsel = jnp.min(jnp.where(is_c & (logits == cm), lane, big), axis=-1, keepdims=True) - N_EXPERTS
    p_g = 1.0 / jnp.sum(jnp.where(is_c, jnp.exp(logits - cm), 0.0), axis=-1, keepdims=True)
    lo = gsel * MOE_PER_GROUP
    in_g = (lane >= lo) & (lane < lo + MOE_PER_GROUP)
    m1 = jnp.max(jnp.where(in_g, logits, NEG_BIG), axis=-1, keepdims=True)
    i1 = jnp.min(jnp.where(in_g & (logits == m1), lane, big), axis=-1, keepdims=True)
    in_g2 = in_g & (lane != i1)
    m2 = jnp.max(jnp.where(in_g2, logits, NEG_BIG), axis=-1, keepdims=True)
    i2 = jnp.min(jnp.where(in_g2 & (logits == m2), lane, big), axis=-1, keepdims=True)
    w1 = p_g / (1.0 + jnp.exp(m2 - m1))
    w2 = p_g - w1
    two = lax.broadcasted_iota(jnp.int32, ids_ref.shape, 1)
    ids_ref[...] = jnp.where(two == 0, i1, i2)
    wts_ref[...] = jnp.where(two == 0, w1, w2)

    @pl.when(pl.program_id(0) == 0)
    def _():
        cnt_ref[...] = jnp.zeros_like(cnt_ref)

    cnt_ref[...] += jnp.sum(((lane == i1) | (lane == i2)).astype(F32), axis=0, keepdims=True)


def _merge_router(outs, h2d, w_out_bf16, norm_w, w_router, b_router, tm):
    n, d = h2d.shape
    grp = pl.BlockSpec((tm, GROUP_W), lambda i: (i, 0))
    full = lambda a, b: pl.BlockSpec((a, b), lambda i: (0, 0))
    rowblk = lambda w: pl.BlockSpec((tm, w), lambda i: (i, 0))
    return pl.pallas_call(
        _merge_router_kernel,
        grid=(n // tm,),
        in_specs=[grp, grp, grp, grp, rowblk(d), full(d, d), full(1, d), full(d, ROUTER_LANES),
                  full(1, ROUTER_LANES)],
        out_specs=[rowblk(d), rowblk(d // 2), rowblk(MOE_TOPK), rowblk(MOE_TOPK), full(1, ROUTER_LANES)],
        out_shape=[jax.ShapeDtypeStruct((n, d), F32), jax.ShapeDtypeStruct((n, d // 2), jnp.uint32),
                   jax.ShapeDtypeStruct((n, MOE_TOPK), jnp.int32),
                   jax.ShapeDtypeStruct((n, MOE_TOPK), F32),
                   jax.ShapeDtypeStruct((1, ROUTER_LANES), F32)],
        compiler_params=_cparams("arbitrary"),
        name="merge_router",
    )(*[o.reshape(n, GROUP_W) for o in outs], h2d, w_out_bf16, norm_w.reshape(1, d), w_router, b_router)


def _row_copy(src_hbm, src_row, dst_ref, dst_row, sem):
    return pltpu.make_async_copy(src_hbm.at[pl.ds(src_row, 1), :], dst_ref.at[pl.ds(dst_row, 1), :], sem)


def _position_kernel(ids_ref, base_ref, pos_ref, run_ref):
    tm = ids_ref.shape[0]

    @pl.when(pl.program_id(0) == 0)
    def _():
        run_ref[...] = jnp.zeros_like(run_ref)

    i1, i2 = ids_ref[:, 0:1], ids_ref[:, 1:2]
    lane = lax.broadcasted_iota(jnp.int32, (tm, ROUTER_LANES), 1)
    hit = ((lane == i1) | (lane == i2)).astype(BF16)
    earlier = (lax.broadcasted_iota(jnp.int32, (tm, tm), 0)
               > lax.broadcasted_iota(jnp.int32, (tm, tm), 1)).astype(BF16)
    rank = jnp.dot(earlier, hit, preferred_element_type=F32) + run_ref[...]
    where_to = rank + base_ref[...]
    p1 = jnp.sum(jnp.where(lane == i1, where_to, 0.0), axis=-1, keepdims=True)
    p2 = jnp.sum(jnp.where(lane == i2, where_to, 0.0), axis=-1, keepdims=True)
    two = lax.broadcasted_iota(jnp.int32, pos_ref.shape, 1)
    pos_ref[...] = jnp.where(two == 0, p1, p2).astype(jnp.int32)
    run_ref[...] += jnp.sum(hit.astype(F32), axis=0, keepdims=True)


def _positions(ids, group_start, tm):
    n = ids.shape[0]
    return pl.pallas_call(
        _position_kernel,
        grid=(n // tm,),
        in_specs=[pl.BlockSpec((tm, MOE_TOPK), lambda i: (i, 0)),
                  pl.BlockSpec((1, ROUTER_LANES), lambda i: (0, 0))],
        out_specs=pl.BlockSpec((tm, MOE_TOPK), lambda i: (i, 0)),
        out_shape=jax.ShapeDtypeStruct((n, MOE_TOPK), jnp.int32),
        scratch_shapes=[pltpu.VMEM((1, ROUTER_LANES), F32)],
        compiler_params=_cparams("arbitrary"),
        name="moe_positions",
    )(ids, group_start)


def _dispatch_kernel(p0_ref, p1_ref, x_ref, xs_in_hbm, xs_hbm, sem):
    del xs_in_hbm
    n = x_ref.shape[0]

    def copies(i):
        return (pltpu.make_async_copy(x_ref.at[pl.ds(i, 1), :], xs_hbm.at[pl.ds(p0_ref[i], 1), :], sem),
                pltpu.make_async_copy(x_ref.at[pl.ds(i, 1), :], xs_hbm.at[pl.ds(p1_ref[i], 1), :], sem))

    def issue(i, c):
        for cp in copies(i):
            cp.start()
        return c

    def drain(i, c):
        for cp in copies(i):
            cp.wait()
        return c

    lax.fori_loop(0, n, issue, 0, unroll=8)
    lax.fori_loop(0, n, drain, 0, unroll=8)


def _dispatch(x, pos0, pos1, n_rows, chunk):
    n, w = x.shape
    smem = lambda: pl.BlockSpec((chunk,), lambda s: (s,), memory_space=pltpu.SMEM)
    return pl.pallas_call(
        _dispatch_kernel,
        grid=(n // chunk,),
        in_specs=[smem(), smem(), pl.BlockSpec((chunk, w), lambda s: (s, 0)),
                  pl.BlockSpec(memory_space=pl.ANY)],
        out_specs=pl.BlockSpec(memory_space=pl.ANY),
        out_shape=jax.ShapeDtypeStruct((n_rows, w), x.dtype),
        input_output_aliases={3: 0},
        scratch_shapes=[pltpu.SemaphoreType.DMA(())],
        compiler_params=_cparams("arbitrary"),
        name="moe_dispatch",
    )(pos0, pos1, x, jnp.zeros((n_rows, w), x.dtype))


def _expert_kernel(te_ref, nt_ref, xs_ref, wg_ref, wu_ref, wd_ref, o_ref):
    i = pl.program_id(0)

    @pl.when(i < nt_ref[0])
    def _():
        p = xs_ref[...]
        xa = lax.bitcast_convert_type(p & jnp.uint32(0xFFFF0000), F32).astype(BF16)
        xb = lax.bitcast_convert_type(p << 16, F32).astype(BF16)
        x = jnp.concatenate([xa, xb], axis=1)
        he = (_silu(jnp.dot(x, wg_ref[...], preferred_element_type=F32))
              * jnp.dot(x, wu_ref[...], preferred_element_type=F32))
        o_ref[...] = jnp.dot(he.astype(BF16), wd_ref[...], preferred_element_type=F32)

    @pl.when(i >= nt_ref[0])
    def _():
        o_ref[...] = jnp.zeros_like(o_ref)


def _experts(xs, tile_expert, n_tiles_used, wg, wu, wd, tm):
    p, half = xs.shape
    d = 2 * half
    grid_spec = pltpu.PrefetchScalarGridSpec(
        num_scalar_prefetch=2,
        grid=(p // tm,),
        in_specs=[pl.BlockSpec((tm, half), lambda i, te, nt: (i, 0)),
                  pl.BlockSpec((None, d, D_EXPERT), lambda i, te, nt: (te[i], 0, 0)),
                  pl.BlockSpec((None, d, D_EXPERT), lambda i, te, nt: (te[i], 0, 0)),
                  pl.BlockSpec((None, D_EXPERT, d), lambda i, te, nt: (te[i], 0, 0))],
        out_specs=pl.BlockSpec((tm, d), lambda i, te, nt: (i, 0)),
    )
    return pl.pallas_call(
        _expert_kernel,
        grid_spec=grid_spec,
        out_shape=jax.ShapeDtypeStruct((p, d), F32),
        compiler_params=_cparams("arbitrary"),
        name="moe_experts",
    )(tile_expert, n_tiles_used, xs, wg, wu, wd)


def _combine_kernel(i0_ref, i1_ref, h1_ref, w_ref, fw_ref, ys_hbm, o_ref, buf0, buf1, sem, *, final_norm):
    n = o_ref.shape[0]

    def issue(i, c):
        _row_copy(ys_hbm, i0_ref[i], buf0, i, sem).start()
        _row_copy(ys_hbm, i1_ref[i], buf1, i, sem).start()
        return c

    def drain(i, c):
        _row_copy(ys_hbm, 0, buf0, i, sem).wait()
        _row_copy(ys_hbm, 0, buf1, i, sem).wait()
        return c

    lax.fori_loop(0, n, issue, 0, unroll=8)
    lax.fori_loop(0, n, drain, 0, unroll=8)
    out = h1_ref[...] + w_ref[:, 0:1] * buf0[...] + w_ref[:, 1:2] * buf1[...]
    if final_norm:
        out = _rms(out, fw_ref[...])
    o_ref[...] = out


def _combine(ys, pos0, pos1, wts, h1, final_w, final_norm, chunk):
    n, d = h1.shape
    smem = lambda: pl.BlockSpec((chunk,), lambda s: (s,), memory_space=pltpu.SMEM)
    return pl.pallas_call(
        functools.partial(_combine_kernel, final_norm=final_norm),
        grid=(n // chunk,),
        in_specs=[smem(), smem(), pl.BlockSpec((chunk, d), lambda s: (s, 0)),
                  pl.BlockSpec((chunk, MOE_TOPK), lambda s: (s, 0)),
                  pl.BlockSpec((1, d), lambda s: (0, 0)), pl.BlockSpec(memory_space=pl.ANY)],
        out_specs=pl.BlockSpec((chunk, d), lambda s: (s, 0)),
        out_shape=jax.ShapeDtypeStruct((n, d), F32),
        scratch_shapes=[pltpu.VMEM((chunk, d), F32), pltpu.VMEM((chunk, d), F32),
                        pltpu.SemaphoreType.DMA(())],
        compiler_params=_cparams("arbitrary"),
        name="moe_combine",
    )(pos0, pos1, h1, wts, final_w.reshape(1, d), ys)


def _moe(hn_packed, ids, wts, counts, h1, wg, wu, wd, final_w, final_norm, tm):
    n = h1.shape[0]
    n_tiles = (n * MOE_TOPK) // tm + N_EXPERTS
    cnt = counts[0, :N_EXPERTS].astype(jnp.int32)
    padded = ((cnt + tm - 1) // tm) * tm
    ends = jnp.cumsum(padded)
    tile_start = jnp.arange(n_tiles, dtype=jnp.int32) * tm
    tile_expert = jnp.minimum(jnp.sum(ends[None, :] <= tile_start[:, None], axis=1),
                              N_EXPERTS - 1).astype(jnp.int32)
    n_tiles_used = (ends[-1] // tm).astype(jnp.int32).reshape(1)
    group_start = jnp.zeros((1, ROUTER_LANES), F32).at[0, :N_EXPERTS].set((ends - padded).astype(F32))
    pos = _positions(ids, group_start, _pick_div(n, 512))
    pos0, pos1 = pos[:, 0], pos[:, 1]
    xs = _dispatch(hn_packed, pos0, pos1, n_tiles * tm, _pick_div(n, 1024))
    ys = _experts(xs, tile_expert, n_tiles_used, wg, wu, wd, tm)
    return _combine(ys, pos0, pos1, wts, h1, final_w, final_norm, _pick_div(n, 1024))


def _pick_div(n, pref):
    while n % pref:
        pref //= 2
    return pref


def _pick(n, pref):
    return pref if n % pref == 0 else n


def kernel(x, norm1_w, w_in, hgrn_lb_logits, hgrn_norm_w, s5_lambda_re, s5_lambda_im, s5_log_dt, s5_b_re, s5_b_im, s5_c_re, s5_c_im, s5_d, s5_w_glu, rwkv_mu, rwkv_w0, rwkv_w2, rwkv_a0, rwkv_a2, rwkv_g2, rwkv_k_k, rwkv_k_a, rwkv_r_k, rwkv_v0, rwkv_v1, rwkv_v2, rwkv_ln_w, rwkv_ln_b, lru_conv_w, lru_conv_b, lru_wa, lru_ba, lru_wx, lru_bx, lru_lambda, merge_gain, w_out, norm2_w, moe_coarse_w, moe_coarse_b, moe_fine_w, moe_fine_b, moe_w_gate, moe_w_up, moe_w_down, final_norm_w):
    bsz, t, d = x.shape
    n = bsz * t
    depth = w_in.shape[0]
    rw = dict(rwkv_mu=rwkv_mu, rwkv_w0=rwkv_w0, rwkv_w2=rwkv_w2, rwkv_a0=rwkv_a0, rwkv_a2=rwkv_a2,
              rwkv_g2=rwkv_g2, rwkv_k_k=rwkv_k_k, rwkv_k_a=rwkv_k_a, rwkv_r_k=rwkv_r_k,
              rwkv_v0=rwkv_v0, rwkv_v1=rwkv_v1, rwkv_v2=rwkv_v2, rwkv_ln_w=rwkv_ln_w,
              rwkv_ln_b=rwkv_ln_b)
    lb_all = jnp.cumsum(jax.nn.softmax(hgrn_lb_logits.astype(F32), axis=0), axis=0)
    lb_all = lb_all - lb_all[:1]

    tm_proj = _pick(n, 512)
    tm_moe = 512
    tb_hgrn = _pick(t, 256)
    tb_lru = _pick(t, 256)
    tb_rwkv = _pick(t, 256)
    tb_s5 = _pick(t, 512)

    h = x.reshape(n, d)
    v_first = None
    for l in range(depth):
        mg = merge_gain[l].reshape(4, GROUP_W)
        proj = _in_proj(h, norm1_w[l], w_in[l].astype(BF16), tm_proj)
        proj3 = proj.reshape(bsz, t, -1)
        o_a = _hgrn(proj3, lb_all[l], hgrn_norm_w[l], mg[0], tb_hgrn)
        mats = _s5_matrices(s5_lambda_re[l], s5_lambda_im[l], s5_log_dt[l], s5_b_re[l], s5_b_im[l],
                            s5_c_re[l], s5_c_im[l])
        o_b = _s5(proj3, mats, s5_d[l], s5_w_glu[l].astype(BF16), mg[1], tb_s5)
        o_c, v_first = _rwkv(proj3, v_first, l, rw, mg[2], tb_rwkv)
        o_d = _lru(proj3, lru_conv_w[l], lru_conv_b[l], _block_diag_weight(lru_wa[l]), lru_ba[l],
                   _block_diag_weight(lru_wx[l]), lru_bx[l], lru_lambda[l], mg[3], tb_lru)
        w_router = jnp.concatenate(
            [moe_fine_w[l].transpose(1, 0, 2).reshape(d, N_EXPERTS), moe_coarse_w[l],
             jnp.zeros((d, ROUTER_LANES - N_EXPERTS - MOE_GROUPS), F32)], axis=1)
        b_router = jnp.concatenate(
            [moe_fine_b[l].reshape(N_EXPERTS), moe_coarse_b[l],
             jnp.zeros((ROUTER_LANES - N_EXPERTS - MOE_GROUPS,), F32)]).reshape(1, ROUTER_LANES)
        h1, hn, ids, wts, counts = _merge_router((o_a, o_b, o_c, o_d), h, w_out[l].astype(BF16),
                                                 norm2_w[l], w_router, b_router, tm_proj)
        h = _moe(hn, ids, wts, counts, h1, moe_w_gate[l].astype(BF16), moe_w_up[l].astype(BF16),
                 moe_w_down[l].astype(BF16), final_norm_w, l == depth - 1, tm_moe)
    return h.reshape(bsz, t, d)
```

```python
import functools
import math

import jax
import jax.numpy as jnp
from jax import lax
from jax.experimental import pallas as pl
from jax.experimental.pallas import tpu as pltpu

F32 = jnp.float32
BF16 = jnp.bfloat16

D_MODEL = 1024
GROUP_W = 256
RMS_EPS = 1e-6
HEAD_W = 64
HGRN_CHUNK = 16
S5_CH = 16
S5_GROUPS = GROUP_W // S5_CH
S5_STATE = 64
S5_CHUNK = 16
RWKV_CHUNK = 64
RWKV_GN_EPS = 64e-5
RWKV_W_LORA = 64
RWKV_A_LORA = 64
RWKV_G_LORA = 128
RWKV_V_LORA = 32
LRU_CONV = 4
LRU_C = 8.0
MOE_GROUPS = 4
MOE_PER_GROUP = 8
N_EXPERTS = MOE_GROUPS * MOE_PER_GROUP
MOE_TOPK = 2
D_EXPERT = 512
ROUTER_LANES = 128
NEG_BIG = -1e30
VMEM_LIMIT = 56 * 1024 * 1024

COL_HQ, COL_HF, COL_HI, COL_HG, COL_S5, COL_R, COL_K, COL_V, COL_LORA, COL_LG, COL_LX = range(11)


def _cparams(*sem):
    return pltpu.CompilerParams(dimension_semantics=sem, vmem_limit_bytes=VMEM_LIMIT)


def _mm(a, b):
    return jnp.dot(a.astype(BF16), b.astype(BF16), preferred_element_type=F32)


def _split2(x):
    hi = x.astype(BF16)
    lo = (x - hi.astype(F32)).astype(BF16)
    return hi, lo


def _dg3(a, b, dims):
    ah, al = _split2(a)
    bh, bl = _split2(b)
    d = lambda x, y: lax.dot_general(x, y, (dims, ((), ())), preferred_element_type=F32)
    return d(ah, bh) + d(ah, bl) + d(al, bh)


_NN = ((1,), (0,))
_NT = ((1,), (1,))
_TN = ((0,), (0,))


def _exact_lhs_mm(m_bf16, x):
    h1 = x.astype(BF16)
    r1 = x - h1.astype(F32)
    h2 = r1.astype(BF16)
    h3 = (r1 - h2.astype(F32)).astype(BF16)
    d = lambda y: jnp.dot(m_bf16, y, preferred_element_type=F32)
    return d(h1) + d(h2) + d(h3)


def _head_sum(x, bd_bf16):
    return jnp.dot(x.astype(BF16), bd_bf16, preferred_element_type=F32)


def _block_diag_mask(n, blk):
    r = lax.broadcasted_iota(jnp.int32, (n, n), 0) // blk
    c = lax.broadcasted_iota(jnp.int32, (n, n), 1) // blk
    return r == c


def _rms(x, w):
    return x * lax.rsqrt(jnp.mean(x * x, axis=-1, keepdims=True) + RMS_EPS) * w


def _silu(x):
    return x * jax.nn.sigmoid(x)


def _softplus(x):
    return jnp.maximum(x, 0.0) + jnp.log(1.0 + jnp.exp(-jnp.abs(x)))


def _in_proj_kernel(x_ref, nw_ref, w_ref, o_ref):
    y = _rms(x_ref[...], nw_ref[...])
    o_ref[...] = jnp.dot(y.astype(BF16), w_ref[...], preferred_element_type=F32)


def _in_proj(h2d, norm_w, w_bf16, tm):
    n, d = h2d.shape
    d_in = w_bf16.shape[1]
    return pl.pallas_call(
        _in_proj_kernel,
        grid=(n // tm,),
        in_specs=[pl.BlockSpec((tm, d), lambda i: (i, 0)),
                  pl.BlockSpec((1, d), lambda i: (0, 0)),
                  pl.BlockSpec((d, d_in), lambda i: (0, 0))],
        out_specs=pl.BlockSpec((tm, d_in), lambda i: (i, 0)),
        out_shape=jax.ShapeDtypeStruct((n, d_in), F32),
        compiler_params=_cparams("parallel"),
        name="in_proj",
    )(h2d, norm_w.reshape(1, d), w_bf16)


def _col_spec(tb, col):
    return pl.BlockSpec((None, tb, GROUP_W), lambda b, t: (b, t, col))


def _row_spec(width=GROUP_W):
    return pl.BlockSpec((1, width), lambda b, t: (0, 0))


def _full_spec(shape):
    return pl.BlockSpec(shape, lambda b, t: (0,) * len(shape))


def _hgrn_kernel(q_ref, f_ref, i_ref, g_ref, lb_ref, nw_ref, mg_ref, o_ref,
                 st_ref, q_s, k_s, v_s, lf_s, o_s, *, reset):
    ch = HGRN_CHUNK
    tb = q_ref.shape[0]
    if reset:
        st_ref[...] = jnp.zeros_like(st_ref)
        return

    lb = lb_ref[...]
    fx = f_ref[...]
    x1 = jnp.log(lb)
    x2 = jnp.log(1.0 - lb) - _softplus(-fx)
    m = jnp.maximum(x1, x2)
    lf_s[...] = m + jnp.log(jnp.exp(x1 - m) + jnp.exp(x2 - m))
    q_s[...] = _silu(q_ref[...])
    k_s[...] = (1.0 - lb) * jax.nn.sigmoid(-fx)
    v_s[...] = _silu(i_ref[...])
    yield

    bd = _block_diag_mask(GROUP_W, HEAD_W)
    bd_bf16 = bd.astype(BF16)
    tri = (lax.broadcasted_iota(jnp.int32, (ch, ch), 0)
           >= lax.broadcasted_iota(jnp.int32, (ch, ch), 1))
    tri_bf16 = tri.astype(BF16)
    tri3 = (lax.broadcasted_iota(jnp.int32, (ch, ch, 1), 0)
            <= lax.broadcasted_iota(jnp.int32, (ch, ch, 1), 1))

    st = st_ref[...]
    for c in range(tb // ch):
        sl = slice(c * ch, (c + 1) * ch)
        qc, kc, vc = q_s[sl, :], k_s[sl, :], v_s[sl, :]
        b = _exact_lhs_mm(tri_bf16, lf_s[sl, :])
        rel = b[None, :, :] - b[:, None, :]
        dec = jnp.exp(jnp.where(tri3, rel, NEG_BIG))
        p = (qc[None, :, :] * kc[:, None, :]) * dec
        sc = jnp.dot(p.reshape(ch * ch, GROUP_W).astype(BF16), bd_bf16,
                     preferred_element_type=F32).reshape(ch, ch, GROUP_W)
        o_intra = jnp.sum(sc * vc[:, None, :], axis=0)
        o_inter = lax.dot_general((qc * jnp.exp(b)).astype(BF16), st.astype(BF16),
                                  (_NT, ((), ())), preferred_element_type=F32)
        b_end = b[ch - 1:ch, :]
        kh = kc * jnp.exp(b_end - b)
        upd = lax.dot_general(vc.astype(BF16), kh.astype(BF16), (_TN, ((), ())),
                              preferred_element_type=F32)
        st = st * jnp.exp(b_end) + jnp.where(bd, upd, 0.0)
        o_s[sl, :] = o_intra + o_inter
        yield
    st_ref[...] = st

    o = o_s[...]
    ms = _head_sum(o * o, bd_bf16) * (1.0 / HEAD_W)
    o = o * lax.rsqrt(ms + RMS_EPS) * nw_ref[...] * _silu(g_ref[...])
    o_ref[...] = _rms(o, mg_ref[...])


def _out_blk(tb):
    return pl.BlockSpec((None, tb, GROUP_W), lambda b, i: (b, i, 0))


def _hgrn_part(proj3, lb, norm_w, merge_g, tb):
    bsz, t, _ = proj3.shape
    blk = pltpu.VMEM((tb, GROUP_W), F32)
    return dict(
        body=_hgrn_kernel, stages=tb // HGRN_CHUNK + 2,
        args=[proj3, proj3, proj3, proj3, lb.reshape(1, -1), norm_w.reshape(1, -1), merge_g.reshape(1, -1)],
        in_specs=[_col_spec(tb, COL_HQ), _col_spec(tb, COL_HF), _col_spec(tb, COL_HI),
                  _col_spec(tb, COL_HG), _row_spec(), _row_spec(), _row_spec()],
        out_specs=[_out_blk(tb)],
        out_shape=[jax.ShapeDtypeStruct((bsz, t, GROUP_W), F32)],
        scratch=[pltpu.VMEM((GROUP_W, GROUP_W), F32), blk, blk, blk, blk, blk])


def _mixer_kernel(*refs, parts):
    groups, i = [], 0
    for kind in range(3):
        for body, counts in parts:
            groups.append(refs[i:i + counts[kind]])
            i += counts[kind]
    k = len(parts)
    per_part = [groups[j] + groups[k + j] + groups[2 * k + j] for j in range(k)]

    @pl.when(pl.program_id(1) == 0)
    def _():
        for (body, _), r in zip(parts, per_part):
            for _step in body(*r, reset=True):
                pass

    runs = [[body(*r, reset=False), 0, counts[3]] for (body, counts), r in zip(parts, per_part)]
    while runs:
        run = min(runs, key=lambda x: x[1] / x[2])
        try:
            next(run[0])
            run[1] += 1
        except StopIteration:
            runs.remove(run)


def _mixers(parts, bsz, n_tblocks, name):
    cat = lambda key: [x for p in parts for x in p[key]]
    light = tuple((p["body"], (len(p["args"]), len(p["out_specs"]), len(p["scratch"]), p["stages"]))
                  for p in parts)
    return pl.pallas_call(
        functools.partial(_mixer_kernel, parts=light),
        grid=(bsz, n_tblocks),
        in_specs=cat("in_specs"),
        out_specs=cat("out_specs"),
        out_shape=cat("out_shape"),
        scratch_shapes=cat("scratch"),
        compiler_params=_cparams("parallel", "arbitrary"),
        name=name,
    )(*cat("args"))


def _lru_kernel(xg_ref, xr_ref, cw_ref, cb_ref, wa_ref, ba_ref, wx_ref, bx_ref, lam_ref, mg_ref,
                o_ref, buf_ref, h_ref, *, reset):
    tb = xr_ref.shape[0]
    if reset:
        buf_ref[0:8, :] = jnp.zeros((8, GROUP_W), F32)
        h_ref[...] = jnp.zeros_like(h_ref)
        return

    xr = xr_ref[...]
    buf_ref[8:8 + tb, :] = xr
    xc = cb_ref[...] + jnp.zeros_like(xr)
    for j in range(LRU_CONV):
        xc = xc + cw_ref[j:j + 1, :] * buf_ref[pl.ds(8 - (LRU_CONV - 1) + j, tb), :]
    buf_ref[0:8, :] = xr[tb - 8:tb, :]

    r = jax.nn.sigmoid(jnp.dot(xc.astype(BF16), wa_ref[...], preferred_element_type=F32) + ba_ref[...])
    gi = jax.nn.sigmoid(jnp.dot(xc.astype(BF16), wx_ref[...], preferred_element_type=F32) + bx_ref[...])
    log_a = -LRU_C * r * _softplus(-lam_ref[...])
    a = jnp.exp(log_a)
    x = jnp.sqrt(1.0 - jnp.exp(2.0 * log_a)) * (gi * xc)
    yield

    rows = lax.broadcasted_iota(jnp.int32, (tb, 1), 0)
    k = 1
    while k < tb:
        keep = rows >= k
        x = x + jnp.where(keep, a * pltpu.roll(x, k, axis=0), 0.0)
        a = jnp.where(keep, a * pltpu.roll(a, k, axis=0), a)
        k *= 2
        yield
    h = x + a * h_ref[...]
    h_ref[...] = h[tb - 1:tb, :]
    o_ref[...] = _rms(jax.nn.gelu(xg_ref[...]) * h, mg_ref[...])


def _lru_part(proj3, conv_w, conv_b, wa_bd, ba, wx_bd, bx, lam, merge_g, tb):
    bsz, t, _ = proj3.shape
    return dict(
        body=_lru_kernel, stages=tb.bit_length() + 1,
        args=[proj3, proj3, conv_w, conv_b.reshape(1, -1), wa_bd, ba.reshape(1, -1), wx_bd,
              bx.reshape(1, -1), lam.reshape(1, -1), merge_g.reshape(1, -1)],
        in_specs=[_col_spec(tb, COL_LG), _col_spec(tb, COL_LX), _full_spec((LRU_CONV, GROUP_W)),
                  _row_spec(), _full_spec((GROUP_W, GROUP_W)), _row_spec(),
                  _full_spec((GROUP_W, GROUP_W)), _row_spec(), _row_spec(), _row_spec()],
        out_specs=[_out_blk(tb)],
        out_shape=[jax.ShapeDtypeStruct((bsz, t, GROUP_W), F32)],
        scratch=[pltpu.VMEM((tb + 8, GROUP_W), F32), pltpu.VMEM((1, GROUP_W), F32)])


def _block_diag_weight(w):
    h, n, _ = w.shape
    eye = jnp.eye(h, dtype=w.dtype)
    return jnp.einsum('hij,hg->higj', w, eye).reshape(h * n, h * n).astype(BF16)


def _s5_matrices(lam_re, lam_im, log_dt, b_re, b_im, c_re, c_im):
    L, G, P, C = S5_CHUNK, S5_GROUPS, S5_STATE, S5_CH
    lr, li = lam_re.astype(F32), lam_im.astype(F32)
    dt = jnp.exp(log_dt.astype(F32))[:, None]
    mag = jnp.exp(lr * dt)
    a_re, a_im = mag * jnp.cos(li * dt), mag * jnp.sin(li * dt)
    den = lr * lr + li * li
    kap_re = ((a_re - 1.0) * lr + a_im * li) / den
    kap_im = (a_im * lr - (a_re - 1.0) * li) / den
    br, bi = b_re.astype(F32), b_im.astype(F32)
    bb_re = kap_re[..., None] * br - kap_im[..., None] * bi
    bb_im = kap_re[..., None] * bi + kap_im[..., None] * br
    cr, ci = c_re.astype(F32), c_im.astype(F32)
    eye = jnp.eye(G, dtype=F32)
    b_cat = jnp.concatenate(
        [jnp.einsum('gpc,gh->gchp', bb_re, eye).reshape(G * C, G * P),
         jnp.einsum('gpc,gh->gchp', bb_im, eye).reshape(G * C, G * P)], axis=1)
    c_cat = jnp.concatenate(
        [jnp.einsum('gcp,gh->gphc', cr, eye).reshape(G * P, G * C),
         -jnp.einsum('gcp,gh->gphc', ci, eye).reshape(G * P, G * C)], axis=0)
    mag_l = jnp.exp(L * lr * dt)
    a_step = jnp.stack([a_re.reshape(G * P), a_im.reshape(G * P)])
    a_chunk = jnp.stack([(mag_l * jnp.cos(L * li * dt)).reshape(G * P),
                         (mag_l * jnp.sin(L * li * dt)).reshape(G * P)])
    return b_cat.astype(BF16), c_cat.astype(BF16), a_step, a_chunk


def _s5_kernel(u0_ref, u1_ref, b_ref, c_ref, a_ref, al_ref, d_ref, wg_ref, mg_ref, o_ref,
               carry_ref, x_s, st_s, y_s):
    nb, tb, lanes = u0_ref.shape
    L = S5_CHUNK
    r = tb // L
    rows = nb * r
    ns = S5_GROUPS * S5_STATE

    @pl.when(pl.program_id(0) == 0)
    def _():
        carry_ref[...] = jnp.zeros_like(carry_ref)

    a_re, a_im = a_ref[0:1, :], a_ref[1:2, :]
    al_re, al_im = al_ref[0:1, :], al_ref[1:2, :]

    def advance(tau):
        sl = pl.ds(tau, r, stride=L)
        u = jnp.concatenate([u0_ref[:, sl, :], u1_ref[:, sl, :]], axis=-1).reshape(rows, 2 * lanes)
        bu = jnp.dot(u.astype(BF16), b_ref[...], preferred_element_type=F32)
        xr, xi = x_s[:, :ns], x_s[:, ns:]
        x_s[:, :ns] = a_re * xr - a_im * xi + bu[:, :ns]
        x_s[:, ns:] = a_re * xi + a_im * xr + bu[:, ns:]

    x_s[...] = jnp.zeros_like(x_s)

    def pass1(tau, c):
        advance(tau)
        return c

    lax.fori_loop(0, L, pass1, 0)

    for b in range(nb):
        def hop(c, carry, b=b):
            xr, xi = carry
            row = pl.ds(b * r + c, 1)
            st_s[row, :] = jnp.concatenate([xr, xi], axis=1)
            p = x_s[row, :]
            return (al_re * xr - al_im * xi + p[:, :ns], al_re * xi + al_im * xr + p[:, ns:])

        xr, xi = lax.fori_loop(0, r, hop, (carry_ref[b:b + 1, :ns], carry_ref[b:b + 1, ns:]))
        carry_ref[b:b + 1, :] = jnp.concatenate([xr, xi], axis=1)

    x_s[...] = st_s[...]

    def pass2(tau, c):
        advance(tau)
        y = jnp.dot(x_s[...].astype(BF16), c_ref[...], preferred_element_type=F32)
        sl = pl.ds(tau, r, stride=L)
        y_s[0, :, sl, :] = y[:, :lanes].reshape(nb, r, lanes)
        y_s[1, :, sl, :] = y[:, lanes:].reshape(nb, r, lanes)
        return c

    lax.fori_loop(0, L, pass2, 0)

    u = jnp.concatenate([u0_ref[...], u1_ref[...]], axis=-1).reshape(nb * tb, 2 * lanes)
    y = jnp.concatenate([y_s[0], y_s[1]], axis=-1).reshape(nb * tb, 2 * lanes)
    y = jax.nn.gelu(y + d_ref[...] * u)
    z = jnp.dot(y.astype(BF16), wg_ref[...], preferred_element_type=F32)
    out = _rms(z[:, :GROUP_W] * jax.nn.sigmoid(z[:, GROUP_W:]), mg_ref[...])
    o_ref[...] = out.reshape(nb, tb, GROUP_W)


def _s5(proj3, mats, d_skip, w_glu_bf16, merge_g, tb):
    bsz, t, _ = proj3.shape
    b_cat, c_cat, a_step, a_chunk = mats
    ns = S5_GROUPS * S5_STATE
    lanes = GROUP_W // 2
    rows = bsz * (tb // S5_CHUNK)
    half = lambda j: pl.BlockSpec((bsz, tb, lanes), lambda i, j=j: (0, i, 2 * COL_S5 + j))
    full = lambda a, b: pl.BlockSpec((a, b), lambda i: (0, 0))
    return pl.pallas_call(
        _s5_kernel,
        grid=(t // tb,),
        in_specs=[half(0), half(1), full(GROUP_W, 2 * ns), full(2 * ns, GROUP_W), full(2, ns),
                  full(2, ns), full(1, GROUP_W), full(GROUP_W, 2 * GROUP_W), full(1, GROUP_W)],
        out_specs=pl.BlockSpec((bsz, tb, GROUP_W), lambda i: (0, i, 0)),
        out_shape=jax.ShapeDtypeStruct((bsz, t, GROUP_W), F32),
        scratch_shapes=[pltpu.VMEM((bsz, 2 * ns), F32), pltpu.VMEM((rows, 2 * ns), F32),
                        pltpu.VMEM((rows, 2 * ns), F32), pltpu.VMEM((2, bsz, tb, lanes), F32)],
        compiler_params=_cparams("arbitrary"),
        name="s5",
    )(proj3, proj3, b_cat, c_cat, a_step, a_chunk, d_skip.reshape(1, -1), w_glu_bf16,
      merge_g.reshape(1, -1))


def _rwkv_kernel(*refs, has_vmix, reset):
    if has_vmix:
        (r_ref, k_ref, v_ref, lo_ref, vf_ref, mu_ref, w0_ref, w2_ref, a0_ref, a2_ref, g2_ref,
         kk_ref, ka_ref, rk_ref, lnw_ref, lnb_ref, mg_ref, v0_ref, v1_ref, v2_ref,
         o_ref, h_ref, prev_ref, lw_s, a_s, b_s, k_s, r_s, v_s, y_s, m_s, n_s, p_s, z_s) = refs
    else:
        (r_ref, k_ref, v_ref, lo_ref, mu_ref, w0_ref, w2_ref, a0_ref, a2_ref, g2_ref,
         kk_ref, ka_ref, rk_ref, lnw_ref, lnb_ref, mg_ref,
         o_ref, vf_out_ref, h_ref, prev_ref, lw_s, a_s, b_s, k_s, r_s, v_s, y_s,
         m_s, n_s, p_s, z_s) = refs
    tb = r_ref.shape[0]
    ch = RWKV_CHUNK
    nh = GROUP_W // HEAD_W
    if reset:
        h_ref[...] = jnp.zeros_like(h_ref)
        prev_ref[...] = jnp.zeros_like(prev_ref)
        return

    row0 = lax.broadcasted_iota(jnp.int32, (tb, 1), 0) == 0

    def mixed(ref, j):
        p = ref[...]
        lanes = slice(j * GROUP_W, (j + 1) * GROUP_W)
        sh = jnp.where(row0, prev_ref[0:1, lanes], pltpu.roll(p, 1, axis=0))
        prev_new = p[tb - 1:tb, :]
        return p + mu_ref[0:1, lanes] * (sh - p), prev_new

    r, pr_ = mixed(r_ref, 0)
    k, pk_ = mixed(k_ref, 1)
    v, pv_ = mixed(v_ref, 2)
    lo, pl_ = mixed(lo_ref, 3)
    for j, pn in enumerate((pr_, pk_, pv_, pl_)):
        prev_ref[0:1, j * GROUP_W:(j + 1) * GROUP_W] = pn

    bd = _block_diag_mask(GROUP_W, HEAD_W)
    bd_bf16 = bd.astype(BF16)

    w_raw = -_softplus(-(w0_ref[...] + _mm(jnp.tanh(lo), w2_ref[...]))) - 0.5
    lw_s[...] = -jnp.exp(w_raw)
    alr = jax.nn.sigmoid(a0_ref[...] + _mm(lo, a2_ref[...]))
    g = _mm(jax.nn.sigmoid(lo), g2_ref[...])
    kkr = k * kk_ref[...]
    kk = kkr / jnp.maximum(jnp.sqrt(_head_sum(kkr * kkr, bd_bf16)), 1e-12)
    k = k * (1.0 + (alr - 1.0) * ka_ref[...])
    if has_vmix:
        gate = jax.nn.sigmoid(v0_ref[...] + _mm(_mm(v, v1_ref[...]), v2_ref[...]))
        v = v + (vf_ref[...] - v) * gate
    else:
        vf_out_ref[...] = v
    a_s[...] = -kk
    b_s[...] = kk * alr
    k_s[...] = k
    r_s[...] = r
    v_s[...] = v
    yield

    n4 = nh * ch
    ri = lax.broadcasted_iota(jnp.int32, (n4, n4), 0)
    ci = lax.broadcasted_iota(jnp.int32, (n4, n4), 1)
    same_head = (ri // ch) == (ci // ch)
    strict = same_head & ((ri % ch) > (ci % ch))
    incl = same_head & ((ri % ch) >= (ci % ch))
    eye = (ri == ci).astype(F32)
    same_blk = {b: (ri // b) == (ci // b) for b in (8, 16, 32, 64)}
    hm = ((lax.broadcasted_iota(jnp.int32, (n4, GROUP_W), 0) // ch)
          == (lax.broadcasted_iota(jnp.int32, (n4, GROUP_W), 1) // HEAD_W))
    tri = (lax.broadcasted_iota(jnp.int32, (ch, ch), 0)
           >= lax.broadcasted_iota(jnp.int32, (ch, ch), 1)).astype(BF16)

    def stack(x):
        return jnp.where(hm, jnp.concatenate([x] * nh, axis=0), 0.0)

    def dot(x, y, dims=_NN):
        return lax.dot_general(x.astype(BF16), y.astype(BF16), (dims, ((), ())),
                               preferred_element_type=F32)

    chunks = range(tb // ch)
    pre = []
    for c in chunks:
        sl = slice(c * ch, (c + 1) * ch)
        lw = lw_s[sl, :]
        cl = _exact_lhs_mm(tri, lw)
        cl_end = cl[ch - 1:ch, :]
        e_in, e_ex = jnp.exp(cl), jnp.exp(cl - lw)
        e_neg, e_end = jnp.exp(-cl), jnp.exp(cl_end - cl)
        av, bv, kv, rv, vv = a_s[sl, :], b_s[sl, :], k_s[sl, :], r_s[sl, :], v_s[sl, :]
        pre.append(dict(
            at4=stack(av * e_ex), rt4=stack(rv * e_in), v4=stack(vv),
            bt4=jnp.concatenate([bv * e_neg] * nh, axis=0),
            kt4=jnp.concatenate([kv * e_neg] * nh, axis=0),
            bh4=stack(bv * e_end), kh4=stack(kv * e_end), g_end=jnp.exp(cl_end)))
        yield
    for d in pre:
        ar = jnp.concatenate([d["at4"], d["rt4"]], axis=0)
        sb = dot(ar, d["bt4"], _NT)
        sk = dot(ar, d["kt4"], _NT)
        d["l_ab"] = jnp.where(strict, sb[:n4], 0.0)
        d["l_ak"] = jnp.where(strict, sk[:n4], 0.0)
        d["l_rb"] = jnp.where(incl, sb[n4:], 0.0)
        d["l_rk"] = jnp.where(incl, sk[n4:], 0.0)
        yield
    for d in pre:
        nb8 = jnp.where(same_blk[8], d["l_ab"], 0.0)
        d["tinv"] = eye + nb8
        d["pw"] = dot(nb8, nb8)
        yield
    for d in pre:
        d["tinv"] = d["tinv"] + dot(d["tinv"], d["pw"])
        d["pw"] = dot(d["pw"], d["pw"])
        yield
    for d in pre:
        d["tinv"] = d["tinv"] + dot(d["tinv"], d["pw"])
        yield
    blk = 8
    while blk < ch:
        for d in pre:
            d["pw"] = dot(jnp.where(same_blk[2 * blk] & ~same_blk[blk], d["l_ab"], 0.0), d["tinv"])
            yield
        for d in pre:
            d["tinv"] = d["tinv"] + dot(d["tinv"], d["pw"])
            yield
        blk *= 2
    for d in pre:
        d["lakv"] = dot(d["l_ak"], d["v4"])
        d["lrkv"] = dot(d["l_rk"], d["v4"])
        d["khv"] = dot(d["kh4"], d["v4"], _TN)
        yield
    for d in pre:
        d["x12"] = dot(d["tinv"], jnp.concatenate([d["at4"], d["lakv"]], axis=1))
        yield
    for c, d in zip(chunks, pre):
        mn = dot(d["bh4"], d["x12"], _TN)
        pz = dot(d["l_rb"], d["x12"])
        m_s[c] = jnp.where(ri == ci, d["g_end"], 0.0) + mn[:, :GROUP_W]
        n_s[c] = mn[:, GROUP_W:] + d["khv"]
        p_s[c] = d["rt4"] + pz[:, :GROUP_W]
        z_s[c] = pz[:, GROUP_W:] + d["lrkv"]
        yield

    h = h_ref[...]
    for c in range(tb // ch):
        y4 = dot(p_s[c], h) + z_s[c]
        y = y4[0:ch]
        for j in range(1, nh):
            y = y + y4[j * ch:(j + 1) * ch]
        y_s[c * ch:(c + 1) * ch, :] = y
        h = _dg3(m_s[c], h, _NN) + n_s[c]
        yield
    h_ref[...] = h

    y = y_s[...]
    r, k, v = r_s[...], k_s[...], v_s[...]
    mean = _head_sum(y, bd_bf16) * (1.0 / HEAD_W)
    d = y - mean
    var = _head_sum(d * d, bd_bf16) * (1.0 / HEAD_W)
    y = d * lax.rsqrt(var + RWKV_GN_EPS) * lnw_ref[...] + lnb_ref[...]
    y = y + _head_sum(r * k * rk_ref[...], bd_bf16) * v
    o_ref[...] = _rms(y * g, mg_ref[...])


def _pad_rows(w, start, total=GROUP_W):
    out = jnp.zeros((total, w.shape[1]), F32).at[start:start + w.shape[0]].set(w)
    return out.astype(BF16)


def _rwkv_part(proj3, v_first, lyr, p, merge_g, tb):
    bsz, t, _ = proj3.shape
    has_vmix = v_first is not None
    row = lambda x: x.reshape(1, -1)
    w2 = _pad_rows(p["rwkv_w2"][lyr], 0)
    a2 = _pad_rows(p["rwkv_a2"][lyr], RWKV_W_LORA)
    g2 = _pad_rows(p["rwkv_g2"][lyr], RWKV_W_LORA + RWKV_A_LORA)
    args = [proj3, proj3, proj3, proj3]
    specs = [_col_spec(tb, COL_R), _col_spec(tb, COL_K), _col_spec(tb, COL_V), _col_spec(tb, COL_LORA)]
    if has_vmix:
        args.append(v_first)
        specs.append(pl.BlockSpec((None, tb, GROUP_W), lambda b, i: (b, i, 0)))
    args += [row(p["rwkv_mu"][lyr]), row(p["rwkv_w0"][lyr]), w2, row(p["rwkv_a0"][lyr]), a2, g2,
             row(p["rwkv_k_k"][lyr]), row(p["rwkv_k_a"][lyr]), row(p["rwkv_r_k"][lyr]),
             row(p["rwkv_ln_w"][lyr]), row(p["rwkv_ln_b"][lyr]), row(merge_g)]
    sq = _full_spec((GROUP_W, GROUP_W))
    specs += [_row_spec(4 * GROUP_W), _row_spec(), sq, _row_spec(), sq, sq,
              _row_spec(), _row_spec(), _row_spec(), _row_spec(), _row_spec(), _row_spec()]
    if has_vmix:
        v1 = jnp.zeros((GROUP_W, 128), F32).at[:, :RWKV_V_LORA].set(p["rwkv_v1"][lyr - 1]).astype(BF16)
        v2 = jnp.zeros((128, GROUP_W), F32).at[:RWKV_V_LORA].set(p["rwkv_v2"][lyr - 1]).astype(BF16)
        args += [row(p["rwkv_v0"][lyr - 1]), v1, v2]
        specs += [_row_spec(), _full_spec((GROUP_W, 128)), _full_spec((128, GROUP_W))]
    out_sds = jax.ShapeDtypeStruct((bsz, t, GROUP_W), F32)
    blk = pltpu.VMEM((tb, GROUP_W), F32)
    mats = pltpu.VMEM((tb // RWKV_CHUNK, GROUP_W, GROUP_W), F32)
    n_out = 1 if has_vmix else 2
    return dict(
        body=functools.partial(_rwkv_kernel, has_vmix=has_vmix), stages=15 * (tb // RWKV_CHUNK) + 2,
        args=args, in_specs=specs,
        out_specs=[_out_blk(tb)] * n_out, out_shape=[out_sds] * n_out,
        scratch=[pltpu.VMEM((GROUP_W, GROUP_W), F32), pltpu.VMEM((1, 4 * GROUP_W), F32),
                 blk, blk, blk, blk, blk, blk, blk, mats, mats, mats, mats])


def _bf16_bits(x):
    u = lax.bitcast_convert_type(x, jnp.uint32)
    r = u + jnp.uint32(0x7FFF) + ((u >> 16) & jnp.uint32(1))
    return r & jnp.uint32(0xFFFF0000)


def _merge_router_kernel(oa_ref, ob_ref, oc_ref, od_ref, h_ref, wo_ref, nw_ref, wr_ref, br_ref,
                         h1_ref, hn_ref, ids_ref, wts_ref, cnt_ref):
    mix = jnp.concatenate([oa_ref[...], ob_ref[...], oc_ref[...], od_ref[...]], axis=1)
    h1 = h_ref[...] + jnp.dot(mix.astype(BF16), wo_ref[...], preferred_element_type=F32)
    h1_ref[...] = h1
    hn = _rms(h1, nw_ref[...])
    half = hn.shape[1] // 2
    hn_ref[...] = _bf16_bits(hn[:, :half]) | (_bf16_bits(hn[:, half:]) >> 16)
    logits = _dg3(hn, wr_ref[...], _NN) + br_ref[...]
    lane = lax.broadcasted_iota(jnp.int32, logits.shape, 1)
    big = jnp.int32(ROUTER_LANES)
    is_c = (lane >= N_EXPERTS) & (lane < N_EXPERTS + MOE_GROUPS)
    cm = jnp.max(jnp.where(is_c, logits, NEG_BIG), axis=-1, keepdims=True)
    gsel = jnp.min(jnp.where(is_c & (logits == cm), lane, big), axis=-1, keepdims=True) - N_EXPERTS
    p_g = 1.0 / jnp.sum(jnp.where(is_c, jnp.exp(logits - cm), 0.0), axis=-1, keepdims=True)
    lo = gsel * MOE_PER_GROUP
    in_g = (lane >= lo) & (lane < lo + MOE_PER_GROUP)
    m1 = jnp.max(jnp.where(in_g, logits, NEG_BIG), axis=-1, keepdims=True)
    i1 = jnp.min(jnp.where(in_g & (logits == m1), lane, big), axis=-1, keepdims=True)
    in_g2 = in_g & (lane != i1)
    m2 = jnp.max(jnp.where(in_g2, logits, NEG_BIG), axis=-1, keepdims=True)
    i2 = jnp.min(jnp.where(in_g2 & (logits == m2), lane, big), axis=-1, keepdims=True)
    w1 = p_g / (1.0 + jnp.exp(m2 - m1))
    w2 = p_g - w1
    two = lax.broadcasted_iota(jnp.int32, ids_ref.shape, 1)
    ids_ref[...] = jnp.where(two == 0, i1, i2)
    wts_ref[...] = jnp.where(two == 0, w1, w2)

    @pl.when(pl.program_id(0) == 0)
    def _():
        cnt_ref[...] = jnp.zeros_like(cnt_ref)

    cnt_ref[...] += jnp.sum(((lane == i1) | (lane == i2)).astype(F32), axis=0, keepdims=True)


def _merge_router(outs, h2d, w_out_bf16, norm_w, w_router, b_router, tm):
    n, d = h2d.shape
    grp = pl.BlockSpec((tm, GROUP_W), lambda i: (i, 0))
    full = lambda a, b: pl.BlockSpec((a, b), lambda i: (0, 0))
    rowblk = lambda w: pl.BlockSpec((tm, w), lambda i: (i, 0))
    return pl.pallas_call(
        _merge_router_kernel,
        grid=(n // tm,),
        in_specs=[grp, grp, grp, grp, rowblk(d), full(d, d), full(1, d), full(d, ROUTER_LANES),
                  full(1, ROUTER_LANES)],
        out_specs=[rowblk(d), rowblk(d // 2), rowblk(MOE_TOPK), rowblk(MOE_TOPK), full(1, ROUTER_LANES)],
        out_shape=[jax.ShapeDtypeStruct((n, d), F32), jax.ShapeDtypeStruct((n, d // 2), jnp.uint32),
                   jax.ShapeDtypeStruct((n, MOE_TOPK), jnp.int32),
                   jax.ShapeDtypeStruct((n, MOE_TOPK), F32),
                   jax.ShapeDtypeStruct((1, ROUTER_LANES), F32)],
        compiler_params=_cparams("arbitrary"),
        name="merge_router",
    )(*[o.reshape(n, GROUP_W) for o in outs], h2d, w_out_bf16, norm_w.reshape(1, d), w_router, b_router)


def _row_copy(src_hbm, src_row, dst_ref, dst_row, sem):
    return pltpu.make_async_copy(src_hbm.at[pl.ds(src_row, 1), :], dst_ref.at[pl.ds(dst_row, 1), :], sem)


def _position_kernel(ids_ref, base_ref, pos_ref, run_ref):
    tm = ids_ref.shape[0]

    @pl.when(pl.program_id(0) == 0)
    def _():
        run_ref[...] = jnp.zeros_like(run_ref)

    i1, i2 = ids_ref[:, 0:1], ids_ref[:, 1:2]
    lane = lax.broadcasted_iota(jnp.int32, (tm, ROUTER_LANES), 1)
    hit = ((lane == i1) | (lane == i2)).astype(BF16)
    earlier = (lax.broadcasted_iota(jnp.int32, (tm, tm), 0)
               > lax.broadcasted_iota(jnp.int32, (tm, tm), 1)).astype(BF16)
    rank = jnp.dot(earlier, hit, preferred_element_type=F32) + run_ref[...]
    where_to = rank + base_ref[...]
    p1 = jnp.sum(jnp.where(lane == i1, where_to, 0.0), axis=-1, keepdims=True)
    p2 = jnp.sum(jnp.where(lane == i2, where_to, 0.0), axis=-1, keepdims=True)
    two = lax.broadcasted_iota(jnp.int32, pos_ref.shape, 1)
    pos_ref[...] = jnp.where(two == 0, p1, p2).astype(jnp.int32)
    run_ref[...] += jnp.sum(hit.astype(F32), axis=0, keepdims=True)


def _positions(ids, group_start, tm):
    n = ids.shape[0]
    return pl.pallas_call(
        _position_kernel,
        grid=(n // tm,),
        in_specs=[pl.BlockSpec((tm, MOE_TOPK), lambda i: (i, 0)),
                  pl.BlockSpec((1, ROUTER_LANES), lambda i: (0, 0))],
        out_specs=pl.BlockSpec((tm, MOE_TOPK), lambda i: (i, 0)),
        out_shape=jax.ShapeDtypeStruct((n, MOE_TOPK), jnp.int32),
        scratch_shapes=[pltpu.VMEM((1, ROUTER_LANES), F32)],
        compiler_params=_cparams("arbitrary"),
        name="moe_positions",
    )(ids, group_start)


def _dispatch_kernel(p0_ref, p1_ref, x_ref, xs_in_hbm, xs_hbm, sem):
    del xs_in_hbm
    n = x_ref.shape[0]

    def copies(i):
        return (pltpu.make_async_copy(x_ref.at[pl.ds(i, 1), :], xs_hbm.at[pl.ds(p0_ref[i], 1), :], sem),
                pltpu.make_async_copy(x_ref.at[pl.ds(i, 1), :], xs_hbm.at[pl.ds(p1_ref[i], 1), :], sem))

    def issue(i, c):
        for cp in copies(i):
            cp.start()
        return c

    def drain(i, c):
        for cp in copies(i):
            cp.wait()
        return c

    lax.fori_loop(0, n, issue, 0, unroll=8)
    lax.fori_loop(0, n, drain, 0, unroll=8)


def _dispatch(x, pos0, pos1, n_rows, chunk):
    n, w = x.shape
    smem = lambda: pl.BlockSpec((chunk,), lambda s: (s,), memory_space=pltpu.SMEM)
    return pl.pallas_call(
        _dispatch_kernel,
        grid=(n // chunk,),
        in_specs=[smem(), smem(), pl.BlockSpec((chunk, w), lambda s: (s, 0)),
                  pl.BlockSpec(memory_space=pl.ANY)],
        out_specs=pl.BlockSpec(memory_space=pl.ANY),
        out_shape=jax.ShapeDtypeStruct((n_rows, w), x.dtype),
        input_output_aliases={3: 0},
        scratch_shapes=[pltpu.SemaphoreType.DMA(())],
        compiler_params=_cparams("arbitrary"),
        name="moe_dispatch",
    )(pos0, pos1, x, jnp.zeros((n_rows, w), x.dtype))


def _expert_kernel(te_ref, nt_ref, xs_ref, wg_ref, wu_ref, wd_ref, o_ref, wg_s, wu_s, wd_s):
    i = pl.program_id(0)

    @pl.when((i == 0) | (te_ref[i] != te_ref[jnp.maximum(i - 1, 0)]))
    def _():
        wg_s[...] = wg_ref[...].astype(BF16)
        wu_s[...] = wu_ref[...].astype(BF16)
        wd_s[...] = wd_ref[...].astype(BF16)

    @pl.when(i < nt_ref[0])
    def _():
        p = xs_ref[...]
        xa = lax.bitcast_convert_type(p & jnp.uint32(0xFFFF0000), F32).astype(BF16)
        xb = lax.bitcast_convert_type(p << 16, F32).astype(BF16)
        x = jnp.concatenate([xa, xb], axis=1)
        he = (_silu(jnp.dot(x, wg_s[...], preferred_element_type=F32))
              * jnp.dot(x, wu_s[...], preferred_element_type=F32))
        o_ref[...] = jnp.dot(he.astype(BF16), wd_s[...], preferred_element_type=F32)

    @pl.when(i >= nt_ref[0])
    def _():
        o_ref[...] = jnp.zeros_like(o_ref)


def _experts(xs, tile_expert, n_tiles_used, wg, wu, wd, layer, tm):
    p, half = xs.shape
    d = 2 * half
    pick = lambda i, te, nt: (layer, te[i], 0, 0)
    grid_spec = pltpu.PrefetchScalarGridSpec(
        num_scalar_prefetch=2,
        grid=(p // tm,),
        in_specs=[pl.BlockSpec((tm, half), lambda i, te, nt: (i, 0)),
                  pl.BlockSpec((None, None, d, D_EXPERT), pick),
                  pl.BlockSpec((None, None, d, D_EXPERT), pick),
                  pl.BlockSpec((None, None, D_EXPERT, d), pick)],
        out_specs=pl.BlockSpec((tm, d), lambda i, te, nt: (i, 0)),
        scratch_shapes=[pltpu.VMEM((d, D_EXPERT), BF16), pltpu.VMEM((d, D_EXPERT), BF16),
                        pltpu.VMEM((D_EXPERT, d), BF16)],
    )
    return pl.pallas_call(
        _expert_kernel,
        grid_spec=grid_spec,
        out_shape=jax.ShapeDtypeStruct((p, d), F32),
        compiler_params=_cparams("arbitrary"),
        name="moe_experts",
    )(tile_expert, n_tiles_used, xs, wg, wu, wd)


def _combine_kernel(i0_ref, i1_ref, h1_ref, w_ref, fw_ref, ys_hbm, o_ref, buf0, buf1, sem, *, final_norm):
    n = o_ref.shape[0]

    def issue(i, c):
        _row_copy(ys_hbm, i0_ref[i], buf0, i, sem).start()
        _row_copy(ys_hbm, i1_ref[i], buf1, i, sem).start()
        return c

    def drain(i, c):
        _row_copy(ys_hbm, 0, buf0, i, sem).wait()
        _row_copy(ys_hbm, 0, buf1, i, sem).wait()
        return c

    lax.fori_loop(0, n, issue, 0, unroll=8)
    lax.fori_loop(0, n, drain, 0, unroll=8)
    out = h1_ref[...] + w_ref[:, 0:1] * buf0[...] + w_ref[:, 1:2] * buf1[...]
    if final_norm:
        out = _rms(out, fw_ref[...])
    o_ref[...] = out


def _combine(ys, pos0, pos1, wts, h1, final_w, final_norm, chunk):
    n, d = h1.shape
    smem = lambda: pl.BlockSpec((chunk,), lambda s: (s,), memory_space=pltpu.SMEM)
    return pl.pallas_call(
        functools.partial(_combine_kernel, final_norm=final_norm),
        grid=(n // chunk,),
        in_specs=[smem(), smem(), pl.BlockSpec((chunk, d), lambda s: (s, 0)),
                  pl.BlockSpec((chunk, MOE_TOPK), lambda s: (s, 0)),
                  pl.BlockSpec((1, d), lambda s: (0, 0)), pl.BlockSpec(memory_space=pl.ANY)],
        out_specs=pl.BlockSpec((chunk, d), lambda s: (s, 0)),
        out_shape=jax.ShapeDtypeStruct((n, d), F32),
        scratch_shapes=[pltpu.VMEM((chunk, d), F32), pltpu.VMEM((chunk, d), F32),
                        pltpu.SemaphoreType.DMA(())],
        compiler_params=_cparams("arbitrary"),
        name="moe_combine",
    )(pos0, pos1, h1, wts, final_w.reshape(1, d), ys)


def _moe(hn_packed, ids, wts, counts, h1, wg, wu, wd, layer, final_w, final_norm, tm):
    n = h1.shape[0]
    n_tiles = (n * MOE_TOPK) // tm + N_EXPERTS
    cnt = counts[0, :N_EXPERTS].astype(jnp.int32)
    padded = ((cnt + tm - 1) // tm) * tm
    ends = jnp.cumsum(padded)
    tile_start = jnp.arange(n_tiles, dtype=jnp.int32) * tm
    tile_expert = jnp.minimum(jnp.sum(ends[None, :] <= tile_start[:, None], axis=1),
                              N_EXPERTS - 1).astype(jnp.int32)
    n_tiles_used = (ends[-1] // tm).astype(jnp.int32).reshape(1)
    group_start = jnp.zeros((1, ROUTER_LANES), F32).at[0, :N_EXPERTS].set((ends - padded).astype(F32))
    pos = _positions(ids, group_start, _pick_div(n, 512))
    pos0, pos1 = pos[:, 0], pos[:, 1]
    xs = _dispatch(hn_packed, pos0, pos1, n_tiles * tm, _pick_div(n, 1024))
    ys = _experts(xs, tile_expert, n_tiles_used, wg, wu, wd, layer, tm)
    return _combine(ys, pos0, pos1, wts, h1, final_w, final_norm, _pick_div(n, 1024))


def _pick_div(n, pref):
    while n % pref:
        pref //= 2
    return pref


def _pick(n, pref):
    return pref if n % pref == 0 else n


def kernel(x, norm1_w, w_in, hgrn_lb_logits, hgrn_norm_w, s5_lambda_re, s5_lambda_im, s5_log_dt, s5_b_re, s5_b_im, s5_c_re, s5_c_im, s5_d, s5_w_glu, rwkv_mu, rwkv_w0, rwkv_w2, rwkv_a0, rwkv_a2, rwkv_g2, rwkv_k_k, rwkv_k_a, rwkv_r_k, rwkv_v0, rwkv_v1, rwkv_v2, rwkv_ln_w, rwkv_ln_b, lru_conv_w, lru_conv_b, lru_wa, lru_ba, lru_wx, lru_bx, lru_lambda, merge_gain, w_out, norm2_w, moe_coarse_w, moe_coarse_b, moe_fine_w, moe_fine_b, moe_w_gate, moe_w_up, moe_w_down, final_norm_w):
    bsz, t, d = x.shape
    n = bsz * t
    depth = w_in.shape[0]
    rw = dict(rwkv_mu=rwkv_mu, rwkv_w0=rwkv_w0, rwkv_w2=rwkv_w2, rwkv_a0=rwkv_a0, rwkv_a2=rwkv_a2,
              rwkv_g2=rwkv_g2, rwkv_k_k=rwkv_k_k, rwkv_k_a=rwkv_k_a, rwkv_r_k=rwkv_r_k,
              rwkv_v0=rwkv_v0, rwkv_v1=rwkv_v1, rwkv_v2=rwkv_v2, rwkv_ln_w=rwkv_ln_w,
              rwkv_ln_b=rwkv_ln_b)
    lb_all = jnp.cumsum(jax.nn.softmax(hgrn_lb_logits.astype(F32), axis=0), axis=0)
    lb_all = lb_all - lb_all[:1]

    tm_proj = _pick(n, 512)
    tm_moe = 512
    tb_mix = _pick(t, 256)
    tb_s5 = _pick(t, 512)

    h = x.reshape(n, d)
    v_first = None
    for l in range(depth):
        mg = merge_gain[l].reshape(4, GROUP_W)
        proj = _in_proj(h, norm1_w[l], w_in[l].astype(BF16), tm_proj)
        proj3 = proj.reshape(bsz, t, -1)
        mats = _s5_matrices(s5_lambda_re[l], s5_lambda_im[l], s5_log_dt[l], s5_b_re[l], s5_b_im[l],
                            s5_c_re[l], s5_c_im[l])
        o_b = _s5(proj3, mats, s5_d[l], s5_w_glu[l].astype(BF16), mg[1], tb_s5)
        parts = [_rwkv_part(proj3, v_first, l, rw, mg[2], tb_mix),
                 _hgrn_part(proj3, lb_all[l], hgrn_norm_w[l], mg[0], tb_mix),
                 _lru_part(proj3, lru_conv_w[l], lru_conv_b[l], _block_diag_weight(lru_wa[l]),
                           lru_ba[l], _block_diag_weight(lru_wx[l]), lru_bx[l], lru_lambda[l],
                           mg[3], tb_mix)]
        outs = _mixers(parts, bsz, t // tb_mix, "mixers")
        o_c, o_a, o_d = outs[0], outs[-2], outs[-1]
        if v_first is None:
            v_first = outs[1]
        w_router = jnp.concatenate(
            [moe_fine_w[l].transpose(1, 0, 2).reshape(d, N_EXPERTS), moe_coarse_w[l],
             jnp.zeros((d, ROUTER_LANES - N_EXPERTS - MOE_GROUPS), F32)], axis=1)
        b_router = jnp.concatenate(
            [moe_fine_b[l].reshape(N_EXPERTS), moe_coarse_b[l],
             jnp.zeros((ROUTER_LANES - N_EXPERTS - MOE_GROUPS,), F32)]).reshape(1, ROUTER_LANES)
        h1, hn, ids, wts, counts = _merge_router((o_a, o_b, o_c, o_d), h, w_out[l].astype(BF16),
                                                 norm2_w[l], w_router, b_router, tm_proj)
        h = _moe(hn, ids, wts, counts, h1, moe_w_gate, moe_w_up, moe_w_down, l,
                 final_norm_w, l == depth - 1, tm_moe)
    return h.reshape(bsz, t, d)
```

```python
import functools
import math

import jax
import jax.numpy as jnp
from jax import lax
from jax.experimental import pallas as pl
from jax.experimental.pallas import tpu as pltpu

F32 = jnp.float32
BF16 = jnp.bfloat16

D_MODEL = 1024
GROUP_W = 256
RMS_EPS = 1e-6
HEAD_W = 64
HGRN_CHUNK = 16
S5_CH = 16
S5_GROUPS = GROUP_W // S5_CH
S5_STATE = 64
S5_CHUNK = 16
S5_STEPS = 4
RWKV_CHUNK = 64
RWKV_GN_EPS = 64e-5
RWKV_W_LORA = 64
RWKV_A_LORA = 64
RWKV_G_LORA = 128
RWKV_V_LORA = 32
LRU_CONV = 4
LRU_C = 8.0
MOE_GROUPS = 4
MOE_PER_GROUP = 8
N_EXPERTS = MOE_GROUPS * MOE_PER_GROUP
MOE_TOPK = 2
D_EXPERT = 512
ROUTER_LANES = 128
NEG_BIG = -1e30
VMEM_LIMIT = 56 * 1024 * 1024

COL_HQ, COL_HF, COL_HI, COL_HG, COL_S5, COL_R, COL_K, COL_V, COL_LORA, COL_LG, COL_LX = range(11)


def _cparams(*sem):
    return pltpu.CompilerParams(dimension_semantics=sem, vmem_limit_bytes=VMEM_LIMIT)


def _mm(a, b):
    return jnp.dot(a.astype(BF16), b.astype(BF16), preferred_element_type=F32)


def _split2(x):
    hi = x.astype(BF16)
    lo = (x - hi.astype(F32)).astype(BF16)
    return hi, lo


def _dg3(a, b, dims):
    ah, al = _split2(a)
    bh, bl = _split2(b)
    d = lambda x, y: lax.dot_general(x, y, (dims, ((), ())), preferred_element_type=F32)
    return d(ah, bh) + d(ah, bl) + d(al, bh)


_NN = ((1,), (0,))
_NT = ((1,), (1,))
_TN = ((0,), (0,))


def _exact_lhs_mm(m_bf16, x):
    h1 = x.astype(BF16)
    r1 = x - h1.astype(F32)
    h2 = r1.astype(BF16)
    h3 = (r1 - h2.astype(F32)).astype(BF16)
    d = lambda y: jnp.dot(m_bf16, y, preferred_element_type=F32)
    return d(h1) + d(h2) + d(h3)


def _head_sum(x, bd_bf16):
    return jnp.dot(x.astype(BF16), bd_bf16, preferred_element_type=F32)


def _block_diag_mask(n, blk):
    r = lax.broadcasted_iota(jnp.int32, (n, n), 0) // blk
    c = lax.broadcasted_iota(jnp.int32, (n, n), 1) // blk
    return r == c


def _rms(x, w):
    return x * lax.rsqrt(jnp.mean(x * x, axis=-1, keepdims=True) + RMS_EPS) * w


def _silu(x):
    return x * jax.nn.sigmoid(x)


def _softplus(x):
    return jnp.maximum(x, 0.0) + jnp.log(1.0 + jnp.exp(-jnp.abs(x)))


def _in_proj_kernel(x_ref, nw_ref, w_ref, o_ref):
    y = _rms(x_ref[...], nw_ref[...])
    o_ref[...] = jnp.dot(y.astype(BF16), w_ref[...], preferred_element_type=F32)


def _in_proj(h2d, norm_w, w_bf16, tm):
    n, d = h2d.shape
    d_in = w_bf16.shape[1]
    return pl.pallas_call(
        _in_proj_kernel,
        grid=(n // tm,),
        in_specs=[pl.BlockSpec((tm, d), lambda i: (i, 0)),
                  pl.BlockSpec((1, d), lambda i: (0, 0)),
                  pl.BlockSpec((d, d_in), lambda i: (0, 0))],
        out_specs=pl.BlockSpec((tm, d_in), lambda i: (i, 0)),
        out_shape=jax.ShapeDtypeStruct((n, d_in), F32),
        compiler_params=_cparams("parallel"),
        name="in_proj",
    )(h2d, norm_w.reshape(1, d), w_bf16)


def _col_spec(tb, col):
    return pl.BlockSpec((None, tb, GROUP_W), lambda b, t: (b, t, col))


def _row_spec(width=GROUP_W):
    return pl.BlockSpec((1, width), lambda b, t: (0, 0))


def _full_spec(shape):
    return pl.BlockSpec(shape, lambda b, t: (0,) * len(shape))


def _hgrn_kernel(q_ref, f_ref, i_ref, g_ref, lb_ref, nw_ref, mg_ref, o_ref,
                 st_ref, q_s, k_s, v_s, lf_s, o_s, *, reset):
    ch = HGRN_CHUNK
    tb = q_ref.shape[0]
    if reset:
        st_ref[...] = jnp.zeros_like(st_ref)
        return

    lb = lb_ref[...]
    fx = f_ref[...]
    x1 = jnp.log(lb)
    x2 = jnp.log(1.0 - lb) - _softplus(-fx)
    m = jnp.maximum(x1, x2)
    lf_s[...] = m + jnp.log(jnp.exp(x1 - m) + jnp.exp(x2 - m))
    q_s[...] = _silu(q_ref[...])
    k_s[...] = (1.0 - lb) * jax.nn.sigmoid(-fx)
    v_s[...] = _silu(i_ref[...])
    yield

    bd = _block_diag_mask(GROUP_W, HEAD_W)
    bd_bf16 = bd.astype(BF16)
    tri = (lax.broadcasted_iota(jnp.int32, (ch, ch), 0)
           >= lax.broadcasted_iota(jnp.int32, (ch, ch), 1))
    tri_bf16 = tri.astype(BF16)
    tri3 = (lax.broadcasted_iota(jnp.int32, (ch, ch, 1), 0)
            <= lax.broadcasted_iota(jnp.int32, (ch, ch, 1), 1))

    st = st_ref[...]
    for c in range(tb // ch):
        sl = slice(c * ch, (c + 1) * ch)
        qc, kc, vc = q_s[sl, :], k_s[sl, :], v_s[sl, :]
        b = _exact_lhs_mm(tri_bf16, lf_s[sl, :])
        rel = b[None, :, :] - b[:, None, :]
        dec = jnp.exp(jnp.where(tri3, rel, NEG_BIG))
        p = (qc[None, :, :] * kc[:, None, :]) * dec
        sc = jnp.dot(p.reshape(ch * ch, GROUP_W).astype(BF16), bd_bf16,
                     preferred_element_type=F32).reshape(ch, ch, GROUP_W)
        o_intra = jnp.sum(sc * vc[:, None, :], axis=0)
        o_inter = lax.dot_general((qc * jnp.exp(b)).astype(BF16), st.astype(BF16),
                                  (_NT, ((), ())), preferred_element_type=F32)
        b_end = b[ch - 1:ch, :]
        kh = kc * jnp.exp(b_end - b)
        upd = lax.dot_general(vc.astype(BF16), kh.astype(BF16), (_TN, ((), ())),
                              preferred_element_type=F32)
        st = st * jnp.exp(b_end) + jnp.where(bd, upd, 0.0)
        o_s[sl, :] = o_intra + o_inter
        yield
    st_ref[...] = st

    o = o_s[...]
    ms = _head_sum(o * o, bd_bf16) * (1.0 / HEAD_W)
    o = o * lax.rsqrt(ms + RMS_EPS) * nw_ref[...] * _silu(g_ref[...])
    o_ref[...] = _rms(o, mg_ref[...])


def _out_blk(tb):
    return pl.BlockSpec((None, tb, GROUP_W), lambda b, i: (b, i, 0))


def _hgrn_part(proj3, lb, norm_w, merge_g, tb):
    bsz, t, _ = proj3.shape
    blk = pltpu.VMEM((tb, GROUP_W), F32)
    return dict(
        body=_hgrn_kernel, stages=tb // HGRN_CHUNK + 2,
        args=[proj3, proj3, proj3, proj3, lb.reshape(1, -1), norm_w.reshape(1, -1), merge_g.reshape(1, -1)],
        in_specs=[_col_spec(tb, COL_HQ), _col_spec(tb, COL_HF), _col_spec(tb, COL_HI),
                  _col_spec(tb, COL_HG), _row_spec(), _row_spec(), _row_spec()],
        out_specs=[_out_blk(tb)],
        out_shape=[jax.ShapeDtypeStruct((bsz, t, GROUP_W), F32)],
        scratch=[pltpu.VMEM((GROUP_W, GROUP_W), F32), blk, blk, blk, blk, blk])


def _mixer_kernel(*refs, parts):
    groups, i = [], 0
    for kind in range(3):
        for body, counts in parts:
            groups.append(refs[i:i + counts[kind]])
            i += counts[kind]
    k = len(parts)
    per_part = [groups[j] + groups[k + j] + groups[2 * k + j] for j in range(k)]

    @pl.when(pl.program_id(1) == 0)
    def _():
        for (body, _), r in zip(parts, per_part):
            for _step in body(*r, reset=True):
                pass

    runs = [[body(*r, reset=False), 0, counts[3]] for (body, counts), r in zip(parts, per_part)]
    while runs:
        run = min(runs, key=lambda x: x[1] / x[2])
        try:
            next(run[0])
            run[1] += 1
        except StopIteration:
            runs.remove(run)


def _mixers(parts, bsz, n_tblocks, name):
    cat = lambda key: [x for p in parts for x in p[key]]
    light = tuple((p["body"], (len(p["args"]), len(p["out_specs"]), len(p["scratch"]), p["stages"]))
                  for p in parts)
    return pl.pallas_call(
        functools.partial(_mixer_kernel, parts=light),
        grid=(bsz, n_tblocks),
        in_specs=cat("in_specs"),
        out_specs=cat("out_specs"),
        out_shape=cat("out_shape"),
        scratch_shapes=cat("scratch"),
        compiler_params=_cparams("parallel", "arbitrary"),
        name=name,
    )(*cat("args"))


def _lru_kernel(xg_ref, xr_ref, cw_ref, cb_ref, wa_ref, ba_ref, wx_ref, bx_ref, lam_ref, mg_ref,
                o_ref, buf_ref, h_ref, *, reset):
    tb = xr_ref.shape[0]
    if reset:
        buf_ref[0:8, :] = jnp.zeros((8, GROUP_W), F32)
        h_ref[...] = jnp.zeros_like(h_ref)
        return

    xr = xr_ref[...]
    buf_ref[8:8 + tb, :] = xr
    xc = cb_ref[...] + jnp.zeros_like(xr)
    for j in range(LRU_CONV):
        xc = xc + cw_ref[j:j + 1, :] * buf_ref[pl.ds(8 - (LRU_CONV - 1) + j, tb), :]
    buf_ref[0:8, :] = xr[tb - 8:tb, :]

    r = jax.nn.sigmoid(jnp.dot(xc.astype(BF16), wa_ref[...], preferred_element_type=F32) + ba_ref[...])
    gi = jax.nn.sigmoid(jnp.dot(xc.astype(BF16), wx_ref[...], preferred_element_type=F32) + bx_ref[...])
    log_a = -LRU_C * r * _softplus(-lam_ref[...])
    a = jnp.exp(log_a)
    x = jnp.sqrt(1.0 - jnp.exp(2.0 * log_a)) * (gi * xc)
    yield

    rows = lax.broadcasted_iota(jnp.int32, (tb, 1), 0)
    k = 1
    while k < tb:
        keep = rows >= k
        x = x + jnp.where(keep, a * pltpu.roll(x, k, axis=0), 0.0)
        a = jnp.where(keep, a * pltpu.roll(a, k, axis=0), a)
        k *= 2
        yield
    h = x + a * h_ref[...]
    h_ref[...] = h[tb - 1:tb, :]
    o_ref[...] = _rms(jax.nn.gelu(xg_ref[...]) * h, mg_ref[...])


def _lru_part(proj3, conv_w, conv_b, wa_bd, ba, wx_bd, bx, lam, merge_g, tb):
    bsz, t, _ = proj3.shape
    return dict(
        body=_lru_kernel, stages=tb.bit_length() + 1,
        args=[proj3, proj3, conv_w, conv_b.reshape(1, -1), wa_bd, ba.reshape(1, -1), wx_bd,
              bx.reshape(1, -1), lam.reshape(1, -1), merge_g.reshape(1, -1)],
        in_specs=[_col_spec(tb, COL_LG), _col_spec(tb, COL_LX), _full_spec((LRU_CONV, GROUP_W)),
                  _row_spec(), _full_spec((GROUP_W, GROUP_W)), _row_spec(),
                  _full_spec((GROUP_W, GROUP_W)), _row_spec(), _row_spec(), _row_spec()],
        out_specs=[_out_blk(tb)],
        out_shape=[jax.ShapeDtypeStruct((bsz, t, GROUP_W), F32)],
        scratch=[pltpu.VMEM((tb + 8, GROUP_W), F32), pltpu.VMEM((1, GROUP_W), F32)])


def _block_diag_weight(w):
    h, n, _ = w.shape
    eye = jnp.eye(h, dtype=w.dtype)
    return jnp.einsum('hij,hg->higj', w, eye).reshape(h * n, h * n).astype(BF16)


def _s5_matrices(lam_re, lam_im, log_dt, b_re, b_im, c_re, c_im):
    L, G, P, C = S5_CHUNK, S5_GROUPS, S5_STATE, S5_CH
    lr, li = lam_re.astype(F32), lam_im.astype(F32)
    dt = jnp.exp(log_dt.astype(F32))[:, None]
    mag = jnp.exp(lr * dt)
    a_re, a_im = mag * jnp.cos(li * dt), mag * jnp.sin(li * dt)
    den = lr * lr + li * li
    kap_re = ((a_re - 1.0) * lr + a_im * li) / den
    kap_im = (a_im * lr - (a_re - 1.0) * li) / den
    br, bi = b_re.astype(F32), b_im.astype(F32)
    bb_re = kap_re[..., None] * br - kap_im[..., None] * bi
    bb_im = kap_re[..., None] * bi + kap_im[..., None] * br
    cr, ci = c_re.astype(F32), c_im.astype(F32)
    eye = jnp.eye(G, dtype=F32)
    J = S5_STEPS
    kk = jnp.arange(J + 1, dtype=F32)[:, None, None]
    pmag = jnp.exp(kk * (lr * dt)[None])
    pw_re, pw_im = pmag * jnp.cos(kk * (li * dt)[None]), pmag * jnp.sin(kk * (li * dt)[None])
    ab_re = pw_re[:J, :, :, None] * bb_re[None] - pw_im[:J, :, :, None] * bb_im[None]
    ab_im = pw_re[:J, :, :, None] * bb_im[None] + pw_im[:J, :, :, None] * bb_re[None]
    rev = lambda x: jnp.stack([x[J - 1 - j] for j in range(J)])
    w_in = jnp.concatenate(
        [jnp.einsum('jgpc,gh->jgchp', rev(ab_re), eye).reshape(J * G * C, G * P),
         jnp.einsum('jgpc,gh->jgchp', rev(ab_im), eye).reshape(J * G * C, G * P)], axis=1)
    ca_re = jnp.einsum('gcp,jgp->jgcp', cr, pw_re[1:]) - jnp.einsum('gcp,jgp->jgcp', ci, pw_im[1:])
    ca_im = jnp.einsum('gcp,jgp->jgcp', cr, pw_im[1:]) + jnp.einsum('gcp,jgp->jgcp', ci, pw_re[1:])
    c_out = jnp.concatenate(
        [jnp.einsum('jgcp,gh->gpjhc', ca_re, eye).reshape(G * P, J * G * C),
         -jnp.einsum('jgcp,gh->gpjhc', ca_im, eye).reshape(G * P, J * G * C)], axis=0)
    taps = jnp.einsum('gop,kgpc->kgco', cr, ab_re) - jnp.einsum('gop,kgpc->kgco', ci, ab_im)
    none = jnp.zeros_like(taps[0])
    d_io = jnp.stack([jnp.stack([taps[j - i] if j >= i else none for j in range(J)])
                      for i in range(J)])
    d_io = jnp.einsum('ijgco,gh->igcjho', d_io, eye).reshape(J * G * C, J * G * C)
    a_grp = jnp.stack([pw_re[J].reshape(G * P), pw_im[J].reshape(G * P)])
    mag_l = jnp.exp(L * lr * dt)
    a_chunk = jnp.stack([(mag_l * jnp.cos(L * li * dt)).reshape(G * P),
                         (mag_l * jnp.sin(L * li * dt)).reshape(G * P)])
    return w_in.astype(BF16), c_out.astype(BF16), d_io.astype(BF16), a_grp, a_chunk


def _s5_kernel(u0_ref, u1_ref, w_ref, c_ref, dio_ref, a_ref, al_ref, d_ref, wg_ref, mg_ref, o_ref,
               carry_ref, x_s, st_s, y_s, bu_s):
    nb, tb, lanes = u0_ref.shape
    L, J = S5_CHUNK, S5_STEPS
    r = tb // L
    rows = nb * r
    ns = S5_GROUPS * S5_STATE

    @pl.when(pl.program_id(0) == 0)
    def _():
        carry_ref[...] = jnp.zeros_like(carry_ref)

    a_re, a_im = a_ref[0:1, :], a_ref[1:2, :]
    al_re, al_im = al_ref[0:1, :], al_ref[1:2, :]

    def inputs(q):
        parts = []
        for j in range(J):
            sl = pl.ds(q * J + j, r, stride=L)
            parts += [u0_ref[:, sl, :], u1_ref[:, sl, :]]
        return jnp.concatenate(parts, axis=-1).reshape(rows, J * 2 * lanes).astype(BF16)

    def advance(q):
        bu = bu_s[q]
        xr, xi = x_s[:, :ns], x_s[:, ns:]
        x_s[:, :ns] = a_re * xr - a_im * xi + bu[:, :ns]
        x_s[:, ns:] = a_re * xi + a_im * xr + bu[:, ns:]

    x_s[...] = jnp.zeros_like(x_s)
    for q in range(L // J):
        bu_s[q] = jnp.dot(inputs(q), w_ref[...], preferred_element_type=F32)
        advance(q)

    for b in range(nb):
        def hop(c, carry, b=b):
            xr, xi = carry
            row = pl.ds(b * r + c, 1)
            st_s[row, :] = jnp.concatenate([xr, xi], axis=1)
            p = x_s[row, :]
            return (al_re * xr - al_im * xi + p[:, :ns], al_re * xi + al_im * xr + p[:, ns:])

        xr, xi = lax.fori_loop(0, r, hop, (carry_ref[b:b + 1, :ns], carry_ref[b:b + 1, ns:]))
        carry_ref[b:b + 1, :] = jnp.concatenate([xr, xi], axis=1)

    x_s[...] = st_s[...]
    for q in range(L // J):
        y = (jnp.dot(x_s[...].astype(BF16), c_ref[...], preferred_element_type=F32)
             + jnp.dot(inputs(q), dio_ref[...], preferred_element_type=F32))
        for j in range(J):
            sl = pl.ds(q * J + j, r, stride=L)
            lo = j * 2 * lanes
            y_s[0, :, sl, :] = y[:, lo:lo + lanes].reshape(nb, r, lanes)
            y_s[1, :, sl, :] = y[:, lo + lanes:lo + 2 * lanes].reshape(nb, r, lanes)
        if q + 1 < L // J:
            advance(q)

    u = jnp.concatenate([u0_ref[...], u1_ref[...]], axis=-1).reshape(nb * tb, 2 * lanes)
    y = jnp.concatenate([y_s[0], y_s[1]], axis=-1).reshape(nb * tb, 2 * lanes)
    y = jax.nn.gelu(y + d_ref[...] * u)
    z = jnp.dot(y.astype(BF16), wg_ref[...], preferred_element_type=F32)
    out = _rms(z[:, :GROUP_W] * jax.nn.sigmoid(z[:, GROUP_W:]), mg_ref[...])
    o_ref[...] = out.reshape(nb, tb, GROUP_W)


def _s5(proj3, mats, d_skip, w_glu_bf16, merge_g, tb):
    bsz, t, _ = proj3.shape
    w_in, c_out, d_io, a_grp, a_chunk = mats
    ns = S5_GROUPS * S5_STATE
    jw = S5_STEPS * GROUP_W
    lanes = GROUP_W // 2
    rows = bsz * (tb // S5_CHUNK)
    half = lambda j: pl.BlockSpec((bsz, tb, lanes), lambda i, j=j: (0, i, 2 * COL_S5 + j))
    full = lambda a, b: pl.BlockSpec((a, b), lambda i: (0, 0))
    return pl.pallas_call(
        _s5_kernel,
        grid=(t // tb,),
        in_specs=[half(0), half(1), full(jw, 2 * ns), full(2 * ns, jw), full(jw, jw), full(2, ns),
                  full(2, ns), full(1, GROUP_W), full(GROUP_W, 2 * GROUP_W), full(1, GROUP_W)],
        out_specs=pl.BlockSpec((bsz, tb, GROUP_W), lambda i: (0, i, 0)),
        out_shape=jax.ShapeDtypeStruct((bsz, t, GROUP_W), F32),
        scratch_shapes=[pltpu.VMEM((bsz, 2 * ns), F32), pltpu.VMEM((rows, 2 * ns), F32),
                        pltpu.VMEM((rows, 2 * ns), F32), pltpu.VMEM((2, bsz, tb, lanes), F32),
                        pltpu.VMEM((S5_CHUNK // S5_STEPS, rows, 2 * ns), F32)],
        compiler_params=_cparams("arbitrary"),
        name="s5",
    )(proj3, proj3, w_in, c_out, d_io, a_grp, a_chunk, d_skip.reshape(1, -1), w_glu_bf16,
      merge_g.reshape(1, -1))


def _rwkv_kernel(*refs, has_vmix, reset):
    if has_vmix:
        (r_ref, k_ref, v_ref, lo_ref, vf_ref, mu_ref, w0_ref, w2_ref, a0_ref, a2_ref, g2_ref,
         kk_ref, ka_ref, rk_ref, lnw_ref, lnb_ref, mg_ref, v0_ref, v1_ref, v2_ref,
         o_ref, h_ref, prev_ref, lw_s, a_s, b_s, k_s, r_s, v_s, y_s, m_s, n_s, p_s, z_s) = refs
    else:
        (r_ref, k_ref, v_ref, lo_ref, mu_ref, w0_ref, w2_ref, a0_ref, a2_ref, g2_ref,
         kk_ref, ka_ref, rk_ref, lnw_ref, lnb_ref, mg_ref,
         o_ref, vf_out_ref, h_ref, prev_ref, lw_s, a_s, b_s, k_s, r_s, v_s, y_s,
         m_s, n_s, p_s, z_s) = refs
    tb = r_ref.shape[0]
    ch = RWKV_CHUNK
    nh = GROUP_W // HEAD_W
    if reset:
        h_ref[...] = jnp.zeros_like(h_ref)
        prev_ref[...] = jnp.zeros_like(prev_ref)
        return

    row0 = lax.broadcasted_iota(jnp.int32, (tb, 1), 0) == 0

    def mixed(ref, j):
        p = ref[...]
        lanes = slice(j * GROUP_W, (j + 1) * GROUP_W)
        sh = jnp.where(row0, prev_ref[0:1, lanes], pltpu.roll(p, 1, axis=0))
        prev_new = p[tb - 1:tb, :]
        return p + mu_ref[0:1, lanes] * (sh - p), prev_new

    r, pr_ = mixed(r_ref, 0)
    k, pk_ = mixed(k_ref, 1)
    v, pv_ = mixed(v_ref, 2)
    lo, pl_ = mixed(lo_ref, 3)
    for j, pn in enumerate((pr_, pk_, pv_, pl_)):
        prev_ref[0:1, j * GROUP_W:(j + 1) * GROUP_W] = pn

    bd = _block_diag_mask(GROUP_W, HEAD_W)
    bd_bf16 = bd.astype(BF16)

    w_raw = -_softplus(-(w0_ref[...] + _mm(jnp.tanh(lo), w2_ref[...]))) - 0.5
    lw_s[...] = -jnp.exp(w_raw)
    alr = jax.nn.sigmoid(a0_ref[...] + _mm(lo, a2_ref[...]))
    g = _mm(jax.nn.sigmoid(lo), g2_ref[...])
    kkr = k * kk_ref[...]
    kk = kkr / jnp.maximum(jnp.sqrt(_head_sum(kkr * kkr, bd_bf16)), 1e-12)
    k = k * (1.0 + (alr - 1.0) * ka_ref[...])
    if has_vmix:
        gate = jax.nn.sigmoid(v0_ref[...] + _mm(_mm(v, v1_ref[...]), v2_ref[...]))
        v = v + (vf_ref[...] - v) * gate
    else:
        vf_out_ref[...] = v
    a_s[...] = -kk
    b_s[...] = kk * alr
    k_s[...] = k
    r_s[...] = r
    v_s[...] = v
    yield

    n4 = nh * ch
    ri = lax.broadcasted_iota(jnp.int32, (n4, n4), 0)
    ci = lax.broadcasted_iota(jnp.int32, (n4, n4), 1)
    same_head = (ri // ch) == (ci // ch)
    strict = same_head & ((ri % ch) > (ci % ch))
    incl = same_head & ((ri % ch) >= (ci % ch))
    eye = (ri == ci).astype(F32)
    same_blk = {b: (ri // b) == (ci // b) for b in (8, 16, 32, 64)}
    hm = ((lax.broadcasted_iota(jnp.int32, (n4, GROUP_W), 0) // ch)
          == (lax.broadcasted_iota(jnp.int32, (n4, GROUP_W), 1) // HEAD_W))
    tri = (lax.broadcasted_iota(jnp.int32, (ch, ch), 0)
           >= lax.broadcasted_iota(jnp.int32, (ch, ch), 1)).astype(BF16)

    def stack(x):
        return jnp.where(hm, jnp.concatenate([x] * nh, axis=0), 0.0)

    def dot(x, y, dims=_NN):
        return lax.dot_general(x.astype(BF16), y.astype(BF16), (dims, ((), ())),
                               preferred_element_type=F32)

    chunks = range(tb // ch)
    pre = []
    for c in chunks:
        sl = slice(c * ch, (c + 1) * ch)
        lw = lw_s[sl, :]
        cl = _exact_lhs_mm(tri, lw)
        cl_end = cl[ch - 1:ch, :]
        e_in, e_ex = jnp.exp(cl), jnp.exp(cl - lw)
        e_neg, e_end = jnp.exp(-cl), jnp.exp(cl_end - cl)
        av, bv, kv, rv, vv = a_s[sl, :], b_s[sl, :], k_s[sl, :], r_s[sl, :], v_s[sl, :]
        pre.append(dict(
            at4=stack(av * e_ex), rt4=stack(rv * e_in), v4=stack(vv),
            bt4=jnp.concatenate([bv * e_neg] * nh, axis=0),
            kt4=jnp.concatenate([kv * e_neg] * nh, axis=0),
            bh4=stack(bv * e_end), kh4=stack(kv * e_end), g_end=jnp.exp(cl_end)))
        yield
    for d in pre:
        ar = jnp.concatenate([d["at4"], d["rt4"]], axis=0)
        sb = dot(ar, d["bt4"], _NT)
        sk = dot(ar, d["kt4"], _NT)
        d["l_ab"] = jnp.where(strict, sb[:n4], 0.0)
        d["l_ak"] = jnp.where(strict, sk[:n4], 0.0)
        d["l_rb"] = jnp.where(incl, sb[n4:], 0.0)
        d["l_rk"] = jnp.where(incl, sk[n4:], 0.0)
        yield
    for d in pre:
        nb8 = jnp.where(same_blk[8], d["l_ab"], 0.0)
        d["tinv"] = eye + nb8
        d["pw"] = dot(nb8, nb8)
        yield
    for d in pre:
        d["tinv"] = d["tinv"] + dot(d["tinv"], d["pw"])
        d["pw"] = dot(d["pw"], d["pw"])
        yield
    for d in pre:
        d["tinv"] = d["tinv"] + dot(d["tinv"], d["pw"])
        yield
    blk = 8
    while blk < ch:
        for d in pre:
            d["pw"] = dot(jnp.where(same_blk[2 * blk] & ~same_blk[blk], d["l_ab"], 0.0), d["tinv"])
            yield
        for d in pre:
            d["tinv"] = d["tinv"] + dot(d["tinv"], d["pw"])
            yield
        blk *= 2
    for d in pre:
        d["lakv"] = dot(d["l_ak"], d["v4"])
        d["lrkv"] = dot(d["l_rk"], d["v4"])
        d["khv"] = dot(d["kh4"], d["v4"], _TN)
        yield
    for d in pre:
        d["x12"] = dot(d["tinv"], jnp.concatenate([d["at4"], d["lakv"]], axis=1))
        yield
    for c, d in zip(chunks, pre):
        mn = dot(d["bh4"], d["x12"], _TN)
        pz = dot(d["l_rb"], d["x12"])
        m_s[c] = jnp.where(ri == ci, d["g_end"], 0.0) + mn[:, :GROUP_W]
        n_s[c] = mn[:, GROUP_W:] + d["khv"]
        p_s[c] = d["rt4"] + pz[:, :GROUP_W]
        z_s[c] = pz[:, GROUP_W:] + d["lrkv"]
        yield

    h = h_ref[...]
    for c in range(tb // ch):
        y4 = dot(p_s[c], h) + z_s[c]
        y = y4[0:ch]
        for j in range(1, nh):
            y = y + y4[j * ch:(j + 1) * ch]
        y_s[c * ch:(c + 1) * ch, :] = y
        h = _dg3(m_s[c], h, _NN) + n_s[c]
        yield
    h_ref[...] = h

    y = y_s[...]
    r, k, v = r_s[...], k_s[...], v_s[...]
    mean = _head_sum(y, bd_bf16) * (1.0 / HEAD_W)
    d = y - mean
    var = _head_sum(d * d, bd_bf16) * (1.0 / HEAD_W)
    y = d * lax.rsqrt(var + RWKV_GN_EPS) * lnw_ref[...] + lnb_ref[...]
    y = y + _head_sum(r * k * rk_ref[...], bd_bf16) * v
    o_ref[...] = _rms(y * g, mg_ref[...])


def _pad_rows(w, start, total=GROUP_W):
    out = jnp.zeros((total, w.shape[1]), F32).at[start:start + w.shape[0]].set(w)
    return out.astype(BF16)


def _rwkv_part(proj3, v_first, lyr, p, merge_g, tb):
    bsz, t, _ = proj3.shape
    has_vmix = v_first is not None
    row = lambda x: x.reshape(1, -1)
    w2 = _pad_rows(p["rwkv_w2"][lyr], 0)
    a2 = _pad_rows(p["rwkv_a2"][lyr], RWKV_W_LORA)
    g2 = _pad_rows(p["rwkv_g2"][lyr], RWKV_W_LORA + RWKV_A_LORA)
    args = [proj3, proj3, proj3, proj3]
    specs = [_col_spec(tb, COL_R), _col_spec(tb, COL_K), _col_spec(tb, COL_V), _col_spec(tb, COL_LORA)]
    if has_vmix:
        args.append(v_first)
        specs.append(pl.BlockSpec((None, tb, GROUP_W), lambda b, i: (b, i, 0)))
    args += [row(p["rwkv_mu"][lyr]), row(p["rwkv_w0"][lyr]), w2, row(p["rwkv_a0"][lyr]), a2, g2,
             row(p["rwkv_k_k"][lyr]), row(p["rwkv_k_a"][lyr]), row(p["rwkv_r_k"][lyr]),
             row(p["rwkv_ln_w"][lyr]), row(p["rwkv_ln_b"][lyr]), row(merge_g)]
    sq = _full_spec((GROUP_W, GROUP_W))
    specs += [_row_spec(4 * GROUP_W), _row_spec(), sq, _row_spec(), sq, sq,
              _row_spec(), _row_spec(), _row_spec(), _row_spec(), _row_spec(), _row_spec()]
    if has_vmix:
        v1 = jnp.zeros((GROUP_W, 128), F32).at[:, :RWKV_V_LORA].set(p["rwkv_v1"][lyr - 1]).astype(BF16)
        v2 = jnp.zeros((128, GROUP_W), F32).at[:RWKV_V_LORA].set(p["rwkv_v2"][lyr - 1]).astype(BF16)
        args += [row(p["rwkv_v0"][lyr - 1]), v1, v2]
        specs += [_row_spec(), _full_spec((GROUP_W, 128)), _full_spec((128, GROUP_W))]
    out_sds = jax.ShapeDtypeStruct((bsz, t, GROUP_W), F32)
    blk = pltpu.VMEM((tb, GROUP_W), F32)
    mats = pltpu.VMEM((tb // RWKV_CHUNK, GROUP_W, GROUP_W), F32)
    n_out = 1 if has_vmix else 2
    return dict(
        body=functools.partial(_rwkv_kernel, has_vmix=has_vmix), stages=15 * (tb // RWKV_CHUNK) + 2,
        args=args, in_specs=specs,
        out_specs=[_out_blk(tb)] * n_out, out_shape=[out_sds] * n_out,
        scratch=[pltpu.VMEM((GROUP_W, GROUP_W), F32), pltpu.VMEM((1, 4 * GROUP_W), F32),
                 blk, blk, blk, blk, blk, blk, blk, mats, mats, mats, mats])


def _bf16_bits(x):
    u = lax.bitcast_convert_type(x, jnp.uint32)
    r = u + jnp.uint32(0x7FFF) + ((u >> 16) & jnp.uint32(1))
    return r & jnp.uint32(0xFFFF0000)


def _merge_router_kernel(oa_ref, ob_ref, oc_ref, od_ref, h_ref, wo_ref, nw_ref, wr_ref, br_ref,
                         h1_ref, hn_ref, ids_ref, wts_ref, cnt_ref):
    mix = jnp.concatenate([oa_ref[...], ob_ref[...], oc_ref[...], od_ref[...]], axis=1)
    h1 = h_ref[...] + jnp.dot(mix.astype(BF16), wo_ref[...], preferred_element_type=F32)
    h1_ref[...] = h1
    hn = _rms(h1, nw_ref[...])
    half = hn.shape[1] // 2
    hn_ref[...] = _bf16_bits(hn[:, :half]) | (_bf16_bits(hn[:, half:]) >> 16)
    logits = _dg3(hn, wr_ref[...], _NN) + br_ref[...]
    lane = lax.broadcasted_iota(jnp.int32, logits.shape, 1)
    big = jnp.int32(ROUTER_LANES)
    is_c = (lane >= N_EXPERTS) & (lane < N_EXPERTS + MOE_GROUPS)
    cm = jnp.max(jnp.where(is_c, logits, NEG_BIG), axis=-1, keepdims=True)
    gsel = jnp.min(jnp.where(is_c & (logits == cm), lane, big), axis=-1, keepdims=True) - N_EXPERTS
    p_g = 1.0 / jnp.sum(jnp.where(is_c, jnp.exp(logits - cm), 0.0), axis=-1, keepdims=True)
    lo = gsel * MOE_PER_GROUP
    in_g = (lane >= lo) & (lane < lo + MOE_PER_GROUP)
    m1 = jnp.max(jnp.where(in_g, logits, NEG_BIG), axis=-1, keepdims=True)
    i1 = jnp.min(jnp.where(in_g & (logits == m1), lane, big), axis=-1, keepdims=True)
    in_g2 = in_g & (lane != i1)
    m2 = jnp.max(jnp.where(in_g2, logits, NEG_BIG), axis=-1, keepdims=True)
    i2 = jnp.min(jnp.where(in_g2 & (logits == m2), lane, big), axis=-1, keepdims=True)
    w1 = p_g / (1.0 + jnp.exp(m2 - m1))
    w2 = p_g - w1
    two = lax.broadcasted_iota(jnp.int32, ids_ref.shape, 1)
    ids_ref[...] = jnp.where(two == 0, i1, i2)
    wts_ref[...] = jnp.where(two == 0, w1, w2)

    @pl.when(pl.program_id(0) == 0)
    def _():
        cnt_ref[...] = jnp.zeros_like(cnt_ref)

    cnt_ref[...] += jnp.sum(((lane == i1) | (lane == i2)).astype(F32), axis=0, keepdims=True)


def _merge_router(outs, h2d, w_out_bf16, norm_w, w_router, b_router, tm):
    n, d = h2d.shape
    grp = pl.BlockSpec((tm, GROUP_W), lambda i: (i, 0))
    full = lambda a, b: pl.BlockSpec((a, b), lambda i: (0, 0))
    rowblk = lambda w: pl.BlockSpec((tm, w), lambda i: (i, 0))
    return pl.pallas_call(
        _merge_router_kernel,
        grid=(n // tm,),
        in_specs=[grp, grp, grp, grp, rowblk(d), full(d, d), full(1, d), full(d, ROUTER_LANES),
                  full(1, ROUTER_LANES)],
        out_specs=[rowblk(d), rowblk(d // 2), rowblk(MOE_TOPK), rowblk(MOE_TOPK), full(1, ROUTER_LANES)],
        out_shape=[jax.ShapeDtypeStruct((n, d), F32), jax.ShapeDtypeStruct((n, d // 2), jnp.uint32),
                   jax.ShapeDtypeStruct((n, MOE_TOPK), jnp.int32),
                   jax.ShapeDtypeStruct((n, MOE_TOPK), F32),
                   jax.ShapeDtypeStruct((1, ROUTER_LANES), F32)],
        compiler_params=_cparams("arbitrary"),
        name="merge_router",
    )(*[o.reshape(n, GROUP_W) for o in outs], h2d, w_out_bf16, norm_w.reshape(1, d), w_router, b_router)


def _row_copy(src_hbm, src_row, dst_ref, dst_row, sem):
    return pltpu.make_async_copy(src_hbm.at[pl.ds(src_row, 1), :], dst_ref.at[pl.ds(dst_row, 1), :], sem)


def _position_kernel(ids_ref, base_ref, pos_ref, run_ref):
    tm = ids_ref.shape[0]

    @pl.when(pl.program_id(0) == 0)
    def _():
        run_ref[...] = jnp.zeros_like(run_ref)

    i1, i2 = ids_ref[:, 0:1], ids_ref[:, 1:2]
    lane = lax.broadcasted_iota(jnp.int32, (tm, ROUTER_LANES), 1)
    hit = ((lane == i1) | (lane == i2)).astype(BF16)
    earlier = (lax.broadcasted_iota(jnp.int32, (tm, tm), 0)
               > lax.broadcasted_iota(jnp.int32, (tm, tm), 1)).astype(BF16)
    rank = jnp.dot(earlier, hit, preferred_element_type=F32) + run_ref[...]
    where_to = rank + base_ref[...]
    p1 = jnp.sum(jnp.where(lane == i1, where_to, 0.0), axis=-1, keepdims=True)
    p2 = jnp.sum(jnp.where(lane == i2, where_to, 0.0), axis=-1, keepdims=True)
    two = lax.broadcasted_iota(jnp.int32, pos_ref.shape, 1)
    pos_ref[...] = jnp.where(two == 0, p1, p2).astype(jnp.int32)
    run_ref[...] += jnp.sum(hit.astype(F32), axis=0, keepdims=True)


def _positions(ids, group_start, tm):
    n = ids.shape[0]
    return pl.pallas_call(
        _position_kernel,
        grid=(n // tm,),
        in_specs=[pl.BlockSpec((tm, MOE_TOPK), lambda i: (i, 0)),
                  pl.BlockSpec((1, ROUTER_LANES), lambda i: (0, 0))],
        out_specs=pl.BlockSpec((tm, MOE_TOPK), lambda i: (i, 0)),
        out_shape=jax.ShapeDtypeStruct((n, MOE_TOPK), jnp.int32),
        scratch_shapes=[pltpu.VMEM((1, ROUTER_LANES), F32)],
        compiler_params=_cparams("arbitrary"),
        name="moe_positions",
    )(ids, group_start)


def _dispatch_kernel(p0_ref, p1_ref, x_ref, xs_in_hbm, xs_hbm, sem):
    del xs_in_hbm
    n = x_ref.shape[0]

    def copies(i):
        return (pltpu.make_async_copy(x_ref.at[pl.ds(i, 1), :], xs_hbm.at[pl.ds(p0_ref[i], 1), :], sem),
                pltpu.make_async_copy(x_ref.at[pl.ds(i, 1), :], xs_hbm.at[pl.ds(p1_ref[i], 1), :], sem))

    def issue(i, c):
        for cp in copies(i):
            cp.start()
        return c

    def drain(i, c):
        for cp in copies(i):
            cp.wait()
        return c

    lax.fori_loop(0, n, issue, 0, unroll=8)
    lax.fori_loop(0, n, drain, 0, unroll=8)


def _dispatch(x, pos0, pos1, n_rows, chunk):
    n, w = x.shape
    smem = lambda: pl.BlockSpec((chunk,), lambda s: (s,), memory_space=pltpu.SMEM)
    return pl.pallas_call(
        _dispatch_kernel,
        grid=(n // chunk,),
        in_specs=[smem(), smem(), pl.BlockSpec((chunk, w), lambda s: (s, 0)),
                  pl.BlockSpec(memory_space=pl.ANY)],
        out_specs=pl.BlockSpec(memory_space=pl.ANY),
        out_shape=jax.ShapeDtypeStruct((n_rows, w), x.dtype),
        input_output_aliases={3: 0},
        scratch_shapes=[pltpu.SemaphoreType.DMA(())],
        compiler_params=_cparams("arbitrary"),
        name="moe_dispatch",
    )(pos0, pos1, x, jnp.zeros((n_rows, w), x.dtype))


def _expert_kernel(te_ref, nt_ref, xs_ref, wg_ref, wu_ref, wd_ref, o_ref, wg_s, wu_s, wd_s):
    i = pl.program_id(0)

    @pl.when((i == 0) | (te_ref[i] != te_ref[jnp.maximum(i - 1, 0)]))
    def _():
        wg_s[...] = wg_ref[...].astype(BF16)
        wu_s[...] = wu_ref[...].astype(BF16)
        wd_s[...] = wd_ref[...].astype(BF16)

    @pl.when(i < nt_ref[0])
    def _():
        p = xs_ref[...]
        xa = lax.bitcast_convert_type(p & jnp.uint32(0xFFFF0000), F32).astype(BF16)
        xb = lax.bitcast_convert_type(p << 16, F32).astype(BF16)
        x = jnp.concatenate([xa, xb], axis=1)
        he = (_silu(jnp.dot(x, wg_s[...], preferred_element_type=F32))
              * jnp.dot(x, wu_s[...], preferred_element_type=F32))
        o_ref[...] = jnp.dot(he.astype(BF16), wd_s[...], preferred_element_type=F32)

    @pl.when(i >= nt_ref[0])
    def _():
        o_ref[...] = jnp.zeros_like(o_ref)


def _experts(xs, tile_expert, n_tiles_used, wg, wu, wd, layer, tm):
    p, half = xs.shape
    d = 2 * half
    pick = lambda i, te, nt: (layer, te[i], 0, 0)
    grid_spec = pltpu.PrefetchScalarGridSpec(
        num_scalar_prefetch=2,
        grid=(p // tm,),
        in_specs=[pl.BlockSpec((tm, half), lambda i, te, nt: (i, 0)),
                  pl.BlockSpec((None, None, d, D_EXPERT), pick),
                  pl.BlockSpec((None, None, d, D_EXPERT), pick),
                  pl.BlockSpec((None, None, D_EXPERT, d), pick)],
        out_specs=pl.BlockSpec((tm, d), lambda i, te, nt: (i, 0)),
        scratch_shapes=[pltpu.VMEM((d, D_EXPERT), BF16), pltpu.VMEM((d, D_EXPERT), BF16),
                        pltpu.VMEM((D_EXPERT, d), BF16)],
    )
    return pl.pallas_call(
        _expert_kernel,
        grid_spec=grid_spec,
        out_shape=jax.ShapeDtypeStruct((p, d), F32),
        compiler_params=_cparams("arbitrary"),
        name="moe_experts",
    )(tile_expert, n_tiles_used, xs, wg, wu, wd)


def _combine_kernel(i0_ref, i1_ref, h1_ref, w_ref, fw_ref, ys_hbm, o_ref, buf0, buf1, sem, *, final_norm):
    n = o_ref.shape[0]

    def issue(i, c):
        _row_copy(ys_hbm, i0_ref[i], buf0, i, sem).start()
        _row_copy(ys_hbm, i1_ref[i], buf1, i, sem).start()
        return c

    def drain(i, c):
        _row_copy(ys_hbm, 0, buf0, i, sem).wait()
        _row_copy(ys_hbm, 0, buf1, i, sem).wait()
        return c

    lax.fori_loop(0, n, issue, 0, unroll=8)
    lax.fori_loop(0, n, drain, 0, unroll=8)
    out = h1_ref[...] + w_ref[:, 0:1] * buf0[...] + w_ref[:, 1:2] * buf1[...]
    if final_norm:
        out = _rms(out, fw_ref[...])
    o_ref[...] = out


def _combine(ys, pos0, pos1, wts, h1, final_w, final_norm, chunk):
    n, d = h1.shape
    smem = lambda: pl.BlockSpec((chunk,), lambda s: (s,), memory_space=pltpu.SMEM)
    return pl.pallas_call(
        functools.partial(_combine_kernel, final_norm=final_norm),
        grid=(n // chunk,),
        in_specs=[smem(), smem(), pl.BlockSpec((chunk, d), lambda s: (s, 0)),
                  pl.BlockSpec((chunk, MOE_TOPK), lambda s: (s, 0)),
                  pl.BlockSpec((1, d), lambda s: (0, 0)), pl.BlockSpec(memory_space=pl.ANY)],
        out_specs=pl.BlockSpec((chunk, d), lambda s: (s, 0)),
        out_shape=jax.ShapeDtypeStruct((n, d), F32),
        scratch_shapes=[pltpu.VMEM((chunk, d), F32), pltpu.VMEM((chunk, d), F32),
                        pltpu.SemaphoreType.DMA(())],
        compiler_params=_cparams("arbitrary"),
        name="moe_combine",
    )(pos0, pos1, h1, wts, final_w.reshape(1, d), ys)


def _moe(hn_packed, ids, wts, counts, h1, wg, wu, wd, layer, final_w, final_norm, tm):
    n = h1.shape[0]
    n_tiles = (n * MOE_TOPK) // tm + N_EXPERTS
    cnt = counts[0, :N_EXPERTS].astype(jnp.int32)
    padded = ((cnt + tm - 1) // tm) * tm
    ends = jnp.cumsum(padded)
    tile_start = jnp.arange(n_tiles, dtype=jnp.int32) * tm
    tile_expert = jnp.minimum(jnp.sum(ends[None, :] <= tile_start[:, None], axis=1),
                              N_EXPERTS - 1).astype(jnp.int32)
    n_tiles_used = (ends[-1] // tm).astype(jnp.int32).reshape(1)
    group_start = jnp.zeros((1, ROUTER_LANES), F32).at[0, :N_EXPERTS].set((ends - padded).astype(F32))
    pos = _positions(ids, group_start, _pick_div(n, 512))
    pos0, pos1 = pos[:, 0], pos[:, 1]
    xs = _dispatch(hn_packed, pos0, pos1, n_tiles * tm, _pick_div(n, 1024))
    ys = _experts(xs, tile_expert, n_tiles_used, wg, wu, wd, layer, tm)
    return _combine(ys, pos0, pos1, wts, h1, final_w, final_norm, _pick_div(n, 1024))


def _pick_div(n, pref):
    while n % pref:
        pref //= 2
    return pref


def _pick(n, pref):
    return pref if n % pref == 0 else n


def kernel(x, norm1_w, w_in, hgrn_lb_logits, hgrn_norm_w, s5_lambda_re, s5_lambda_im, s5_log_dt, s5_b_re, s5_b_im, s5_c_re, s5_c_im, s5_d, s5_w_glu, rwkv_mu, rwkv_w0, rwkv_w2, rwkv_a0, rwkv_a2, rwkv_g2, rwkv_k_k, rwkv_k_a, rwkv_r_k, rwkv_v0, rwkv_v1, rwkv_v2, rwkv_ln_w, rwkv_ln_b, lru_conv_w, lru_conv_b, lru_wa, lru_ba, lru_wx, lru_bx, lru_lambda, merge_gain, w_out, norm2_w, moe_coarse_w, moe_coarse_b, moe_fine_w, moe_fine_b, moe_w_gate, moe_w_up, moe_w_down, final_norm_w):
    bsz, t, d = x.shape
    n = bsz * t
    depth = w_in.shape[0]
    rw = dict(rwkv_mu=rwkv_mu, rwkv_w0=rwkv_w0, rwkv_w2=rwkv_w2, rwkv_a0=rwkv_a0, rwkv_a2=rwkv_a2,
              rwkv_g2=rwkv_g2, rwkv_k_k=rwkv_k_k, rwkv_k_a=rwkv_k_a, rwkv_r_k=rwkv_r_k,
              rwkv_v0=rwkv_v0, rwkv_v1=rwkv_v1, rwkv_v2=rwkv_v2, rwkv_ln_w=rwkv_ln_w,
              rwkv_ln_b=rwkv_ln_b)
    lb_all = jnp.cumsum(jax.nn.softmax(hgrn_lb_logits.astype(F32), axis=0), axis=0)
    lb_all = lb_all - lb_all[:1]

    tm_proj = _pick(n, 512)
    tm_moe = 512
    tb_mix = _pick(t, 256)
    tb_s5 = _pick(t, 512)

    h = x.reshape(n, d)
    v_first = None
    for l in range(depth):
        mg = merge_gain[l].reshape(4, GROUP_W)
        proj = _in_proj(h, norm1_w[l], w_in[l].astype(BF16), tm_proj)
        proj3 = proj.reshape(bsz, t, -1)
        mats = _s5_matrices(s5_lambda_re[l], s5_lambda_im[l], s5_log_dt[l], s5_b_re[l], s5_b_im[l],
                            s5_c_re[l], s5_c_im[l])
        o_b = _s5(proj3, mats, s5_d[l], s5_w_glu[l].astype(BF16), mg[1], tb_s5)
        parts = [_rwkv_part(proj3, v_first, l, rw, mg[2], tb_mix),
                 _hgrn_part(proj3, lb_all[l], hgrn_norm_w[l], mg[0], tb_mix),
                 _lru_part(proj3, lru_conv_w[l], lru_conv_b[l], _block_diag_weight(lru_wa[l]),
                           lru_ba[l], _block_diag_weight(lru_wx[l]), lru_bx[l], lru_lambda[l],
                           mg[3], tb_mix)]
        outs = _mixers(parts, bsz, t // tb_mix, "mixers")
        o_c, o_a, o_d = outs[0], outs[-2], outs[-1]
        if v_first is None:
            v_first = outs[1]
        w_router = jnp.concatenate(
            [moe_fine_w[l].transpose(1, 0, 2).reshape(d, N_EXPERTS), moe_coarse_w[l],
             jnp.zeros((d, ROUTER_LANES - N_EXPERTS - MOE_GROUPS), F32)], axis=1)
        b_router = jnp.concatenate(
            [moe_fine_b[l].reshape(N_EXPERTS), moe_coarse_b[l],
             jnp.zeros((ROUTER_LANES - N_EXPERTS - MOE_GROUPS,), F32)]).reshape(1, ROUTER_LANES)
        h1, hn, ids, wts, counts = _merge_router((o_a, o_b, o_c, o_d), h, w_out[l].astype(BF16),
                                                 norm2_w[l], w_router, b_router, tm_proj)
        h = _moe(hn, ids, wts, counts, h1, moe_w_gate, moe_w_up, moe_w_down, l,
                 final_norm_w, l == depth - 1, tm_moe)
    return h.reshape(bsz, t, d)
```

```python
import functools
import math

import jax
import jax.numpy as jnp
from jax import lax
from jax.experimental import pallas as pl
from jax.experimental.pallas import tpu as pltpu

F32 = jnp.float32
BF16 = jnp.bfloat16

D_MODEL = 1024
GROUP_W = 256
RMS_EPS = 1e-6
HEAD_W = 64
HGRN_CHUNK = 16
S5_CH = 16
S5_GROUPS = GROUP_W // S5_CH
S5_STATE = 64
S5_CHUNK = 16
S5_STEPS = 4
RWKV_CHUNK = 64
RWKV_GN_EPS = 64e-5
RWKV_W_LORA = 64
RWKV_A_LORA = 64
RWKV_G_LORA = 128
RWKV_V_LORA = 32
LRU_CONV = 4
LRU_C = 8.0
MOE_GROUPS = 4
MOE_PER_GROUP = 8
N_EXPERTS = MOE_GROUPS * MOE_PER_GROUP
MOE_TOPK = 2
D_EXPERT = 512
LANES = 128
ROUTER_LANES = LANES
NEG_BIG = -1e30
VMEM_LIMIT = 56 * 1024 * 1024

COL_HQ, COL_HF, COL_HI, COL_HG, COL_S5, COL_R, COL_K, COL_V, COL_LORA, COL_LG, COL_LX = range(11)


def _cparams(*sem):
    return pltpu.CompilerParams(dimension_semantics=sem, vmem_limit_bytes=VMEM_LIMIT)


def _mm(a, b):
    return jnp.dot(a.astype(BF16), b.astype(BF16), preferred_element_type=F32)


def _split2(x):
    hi = x.astype(BF16)
    lo = (x - hi.astype(F32)).astype(BF16)
    return hi, lo


def _dg3(a, b, dims):
    ah, al = _split2(a)
    bh, bl = _split2(b)
    d = lambda x, y: lax.dot_general(x, y, (dims, ((), ())), preferred_element_type=F32)
    return d(ah, bh) + d(ah, bl) + d(al, bh)


_NN = ((1,), (0,))
_NT = ((1,), (1,))
_TN = ((0,), (0,))


def _exact_lhs_mm(m_bf16, x):
    h1 = x.astype(BF16)
    r1 = x - h1.astype(F32)
    h2 = r1.astype(BF16)
    h3 = (r1 - h2.astype(F32)).astype(BF16)
    d = lambda y: jnp.dot(m_bf16, y, preferred_element_type=F32)
    return d(h1) + d(h2) + d(h3)


def _head_sum(x, bd_bf16):
    return jnp.dot(x.astype(BF16), bd_bf16, preferred_element_type=F32)


def _block_diag_mask(n, blk):
    r = lax.broadcasted_iota(jnp.int32, (n, n), 0) // blk
    c = lax.broadcasted_iota(jnp.int32, (n, n), 1) // blk
    return r == c


def _rms(x, w):
    return x * lax.rsqrt(jnp.mean(x * x, axis=-1, keepdims=True) + RMS_EPS) * w


def _silu(x):
    return x * jax.nn.sigmoid(x)


def _softplus(x):
    return jnp.maximum(x, 0.0) + jnp.log(1.0 + jnp.exp(-jnp.abs(x)))


def _in_proj_kernel(x_ref, nw_ref, w_ref, o_ref):
    y = _rms(x_ref[...], nw_ref[...])
    o_ref[...] = jnp.dot(y.astype(BF16), w_ref[...], preferred_element_type=F32)


def _in_proj(h2d, norm_w, w_bf16, tm):
    n, d = h2d.shape
    d_in = w_bf16.shape[1]
    return pl.pallas_call(
        _in_proj_kernel,
        grid=(n // tm,),
        in_specs=[pl.BlockSpec((tm, d), lambda i: (i, 0)),
                  pl.BlockSpec((1, d), lambda i: (0, 0)),
                  pl.BlockSpec((d, d_in), lambda i: (0, 0))],
        out_specs=pl.BlockSpec((tm, d_in), lambda i: (i, 0)),
        out_shape=jax.ShapeDtypeStruct((n, d_in), F32),
        compiler_params=_cparams("parallel"),
        name="in_proj",
    )(h2d, norm_w.reshape(1, d), w_bf16)


def _col_spec(tb, col):
    return pl.BlockSpec((None, tb, GROUP_W), lambda b, t: (b, t, col))


def _row_spec(width=GROUP_W):
    return pl.BlockSpec((1, width), lambda b, t: (0, 0))


def _full_spec(shape):
    return pl.BlockSpec(shape, lambda b, t: (0,) * len(shape))


def _hgrn_kernel(q_ref, f_ref, i_ref, g_ref, lb_ref, nw_ref, mg_ref, o_ref,
                 st_ref, q_s, k_s, v_s, lf_s, o_s, *, reset):
    ch = HGRN_CHUNK
    tb = q_ref.shape[0]
    if reset:
        st_ref[...] = jnp.zeros_like(st_ref)
        return

    lb = lb_ref[...]
    fx = f_ref[...]
    x1 = jnp.log(lb)
    x2 = jnp.log(1.0 - lb) - _softplus(-fx)
    m = jnp.maximum(x1, x2)
    lf_s[...] = m + jnp.log(jnp.exp(x1 - m) + jnp.exp(x2 - m))
    q_s[...] = _silu(q_ref[...])
    k_s[...] = (1.0 - lb) * jax.nn.sigmoid(-fx)
    v_s[...] = _silu(i_ref[...])
    yield

    bd = _block_diag_mask(GROUP_W, HEAD_W)
    bd_bf16 = bd.astype(BF16)
    tri = (lax.broadcasted_iota(jnp.int32, (ch, ch), 0)
           >= lax.broadcasted_iota(jnp.int32, (ch, ch), 1))
    tri_bf16 = tri.astype(BF16)
    tri3 = (lax.broadcasted_iota(jnp.int32, (ch, ch, 1), 0)
            <= lax.broadcasted_iota(jnp.int32, (ch, ch, 1), 1))

    st = st_ref[...]
    for c in range(tb // ch):
        sl = slice(c * ch, (c + 1) * ch)
        qc, kc, vc = q_s[sl, :], k_s[sl, :], v_s[sl, :]
        b = _exact_lhs_mm(tri_bf16, lf_s[sl, :])
        rel = b[None, :, :] - b[:, None, :]
        dec = jnp.exp(jnp.where(tri3, rel, NEG_BIG))
        p = (qc[None, :, :] * kc[:, None, :]) * dec
        sc = jnp.dot(p.reshape(ch * ch, GROUP_W).astype(BF16), bd_bf16,
                     preferred_element_type=F32).reshape(ch, ch, GROUP_W)
        o_intra = jnp.sum(sc * vc[:, None, :], axis=0)
        o_inter = lax.dot_general((qc * jnp.exp(b)).astype(BF16), st.astype(BF16),
                                  (_NT, ((), ())), preferred_element_type=F32)
        b_end = b[ch - 1:ch, :]
        kh = kc * jnp.exp(b_end - b)
        upd = lax.dot_general(vc.astype(BF16), kh.astype(BF16), (_TN, ((), ())),
                              preferred_element_type=F32)
        st = st * jnp.exp(b_end) + jnp.where(bd, upd, 0.0)
        o_s[sl, :] = o_intra + o_inter
        yield
    st_ref[...] = st

    o = o_s[...]
    ms = _head_sum(o * o, bd_bf16) * (1.0 / HEAD_W)
    o = o * lax.rsqrt(ms + RMS_EPS) * nw_ref[...] * _silu(g_ref[...])
    o_ref[...] = _rms(o, mg_ref[...])


def _out_blk(tb):
    return pl.BlockSpec((None, tb, GROUP_W), lambda b, i: (b, i, 0))


def _hgrn_part(proj3, lb, norm_w, merge_g, tb):
    bsz, t, _ = proj3.shape
    blk = pltpu.VMEM((tb, GROUP_W), F32)
    return dict(
        body=_hgrn_kernel, stages=tb // HGRN_CHUNK + 2,
        args=[proj3, proj3, proj3, proj3, lb.reshape(1, -1), norm_w.reshape(1, -1), merge_g.reshape(1, -1)],
        in_specs=[_col_spec(tb, COL_HQ), _col_spec(tb, COL_HF), _col_spec(tb, COL_HI),
                  _col_spec(tb, COL_HG), _row_spec(), _row_spec(), _row_spec()],
        out_specs=[_out_blk(tb)],
        out_shape=[jax.ShapeDtypeStruct((bsz, t, GROUP_W), F32)],
        scratch=[pltpu.VMEM((GROUP_W, GROUP_W), F32), blk, blk, blk, blk, blk])


def _mixer_kernel(*refs, parts):
    groups, i = [], 0
    for kind in range(3):
        for body, counts in parts:
            groups.append(refs[i:i + counts[kind]])
            i += counts[kind]
    k = len(parts)
    per_part = [groups[j] + groups[k + j] + groups[2 * k + j] for j in range(k)]

    @pl.when(pl.program_id(1) == 0)
    def _():
        for (body, _), r in zip(parts, per_part):
            for _step in body(*r, reset=True):
                pass

    runs = [[body(*r, reset=False), 0, counts[3]] for (body, counts), r in zip(parts, per_part)]
    while runs:
        run = min(runs, key=lambda x: x[1] / x[2])
        try:
            next(run[0])
            run[1] += 1
        except StopIteration:
            runs.remove(run)


def _mixers(parts, bsz, n_tblocks, name):
    cat = lambda key: [x for p in parts for x in p[key]]
    light = tuple((p["body"], (len(p["args"]), len(p["out_specs"]), len(p["scratch"]), p["stages"]))
                  for p in parts)
    return pl.pallas_call(
        functools.partial(_mixer_kernel, parts=light),
        grid=(bsz, n_tblocks),
        in_specs=cat("in_specs"),
        out_specs=cat("out_specs"),
        out_shape=cat("out_shape"),
        scratch_shapes=cat("scratch"),
        compiler_params=_cparams("parallel", "arbitrary"),
        name=name,
    )(*cat("args"))


def _lru_kernel(xg_ref, xr_ref, cw_ref, cb_ref, wa_ref, ba_ref, wx_ref, bx_ref, lam_ref, mg_ref,
                o_ref, buf_ref, h_ref, *, reset):
    tb = xr_ref.shape[0]
    if reset:
        buf_ref[0:8, :] = jnp.zeros((8, GROUP_W), F32)
        h_ref[...] = jnp.zeros_like(h_ref)
        return

    xr = xr_ref[...]
    buf_ref[8:8 + tb, :] = xr
    xc = cb_ref[...] + jnp.zeros_like(xr)
    for j in range(LRU_CONV):
        xc = xc + cw_ref[j:j + 1, :] * buf_ref[pl.ds(8 - (LRU_CONV - 1) + j, tb), :]
    buf_ref[0:8, :] = xr[tb - 8:tb, :]

    r = jax.nn.sigmoid(jnp.dot(xc.astype(BF16), wa_ref[...], preferred_element_type=F32) + ba_ref[...])
    gi = jax.nn.sigmoid(jnp.dot(xc.astype(BF16), wx_ref[...], preferred_element_type=F32) + bx_ref[...])
    log_a = -LRU_C * r * _softplus(-lam_ref[...])
    a = jnp.exp(log_a)
    x = jnp.sqrt(1.0 - jnp.exp(2.0 * log_a)) * (gi * xc)
    yield

    rows = lax.broadcasted_iota(jnp.int32, (tb, 1), 0)
    k = 1
    while k < tb:
        keep = rows >= k
        x = x + jnp.where(keep, a * pltpu.roll(x, k, axis=0), 0.0)
        a = jnp.where(keep, a * pltpu.roll(a, k, axis=0), a)
        k *= 2
        yield
    h = x + a * h_ref[...]
    h_ref[...] = h[tb - 1:tb, :]
    o_ref[...] = _rms(jax.nn.gelu(xg_ref[...]) * h, mg_ref[...])


def _lru_part(proj3, conv_w, conv_b, wa_bd, ba, wx_bd, bx, lam, merge_g, tb):
    bsz, t, _ = proj3.shape
    return dict(
        body=_lru_kernel, stages=tb.bit_length() + 1,
        args=[proj3, proj3, conv_w, conv_b.reshape(1, -1), wa_bd, ba.reshape(1, -1), wx_bd,
              bx.reshape(1, -1), lam.reshape(1, -1), merge_g.reshape(1, -1)],
        in_specs=[_col_spec(tb, COL_LG), _col_spec(tb, COL_LX), _full_spec((LRU_CONV, GROUP_W)),
                  _row_spec(), _full_spec((GROUP_W, GROUP_W)), _row_spec(),
                  _full_spec((GROUP_W, GROUP_W)), _row_spec(), _row_spec(), _row_spec()],
        out_specs=[_out_blk(tb)],
        out_shape=[jax.ShapeDtypeStruct((bsz, t, GROUP_W), F32)],
        scratch=[pltpu.VMEM((tb + 8, GROUP_W), F32), pltpu.VMEM((1, GROUP_W), F32)])


def _block_diag_weight(w):
    h, n, _ = w.shape
    eye = jnp.eye(h, dtype=w.dtype)
    return jnp.einsum('hij,hg->higj', w, eye).reshape(h * n, h * n).astype(BF16)


def _s5_matrices(lam_re, lam_im, log_dt, b_re, b_im, c_re, c_im):
    L, G, P, C = S5_CHUNK, S5_GROUPS, S5_STATE, S5_CH
    lr, li = lam_re.astype(F32), lam_im.astype(F32)
    dt = jnp.exp(log_dt.astype(F32))[:, None]
    mag = jnp.exp(lr * dt)
    a_re, a_im = mag * jnp.cos(li * dt), mag * jnp.sin(li * dt)
    den = lr * lr + li * li
    kap_re = ((a_re - 1.0) * lr + a_im * li) / den
    kap_im = (a_im * lr - (a_re - 1.0) * li) / den
    br, bi = b_re.astype(F32), b_im.astype(F32)
    bb_re = kap_re[..., None] * br - kap_im[..., None] * bi
    bb_im = kap_re[..., None] * bi + kap_im[..., None] * br
    cr, ci = c_re.astype(F32), c_im.astype(F32)
    eye = jnp.eye(G, dtype=F32)
    J = S5_STEPS
    kk = jnp.arange(J + 1, dtype=F32)[:, None, None]
    pmag = jnp.exp(kk * (lr * dt)[None])
    pw_re, pw_im = pmag * jnp.cos(kk * (li * dt)[None]), pmag * jnp.sin(kk * (li * dt)[None])
    ab_re = pw_re[:J, :, :, None] * bb_re[None] - pw_im[:J, :, :, None] * bb_im[None]
    ab_im = pw_re[:J, :, :, None] * bb_im[None] + pw_im[:J, :, :, None] * bb_re[None]
    rev = lambda x: jnp.stack([x[J - 1 - j] for j in range(J)])
    w_in = jnp.concatenate(
        [jnp.einsum('jgpc,gh->jgchp', rev(ab_re), eye).reshape(J * G * C, G * P),
         jnp.einsum('jgpc,gh->jgchp', rev(ab_im), eye).reshape(J * G * C, G * P)], axis=1)
    ca_re = jnp.einsum('gcp,jgp->jgcp', cr, pw_re[1:]) - jnp.einsum('gcp,jgp->jgcp', ci, pw_im[1:])
    ca_im = jnp.einsum('gcp,jgp->jgcp', cr, pw_im[1:]) + jnp.einsum('gcp,jgp->jgcp', ci, pw_re[1:])
    c_out = jnp.concatenate(
        [jnp.einsum('jgcp,gh->gpjhc', ca_re, eye).reshape(G * P, J * G * C),
         -jnp.einsum('jgcp,gh->gpjhc', ca_im, eye).reshape(G * P, J * G * C)], axis=0)
    taps = jnp.einsum('gop,kgpc->kgco', cr, ab_re) - jnp.einsum('gop,kgpc->kgco', ci, ab_im)
    none = jnp.zeros_like(taps[0])
    d_io = jnp.stack([jnp.stack([taps[j - i] if j >= i else none for j in range(J)])
                      for i in range(J)])
    d_io = jnp.einsum('ijgco,gh->igcjho', d_io, eye).reshape(J * G * C, J * G * C)
    a_grp = jnp.stack([pw_re[J].reshape(G * P), pw_im[J].reshape(G * P)])
    mag_l = jnp.exp(L * lr * dt)
    a_chunk = jnp.stack([(mag_l * jnp.cos(L * li * dt)).reshape(G * P),
                         (mag_l * jnp.sin(L * li * dt)).reshape(G * P)])
    return w_in.astype(BF16), c_out.astype(BF16), d_io.astype(BF16), a_grp, a_chunk


def _s5_kernel(u0_ref, u1_ref, w_ref, c_ref, dio_ref, a_ref, al_ref, d_ref, wg_ref, mg_ref, o_ref,
               carry_ref, x_s, st_s, y_s, bu_s):
    nb, tb, lanes = u0_ref.shape
    L, J = S5_CHUNK, S5_STEPS
    r = tb // L
    rows = nb * r
    ns = S5_GROUPS * S5_STATE

    @pl.when(pl.program_id(0) == 0)
    def _():
        carry_ref[...] = jnp.zeros_like(carry_ref)

    a_re, a_im = a_ref[0:1, :], a_ref[1:2, :]
    al_re, al_im = al_ref[0:1, :], al_ref[1:2, :]

    def inputs(q):
        parts = []
        for j in range(J):
            sl = pl.ds(q * J + j, r, stride=L)
            parts += [u0_ref[:, sl, :], u1_ref[:, sl, :]]
        return jnp.concatenate(parts, axis=-1).reshape(rows, J * 2 * lanes).astype(BF16)

    def advance(q):
        bu = bu_s[q]
        xr, xi = x_s[:, :ns], x_s[:, ns:]
        x_s[:, :ns] = a_re * xr - a_im * xi + bu[:, :ns]
        x_s[:, ns:] = a_re * xi + a_im * xr + bu[:, ns:]

    x_s[...] = jnp.zeros_like(x_s)
    for q in range(L // J):
        bu_s[q] = jnp.dot(inputs(q), w_ref[...], preferred_element_type=F32)
        advance(q)

    for b in range(nb):
        def hop(c, carry, b=b):
            xr, xi = carry
            row = pl.ds(b * r + c, 1)
            st_s[row, :] = jnp.concatenate([xr, xi], axis=1)
            p = x_s[row, :]
            return (al_re * xr - al_im * xi + p[:, :ns], al_re * xi + al_im * xr + p[:, ns:])

        xr, xi = lax.fori_loop(0, r, hop, (carry_ref[b:b + 1, :ns], carry_ref[b:b + 1, ns:]))
        carry_ref[b:b + 1, :] = jnp.concatenate([xr, xi], axis=1)

    x_s[...] = st_s[...]
    for q in range(L // J):
        y = (jnp.dot(x_s[...].astype(BF16), c_ref[...], preferred_element_type=F32)
             + jnp.dot(inputs(q), dio_ref[...], preferred_element_type=F32))
        for j in range(J):
            sl = pl.ds(q * J + j, r, stride=L)
            lo = j * 2 * lanes
            y_s[0, :, sl, :] = y[:, lo:lo + lanes].reshape(nb, r, lanes)
            y_s[1, :, sl, :] = y[:, lo + lanes:lo + 2 * lanes].reshape(nb, r, lanes)
        if q + 1 < L // J:
            advance(q)

    u = jnp.concatenate([u0_ref[...], u1_ref[...]], axis=-1).reshape(nb * tb, 2 * lanes)
    y = jnp.concatenate([y_s[0], y_s[1]], axis=-1).reshape(nb * tb, 2 * lanes)
    y = jax.nn.gelu(y + d_ref[...] * u)
    z = jnp.dot(y.astype(BF16), wg_ref[...], preferred_element_type=F32)
    out = _rms(z[:, :GROUP_W] * jax.nn.sigmoid(z[:, GROUP_W:]), mg_ref[...])
    o_ref[...] = out.reshape(nb, tb, GROUP_W)


def _s5(proj3, mats, d_skip, w_glu_bf16, merge_g, tb):
    bsz, t, _ = proj3.shape
    w_in, c_out, d_io, a_grp, a_chunk = mats
    ns = S5_GROUPS * S5_STATE
    jw = S5_STEPS * GROUP_W
    lanes = GROUP_W // 2
    rows = bsz * (tb // S5_CHUNK)
    half = lambda j: pl.BlockSpec((bsz, tb, lanes), lambda i, j=j: (0, i, 2 * COL_S5 + j))
    full = lambda a, b: pl.BlockSpec((a, b), lambda i: (0, 0))
    return pl.pallas_call(
        _s5_kernel,
        grid=(t // tb,),
        in_specs=[half(0), half(1), full(jw, 2 * ns), full(2 * ns, jw), full(jw, jw), full(2, ns),
                  full(2, ns), full(1, GROUP_W), full(GROUP_W, 2 * GROUP_W), full(1, GROUP_W)],
        out_specs=pl.BlockSpec((bsz, tb, GROUP_W), lambda i: (0, i, 0)),
        out_shape=jax.ShapeDtypeStruct((bsz, t, GROUP_W), F32),
        scratch_shapes=[pltpu.VMEM((bsz, 2 * ns), F32), pltpu.VMEM((rows, 2 * ns), F32),
                        pltpu.VMEM((rows, 2 * ns), F32), pltpu.VMEM((2, bsz, tb, lanes), F32),
                        pltpu.VMEM((S5_CHUNK // S5_STEPS, rows, 2 * ns), F32)],
        compiler_params=_cparams("arbitrary"),
        name="s5",
    )(proj3, proj3, w_in, c_out, d_io, a_grp, a_chunk, d_skip.reshape(1, -1), w_glu_bf16,
      merge_g.reshape(1, -1))


def _rwkv_kernel(*refs, has_vmix, reset):
    if has_vmix:
        (r_ref, k_ref, v_ref, lo_ref, vf_ref, mu_ref, w0_ref, w2_ref, a0_ref, a2_ref, g2_ref,
         kk_ref, ka_ref, rk_ref, lnw_ref, lnb_ref, mg_ref, v0_ref, v1_ref, v2_ref,
         o_ref, h_ref, prev_ref, lw_s, a_s, b_s, k_s, r_s, v_s, y_s, m_s, n_s, p_s, z_s) = refs
    else:
        (r_ref, k_ref, v_ref, lo_ref, mu_ref, w0_ref, w2_ref, a0_ref, a2_ref, g2_ref,
         kk_ref, ka_ref, rk_ref, lnw_ref, lnb_ref, mg_ref,
         o_ref, vf_out_ref, h_ref, prev_ref, lw_s, a_s, b_s, k_s, r_s, v_s, y_s,
         m_s, n_s, p_s, z_s) = refs
    tb = r_ref.shape[0]
    ch = RWKV_CHUNK
    nh = GROUP_W // HEAD_W
    if reset:
        h_ref[...] = jnp.zeros_like(h_ref)
        prev_ref[...] = jnp.zeros_like(prev_ref)
        return

    row0 = lax.broadcasted_iota(jnp.int32, (tb, 1), 0) == 0

    def mixed(ref, j):
        p = ref[...]
        lanes = slice(j * GROUP_W, (j + 1) * GROUP_W)
        sh = jnp.where(row0, prev_ref[0:1, lanes], pltpu.roll(p, 1, axis=0))
        prev_new = p[tb - 1:tb, :]
        return p + mu_ref[0:1, lanes] * (sh - p), prev_new

    r, pr_ = mixed(r_ref, 0)
    k, pk_ = mixed(k_ref, 1)
    v, pv_ = mixed(v_ref, 2)
    lo, pl_ = mixed(lo_ref, 3)
    for j, pn in enumerate((pr_, pk_, pv_, pl_)):
        prev_ref[0:1, j * GROUP_W:(j + 1) * GROUP_W] = pn

    bd = _block_diag_mask(GROUP_W, HEAD_W)
    bd_bf16 = bd.astype(BF16)

    w_raw = -_softplus(-(w0_ref[...] + _mm(jnp.tanh(lo), w2_ref[...]))) - 0.5
    lw_s[...] = -jnp.exp(w_raw)
    alr = jax.nn.sigmoid(a0_ref[...] + _mm(lo, a2_ref[...]))
    g = _mm(jax.nn.sigmoid(lo), g2_ref[...])
    kkr = k * kk_ref[...]
    kk = kkr / jnp.maximum(jnp.sqrt(_head_sum(kkr * kkr, bd_bf16)), 1e-12)
    k = k * (1.0 + (alr - 1.0) * ka_ref[...])
    if has_vmix:
        gate = jax.nn.sigmoid(v0_ref[...] + _mm(_mm(v, v1_ref[...]), v2_ref[...]))
        v = v + (vf_ref[...] - v) * gate
    else:
        vf_out_ref[...] = v
    a_s[...] = -kk
    b_s[...] = kk * alr
    k_s[...] = k
    r_s[...] = r
    v_s[...] = v
    yield

    n4 = nh * ch
    ri = lax.broadcasted_iota(jnp.int32, (n4, n4), 0)
    ci = lax.broadcasted_iota(jnp.int32, (n4, n4), 1)
    same_head = (ri // ch) == (ci // ch)
    strict = same_head & ((ri % ch) > (ci % ch))
    incl = same_head & ((ri % ch) >= (ci % ch))
    eye = (ri == ci).astype(F32)
    same_blk = {b: (ri // b) == (ci // b) for b in (8, 16, 32, 64)}
    hm = ((lax.broadcasted_iota(jnp.int32, (n4, GROUP_W), 0) // ch)
          == (lax.broadcasted_iota(jnp.int32, (n4, GROUP_W), 1) // HEAD_W))
    tri = (lax.broadcasted_iota(jnp.int32, (ch, ch), 0)
           >= lax.broadcasted_iota(jnp.int32, (ch, ch), 1)).astype(BF16)

    def stack(x):
        return jnp.where(hm, jnp.concatenate([x] * nh, axis=0), 0.0)

    def dot(x, y, dims=_NN):
        return lax.dot_general(x.astype(BF16), y.astype(BF16), (dims, ((), ())),
                               preferred_element_type=F32)

    chunks = range(tb // ch)
    pre = []
    for c in chunks:
        sl = slice(c * ch, (c + 1) * ch)
        lw = lw_s[sl, :]
        cl = _exact_lhs_mm(tri, lw)
        cl_end = cl[ch - 1:ch, :]
        e_in, e_ex = jnp.exp(cl), jnp.exp(cl - lw)
        e_neg, e_end = jnp.exp(-cl), jnp.exp(cl_end - cl)
        av, bv, kv, rv, vv = a_s[sl, :], b_s[sl, :], k_s[sl, :], r_s[sl, :], v_s[sl, :]
        pre.append(dict(
            at4=stack(av * e_ex), rt4=stack(rv * e_in), v4=stack(vv),
            bt4=jnp.concatenate([bv * e_neg] * nh, axis=0),
            kt4=jnp.concatenate([kv * e_neg] * nh, axis=0),
            bh4=stack(bv * e_end), kh4=stack(kv * e_end), g_end=jnp.exp(cl_end)))
        yield
    for d in pre:
        ar = jnp.concatenate([d["at4"], d["rt4"]], axis=0)
        sb = dot(ar, d["bt4"], _NT)
        sk = dot(ar, d["kt4"], _NT)
        d["l_ab"] = jnp.where(strict, sb[:n4], 0.0)
        d["l_ak"] = jnp.where(strict, sk[:n4], 0.0)
        d["l_rb"] = jnp.where(incl, sb[n4:], 0.0)
        d["l_rk"] = jnp.where(incl, sk[n4:], 0.0)
        yield
    for d in pre:
        nb8 = jnp.where(same_blk[8], d["l_ab"], 0.0)
        d["tinv"] = eye + nb8
        d["pw"] = dot(nb8, nb8)
        yield
    for d in pre:
        d["tinv"] = d["tinv"] + dot(d["tinv"], d["pw"])
        d["pw"] = dot(d["pw"], d["pw"])
        yield
    for d in pre:
        d["tinv"] = d["tinv"] + dot(d["tinv"], d["pw"])
        yield
    blk = 8
    while blk < ch:
        for d in pre:
            d["pw"] = dot(jnp.where(same_blk[2 * blk] & ~same_blk[blk], d["l_ab"], 0.0), d["tinv"])
            yield
        for d in pre:
            d["tinv"] = d["tinv"] + dot(d["tinv"], d["pw"])
            yield
        blk *= 2
    for d in pre:
        d["lakv"] = dot(d["l_ak"], d["v4"])
        d["lrkv"] = dot(d["l_rk"], d["v4"])
        d["khv"] = dot(d["kh4"], d["v4"], _TN)
        yield
    for d in pre:
        d["x12"] = dot(d["tinv"], jnp.concatenate([d["at4"], d["lakv"]], axis=1))
        yield
    for c, d in zip(chunks, pre):
        mn = dot(d["bh4"], d["x12"], _TN)
        pz = dot(d["l_rb"], d["x12"])
        m_s[c] = jnp.where(ri == ci, d["g_end"], 0.0) + mn[:, :GROUP_W]
        n_s[c] = mn[:, GROUP_W:] + d["khv"]
        p_s[c] = d["rt4"] + pz[:, :GROUP_W]
        z_s[c] = pz[:, GROUP_W:] + d["lrkv"]
        yield

    h = h_ref[...]
    for c in range(tb // ch):
        y4 = dot(p_s[c], h) + z_s[c]
        y = y4[0:ch]
        for j in range(1, nh):
            y = y + y4[j * ch:(j + 1) * ch]
        y_s[c * ch:(c + 1) * ch, :] = y
        h = _dg3(m_s[c], h, _NN) + n_s[c]
        yield
    h_ref[...] = h

    y = y_s[...]
    r, k, v = r_s[...], k_s[...], v_s[...]
    mean = _head_sum(y, bd_bf16) * (1.0 / HEAD_W)
    d = y - mean
    var = _head_sum(d * d, bd_bf16) * (1.0 / HEAD_W)
    y = d * lax.rsqrt(var + RWKV_GN_EPS) * lnw_ref[...] + lnb_ref[...]
    y = y + _head_sum(r * k * rk_ref[...], bd_bf16) * v
    o_ref[...] = _rms(y * g, mg_ref[...])


def _pad_rows(w, start, total=GROUP_W):
    out = jnp.zeros((total, w.shape[1]), F32).at[start:start + w.shape[0]].set(w)
    return out.astype(BF16)


def _rwkv_part(proj3, v_first, lyr, p, merge_g, tb):
    bsz, t, _ = proj3.shape
    has_vmix = v_first is not None
    row = lambda x: x.reshape(1, -1)
    w2 = _pad_rows(p["rwkv_w2"][lyr], 0)
    a2 = _pad_rows(p["rwkv_a2"][lyr], RWKV_W_LORA)
    g2 = _pad_rows(p["rwkv_g2"][lyr], RWKV_W_LORA + RWKV_A_LORA)
    args = [proj3, proj3, proj3, proj3]
    specs = [_col_spec(tb, COL_R), _col_spec(tb, COL_K), _col_spec(tb, COL_V), _col_spec(tb, COL_LORA)]
    if has_vmix:
        args.append(v_first)
        specs.append(pl.BlockSpec((None, tb, GROUP_W), lambda b, i: (b, i, 0)))
    args += [row(p["rwkv_mu"][lyr]), row(p["rwkv_w0"][lyr]), w2, row(p["rwkv_a0"][lyr]), a2, g2,
             row(p["rwkv_k_k"][lyr]), row(p["rwkv_k_a"][lyr]), row(p["rwkv_r_k"][lyr]),
             row(p["rwkv_ln_w"][lyr]), row(p["rwkv_ln_b"][lyr]), row(merge_g)]
    sq = _full_spec((GROUP_W, GROUP_W))
    specs += [_row_spec(4 * GROUP_W), _row_spec(), sq, _row_spec(), sq, sq,
              _row_spec(), _row_spec(), _row_spec(), _row_spec(), _row_spec(), _row_spec()]
    if has_vmix:
        v1 = jnp.zeros((GROUP_W, 128), F32).at[:, :RWKV_V_LORA].set(p["rwkv_v1"][lyr - 1]).astype(BF16)
        v2 = jnp.zeros((128, GROUP_W), F32).at[:RWKV_V_LORA].set(p["rwkv_v2"][lyr - 1]).astype(BF16)
        args += [row(p["rwkv_v0"][lyr - 1]), v1, v2]
        specs += [_row_spec(), _full_spec((GROUP_W, 128)), _full_spec((128, GROUP_W))]
    out_sds = jax.ShapeDtypeStruct((bsz, t, GROUP_W), F32)
    blk = pltpu.VMEM((tb, GROUP_W), F32)
    mats = pltpu.VMEM((tb // RWKV_CHUNK, GROUP_W, GROUP_W), F32)
    n_out = 1 if has_vmix else 2
    return dict(
        body=functools.partial(_rwkv_kernel, has_vmix=has_vmix), stages=15 * (tb // RWKV_CHUNK) + 2,
        args=args, in_specs=specs,
        out_specs=[_out_blk(tb)] * n_out, out_shape=[out_sds] * n_out,
        scratch=[pltpu.VMEM((GROUP_W, GROUP_W), F32), pltpu.VMEM((1, 4 * GROUP_W), F32),
                 blk, blk, blk, blk, blk, blk, blk, mats, mats, mats, mats])


def _bf16_bits(x):
    u = lax.bitcast_convert_type(x, jnp.uint32)
    r = u + jnp.uint32(0x7FFF) + ((u >> 16) & jnp.uint32(1))
    return r & jnp.uint32(0xFFFF0000)


def _merge_router_kernel(oa_ref, ob_ref, oc_ref, od_ref, h_ref, wo_ref, nw_ref, wr_ref, br_ref,
                         h1_ref, hn_ref, ids_ref, wts_ref, cnt_ref):
    mix = jnp.concatenate([oa_ref[...], ob_ref[...], oc_ref[...], od_ref[...]], axis=1)
    h1 = h_ref[...] + jnp.dot(mix.astype(BF16), wo_ref[...], preferred_element_type=F32)
    h1_ref[...] = h1
    hn = _rms(h1, nw_ref[...])
    half = hn.shape[1] // 2
    packed = _bf16_bits(hn[:, :half]) | (_bf16_bits(hn[:, half:]) >> 16)
    for j in range(hn_ref.shape[1]):
        hn_ref[:, j, :] = packed[:, j * LANES:(j + 1) * LANES]
    logits = _dg3(hn, wr_ref[...], _NN) + br_ref[...]
    lane = lax.broadcasted_iota(jnp.int32, logits.shape, 1)
    big = jnp.int32(ROUTER_LANES)
    is_c = (lane >= N_EXPERTS) & (lane < N_EXPERTS + MOE_GROUPS)
    cm = jnp.max(jnp.where(is_c, logits, NEG_BIG), axis=-1, keepdims=True)
    gsel = jnp.min(jnp.where(is_c & (logits == cm), lane, big), axis=-1, keepdims=True) - N_EXPERTS
    p_g = 1.0 / jnp.sum(jnp.where(is_c, jnp.exp(logits - cm), 0.0), axis=-1, keepdims=True)
    lo = gsel * MOE_PER_GROUP
    in_g = (lane >= lo) & (lane < lo + MOE_PER_GROUP)
    m1 = jnp.max(jnp.where(in_g, logits, NEG_BIG), axis=-1, keepdims=True)
    i1 = jnp.min(jnp.where(in_g & (logits == m1), lane, big), axis=-1, keepdims=True)
    in_g2 = in_g & (lane != i1)
    m2 = jnp.max(jnp.where(in_g2, logits, NEG_BIG), axis=-1, keepdims=True)
    i2 = jnp.min(jnp.where(in_g2 & (logits == m2), lane, big), axis=-1, keepdims=True)
    w1 = p_g / (1.0 + jnp.exp(m2 - m1))
    w2 = p_g - w1
    two = lax.broadcasted_iota(jnp.int32, ids_ref.shape, 1)
    ids_ref[...] = jnp.where(two == 0, i1, i2)
    wts_ref[...] = jnp.where(two == 0, w1, w2)

    @pl.when(pl.program_id(0) == 0)
    def _():
        cnt_ref[...] = jnp.zeros_like(cnt_ref)

    cnt_ref[...] += jnp.sum(((lane == i1) | (lane == i2)).astype(F32), axis=0, keepdims=True)


def _merge_router(outs, h2d, w_out_bf16, norm_w, w_router, b_router, tm):
    n, d = h2d.shape
    grp = pl.BlockSpec((tm, GROUP_W), lambda i: (i, 0))
    full = lambda a, b: pl.BlockSpec((a, b), lambda i: (0, 0))
    rowblk = lambda w: pl.BlockSpec((tm, w), lambda i: (i, 0))
    return pl.pallas_call(
        _merge_router_kernel,
        grid=(n // tm,),
        in_specs=[grp, grp, grp, grp, rowblk(d), full(d, d), full(1, d), full(d, ROUTER_LANES),
                  full(1, ROUTER_LANES)],
        out_specs=[rowblk(d), pl.BlockSpec((tm, d // 2 // LANES, LANES), lambda i: (i, 0, 0)),
                   rowblk(MOE_TOPK), rowblk(MOE_TOPK), full(1, ROUTER_LANES)],
        out_shape=[jax.ShapeDtypeStruct((n, d), F32),
                   jax.ShapeDtypeStruct((n, d // 2 // LANES, LANES), jnp.uint32),
                   jax.ShapeDtypeStruct((n, MOE_TOPK), jnp.int32),
                   jax.ShapeDtypeStruct((n, MOE_TOPK), F32),
                   jax.ShapeDtypeStruct((1, ROUTER_LANES), F32)],
        compiler_params=_cparams("arbitrary"),
        name="merge_router",
    )(*[o.reshape(n, GROUP_W) for o in outs], h2d, w_out_bf16, norm_w.reshape(1, d), w_router, b_router)


def _row_copy(src_hbm, src_row, dst_ref, dst_row, sem):
    return pltpu.make_async_copy(src_hbm.at[pl.ds(src_row, 1), :], dst_ref.at[pl.ds(dst_row, 1), :], sem)


def _position_kernel(ids_ref, base_ref, pos_ref, run_ref):
    tm = ids_ref.shape[0]

    @pl.when(pl.program_id(0) == 0)
    def _():
        run_ref[...] = jnp.zeros_like(run_ref)

    i1, i2 = ids_ref[:, 0:1], ids_ref[:, 1:2]
    lane = lax.broadcasted_iota(jnp.int32, (tm, ROUTER_LANES), 1)
    hit = ((lane == i1) | (lane == i2)).astype(BF16)
    earlier = (lax.broadcasted_iota(jnp.int32, (tm, tm), 0)
               > lax.broadcasted_iota(jnp.int32, (tm, tm), 1)).astype(BF16)
    rank = jnp.dot(earlier, hit, preferred_element_type=F32) + run_ref[...]
    where_to = rank + base_ref[...]
    p1 = jnp.sum(jnp.where(lane == i1, where_to, 0.0), axis=-1, keepdims=True)
    p2 = jnp.sum(jnp.where(lane == i2, where_to, 0.0), axis=-1, keepdims=True)
    two = lax.broadcasted_iota(jnp.int32, pos_ref.shape, 1)
    pos_ref[...] = jnp.where(two == 0, p1, p2).astype(jnp.int32)
    run_ref[...] += jnp.sum(hit.astype(F32), axis=0, keepdims=True)


def _positions(ids, group_start, tm):
    n = ids.shape[0]
    return pl.pallas_call(
        _position_kernel,
        grid=(n // tm,),
        in_specs=[pl.BlockSpec((tm, MOE_TOPK), lambda i: (i, 0)),
                  pl.BlockSpec((1, ROUTER_LANES), lambda i: (0, 0))],
        out_specs=pl.BlockSpec((tm, MOE_TOPK), lambda i: (i, 0)),
        out_shape=jax.ShapeDtypeStruct((n, MOE_TOPK), jnp.int32),
        scratch_shapes=[pltpu.VMEM((1, ROUTER_LANES), F32)],
        compiler_params=_cparams("arbitrary"),
        name="moe_positions",
    )(ids, group_start)


def _dispatch_kernel(p0_ref, p1_ref, x_ref, xs_in_hbm, xs_hbm, sem):
    del xs_in_hbm
    n = x_ref.shape[0]

    def copies(i):
        return (pltpu.make_async_copy(x_ref.at[i], xs_hbm.at[p0_ref[i]], sem),
                pltpu.make_async_copy(x_ref.at[i], xs_hbm.at[p1_ref[i]], sem))

    def issue(i, c):
        for prio, cp in enumerate(copies(i)):
            cp.start(priority=prio)
        return c

    def drain(i, c):
        for cp in copies(i):
            cp.wait()
        return c

    lax.fori_loop(0, n, issue, 0, unroll=8)
    lax.fori_loop(0, n, drain, 0, unroll=8)


def _dispatch(x, pos0, pos1, n_rows, chunk):
    n, slabs, lanes = x.shape
    smem = lambda: pl.BlockSpec((chunk,), lambda s: (s,), memory_space=pltpu.SMEM)
    return pl.pallas_call(
        _dispatch_kernel,
        grid=(n // chunk,),
        in_specs=[smem(), smem(), pl.BlockSpec((chunk, slabs, lanes), lambda s: (s, 0, 0)),
                  pl.BlockSpec(memory_space=pl.ANY)],
        out_specs=pl.BlockSpec(memory_space=pl.ANY),
        out_shape=jax.ShapeDtypeStruct((n_rows, slabs, lanes), x.dtype),
        input_output_aliases={3: 0},
        scratch_shapes=[pltpu.SemaphoreType.DMA(())],
        compiler_params=_cparams("arbitrary"),
        name="moe_dispatch",
    )(pos0, pos1, x, jnp.zeros((n_rows, slabs, lanes), x.dtype))


def _slabs_to_rows(ref):
    return jnp.concatenate([ref[:, j, :] for j in range(ref.shape[1])], axis=1)


def _expert_kernel(te_ref, nt_ref, xs_ref, wg_ref, wu_ref, wd_ref, o_ref, wg_s, wu_s, wd_s):
    i = pl.program_id(0)

    @pl.when((i == 0) | (te_ref[i] != te_ref[jnp.maximum(i - 1, 0)]))
    def _():
        wg_s[...] = wg_ref[...].astype(BF16)
        wu_s[...] = wu_ref[...].astype(BF16)
        wd_s[...] = wd_ref[...].astype(BF16)

    @pl.when(i < nt_ref[0])
    def _():
        p = _slabs_to_rows(xs_ref)
        xa = lax.bitcast_convert_type(p & jnp.uint32(0xFFFF0000), F32).astype(BF16)
        xb = lax.bitcast_convert_type(p << 16, F32).astype(BF16)
        x = jnp.concatenate([xa, xb], axis=1)
        he = (_silu(jnp.dot(x, wg_s[...], preferred_element_type=F32))
              * jnp.dot(x, wu_s[...], preferred_element_type=F32))
        o_ref[...] = jnp.dot(he.astype(BF16), wd_s[...], preferred_element_type=F32)

    @pl.when(i >= nt_ref[0])
    def _():
        o_ref[...] = jnp.zeros_like(o_ref)


def _experts(xs, tile_expert, n_tiles_used, wg, wu, wd, layer, tm):
    p, slabs, lanes = xs.shape
    d = 2 * slabs * lanes
    pick = lambda i, te, nt: (layer, te[i], 0, 0)
    grid_spec = pltpu.PrefetchScalarGridSpec(
        num_scalar_prefetch=2,
        grid=(p // tm,),
        in_specs=[pl.BlockSpec((tm, slabs, lanes), lambda i, te, nt: (i, 0, 0)),
                  pl.BlockSpec((None, None, d, D_EXPERT), pick),
                  pl.BlockSpec((None, None, d, D_EXPERT), pick),
                  pl.BlockSpec((None, None, D_EXPERT, d), pick)],
        out_specs=pl.BlockSpec((tm, d), lambda i, te, nt: (i, 0)),
        scratch_shapes=[pltpu.VMEM((d, D_EXPERT), BF16), pltpu.VMEM((d, D_EXPERT), BF16),
                        pltpu.VMEM((D_EXPERT, d), BF16)],
    )
    return pl.pallas_call(
        _expert_kernel,
        grid_spec=grid_spec,
        out_shape=jax.ShapeDtypeStruct((p, d), F32),
        compiler_params=_cparams("arbitrary"),
        name="moe_experts",
    )(tile_expert, n_tiles_used, xs, wg, wu, wd)


def _combine_kernel(i0_ref, i1_ref, h1_ref, w_ref, fw_ref, ys_hbm, o_ref, buf0, buf1, sem, *, final_norm):
    n = o_ref.shape[0]

    def issue(i, c):
        _row_copy(ys_hbm, i0_ref[i], buf0, i, sem).start(priority=0)
        _row_copy(ys_hbm, i1_ref[i], buf1, i, sem).start(priority=1)
        return c

    def drain(i, c):
        _row_copy(ys_hbm, 0, buf0, i, sem).wait()
        _row_copy(ys_hbm, 0, buf1, i, sem).wait()
        return c

    lax.fori_loop(0, n, issue, 0, unroll=8)
    lax.fori_loop(0, n, drain, 0, unroll=8)
    out = h1_ref[...] + w_ref[:, 0:1] * buf0[...] + w_ref[:, 1:2] * buf1[...]
    if final_norm:
        out = _rms(out, fw_ref[...])
    o_ref[...] = out


def _combine(ys, pos0, pos1, wts, h1, final_w, final_norm, chunk):
    n, d = h1.shape
    smem = lambda: pl.BlockSpec((chunk,), lambda s: (s,), memory_space=pltpu.SMEM)
    return pl.pallas_call(
        functools.partial(_combine_kernel, final_norm=final_norm),
        grid=(n // chunk,),
        in_specs=[smem(), smem(), pl.BlockSpec((chunk, d), lambda s: (s, 0)),
                  pl.BlockSpec((chunk, MOE_TOPK), lambda s: (s, 0)),
                  pl.BlockSpec((1, d), lambda s: (0, 0)), pl.BlockSpec(memory_space=pl.ANY)],
        out_specs=pl.BlockSpec((chunk, d), lambda s: (s, 0)),
        out_shape=jax.ShapeDtypeStruct((n, d), F32),
        scratch_shapes=[pltpu.VMEM((chunk, d), F32), pltpu.VMEM((chunk, d), F32),
                        pltpu.SemaphoreType.DMA(())],
        compiler_params=_cparams("arbitrary"),
        name="moe_combine",
    )(pos0, pos1, h1, wts, final_w.reshape(1, d), ys)


def _moe(hn_packed, ids, wts, counts, h1, wg, wu, wd, layer, final_w, final_norm, tm):
    n = h1.shape[0]
    n_tiles = (n * MOE_TOPK) // tm + N_EXPERTS
    cnt = counts[0, :N_EXPERTS].astype(jnp.int32)
    padded = ((cnt + tm - 1) // tm) * tm
    ends = jnp.cumsum(padded)
    tile_start = jnp.arange(n_tiles, dtype=jnp.int32) * tm
    tile_expert = jnp.minimum(jnp.sum(ends[None, :] <= tile_start[:, None], axis=1),
                              N_EXPERTS - 1).astype(jnp.int32)
    n_tiles_used = (ends[-1] // tm).astype(jnp.int32).reshape(1)
    group_start = jnp.zeros((1, ROUTER_LANES), F32).at[0, :N_EXPERTS].set((ends - padded).astype(F32))
    pos = _positions(ids, group_start, _pick_div(n, 512))
    pos0, pos1 = pos[:, 0], pos[:, 1]
    xs = _dispatch(hn_packed, pos0, pos1, n_tiles * tm, _pick_div(n, 1024))
    ys = _experts(xs, tile_expert, n_tiles_used, wg, wu, wd, layer, tm)
    return _combine(ys, pos0, pos1, wts, h1, final_w, final_norm, _pick_div(n, 1024))


def _pick_div(n, pref):
    while n % pref:
        pref //= 2
    return pref


def _pick(n, pref):
    return pref if n % pref == 0 else n


def kernel(x, norm1_w, w_in, hgrn_lb_logits, hgrn_norm_w, s5_lambda_re, s5_lambda_im, s5_log_dt, s5_b_re, s5_b_im, s5_c_re, s5_c_im, s5_d, s5_w_glu, rwkv_mu, rwkv_w0, rwkv_w2, rwkv_a0, rwkv_a2, rwkv_g2, rwkv_k_k, rwkv_k_a, rwkv_r_k, rwkv_v0, rwkv_v1, rwkv_v2, rwkv_ln_w, rwkv_ln_b, lru_conv_w, lru_conv_b, lru_wa, lru_ba, lru_wx, lru_bx, lru_lambda, merge_gain, w_out, norm2_w, moe_coarse_w, moe_coarse_b, moe_fine_w, moe_fine_b, moe_w_gate, moe_w_up, moe_w_down, final_norm_w):
    bsz, t, d = x.shape
    n = bsz * t
    depth = w_in.shape[0]
    rw = dict(rwkv_mu=rwkv_mu, rwkv_w0=rwkv_w0, rwkv_w2=rwkv_w2, rwkv_a0=rwkv_a0, rwkv_a2=rwkv_a2,
              rwkv_g2=rwkv_g2, rwkv_k_k=rwkv_k_k, rwkv_k_a=rwkv_k_a, rwkv_r_k=rwkv_r_k,
              rwkv_v0=rwkv_v0, rwkv_v1=rwkv_v1, rwkv_v2=rwkv_v2, rwkv_ln_w=rwkv_ln_w,
              rwkv_ln_b=rwkv_ln_b)
    lb_all = jnp.cumsum(jax.nn.softmax(hgrn_lb_logits.astype(F32), axis=0), axis=0)
    lb_all = lb_all - lb_all[:1]

    tm_proj = _pick(n, 512)
    tm_moe = 512
    tb_mix = _pick(t, 256)
    tb_s5 = _pick(t, 512)

    h = x.reshape(n, d)
    v_first = None
    for l in range(depth):
        mg = merge_gain[l].reshape(4, GROUP_W)
        proj = _in_proj(h, norm1_w[l], w_in[l].astype(BF16), tm_proj)
        proj3 = proj.reshape(bsz, t, -1)
        mats = _s5_matrices(s5_lambda_re[l], s5_lambda_im[l], s5_log_dt[l], s5_b_re[l], s5_b_im[l],
                            s5_c_re[l], s5_c_im[l])
        o_b = _s5(proj3, mats, s5_d[l], s5_w_glu[l].astype(BF16), mg[1], tb_s5)
        parts = [_rwkv_part(proj3, v_first, l, rw, mg[2], tb_mix),
                 _hgrn_part(proj3, lb_all[l], hgrn_norm_w[l], mg[0], tb_mix),
                 _lru_part(proj3, lru_conv_w[l], lru_conv_b[l], _block_diag_weight(lru_wa[l]),
                           lru_ba[l], _block_diag_weight(lru_wx[l]), lru_bx[l], lru_lambda[l],
                           mg[3], tb_mix)]
        outs = _mixers(parts, bsz, t // tb_mix, "mixers")
        o_c, o_a, o_d = outs[0], outs[-2], outs[-1]
        if v_first is None:
            v_first = outs[1]
        w_router = jnp.concatenate(
            [moe_fine_w[l].transpose(1, 0, 2).reshape(d, N_EXPERTS), moe_coarse_w[l],
             jnp.zeros((d, ROUTER_LANES - N_EXPERTS - MOE_GROUPS), F32)], axis=1)
        b_router = jnp.concatenate(
            [moe_fine_b[l].reshape(N_EXPERTS), moe_coarse_b[l],
             jnp.zeros((ROUTER_LANES - N_EXPERTS - MOE_GROUPS,), F32)]).reshape(1, ROUTER_LANES)
        h1, hn, ids, wts, counts = _merge_router((o_a, o_b, o_c, o_d), h, w_out[l].astype(BF16),
                                                 norm2_w[l], w_router, b_router, tm_proj)
        h = _moe(hn, ids, wts, counts, h1, moe_w_gate, moe_w_up, moe_w_down, l,
                 final_norm_w, l == depth - 1, tm_moe)
    return h.reshape(bsz, t, d)
```

```python
import functools
import math

import jax
import jax.numpy as jnp
from jax import lax
from jax.experimental import pallas as pl
from jax.experimental.pallas import tpu as pltpu

F32 = jnp.float32
BF16 = jnp.bfloat16

D_MODEL = 1024
GROUP_W = 256
RMS_EPS = 1e-6
HEAD_W = 64
HGRN_CHUNK = 16
S5_CH = 16
S5_GROUPS = GROUP_W // S5_CH
S5_STATE = 64
S5_CHUNK = 16
S5_STEPS = 4
RWKV_CHUNK = 64
RWKV_GN_EPS = 64e-5
RWKV_W_LORA = 64
RWKV_A_LORA = 64
RWKV_G_LORA = 128
RWKV_V_LORA = 32
LRU_CONV = 4
LRU_C = 8.0
MOE_GROUPS = 4
MOE_PER_GROUP = 8
N_EXPERTS = MOE_GROUPS * MOE_PER_GROUP
MOE_TOPK = 2
D_EXPERT = 512
LANES = 128
ROUTER_LANES = LANES
NEG_BIG = -1e30
VMEM_LIMIT = 56 * 1024 * 1024

COL_HQ, COL_HF, COL_HI, COL_HG, COL_S5, COL_R, COL_K, COL_V, COL_LORA, COL_LG, COL_LX = range(11)


def _cparams(*sem):
    return pltpu.CompilerParams(dimension_semantics=sem, vmem_limit_bytes=VMEM_LIMIT)


def _mm(a, b):
    return jnp.dot(a.astype(BF16), b.astype(BF16), preferred_element_type=F32)


def _split2(x):
    hi = x.astype(BF16)
    lo = (x - hi.astype(F32)).astype(BF16)
    return hi, lo


def _dg3(a, b, dims):
    ah, al = _split2(a)
    bh, bl = _split2(b)
    d = lambda x, y: lax.dot_general(x, y, (dims, ((), ())), preferred_element_type=F32)
    return d(ah, bh) + d(ah, bl) + d(al, bh)


_NN = ((1,), (0,))
_NT = ((1,), (1,))
_TN = ((0,), (0,))


def _exact_lhs_mm(m_bf16, x):
    h1 = x.astype(BF16)
    r1 = x - h1.astype(F32)
    h2 = r1.astype(BF16)
    h3 = (r1 - h2.astype(F32)).astype(BF16)
    d = lambda y: jnp.dot(m_bf16, y, preferred_element_type=F32)
    return d(h1) + d(h2) + d(h3)


def _head_sum(x, bd_bf16):
    return jnp.dot(x.astype(BF16), bd_bf16, preferred_element_type=F32)


def _block_diag_mask(n, blk):
    r = lax.broadcasted_iota(jnp.int32, (n, n), 0) // blk
    c = lax.broadcasted_iota(jnp.int32, (n, n), 1) // blk
    return r == c


def _rms(x, w):
    return x * lax.rsqrt(jnp.mean(x * x, axis=-1, keepdims=True) + RMS_EPS) * w


def _silu(x):
    return x * jax.nn.sigmoid(x)


def _softplus(x):
    return jnp.maximum(x, 0.0) + jnp.log(1.0 + jnp.exp(-jnp.abs(x)))


def _in_proj_kernel(x_ref, nw_ref, w_ref, o_ref):
    y = _rms(x_ref[...], nw_ref[...])
    o_ref[...] = jnp.dot(y.astype(BF16), w_ref[...], preferred_element_type=F32)


def _in_proj(h2d, norm_w, w_bf16, tm):
    n, d = h2d.shape
    d_in = w_bf16.shape[1]
    return pl.pallas_call(
        _in_proj_kernel,
        grid=(n // tm,),
        in_specs=[pl.BlockSpec((tm, d), lambda i: (i, 0)),
                  pl.BlockSpec((1, d), lambda i: (0, 0)),
                  pl.BlockSpec((d, d_in), lambda i: (0, 0))],
        out_specs=pl.BlockSpec((tm, d_in), lambda i: (i, 0)),
        out_shape=jax.ShapeDtypeStruct((n, d_in), F32),
        compiler_params=_cparams("parallel"),
        name="in_proj",
    )(h2d, norm_w.reshape(1, d), w_bf16)


def _col_spec(tb, col):
    return pl.BlockSpec((None, tb, GROUP_W), lambda b, t: (b, t, col))


def _row_spec(width=GROUP_W):
    return pl.BlockSpec((1, width), lambda b, t: (0, 0))


def _full_spec(shape):
    return pl.BlockSpec(shape, lambda b, t: (0,) * len(shape))


def _hgrn_kernel(q_ref, f_ref, i_ref, g_ref, lb_ref, nw_ref, mg_ref, o_ref,
                 st_ref, q_s, k_s, v_s, lf_s, o_s, *, reset):
    ch = HGRN_CHUNK
    tb = q_ref.shape[0]
    if reset:
        st_ref[...] = jnp.zeros_like(st_ref)
        return

    lb = lb_ref[...]
    fx = f_ref[...]
    x1 = jnp.log(lb)
    x2 = jnp.log(1.0 - lb) - _softplus(-fx)
    m = jnp.maximum(x1, x2)
    lf_s[...] = m + jnp.log(jnp.exp(x1 - m) + jnp.exp(x2 - m))
    q_s[...] = _silu(q_ref[...])
    k_s[...] = (1.0 - lb) * jax.nn.sigmoid(-fx)
    v_s[...] = _silu(i_ref[...])
    yield

    bd = _block_diag_mask(GROUP_W, HEAD_W)
    bd_bf16 = bd.astype(BF16)
    tri = (lax.broadcasted_iota(jnp.int32, (ch, ch), 0)
           >= lax.broadcasted_iota(jnp.int32, (ch, ch), 1))
    tri_bf16 = tri.astype(BF16)
    tri3 = (lax.broadcasted_iota(jnp.int32, (ch, ch, 1), 0)
            <= lax.broadcasted_iota(jnp.int32, (ch, ch, 1), 1))

    st = st_ref[...]
    for c in range(tb // ch):
        sl = slice(c * ch, (c + 1) * ch)
        qc, kc, vc = q_s[sl, :], k_s[sl, :], v_s[sl, :]
        b = _exact_lhs_mm(tri_bf16, lf_s[sl, :])
        rel = b[None, :, :] - b[:, None, :]
        dec = jnp.exp(jnp.where(tri3, rel, NEG_BIG))
        p = (qc[None, :, :] * kc[:, None, :]) * dec
        sc = jnp.dot(p.reshape(ch * ch, GROUP_W).astype(BF16), bd_bf16,
                     preferred_element_type=F32).reshape(ch, ch, GROUP_W)
        o_intra = jnp.sum(sc * vc[:, None, :], axis=0)
        o_inter = lax.dot_general((qc * jnp.exp(b)).astype(BF16), st.astype(BF16),
                                  (_NT, ((), ())), preferred_element_type=F32)
        b_end = b[ch - 1:ch, :]
        kh = kc * jnp.exp(b_end - b)
        upd = lax.dot_general(vc.astype(BF16), kh.astype(BF16), (_TN, ((), ())),
                              preferred_element_type=F32)
        st = st * jnp.exp(b_end) + jnp.where(bd, upd, 0.0)
        o_s[sl, :] = o_intra + o_inter
        yield
    st_ref[...] = st

    o = o_s[...]
    ms = _head_sum(o * o, bd_bf16) * (1.0 / HEAD_W)
    o = o * lax.rsqrt(ms + RMS_EPS) * nw_ref[...] * _silu(g_ref[...])
    o_ref[...] = _rms(o, mg_ref[...])


def _out_blk(tb):
    return pl.BlockSpec((None, tb, GROUP_W), lambda b, i: (b, i, 0))


def _hgrn_part(proj3, lb, norm_w, merge_g, tb):
    bsz, t, _ = proj3.shape
    blk = pltpu.VMEM((tb, GROUP_W), F32)
    return dict(
        body=_hgrn_kernel, stages=tb // HGRN_CHUNK + 2,
        args=[proj3, proj3, proj3, proj3, lb.reshape(1, -1), norm_w.reshape(1, -1), merge_g.reshape(1, -1)],
        in_specs=[_col_spec(tb, COL_HQ), _col_spec(tb, COL_HF), _col_spec(tb, COL_HI),
                  _col_spec(tb, COL_HG), _row_spec(), _row_spec(), _row_spec()],
        out_specs=[_out_blk(tb)],
        out_shape=[jax.ShapeDtypeStruct((bsz, t, GROUP_W), F32)],
        scratch=[pltpu.VMEM((GROUP_W, GROUP_W), F32), blk, blk, blk, blk, blk])


def _mixer_kernel(*refs, parts):
    groups, i = [], 0
    for kind in range(3):
        for body, counts in parts:
            groups.append(refs[i:i + counts[kind]])
            i += counts[kind]
    k = len(parts)
    per_part = [groups[j] + groups[k + j] + groups[2 * k + j] for j in range(k)]

    @pl.when(pl.program_id(1) == 0)
    def _():
        for (body, _), r in zip(parts, per_part):
            for _step in body(*r, reset=True):
                pass

    runs = [[body(*r, reset=False), 0, counts[3]] for (body, counts), r in zip(parts, per_part)]
    while runs:
        run = min(runs, key=lambda x: x[1] / x[2])
        try:
            next(run[0])
            run[1] += 1
        except StopIteration:
            runs.remove(run)


def _mixers(parts, bsz, n_tblocks, name):
    cat = lambda key: [x for p in parts for x in p[key]]
    light = tuple((p["body"], (len(p["args"]), len(p["out_specs"]), len(p["scratch"]), p["stages"]))
                  for p in parts)
    return pl.pallas_call(
        functools.partial(_mixer_kernel, parts=light),
        grid=(bsz, n_tblocks),
        in_specs=cat("in_specs"),
        out_specs=cat("out_specs"),
        out_shape=cat("out_shape"),
        scratch_shapes=cat("scratch"),
        compiler_params=_cparams("parallel", "arbitrary"),
        name=name,
    )(*cat("args"))


def _lru_kernel(xg_ref, xr_ref, cw_ref, cb_ref, wa_ref, ba_ref, wx_ref, bx_ref, lam_ref, mg_ref,
                o_ref, buf_ref, h_ref, *, reset):
    tb = xr_ref.shape[0]
    if reset:
        buf_ref[0:8, :] = jnp.zeros((8, GROUP_W), F32)
        h_ref[...] = jnp.zeros_like(h_ref)
        return

    xr = xr_ref[...]
    buf_ref[8:8 + tb, :] = xr
    xc = cb_ref[...] + jnp.zeros_like(xr)
    for j in range(LRU_CONV):
        xc = xc + cw_ref[j:j + 1, :] * buf_ref[pl.ds(8 - (LRU_CONV - 1) + j, tb), :]
    buf_ref[0:8, :] = xr[tb - 8:tb, :]

    r = jax.nn.sigmoid(jnp.dot(xc.astype(BF16), wa_ref[...], preferred_element_type=F32) + ba_ref[...])
    gi = jax.nn.sigmoid(jnp.dot(xc.astype(BF16), wx_ref[...], preferred_element_type=F32) + bx_ref[...])
    log_a = -LRU_C * r * _softplus(-lam_ref[...])
    a = jnp.exp(log_a)
    x = jnp.sqrt(1.0 - jnp.exp(2.0 * log_a)) * (gi * xc)
    yield

    rows = lax.broadcasted_iota(jnp.int32, (tb, 1), 0)
    k = 1
    while k < tb:
        keep = rows >= k
        x = x + jnp.where(keep, a * pltpu.roll(x, k, axis=0), 0.0)
        a = jnp.where(keep, a * pltpu.roll(a, k, axis=0), a)
        k *= 2
        yield
    h = x + a * h_ref[...]
    h_ref[...] = h[tb - 1:tb, :]
    o_ref[...] = _rms(jax.nn.gelu(xg_ref[...]) * h, mg_ref[...])


def _lru_part(proj3, conv_w, conv_b, wa_bd, ba, wx_bd, bx, lam, merge_g, tb):
    bsz, t, _ = proj3.shape
    return dict(
        body=_lru_kernel, stages=tb.bit_length() + 1,
        args=[proj3, proj3, conv_w, conv_b.reshape(1, -1), wa_bd, ba.reshape(1, -1), wx_bd,
              bx.reshape(1, -1), lam.reshape(1, -1), merge_g.reshape(1, -1)],
        in_specs=[_col_spec(tb, COL_LG), _col_spec(tb, COL_LX), _full_spec((LRU_CONV, GROUP_W)),
                  _row_spec(), _full_spec((GROUP_W, GROUP_W)), _row_spec(),
                  _full_spec((GROUP_W, GROUP_W)), _row_spec(), _row_spec(), _row_spec()],
        out_specs=[_out_blk(tb)],
        out_shape=[jax.ShapeDtypeStruct((bsz, t, GROUP_W), F32)],
        scratch=[pltpu.VMEM((tb + 8, GROUP_W), F32), pltpu.VMEM((1, GROUP_W), F32)])


def _block_diag_weight(w):
    h, n, _ = w.shape
    eye = jnp.eye(h, dtype=w.dtype)
    return jnp.einsum('hij,hg->higj', w, eye).reshape(h * n, h * n).astype(BF16)


def _s5_matrices(lam_re, lam_im, log_dt, b_re, b_im, c_re, c_im):
    L, G, P, C = S5_CHUNK, S5_GROUPS, S5_STATE, S5_CH
    lr, li = lam_re.astype(F32), lam_im.astype(F32)
    dt = jnp.exp(log_dt.astype(F32))[:, None]
    mag = jnp.exp(lr * dt)
    a_re, a_im = mag * jnp.cos(li * dt), mag * jnp.sin(li * dt)
    den = lr * lr + li * li
    kap_re = ((a_re - 1.0) * lr + a_im * li) / den
    kap_im = (a_im * lr - (a_re - 1.0) * li) / den
    br, bi = b_re.astype(F32), b_im.astype(F32)
    bb_re = kap_re[..., None] * br - kap_im[..., None] * bi
    bb_im = kap_re[..., None] * bi + kap_im[..., None] * br
    cr, ci = c_re.astype(F32), c_im.astype(F32)
    eye = jnp.eye(G, dtype=F32)
    J = S5_STEPS
    kk = jnp.arange(J + 1, dtype=F32)[:, None, None]
    pmag = jnp.exp(kk * (lr * dt)[None])
    pw_re, pw_im = pmag * jnp.cos(kk * (li * dt)[None]), pmag * jnp.sin(kk * (li * dt)[None])
    ab_re = pw_re[:J, :, :, None] * bb_re[None] - pw_im[:J, :, :, None] * bb_im[None]
    ab_im = pw_re[:J, :, :, None] * bb_im[None] + pw_im[:J, :, :, None] * bb_re[None]
    rev = lambda x: jnp.stack([x[J - 1 - j] for j in range(J)])
    w_in = jnp.concatenate(
        [jnp.einsum('jgpc,gh->jgchp', rev(ab_re), eye).reshape(J * G * C, G * P),
         jnp.einsum('jgpc,gh->jgchp', rev(ab_im), eye).reshape(J * G * C, G * P)], axis=1)
    ca_re = jnp.einsum('gcp,jgp->jgcp', cr, pw_re[1:]) - jnp.einsum('gcp,jgp->jgcp', ci, pw_im[1:])
    ca_im = jnp.einsum('gcp,jgp->jgcp', cr, pw_im[1:]) + jnp.einsum('gcp,jgp->jgcp', ci, pw_re[1:])
    c_out = jnp.concatenate(
        [jnp.einsum('jgcp,gh->gpjhc', ca_re, eye).reshape(G * P, J * G * C),
         -jnp.einsum('jgcp,gh->gpjhc', ca_im, eye).reshape(G * P, J * G * C)], axis=0)
    taps = jnp.einsum('gop,kgpc->kgco', cr, ab_re) - jnp.einsum('gop,kgpc->kgco', ci, ab_im)
    none = jnp.zeros_like(taps[0])
    d_io = jnp.stack([jnp.stack([taps[j - i] if j >= i else none for j in range(J)])
                      for i in range(J)])
    d_io = jnp.einsum('ijgco,gh->igcjho', d_io, eye).reshape(J * G * C, J * G * C)
    a_grp = jnp.stack([pw_re[J].reshape(G * P), pw_im[J].reshape(G * P)])
    mag_l = jnp.exp(L * lr * dt)
    a_chunk = jnp.stack([(mag_l * jnp.cos(L * li * dt)).reshape(G * P),
                         (mag_l * jnp.sin(L * li * dt)).reshape(G * P)])
    return w_in.astype(BF16), c_out.astype(BF16), d_io.astype(BF16), a_grp, a_chunk


def _s5_kernel(u0_ref, u1_ref, w_ref, c_ref, dio_ref, a_ref, al_ref, d_ref, wg_ref, mg_ref, o_ref,
               carry_ref, x_s, st_s, y_s, bu_s):
    nb, tb, lanes = u0_ref.shape
    L, J = S5_CHUNK, S5_STEPS
    r = tb // L
    rows = nb * r
    ns = S5_GROUPS * S5_STATE

    @pl.when(pl.program_id(0) == 0)
    def _():
        carry_ref[...] = jnp.zeros_like(carry_ref)

    a_re, a_im = a_ref[0:1, :], a_ref[1:2, :]
    al_re, al_im = al_ref[0:1, :], al_ref[1:2, :]

    def inputs(q):
        parts = []
        for j in range(J):
            sl = pl.ds(q * J + j, r, stride=L)
            parts += [u0_ref[:, sl, :], u1_ref[:, sl, :]]
        return jnp.concatenate(parts, axis=-1).reshape(rows, J * 2 * lanes).astype(BF16)

    def advance(q):
        bu = bu_s[q]
        xr, xi = x_s[:, :ns], x_s[:, ns:]
        x_s[:, :ns] = a_re * xr - a_im * xi + bu[:, :ns]
        x_s[:, ns:] = a_re * xi + a_im * xr + bu[:, ns:]

    x_s[...] = jnp.zeros_like(x_s)
    for q in range(L // J):
        bu_s[q] = jnp.dot(inputs(q), w_ref[...], preferred_element_type=F32)
        advance(q)

    for b in range(nb):
        def hop(c, carry, b=b):
            xr, xi = carry
            row = pl.ds(b * r + c, 1)
            st_s[row, :] = jnp.concatenate([xr, xi], axis=1)
            p = x_s[row, :]
            return (al_re * xr - al_im * xi + p[:, :ns], al_re * xi + al_im * xr + p[:, ns:])

        xr, xi = lax.fori_loop(0, r, hop, (carry_ref[b:b + 1, :ns], carry_ref[b:b + 1, ns:]))
        carry_ref[b:b + 1, :] = jnp.concatenate([xr, xi], axis=1)

    x_s[...] = st_s[...]
    for q in range(L // J):
        y = (jnp.dot(x_s[...].astype(BF16), c_ref[...], preferred_element_type=F32)
             + jnp.dot(inputs(q), dio_ref[...], preferred_element_type=F32))
        for j in range(J):
            sl = pl.ds(q * J + j, r, stride=L)
            lo = j * 2 * lanes
            y_s[0, :, sl, :] = y[:, lo:lo + lanes].reshape(nb, r, lanes)
            y_s[1, :, sl, :] = y[:, lo + lanes:lo + 2 * lanes].reshape(nb, r, lanes)
        if q + 1 < L // J:
            advance(q)

    u = jnp.concatenate([u0_ref[...], u1_ref[...]], axis=-1).reshape(nb * tb, 2 * lanes)
    y = jnp.concatenate([y_s[0], y_s[1]], axis=-1).reshape(nb * tb, 2 * lanes)
    y = jax.nn.gelu(y + d_ref[...] * u)
    z = jnp.dot(y.astype(BF16), wg_ref[...], preferred_element_type=F32)
    out = _rms(z[:, :GROUP_W] * jax.nn.sigmoid(z[:, GROUP_W:]), mg_ref[...])
    o_ref[...] = out.reshape(nb, tb, GROUP_W)


def _s5(proj3, mats, d_skip, w_glu_bf16, merge_g, tb):
    bsz, t, _ = proj3.shape
    w_in, c_out, d_io, a_grp, a_chunk = mats
    ns = S5_GROUPS * S5_STATE
    jw = S5_STEPS * GROUP_W
    lanes = GROUP_W // 2
    rows = bsz * (tb // S5_CHUNK)
    half = lambda j: pl.BlockSpec((bsz, tb, lanes), lambda i, j=j: (0, i, 2 * COL_S5 + j))
    full = lambda a, b: pl.BlockSpec((a, b), lambda i: (0, 0))
    return pl.pallas_call(
        _s5_kernel,
        grid=(t // tb,),
        in_specs=[half(0), half(1), full(jw, 2 * ns), full(2 * ns, jw), full(jw, jw), full(2, ns),
                  full(2, ns), full(1, GROUP_W), full(GROUP_W, 2 * GROUP_W), full(1, GROUP_W)],
        out_specs=pl.BlockSpec((bsz, tb, GROUP_W), lambda i: (0, i, 0)),
        out_shape=jax.ShapeDtypeStruct((bsz, t, GROUP_W), F32),
        scratch_shapes=[pltpu.VMEM((bsz, 2 * ns), F32), pltpu.VMEM((rows, 2 * ns), F32),
                        pltpu.VMEM((rows, 2 * ns), F32), pltpu.VMEM((2, bsz, tb, lanes), F32),
                        pltpu.VMEM((S5_CHUNK // S5_STEPS, rows, 2 * ns), F32)],
        compiler_params=_cparams("arbitrary"),
        name="s5",
    )(proj3, proj3, w_in, c_out, d_io, a_grp, a_chunk, d_skip.reshape(1, -1), w_glu_bf16,
      merge_g.reshape(1, -1))


def _rwkv_kernel(*refs, has_vmix, reset):
    if has_vmix:
        (r_ref, k_ref, v_ref, lo_ref, vf_ref, mu_ref, w0_ref, w2_ref, a0_ref, a2_ref, g2_ref,
         kk_ref, ka_ref, rk_ref, lnw_ref, lnb_ref, mg_ref, v0_ref, v1_ref, v2_ref,
         o_ref, h_ref, prev_ref, lw_s, a_s, b_s, k_s, r_s, v_s, y_s, m_s, n_s, p_s, z_s) = refs
    else:
        (r_ref, k_ref, v_ref, lo_ref, mu_ref, w0_ref, w2_ref, a0_ref, a2_ref, g2_ref,
         kk_ref, ka_ref, rk_ref, lnw_ref, lnb_ref, mg_ref,
         o_ref, vf_out_ref, h_ref, prev_ref, lw_s, a_s, b_s, k_s, r_s, v_s, y_s,
         m_s, n_s, p_s, z_s) = refs
    tb = r_ref.shape[0]
    ch = RWKV_CHUNK
    nh = GROUP_W // HEAD_W
    if reset:
        h_ref[...] = jnp.zeros_like(h_ref)
        prev_ref[...] = jnp.zeros_like(prev_ref)
        return

    row0 = lax.broadcasted_iota(jnp.int32, (tb, 1), 0) == 0

    def mixed(ref, j):
        p = ref[...]
        lanes = slice(j * GROUP_W, (j + 1) * GROUP_W)
        sh = jnp.where(row0, prev_ref[0:1, lanes], pltpu.roll(p, 1, axis=0))
        prev_new = p[tb - 1:tb, :]
        return p + mu_ref[0:1, lanes] * (sh - p), prev_new

    r, pr_ = mixed(r_ref, 0)
    k, pk_ = mixed(k_ref, 1)
    v, pv_ = mixed(v_ref, 2)
    lo, pl_ = mixed(lo_ref, 3)
    for j, pn in enumerate((pr_, pk_, pv_, pl_)):
        prev_ref[0:1, j * GROUP_W:(j + 1) * GROUP_W] = pn

    bd = _block_diag_mask(GROUP_W, HEAD_W)
    bd_bf16 = bd.astype(BF16)

    lo_wa, lo_g = lo[:, :LANES], lo[:, LANES:]
    w_raw = -_softplus(-(w0_ref[...] + _mm(jnp.tanh(lo_wa), w2_ref[...]))) - 0.5
    lw_s[...] = -jnp.exp(w_raw)
    alr = jax.nn.sigmoid(a0_ref[...] + _mm(lo_wa, a2_ref[...]))
    g = _mm(jax.nn.sigmoid(lo_g), g2_ref[...])
    kkr = k * kk_ref[...]
    kk = kkr / jnp.maximum(jnp.sqrt(_head_sum(kkr * kkr, bd_bf16)), 1e-12)
    k = k * (1.0 + (alr - 1.0) * ka_ref[...])
    if has_vmix:
        gate = jax.nn.sigmoid(v0_ref[...] + _mm(_mm(v, v1_ref[...]), v2_ref[...]))
        v = v + (vf_ref[...] - v) * gate
    else:
        vf_out_ref[...] = v
    a_s[...] = -kk
    b_s[...] = kk * alr
    k_s[...] = k
    r_s[...] = r
    v_s[...] = v
    yield

    n4 = nh * ch
    ri = lax.broadcasted_iota(jnp.int32, (n4, n4), 0)
    ci = lax.broadcasted_iota(jnp.int32, (n4, n4), 1)
    same_head = (ri // ch) == (ci // ch)
    strict = same_head & ((ri % ch) > (ci % ch))
    incl = same_head & ((ri % ch) >= (ci % ch))
    eye = (ri == ci).astype(F32)
    same_blk = {b: (ri // b) == (ci // b) for b in (8, 16, 32, 64)}
    hm = ((lax.broadcasted_iota(jnp.int32, (n4, GROUP_W), 0) // ch)
          == (lax.broadcasted_iota(jnp.int32, (n4, GROUP_W), 1) // HEAD_W))
    tri = (lax.broadcasted_iota(jnp.int32, (ch, ch), 0)
           >= lax.broadcasted_iota(jnp.int32, (ch, ch), 1)).astype(BF16)

    def stack(x):
        return jnp.where(hm, jnp.concatenate([x] * nh, axis=0), 0.0)

    def dot(x, y, dims=_NN):
        return lax.dot_general(x.astype(BF16), y.astype(BF16), (dims, ((), ())),
                               preferred_element_type=F32)

    chunks = range(tb // ch)
    pre = []
    for c in chunks:
        sl = slice(c * ch, (c + 1) * ch)
        lw = lw_s[sl, :]
        cl = _exact_lhs_mm(tri, lw)
        cl_end = cl[ch - 1:ch, :]
        e_in, e_ex = jnp.exp(cl), jnp.exp(cl - lw)
        e_neg, e_end = jnp.exp(-cl), jnp.exp(cl_end - cl)
        av, bv, kv, rv, vv = a_s[sl, :], b_s[sl, :], k_s[sl, :], r_s[sl, :], v_s[sl, :]
        pre.append(dict(
            at4=stack(av * e_ex), rt4=stack(rv * e_in), v4=stack(vv),
            bt4=jnp.concatenate([bv * e_neg] * nh, axis=0),
            kt4=jnp.concatenate([kv * e_neg] * nh, axis=0),
            bh4=stack(bv * e_end), kh4=stack(kv * e_end), g_end=jnp.exp(cl_end)))
        yield
    for d in pre:
        ar = jnp.concatenate([d["at4"], d["rt4"]], axis=0)
        sb = dot(ar, d["bt4"], _NT)
        sk = dot(ar, d["kt4"], _NT)
        d["l_ab"] = jnp.where(strict, sb[:n4], 0.0)
        d["l_ak"] = jnp.where(strict, sk[:n4], 0.0)
        d["l_rb"] = jnp.where(incl, sb[n4:], 0.0)
        d["l_rk"] = jnp.where(incl, sk[n4:], 0.0)
        yield
    for d in pre:
        nb8 = jnp.where(same_blk[8], d["l_ab"], 0.0)
        d["tinv"] = eye + nb8
        d["pw"] = dot(nb8, nb8)
        yield
    for d in pre:
        d["tinv"] = d["tinv"] + dot(d["tinv"], d["pw"])
        d["pw"] = dot(d["pw"], d["pw"])
        yield
    for d in pre:
        d["tinv"] = d["tinv"] + dot(d["tinv"], d["pw"])
        yield
    blk = 8
    while blk < ch:
        for d in pre:
            d["pw"] = dot(jnp.where(same_blk[2 * blk] & ~same_blk[blk], d["l_ab"], 0.0), d["tinv"])
            yield
        for d in pre:
            d["tinv"] = d["tinv"] + dot(d["tinv"], d["pw"])
            yield
        blk *= 2
    for d in pre:
        d["lakv"] = dot(d["l_ak"], d["v4"])
        d["lrkv"] = dot(d["l_rk"], d["v4"])
        d["khv"] = dot(d["kh4"], d["v4"], _TN)
        yield
    for d in pre:
        d["x12"] = dot(d["tinv"], jnp.concatenate([d["at4"], d["lakv"]], axis=1))
        yield
    for c, d in zip(chunks, pre):
        mn = dot(d["bh4"], d["x12"], _TN)
        pz = dot(d["l_rb"], d["x12"])
        m_s[c] = jnp.where(ri == ci, d["g_end"], 0.0) + mn[:, :GROUP_W]
        n_s[c] = mn[:, GROUP_W:] + d["khv"]
        p_s[c] = d["rt4"] + pz[:, :GROUP_W]
        z_s[c] = pz[:, GROUP_W:] + d["lrkv"]
        yield

    h = h_ref[...]
    for c in range(tb // ch):
        y4 = dot(p_s[c], h) + z_s[c]
        y = y4[0:ch]
        for j in range(1, nh):
            y = y + y4[j * ch:(j + 1) * ch]
        y_s[c * ch:(c + 1) * ch, :] = y
        h = dot(m_s[c], h) + n_s[c]
        yield
    h_ref[...] = h

    y = y_s[...]
    r, k, v = r_s[...], k_s[...], v_s[...]
    mean = _head_sum(y, bd_bf16) * (1.0 / HEAD_W)
    d = y - mean
    var = _head_sum(d * d, bd_bf16) * (1.0 / HEAD_W)
    y = d * lax.rsqrt(var + RWKV_GN_EPS) * lnw_ref[...] + lnb_ref[...]
    y = y + _head_sum(r * k * rk_ref[...], bd_bf16) * v
    o_ref[...] = _rms(y * g, mg_ref[...])


def _pad_rows(w, start, total=LANES):
    out = jnp.zeros((total, w.shape[1]), F32).at[start:start + w.shape[0]].set(w)
    return out.astype(BF16)


def _rwkv_part(proj3, v_first, lyr, p, merge_g, tb):
    bsz, t, _ = proj3.shape
    has_vmix = v_first is not None
    row = lambda x: x.reshape(1, -1)
    w2 = _pad_rows(p["rwkv_w2"][lyr], 0)
    a2 = _pad_rows(p["rwkv_a2"][lyr], RWKV_W_LORA)
    g2 = p["rwkv_g2"][lyr].astype(BF16)
    args = [proj3, proj3, proj3, proj3]
    specs = [_col_spec(tb, COL_R), _col_spec(tb, COL_K), _col_spec(tb, COL_V), _col_spec(tb, COL_LORA)]
    if has_vmix:
        args.append(v_first)
        specs.append(pl.BlockSpec((None, tb, GROUP_W), lambda b, i: (b, i, 0)))
    args += [row(p["rwkv_mu"][lyr]), row(p["rwkv_w0"][lyr]), w2, row(p["rwkv_a0"][lyr]), a2, g2,
             row(p["rwkv_k_k"][lyr]), row(p["rwkv_k_a"][lyr]), row(p["rwkv_r_k"][lyr]),
             row(p["rwkv_ln_w"][lyr]), row(p["rwkv_ln_b"][lyr]), row(merge_g)]
    sq = _full_spec((LANES, GROUP_W))
    specs += [_row_spec(4 * GROUP_W), _row_spec(), sq, _row_spec(), sq, sq,
              _row_spec(), _row_spec(), _row_spec(), _row_spec(), _row_spec(), _row_spec()]
    if has_vmix:
        v1 = jnp.zeros((GROUP_W, 128), F32).at[:, :RWKV_V_LORA].set(p["rwkv_v1"][lyr - 1]).astype(BF16)
        v2 = jnp.zeros((128, GROUP_W), F32).at[:RWKV_V_LORA].set(p["rwkv_v2"][lyr - 1]).astype(BF16)
        args += [row(p["rwkv_v0"][lyr - 1]), v1, v2]
        specs += [_row_spec(), _full_spec((GROUP_W, 128)), _full_spec((128, GROUP_W))]
    out_sds = jax.ShapeDtypeStruct((bsz, t, GROUP_W), F32)
    blk = pltpu.VMEM((tb, GROUP_W), F32)
    mats = pltpu.VMEM((tb // RWKV_CHUNK, GROUP_W, GROUP_W), F32)
    n_out = 1 if has_vmix else 2
    return dict(
        body=functools.partial(_rwkv_kernel, has_vmix=has_vmix), stages=15 * (tb // RWKV_CHUNK) + 2,
        args=args, in_specs=specs,
        out_specs=[_out_blk(tb)] * n_out, out_shape=[out_sds] * n_out,
        scratch=[pltpu.VMEM((GROUP_W, GROUP_W), F32), pltpu.VMEM((1, 4 * GROUP_W), F32),
                 blk, blk, blk, blk, blk, blk, blk, mats, mats, mats, mats])


def _bf16_bits(x):
    u = lax.bitcast_convert_type(x, jnp.uint32)
    r = u + jnp.uint32(0x7FFF) + ((u >> 16) & jnp.uint32(1))
    return r & jnp.uint32(0xFFFF0000)


def _slab_copies(rows_ref, slab_hbm, row0, sem):
    tm = rows_ref.shape[0]
    return [(rows_ref.at[:, pl.ds(j * LANES, LANES)], slab_hbm.at[pl.ds(row0, tm), j, :], sem)
            for j in range(slab_hbm.shape[1])]


def _merge_router_kernel(oa_ref, ob_ref, oc_ref, od_ref, h_ref, wo_ref, nw_ref, wr_ref, br_ref,
                         h1_ref, hn_hbm, ids_ref, wts_ref, cnt_ref, pk_s, sem):
    i = pl.program_id(0)
    steps = pl.num_programs(0)
    tm = h_ref.shape[0]
    slot = i % 2

    def writes(s, step):
        return [pltpu.make_async_copy(v, h, m) for v, h, m in
                _slab_copies(pk_s.at[s], hn_hbm, step * tm, sem.at[s])]

    mix = jnp.concatenate([oa_ref[...], ob_ref[...], oc_ref[...], od_ref[...]], axis=1)
    h1 = h_ref[...] + jnp.dot(mix.astype(BF16), wo_ref[...], preferred_element_type=F32)
    h1_ref[...] = h1
    hn = _rms(h1, nw_ref[...])
    half = hn.shape[1] // 2

    @pl.when(i >= 2)
    def _():
        for cp in writes(slot, i - 2):
            cp.wait()

    pk_s[slot] = _bf16_bits(hn[:, :half]) | (_bf16_bits(hn[:, half:]) >> 16)
    for cp in writes(slot, i):
        cp.start()

    @pl.when(i == steps - 1)
    def _():
        for cp in writes(slot, i):
            cp.wait()

    @pl.when((i == steps - 1) & (i >= 1))
    def _():
        for cp in writes(1 - slot, i - 1):
            cp.wait()

    logits = _dg3(hn, wr_ref[...], _NN) + br_ref[...]
    lane = lax.broadcasted_iota(jnp.int32, logits.shape, 1)
    big = jnp.int32(ROUTER_LANES)
    is_c = (lane >= N_EXPERTS) & (lane < N_EXPERTS + MOE_GROUPS)
    cm = jnp.max(jnp.where(is_c, logits, NEG_BIG), axis=-1, keepdims=True)
    gsel = jnp.min(jnp.where(is_c & (logits == cm), lane, big), axis=-1, keepdims=True) - N_EXPERTS
    p_g = 1.0 / jnp.sum(jnp.where(is_c, jnp.exp(logits - cm), 0.0), axis=-1, keepdims=True)
    lo = gsel * MOE_PER_GROUP
    in_g = (lane >= lo) & (lane < lo + MOE_PER_GROUP)
    m1 = jnp.max(jnp.where(in_g, logits, NEG_BIG), axis=-1, keepdims=True)
    i1 = jnp.min(jnp.where(in_g & (logits == m1), lane, big), axis=-1, keepdims=True)
    in_g2 = in_g & (lane != i1)
    m2 = jnp.max(jnp.where(in_g2, logits, NEG_BIG), axis=-1, keepdims=True)
    i2 = jnp.min(jnp.where(in_g2 & (logits == m2), lane, big), axis=-1, keepdims=True)
    w1 = p_g / (1.0 + jnp.exp(m2 - m1))
    w2 = p_g - w1
    two = lax.broadcasted_iota(jnp.int32, ids_ref.shape, 1)
    ids_ref[...] = jnp.where(two == 0, i1, i2)
    wts_ref[...] = jnp.where(two == 0, w1, w2)

    @pl.when(pl.program_id(0) == 0)
    def _():
        cnt_ref[...] = jnp.zeros_like(cnt_ref)

    cnt_ref[...] += jnp.sum(((lane == i1) | (lane == i2)).astype(F32), axis=0, keepdims=True)


def _merge_router(outs, h2d, w_out_bf16, norm_w, w_router, b_router, tm):
    n, d = h2d.shape
    grp = pl.BlockSpec((tm, GROUP_W), lambda i: (i, 0))
    full = lambda a, b: pl.BlockSpec((a, b), lambda i: (0, 0))
    rowblk = lambda w: pl.BlockSpec((tm, w), lambda i: (i, 0))
    return pl.pallas_call(
        _merge_router_kernel,
        grid=(n // tm,),
        in_specs=[grp, grp, grp, grp, rowblk(d), full(d, d), full(1, d), full(d, ROUTER_LANES),
                  full(1, ROUTER_LANES)],
        out_specs=[rowblk(d), pl.BlockSpec(memory_space=pl.ANY),
                   rowblk(MOE_TOPK), rowblk(MOE_TOPK), full(1, ROUTER_LANES)],
        out_shape=[jax.ShapeDtypeStruct((n, d), F32),
                   jax.ShapeDtypeStruct((n, d // 2 // LANES, LANES), jnp.uint32),
                   jax.ShapeDtypeStruct((n, MOE_TOPK), jnp.int32),
                   jax.ShapeDtypeStruct((n, MOE_TOPK), F32),
                   jax.ShapeDtypeStruct((1, ROUTER_LANES), F32)],
        scratch_shapes=[pltpu.VMEM((2, tm, d // 2), jnp.uint32), pltpu.SemaphoreType.DMA((2,))],
        compiler_params=_cparams("arbitrary"),
        name="merge_router",
    )(*[o.reshape(n, GROUP_W) for o in outs], h2d, w_out_bf16, norm_w.reshape(1, d), w_router, b_router)


def _row_copy(src_hbm, src_row, dst_ref, dst_row, sem):
    return pltpu.make_async_copy(src_hbm.at[pl.ds(src_row, 1), :], dst_ref.at[pl.ds(dst_row, 1), :], sem)


def _position_kernel(ids_ref, base_ref, pos_ref, run_ref):
    tm = ids_ref.shape[0]

    @pl.when(pl.program_id(0) == 0)
    def _():
        run_ref[...] = jnp.zeros_like(run_ref)

    i1, i2 = ids_ref[:, 0:1], ids_ref[:, 1:2]
    lane = lax.broadcasted_iota(jnp.int32, (tm, ROUTER_LANES), 1)
    hit = ((lane == i1) | (lane == i2)).astype(BF16)
    earlier = (lax.broadcasted_iota(jnp.int32, (tm, tm), 0)
               > lax.broadcasted_iota(jnp.int32, (tm, tm), 1)).astype(BF16)
    rank = jnp.dot(earlier, hit, preferred_element_type=F32) + run_ref[...]
    where_to = rank + base_ref[...]
    p1 = jnp.sum(jnp.where(lane == i1, where_to, 0.0), axis=-1, keepdims=True)
    p2 = jnp.sum(jnp.where(lane == i2, where_to, 0.0), axis=-1, keepdims=True)
    two = lax.broadcasted_iota(jnp.int32, pos_ref.shape, 1)
    pos_ref[...] = jnp.where(two == 0, p1, p2).astype(jnp.int32)
    run_ref[...] += jnp.sum(hit.astype(F32), axis=0, keepdims=True)


def _positions(ids, group_start, tm):
    n = ids.shape[0]
    return pl.pallas_call(
        _position_kernel,
        grid=(n // tm,),
        in_specs=[pl.BlockSpec((tm, MOE_TOPK), lambda i: (i, 0)),
                  pl.BlockSpec((1, ROUTER_LANES), lambda i: (0, 0))],
        out_specs=pl.BlockSpec((tm, MOE_TOPK), lambda i: (i, 0)),
        out_shape=jax.ShapeDtypeStruct((n, MOE_TOPK), jnp.int32),
        scratch_shapes=[pltpu.VMEM((1, ROUTER_LANES), F32)],
        compiler_params=_cparams("arbitrary"),
        name="moe_positions",
    )(ids, group_start)


def _dispatch_kernel(p0_ref, p1_ref, x_ref, xs_in_hbm, xs_hbm, sem):
    del xs_in_hbm
    n = x_ref.shape[0]

    def copies(i):
        return (pltpu.make_async_copy(x_ref.at[i], xs_hbm.at[p0_ref[i]], sem),
                pltpu.make_async_copy(x_ref.at[i], xs_hbm.at[p1_ref[i]], sem))

    def issue(i, c):
        for prio, cp in enumerate(copies(i)):
            cp.start(priority=prio)
        return c

    def drain(i, c):
        for cp in copies(i):
            cp.wait()
        return c

    lax.fori_loop(0, n, issue, 0, unroll=8)
    lax.fori_loop(0, n, drain, 0, unroll=8)


def _dispatch(x, pos0, pos1, n_rows, chunk):
    n, slabs, lanes = x.shape
    smem = lambda: pl.BlockSpec((chunk,), lambda s: (s,), memory_space=pltpu.SMEM)
    return pl.pallas_call(
        _dispatch_kernel,
        grid=(n // chunk,),
        in_specs=[smem(), smem(), pl.BlockSpec((chunk, slabs, lanes), lambda s: (s, 0, 0)),
                  pl.BlockSpec(memory_space=pl.ANY)],
        out_specs=pl.BlockSpec(memory_space=pl.ANY),
        out_shape=jax.ShapeDtypeStruct((n_rows, slabs, lanes), x.dtype),
        input_output_aliases={3: 0},
        scratch_shapes=[pltpu.SemaphoreType.DMA(())],
        compiler_params=_cparams("arbitrary"),
        name="moe_dispatch",
    )(pos0, pos1, x, jnp.zeros((n_rows, slabs, lanes), x.dtype))


def _expert_kernel(te_ref, nt_ref, xs_hbm, wg_ref, wu_ref, wd_ref, o_ref, wg_s, wu_s, wd_s, x_s, sem):
    i = pl.program_id(0)
    tm = o_ref.shape[0]
    slot = i % 2

    def reads(s, tile):
        return [pltpu.make_async_copy(h, v, m) for v, h, m in
                _slab_copies(x_s.at[s], xs_hbm, tile * tm, sem.at[s])]

    @pl.when(i == 0)
    def _():
        for cp in reads(0, 0):
            cp.start()

    @pl.when(i + 1 < pl.num_programs(0))
    def _():
        for cp in reads(1 - slot, i + 1):
            cp.start()

    @pl.when((i == 0) | (te_ref[i] != te_ref[jnp.maximum(i - 1, 0)]))
    def _():
        wg_s[...] = wg_ref[...].astype(BF16)
        wu_s[...] = wu_ref[...].astype(BF16)
        wd_s[...] = wd_ref[...].astype(BF16)

    for cp in reads(slot, i):
        cp.wait()

    @pl.when(i < nt_ref[0])
    def _():
        p = x_s[slot]
        xa = lax.bitcast_convert_type(p & jnp.uint32(0xFFFF0000), F32).astype(BF16)
        xb = lax.bitcast_convert_type(p << 16, F32).astype(BF16)
        x = jnp.concatenate([xa, xb], axis=1)
        he = (_silu(jnp.dot(x, wg_s[...], preferred_element_type=F32))
              * jnp.dot(x, wu_s[...], preferred_element_type=F32))
        o_ref[...] = jnp.dot(he.astype(BF16), wd_s[...], preferred_element_type=F32)

    @pl.when(i >= nt_ref[0])
    def _():
        o_ref[...] = jnp.zeros_like(o_ref)


def _experts(xs, tile_expert, n_tiles_used, wg, wu, wd, layer, tm):
    p, slabs, lanes = xs.shape
    d = 2 * slabs * lanes
    pick = lambda i, te, nt: (layer, te[i], 0, 0)
    grid_spec = pltpu.PrefetchScalarGridSpec(
        num_scalar_prefetch=2,
        grid=(p // tm,),
        in_specs=[pl.BlockSpec(memory_space=pl.ANY),
                  pl.BlockSpec((None, None, d, D_EXPERT), pick),
                  pl.BlockSpec((None, None, d, D_EXPERT), pick),
                  pl.BlockSpec((None, None, D_EXPERT, d), pick)],
        out_specs=pl.BlockSpec((tm, d), lambda i, te, nt: (i, 0)),
        scratch_shapes=[pltpu.VMEM((d, D_EXPERT), BF16), pltpu.VMEM((d, D_EXPERT), BF16),
                        pltpu.VMEM((D_EXPERT, d), BF16), pltpu.VMEM((2, tm, slabs * lanes), xs.dtype),
                        pltpu.SemaphoreType.DMA((2,))],
    )
    return pl.pallas_call(
        _expert_kernel,
        grid_spec=grid_spec,
        out_shape=jax.ShapeDtypeStruct((p, d), F32),
        compiler_params=_cparams("arbitrary"),
        name="moe_experts",
    )(tile_expert, n_tiles_used, xs, wg, wu, wd)


def _combine_kernel(i0_ref, i1_ref, h1_ref, w_ref, fw_ref, ys_hbm, o_ref, buf0, buf1, sem, *, final_norm):
    n = o_ref.shape[0]

    def issue(i, c):
        _row_copy(ys_hbm, i0_ref[i], buf0, i, sem).start(priority=0)
        _row_copy(ys_hbm, i1_ref[i], buf1, i, sem).start(priority=1)
        return c

    def drain(i, c):
        _row_copy(ys_hbm, 0, buf0, i, sem).wait()
        _row_copy(ys_hbm, 0, buf1, i, sem).wait()
        return c

    lax.fori_loop(0, n, issue, 0, unroll=8)
    lax.fori_loop(0, n, drain, 0, unroll=8)
    out = h1_ref[...] + w_ref[:, 0:1] * buf0[...] + w_ref[:, 1:2] * buf1[...]
    if final_norm:
        out = _rms(out, fw_ref[...])
    o_ref[...] = out


def _combine(ys, pos0, pos1, wts, h1, final_w, final_norm, chunk):
    n, d = h1.shape
    smem = lambda: pl.BlockSpec((chunk,), lambda s: (s,), memory_space=pltpu.SMEM)
    return pl.pallas_call(
        functools.partial(_combine_kernel, final_norm=final_norm),
        grid=(n // chunk,),
        in_specs=[smem(), smem(), pl.BlockSpec((chunk, d), lambda s: (s, 0)),
                  pl.BlockSpec((chunk, MOE_TOPK), lambda s: (s, 0)),
                  pl.BlockSpec((1, d), lambda s: (0, 0)), pl.BlockSpec(memory_space=pl.ANY)],
        out_specs=pl.BlockSpec((chunk, d), lambda s: (s, 0)),
        out_shape=jax.ShapeDtypeStruct((n, d), F32),
        scratch_shapes=[pltpu.VMEM((chunk, d), F32), pltpu.VMEM((chunk, d), F32),
                        pltpu.SemaphoreType.DMA(())],
        compiler_params=_cparams("arbitrary"),
        name="moe_combine",
    )(pos0, pos1, h1, wts, final_w.reshape(1, d), ys)


def _moe(hn_packed, ids, wts, counts, h1, wg, wu, wd, layer, final_w, final_norm, tm):
    n = h1.shape[0]
    n_tiles = (n * MOE_TOPK) // tm + N_EXPERTS
    cnt = counts[0, :N_EXPERTS].astype(jnp.int32)
    padded = ((cnt + tm - 1) // tm) * tm
    ends = jnp.cumsum(padded)
    tile_start = jnp.arange(n_tiles, dtype=jnp.int32) * tm
    tile_expert = jnp.minimum(jnp.sum(ends[None, :] <= tile_start[:, None], axis=1),
                              N_EXPERTS - 1).astype(jnp.int32)
    n_tiles_used = (ends[-1] // tm).astype(jnp.int32).reshape(1)
    group_start = jnp.zeros((1, ROUTER_LANES), F32).at[0, :N_EXPERTS].set((ends - padded).astype(F32))
    pos = _positions(ids, group_start, _pick_div(n, 512))
    pos0, pos1 = pos[:, 0], pos[:, 1]
    xs = _dispatch(hn_packed, pos0, pos1, n_tiles * tm, _pick_div(n, 1024))
    ys = _experts(xs, tile_expert, n_tiles_used, wg, wu, wd, layer, tm)
    return _combine(ys, pos0, pos1, wts, h1, final_w, final_norm, _pick_div(n, 1024))


def _pick_div(n, pref):
    while n % pref:
        pref //= 2
    return pref


def _pick(n, pref):
    return pref if n % pref == 0 else n


def kernel(x, norm1_w, w_in, hgrn_lb_logits, hgrn_norm_w, s5_lambda_re, s5_lambda_im, s5_log_dt, s5_b_re, s5_b_im, s5_c_re, s5_c_im, s5_d, s5_w_glu, rwkv_mu, rwkv_w0, rwkv_w2, rwkv_a0, rwkv_a2, rwkv_g2, rwkv_k_k, rwkv_k_a, rwkv_r_k, rwkv_v0, rwkv_v1, rwkv_v2, rwkv_ln_w, rwkv_ln_b, lru_conv_w, lru_conv_b, lru_wa, lru_ba, lru_wx, lru_bx, lru_lambda, merge_gain, w_out, norm2_w, moe_coarse_w, moe_coarse_b, moe_fine_w, moe_fine_b, moe_w_gate, moe_w_up, moe_w_down, final_norm_w):
    bsz, t, d = x.shape
    n = bsz * t
    depth = w_in.shape[0]
    rw = dict(rwkv_mu=rwkv_mu, rwkv_w0=rwkv_w0, rwkv_w2=rwkv_w2, rwkv_a0=rwkv_a0, rwkv_a2=rwkv_a2,
              rwkv_g2=rwkv_g2, rwkv_k_k=rwkv_k_k, rwkv_k_a=rwkv_k_a, rwkv_r_k=rwkv_r_k,
              rwkv_v0=rwkv_v0, rwkv_v1=rwkv_v1, rwkv_v2=rwkv_v2, rwkv_ln_w=rwkv_ln_w,
              rwkv_ln_b=rwkv_ln_b)
    lb_all = jnp.cumsum(jax.nn.softmax(hgrn_lb_logits.astype(F32), axis=0), axis=0)
    lb_all = lb_all - lb_all[:1]

    tm_proj = _pick(n, 512)
    tm_moe = 512
    tb_mix = _pick(t, 256)
    tb_s5 = _pick(t, 512)

    h = x.reshape(n, d)
    v_first = None
    for l in range(depth):
        mg = merge_gain[l].reshape(4, GROUP_W)
        proj = _in_proj(h, norm1_w[l], w_in[l].astype(BF16), tm_proj)
        proj3 = proj.reshape(bsz, t, -1)
        mats = _s5_matrices(s5_lambda_re[l], s5_lambda_im[l], s5_log_dt[l], s5_b_re[l], s5_b_im[l],
                            s5_c_re[l], s5_c_im[l])
        o_b = _s5(proj3, mats, s5_d[l], s5_w_glu[l].astype(BF16), mg[1], tb_s5)
        parts = [_rwkv_part(proj3, v_first, l, rw, mg[2], tb_mix),
                 _hgrn_part(proj3, lb_all[l], hgrn_norm_w[l], mg[0], tb_mix),
                 _lru_part(proj3, lru_conv_w[l], lru_conv_b[l], _block_diag_weight(lru_wa[l]),
                           lru_ba[l], _block_diag_weight(lru_wx[l]), lru_bx[l], lru_lambda[l],
                           mg[3], tb_mix)]
        outs = _mixers(parts, bsz, t // tb_mix, "mixers")
        o_c, o_a, o_d = outs[0], outs[-2], outs[-1]
        if v_first is None:
            v_first = outs[1]
        w_router = jnp.concatenate(
            [moe_fine_w[l].transpose(1, 0, 2).reshape(d, N_EXPERTS), moe_coarse_w[l],
             jnp.zeros((d, ROUTER_LANES - N_EXPERTS - MOE_GROUPS), F32)], axis=1)
        b_router = jnp.concatenate(
            [moe_fine_b[l].reshape(N_EXPERTS), moe_coarse_b[l],
             jnp.zeros((ROUTER_LANES - N_EXPERTS - MOE_GROUPS,), F32)]).reshape(1, ROUTER_LANES)
        h1, hn, ids, wts, counts = _merge_router((o_a, o_b, o_c, o_d), h, w_out[l].astype(BF16),
                                                 norm2_w[l], w_router, b_router, tm_proj)
        h = _moe(hn, ids, wts, counts, h1, moe_w_gate, moe_w_up, moe_w_down, l,
                 final_norm_w, l == depth - 1, tm_moe)
    return h.reshape(bsz, t, d)
```

```python
import functools
import math

import jax
import jax.numpy as jnp
from jax import lax
from jax.experimental import pallas as pl
from jax.experimental.pallas import tpu as pltpu

F32 = jnp.float32
BF16 = jnp.bfloat16

D_MODEL = 1024
GROUP_W = 256
RMS_EPS = 1e-6
HEAD_W = 64
HGRN_CHUNK = 16
S5_CH = 16
S5_GROUPS = GROUP_W // S5_CH
S5_STATE = 64
S5_CHUNK = 16
S5_STEPS = 4
RWKV_CHUNK = 64
RWKV_GN_EPS = 64e-5
RWKV_W_LORA = 64
RWKV_A_LORA = 64
RWKV_G_LORA = 128
RWKV_V_LORA = 32
LRU_CONV = 4
LRU_C = 8.0
MOE_GROUPS = 4
MOE_PER_GROUP = 8
N_EXPERTS = MOE_GROUPS * MOE_PER_GROUP
MOE_TOPK = 2
D_EXPERT = 512
LANES = 128
ROUTER_LANES = LANES
NEG_BIG = -1e30
VMEM_LIMIT = 56 * 1024 * 1024

COL_HQ, COL_HF, COL_HI, COL_HG, COL_S5, COL_R, COL_K, COL_V, COL_LORA, COL_LG, COL_LX = range(11)
EXT_A, EXT_B, EXT_G = range(3)


def _cparams(*sem):
    return pltpu.CompilerParams(dimension_semantics=sem, vmem_limit_bytes=VMEM_LIMIT)


def _mm(a, b):
    return jnp.dot(a.astype(BF16), b.astype(BF16), preferred_element_type=F32)


def _split2(x):
    hi = x.astype(BF16)
    lo = (x - hi.astype(F32)).astype(BF16)
    return hi, lo


def _dg3(a, b, dims):
    ah, al = _split2(a)
    bh, bl = _split2(b)
    d = lambda x, y: lax.dot_general(x, y, (dims, ((), ())), preferred_element_type=F32)
    return d(ah, bh) + d(ah, bl) + d(al, bh)


_NN = ((1,), (0,))
_NT = ((1,), (1,))
_TN = ((0,), (0,))


def _exact_lhs_mm(m_bf16, x):
    h1 = x.astype(BF16)
    r1 = x - h1.astype(F32)
    h2 = r1.astype(BF16)
    h3 = (r1 - h2.astype(F32)).astype(BF16)
    d = lambda y: jnp.dot(m_bf16, y, preferred_element_type=F32)
    return d(h1) + d(h2) + d(h3)


def _head_sum(x, bd_bf16):
    return jnp.dot(x.astype(BF16), bd_bf16, preferred_element_type=F32)


def _block_diag_mask(n, blk):
    r = lax.broadcasted_iota(jnp.int32, (n, n), 0) // blk
    c = lax.broadcasted_iota(jnp.int32, (n, n), 1) // blk
    return r == c


def _rms(x, w):
    return x * lax.rsqrt(jnp.mean(x * x, axis=-1, keepdims=True) + RMS_EPS) * w


def _silu(x):
    return x * jax.nn.sigmoid(x)


def _softplus(x):
    return jnp.maximum(x, 0.0) + jnp.log(1.0 + jnp.exp(-jnp.abs(x)))


def _in_proj_kernel(*refs, has_vmix, tiles_per_seq):
    if has_vmix:
        (x_ref, nw_ref, w_ref, mu_ref, w0_ref, w2_ref, a0_ref, a2_ref, g2_ref, kk_ref, ka_ref,
         vf_ref, v0_ref, v1_ref, v2_ref, o_ref, o2_ref, prev_s) = refs
    else:
        (x_ref, nw_ref, w_ref, mu_ref, w0_ref, w2_ref, a0_ref, a2_ref, g2_ref, kk_ref, ka_ref,
         o_ref, o2_ref, vf_out_ref, prev_s) = refs
    tm = x_ref.shape[0]
    gw = GROUP_W
    c0, c1 = COL_R * gw, (COL_LORA + 1) * gw
    y = _rms(x_ref[...], nw_ref[...]).astype(BF16)
    raw = jnp.dot(y, w_ref[:, c0:c1], preferred_element_type=F32)
    o_ref[:, :c0] = jnp.dot(y, w_ref[:, :c0], preferred_element_type=F32)
    o_ref[:, c1:] = jnp.dot(y, w_ref[:, c1:], preferred_element_type=F32)

    first = (pl.program_id(0) % tiles_per_seq) == 0
    prev = jnp.where(first, 0.0, prev_s[...])
    row0 = lax.broadcasted_iota(jnp.int32, (tm, 1), 0) == 0
    sh = jnp.where(row0, prev, pltpu.roll(raw, 1, axis=0))
    prev_s[...] = raw[tm - 1:tm, :]
    pf = raw + mu_ref[...] * (sh - raw)
    r, k, v, lo = (pf[:, j * GROUP_W:(j + 1) * GROUP_W] for j in range(4))

    bd_bf16 = _block_diag_mask(GROUP_W, HEAD_W).astype(BF16)
    lo_wa, lo_g = lo[:, :LANES], lo[:, LANES:]
    w_raw = -_softplus(-(w0_ref[...] + _mm(jnp.tanh(lo_wa), w2_ref[...]))) - 0.5
    alr = jax.nn.sigmoid(a0_ref[...] + _mm(lo_wa, a2_ref[...]))
    kkr = k * kk_ref[...]
    kk = kkr / jnp.maximum(jnp.sqrt(_head_sum(kkr * kkr, bd_bf16)), 1e-12)
    if has_vmix:
        gate = jax.nn.sigmoid(v0_ref[...] + _mm(_mm(v, v1_ref[...]), v2_ref[...]))
        v = v + (vf_ref[...] - v) * gate
    else:
        vf_out_ref[...] = v
    o_ref[:, c0:c0 + GROUP_W] = r
    o_ref[:, c0 + GROUP_W:c0 + 2 * GROUP_W] = k * (1.0 + (alr - 1.0) * ka_ref[...])
    o_ref[:, c0 + 2 * GROUP_W:c0 + 3 * GROUP_W] = v
    o_ref[:, c0 + 3 * GROUP_W:c1] = -jnp.exp(w_raw)
    o2_ref[:, EXT_A * gw:(EXT_A + 1) * gw] = -kk
    o2_ref[:, EXT_B * gw:(EXT_B + 1) * gw] = kk * alr
    o2_ref[:, EXT_G * gw:(EXT_G + 1) * gw] = _mm(jax.nn.sigmoid(lo_g), g2_ref[...])


def _pad_rows(w, start, total=LANES):
    out = jnp.zeros((total, w.shape[1]), F32).at[start:start + w.shape[0]].set(w)
    return out.astype(BF16)


def _in_proj(h2d, norm_w, w_bf16, v_first, lyr, p, t, tm):
    n, d = h2d.shape
    d_in = w_bf16.shape[1]
    has_vmix = v_first is not None
    row = lambda x: x.reshape(1, -1)
    full = lambda a, b: pl.BlockSpec((a, b), lambda i: (0, 0))
    rowblk = lambda w: pl.BlockSpec((tm, w), lambda i: (i, 0))
    args = [h2d, row(norm_w), w_bf16, row(p["rwkv_mu"][lyr]), row(p["rwkv_w0"][lyr]),
            _pad_rows(p["rwkv_w2"][lyr], 0), row(p["rwkv_a0"][lyr]),
            _pad_rows(p["rwkv_a2"][lyr], RWKV_W_LORA), p["rwkv_g2"][lyr].astype(BF16),
            row(p["rwkv_k_k"][lyr]), row(p["rwkv_k_a"][lyr])]
    specs = [rowblk(d), full(1, d), full(d, d_in), full(1, 4 * GROUP_W), full(1, GROUP_W),
             full(LANES, GROUP_W), full(1, GROUP_W), full(LANES, GROUP_W), full(LANES, GROUP_W),
             full(1, GROUP_W), full(1, GROUP_W)]
    out_specs = [rowblk(d_in), rowblk(3 * GROUP_W)]
    out_shape = [jax.ShapeDtypeStruct((n, d_in), F32), jax.ShapeDtypeStruct((n, 3 * GROUP_W), F32)]
    if has_vmix:
        v1 = jnp.zeros((GROUP_W, LANES), F32).at[:, :RWKV_V_LORA].set(p["rwkv_v1"][lyr - 1]).astype(BF16)
        v2 = jnp.zeros((LANES, GROUP_W), F32).at[:RWKV_V_LORA].set(p["rwkv_v2"][lyr - 1]).astype(BF16)
        args += [v_first, row(p["rwkv_v0"][lyr - 1]), v1, v2]
        specs += [rowblk(GROUP_W), full(1, GROUP_W), full(GROUP_W, LANES), full(LANES, GROUP_W)]
    else:
        out_specs.append(rowblk(GROUP_W))
        out_shape.append(jax.ShapeDtypeStruct((n, GROUP_W), F32))
    res = pl.pallas_call(
        functools.partial(_in_proj_kernel, has_vmix=has_vmix, tiles_per_seq=t // tm),
        grid=(n // tm,),
        in_specs=specs,
        out_specs=out_specs,
        out_shape=out_shape,
        scratch_shapes=[pltpu.VMEM((1, 4 * GROUP_W), F32)],
        compiler_params=_cparams("arbitrary"),
        name="in_proj",
    )(*args)
    return res[0], res[1], (v_first if has_vmix else res[2])


def _col_spec(tb, col):
    return pl.BlockSpec((None, tb, GROUP_W), lambda b, t: (b, t, col))


def _row_spec(width=GROUP_W):
    return pl.BlockSpec((1, width), lambda b, t: (0, 0))


def _full_spec(shape):
    return pl.BlockSpec(shape, lambda b, t: (0,) * len(shape))


def _hgrn_kernel(q_ref, f_ref, i_ref, g_ref, lb_ref, nw_ref, mg_ref, o_ref,
                 st_ref, q_s, k_s, v_s, lf_s, o_s, *, reset):
    ch = HGRN_CHUNK
    tb = q_ref.shape[0]
    if reset:
        st_ref[...] = jnp.zeros_like(st_ref)
        return

    lb = lb_ref[...]
    fx = f_ref[...]
    x1 = jnp.log(lb)
    x2 = jnp.log(1.0 - lb) - _softplus(-fx)
    m = jnp.maximum(x1, x2)
    lf_s[...] = m + jnp.log(jnp.exp(x1 - m) + jnp.exp(x2 - m))
    q_s[...] = _silu(q_ref[...])
    k_s[...] = (1.0 - lb) * jax.nn.sigmoid(-fx)
    v_s[...] = _silu(i_ref[...])
    yield

    bd = _block_diag_mask(GROUP_W, HEAD_W)
    bd_bf16 = bd.astype(BF16)
    tri = (lax.broadcasted_iota(jnp.int32, (ch, ch), 0)
           >= lax.broadcasted_iota(jnp.int32, (ch, ch), 1))
    tri_bf16 = tri.astype(BF16)
    tri3 = (lax.broadcasted_iota(jnp.int32, (ch, ch, 1), 0)
            <= lax.broadcasted_iota(jnp.int32, (ch, ch, 1), 1))

    st = st_ref[...]
    for c in range(tb // ch):
        sl = slice(c * ch, (c + 1) * ch)
        qc, kc, vc = q_s[sl, :], k_s[sl, :], v_s[sl, :]
        b = _exact_lhs_mm(tri_bf16, lf_s[sl, :])
        rel = b[None, :, :] - b[:, None, :]
        dec = jnp.exp(jnp.where(tri3, rel, NEG_BIG))
        p = (qc[None, :, :] * kc[:, None, :]) * dec
        sc = jnp.dot(p.reshape(ch * ch, GROUP_W).astype(BF16), bd_bf16,
                     preferred_element_type=F32).reshape(ch, ch, GROUP_W)
        o_intra = jnp.sum(sc * vc[:, None, :], axis=0)
        o_inter = lax.dot_general((qc * jnp.exp(b)).astype(BF16), st.astype(BF16),
                                  (_NT, ((), ())), preferred_element_type=F32)
        b_end = b[ch - 1:ch, :]
        kh = kc * jnp.exp(b_end - b)
        upd = lax.dot_general(vc.astype(BF16), kh.astype(BF16), (_TN, ((), ())),
                              preferred_element_type=F32)
        st = st * jnp.exp(b_end) + jnp.where(bd, upd, 0.0)
        o_s[sl, :] = o_intra + o_inter
        yield
    st_ref[...] = st

    o = o_s[...]
    ms = _head_sum(o * o, bd_bf16) * (1.0 / HEAD_W)
    o = o * lax.rsqrt(ms + RMS_EPS) * nw_ref[...] * _silu(g_ref[...])
    o_ref[...] = _rms(o, mg_ref[...])


def _out_blk(tb):
    return pl.BlockSpec((None, tb, GROUP_W), lambda b, i: (b, i, 0))


def _hgrn_part(proj3, lb, norm_w, merge_g, tb):
    bsz, t, _ = proj3.shape
    blk = pltpu.VMEM((tb, GROUP_W), F32)
    return dict(
        body=_hgrn_kernel, stages=tb // HGRN_CHUNK + 2,
        args=[proj3, proj3, proj3, proj3, lb.reshape(1, -1), norm_w.reshape(1, -1), merge_g.reshape(1, -1)],
        in_specs=[_col_spec(tb, COL_HQ), _col_spec(tb, COL_HF), _col_spec(tb, COL_HI),
                  _col_spec(tb, COL_HG), _row_spec(), _row_spec(), _row_spec()],
        out_specs=[_out_blk(tb)],
        out_shape=[jax.ShapeDtypeStruct((bsz, t, GROUP_W), F32)],
        scratch=[pltpu.VMEM((GROUP_W, GROUP_W), F32), blk, blk, blk, blk, blk])


def _mixer_kernel(*refs, parts):
    groups, i = [], 0
    for kind in range(3):
        for body, counts in parts:
            groups.append(refs[i:i + counts[kind]])
            i += counts[kind]
    k = len(parts)
    per_part = [groups[j] + groups[k + j] + groups[2 * k + j] for j in range(k)]

    @pl.when(pl.program_id(1) == 0)
    def _():
        for (body, _), r in zip(parts, per_part):
            for _step in body(*r, reset=True):
                pass

    runs = [[body(*r, reset=False), 0, counts[3]] for (body, counts), r in zip(parts, per_part)]
    while runs:
        run = min(runs, key=lambda x: x[1] / x[2])
        try:
            next(run[0])
            run[1] += 1
        except StopIteration:
            runs.remove(run)


def _mixers(parts, bsz, n_tblocks, name):
    cat = lambda key: [x for p in parts for x in p[key]]
    light = tuple((p["body"], (len(p["args"]), len(p["out_specs"]), len(p["scratch"]), p["stages"]))
                  for p in parts)
    return pl.pallas_call(
        functools.partial(_mixer_kernel, parts=light),
        grid=(bsz, n_tblocks),
        in_specs=cat("in_specs"),
        out_specs=cat("out_specs"),
        out_shape=cat("out_shape"),
        scratch_shapes=cat("scratch"),
        compiler_params=_cparams("parallel", "arbitrary"),
        name=name,
    )(*cat("args"))


def _lru_kernel(xg_ref, xr_ref, cw_ref, cb_ref, wa_ref, ba_ref, wx_ref, bx_ref, lam_ref, mg_ref,
                o_ref, buf_ref, h_ref, *, reset):
    tb = xr_ref.shape[0]
    if reset:
        buf_ref[0:8, :] = jnp.zeros((8, GROUP_W), F32)
        h_ref[...] = jnp.zeros_like(h_ref)
        return

    xr = xr_ref[...]
    buf_ref[8:8 + tb, :] = xr
    xc = cb_ref[...] + jnp.zeros_like(xr)
    for j in range(LRU_CONV):
        xc = xc + cw_ref[j:j + 1, :] * buf_ref[pl.ds(8 - (LRU_CONV - 1) + j, tb), :]
    buf_ref[0:8, :] = xr[tb - 8:tb, :]

    r = jax.nn.sigmoid(jnp.dot(xc.astype(BF16), wa_ref[...], preferred_element_type=F32) + ba_ref[...])
    gi = jax.nn.sigmoid(jnp.dot(xc.astype(BF16), wx_ref[...], preferred_element_type=F32) + bx_ref[...])
    log_a = -LRU_C * r * _softplus(-lam_ref[...])
    a = jnp.exp(log_a)
    x = jnp.sqrt(1.0 - jnp.exp(2.0 * log_a)) * (gi * xc)
    yield

    rows = lax.broadcasted_iota(jnp.int32, (tb, 1), 0)
    k = 1
    while k < tb:
        keep = rows >= k
        x = x + jnp.where(keep, a * pltpu.roll(x, k, axis=0), 0.0)
        a = jnp.where(keep, a * pltpu.roll(a, k, axis=0), a)
        k *= 2
        yield
    h = x + a * h_ref[...]
    h_ref[...] = h[tb - 1:tb, :]
    o_ref[...] = _rms(jax.nn.gelu(xg_ref[...]) * h, mg_ref[...])


def _lru_part(proj3, conv_w, conv_b, wa_bd, ba, wx_bd, bx, lam, merge_g, tb):
    bsz, t, _ = proj3.shape
    return dict(
        body=_lru_kernel, stages=tb.bit_length() + 1,
        args=[proj3, proj3, conv_w, conv_b.reshape(1, -1), wa_bd, ba.reshape(1, -1), wx_bd,
              bx.reshape(1, -1), lam.reshape(1, -1), merge_g.reshape(1, -1)],
        in_specs=[_col_spec(tb, COL_LG), _col_spec(tb, COL_LX), _full_spec((LRU_CONV, GROUP_W)),
                  _row_spec(), _full_spec((GROUP_W, GROUP_W)), _row_spec(),
                  _full_spec((GROUP_W, GROUP_W)), _row_spec(), _row_spec(), _row_spec()],
        out_specs=[_out_blk(tb)],
        out_shape=[jax.ShapeDtypeStruct((bsz, t, GROUP_W), F32)],
        scratch=[pltpu.VMEM((tb + 8, GROUP_W), F32), pltpu.VMEM((1, GROUP_W), F32)])


def _block_diag_weight(w):
    h, n, _ = w.shape
    eye = jnp.eye(h, dtype=w.dtype)
    return jnp.einsum('hij,hg->higj', w, eye).reshape(h * n, h * n).astype(BF16)


def _s5_matrices(lam_re, lam_im, log_dt, b_re, b_im, c_re, c_im):
    L, G, P, C = S5_CHUNK, S5_GROUPS, S5_STATE, S5_CH
    lr, li = lam_re.astype(F32), lam_im.astype(F32)
    dt = jnp.exp(log_dt.astype(F32))[:, None]
    mag = jnp.exp(lr * dt)
    a_re, a_im = mag * jnp.cos(li * dt), mag * jnp.sin(li * dt)
    den = lr * lr + li * li
    kap_re = ((a_re - 1.0) * lr + a_im * li) / den
    kap_im = (a_im * lr - (a_re - 1.0) * li) / den
    br, bi = b_re.astype(F32), b_im.astype(F32)
    bb_re = kap_re[..., None] * br - kap_im[..., None] * bi
    bb_im = kap_re[..., None] * bi + kap_im[..., None] * br
    cr, ci = c_re.astype(F32), c_im.astype(F32)
    eye = jnp.eye(G, dtype=F32)
    J = S5_STEPS
    kk = jnp.arange(J + 1, dtype=F32)[:, None, None]
    pmag = jnp.exp(kk * (lr * dt)[None])
    pw_re, pw_im = pmag * jnp.cos(kk * (li * dt)[None]), pmag * jnp.sin(kk * (li * dt)[None])
    ab_re = pw_re[:J, :, :, None] * bb_re[None] - pw_im[:J, :, :, None] * bb_im[None]
    ab_im = pw_re[:J, :, :, None] * bb_im[None] + pw_im[:J, :, :, None] * bb_re[None]
    rev = lambda x: jnp.stack([x[J - 1 - j] for j in range(J)])
    w_in = jnp.concatenate(
        [jnp.einsum('jgpc,gh->jgchp', rev(ab_re), eye).reshape(J * G * C, G * P),
         jnp.einsum('jgpc,gh->jgchp', rev(ab_im), eye).reshape(J * G * C, G * P)], axis=1)
    ca_re = jnp.einsum('gcp,jgp->jgcp', cr, pw_re[1:]) - jnp.einsum('gcp,jgp->jgcp', ci, pw_im[1:])
    ca_im = jnp.einsum('gcp,jgp->jgcp', cr, pw_im[1:]) + jnp.einsum('gcp,jgp->jgcp', ci, pw_re[1:])
    c_out = jnp.concatenate(
        [jnp.einsum('jgcp,gh->gpjhc', ca_re, eye).reshape(G * P, J * G * C),
         -jnp.einsum('jgcp,gh->gpjhc', ca_im, eye).reshape(G * P, J * G * C)], axis=0)
    taps = jnp.einsum('gop,kgpc->kgco', cr, ab_re) - jnp.einsum('gop,kgpc->kgco', ci, ab_im)
    none = jnp.zeros_like(taps[0])
    d_io = jnp.stack([jnp.stack([taps[j - i] if j >= i else none for j in range(J)])
                      for i in range(J)])
    d_io = jnp.einsum('ijgco,gh->igcjho', d_io, eye).reshape(J * G * C, J * G * C)
    a_grp = jnp.stack([pw_re[J].reshape(G * P), pw_im[J].reshape(G * P)])
    mag_l = jnp.exp(L * lr * dt)
    a_chunk = jnp.stack([(mag_l * jnp.cos(L * li * dt)).reshape(G * P),
                         (mag_l * jnp.sin(L * li * dt)).reshape(G * P)])
    return w_in.astype(BF16), c_out.astype(BF16), d_io.astype(BF16), a_grp, a_chunk


def _s5_kernel(u0_ref, u1_ref, w_ref, c_ref, dio_ref, a_ref, al_ref, d_ref, wg_ref, mg_ref, o_ref,
               carry_ref, x_s, st_s, y_s, bu_s):
    nb, tb, lanes = u0_ref.shape
    L, J = S5_CHUNK, S5_STEPS
    r = tb // L
    rows = nb * r
    ns = S5_GROUPS * S5_STATE

    @pl.when(pl.program_id(0) == 0)
    def _():
        carry_ref[...] = jnp.zeros_like(carry_ref)

    a_re, a_im = a_ref[0:1, :], a_ref[1:2, :]
    al_re, al_im = al_ref[0:1, :], al_ref[1:2, :]

    def inputs(q):
        parts = []
        for j in range(J):
            sl = pl.ds(q * J + j, r, stride=L)
            parts += [u0_ref[:, sl, :], u1_ref[:, sl, :]]
        return jnp.concatenate(parts, axis=-1).reshape(rows, J * 2 * lanes).astype(BF16)

    def advance(q):
        bu = bu_s[q]
        xr, xi = x_s[:, :ns], x_s[:, ns:]
        x_s[:, :ns] = a_re * xr - a_im * xi + bu[:, :ns]
        x_s[:, ns:] = a_re * xi + a_im * xr + bu[:, ns:]

    x_s[...] = jnp.zeros_like(x_s)
    for q in range(L // J):
        bu_s[q] = jnp.dot(inputs(q), w_ref[...], preferred_element_type=F32)
        advance(q)

    for b in range(nb):
        def hop(c, carry, b=b):
            xr, xi = carry
            row = pl.ds(b * r + c, 1)
            st_s[row, :] = jnp.concatenate([xr, xi], axis=1)
            p = x_s[row, :]
            return (al_re * xr - al_im * xi + p[:, :ns], al_re * xi + al_im * xr + p[:, ns:])

        xr, xi = lax.fori_loop(0, r, hop, (carry_ref[b:b + 1, :ns], carry_ref[b:b + 1, ns:]))
        carry_ref[b:b + 1, :] = jnp.concatenate([xr, xi], axis=1)

    x_s[...] = st_s[...]
    for q in range(L // J):
        y = (jnp.dot(x_s[...].astype(BF16), c_ref[...], preferred_element_type=F32)
             + jnp.dot(inputs(q), dio_ref[...], preferred_element_type=F32))
        for j in range(J):
            sl = pl.ds(q * J + j, r, stride=L)
            lo = j * 2 * lanes
            y_s[0, :, sl, :] = y[:, lo:lo + lanes].reshape(nb, r, lanes)
            y_s[1, :, sl, :] = y[:, lo + lanes:lo + 2 * lanes].reshape(nb, r, lanes)
        if q + 1 < L // J:
            advance(q)

    u = jnp.concatenate([u0_ref[...], u1_ref[...]], axis=-1).reshape(nb * tb, 2 * lanes)
    y = jnp.concatenate([y_s[0], y_s[1]], axis=-1).reshape(nb * tb, 2 * lanes)
    y = jax.nn.gelu(y + d_ref[...] * u)
    z = jnp.dot(y.astype(BF16), wg_ref[...], preferred_element_type=F32)
    out = _rms(z[:, :GROUP_W] * jax.nn.sigmoid(z[:, GROUP_W:]), mg_ref[...])
    o_ref[...] = out.reshape(nb, tb, GROUP_W)


def _s5(proj3, mats, d_skip, w_glu_bf16, merge_g, tb):
    bsz, t, _ = proj3.shape
    w_in, c_out, d_io, a_grp, a_chunk = mats
    ns = S5_GROUPS * S5_STATE
    jw = S5_STEPS * GROUP_W
    lanes = GROUP_W // 2
    rows = bsz * (tb // S5_CHUNK)
    half = lambda j: pl.BlockSpec((bsz, tb, lanes), lambda i, j=j: (0, i, 2 * COL_S5 + j))
    full = lambda a, b: pl.BlockSpec((a, b), lambda i: (0, 0))
    return pl.pallas_call(
        _s5_kernel,
        grid=(t // tb,),
        in_specs=[half(0), half(1), full(jw, 2 * ns), full(2 * ns, jw), full(jw, jw), full(2, ns),
                  full(2, ns), full(1, GROUP_W), full(GROUP_W, 2 * GROUP_W), full(1, GROUP_W)],
        out_specs=pl.BlockSpec((bsz, tb, GROUP_W), lambda i: (0, i, 0)),
        out_shape=jax.ShapeDtypeStruct((bsz, t, GROUP_W), F32),
        scratch_shapes=[pltpu.VMEM((bsz, 2 * ns), F32), pltpu.VMEM((rows, 2 * ns), F32),
                        pltpu.VMEM((rows, 2 * ns), F32), pltpu.VMEM((2, bsz, tb, lanes), F32),
                        pltpu.VMEM((S5_CHUNK // S5_STEPS, rows, 2 * ns), F32)],
        compiler_params=_cparams("arbitrary"),
        name="s5",
    )(proj3, proj3, w_in, c_out, d_io, a_grp, a_chunk, d_skip.reshape(1, -1), w_glu_bf16,
      merge_g.reshape(1, -1))


def _rwkv_kernel(r_s, k_s, v_s, lw_s, a_s, b_s, g_ref, rk_ref, lnw_ref, lnb_ref, mg_ref,
                 o_ref, h_ref, y_s, m_s, n_s, p_s, z_s, *, reset):
    tb = r_s.shape[0]
    ch = RWKV_CHUNK
    nh = GROUP_W // HEAD_W
    if reset:
        h_ref[...] = jnp.zeros_like(h_ref)
        return

    bd_bf16 = _block_diag_mask(GROUP_W, HEAD_W).astype(BF16)

    n4 = nh * ch
    ri = lax.broadcasted_iota(jnp.int32, (n4, n4), 0)
    ci = lax.broadcasted_iota(jnp.int32, (n4, n4), 1)
    same_head = (ri // ch) == (ci // ch)
    strict = same_head & ((ri % ch) > (ci % ch))
    incl = same_head & ((ri % ch) >= (ci % ch))
    eye = (ri == ci).astype(F32)
    same_blk = {b: (ri // b) == (ci // b) for b in (8, 16, 32, 64)}
    hm = ((lax.broadcasted_iota(jnp.int32, (n4, GROUP_W), 0) // ch)
          == (lax.broadcasted_iota(jnp.int32, (n4, GROUP_W), 1) // HEAD_W))
    tri = (lax.broadcasted_iota(jnp.int32, (ch, ch), 0)
           >= lax.broadcasted_iota(jnp.int32, (ch, ch), 1)).astype(BF16)

    def stack(x):
        return jnp.where(hm, jnp.concatenate([x] * nh, axis=0), 0.0)

    def dot(x, y, dims=_NN):
        return lax.dot_general(x.astype(BF16), y.astype(BF16), (dims, ((), ())),
                               preferred_element_type=F32)

    chunks = range(tb // ch)
    pre = []
    for c in chunks:
        sl = slice(c * ch, (c + 1) * ch)
        lw = lw_s[sl, :]
        cl = _exact_lhs_mm(tri, lw)
        cl_end = cl[ch - 1:ch, :]
        e_in, e_ex = jnp.exp(cl), jnp.exp(cl - lw)
        e_neg, e_end = jnp.exp(-cl), jnp.exp(cl_end - cl)
        av, bv, kv, rv, vv = a_s[sl, :], b_s[sl, :], k_s[sl, :], r_s[sl, :], v_s[sl, :]
        pre.append(dict(
            at4=stack(av * e_ex), rt4=stack(rv * e_in), v4=stack(vv),
            bt4=jnp.concatenate([bv * e_neg] * nh, axis=0),
            kt4=jnp.concatenate([kv * e_neg] * nh, axis=0),
            bh4=stack(bv * e_end), kh4=stack(kv * e_end), g_end=jnp.exp(cl_end)))
        yield
    for d in pre:
        ar = jnp.concatenate([d["at4"], d["rt4"]], axis=0)
        sb = dot(ar, d["bt4"], _NT)
        sk = dot(ar, d["kt4"], _NT)
        d["l_ab"] = jnp.where(strict, sb[:n4], 0.0)
        d["l_ak"] = jnp.where(strict, sk[:n4], 0.0)
        d["l_rb"] = jnp.where(incl, sb[n4:], 0.0)
        d["l_rk"] = jnp.where(incl, sk[n4:], 0.0)
        yield
    for d in pre:
        nb8 = jnp.where(same_blk[8], d["l_ab"], 0.0)
        d["tinv"] = eye + nb8
        d["pw"] = dot(nb8, nb8)
        yield
    for d in pre:
        d["tinv"] = d["tinv"] + dot(d["tinv"], d["pw"])
        d["pw"] = dot(d["pw"], d["pw"])
        yield
    for d in pre:
        d["tinv"] = d["tinv"] + dot(d["tinv"], d["pw"])
        yield
    blk = 8
    while blk < ch:
        for d in pre:
            d["pw"] = dot(jnp.where(same_blk[2 * blk] & ~same_blk[blk], d["l_ab"], 0.0), d["tinv"])
            yield
        for d in pre:
            d["tinv"] = d["tinv"] + dot(d["tinv"], d["pw"])
            yield
        blk *= 2
    for d in pre:
        d["lakv"] = dot(d["l_ak"], d["v4"])
        d["lrkv"] = dot(d["l_rk"], d["v4"])
        d["khv"] = dot(d["kh4"], d["v4"], _TN)
        yield
    for d in pre:
        d["x12"] = dot(d["tinv"], jnp.concatenate([d["at4"], d["lakv"]], axis=1))
        yield
    for c, d in zip(chunks, pre):
        mn = dot(d["bh4"], d["x12"], _TN)
        pz = dot(d["l_rb"], d["x12"])
        m_s[c] = jnp.where(ri == ci, d["g_end"], 0.0) + mn[:, :GROUP_W]
        n_s[c] = mn[:, GROUP_W:] + d["khv"]
        p_s[c] = d["rt4"] + pz[:, :GROUP_W]
        z_s[c] = pz[:, GROUP_W:] + d["lrkv"]
        yield

    h = h_ref[...]
    for c in range(tb // ch):
        y4 = dot(p_s[c], h) + z_s[c]
        y = y4[0:ch]
        for j in range(1, nh):
            y = y + y4[j * ch:(j + 1) * ch]
        y_s[c * ch:(c + 1) * ch, :] = y
        h = dot(m_s[c], h) + n_s[c]
        yield
    h_ref[...] = h

    y = y_s[...]
    r, k, v = r_s[...], k_s[...], v_s[...]
    mean = _head_sum(y, bd_bf16) * (1.0 / HEAD_W)
    d = y - mean
    var = _head_sum(d * d, bd_bf16) * (1.0 / HEAD_W)
    y = d * lax.rsqrt(var + RWKV_GN_EPS) * lnw_ref[...] + lnb_ref[...]
    y = y + _head_sum(r * k * rk_ref[...], bd_bf16) * v
    o_ref[...] = _rms(y * g_ref[...], mg_ref[...])


def _rwkv_part(proj3, abg3, lyr, p, merge_g, tb):
    bsz, t, _ = proj3.shape
    row = lambda x: x.reshape(1, -1)
    abg = lambda j: _col_spec(tb, j)
    mats = pltpu.VMEM((tb // RWKV_CHUNK, GROUP_W, GROUP_W), F32)
    return dict(
        body=_rwkv_kernel, stages=15 * (tb // RWKV_CHUNK) + 1,
        args=[proj3, proj3, proj3, proj3, abg3, abg3, abg3, row(p["rwkv_r_k"][lyr]),
              row(p["rwkv_ln_w"][lyr]), row(p["rwkv_ln_b"][lyr]), row(merge_g)],
        in_specs=[_col_spec(tb, COL_R), _col_spec(tb, COL_K), _col_spec(tb, COL_V),
                  _col_spec(tb, COL_LORA), abg(EXT_A), abg(EXT_B), abg(EXT_G),
                  _row_spec(), _row_spec(), _row_spec(), _row_spec()],
        out_specs=[_out_blk(tb)], out_shape=[jax.ShapeDtypeStruct((bsz, t, GROUP_W), F32)],
        scratch=[pltpu.VMEM((GROUP_W, GROUP_W), F32), pltpu.VMEM((tb, GROUP_W), F32),
                 mats, mats, mats, mats])


def _bf16_bits(x):
    u = lax.bitcast_convert_type(x, jnp.uint32)
    r = u + jnp.uint32(0x7FFF) + ((u >> 16) & jnp.uint32(1))
    return r & jnp.uint32(0xFFFF0000)


def _slab_copies(rows_ref, slab_hbm, row0, sem):
    tm = rows_ref.shape[0]
    return [(rows_ref.at[:, pl.ds(j * LANES, LANES)], slab_hbm.at[pl.ds(row0, tm), j, :], sem)
            for j in range(slab_hbm.shape[1])]


def _merge_router_kernel(oa_ref, ob_ref, oc_ref, od_ref, h_ref, wo_ref, nw_ref, wr_ref, br_ref,
                         h1_ref, hn_hbm, ids_ref, wts_ref, cnt_ref, pk_s, sem):
    i = pl.program_id(0)
    steps = pl.num_programs(0)
    tm = h_ref.shape[0]
    slot = i % 2

    def writes(s, step):
        return [pltpu.make_async_copy(v, h, m) for v, h, m in
                _slab_copies(pk_s.at[s], hn_hbm, step * tm, sem.at[s])]

    mix = jnp.concatenate([oa_ref[...], ob_ref[...], oc_ref[...], od_ref[...]], axis=1)
    h1 = h_ref[...] + jnp.dot(mix.astype(BF16), wo_ref[...], preferred_element_type=F32)
    h1_ref[...] = h1
    hn = _rms(h1, nw_ref[...])
    half = hn.shape[1] // 2

    @pl.when(i >= 2)
    def _():
        for cp in writes(slot, i - 2):
            cp.wait()

    pk_s[slot] = _bf16_bits(hn[:, :half]) | (_bf16_bits(hn[:, half:]) >> 16)
    for cp in writes(slot, i):
        cp.start()

    @pl.when(i == steps - 1)
    def _():
        for cp in writes(slot, i):
            cp.wait()

    @pl.when((i == steps - 1) & (i >= 1))
    def _():
        for cp in writes(1 - slot, i - 1):
            cp.wait()

    logits = _dg3(hn, wr_ref[...], _NN) + br_ref[...]
    lane = lax.broadcasted_iota(jnp.int32, logits.shape, 1)
    big = jnp.int32(ROUTER_LANES)
    is_c = (lane >= N_EXPERTS) & (lane < N_EXPERTS + MOE_GROUPS)
    cm = jnp.max(jnp.where(is_c, logits, NEG_BIG), axis=-1, keepdims=True)
    gsel = jnp.min(jnp.where(is_c & (logits == cm), lane, big), axis=-1, keepdims=True) - N_EXPERTS
    p_g = 1.0 / jnp.sum(jnp.where(is_c, jnp.exp(logits - cm), 0.0), axis=-1, keepdims=True)
    lo = gsel * MOE_PER_GROUP
    in_g = (lane >= lo) & (lane < lo + MOE_PER_GROUP)
    m1 = jnp.max(jnp.where(in_g, logits, NEG_BIG), axis=-1, keepdims=True)
    i1 = jnp.min(jnp.where(in_g & (logits == m1), lane, big), axis=-1, keepdims=True)
    in_g2 = in_g & (lane != i1)
    m2 = jnp.max(jnp.where(in_g2, logits, NEG_BIG), axis=-1, keepdims=True)
    i2 = jnp.min(jnp.where(in_g2 & (logits == m2), lane, big), axis=-1, keepdims=True)
    w1 = p_g / (1.0 + jnp.exp(m2 - m1))
    w2 = p_g - w1
    two = lax.broadcasted_iota(jnp.int32, ids_ref.shape, 1)
    ids_ref[...] = jnp.where(two == 0, i1, i2)
    wts_ref[...] = jnp.where(two == 0, w1, w2)

    @pl.when(pl.program_id(0) == 0)
    def _():
        cnt_ref[...] = jnp.zeros_like(cnt_ref)

    cnt_ref[...] += jnp.sum(((lane == i1) | (lane == i2)).astype(F32), axis=0, keepdims=True)


def _merge_router(outs, h2d, w_out_bf16, norm_w, w_router, b_router, tm):
    n, d = h2d.shape
    grp = pl.BlockSpec((tm, GROUP_W), lambda i: (i, 0))
    full = lambda a, b: pl.BlockSpec((a, b), lambda i: (0, 0))
    rowblk = lambda w: pl.BlockSpec((tm, w), lambda i: (i, 0))
    return pl.pallas_call(
        _merge_router_kernel,
        grid=(n // tm,),
        in_specs=[grp, grp, grp, grp, rowblk(d), full(d, d), full(1, d), full(d, ROUTER_LANES),
                  full(1, ROUTER_LANES)],
        out_specs=[rowblk(d), pl.BlockSpec(memory_space=pl.ANY),
                   rowblk(MOE_TOPK), rowblk(MOE_TOPK), full(1, ROUTER_LANES)],
        out_shape=[jax.ShapeDtypeStruct((n, d), F32),
                   jax.ShapeDtypeStruct((n, d // 2 // LANES, LANES), jnp.uint32),
                   jax.ShapeDtypeStruct((n, MOE_TOPK), jnp.int32),
                   jax.ShapeDtypeStruct((n, MOE_TOPK), F32),
                   jax.ShapeDtypeStruct((1, ROUTER_LANES), F32)],
        scratch_shapes=[pltpu.VMEM((2, tm, d // 2), jnp.uint32), pltpu.SemaphoreType.DMA((2,))],
        compiler_params=_cparams("arbitrary"),
        name="merge_router",
    )(*[o.reshape(n, GROUP_W) for o in outs], h2d, w_out_bf16, norm_w.reshape(1, d), w_router, b_router)


def _row_copy(src_hbm, src_row, dst_ref, dst_row, sem):
    return pltpu.make_async_copy(src_hbm.at[pl.ds(src_row, 1), :], dst_ref.at[pl.ds(dst_row, 1), :], sem)


def _position_kernel(ids_ref, base_ref, pos_ref, run_ref):
    tm = ids_ref.shape[0]

    @pl.when(pl.program_id(0) == 0)
    def _():
        run_ref[...] = jnp.zeros_like(run_ref)

    i1, i2 = ids_ref[:, 0:1], ids_ref[:, 1:2]
    lane = lax.broadcasted_iota(jnp.int32, (tm, ROUTER_LANES), 1)
    hit = ((lane == i1) | (lane == i2)).astype(BF16)
    earlier = (lax.broadcasted_iota(jnp.int32, (tm, tm), 0)
               > lax.broadcasted_iota(jnp.int32, (tm, tm), 1)).astype(BF16)
    rank = jnp.dot(earlier, hit, preferred_element_type=F32) + run_ref[...]
    where_to = rank + base_ref[...]
    p1 = jnp.sum(jnp.where(lane == i1, where_to, 0.0), axis=-1, keepdims=True)
    p2 = jnp.sum(jnp.where(lane == i2, where_to, 0.0), axis=-1, keepdims=True)
    two = lax.broadcasted_iota(jnp.int32, pos_ref.shape, 1)
    pos_ref[...] = jnp.where(two == 0, p1, p2).astype(jnp.int32)
    run_ref[...] += jnp.sum(hit.astype(F32), axis=0, keepdims=True)


def _positions(ids, group_start, tm):
    n = ids.shape[0]
    return pl.pallas_call(
        _position_kernel,
        grid=(n // tm,),
        in_specs=[pl.BlockSpec((tm, MOE_TOPK), lambda i: (i, 0)),
                  pl.BlockSpec((1, ROUTER_LANES), lambda i: (0, 0))],
        out_specs=pl.BlockSpec((tm, MOE_TOPK), lambda i: (i, 0)),
        out_shape=jax.ShapeDtypeStruct((n, MOE_TOPK), jnp.int32),
        scratch_shapes=[pltpu.VMEM((1, ROUTER_LANES), F32)],
        compiler_params=_cparams("arbitrary"),
        name="moe_positions",
    )(ids, group_start)


def _dispatch_kernel(p0_ref, p1_ref, x_ref, xs_in_hbm, xs_hbm, sem):
    del xs_in_hbm
    n = x_ref.shape[0]

    def copies(i):
        return (pltpu.make_async_copy(x_ref.at[i], xs_hbm.at[p0_ref[i]], sem),
                pltpu.make_async_copy(x_ref.at[i], xs_hbm.at[p1_ref[i]], sem))

    def issue(i, c):
        for prio, cp in enumerate(copies(i)):
            cp.start(priority=prio)
        return c

    def drain(i, c):
        for cp in copies(i):
            cp.wait()
        return c

    lax.fori_loop(0, n, issue, 0, unroll=8)
    lax.fori_loop(0, n, drain, 0, unroll=8)


def _dispatch(x, pos0, pos1, n_rows, chunk):
    n, slabs, lanes = x.shape
    smem = lambda: pl.BlockSpec((chunk,), lambda s: (s,), memory_space=pltpu.SMEM)
    return pl.pallas_call(
        _dispatch_kernel,
        grid=(n // chunk,),
        in_specs=[smem(), smem(), pl.BlockSpec((chunk, slabs, lanes), lambda s: (s, 0, 0)),
                  pl.BlockSpec(memory_space=pl.ANY)],
        out_specs=pl.BlockSpec(memory_space=pl.ANY),
        out_shape=jax.ShapeDtypeStruct((n_rows, slabs, lanes), x.dtype),
        input_output_aliases={3: 0},
        scratch_shapes=[pltpu.SemaphoreType.DMA(())],
        compiler_params=_cparams("arbitrary"),
        name="moe_dispatch",
    )(pos0, pos1, x, jnp.zeros((n_rows, slabs, lanes), x.dtype))


def _expert_kernel(te_ref, nt_ref, xs_hbm, wg_ref, wu_ref, wd_ref, o_ref, wg_s, wu_s, wd_s, x_s, sem):
    i = pl.program_id(0)
    tm = o_ref.shape[0]
    slot = i % 2

    def reads(s, tile):
        return [pltpu.make_async_copy(h, v, m) for v, h, m in
                _slab_copies(x_s.at[s], xs_hbm, tile * tm, sem.at[s])]

    @pl.when(i == 0)
    def _():
        for cp in reads(0, 0):
            cp.start()

    @pl.when(i + 1 < pl.num_programs(0))
    def _():
        for cp in reads(1 - slot, i + 1):
            cp.start()

    @pl.when((i == 0) | (te_ref[i] != te_ref[jnp.maximum(i - 1, 0)]))
    def _():
        wg_s[...] = wg_ref[...].astype(BF16)
        wu_s[...] = wu_ref[...].astype(BF16)
        wd_s[...] = wd_ref[...].astype(BF16)

    for cp in reads(slot, i):
        cp.wait()

    @pl.when(i < nt_ref[0])
    def _():
        p = x_s[slot]
        xa = lax.bitcast_convert_type(p & jnp.uint32(0xFFFF0000), F32).astype(BF16)
        xb = lax.bitcast_convert_type(p << 16, F32).astype(BF16)
        x = jnp.concatenate([xa, xb], axis=1)
        he = (_silu(jnp.dot(x, wg_s[...], preferred_element_type=F32))
              * jnp.dot(x, wu_s[...], preferred_element_type=F32))
        o_ref[...] = jnp.dot(he.astype(BF16), wd_s[...], preferred_element_type=F32)

    @pl.when(i >= nt_ref[0])
    def _():
        o_ref[...] = jnp.zeros_like(o_ref)


def _experts(xs, tile_expert, n_tiles_used, wg, wu, wd, layer, tm):
    p, slabs, lanes = xs.shape
    d = 2 * slabs * lanes
    pick = lambda i, te, nt: (layer, te[i], 0, 0)
    grid_spec = pltpu.PrefetchScalarGridSpec(
        num_scalar_prefetch=2,
        grid=(p // tm,),
        in_specs=[pl.BlockSpec(memory_space=pl.ANY),
                  pl.BlockSpec((None, None, d, D_EXPERT), pick),
                  pl.BlockSpec((None, None, d, D_EXPERT), pick),
                  pl.BlockSpec((None, None, D_EXPERT, d), pick)],
        out_specs=pl.BlockSpec((tm, d), lambda i, te, nt: (i, 0)),
        scratch_shapes=[pltpu.VMEM((d, D_EXPERT), BF16), pltpu.VMEM((d, D_EXPERT), BF16),
                        pltpu.VMEM((D_EXPERT, d), BF16), pltpu.VMEM((2, tm, slabs * lanes), xs.dtype),
                        pltpu.SemaphoreType.DMA((2,))],
    )
    return pl.pallas_call(
        _expert_kernel,
        grid_spec=grid_spec,
        out_shape=jax.ShapeDtypeStruct((p, d), F32),
        compiler_params=_cparams("arbitrary"),
        name="moe_experts",
    )(tile_expert, n_tiles_used, xs, wg, wu, wd)


def _combine_kernel(i0_ref, i1_ref, h1_ref, w_ref, fw_ref, ys_hbm, o_ref, buf0, buf1, sem, *, final_norm):
    n = o_ref.shape[0]

    def issue(i, c):
        _row_copy(ys_hbm, i0_ref[i], buf0, i, sem).start(priority=0)
        _row_copy(ys_hbm, i1_ref[i], buf1, i, sem).start(priority=1)
        return c

    def drain(i, c):
        _row_copy(ys_hbm, 0, buf0, i, sem).wait()
        _row_copy(ys_hbm, 0, buf1, i, sem).wait()
        return c

    lax.fori_loop(0, n, issue, 0, unroll=8)
    lax.fori_loop(0, n, drain, 0, unroll=8)
    out = h1_ref[...] + w_ref[:, 0:1] * buf0[...] + w_ref[:, 1:2] * buf1[...]
    if final_norm:
        out = _rms(out, fw_ref[...])
    o_ref[...] = out


def _combine(ys, pos0, pos1, wts, h1, final_w, final_norm, chunk):
    n, d = h1.shape
    smem = lambda: pl.BlockSpec((chunk,), lambda s: (s,), memory_space=pltpu.SMEM)
    return pl.pallas_call(
        functools.partial(_combine_kernel, final_norm=final_norm),
        grid=(n // chunk,),
        in_specs=[smem(), smem(), pl.BlockSpec((chunk, d), lambda s: (s, 0)),
                  pl.BlockSpec((chunk, MOE_TOPK), lambda s: (s, 0)),
                  pl.BlockSpec((1, d), lambda s: (0, 0)), pl.BlockSpec(memory_space=pl.ANY)],
        out_specs=pl.BlockSpec((chunk, d), lambda s: (s, 0)),
        out_shape=jax.ShapeDtypeStruct((n, d), F32),
        scratch_shapes=[pltpu.VMEM((chunk, d), F32), pltpu.VMEM((chunk, d), F32),
                        pltpu.SemaphoreType.DMA(())],
        compiler_params=_cparams("arbitrary"),
        name="moe_combine",
    )(pos0, pos1, h1, wts, final_w.reshape(1, d), ys)


def _moe(hn_packed, ids, wts, counts, h1, wg, wu, wd, layer, final_w, final_norm, tm):
    n = h1.shape[0]
    n_tiles = (n * MOE_TOPK) // tm + N_EXPERTS
    cnt = counts[0, :N_EXPERTS].astype(jnp.int32)
    padded = ((cnt + tm - 1) // tm) * tm
    ends = jnp.cumsum(padded)
    tile_start = jnp.arange(n_tiles, dtype=jnp.int32) * tm
    tile_expert = jnp.minimum(jnp.sum(ends[None, :] <= tile_start[:, None], axis=1),
                              N_EXPERTS - 1).astype(jnp.int32)
    n_tiles_used = (ends[-1] // tm).astype(jnp.int32).reshape(1)
    group_start = jnp.zeros((1, ROUTER_LANES), F32).at[0, :N_EXPERTS].set((ends - padded).astype(F32))
    pos = _positions(ids, group_start, _pick_div(n, 512))
    pos0, pos1 = pos[:, 0], pos[:, 1]
    xs = _dispatch(hn_packed, pos0, pos1, n_tiles * tm, _pick_div(n, 1024))
    ys = _experts(xs, tile_expert, n_tiles_used, wg, wu, wd, layer, tm)
    return _combine(ys, pos0, pos1, wts, h1, final_w, final_norm, _pick_div(n, 1024))


def _pick_div(n, pref):
    while n % pref:
        pref //= 2
    return pref


def _pick(n, pref):
    return pref if n % pref == 0 else n


def kernel(x, norm1_w, w_in, hgrn_lb_logits, hgrn_norm_w, s5_lambda_re, s5_lambda_im, s5_log_dt, s5_b_re, s5_b_im, s5_c_re, s5_c_im, s5_d, s5_w_glu, rwkv_mu, rwkv_w0, rwkv_w2, rwkv_a0, rwkv_a2, rwkv_g2, rwkv_k_k, rwkv_k_a, rwkv_r_k, rwkv_v0, rwkv_v1, rwkv_v2, rwkv_ln_w, rwkv_ln_b, lru_conv_w, lru_conv_b, lru_wa, lru_ba, lru_wx, lru_bx, lru_lambda, merge_gain, w_out, norm2_w, moe_coarse_w, moe_coarse_b, moe_fine_w, moe_fine_b, moe_w_gate, moe_w_up, moe_w_down, final_norm_w):
    bsz, t, d = x.shape
    n = bsz * t
    depth = w_in.shape[0]
    rw = dict(rwkv_mu=rwkv_mu, rwkv_w0=rwkv_w0, rwkv_w2=rwkv_w2, rwkv_a0=rwkv_a0, rwkv_a2=rwkv_a2,
              rwkv_g2=rwkv_g2, rwkv_k_k=rwkv_k_k, rwkv_k_a=rwkv_k_a, rwkv_r_k=rwkv_r_k,
              rwkv_v0=rwkv_v0, rwkv_v1=rwkv_v1, rwkv_v2=rwkv_v2, rwkv_ln_w=rwkv_ln_w,
              rwkv_ln_b=rwkv_ln_b)
    lb_all = jnp.cumsum(jax.nn.softmax(hgrn_lb_logits.astype(F32), axis=0), axis=0)
    lb_all = lb_all - lb_all[:1]

    tm_proj = _pick(n, 512)
    tm_in = _pick_div(t, 512)
    tm_moe = 512
    tb_mix = _pick(t, 256)
    tb_s5 = _pick(t, 512)

    h = x.reshape(n, d)
    v_first = None
    for l in range(depth):
        mg = merge_gain[l].reshape(4, GROUP_W)
        proj, abg, v_first = _in_proj(h, norm1_w[l], w_in[l].astype(BF16), v_first, l, rw, t, tm_in)
        proj3 = proj.reshape(bsz, t, -1)
        abg3 = abg.reshape(bsz, t, -1)
        mats = _s5_matrices(s5_lambda_re[l], s5_lambda_im[l], s5_log_dt[l], s5_b_re[l], s5_b_im[l],
                            s5_c_re[l], s5_c_im[l])
        o_b = _s5(proj3, mats, s5_d[l], s5_w_glu[l].astype(BF16), mg[1], tb_s5)
        parts = [_rwkv_part(proj3, abg3, l, rw, mg[2], tb_mix),
                 _hgrn_part(proj3, lb_all[l], hgrn_norm_w[l], mg[0], tb_mix),
                 _lru_part(proj3, lru_conv_w[l], lru_conv_b[l], _block_diag_weight(lru_wa[l]),
                           lru_ba[l], _block_diag_weight(lru_wx[l]), lru_bx[l], lru_lambda[l],
                           mg[3], tb_mix)]
        outs = _mixers(parts, bsz, t // tb_mix, "mixers")
        o_c, o_a, o_d = outs
        w_router = jnp.concatenate(
            [moe_fine_w[l].transpose(1, 0, 2).reshape(d, N_EXPERTS), moe_coarse_w[l],
             jnp.zeros((d, ROUTER_LANES - N_EXPERTS - MOE_GROUPS), F32)], axis=1)
        b_router = jnp.concatenate(
            [moe_fine_b[l].reshape(N_EXPERTS), moe_coarse_b[l],
             jnp.zeros((ROUTER_LANES - N_EXPERTS - MOE_GROUPS,), F32)]).reshape(1, ROUTER_LANES)
        h1, hn, ids, wts, counts = _merge_router((o_a, o_b, o_c, o_d), h, w_out[l].astype(BF16),
                                                 norm2_w[l], w_router, b_router, tm_proj)
        h = _moe(hn, ids, wts, counts, h1, moe_w_gate, moe_w_up, moe_w_down, l,
                 final_norm_w, l == depth - 1, tm_moe)
    return h.reshape(bsz, t, d)
```

```python
import functools
import math

import jax
import jax.numpy as jnp
from jax import lax
from jax.experimental import pallas as pl
from jax.experimental.pallas import tpu as pltpu

F32 = jnp.float32
BF16 = jnp.bfloat16

D_MODEL = 1024
GROUP_W = 256
RMS_EPS = 1e-6
HEAD_W = 64
HGRN_CHUNK = 16
S5_CH = 16
S5_GROUPS = GROUP_W // S5_CH
S5_STATE = 64
S5_CHUNK = 16
S5_STEPS = 4
RWKV_CHUNK = 64
RWKV_GN_EPS = 64e-5
RWKV_W_LORA = 64
RWKV_A_LORA = 64
RWKV_G_LORA = 128
RWKV_V_LORA = 32
LRU_CONV = 4
LRU_C = 8.0
MOE_GROUPS = 4
MOE_PER_GROUP = 8
N_EXPERTS = MOE_GROUPS * MOE_PER_GROUP
MOE_TOPK = 2
D_EXPERT = 512
LANES = 128
ROUTER_LANES = LANES
NEG_BIG = -1e30
VMEM_LIMIT = 56 * 1024 * 1024

COL_HQ, COL_HF, COL_HI, COL_HG, COL_S5, COL_R, COL_K, COL_V, COL_LORA, COL_LG, COL_LX = range(11)
EXT_A, EXT_B, EXT_G = range(3)


def _cparams(*sem):
    return pltpu.CompilerParams(dimension_semantics=sem, vmem_limit_bytes=VMEM_LIMIT)


def _mm(a, b):
    return jnp.dot(a.astype(BF16), b.astype(BF16), preferred_element_type=F32)


def _split2(x):
    hi = x.astype(BF16)
    lo = (x - hi.astype(F32)).astype(BF16)
    return hi, lo


def _dg3(a, b, dims):
    ah, al = _split2(a)
    bh, bl = _split2(b)
    d = lambda x, y: lax.dot_general(x, y, (dims, ((), ())), preferred_element_type=F32)
    return d(ah, bh) + d(ah, bl) + d(al, bh)


_NN = ((1,), (0,))
_NT = ((1,), (1,))
_TN = ((0,), (0,))


def _exact_lhs_mm(m_bf16, x):
    h1 = x.astype(BF16)
    r1 = x - h1.astype(F32)
    h2 = r1.astype(BF16)
    h3 = (r1 - h2.astype(F32)).astype(BF16)
    d = lambda y: jnp.dot(m_bf16, y, preferred_element_type=F32)
    return d(h1) + d(h2) + d(h3)


def _head_sum(x, bd_bf16):
    return jnp.dot(x.astype(BF16), bd_bf16, preferred_element_type=F32)


def _block_diag_mask(n, blk):
    r = lax.broadcasted_iota(jnp.int32, (n, n), 0) // blk
    c = lax.broadcasted_iota(jnp.int32, (n, n), 1) // blk
    return r == c


def _rms(x, w):
    return x * lax.rsqrt(jnp.mean(x * x, axis=-1, keepdims=True) + RMS_EPS) * w


def _silu(x):
    return x * jax.nn.sigmoid(x)


def _softplus(x):
    return jnp.maximum(x, 0.0) + jnp.log(1.0 + jnp.exp(-jnp.abs(x)))


def _in_proj_kernel(*refs, has_vmix, tiles_per_seq):
    if has_vmix:
        (x_ref, nw_ref, w_ref, mu_ref, w0_ref, w2_ref, a0_ref, a2_ref, g2_ref, kk_ref, ka_ref,
         vf_ref, v0_ref, v1_ref, v2_ref, o_ref, o2_ref, prev_s) = refs
    else:
        (x_ref, nw_ref, w_ref, mu_ref, w0_ref, w2_ref, a0_ref, a2_ref, g2_ref, kk_ref, ka_ref,
         o_ref, o2_ref, vf_out_ref, prev_s) = refs
    tm = x_ref.shape[0]
    gw = GROUP_W
    c0, c1 = COL_R * gw, (COL_LORA + 1) * gw
    @pl.when((pl.program_id(0) % tiles_per_seq) == 0)
    def _():
        prev_s[...] = jnp.zeros_like(prev_s)

    y = _rms(x_ref[...], nw_ref[...]).astype(BF16)
    raw = jnp.dot(y, w_ref[:, c0:c1], preferred_element_type=F32)

    row0 = lax.broadcasted_iota(jnp.int32, (tm, 1), 0) == 0
    sh = jnp.where(row0, prev_s[...], pltpu.roll(raw, 1, axis=0))
    prev_s[...] = raw[tm - 1:tm, :]
    pf = raw + mu_ref[...] * (sh - raw)
    r, k, v, lo = (pf[:, j * GROUP_W:(j + 1) * GROUP_W] for j in range(4))

    o_ref[:, :c0] = jnp.dot(y, w_ref[:, :c0], preferred_element_type=F32)
    bd_bf16 = _block_diag_mask(GROUP_W, HEAD_W).astype(BF16)
    lo_wa, lo_g = lo[:, :LANES], lo[:, LANES:]
    w_raw = -_softplus(-(w0_ref[...] + _mm(jnp.tanh(lo_wa), w2_ref[...]))) - 0.5
    alr = jax.nn.sigmoid(a0_ref[...] + _mm(lo_wa, a2_ref[...]))
    o_ref[:, c1:] = jnp.dot(y, w_ref[:, c1:], preferred_element_type=F32)
    kkr = k * kk_ref[...]
    kk = kkr / jnp.maximum(jnp.sqrt(_head_sum(kkr * kkr, bd_bf16)), 1e-12)
    if has_vmix:
        gate = jax.nn.sigmoid(v0_ref[...] + _mm(_mm(v, v1_ref[...]), v2_ref[...]))
        v = v + (vf_ref[...] - v) * gate
    else:
        vf_out_ref[...] = v
    o_ref[:, c0:c0 + GROUP_W] = r
    o_ref[:, c0 + GROUP_W:c0 + 2 * GROUP_W] = k * (1.0 + (alr - 1.0) * ka_ref[...])
    o_ref[:, c0 + 2 * GROUP_W:c0 + 3 * GROUP_W] = v
    o_ref[:, c0 + 3 * GROUP_W:c1] = -jnp.exp(w_raw)
    o2_ref[:, EXT_A * gw:(EXT_A + 1) * gw] = -kk
    o2_ref[:, EXT_B * gw:(EXT_B + 1) * gw] = kk * alr
    o2_ref[:, EXT_G * gw:(EXT_G + 1) * gw] = _mm(jax.nn.sigmoid(lo_g), g2_ref[...])


def _pad_rows(w, start, total=LANES):
    out = jnp.zeros((total, w.shape[1]), F32).at[start:start + w.shape[0]].set(w)
    return out.astype(BF16)


def _in_proj(h2d, norm_w, w_bf16, v_first, lyr, p, t, tm):
    n, d = h2d.shape
    d_in = w_bf16.shape[1]
    has_vmix = v_first is not None
    row = lambda x: x.reshape(1, -1)
    full = lambda a, b: pl.BlockSpec((a, b), lambda i: (0, 0))
    rowblk = lambda w: pl.BlockSpec((tm, w), lambda i: (i, 0))
    args = [h2d, row(norm_w), w_bf16, row(p["rwkv_mu"][lyr]), row(p["rwkv_w0"][lyr]),
            _pad_rows(p["rwkv_w2"][lyr], 0), row(p["rwkv_a0"][lyr]),
            _pad_rows(p["rwkv_a2"][lyr], RWKV_W_LORA), p["rwkv_g2"][lyr].astype(BF16),
            row(p["rwkv_k_k"][lyr]), row(p["rwkv_k_a"][lyr])]
    specs = [rowblk(d), full(1, d), full(d, d_in), full(1, 4 * GROUP_W), full(1, GROUP_W),
             full(LANES, GROUP_W), full(1, GROUP_W), full(LANES, GROUP_W), full(LANES, GROUP_W),
             full(1, GROUP_W), full(1, GROUP_W)]
    out_specs = [rowblk(d_in), rowblk(3 * GROUP_W)]
    out_shape = [jax.ShapeDtypeStruct((n, d_in), F32), jax.ShapeDtypeStruct((n, 3 * GROUP_W), F32)]
    if has_vmix:
        v1 = jnp.zeros((GROUP_W, LANES), F32).at[:, :RWKV_V_LORA].set(p["rwkv_v1"][lyr - 1]).astype(BF16)
        v2 = jnp.zeros((LANES, GROUP_W), F32).at[:RWKV_V_LORA].set(p["rwkv_v2"][lyr - 1]).astype(BF16)
        args += [v_first, row(p["rwkv_v0"][lyr - 1]), v1, v2]
        specs += [rowblk(GROUP_W), full(1, GROUP_W), full(GROUP_W, LANES), full(LANES, GROUP_W)]
    else:
        out_specs.append(rowblk(GROUP_W))
        out_shape.append(jax.ShapeDtypeStruct((n, GROUP_W), F32))
    res = pl.pallas_call(
        functools.partial(_in_proj_kernel, has_vmix=has_vmix, tiles_per_seq=t // tm),
        grid=(n // tm,),
        in_specs=specs,
        out_specs=out_specs,
        out_shape=out_shape,
        scratch_shapes=[pltpu.VMEM((1, 4 * GROUP_W), F32)],
        compiler_params=_cparams("arbitrary"),
        name="in_proj",
    )(*args)
    return res[0], res[1], (v_first if has_vmix else res[2])


def _col_spec(tb, col):
    return pl.BlockSpec((None, tb, GROUP_W), lambda b, t: (b, t, col))


def _row_spec(width=GROUP_W):
    return pl.BlockSpec((1, width), lambda b, t: (0, 0))


def _full_spec(shape):
    return pl.BlockSpec(shape, lambda b, t: (0,) * len(shape))


def _hgrn_kernel(q_ref, f_ref, i_ref, g_ref, lb_ref, nw_ref, mg_ref, o_ref,
                 st_ref, q_s, k_s, v_s, lf_s, o_s, *, reset):
    ch = HGRN_CHUNK
    tb = q_ref.shape[0]
    if reset:
        st_ref[...] = jnp.zeros_like(st_ref)
        return

    lb = lb_ref[...]
    fx = f_ref[...]
    x1 = jnp.log(lb)
    x2 = jnp.log(1.0 - lb) - _softplus(-fx)
    m = jnp.maximum(x1, x2)
    lf_s[...] = m + jnp.log(jnp.exp(x1 - m) + jnp.exp(x2 - m))
    q_s[...] = _silu(q_ref[...])
    k_s[...] = (1.0 - lb) * jax.nn.sigmoid(-fx)
    v_s[...] = _silu(i_ref[...])
    yield

    bd = _block_diag_mask(GROUP_W, HEAD_W)
    bd_bf16 = bd.astype(BF16)
    tri = (lax.broadcasted_iota(jnp.int32, (ch, ch), 0)
           >= lax.broadcasted_iota(jnp.int32, (ch, ch), 1))
    tri_bf16 = tri.astype(BF16)
    tri3 = (lax.broadcasted_iota(jnp.int32, (ch, ch, 1), 0)
            <= lax.broadcasted_iota(jnp.int32, (ch, ch, 1), 1))

    st = st_ref[...]
    for c in range(tb // ch):
        sl = slice(c * ch, (c + 1) * ch)
        qc, kc, vc = q_s[sl, :], k_s[sl, :], v_s[sl, :]
        b = _exact_lhs_mm(tri_bf16, lf_s[sl, :])
        rel = b[None, :, :] - b[:, None, :]
        dec = jnp.exp(jnp.where(tri3, rel, NEG_BIG))
        p = (qc[None, :, :] * kc[:, None, :]) * dec
        sc = jnp.dot(p.reshape(ch * ch, GROUP_W).astype(BF16), bd_bf16,
                     preferred_element_type=F32).reshape(ch, ch, GROUP_W)
        o_intra = jnp.sum(sc * vc[:, None, :], axis=0)
        o_inter = lax.dot_general((qc * jnp.exp(b)).astype(BF16), st.astype(BF16),
                                  (_NT, ((), ())), preferred_element_type=F32)
        b_end = b[ch - 1:ch, :]
        kh = kc * jnp.exp(b_end - b)
        upd = lax.dot_general(vc.astype(BF16), kh.astype(BF16), (_TN, ((), ())),
                              preferred_element_type=F32)
        st = st * jnp.exp(b_end) + jnp.where(bd, upd, 0.0)
        o_s[sl, :] = o_intra + o_inter
        yield
    st_ref[...] = st

    o = o_s[...]
    ms = _head_sum(o * o, bd_bf16) * (1.0 / HEAD_W)
    o = o * lax.rsqrt(ms + RMS_EPS) * nw_ref[...] * _silu(g_ref[...])
    o_ref[...] = _rms(o, mg_ref[...])


def _out_blk(tb):
    return pl.BlockSpec((None, tb, GROUP_W), lambda b, i: (b, i, 0))


def _hgrn_part(proj3, lb, norm_w, merge_g, tb):
    bsz, t, _ = proj3.shape
    blk = pltpu.VMEM((tb, GROUP_W), F32)
    return dict(
        body=_hgrn_kernel, stages=tb // HGRN_CHUNK + 2,
        args=[proj3, proj3, proj3, proj3, lb.reshape(1, -1), norm_w.reshape(1, -1), merge_g.reshape(1, -1)],
        in_specs=[_col_spec(tb, COL_HQ), _col_spec(tb, COL_HF), _col_spec(tb, COL_HI),
                  _col_spec(tb, COL_HG), _row_spec(), _row_spec(), _row_spec()],
        out_specs=[_out_blk(tb)],
        out_shape=[jax.ShapeDtypeStruct((bsz, t, GROUP_W), F32)],
        scratch=[pltpu.VMEM((GROUP_W, GROUP_W), F32), blk, blk, blk, blk, blk])


def _mixer_kernel(*refs, parts):
    groups, i = [], 0
    for kind in range(3):
        for body, counts in parts:
            groups.append(refs[i:i + counts[kind]])
            i += counts[kind]
    k = len(parts)
    per_part = [groups[j] + groups[k + j] + groups[2 * k + j] for j in range(k)]

    @pl.when(pl.program_id(1) == 0)
    def _():
        for (body, _), r in zip(parts, per_part):
            for _step in body(*r, reset=True):
                pass

    runs = [[body(*r, reset=False), 0, counts[3]] for (body, counts), r in zip(parts, per_part)]
    while runs:
        run = min(runs, key=lambda x: x[1] / x[2])
        try:
            next(run[0])
            run[1] += 1
        except StopIteration:
            runs.remove(run)


def _mixers(parts, bsz, n_tblocks, name):
    cat = lambda key: [x for p in parts for x in p[key]]
    light = tuple((p["body"], (len(p["args"]), len(p["out_specs"]), len(p["scratch"]), p["stages"]))
                  for p in parts)
    return pl.pallas_call(
        functools.partial(_mixer_kernel, parts=light),
        grid=(bsz, n_tblocks),
        in_specs=cat("in_specs"),
        out_specs=cat("out_specs"),
        out_shape=cat("out_shape"),
        scratch_shapes=cat("scratch"),
        compiler_params=_cparams("parallel", "arbitrary"),
        name=name,
    )(*cat("args"))


def _lru_kernel(xg_ref, xr_ref, cw_ref, cb_ref, wa_ref, ba_ref, wx_ref, bx_ref, lam_ref, mg_ref,
                o_ref, buf_ref, h_ref, *, reset):
    tb = xr_ref.shape[0]
    if reset:
        buf_ref[0:8, :] = jnp.zeros((8, GROUP_W), F32)
        h_ref[...] = jnp.zeros_like(h_ref)
        return

    xr = xr_ref[...]
    buf_ref[8:8 + tb, :] = xr
    xc = cb_ref[...] + jnp.zeros_like(xr)
    for j in range(LRU_CONV):
        xc = xc + cw_ref[j:j + 1, :] * buf_ref[pl.ds(8 - (LRU_CONV - 1) + j, tb), :]
    buf_ref[0:8, :] = xr[tb - 8:tb, :]

    r = jax.nn.sigmoid(jnp.dot(xc.astype(BF16), wa_ref[...], preferred_element_type=F32) + ba_ref[...])
    gi = jax.nn.sigmoid(jnp.dot(xc.astype(BF16), wx_ref[...], preferred_element_type=F32) + bx_ref[...])
    log_a = -LRU_C * r * _softplus(-lam_ref[...])
    a = jnp.exp(log_a)
    x = jnp.sqrt(1.0 - jnp.exp(2.0 * log_a)) * (gi * xc)
    yield

    rows = lax.broadcasted_iota(jnp.int32, (tb, 1), 0)
    k = 1
    while k < tb:
        keep = rows >= k
        x = x + jnp.where(keep, a * pltpu.roll(x, k, axis=0), 0.0)
        a = jnp.where(keep, a * pltpu.roll(a, k, axis=0), a)
        k *= 2
        yield
    h = x + a * h_ref[...]
    h_ref[...] = h[tb - 1:tb, :]
    o_ref[...] = _rms(jax.nn.gelu(xg_ref[...]) * h, mg_ref[...])


def _lru_part(proj3, conv_w, conv_b, wa_bd, ba, wx_bd, bx, lam, merge_g, tb):
    bsz, t, _ = proj3.shape
    return dict(
        body=_lru_kernel, stages=tb.bit_length() + 1,
        args=[proj3, proj3, conv_w, conv_b.reshape(1, -1), wa_bd, ba.reshape(1, -1), wx_bd,
              bx.reshape(1, -1), lam.reshape(1, -1), merge_g.reshape(1, -1)],
        in_specs=[_col_spec(tb, COL_LG), _col_spec(tb, COL_LX), _full_spec((LRU_CONV, GROUP_W)),
                  _row_spec(), _full_spec((GROUP_W, GROUP_W)), _row_spec(),
                  _full_spec((GROUP_W, GROUP_W)), _row_spec(), _row_spec(), _row_spec()],
        out_specs=[_out_blk(tb)],
        out_shape=[jax.ShapeDtypeStruct((bsz, t, GROUP_W), F32)],
        scratch=[pltpu.VMEM((tb + 8, GROUP_W), F32), pltpu.VMEM((1, GROUP_W), F32)])


def _block_diag_weight(w):
    h, n, _ = w.shape
    eye = jnp.eye(h, dtype=w.dtype)
    return jnp.einsum('hij,hg->higj', w, eye).reshape(h * n, h * n).astype(BF16)


def _s5_matrices(lam_re, lam_im, log_dt, b_re, b_im, c_re, c_im):
    L, G, P, C = S5_CHUNK, S5_GROUPS, S5_STATE, S5_CH
    lr, li = lam_re.astype(F32), lam_im.astype(F32)
    dt = jnp.exp(log_dt.astype(F32))[:, None]
    mag = jnp.exp(lr * dt)
    a_re, a_im = mag * jnp.cos(li * dt), mag * jnp.sin(li * dt)
    den = lr * lr + li * li
    kap_re = ((a_re - 1.0) * lr + a_im * li) / den
    kap_im = (a_im * lr - (a_re - 1.0) * li) / den
    br, bi = b_re.astype(F32), b_im.astype(F32)
    bb_re = kap_re[..., None] * br - kap_im[..., None] * bi
    bb_im = kap_re[..., None] * bi + kap_im[..., None] * br
    cr, ci = c_re.astype(F32), c_im.astype(F32)
    eye = jnp.eye(G, dtype=F32)
    J = S5_STEPS
    kk = jnp.arange(J + 1, dtype=F32)[:, None, None]
    pmag = jnp.exp(kk * (lr * dt)[None])
    pw_re, pw_im = pmag * jnp.cos(kk * (li * dt)[None]), pmag * jnp.sin(kk * (li * dt)[None])
    ab_re = pw_re[:J, :, :, None] * bb_re[None] - pw_im[:J, :, :, None] * bb_im[None]
    ab_im = pw_re[:J, :, :, None] * bb_im[None] + pw_im[:J, :, :, None] * bb_re[None]
    rev = lambda x: jnp.stack([x[J - 1 - j] for j in range(J)])
    w_in = jnp.concatenate(
        [jnp.einsum('jgpc,gh->jgchp', rev(ab_re), eye).reshape(J * G * C, G * P),
         jnp.einsum('jgpc,gh->jgchp', rev(ab_im), eye).reshape(J * G * C, G * P)], axis=1)
    ca_re = jnp.einsum('gcp,jgp->jgcp', cr, pw_re[1:]) - jnp.einsum('gcp,jgp->jgcp', ci, pw_im[1:])
    ca_im = jnp.einsum('gcp,jgp->jgcp', cr, pw_im[1:]) + jnp.einsum('gcp,jgp->jgcp', ci, pw_re[1:])
    c_out = jnp.concatenate(
        [jnp.einsum('jgcp,gh->gpjhc', ca_re, eye).reshape(G * P, J * G * C),
         -jnp.einsum('jgcp,gh->gpjhc', ca_im, eye).reshape(G * P, J * G * C)], axis=0)
    taps = jnp.einsum('gop,kgpc->kgco', cr, ab_re) - jnp.einsum('gop,kgpc->kgco', ci, ab_im)
    none = jnp.zeros_like(taps[0])
    d_io = jnp.stack([jnp.stack([taps[j - i] if j >= i else none for j in range(J)])
                      for i in range(J)])
    d_io = jnp.einsum('ijgco,gh->igcjho', d_io, eye).reshape(J * G * C, J * G * C)
    a_grp = jnp.stack([pw_re[J].reshape(G * P), pw_im[J].reshape(G * P)])
    mag_l = jnp.exp(L * lr * dt)
    a_chunk = jnp.stack([(mag_l * jnp.cos(L * li * dt)).reshape(G * P),
                         (mag_l * jnp.sin(L * li * dt)).reshape(G * P)])
    return w_in.astype(BF16), c_out.astype(BF16), d_io.astype(BF16), a_grp, a_chunk


def _s5_kernel(u0_ref, u1_ref, w_ref, c_ref, dio_ref, a_ref, al_ref, d_ref, wg_ref, mg_ref, o_ref,
               carry_ref, x_s, st_s, y_s, bu_s):
    nb, tb, lanes = u0_ref.shape
    L, J = S5_CHUNK, S5_STEPS
    r = tb // L
    rows = nb * r
    ns = S5_GROUPS * S5_STATE

    @pl.when(pl.program_id(0) == 0)
    def _():
        carry_ref[...] = jnp.zeros_like(carry_ref)

    a_re, a_im = a_ref[0:1, :], a_ref[1:2, :]
    al_re, al_im = al_ref[0:1, :], al_ref[1:2, :]

    def inputs(q):
        parts = []
        for j in range(J):
            sl = pl.ds(q * J + j, r, stride=L)
            parts += [u0_ref[:, sl, :], u1_ref[:, sl, :]]
        return jnp.concatenate(parts, axis=-1).reshape(rows, J * 2 * lanes).astype(BF16)

    def advance(q):
        bu = bu_s[q]
        xr, xi = x_s[:, :ns], x_s[:, ns:]
        x_s[:, :ns] = a_re * xr - a_im * xi + bu[:, :ns]
        x_s[:, ns:] = a_re * xi + a_im * xr + bu[:, ns:]

    x_s[...] = jnp.zeros_like(x_s)
    for q in range(L // J):
        bu_s[q] = jnp.dot(inputs(q), w_ref[...], preferred_element_type=F32)
        advance(q)

    for b in range(nb):
        def hop(c, carry, b=b):
            xr, xi = carry
            row = pl.ds(b * r + c, 1)
            st_s[row, :] = jnp.concatenate([xr, xi], axis=1)
            p = x_s[row, :]
            return (al_re * xr - al_im * xi + p[:, :ns], al_re * xi + al_im * xr + p[:, ns:])

        xr, xi = lax.fori_loop(0, r, hop, (carry_ref[b:b + 1, :ns], carry_ref[b:b + 1, ns:]))
        carry_ref[b:b + 1, :] = jnp.concatenate([xr, xi], axis=1)

    x_s[...] = st_s[...]
    for q in range(L // J):
        y = (jnp.dot(x_s[...].astype(BF16), c_ref[...], preferred_element_type=F32)
             + jnp.dot(inputs(q), dio_ref[...], preferred_element_type=F32))
        for j in range(J):
            sl = pl.ds(q * J + j, r, stride=L)
            lo = j * 2 * lanes
            y_s[0, :, sl, :] = y[:, lo:lo + lanes].reshape(nb, r, lanes)
            y_s[1, :, sl, :] = y[:, lo + lanes:lo + 2 * lanes].reshape(nb, r, lanes)
        if q + 1 < L // J:
            advance(q)

    u = jnp.concatenate([u0_ref[...], u1_ref[...]], axis=-1).reshape(nb * tb, 2 * lanes)
    y = jnp.concatenate([y_s[0], y_s[1]], axis=-1).reshape(nb * tb, 2 * lanes)
    y = jax.nn.gelu(y + d_ref[...] * u)
    z = jnp.dot(y.astype(BF16), wg_ref[...], preferred_element_type=F32)
    out = _rms(z[:, :GROUP_W] * jax.nn.sigmoid(z[:, GROUP_W:]), mg_ref[...])
    o_ref[...] = out.reshape(nb, tb, GROUP_W)


def _s5(proj3, mats, d_skip, w_glu_bf16, merge_g, tb):
    bsz, t, _ = proj3.shape
    w_in, c_out, d_io, a_grp, a_chunk = mats
    ns = S5_GROUPS * S5_STATE
    jw = S5_STEPS * GROUP_W
    lanes = GROUP_W // 2
    rows = bsz * (tb // S5_CHUNK)
    half = lambda j: pl.BlockSpec((bsz, tb, lanes), lambda i, j=j: (0, i, 2 * COL_S5 + j))
    full = lambda a, b: pl.BlockSpec((a, b), lambda i: (0, 0))
    return pl.pallas_call(
        _s5_kernel,
        grid=(t // tb,),
        in_specs=[half(0), half(1), full(jw, 2 * ns), full(2 * ns, jw), full(jw, jw), full(2, ns),
                  full(2, ns), full(1, GROUP_W), full(GROUP_W, 2 * GROUP_W), full(1, GROUP_W)],
        out_specs=pl.BlockSpec((bsz, tb, GROUP_W), lambda i: (0, i, 0)),
        out_shape=jax.ShapeDtypeStruct((bsz, t, GROUP_W), F32),
        scratch_shapes=[pltpu.VMEM((bsz, 2 * ns), F32), pltpu.VMEM((rows, 2 * ns), F32),
                        pltpu.VMEM((rows, 2 * ns), F32), pltpu.VMEM((2, bsz, tb, lanes), F32),
                        pltpu.VMEM((S5_CHUNK // S5_STEPS, rows, 2 * ns), F32)],
        compiler_params=_cparams("arbitrary"),
        name="s5",
    )(proj3, proj3, w_in, c_out, d_io, a_grp, a_chunk, d_skip.reshape(1, -1), w_glu_bf16,
      merge_g.reshape(1, -1))


def _rwkv_kernel(r_s, k_s, v_s, lw_s, a_s, b_s, g_ref, rk_ref, lnw_ref, lnb_ref, mg_ref,
                 o_ref, h_ref, y_s, m_s, n_s, p_s, z_s, *, reset):
    tb = r_s.shape[0]
    ch = RWKV_CHUNK
    nh = GROUP_W // HEAD_W
    if reset:
        h_ref[...] = jnp.zeros_like(h_ref)
        return

    bd_bf16 = _block_diag_mask(GROUP_W, HEAD_W).astype(BF16)

    n4 = nh * ch
    ri = lax.broadcasted_iota(jnp.int32, (n4, n4), 0)
    ci = lax.broadcasted_iota(jnp.int32, (n4, n4), 1)
    same_head = (ri // ch) == (ci // ch)
    strict = same_head & ((ri % ch) > (ci % ch))
    incl = same_head & ((ri % ch) >= (ci % ch))
    eye = (ri == ci).astype(F32)
    same_blk = {b: (ri // b) == (ci // b) for b in (8, 16, 32, 64)}
    hm = ((lax.broadcasted_iota(jnp.int32, (n4, GROUP_W), 0) // ch)
          == (lax.broadcasted_iota(jnp.int32, (n4, GROUP_W), 1) // HEAD_W))
    tri = (lax.broadcasted_iota(jnp.int32, (ch, ch), 0)
           >= lax.broadcasted_iota(jnp.int32, (ch, ch), 1)).astype(BF16)

    def stack(x):
        return jnp.where(hm, jnp.concatenate([x] * nh, axis=0), 0.0)

    def dot(x, y, dims=_NN):
        return lax.dot_general(x.astype(BF16), y.astype(BF16), (dims, ((), ())),
                               preferred_element_type=F32)

    chunks = range(tb // ch)
    pre = []
    for c in chunks:
        sl = slice(c * ch, (c + 1) * ch)
        lw = lw_s[sl, :]
        cl = _exact_lhs_mm(tri, lw)
        cl_end = cl[ch - 1:ch, :]
        e_in, e_ex = jnp.exp(cl), jnp.exp(cl - lw)
        e_neg, e_end = jnp.exp(-cl), jnp.exp(cl_end - cl)
        av, bv, kv, rv, vv = a_s[sl, :], b_s[sl, :], k_s[sl, :], r_s[sl, :], v_s[sl, :]
        pre.append(dict(
            at4=stack(av * e_ex), rt4=stack(rv * e_in), v4=stack(vv),
            bt4=jnp.concatenate([bv * e_neg] * nh, axis=0),
            kt4=jnp.concatenate([kv * e_neg] * nh, axis=0),
            bh4=stack(bv * e_end), kh4=stack(kv * e_end), g_end=jnp.exp(cl_end)))
        yield
    for d in pre:
        ar = jnp.concatenate([d["at4"], d["rt4"]], axis=0)
        sb = dot(ar, d["bt4"], _NT)
        sk = dot(ar, d["kt4"], _NT)
        d["l_ab"] = jnp.where(strict, sb[:n4], 0.0)
        d["l_ak"] = jnp.where(strict, sk[:n4], 0.0)
        d["l_rb"] = jnp.where(incl, sb[n4:], 0.0)
        d["l_rk"] = jnp.where(incl, sk[n4:], 0.0)
        yield
    for d in pre:
        nb8 = jnp.where(same_blk[8], d["l_ab"], 0.0)
        d["tinv"] = eye + nb8
        d["pw"] = dot(nb8, nb8)
        yield
    for d in pre:
        d["tinv"] = d["tinv"] + dot(d["tinv"], d["pw"])
        d["pw"] = dot(d["pw"], d["pw"])
        yield
    for d in pre:
        d["tinv"] = d["tinv"] + dot(d["tinv"], d["pw"])
        yield
    blk = 8
    while blk < ch:
        for d in pre:
            d["pw"] = dot(jnp.where(same_blk[2 * blk] & ~same_blk[blk], d["l_ab"], 0.0), d["tinv"])
            yield
        for d in pre:
            d["tinv"] = d["tinv"] + dot(d["tinv"], d["pw"])
            yield
        blk *= 2
    for d in pre:
        d["lakv"] = dot(d["l_ak"], d["v4"])
        d["lrkv"] = dot(d["l_rk"], d["v4"])
        d["khv"] = dot(d["kh4"], d["v4"], _TN)
        yield
    for d in pre:
        d["x12"] = dot(d["tinv"], jnp.concatenate([d["at4"], d["lakv"]], axis=1))
        yield
    for c, d in zip(chunks, pre):
        mn = dot(d["bh4"], d["x12"], _TN)
        pz = dot(d["l_rb"], d["x12"])
        m_s[c] = jnp.where(ri == ci, d["g_end"], 0.0) + mn[:, :GROUP_W]
        n_s[c] = mn[:, GROUP_W:] + d["khv"]
        p_s[c] = d["rt4"] + pz[:, :GROUP_W]
        z_s[c] = pz[:, GROUP_W:] + d["lrkv"]
        yield

    h = h_ref[...]
    for c in range(tb // ch):
        y4 = dot(p_s[c], h) + z_s[c]
        y = y4[0:ch]
        for j in range(1, nh):
            y = y + y4[j * ch:(j + 1) * ch]
        y_s[c * ch:(c + 1) * ch, :] = y
        h = dot(m_s[c], h) + n_s[c]
        yield
    h_ref[...] = h

    y = y_s[...]
    r, k, v = r_s[...], k_s[...], v_s[...]
    mean = _head_sum(y, bd_bf16) * (1.0 / HEAD_W)
    d = y - mean
    var = _head_sum(d * d, bd_bf16) * (1.0 / HEAD_W)
    y = d * lax.rsqrt(var + RWKV_GN_EPS) * lnw_ref[...] + lnb_ref[...]
    y = y + _head_sum(r * k * rk_ref[...], bd_bf16) * v
    o_ref[...] = _rms(y * g_ref[...], mg_ref[...])


def _rwkv_part(proj3, abg3, lyr, p, merge_g, tb):
    bsz, t, _ = proj3.shape
    row = lambda x: x.reshape(1, -1)
    abg = lambda j: _col_spec(tb, j)
    mats = pltpu.VMEM((tb // RWKV_CHUNK, GROUP_W, GROUP_W), F32)
    return dict(
        body=_rwkv_kernel, stages=15 * (tb // RWKV_CHUNK) + 1,
        args=[proj3, proj3, proj3, proj3, abg3, abg3, abg3, row(p["rwkv_r_k"][lyr]),
              row(p["rwkv_ln_w"][lyr]), row(p["rwkv_ln_b"][lyr]), row(merge_g)],
        in_specs=[_col_spec(tb, COL_R), _col_spec(tb, COL_K), _col_spec(tb, COL_V),
                  _col_spec(tb, COL_LORA), abg(EXT_A), abg(EXT_B), abg(EXT_G),
                  _row_spec(), _row_spec(), _row_spec(), _row_spec()],
        out_specs=[_out_blk(tb)], out_shape=[jax.ShapeDtypeStruct((bsz, t, GROUP_W), F32)],
        scratch=[pltpu.VMEM((GROUP_W, GROUP_W), F32), pltpu.VMEM((tb, GROUP_W), F32),
                 mats, mats, mats, mats])


def _bf16_bits(x):
    u = lax.bitcast_convert_type(x, jnp.uint32)
    r = u + jnp.uint32(0x7FFF) + ((u >> 16) & jnp.uint32(1))
    return r & jnp.uint32(0xFFFF0000)


def _slab_copies(rows_ref, slab_hbm, row0, sem):
    tm = rows_ref.shape[0]
    return [(rows_ref.at[:, pl.ds(j * LANES, LANES)], slab_hbm.at[pl.ds(row0, tm), j, :], sem)
            for j in range(slab_hbm.shape[1])]


def _merge_router_kernel(oa_ref, ob_ref, oc_ref, od_ref, h_ref, wo_ref, nw_ref, wr_ref, br_ref,
                         h1_ref, hn_hbm, ids_ref, wts_ref, cnt_ref, pk_s, sem):
    i = pl.program_id(0)
    steps = pl.num_programs(0)
    tm = h_ref.shape[0]
    slot = i % 2

    def writes(s, step):
        return [pltpu.make_async_copy(v, h, m) for v, h, m in
                _slab_copies(pk_s.at[s], hn_hbm, step * tm, sem.at[s])]

    mix = jnp.concatenate([oa_ref[...], ob_ref[...], oc_ref[...], od_ref[...]], axis=1)
    h1 = h_ref[...] + jnp.dot(mix.astype(BF16), wo_ref[...], preferred_element_type=F32)
    h1_ref[...] = h1
    hn = _rms(h1, nw_ref[...])
    half = hn.shape[1] // 2

    @pl.when(i >= 2)
    def _():
        for cp in writes(slot, i - 2):
            cp.wait()

    pk_s[slot] = _bf16_bits(hn[:, :half]) | (_bf16_bits(hn[:, half:]) >> 16)
    for cp in writes(slot, i):
        cp.start()

    @pl.when(i == steps - 1)
    def _():
        for cp in writes(slot, i):
            cp.wait()

    @pl.when((i == steps - 1) & (i >= 1))
    def _():
        for cp in writes(1 - slot, i - 1):
            cp.wait()

    logits = _dg3(hn, wr_ref[...], _NN) + br_ref[...]
    lane = lax.broadcasted_iota(jnp.int32, logits.shape, 1)
    big = jnp.int32(ROUTER_LANES)
    is_c = (lane >= N_EXPERTS) & (lane < N_EXPERTS + MOE_GROUPS)
    cm = jnp.max(jnp.where(is_c, logits, NEG_BIG), axis=-1, keepdims=True)
    gsel = jnp.min(jnp.where(is_c & (logits == cm), lane, big), axis=-1, keepdims=True) - N_EXPERTS
    p_g = 1.0 / jnp.sum(jnp.where(is_c, jnp.exp(logits - cm), 0.0), axis=-1, keepdims=True)
    lo = gsel * MOE_PER_GROUP
    in_g = (lane >= lo) & (lane < lo + MOE_PER_GROUP)
    m1 = jnp.max(jnp.where(in_g, logits, NEG_BIG), axis=-1, keepdims=True)
    i1 = jnp.min(jnp.where(in_g & (logits == m1), lane, big), axis=-1, keepdims=True)
    in_g2 = in_g & (lane != i1)
    m2 = jnp.max(jnp.where(in_g2, logits, NEG_BIG), axis=-1, keepdims=True)
    i2 = jnp.min(jnp.where(in_g2 & (logits == m2), lane, big), axis=-1, keepdims=True)
    w1 = p_g / (1.0 + jnp.exp(m2 - m1))
    w2 = p_g - w1
    two = lax.broadcasted_iota(jnp.int32, ids_ref.shape, 1)
    ids_ref[...] = jnp.where(two == 0, i1, i2)
    wts_ref[...] = jnp.where(two == 0, w1, w2)

    @pl.when(pl.program_id(0) == 0)
    def _():
        cnt_ref[...] = jnp.zeros_like(cnt_ref)

    cnt_ref[...] += jnp.sum(((lane == i1) | (lane == i2)).astype(F32), axis=0, keepdims=True)


def _merge_router(outs, h2d, w_out_bf16, norm_w, w_router, b_router, tm):
    n, d = h2d.shape
    grp = pl.BlockSpec((tm, GROUP_W), lambda i: (i, 0))
    full = lambda a, b: pl.BlockSpec((a, b), lambda i: (0, 0))
    rowblk = lambda w: pl.BlockSpec((tm, w), lambda i: (i, 0))
    return pl.pallas_call(
        _merge_router_kernel,
        grid=(n // tm,),
        in_specs=[grp, grp, grp, grp, rowblk(d), full(d, d), full(1, d), full(d, ROUTER_LANES),
                  full(1, ROUTER_LANES)],
        out_specs=[rowblk(d), pl.BlockSpec(memory_space=pl.ANY),
                   rowblk(MOE_TOPK), rowblk(MOE_TOPK), full(1, ROUTER_LANES)],
        out_shape=[jax.ShapeDtypeStruct((n, d), F32),
                   jax.ShapeDtypeStruct((n, d // 2 // LANES, LANES), jnp.uint32),
                   jax.ShapeDtypeStruct((n, MOE_TOPK), jnp.int32),
                   jax.ShapeDtypeStruct((n, MOE_TOPK), F32),
                   jax.ShapeDtypeStruct((1, ROUTER_LANES), F32)],
        scratch_shapes=[pltpu.VMEM((2, tm, d // 2), jnp.uint32), pltpu.SemaphoreType.DMA((2,))],
        compiler_params=_cparams("arbitrary"),
        name="merge_router",
    )(*[o.reshape(n, GROUP_W) for o in outs], h2d, w_out_bf16, norm_w.reshape(1, d), w_router, b_router)


def _row_copy(src_hbm, src_row, dst_ref, dst_row, sem):
    return pltpu.make_async_copy(src_hbm.at[pl.ds(src_row, 1), :], dst_ref.at[pl.ds(dst_row, 1), :], sem)


def _position_kernel(ids_ref, base_ref, pos_ref, run_ref):
    tm = ids_ref.shape[0]

    @pl.when(pl.program_id(0) == 0)
    def _():
        run_ref[...] = jnp.zeros_like(run_ref)

    i1, i2 = ids_ref[:, 0:1], ids_ref[:, 1:2]
    lane = lax.broadcasted_iota(jnp.int32, (tm, ROUTER_LANES), 1)
    hit = ((lane == i1) | (lane == i2)).astype(BF16)
    earlier = (lax.broadcasted_iota(jnp.int32, (tm, tm), 0)
               > lax.broadcasted_iota(jnp.int32, (tm, tm), 1)).astype(BF16)
    rank = jnp.dot(earlier, hit, preferred_element_type=F32) + run_ref[...]
    where_to = rank + base_ref[...]
    p1 = jnp.sum(jnp.where(lane == i1, where_to, 0.0), axis=-1, keepdims=True)
    p2 = jnp.sum(jnp.where(lane == i2, where_to, 0.0), axis=-1, keepdims=True)
    two = lax.broadcasted_iota(jnp.int32, pos_ref.shape, 1)
    pos_ref[...] = jnp.where(two == 0, p1, p2).astype(jnp.int32)
    run_ref[...] += jnp.sum(hit.astype(F32), axis=0, keepdims=True)


def _positions(ids, group_start, tm):
    n = ids.shape[0]
    return pl.pallas_call(
        _position_kernel,
        grid=(n // tm,),
        in_specs=[pl.BlockSpec((tm, MOE_TOPK), lambda i: (i, 0)),
                  pl.BlockSpec((1, ROUTER_LANES), lambda i: (0, 0))],
        out_specs=pl.BlockSpec((tm, MOE_TOPK), lambda i: (i, 0)),
        out_shape=jax.ShapeDtypeStruct((n, MOE_TOPK), jnp.int32),
        scratch_shapes=[pltpu.VMEM((1, ROUTER_LANES), F32)],
        compiler_params=_cparams("arbitrary"),
        name="moe_positions",
    )(ids, group_start)


def _dispatch_kernel(p0_ref, p1_ref, x_ref, xs_in_hbm, xs_hbm, sem):
    del xs_in_hbm
    n = x_ref.shape[0]

    def copies(i):
        return (pltpu.make_async_copy(x_ref.at[i], xs_hbm.at[p0_ref[i]], sem),
                pltpu.make_async_copy(x_ref.at[i], xs_hbm.at[p1_ref[i]], sem))

    def issue(i, c):
        for prio, cp in enumerate(copies(i)):
            cp.start(priority=prio)
        return c

    def drain(i, c):
        for cp in copies(i):
            cp.wait()
        return c

    lax.fori_loop(0, n, issue, 0, unroll=8)
    lax.fori_loop(0, n, drain, 0, unroll=8)


def _dispatch(x, pos0, pos1, n_rows, chunk):
    n, slabs, lanes = x.shape
    smem = lambda: pl.BlockSpec((chunk,), lambda s: (s,), memory_space=pltpu.SMEM)
    return pl.pallas_call(
        _dispatch_kernel,
        grid=(n // chunk,),
        in_specs=[smem(), smem(), pl.BlockSpec((chunk, slabs, lanes), lambda s: (s, 0, 0)),
                  pl.BlockSpec(memory_space=pl.ANY)],
        out_specs=pl.BlockSpec(memory_space=pl.ANY),
        out_shape=jax.ShapeDtypeStruct((n_rows, slabs, lanes), x.dtype),
        input_output_aliases={3: 0},
        scratch_shapes=[pltpu.SemaphoreType.DMA(())],
        compiler_params=_cparams("arbitrary"),
        name="moe_dispatch",
    )(pos0, pos1, x, jnp.zeros((n_rows, slabs, lanes), x.dtype))


def _expert_kernel(te_ref, nt_ref, xs_hbm, wg_ref, wu_ref, wd_ref, o_ref, wg_s, wu_s, wd_s, x_s, sem):
    i = pl.program_id(0)
    tm = o_ref.shape[0]
    slot = i % 2

    def reads(s, tile):
        return [pltpu.make_async_copy(h, v, m) for v, h, m in
                _slab_copies(x_s.at[s], xs_hbm, tile * tm, sem.at[s])]

    @pl.when(i == 0)
    def _():
        for cp in reads(0, 0):
            cp.start()

    @pl.when(i + 1 < pl.num_programs(0))
    def _():
        for cp in reads(1 - slot, i + 1):
            cp.start()

    @pl.when((i == 0) | (te_ref[i] != te_ref[jnp.maximum(i - 1, 0)]))
    def _():
        wg_s[...] = wg_ref[...].astype(BF16)
        wu_s[...] = wu_ref[...].astype(BF16)
        wd_s[...] = wd_ref[...].astype(BF16)

    for cp in reads(slot, i):
        cp.wait()

    @pl.when(i < nt_ref[0])
    def _():
        p = x_s[slot]
        xa = lax.bitcast_convert_type(p & jnp.uint32(0xFFFF0000), F32).astype(BF16)
        xb = lax.bitcast_convert_type(p << 16, F32).astype(BF16)
        x = jnp.concatenate([xa, xb], axis=1)
        he = (_silu(jnp.dot(x, wg_s[...], preferred_element_type=F32))
              * jnp.dot(x, wu_s[...], preferred_element_type=F32))
        o_ref[...] = jnp.dot(he.astype(BF16), wd_s[...], preferred_element_type=F32)

    @pl.when(i >= nt_ref[0])
    def _():
        o_ref[...] = jnp.zeros_like(o_ref)


def _experts(xs, tile_expert, n_tiles_used, wg, wu, wd, layer, tm):
    p, slabs, lanes = xs.shape
    d = 2 * slabs * lanes
    pick = lambda i, te, nt: (layer, te[i], 0, 0)
    grid_spec = pltpu.PrefetchScalarGridSpec(
        num_scalar_prefetch=2,
        grid=(p // tm,),
        in_specs=[pl.BlockSpec(memory_space=pl.ANY),
                  pl.BlockSpec((None, None, d, D_EXPERT), pick),
                  pl.BlockSpec((None, None, d, D_EXPERT), pick),
                  pl.BlockSpec((None, None, D_EXPERT, d), pick)],
        out_specs=pl.BlockSpec((tm, d), lambda i, te, nt: (i, 0)),
        scratch_shapes=[pltpu.VMEM((d, D_EXPERT), BF16), pltpu.VMEM((d, D_EXPERT), BF16),
                        pltpu.VMEM((D_EXPERT, d), BF16), pltpu.VMEM((2, tm, slabs * lanes), xs.dtype),
                        pltpu.SemaphoreType.DMA((2,))],
    )
    return pl.pallas_call(
        _expert_kernel,
        grid_spec=grid_spec,
        out_shape=jax.ShapeDtypeStruct((p, d), F32),
        compiler_params=_cparams("arbitrary"),
        name="moe_experts",
    )(tile_expert, n_tiles_used, xs, wg, wu, wd)


def _combine_kernel(i0_ref, i1_ref, h1_ref, w_ref, fw_ref, ys_hbm, o_ref, buf0, buf1, sem, *, final_norm):
    n = o_ref.shape[0]

    def issue(i, c):
        _row_copy(ys_hbm, i0_ref[i], buf0, i, sem).start(priority=0)
        _row_copy(ys_hbm, i1_ref[i], buf1, i, sem).start(priority=1)
        return c

    def drain(i, c):
        _row_copy(ys_hbm, 0, buf0, i, sem).wait()
        _row_copy(ys_hbm, 0, buf1, i, sem).wait()
        return c

    lax.fori_loop(0, n, issue, 0, unroll=8)
    lax.fori_loop(0, n, drain, 0, unroll=8)
    out = h1_ref[...] + w_ref[:, 0:1] * buf0[...] + w_ref[:, 1:2] * buf1[...]
    if final_norm:
        out = _rms(out, fw_ref[...])
    o_ref[...] = out


def _combine(ys, pos0, pos1, wts, h1, final_w, final_norm, chunk):
    n, d = h1.shape
    smem = lambda: pl.BlockSpec((chunk,), lambda s: (s,), memory_space=pltpu.SMEM)
    return pl.pallas_call(
        functools.partial(_combine_kernel, final_norm=final_norm),
        grid=(n // chunk,),
        in_specs=[smem(), smem(), pl.BlockSpec((chunk, d), lambda s: (s, 0)),
                  pl.BlockSpec((chunk, MOE_TOPK), lambda s: (s, 0)),
                  pl.BlockSpec((1, d), lambda s: (0, 0)), pl.BlockSpec(memory_space=pl.ANY)],
        out_specs=pl.BlockSpec((chunk, d), lambda s: (s, 0)),
        out_shape=jax.ShapeDtypeStruct((n, d), F32),
        scratch_shapes=[pltpu.VMEM((chunk, d), F32), pltpu.VMEM((chunk, d), F32),
                        pltpu.SemaphoreType.DMA(())],
        compiler_params=_cparams("arbitrary"),
        name="moe_combine",
    )(pos0, pos1, h1, wts, final_w.reshape(1, d), ys)


def _moe(hn_packed, ids, wts, counts, h1, wg, wu, wd, layer, final_w, final_norm, tm):
    n = h1.shape[0]
    n_tiles = (n * MOE_TOPK) // tm + N_EXPERTS
    cnt = counts[0, :N_EXPERTS].astype(jnp.int32)
    padded = ((cnt + tm - 1) // tm) * tm
    ends = jnp.cumsum(padded)
    tile_start = jnp.arange(n_tiles, dtype=jnp.int32) * tm
    tile_expert = jnp.minimum(jnp.sum(ends[None, :] <= tile_start[:, None], axis=1),
                              N_EXPERTS - 1).astype(jnp.int32)
    n_tiles_used = (ends[-1] // tm).astype(jnp.int32).reshape(1)
    group_start = jnp.zeros((1, ROUTER_LANES), F32).at[0, :N_EXPERTS].set((ends - padded).astype(F32))
    pos = _positions(ids, group_start, _pick_div(n, 512))
    pos0, pos1 = pos[:, 0], pos[:, 1]
    xs = _dispatch(hn_packed, pos0, pos1, n_tiles * tm, _pick_div(n, 1024))
    ys = _experts(xs, tile_expert, n_tiles_used, wg, wu, wd, layer, tm)
    return _combine(ys, pos0, pos1, wts, h1, final_w, final_norm, _pick_div(n, 1024))


def _pick_div(n, pref):
    while n % pref:
        pref //= 2
    return pref


def _pick(n, pref):
    return pref if n % pref == 0 else n


def kernel(x, norm1_w, w_in, hgrn_lb_logits, hgrn_norm_w, s5_lambda_re, s5_lambda_im, s5_log_dt, s5_b_re, s5_b_im, s5_c_re, s5_c_im, s5_d, s5_w_glu, rwkv_mu, rwkv_w0, rwkv_w2, rwkv_a0, rwkv_a2, rwkv_g2, rwkv_k_k, rwkv_k_a, rwkv_r_k, rwkv_v0, rwkv_v1, rwkv_v2, rwkv_ln_w, rwkv_ln_b, lru_conv_w, lru_conv_b, lru_wa, lru_ba, lru_wx, lru_bx, lru_lambda, merge_gain, w_out, norm2_w, moe_coarse_w, moe_coarse_b, moe_fine_w, moe_fine_b, moe_w_gate, moe_w_up, moe_w_down, final_norm_w):
    bsz, t, d = x.shape
    n = bsz * t
    depth = w_in.shape[0]
    rw = dict(rwkv_mu=rwkv_mu, rwkv_w0=rwkv_w0, rwkv_w2=rwkv_w2, rwkv_a0=rwkv_a0, rwkv_a2=rwkv_a2,
              rwkv_g2=rwkv_g2, rwkv_k_k=rwkv_k_k, rwkv_k_a=rwkv_k_a, rwkv_r_k=rwkv_r_k,
              rwkv_v0=rwkv_v0, rwkv_v1=rwkv_v1, rwkv_v2=rwkv_v2, rwkv_ln_w=rwkv_ln_w,
              rwkv_ln_b=rwkv_ln_b)
    lb_all = jnp.cumsum(jax.nn.softmax(hgrn_lb_logits.astype(F32), axis=0), axis=0)
    lb_all = lb_all - lb_all[:1]

    tm_proj = _pick(n, 512)
    tm_in = _pick_div(t, 512)
    tm_moe = 512
    tb_mix = _pick(t, 256)
    tb_s5 = _pick(t, 512)

    h = x.reshape(n, d)
    v_first = None
    for l in range(depth):
        mg = merge_gain[l].reshape(4, GROUP_W)
        proj, abg, v_first = _in_proj(h, norm1_w[l], w_in[l].astype(BF16), v_first, l, rw, t, tm_in)
        proj3 = proj.reshape(bsz, t, -1)
        abg3 = abg.reshape(bsz, t, -1)
        mats = _s5_matrices(s5_lambda_re[l], s5_lambda_im[l], s5_log_dt[l], s5_b_re[l], s5_b_im[l],
                            s5_c_re[l], s5_c_im[l])
        o_b = _s5(proj3, mats, s5_d[l], s5_w_glu[l].astype(BF16), mg[1], tb_s5)
        parts = [_rwkv_part(proj3, abg3, l, rw, mg[2], tb_mix),
                 _hgrn_part(proj3, lb_all[l], hgrn_norm_w[l], mg[0], tb_mix),
                 _lru_part(proj3, lru_conv_w[l], lru_conv_b[l], _block_diag_weight(lru_wa[l]),
                           lru_ba[l], _block_diag_weight(lru_wx[l]), lru_bx[l], lru_lambda[l],
                           mg[3], tb_mix)]
        outs = _mixers(parts, bsz, t // tb_mix, "mixers")
        o_c, o_a, o_d = outs
        w_router = jnp.concatenate(
            [moe_fine_w[l].transpose(1, 0, 2).reshape(d, N_EXPERTS), moe_coarse_w[l],
             jnp.zeros((d, ROUTER_LANES - N_EXPERTS - MOE_GROUPS), F32)], axis=1)
        b_router = jnp.concatenate(
            [moe_fine_b[l].reshape(N_EXPERTS), moe_coarse_b[l],
             jnp.zeros((ROUTER_LANES - N_EXPERTS - MOE_GROUPS,), F32)]).reshape(1, ROUTER_LANES)
        h1, hn, ids, wts, counts = _merge_router((o_a, o_b, o_c, o_d), h, w_out[l].astype(BF16),
                                                 norm2_w[l], w_router, b_router, tm_proj)
        h = _moe(hn, ids, wts, counts, h1, moe_w_gate, moe_w_up, moe_w_down, l,
                 final_norm_w, l == depth - 1, tm_moe)
    return h.reshape(bsz, t, d)
```

```python
import functools

import jax
import jax.numpy as jnp
from jax import lax
from jax.experimental import pallas as pl
from jax.experimental.pallas import tpu as pltpu

F32 = jnp.float32
BF16 = jnp.bfloat16

GROUP_W = 256
RMS_EPS = 1e-6
HEAD_W = 64
HGRN_CHUNK = 16
S5_CH = 16
S5_GROUPS = GROUP_W // S5_CH
S5_STATE = 64
S5_CHUNK = 16
S5_STEPS = 4
RWKV_CHUNK = 64
RWKV_INV_BLOCK = 8
RWKV_GN_EPS = 64e-5
RWKV_W_LORA = 64
RWKV_V_LORA = 32
LRU_CONV = 4
LRU_C = 8.0
MOE_GROUPS = 4
MOE_PER_GROUP = 8
N_EXPERTS = MOE_GROUPS * MOE_PER_GROUP
MOE_TOPK = 2
D_EXPERT = 512
LANES = 128
SUBLANES = 8
ROUTER_LANES = LANES
NEG_BIG = -1e30
V7X_VMEM_BYTES = 64 * 1024 * 1024
VMEM_LIMIT = V7X_VMEM_BYTES - 8 * 1024 * 1024

COL_HQ, COL_HF, COL_HI, COL_HG, COL_S5, COL_R, COL_K, COL_V, COL_LORA, COL_LG, COL_LX = range(11)
EXT_A, EXT_B, EXT_G = range(3)


def _cparams(*sem):
    return pltpu.CompilerParams(dimension_semantics=sem, vmem_limit_bytes=VMEM_LIMIT)


def _mm(a, b):
    return jnp.dot(a.astype(BF16), b.astype(BF16), preferred_element_type=F32)


def _split2(x):
    hi = x.astype(BF16)
    lo = (x - hi.astype(F32)).astype(BF16)
    return hi, lo


def _dg3(a, b, dims):
    ah, al = _split2(a)
    bh, bl = _split2(b)
    d = lambda x, y: lax.dot_general(x, y, (dims, ((), ())), preferred_element_type=F32)
    return d(ah, bh) + d(ah, bl) + d(al, bh)


_NN = ((1,), (0,))
_NT = ((1,), (1,))
_TN = ((0,), (0,))


def _exact_lhs_mm(m_bf16, x):
    h1 = x.astype(BF16)
    r1 = x - h1.astype(F32)
    h2 = r1.astype(BF16)
    h3 = (r1 - h2.astype(F32)).astype(BF16)
    d = lambda y: jnp.dot(m_bf16, y, preferred_element_type=F32)
    return d(h1) + d(h2) + d(h3)


def _head_sum(x, bd_bf16):
    return jnp.dot(x.astype(BF16), bd_bf16, preferred_element_type=F32)


def _block_diag_mask(n, blk):
    r = lax.broadcasted_iota(jnp.int32, (n, n), 0) // blk
    c = lax.broadcasted_iota(jnp.int32, (n, n), 1) // blk
    return r == c


def _rms(x, w):
    return x * lax.rsqrt(jnp.mean(x * x, axis=-1, keepdims=True) + RMS_EPS) * w


def _silu(x):
    return x * jax.nn.sigmoid(x)


def _softplus(x):
    return jnp.maximum(x, 0.0) + jnp.log(1.0 + jnp.exp(-jnp.abs(x)))


def _in_proj_kernel(*refs, has_vmix, tiles_per_seq):
    if has_vmix:
        (x_ref, nw_ref, w_ref, mu_ref, w0_ref, w2_ref, a0_ref, a2_ref, g2_ref, kk_ref, ka_ref,
         vf_ref, v0_ref, v1_ref, v2_ref, o_ref, o2_ref, prev_s) = refs
    else:
        (x_ref, nw_ref, w_ref, mu_ref, w0_ref, w2_ref, a0_ref, a2_ref, g2_ref, kk_ref, ka_ref,
         o_ref, o2_ref, vf_out_ref, prev_s) = refs
    tm = x_ref.shape[0]
    gw = GROUP_W
    c0, c1 = COL_R * gw, (COL_LORA + 1) * gw
    @pl.when((pl.program_id(0) % tiles_per_seq) == 0)
    def _():
        prev_s[...] = jnp.zeros_like(prev_s)

    y = _rms(x_ref[...], nw_ref[...]).astype(BF16)
    raw = jnp.dot(y, w_ref[:, c0:c1], preferred_element_type=F32)

    row0 = lax.broadcasted_iota(jnp.int32, (tm, 1), 0) == 0
    sh = jnp.where(row0, prev_s[...], pltpu.roll(raw, 1, axis=0))
    prev_s[...] = raw[tm - 1:tm, :]
    pf = raw + mu_ref[...] * (sh - raw)
    r, k, v, lo = (pf[:, j * GROUP_W:(j + 1) * GROUP_W] for j in range(4))

    o_ref[:, :c0] = jnp.dot(y, w_ref[:, :c0], preferred_element_type=F32)
    bd_bf16 = _block_diag_mask(GROUP_W, HEAD_W).astype(BF16)
    lo_wa, lo_g = lo[:, :LANES], lo[:, LANES:]
    w_raw = -_softplus(-(w0_ref[...] + _mm(jnp.tanh(lo_wa), w2_ref[...]))) - 0.5
    alr = jax.nn.sigmoid(a0_ref[...] + _mm(lo_wa, a2_ref[...]))
    o_ref[:, c1:] = jnp.dot(y, w_ref[:, c1:], preferred_element_type=F32)
    kkr = k * kk_ref[...]
    kk = kkr / jnp.maximum(jnp.sqrt(_head_sum(kkr * kkr, bd_bf16)), 1e-12)
    if has_vmix:
        gate = jax.nn.sigmoid(v0_ref[...] + _mm(_mm(v, v1_ref[...]), v2_ref[...]))
        v = v + (vf_ref[...] - v) * gate
    else:
        vf_out_ref[...] = v
    o_ref[:, c0:c0 + GROUP_W] = r
    o_ref[:, c0 + GROUP_W:c0 + 2 * GROUP_W] = k * (1.0 + (alr - 1.0) * ka_ref[...])
    o_ref[:, c0 + 2 * GROUP_W:c0 + 3 * GROUP_W] = v
    o_ref[:, c0 + 3 * GROUP_W:c1] = -jnp.exp(w_raw)
    o2_ref[:, EXT_A * gw:(EXT_A + 1) * gw] = -kk
    o2_ref[:, EXT_B * gw:(EXT_B + 1) * gw] = kk * alr
    o2_ref[:, EXT_G * gw:(EXT_G + 1) * gw] = _mm(jax.nn.sigmoid(lo_g), g2_ref[...])


def _pad_rows(w, start, total=LANES):
    out = jnp.zeros((total, w.shape[1]), F32).at[start:start + w.shape[0]].set(w)
    return out.astype(BF16)


def _in_proj(h2d, norm_w, w_bf16, v_first, lyr, p, t, tm):
    n, d = h2d.shape
    d_in = w_bf16.shape[1]
    has_vmix = v_first is not None
    row = lambda x: x.reshape(1, -1)
    full = lambda a, b: pl.BlockSpec((a, b), lambda i: (0, 0))
    rowblk = lambda w: pl.BlockSpec((tm, w), lambda i: (i, 0))
    args = [h2d, row(norm_w), w_bf16, row(p["rwkv_mu"][lyr]), row(p["rwkv_w0"][lyr]),
            _pad_rows(p["rwkv_w2"][lyr], 0), row(p["rwkv_a0"][lyr]),
            _pad_rows(p["rwkv_a2"][lyr], RWKV_W_LORA), p["rwkv_g2"][lyr].astype(BF16),
            row(p["rwkv_k_k"][lyr]), row(p["rwkv_k_a"][lyr])]
    specs = [rowblk(d), full(1, d), full(d, d_in), full(1, 4 * GROUP_W), full(1, GROUP_W),
             full(LANES, GROUP_W), full(1, GROUP_W), full(LANES, GROUP_W), full(LANES, GROUP_W),
             full(1, GROUP_W), full(1, GROUP_W)]
    out_specs = [rowblk(d_in), rowblk(3 * GROUP_W)]
    out_shape = [jax.ShapeDtypeStruct((n, d_in), F32), jax.ShapeDtypeStruct((n, 3 * GROUP_W), F32)]
    if has_vmix:
        v1 = jnp.zeros((GROUP_W, LANES), F32).at[:, :RWKV_V_LORA].set(p["rwkv_v1"][lyr - 1]).astype(BF16)
        v2 = jnp.zeros((LANES, GROUP_W), F32).at[:RWKV_V_LORA].set(p["rwkv_v2"][lyr - 1]).astype(BF16)
        args += [v_first, row(p["rwkv_v0"][lyr - 1]), v1, v2]
        specs += [rowblk(GROUP_W), full(1, GROUP_W), full(GROUP_W, LANES), full(LANES, GROUP_W)]
    else:
        out_specs.append(rowblk(GROUP_W))
        out_shape.append(jax.ShapeDtypeStruct((n, GROUP_W), F32))
    res = pl.pallas_call(
        functools.partial(_in_proj_kernel, has_vmix=has_vmix, tiles_per_seq=t // tm),
        grid=(n // tm,),
        in_specs=specs,
        out_specs=out_specs,
        out_shape=out_shape,
        scratch_shapes=[pltpu.VMEM((1, 4 * GROUP_W), F32)],
        compiler_params=_cparams("arbitrary"),
        name="in_proj",
    )(*args)
    return res[0], res[1], (v_first if has_vmix else res[2])


def _col_spec(tb, col):
    return pl.BlockSpec((None, tb, GROUP_W), lambda b, t: (b, t, col))


def _row_spec(width=GROUP_W):
    return pl.BlockSpec((1, width), lambda b, t: (0, 0))


def _full_spec(shape):
    return pl.BlockSpec(shape, lambda b, t: (0,) * len(shape))


def _hgrn_kernel(q_ref, f_ref, i_ref, g_ref, lb_ref, nw_ref, mg_ref, o_ref,
                 st_ref, q_s, k_s, v_s, lf_s, o_s, *, reset):
    ch = HGRN_CHUNK
    tb = q_ref.shape[0]
    if reset:
        st_ref[...] = jnp.zeros_like(st_ref)
        return

    lb = lb_ref[...]
    fx = f_ref[...]
    x1 = jnp.log(lb)
    x2 = jnp.log(1.0 - lb) - _softplus(-fx)
    m = jnp.maximum(x1, x2)
    lf_s[...] = m + jnp.log(jnp.exp(x1 - m) + jnp.exp(x2 - m))
    q_s[...] = _silu(q_ref[...])
    k_s[...] = (1.0 - lb) * jax.nn.sigmoid(-fx)
    v_s[...] = _silu(i_ref[...])
    yield

    bd = _block_diag_mask(GROUP_W, HEAD_W)
    bd_bf16 = bd.astype(BF16)
    tri = (lax.broadcasted_iota(jnp.int32, (ch, ch), 0)
           >= lax.broadcasted_iota(jnp.int32, (ch, ch), 1))
    tri_bf16 = tri.astype(BF16)
    tri3 = (lax.broadcasted_iota(jnp.int32, (ch, ch, 1), 0)
            <= lax.broadcasted_iota(jnp.int32, (ch, ch, 1), 1))

    st = st_ref[...]
    for c in range(tb // ch):
        sl = slice(c * ch, (c + 1) * ch)
        qc, kc, vc = q_s[sl, :], k_s[sl, :], v_s[sl, :]
        b = _exact_lhs_mm(tri_bf16, lf_s[sl, :])
        rel = b[None, :, :] - b[:, None, :]
        dec = jnp.exp(jnp.where(tri3, rel, NEG_BIG))
        p = (qc[None, :, :] * kc[:, None, :]) * dec
        sc = jnp.dot(p.reshape(ch * ch, GROUP_W).astype(BF16), bd_bf16,
                     preferred_element_type=F32).reshape(ch, ch, GROUP_W)
        o_intra = jnp.sum(sc * vc[:, None, :], axis=0)
        o_inter = lax.dot_general((qc * jnp.exp(b)).astype(BF16), st.astype(BF16),
                                  (_NT, ((), ())), preferred_element_type=F32)
        b_end = b[ch - 1:ch, :]
        kh = kc * jnp.exp(b_end - b)
        upd = lax.dot_general(vc.astype(BF16), kh.astype(BF16), (_TN, ((), ())),
                              preferred_element_type=F32)
        st = st * jnp.exp(b_end) + jnp.where(bd, upd, 0.0)
        o_s[sl, :] = o_intra + o_inter
        yield
    st_ref[...] = st

    o = o_s[...]
    ms = _head_sum(o * o, bd_bf16) * (1.0 / HEAD_W)
    o = o * lax.rsqrt(ms + RMS_EPS) * nw_ref[...] * _silu(g_ref[...])
    o_ref[...] = _rms(o, mg_ref[...])


def _out_blk(tb):
    return pl.BlockSpec((None, tb, GROUP_W), lambda b, i: (b, i, 0))


def _hgrn_part(proj3, lb, norm_w, merge_g, tb):
    bsz, t, _ = proj3.shape
    blk = pltpu.VMEM((tb, GROUP_W), F32)
    return dict(
        body=_hgrn_kernel, stages=tb // HGRN_CHUNK + 2,
        args=[proj3, proj3, proj3, proj3, lb.reshape(1, -1), norm_w.reshape(1, -1), merge_g.reshape(1, -1)],
        in_specs=[_col_spec(tb, COL_HQ), _col_spec(tb, COL_HF), _col_spec(tb, COL_HI),
                  _col_spec(tb, COL_HG), _row_spec(), _row_spec(), _row_spec()],
        out_specs=[_out_blk(tb)],
        out_shape=[jax.ShapeDtypeStruct((bsz, t, GROUP_W), F32)],
        scratch=[pltpu.VMEM((GROUP_W, GROUP_W), F32), blk, blk, blk, blk, blk])


def _mixer_kernel(*refs, parts):
    groups, i = [], 0
    for kind in range(3):
        for body, counts in parts:
            groups.append(refs[i:i + counts[kind]])
            i += counts[kind]
    k = len(parts)
    per_part = [groups[j] + groups[k + j] + groups[2 * k + j] for j in range(k)]

    @pl.when(pl.program_id(1) == 0)
    def _():
        for (body, _), r in zip(parts, per_part):
            for _step in body(*r, reset=True):
                pass

    runs = [[body(*r, reset=False), 0, counts[3]] for (body, counts), r in zip(parts, per_part)]
    while runs:
        run = min(runs, key=lambda x: x[1] / x[2])
        try:
            next(run[0])
            run[1] += 1
        except StopIteration:
            runs.remove(run)


def _mixers(parts, bsz, n_tblocks, name):
    cat = lambda key: [x for p in parts for x in p[key]]
    light = tuple((p["body"], (len(p["args"]), len(p["out_specs"]), len(p["scratch"]), p["stages"]))
                  for p in parts)
    return pl.pallas_call(
        functools.partial(_mixer_kernel, parts=light),
        grid=(bsz, n_tblocks),
        in_specs=cat("in_specs"),
        out_specs=cat("out_specs"),
        out_shape=cat("out_shape"),
        scratch_shapes=cat("scratch"),
        compiler_params=_cparams("parallel", "arbitrary"),
        name=name,
    )(*cat("args"))


def _lru_kernel(xg_ref, xr_ref, cw_ref, cb_ref, wa_ref, ba_ref, wx_ref, bx_ref, lam_ref, mg_ref,
                o_ref, buf_ref, h_ref, *, reset):
    tb = xr_ref.shape[0]
    pad = SUBLANES
    if reset:
        buf_ref[0:pad, :] = jnp.zeros((pad, GROUP_W), F32)
        h_ref[...] = jnp.zeros_like(h_ref)
        return

    xr = xr_ref[...]
    buf_ref[pad:pad + tb, :] = xr
    xc = cb_ref[...] + jnp.zeros_like(xr)
    for j in range(LRU_CONV):
        xc = xc + cw_ref[j:j + 1, :] * buf_ref[pl.ds(pad - (LRU_CONV - 1) + j, tb), :]
    buf_ref[0:pad, :] = xr[tb - pad:tb, :]

    r = jax.nn.sigmoid(jnp.dot(xc.astype(BF16), wa_ref[...], preferred_element_type=F32) + ba_ref[...])
    gi = jax.nn.sigmoid(jnp.dot(xc.astype(BF16), wx_ref[...], preferred_element_type=F32) + bx_ref[...])
    log_a = -LRU_C * r * _softplus(-lam_ref[...])
    a = jnp.exp(log_a)
    x = jnp.sqrt(1.0 - jnp.exp(2.0 * log_a)) * (gi * xc)
    yield

    rows = lax.broadcasted_iota(jnp.int32, (tb, 1), 0)
    k = 1
    while k < tb:
        keep = rows >= k
        x = x + jnp.where(keep, a * pltpu.roll(x, k, axis=0), 0.0)
        a = jnp.where(keep, a * pltpu.roll(a, k, axis=0), a)
        k *= 2
        yield
    h = x + a * h_ref[...]
    h_ref[...] = h[tb - 1:tb, :]
    o_ref[...] = _rms(jax.nn.gelu(xg_ref[...]) * h, mg_ref[...])


def _lru_part(proj3, conv_w, conv_b, wa_bd, ba, wx_bd, bx, lam, merge_g, tb):
    bsz, t, _ = proj3.shape
    return dict(
        body=_lru_kernel, stages=tb.bit_length() + 1,
        args=[proj3, proj3, conv_w, conv_b.reshape(1, -1), wa_bd, ba.reshape(1, -1), wx_bd,
              bx.reshape(1, -1), lam.reshape(1, -1), merge_g.reshape(1, -1)],
        in_specs=[_col_spec(tb, COL_LG), _col_spec(tb, COL_LX), _full_spec((LRU_CONV, GROUP_W)),
                  _row_spec(), _full_spec((GROUP_W, GROUP_W)), _row_spec(),
                  _full_spec((GROUP_W, GROUP_W)), _row_spec(), _row_spec(), _row_spec()],
        out_specs=[_out_blk(tb)],
        out_shape=[jax.ShapeDtypeStruct((bsz, t, GROUP_W), F32)],
        scratch=[pltpu.VMEM((tb + SUBLANES, GROUP_W), F32), pltpu.VMEM((1, GROUP_W), F32)])


def _block_diag_weight(w):
    h, n, _ = w.shape
    eye = jnp.eye(h, dtype=w.dtype)
    return jnp.einsum('hij,hg->higj', w, eye).reshape(h * n, h * n).astype(BF16)


def _s5_matrices(lam_re, lam_im, log_dt, b_re, b_im, c_re, c_im):
    L, G, P, C = S5_CHUNK, S5_GROUPS, S5_STATE, S5_CH
    lr, li = lam_re.astype(F32), lam_im.astype(F32)
    dt = jnp.exp(log_dt.astype(F32))[:, None]
    mag = jnp.exp(lr * dt)
    a_re, a_im = mag * jnp.cos(li * dt), mag * jnp.sin(li * dt)
    den = lr * lr + li * li
    kap_re = ((a_re - 1.0) * lr + a_im * li) / den
    kap_im = (a_im * lr - (a_re - 1.0) * li) / den
    br, bi = b_re.astype(F32), b_im.astype(F32)
    bb_re = kap_re[..., None] * br - kap_im[..., None] * bi
    bb_im = kap_re[..., None] * bi + kap_im[..., None] * br
    cr, ci = c_re.astype(F32), c_im.astype(F32)
    eye = jnp.eye(G, dtype=F32)
    J = S5_STEPS
    kk = jnp.arange(J + 1, dtype=F32)[:, None, None]
    pmag = jnp.exp(kk * (lr * dt)[None])
    pw_re, pw_im = pmag * jnp.cos(kk * (li * dt)[None]), pmag * jnp.sin(kk * (li * dt)[None])
    ab_re = pw_re[:J, :, :, None] * bb_re[None] - pw_im[:J, :, :, None] * bb_im[None]
    ab_im = pw_re[:J, :, :, None] * bb_im[None] + pw_im[:J, :, :, None] * bb_re[None]
    rev = lambda x: jnp.stack([x[J - 1 - j] for j in range(J)])
    w_in = jnp.concatenate(
        [jnp.einsum('jgpc,gh->jgchp', rev(ab_re), eye).reshape(J * G * C, G * P),
         jnp.einsum('jgpc,gh->jgchp', rev(ab_im), eye).reshape(J * G * C, G * P)], axis=1)
    ca_re = jnp.einsum('gcp,jgp->jgcp', cr, pw_re[1:]) - jnp.einsum('gcp,jgp->jgcp', ci, pw_im[1:])
    ca_im = jnp.einsum('gcp,jgp->jgcp', cr, pw_im[1:]) + jnp.einsum('gcp,jgp->jgcp', ci, pw_re[1:])
    c_out = jnp.concatenate(
        [jnp.einsum('jgcp,gh->gpjhc', ca_re, eye).reshape(G * P, J * G * C),
         -jnp.einsum('jgcp,gh->gpjhc', ca_im, eye).reshape(G * P, J * G * C)], axis=0)
    taps = jnp.einsum('gop,kgpc->kgco', cr, ab_re) - jnp.einsum('gop,kgpc->kgco', ci, ab_im)
    none = jnp.zeros_like(taps[0])
    d_io = jnp.stack([jnp.stack([taps[j - i] if j >= i else none for j in range(J)])
                      for i in range(J)])
    d_io = jnp.einsum('ijgco,gh->igcjho', d_io, eye).reshape(J * G * C, J * G * C)
    a_grp = jnp.stack([pw_re[J].reshape(G * P), pw_im[J].reshape(G * P)])
    mag_l = jnp.exp(L * lr * dt)
    a_chunk = jnp.stack([(mag_l * jnp.cos(L * li * dt)).reshape(G * P),
                         (mag_l * jnp.sin(L * li * dt)).reshape(G * P)])
    return w_in.astype(BF16), c_out.astype(BF16), d_io.astype(BF16), a_grp, a_chunk


def _s5_kernel(u0_ref, u1_ref, w_ref, c_ref, dio_ref, a_ref, al_ref, d_ref, wg_ref, mg_ref, o_ref,
               carry_ref, x_s, st_s, y_s, bu_s):
    nb, tb, lanes = u0_ref.shape
    L, J = S5_CHUNK, S5_STEPS
    r = tb // L
    rows = nb * r
    ns = S5_GROUPS * S5_STATE

    @pl.when(pl.program_id(0) == 0)
    def _():
        carry_ref[...] = jnp.zeros_like(carry_ref)

    a_re, a_im = a_ref[0:1, :], a_ref[1:2, :]
    al_re, al_im = al_ref[0:1, :], al_ref[1:2, :]

    def inputs(q):
        parts = []
        for j in range(J):
            sl = pl.ds(q * J + j, r, stride=L)
            parts += [u0_ref[:, sl, :], u1_ref[:, sl, :]]
        return jnp.concatenate(parts, axis=-1).reshape(rows, J * 2 * lanes).astype(BF16)

    def advance(q):
        bu = bu_s[q]
        xr, xi = x_s[:, :ns], x_s[:, ns:]
        x_s[:, :ns] = a_re * xr - a_im * xi + bu[:, :ns]
        x_s[:, ns:] = a_re * xi + a_im * xr + bu[:, ns:]

    x_s[...] = jnp.zeros_like(x_s)
    for q in range(L // J):
        bu_s[q] = jnp.dot(inputs(q), w_ref[...], preferred_element_type=F32)
        advance(q)

    for b in range(nb):
        def hop(c, carry, b=b):
            xr, xi = carry
            row = pl.ds(b * r + c, 1)
            st_s[row, :] = jnp.concatenate([xr, xi], axis=1)
            p = x_s[row, :]
            return (al_re * xr - al_im * xi + p[:, :ns], al_re * xi + al_im * xr + p[:, ns:])

        xr, xi = lax.fori_loop(0, r, hop, (carry_ref[b:b + 1, :ns], carry_ref[b:b + 1, ns:]))
        carry_ref[b:b + 1, :] = jnp.concatenate([xr, xi], axis=1)

    x_s[...] = st_s[...]
    for q in range(L // J):
        y = (jnp.dot(x_s[...].astype(BF16), c_ref[...], preferred_element_type=F32)
             + jnp.dot(inputs(q), dio_ref[...], preferred_element_type=F32))
        for j in range(J):
            sl = pl.ds(q * J + j, r, stride=L)
            lo = j * 2 * lanes
            y_s[0, :, sl, :] = y[:, lo:lo + lanes].reshape(nb, r, lanes)
            y_s[1, :, sl, :] = y[:, lo + lanes:lo + 2 * lanes].reshape(nb, r, lanes)
        if q + 1 < L // J:
            advance(q)

    u = jnp.concatenate([u0_ref[...], u1_ref[...]], axis=-1).reshape(nb * tb, 2 * lanes)
    y = jnp.concatenate([y_s[0], y_s[1]], axis=-1).reshape(nb * tb, 2 * lanes)
    y = jax.nn.gelu(y + d_ref[...] * u)
    z = jnp.dot(y.astype(BF16), wg_ref[...], preferred_element_type=F32)
    out = _rms(z[:, :GROUP_W] * jax.nn.sigmoid(z[:, GROUP_W:]), mg_ref[...])
    o_ref[...] = out.reshape(nb, tb, GROUP_W)


def _s5(proj3, mats, d_skip, w_glu_bf16, merge_g, tb):
    bsz, t, _ = proj3.shape
    w_in, c_out, d_io, a_grp, a_chunk = mats
    ns = S5_GROUPS * S5_STATE
    jw = S5_STEPS * GROUP_W
    lanes = GROUP_W // 2
    rows = bsz * (tb // S5_CHUNK)
    half = lambda j: pl.BlockSpec((bsz, tb, lanes), lambda i, j=j: (0, i, 2 * COL_S5 + j))
    full = lambda a, b: pl.BlockSpec((a, b), lambda i: (0, 0))
    return pl.pallas_call(
        _s5_kernel,
        grid=(t // tb,),
        in_specs=[half(0), half(1), full(jw, 2 * ns), full(2 * ns, jw), full(jw, jw), full(2, ns),
                  full(2, ns), full(1, GROUP_W), full(GROUP_W, 2 * GROUP_W), full(1, GROUP_W)],
        out_specs=pl.BlockSpec((bsz, tb, GROUP_W), lambda i: (0, i, 0)),
        out_shape=jax.ShapeDtypeStruct((bsz, t, GROUP_W), F32),
        scratch_shapes=[pltpu.VMEM((bsz, 2 * ns), F32), pltpu.VMEM((rows, 2 * ns), F32),
                        pltpu.VMEM((rows, 2 * ns), F32), pltpu.VMEM((2, bsz, tb, lanes), F32),
                        pltpu.VMEM((S5_CHUNK // S5_STEPS, rows, 2 * ns), F32)],
        compiler_params=_cparams("arbitrary"),
        name="s5",
    )(proj3, proj3, w_in, c_out, d_io, a_grp, a_chunk, d_skip.reshape(1, -1), w_glu_bf16,
      merge_g.reshape(1, -1))


def _rwkv_kernel(r_s, k_s, v_s, lw_s, a_s, b_s, g_ref, rk_ref, lnw_ref, lnb_ref, mg_ref,
                 o_ref, h_ref, y_s, m_s, n_s, p_s, z_s, *, reset):
    tb = r_s.shape[0]
    ch = RWKV_CHUNK
    nh = GROUP_W // HEAD_W
    if reset:
        h_ref[...] = jnp.zeros_like(h_ref)
        return

    bd_bf16 = _block_diag_mask(GROUP_W, HEAD_W).astype(BF16)

    n4 = nh * ch
    ri = lax.broadcasted_iota(jnp.int32, (n4, n4), 0)
    ci = lax.broadcasted_iota(jnp.int32, (n4, n4), 1)
    same_head = (ri // ch) == (ci // ch)
    strict = same_head & ((ri % ch) > (ci % ch))
    incl = same_head & ((ri % ch) >= (ci % ch))
    eye = (ri == ci).astype(F32)
    blk0 = RWKV_INV_BLOCK
    sizes = [blk0 << i for i in range((ch // blk0).bit_length())]
    same_blk = {b: (ri // b) == (ci // b) for b in sizes}
    hm = ((lax.broadcasted_iota(jnp.int32, (n4, GROUP_W), 0) // ch)
          == (lax.broadcasted_iota(jnp.int32, (n4, GROUP_W), 1) // HEAD_W))
    tri = (lax.broadcasted_iota(jnp.int32, (ch, ch), 0)
           >= lax.broadcasted_iota(jnp.int32, (ch, ch), 1)).astype(BF16)

    def stack(x):
        return jnp.where(hm, jnp.concatenate([x] * nh, axis=0), 0.0)

    def dot(x, y, dims=_NN):
        return lax.dot_general(x.astype(BF16), y.astype(BF16), (dims, ((), ())),
                               preferred_element_type=F32)

    chunks = range(tb // ch)
    pre = []
    for c in chunks:
        sl = slice(c * ch, (c + 1) * ch)
        lw = lw_s[sl, :]
        cl = _exact_lhs_mm(tri, lw)
        cl_end = cl[ch - 1:ch, :]
        e_in, e_ex = jnp.exp(cl), jnp.exp(cl - lw)
        e_neg, e_end = jnp.exp(-cl), jnp.exp(cl_end - cl)
        av, bv, kv, rv, vv = a_s[sl, :], b_s[sl, :], k_s[sl, :], r_s[sl, :], v_s[sl, :]
        pre.append(dict(
            at4=stack(av * e_ex), rt4=stack(rv * e_in), v4=stack(vv),
            bt4=jnp.concatenate([bv * e_neg] * nh, axis=0),
            kt4=jnp.concatenate([kv * e_neg] * nh, axis=0),
            bh4=stack(bv * e_end), kh4=stack(kv * e_end), g_end=jnp.exp(cl_end)))
        yield
    for d in pre:
        ar = jnp.concatenate([d["at4"], d["rt4"]], axis=0)
        sb = dot(ar, d["bt4"], _NT)
        sk = dot(ar, d["kt4"], _NT)
        d["l_ab"] = jnp.where(strict, sb[:n4], 0.0)
        d["l_ak"] = jnp.where(strict, sk[:n4], 0.0)
        d["l_rb"] = jnp.where(incl, sb[n4:], 0.0)
        d["l_rk"] = jnp.where(incl, sk[n4:], 0.0)
        yield
    for d in pre:
        nb8 = jnp.where(same_blk[blk0], d["l_ab"], 0.0)
        d["tinv"] = eye + nb8
        d["pw"] = dot(nb8, nb8)
        yield
    for d in pre:
        d["tinv"] = d["tinv"] + dot(d["tinv"], d["pw"])
        d["pw"] = dot(d["pw"], d["pw"])
        yield
    for d in pre:
        d["tinv"] = d["tinv"] + dot(d["tinv"], d["pw"])
        yield
    blk = blk0
    while blk < ch:
        for d in pre:
            d["pw"] = dot(jnp.where(same_blk[2 * blk] & ~same_blk[blk], d["l_ab"], 0.0), d["tinv"])
            yield
        for d in pre:
            d["tinv"] = d["tinv"] + dot(d["tinv"], d["pw"])
            yield
        blk *= 2
    for d in pre:
        d["lakv"] = dot(d["l_ak"], d["v4"])
        d["lrkv"] = dot(d["l_rk"], d["v4"])
        d["khv"] = dot(d["kh4"], d["v4"], _TN)
        yield
    for d in pre:
        d["x12"] = dot(d["tinv"], jnp.concatenate([d["at4"], d["lakv"]], axis=1))
        yield
    for c, d in zip(chunks, pre):
        mn = dot(d["bh4"], d["x12"], _TN)
        pz = dot(d["l_rb"], d["x12"])
        m_s[c] = jnp.where(ri == ci, d["g_end"], 0.0) + mn[:, :GROUP_W]
        n_s[c] = mn[:, GROUP_W:] + d["khv"]
        p_s[c] = d["rt4"] + pz[:, :GROUP_W]
        z_s[c] = pz[:, GROUP_W:] + d["lrkv"]
        yield

    h = h_ref[...]
    for c in range(tb // ch):
        y4 = dot(p_s[c], h) + z_s[c]
        y = y4[0:ch]
        for j in range(1, nh):
            y = y + y4[j * ch:(j + 1) * ch]
        y_s[c * ch:(c + 1) * ch, :] = y
        h = dot(m_s[c], h) + n_s[c]
        yield
    h_ref[...] = h

    y = y_s[...]
    r, k, v = r_s[...], k_s[...], v_s[...]
    mean = _head_sum(y, bd_bf16) * (1.0 / HEAD_W)
    d = y - mean
    var = _head_sum(d * d, bd_bf16) * (1.0 / HEAD_W)
    y = d * lax.rsqrt(var + RWKV_GN_EPS) * lnw_ref[...] + lnb_ref[...]
    y = y + _head_sum(r * k * rk_ref[...], bd_bf16) * v
    o_ref[...] = _rms(y * g_ref[...], mg_ref[...])


def _rwkv_part(proj3, abg3, lyr, p, merge_g, tb):
    bsz, t, _ = proj3.shape
    row = lambda x: x.reshape(1, -1)
    abg = lambda j: _col_spec(tb, j)
    mats = pltpu.VMEM((tb // RWKV_CHUNK, GROUP_W, GROUP_W), F32)
    return dict(
        body=_rwkv_kernel, stages=15 * (tb // RWKV_CHUNK) + 1,
        args=[proj3, proj3, proj3, proj3, abg3, abg3, abg3, row(p["rwkv_r_k"][lyr]),
              row(p["rwkv_ln_w"][lyr]), row(p["rwkv_ln_b"][lyr]), row(merge_g)],
        in_specs=[_col_spec(tb, COL_R), _col_spec(tb, COL_K), _col_spec(tb, COL_V),
                  _col_spec(tb, COL_LORA), abg(EXT_A), abg(EXT_B), abg(EXT_G),
                  _row_spec(), _row_spec(), _row_spec(), _row_spec()],
        out_specs=[_out_blk(tb)], out_shape=[jax.ShapeDtypeStruct((bsz, t, GROUP_W), F32)],
        scratch=[pltpu.VMEM((GROUP_W, GROUP_W), F32), pltpu.VMEM((tb, GROUP_W), F32),
                 mats, mats, mats, mats])


def _bf16_bits(x):
    u = lax.bitcast_convert_type(x, jnp.uint32)
    r = u + jnp.uint32(0x7FFF) + ((u >> 16) & jnp.uint32(1))
    return r & jnp.uint32(0xFFFF0000)


def _slab_copies(rows_ref, slab_hbm, row0, sem):
    tm = rows_ref.shape[0]
    return [(rows_ref.at[:, pl.ds(j * LANES, LANES)], slab_hbm.at[pl.ds(row0, tm), j, :], sem)
            for j in range(slab_hbm.shape[1])]


def _merge_router_kernel(oa_ref, ob_ref, oc_ref, od_ref, h_ref, wo_ref, nw_ref, wr_ref, br_ref,
                         h1_ref, hn_hbm, ids_ref, wts_ref, cnt_ref, pk_s, sem):
    i = pl.program_id(0)
    steps = pl.num_programs(0)
    tm = h_ref.shape[0]
    slot = i % 2

    def writes(s, step):
        return [pltpu.make_async_copy(v, h, m) for v, h, m in
                _slab_copies(pk_s.at[s], hn_hbm, step * tm, sem.at[s])]

    mix = jnp.concatenate([oa_ref[...], ob_ref[...], oc_ref[...], od_ref[...]], axis=1)
    h1 = h_ref[...] + jnp.dot(mix.astype(BF16), wo_ref[...], preferred_element_type=F32)
    h1_ref[...] = h1
    hn = _rms(h1, nw_ref[...])
    half = hn.shape[1] // 2

    @pl.when(i >= 2)
    def _():
        for cp in writes(slot, i - 2):
            cp.wait()

    pk_s[slot] = _bf16_bits(hn[:, :half]) | (_bf16_bits(hn[:, half:]) >> 16)
    for cp in writes(slot, i):
        cp.start()

    @pl.when(i == steps - 1)
    def _():
        for cp in writes(slot, i):
            cp.wait()

    @pl.when((i == steps - 1) & (i >= 1))
    def _():
        for cp in writes(1 - slot, i - 1):
            cp.wait()

    logits = _dg3(hn, wr_ref[...], _NN) + br_ref[...]
    lane = lax.broadcasted_iota(jnp.int32, logits.shape, 1)
    big = jnp.int32(ROUTER_LANES)
    is_c = (lane >= N_EXPERTS) & (lane < N_EXPERTS + MOE_GROUPS)
    cm = jnp.max(jnp.where(is_c, logits, NEG_BIG), axis=-1, keepdims=True)
    gsel = jnp.min(jnp.where(is_c & (logits == cm), lane, big), axis=-1, keepdims=True) - N_EXPERTS
    p_g = 1.0 / jnp.sum(jnp.where(is_c, jnp.exp(logits - cm), 0.0), axis=-1, keepdims=True)
    lo = gsel * MOE_PER_GROUP
    in_g = (lane >= lo) & (lane < lo + MOE_PER_GROUP)
    m1 = jnp.max(jnp.where(in_g, logits, NEG_BIG), axis=-1, keepdims=True)
    i1 = jnp.min(jnp.where(in_g & (logits == m1), lane, big), axis=-1, keepdims=True)
    in_g2 = in_g & (lane != i1)
    m2 = jnp.max(jnp.where(in_g2, logits, NEG_BIG), axis=-1, keepdims=True)
    i2 = jnp.min(jnp.where(in_g2 & (logits == m2), lane, big), axis=-1, keepdims=True)
    w1 = p_g / (1.0 + jnp.exp(m2 - m1))
    w2 = p_g - w1
    two = lax.broadcasted_iota(jnp.int32, ids_ref.shape, 1)
    ids_ref[...] = jnp.where(two == 0, i1, i2)
    wts_ref[...] = jnp.where(two == 0, w1, w2)

    @pl.when(pl.program_id(0) == 0)
    def _():
        cnt_ref[...] = jnp.zeros_like(cnt_ref)

    cnt_ref[...] += jnp.sum(((lane == i1) | (lane == i2)).astype(F32), axis=0, keepdims=True)


def _merge_router(outs, h2d, w_out_bf16, norm_w, w_router, b_router, tm):
    n, d = h2d.shape
    grp = pl.BlockSpec((tm, GROUP_W), lambda i: (i, 0))
    full = lambda a, b: pl.BlockSpec((a, b), lambda i: (0, 0))
    rowblk = lambda w: pl.BlockSpec((tm, w), lambda i: (i, 0))
    return pl.pallas_call(
        _merge_router_kernel,
        grid=(n // tm,),
        in_specs=[grp, grp, grp, grp, rowblk(d), full(d, d), full(1, d), full(d, ROUTER_LANES),
                  full(1, ROUTER_LANES)],
        out_specs=[rowblk(d), pl.BlockSpec(memory_space=pl.ANY),
                   rowblk(MOE_TOPK), rowblk(MOE_TOPK), full(1, ROUTER_LANES)],
        out_shape=[jax.ShapeDtypeStruct((n, d), F32),
                   jax.ShapeDtypeStruct((n, d // 2 // LANES, LANES), jnp.uint32),
                   jax.ShapeDtypeStruct((n, MOE_TOPK), jnp.int32),
                   jax.ShapeDtypeStruct((n, MOE_TOPK), F32),
                   jax.ShapeDtypeStruct((1, ROUTER_LANES), F32)],
        scratch_shapes=[pltpu.VMEM((2, tm, d // 2), jnp.uint32), pltpu.SemaphoreType.DMA((2,))],
        compiler_params=_cparams("arbitrary"),
        name="merge_router",
    )(*[o.reshape(n, GROUP_W) for o in outs], h2d, w_out_bf16, norm_w.reshape(1, d), w_router, b_router)


def _row_copy(src_hbm, src_row, dst_ref, dst_row, sem):
    return pltpu.make_async_copy(src_hbm.at[pl.ds(src_row, 1), :], dst_ref.at[pl.ds(dst_row, 1), :], sem)


def _position_kernel(ids_ref, base_ref, pos_ref, run_ref):
    tm = ids_ref.shape[0]

    @pl.when(pl.program_id(0) == 0)
    def _():
        run_ref[...] = jnp.zeros_like(run_ref)

    i1, i2 = ids_ref[:, 0:1], ids_ref[:, 1:2]
    lane = lax.broadcasted_iota(jnp.int32, (tm, ROUTER_LANES), 1)
    hit = ((lane == i1) | (lane == i2)).astype(BF16)
    earlier = (lax.broadcasted_iota(jnp.int32, (tm, tm), 0)
               > lax.broadcasted_iota(jnp.int32, (tm, tm), 1)).astype(BF16)
    rank = jnp.dot(earlier, hit, preferred_element_type=F32) + run_ref[...]
    where_to = rank + base_ref[...]
    p1 = jnp.sum(jnp.where(lane == i1, where_to, 0.0), axis=-1, keepdims=True)
    p2 = jnp.sum(jnp.where(lane == i2, where_to, 0.0), axis=-1, keepdims=True)
    two = lax.broadcasted_iota(jnp.int32, pos_ref.shape, 1)
    pos_ref[...] = jnp.where(two == 0, p1, p2).astype(jnp.int32)
    run_ref[...] += jnp.sum(hit.astype(F32), axis=0, keepdims=True)


def _positions(ids, group_start, tm):
    n = ids.shape[0]
    return pl.pallas_call(
        _position_kernel,
        grid=(n // tm,),
        in_specs=[pl.BlockSpec((tm, MOE_TOPK), lambda i: (i, 0)),
                  pl.BlockSpec((1, ROUTER_LANES), lambda i: (0, 0))],
        out_specs=pl.BlockSpec((tm, MOE_TOPK), lambda i: (i, 0)),
        out_shape=jax.ShapeDtypeStruct((n, MOE_TOPK), jnp.int32),
        scratch_shapes=[pltpu.VMEM((1, ROUTER_LANES), F32)],
        compiler_params=_cparams("arbitrary"),
        name="moe_positions",
    )(ids, group_start)


def _dispatch_kernel(p0_ref, p1_ref, x_ref, xs_in_hbm, xs_hbm, sem):
    del xs_in_hbm
    n = x_ref.shape[0]

    def copies(i):
        return (pltpu.make_async_copy(x_ref.at[i], xs_hbm.at[p0_ref[i]], sem),
                pltpu.make_async_copy(x_ref.at[i], xs_hbm.at[p1_ref[i]], sem))

    def issue(i, c):
        for prio, cp in enumerate(copies(i)):
            cp.start(priority=prio)
        return c

    def drain(i, c):
        for cp in copies(i):
            cp.wait()
        return c

    lax.fori_loop(0, n, issue, 0, unroll=8)
    lax.fori_loop(0, n, drain, 0, unroll=8)


def _dispatch(x, pos0, pos1, n_rows, chunk):
    n, slabs, lanes = x.shape
    smem = lambda: pl.BlockSpec((chunk,), lambda s: (s,), memory_space=pltpu.SMEM)
    return pl.pallas_call(
        _dispatch_kernel,
        grid=(n // chunk,),
        in_specs=[smem(), smem(), pl.BlockSpec((chunk, slabs, lanes), lambda s: (s, 0, 0)),
                  pl.BlockSpec(memory_space=pl.ANY)],
        out_specs=pl.BlockSpec(memory_space=pl.ANY),
        out_shape=jax.ShapeDtypeStruct((n_rows, slabs, lanes), x.dtype),
        input_output_aliases={3: 0},
        scratch_shapes=[pltpu.SemaphoreType.DMA(())],
        compiler_params=_cparams("arbitrary"),
        name="moe_dispatch",
    )(pos0, pos1, x, jnp.zeros((n_rows, slabs, lanes), x.dtype))


def _expert_kernel(te_ref, nt_ref, xs_hbm, wg_ref, wu_ref, wd_ref, o_ref, wg_s, wu_s, wd_s, x_s, sem):
    i = pl.program_id(0)
    tm = o_ref.shape[0]
    slot = i % 2

    def reads(s, tile):
        return [pltpu.make_async_copy(h, v, m) for v, h, m in
                _slab_copies(x_s.at[s], xs_hbm, tile * tm, sem.at[s])]

    @pl.when(i == 0)
    def _():
        for cp in reads(0, 0):
            cp.start()

    @pl.when(i + 1 < pl.num_programs(0))
    def _():
        for cp in reads(1 - slot, i + 1):
            cp.start()

    @pl.when((i == 0) | (te_ref[i] != te_ref[jnp.maximum(i - 1, 0)]))
    def _():
        wg_s[...] = wg_ref[...].astype(BF16)
        wu_s[...] = wu_ref[...].astype(BF16)
        wd_s[...] = wd_ref[...].astype(BF16)

    for cp in reads(slot, i):
        cp.wait()

    @pl.when(i < nt_ref[0])
    def _():
        p = x_s[slot]
        xa = lax.bitcast_convert_type(p & jnp.uint32(0xFFFF0000), F32).astype(BF16)
        xb = lax.bitcast_convert_type(p << 16, F32).astype(BF16)
        x = jnp.concatenate([xa, xb], axis=1)
        he = (_silu(jnp.dot(x, wg_s[...], preferred_element_type=F32))
              * jnp.dot(x, wu_s[...], preferred_element_type=F32))
        o_ref[...] = jnp.dot(he.astype(BF16), wd_s[...], preferred_element_type=F32)

    @pl.when(i >= nt_ref[0])
    def _():
        o_ref[...] = jnp.zeros_like(o_ref)


def _experts(xs, tile_expert, n_tiles_used, wg, wu, wd, layer, tm):
    p, slabs, lanes = xs.shape
    d = 2 * slabs * lanes
    pick = lambda i, te, nt: (layer, te[i], 0, 0)
    grid_spec = pltpu.PrefetchScalarGridSpec(
        num_scalar_prefetch=2,
        grid=(p // tm,),
        in_specs=[pl.BlockSpec(memory_space=pl.ANY),
                  pl.BlockSpec((None, None, d, D_EXPERT), pick),
                  pl.BlockSpec((None, None, d, D_EXPERT), pick),
                  pl.BlockSpec((None, None, D_EXPERT, d), pick)],
        out_specs=pl.BlockSpec((tm, d), lambda i, te, nt: (i, 0)),
        scratch_shapes=[pltpu.VMEM((d, D_EXPERT), BF16), pltpu.VMEM((d, D_EXPERT), BF16),
                        pltpu.VMEM((D_EXPERT, d), BF16), pltpu.VMEM((2, tm, slabs * lanes), xs.dtype),
                        pltpu.SemaphoreType.DMA((2,))],
    )
    return pl.pallas_call(
        _expert_kernel,
        grid_spec=grid_spec,
        out_shape=jax.ShapeDtypeStruct((p, d), F32),
        compiler_params=_cparams("arbitrary"),
        name="moe_experts",
    )(tile_expert, n_tiles_used, xs, wg, wu, wd)


def _combine_kernel(i0_ref, i1_ref, h1_ref, w_ref, fw_ref, ys_hbm, o_ref, buf0, buf1, sem, *, final_norm):
    n = o_ref.shape[0]

    def issue(i, c):
        _row_copy(ys_hbm, i0_ref[i], buf0, i, sem).start(priority=0)
        _row_copy(ys_hbm, i1_ref[i], buf1, i, sem).start(priority=1)
        return c

    def drain(i, c):
        _row_copy(ys_hbm, 0, buf0, i, sem).wait()
        _row_copy(ys_hbm, 0, buf1, i, sem).wait()
        return c

    lax.fori_loop(0, n, issue, 0, unroll=8)
    lax.fori_loop(0, n, drain, 0, unroll=8)
    out = h1_ref[...] + w_ref[:, 0:1] * buf0[...] + w_ref[:, 1:2] * buf1[...]
    if final_norm:
        out = _rms(out, fw_ref[...])
    o_ref[...] = out


def _combine(ys, pos0, pos1, wts, h1, final_w, final_norm, chunk):
    n, d = h1.shape
    smem = lambda: pl.BlockSpec((chunk,), lambda s: (s,), memory_space=pltpu.SMEM)
    return pl.pallas_call(
        functools.partial(_combine_kernel, final_norm=final_norm),
        grid=(n // chunk,),
        in_specs=[smem(), smem(), pl.BlockSpec((chunk, d), lambda s: (s, 0)),
                  pl.BlockSpec((chunk, MOE_TOPK), lambda s: (s, 0)),
                  pl.BlockSpec((1, d), lambda s: (0, 0)), pl.BlockSpec(memory_space=pl.ANY)],
        out_specs=pl.BlockSpec((chunk, d), lambda s: (s, 0)),
        out_shape=jax.ShapeDtypeStruct((n, d), F32),
        scratch_shapes=[pltpu.VMEM((chunk, d), F32), pltpu.VMEM((chunk, d), F32),
                        pltpu.SemaphoreType.DMA(())],
        compiler_params=_cparams("arbitrary"),
        name="moe_combine",
    )(pos0, pos1, h1, wts, final_w.reshape(1, d), ys)


def _moe(hn_packed, ids, wts, counts, h1, wg, wu, wd, layer, final_w, final_norm, tm):
    n = h1.shape[0]
    n_tiles = (n * MOE_TOPK) // tm + N_EXPERTS
    cnt = counts[0, :N_EXPERTS].astype(jnp.int32)
    padded = ((cnt + tm - 1) // tm) * tm
    ends = jnp.cumsum(padded)
    tile_start = jnp.arange(n_tiles, dtype=jnp.int32) * tm
    tile_expert = jnp.minimum(jnp.sum(ends[None, :] <= tile_start[:, None], axis=1),
                              N_EXPERTS - 1).astype(jnp.int32)
    n_tiles_used = (ends[-1] // tm).astype(jnp.int32).reshape(1)
    group_start = jnp.zeros((1, ROUTER_LANES), F32).at[0, :N_EXPERTS].set((ends - padded).astype(F32))
    pos = _positions(ids, group_start, _pick_div(n, 512))
    pos0, pos1 = pos[:, 0], pos[:, 1]
    xs = _dispatch(hn_packed, pos0, pos1, n_tiles * tm, _pick_div(n, 1024))
    ys = _experts(xs, tile_expert, n_tiles_used, wg, wu, wd, layer, tm)
    return _combine(ys, pos0, pos1, wts, h1, final_w, final_norm, _pick_div(n, 1024))


def _pick_div(n, pref):
    while n % pref:
        pref //= 2
    return pref


def _pick(n, pref):
    return pref if n % pref == 0 else n


def kernel(x, norm1_w, w_in, hgrn_lb_logits, hgrn_norm_w, s5_lambda_re, s5_lambda_im, s5_log_dt, s5_b_re, s5_b_im, s5_c_re, s5_c_im, s5_d, s5_w_glu, rwkv_mu, rwkv_w0, rwkv_w2, rwkv_a0, rwkv_a2, rwkv_g2, rwkv_k_k, rwkv_k_a, rwkv_r_k, rwkv_v0, rwkv_v1, rwkv_v2, rwkv_ln_w, rwkv_ln_b, lru_conv_w, lru_conv_b, lru_wa, lru_ba, lru_wx, lru_bx, lru_lambda, merge_gain, w_out, norm2_w, moe_coarse_w, moe_coarse_b, moe_fine_w, moe_fine_b, moe_w_gate, moe_w_up, moe_w_down, final_norm_w):
    bsz, t, d = x.shape
    n = bsz * t
    depth = w_in.shape[0]
    rw = dict(rwkv_mu=rwkv_mu, rwkv_w0=rwkv_w0, rwkv_w2=rwkv_w2, rwkv_a0=rwkv_a0, rwkv_a2=rwkv_a2,
              rwkv_g2=rwkv_g2, rwkv_k_k=rwkv_k_k, rwkv_k_a=rwkv_k_a, rwkv_r_k=rwkv_r_k,
              rwkv_v0=rwkv_v0, rwkv_v1=rwkv_v1, rwkv_v2=rwkv_v2, rwkv_ln_w=rwkv_ln_w,
              rwkv_ln_b=rwkv_ln_b)
    lb_all = jnp.cumsum(jax.nn.softmax(hgrn_lb_logits.astype(F32), axis=0), axis=0)
    lb_all = lb_all - lb_all[:1]

    tm_proj = _pick(n, 512)
    tm_in = _pick_div(t, 512)
    tm_moe = 512
    tb_mix = _pick(t, 256)
    tb_s5 = _pick(t, 512)

    h = x.reshape(n, d)
    v_first = None
    for l in range(depth):
        mg = merge_gain[l].reshape(4, GROUP_W)
        proj, abg, v_first = _in_proj(h, norm1_w[l], w_in[l].astype(BF16), v_first, l, rw, t, tm_in)
        proj3 = proj.reshape(bsz, t, -1)
        abg3 = abg.reshape(bsz, t, -1)
        mats = _s5_matrices(s5_lambda_re[l], s5_lambda_im[l], s5_log_dt[l], s5_b_re[l], s5_b_im[l],
                            s5_c_re[l], s5_c_im[l])
        o_b = _s5(proj3, mats, s5_d[l], s5_w_glu[l].astype(BF16), mg[1], tb_s5)
        parts = [_rwkv_part(proj3, abg3, l, rw, mg[2], tb_mix),
                 _hgrn_part(proj3, lb_all[l], hgrn_norm_w[l], mg[0], tb_mix),
                 _lru_part(proj3, lru_conv_w[l], lru_conv_b[l], _block_diag_weight(lru_wa[l]),
                           lru_ba[l], _block_diag_weight(lru_wx[l]), lru_bx[l], lru_lambda[l],
                           mg[3], tb_mix)]
        outs = _mixers(parts, bsz, t // tb_mix, "mixers")
        o_c, o_a, o_d = outs
        w_router = jnp.concatenate(
            [moe_fine_w[l].transpose(1, 0, 2).reshape(d, N_EXPERTS), moe_coarse_w[l],
             jnp.zeros((d, ROUTER_LANES - N_EXPERTS - MOE_GROUPS), F32)], axis=1)
        b_router = jnp.concatenate(
            [moe_fine_b[l].reshape(N_EXPERTS), moe_coarse_b[l],
             jnp.zeros((ROUTER_LANES - N_EXPERTS - MOE_GROUPS,), F32)]).reshape(1, ROUTER_LANES)
        h1, hn, ids, wts, counts = _merge_router((o_a, o_b, o_c, o_d), h, w_out[l].astype(BF16),
                                                 norm2_w[l], w_router, b_router, tm_proj)
        h = _moe(hn, ids, wts, counts, h1, moe_w_gate, moe_w_up, moe_w_down, l,
                 final_norm_w, l == depth - 1, tm_moe)
    return h.reshape(bsz, t, d)
```

```python
import functools

import jax
import jax.numpy as jnp
from jax import lax
from jax.experimental import pallas as pl
from jax.experimental.pallas import tpu as pltpu

F32 = jnp.float32
BF16 = jnp.bfloat16

GROUP_W = 256
RMS_EPS = 1e-6
HEAD_W = 64
HGRN_CHUNK = 32
S5_CH = 16
S5_GROUPS = GROUP_W // S5_CH
S5_STATE = 64
S5_CHUNK = 16
S5_STEPS = 4
RWKV_CHUNK = 64
RWKV_INV_BLOCK = 8
RWKV_GN_EPS = 64e-5
RWKV_W_LORA = 64
RWKV_V_LORA = 32
LRU_CONV = 4
LRU_C = 8.0
MOE_GROUPS = 4
MOE_PER_GROUP = 8
N_EXPERTS = MOE_GROUPS * MOE_PER_GROUP
MOE_TOPK = 2
D_EXPERT = 512
LANES = 128
SUBLANES = 8
ROUTER_LANES = LANES
NEG_BIG = -1e30
V7X_VMEM_BYTES = 64 * 1024 * 1024
VMEM_LIMIT = V7X_VMEM_BYTES - 8 * 1024 * 1024

COL_HQ, COL_HF, COL_HI, COL_HG, COL_S5, COL_R, COL_K, COL_V, COL_LORA, COL_LG, COL_LX = range(11)
EXT_A, EXT_B, EXT_G = range(3)


def _cparams(*sem):
    return pltpu.CompilerParams(dimension_semantics=sem, vmem_limit_bytes=VMEM_LIMIT)


def _mm(a, b):
    return jnp.dot(a.astype(BF16), b.astype(BF16), preferred_element_type=F32)


def _split2(x):
    hi = x.astype(BF16)
    lo = (x - hi.astype(F32)).astype(BF16)
    return hi, lo


def _dg3(a, b, dims):
    ah, al = _split2(a)
    bh, bl = _split2(b)
    d = lambda x, y: lax.dot_general(x, y, (dims, ((), ())), preferred_element_type=F32)
    return d(ah, bh) + d(ah, bl) + d(al, bh)


_NN = ((1,), (0,))
_NT = ((1,), (1,))
_TN = ((0,), (0,))


def _exact_lhs_mm(m_bf16, x):
    h1 = x.astype(BF16)
    r1 = x - h1.astype(F32)
    h2 = r1.astype(BF16)
    h3 = (r1 - h2.astype(F32)).astype(BF16)
    d = lambda y: jnp.dot(m_bf16, y, preferred_element_type=F32)
    return d(h1) + d(h2) + d(h3)


def _head_sum(x, bd_bf16):
    return jnp.dot(x.astype(BF16), bd_bf16, preferred_element_type=F32)


def _block_diag_mask(n, blk):
    r = lax.broadcasted_iota(jnp.int32, (n, n), 0) // blk
    c = lax.broadcasted_iota(jnp.int32, (n, n), 1) // blk
    return r == c


def _rms(x, w):
    return x * lax.rsqrt(jnp.mean(x * x, axis=-1, keepdims=True) + RMS_EPS) * w


def _silu(x):
    return x * jax.nn.sigmoid(x)


def _softplus(x):
    return jnp.maximum(x, 0.0) + jnp.log(1.0 + jnp.exp(-jnp.abs(x)))


def _in_proj_kernel(*refs, has_vmix, tiles_per_seq):
    if has_vmix:
        (x_ref, nw_ref, w_ref, mu_ref, w0_ref, w2_ref, a0_ref, a2_ref, g2_ref, kk_ref, ka_ref,
         vf_ref, v0_ref, v1_ref, v2_ref, o_ref, o2_ref, prev_s) = refs
    else:
        (x_ref, nw_ref, w_ref, mu_ref, w0_ref, w2_ref, a0_ref, a2_ref, g2_ref, kk_ref, ka_ref,
         o_ref, o2_ref, vf_out_ref, prev_s) = refs
    tm = x_ref.shape[0]
    gw = GROUP_W
    c0, c1 = COL_R * gw, (COL_LORA + 1) * gw
    @pl.when((pl.program_id(0) % tiles_per_seq) == 0)
    def _():
        prev_s[...] = jnp.zeros_like(prev_s)

    y = _rms(x_ref[...], nw_ref[...]).astype(BF16)
    raw = jnp.dot(y, w_ref[:, c0:c1], preferred_element_type=F32)

    row0 = lax.broadcasted_iota(jnp.int32, (tm, 1), 0) == 0
    sh = jnp.where(row0, prev_s[...], pltpu.roll(raw, 1, axis=0))
    prev_s[...] = raw[tm - 1:tm, :]
    pf = raw + mu_ref[...] * (sh - raw)
    r, k, v, lo = (pf[:, j * GROUP_W:(j + 1) * GROUP_W] for j in range(4))

    o_ref[:, :c0] = jnp.dot(y, w_ref[:, :c0], preferred_element_type=F32)
    bd_bf16 = _block_diag_mask(GROUP_W, HEAD_W).astype(BF16)
    lo_wa, lo_g = lo[:, :LANES], lo[:, LANES:]
    w_raw = -_softplus(-(w0_ref[...] + _mm(jnp.tanh(lo_wa), w2_ref[...]))) - 0.5
    alr = jax.nn.sigmoid(a0_ref[...] + _mm(lo_wa, a2_ref[...]))
    o_ref[:, c1:] = jnp.dot(y, w_ref[:, c1:], preferred_element_type=F32)
    kkr = k * kk_ref[...]
    kk = kkr / jnp.maximum(jnp.sqrt(_head_sum(kkr * kkr, bd_bf16)), 1e-12)
    if has_vmix:
        gate = jax.nn.sigmoid(v0_ref[...] + _mm(_mm(v, v1_ref[...]), v2_ref[...]))
        v = v + (vf_ref[...] - v) * gate
    else:
        vf_out_ref[...] = v
    o_ref[:, c0:c0 + GROUP_W] = r
    o_ref[:, c0 + GROUP_W:c0 + 2 * GROUP_W] = k * (1.0 + (alr - 1.0) * ka_ref[...])
    o_ref[:, c0 + 2 * GROUP_W:c0 + 3 * GROUP_W] = v
    o_ref[:, c0 + 3 * GROUP_W:c1] = -jnp.exp(w_raw)
    o2_ref[:, EXT_A * gw:(EXT_A + 1) * gw] = -kk
    o2_ref[:, EXT_B * gw:(EXT_B + 1) * gw] = kk * alr
    o2_ref[:, EXT_G * gw:(EXT_G + 1) * gw] = _mm(jax.nn.sigmoid(lo_g), g2_ref[...])


def _pad_rows(w, start, total=LANES):
    out = jnp.zeros((total, w.shape[1]), F32).at[start:start + w.shape[0]].set(w)
    return out.astype(BF16)


def _in_proj(h2d, norm_w, w_bf16, v_first, lyr, p, t, tm):
    n, d = h2d.shape
    d_in = w_bf16.shape[1]
    has_vmix = v_first is not None
    row = lambda x: x.reshape(1, -1)
    full = lambda a, b: pl.BlockSpec((a, b), lambda i: (0, 0))
    rowblk = lambda w: pl.BlockSpec((tm, w), lambda i: (i, 0))
    args = [h2d, row(norm_w), w_bf16, row(p["rwkv_mu"][lyr]), row(p["rwkv_w0"][lyr]),
            _pad_rows(p["rwkv_w2"][lyr], 0), row(p["rwkv_a0"][lyr]),
            _pad_rows(p["rwkv_a2"][lyr], RWKV_W_LORA), p["rwkv_g2"][lyr].astype(BF16),
            row(p["rwkv_k_k"][lyr]), row(p["rwkv_k_a"][lyr])]
    specs = [rowblk(d), full(1, d), full(d, d_in), full(1, 4 * GROUP_W), full(1, GROUP_W),
             full(LANES, GROUP_W), full(1, GROUP_W), full(LANES, GROUP_W), full(LANES, GROUP_W),
             full(1, GROUP_W), full(1, GROUP_W)]
    out_specs = [rowblk(d_in), rowblk(3 * GROUP_W)]
    out_shape = [jax.ShapeDtypeStruct((n, d_in), F32), jax.ShapeDtypeStruct((n, 3 * GROUP_W), F32)]
    if has_vmix:
        v1 = jnp.zeros((GROUP_W, LANES), F32).at[:, :RWKV_V_LORA].set(p["rwkv_v1"][lyr - 1]).astype(BF16)
        v2 = jnp.zeros((LANES, GROUP_W), F32).at[:RWKV_V_LORA].set(p["rwkv_v2"][lyr - 1]).astype(BF16)
        args += [v_first, row(p["rwkv_v0"][lyr - 1]), v1, v2]
        specs += [rowblk(GROUP_W), full(1, GROUP_W), full(GROUP_W, LANES), full(LANES, GROUP_W)]
    else:
        out_specs.append(rowblk(GROUP_W))
        out_shape.append(jax.ShapeDtypeStruct((n, GROUP_W), F32))
    res = pl.pallas_call(
        functools.partial(_in_proj_kernel, has_vmix=has_vmix, tiles_per_seq=t // tm),
        grid=(n // tm,),
        in_specs=specs,
        out_specs=out_specs,
        out_shape=out_shape,
        scratch_shapes=[pltpu.VMEM((1, 4 * GROUP_W), F32)],
        compiler_params=_cparams("arbitrary"),
        name="in_proj",
    )(*args)
    return res[0], res[1], (v_first if has_vmix else res[2])


def _col_spec(tb, col):
    return pl.BlockSpec((None, tb, GROUP_W), lambda b, t: (b, t, col))


def _row_spec(width=GROUP_W):
    return pl.BlockSpec((1, width), lambda b, t: (0, 0))


def _full_spec(shape):
    return pl.BlockSpec(shape, lambda b, t: (0,) * len(shape))


def _hgrn_kernel(q_ref, f_ref, i_ref, g_ref, lb_ref, nw_ref, mg_ref, o_ref,
                 st_ref, q_s, k_s, v_s, lf_s, o_s, *, reset):
    ch = HGRN_CHUNK
    tb = q_ref.shape[0]
    if reset:
        st_ref[...] = jnp.zeros_like(st_ref)
        return

    lb = lb_ref[...]
    fx = f_ref[...]
    x1 = jnp.log(lb)
    x2 = jnp.log(1.0 - lb) - _softplus(-fx)
    m = jnp.maximum(x1, x2)
    lf_s[...] = m + jnp.log(jnp.exp(x1 - m) + jnp.exp(x2 - m))
    q_s[...] = _silu(q_ref[...])
    k_s[...] = (1.0 - lb) * jax.nn.sigmoid(-fx)
    v_s[...] = _silu(i_ref[...])
    yield

    bd = _block_diag_mask(GROUP_W, HEAD_W)
    bd_bf16 = bd.astype(BF16)
    tri = (lax.broadcasted_iota(jnp.int32, (ch, ch), 0)
           >= lax.broadcasted_iota(jnp.int32, (ch, ch), 1))
    tri_bf16 = tri.astype(BF16)
    tri3 = (lax.broadcasted_iota(jnp.int32, (ch, ch, 1), 0)
            <= lax.broadcasted_iota(jnp.int32, (ch, ch, 1), 1))

    st = st_ref[...]
    for c in range(tb // ch):
        sl = slice(c * ch, (c + 1) * ch)
        qc, kc, vc = q_s[sl, :], k_s[sl, :], v_s[sl, :]
        b = _exact_lhs_mm(tri_bf16, lf_s[sl, :])
        rel = b[None, :, :] - b[:, None, :]
        dec = jnp.exp(jnp.where(tri3, rel, NEG_BIG))
        p = (qc[None, :, :] * kc[:, None, :]) * dec
        sc = jnp.dot(p.reshape(ch * ch, GROUP_W).astype(BF16), bd_bf16,
                     preferred_element_type=F32).reshape(ch, ch, GROUP_W)
        o_intra = jnp.sum(sc * vc[:, None, :], axis=0)
        o_inter = lax.dot_general((qc * jnp.exp(b)).astype(BF16), st.astype(BF16),
                                  (_NT, ((), ())), preferred_element_type=F32)
        b_end = b[ch - 1:ch, :]
        kh = kc * jnp.exp(b_end - b)
        upd = lax.dot_general(vc.astype(BF16), kh.astype(BF16), (_TN, ((), ())),
                              preferred_element_type=F32)
        st = st * jnp.exp(b_end) + jnp.where(bd, upd, 0.0)
        o_s[sl, :] = o_intra + o_inter
        yield
    st_ref[...] = st

    o = o_s[...]
    ms = _head_sum(o * o, bd_bf16) * (1.0 / HEAD_W)
    o = o * lax.rsqrt(ms + RMS_EPS) * nw_ref[...] * _silu(g_ref[...])
    o_ref[...] = _rms(o, mg_ref[...])


def _out_blk(tb):
    return pl.BlockSpec((None, tb, GROUP_W), lambda b, i: (b, i, 0))


def _hgrn_part(proj3, lb, norm_w, merge_g, tb):
    bsz, t, _ = proj3.shape
    blk = pltpu.VMEM((tb, GROUP_W), F32)
    return dict(
        body=_hgrn_kernel, stages=tb // HGRN_CHUNK + 2,
        args=[proj3, proj3, proj3, proj3, lb.reshape(1, -1), norm_w.reshape(1, -1), merge_g.reshape(1, -1)],
        in_specs=[_col_spec(tb, COL_HQ), _col_spec(tb, COL_HF), _col_spec(tb, COL_HI),
                  _col_spec(tb, COL_HG), _row_spec(), _row_spec(), _row_spec()],
        out_specs=[_out_blk(tb)],
        out_shape=[jax.ShapeDtypeStruct((bsz, t, GROUP_W), F32)],
        scratch=[pltpu.VMEM((GROUP_W, GROUP_W), F32), blk, blk, blk, blk, blk])


def _mixer_kernel(*refs, parts):
    groups, i = [], 0
    for kind in range(3):
        for body, counts in parts:
            groups.append(refs[i:i + counts[kind]])
            i += counts[kind]
    k = len(parts)
    per_part = [groups[j] + groups[k + j] + groups[2 * k + j] for j in range(k)]

    @pl.when(pl.program_id(1) == 0)
    def _():
        for (body, _), r in zip(parts, per_part):
            for _step in body(*r, reset=True):
                pass

    runs = [[body(*r, reset=False), 0, counts[3]] for (body, counts), r in zip(parts, per_part)]
    while runs:
        run = min(runs, key=lambda x: x[1] / x[2])
        try:
            next(run[0])
            run[1] += 1
        except StopIteration:
            runs.remove(run)


def _mixers(parts, bsz, n_tblocks, name):
    cat = lambda key: [x for p in parts for x in p[key]]
    light = tuple((p["body"], (len(p["args"]), len(p["out_specs"]), len(p["scratch"]), p["stages"]))
                  for p in parts)
    return pl.pallas_call(
        functools.partial(_mixer_kernel, parts=light),
        grid=(bsz, n_tblocks),
        in_specs=cat("in_specs"),
        out_specs=cat("out_specs"),
        out_shape=cat("out_shape"),
        scratch_shapes=cat("scratch"),
        compiler_params=_cparams("parallel", "arbitrary"),
        name=name,
    )(*cat("args"))


def _lru_kernel(xg_ref, xr_ref, cw_ref, cb_ref, wa_ref, ba_ref, wx_ref, bx_ref, lam_ref, mg_ref,
                o_ref, buf_ref, h_ref, *, reset):
    tb = xr_ref.shape[0]
    pad = SUBLANES
    if reset:
        buf_ref[0:pad, :] = jnp.zeros((pad, GROUP_W), F32)
        h_ref[...] = jnp.zeros_like(h_ref)
        return

    xr = xr_ref[...]
    buf_ref[pad:pad + tb, :] = xr
    xc = cb_ref[...] + jnp.zeros_like(xr)
    for j in range(LRU_CONV):
        xc = xc + cw_ref[j:j + 1, :] * buf_ref[pl.ds(pad - (LRU_CONV - 1) + j, tb), :]
    buf_ref[0:pad, :] = xr[tb - pad:tb, :]

    r = jax.nn.sigmoid(jnp.dot(xc.astype(BF16), wa_ref[...], preferred_element_type=F32) + ba_ref[...])
    gi = jax.nn.sigmoid(jnp.dot(xc.astype(BF16), wx_ref[...], preferred_element_type=F32) + bx_ref[...])
    log_a = -LRU_C * r * _softplus(-lam_ref[...])
    a = jnp.exp(log_a)
    x = jnp.sqrt(1.0 - jnp.exp(2.0 * log_a)) * (gi * xc)
    yield

    rows = lax.broadcasted_iota(jnp.int32, (tb, 1), 0)
    k = 1
    while k < tb:
        keep = rows >= k
        x = x + jnp.where(keep, a * pltpu.roll(x, k, axis=0), 0.0)
        a = jnp.where(keep, a * pltpu.roll(a, k, axis=0), a)
        k *= 2
        yield
    h = x + a * h_ref[...]
    h_ref[...] = h[tb - 1:tb, :]
    o_ref[...] = _rms(jax.nn.gelu(xg_ref[...]) * h, mg_ref[...])


def _lru_part(proj3, conv_w, conv_b, wa_bd, ba, wx_bd, bx, lam, merge_g, tb):
    bsz, t, _ = proj3.shape
    return dict(
        body=_lru_kernel, stages=tb.bit_length() + 1,
        args=[proj3, proj3, conv_w, conv_b.reshape(1, -1), wa_bd, ba.reshape(1, -1), wx_bd,
              bx.reshape(1, -1), lam.reshape(1, -1), merge_g.reshape(1, -1)],
        in_specs=[_col_spec(tb, COL_LG), _col_spec(tb, COL_LX), _full_spec((LRU_CONV, GROUP_W)),
                  _row_spec(), _full_spec((GROUP_W, GROUP_W)), _row_spec(),
                  _full_spec((GROUP_W, GROUP_W)), _row_spec(), _row_spec(), _row_spec()],
        out_specs=[_out_blk(tb)],
        out_shape=[jax.ShapeDtypeStruct((bsz, t, GROUP_W), F32)],
        scratch=[pltpu.VMEM((tb + SUBLANES, GROUP_W), F32), pltpu.VMEM((1, GROUP_W), F32)])


def _block_diag_weight(w):
    h, n, _ = w.shape
    eye = jnp.eye(h, dtype=w.dtype)
    return jnp.einsum('hij,hg->higj', w, eye).reshape(h * n, h * n).astype(BF16)


def _s5_matrices(lam_re, lam_im, log_dt, b_re, b_im, c_re, c_im):
    L, G, P, C = S5_CHUNK, S5_GROUPS, S5_STATE, S5_CH
    lr, li = lam_re.astype(F32), lam_im.astype(F32)
    dt = jnp.exp(log_dt.astype(F32))[:, None]
    mag = jnp.exp(lr * dt)
    a_re, a_im = mag * jnp.cos(li * dt), mag * jnp.sin(li * dt)
    den = lr * lr + li * li
    kap_re = ((a_re - 1.0) * lr + a_im * li) / den
    kap_im = (a_im * lr - (a_re - 1.0) * li) / den
    br, bi = b_re.astype(F32), b_im.astype(F32)
    bb_re = kap_re[..., None] * br - kap_im[..., None] * bi
    bb_im = kap_re[..., None] * bi + kap_im[..., None] * br
    cr, ci = c_re.astype(F32), c_im.astype(F32)
    eye = jnp.eye(G, dtype=F32)
    J = S5_STEPS
    kk = jnp.arange(J + 1, dtype=F32)[:, None, None]
    pmag = jnp.exp(kk * (lr * dt)[None])
    pw_re, pw_im = pmag * jnp.cos(kk * (li * dt)[None]), pmag * jnp.sin(kk * (li * dt)[None])
    ab_re = pw_re[:J, :, :, None] * bb_re[None] - pw_im[:J, :, :, None] * bb_im[None]
    ab_im = pw_re[:J, :, :, None] * bb_im[None] + pw_im[:J, :, :, None] * bb_re[None]
    rev = lambda x: jnp.stack([x[J - 1 - j] for j in range(J)])
    w_in = jnp.concatenate(
        [jnp.einsum('jgpc,gh->jgchp', rev(ab_re), eye).reshape(J * G * C, G * P),
         jnp.einsum('jgpc,gh->jgchp', rev(ab_im), eye).reshape(J * G * C, G * P)], axis=1)
    ca_re = jnp.einsum('gcp,jgp->jgcp', cr, pw_re[1:]) - jnp.einsum('gcp,jgp->jgcp', ci, pw_im[1:])
    ca_im = jnp.einsum('gcp,jgp->jgcp', cr, pw_im[1:]) + jnp.einsum('gcp,jgp->jgcp', ci, pw_re[1:])
    c_out = jnp.concatenate(
        [jnp.einsum('jgcp,gh->gpjhc', ca_re, eye).reshape(G * P, J * G * C),
         -jnp.einsum('jgcp,gh->gpjhc', ca_im, eye).reshape(G * P, J * G * C)], axis=0)
    taps = jnp.einsum('gop,kgpc->kgco', cr, ab_re) - jnp.einsum('gop,kgpc->kgco', ci, ab_im)
    none = jnp.zeros_like(taps[0])
    d_io = jnp.stack([jnp.stack([taps[j - i] if j >= i else none for j in range(J)])
                      for i in range(J)])
    d_io = jnp.einsum('ijgco,gh->igcjho', d_io, eye).reshape(J * G * C, J * G * C)
    a_grp = jnp.stack([pw_re[J].reshape(G * P), pw_im[J].reshape(G * P)])
    mag_l = jnp.exp(L * lr * dt)
    a_chunk = jnp.stack([(mag_l * jnp.cos(L * li * dt)).reshape(G * P),
                         (mag_l * jnp.sin(L * li * dt)).reshape(G * P)])
    return w_in.astype(BF16), c_out.astype(BF16), d_io.astype(BF16), a_grp, a_chunk


def _s5_kernel(u0_ref, u1_ref, w_ref, c_ref, dio_ref, a_ref, al_ref, d_ref, wg_ref, mg_ref, o_ref,
               carry_ref, x_s, st_s, y_s, bu_s):
    nb, tb, lanes = u0_ref.shape
    L, J = S5_CHUNK, S5_STEPS
    r = tb // L
    rows = nb * r
    ns = S5_GROUPS * S5_STATE

    @pl.when(pl.program_id(0) == 0)
    def _():
        carry_ref[...] = jnp.zeros_like(carry_ref)

    a_re, a_im = a_ref[0:1, :], a_ref[1:2, :]
    al_re, al_im = al_ref[0:1, :], al_ref[1:2, :]

    def inputs(q):
        parts = []
        for j in range(J):
            sl = pl.ds(q * J + j, r, stride=L)
            parts += [u0_ref[:, sl, :], u1_ref[:, sl, :]]
        return jnp.concatenate(parts, axis=-1).reshape(rows, J * 2 * lanes).astype(BF16)

    def advance(q):
        bu = bu_s[q]
        xr, xi = x_s[:, :ns], x_s[:, ns:]
        x_s[:, :ns] = a_re * xr - a_im * xi + bu[:, :ns]
        x_s[:, ns:] = a_re * xi + a_im * xr + bu[:, ns:]

    x_s[...] = jnp.zeros_like(x_s)
    for q in range(L // J):
        bu_s[q] = jnp.dot(inputs(q), w_ref[...], preferred_element_type=F32)
        advance(q)

    for b in range(nb):
        def hop(c, carry, b=b):
            xr, xi = carry
            row = pl.ds(b * r + c, 1)
            st_s[row, :] = jnp.concatenate([xr, xi], axis=1)
            p = x_s[row, :]
            return (al_re * xr - al_im * xi + p[:, :ns], al_re * xi + al_im * xr + p[:, ns:])

        xr, xi = lax.fori_loop(0, r, hop, (carry_ref[b:b + 1, :ns], carry_ref[b:b + 1, ns:]))
        carry_ref[b:b + 1, :] = jnp.concatenate([xr, xi], axis=1)

    x_s[...] = st_s[...]
    for q in range(L // J):
        y = (jnp.dot(x_s[...].astype(BF16), c_ref[...], preferred_element_type=F32)
             + jnp.dot(inputs(q), dio_ref[...], preferred_element_type=F32))
        for j in range(J):
            sl = pl.ds(q * J + j, r, stride=L)
            lo = j * 2 * lanes
            y_s[0, :, sl, :] = y[:, lo:lo + lanes].reshape(nb, r, lanes)
            y_s[1, :, sl, :] = y[:, lo + lanes:lo + 2 * lanes].reshape(nb, r, lanes)
        if q + 1 < L // J:
            advance(q)

    u = jnp.concatenate([u0_ref[...], u1_ref[...]], axis=-1).reshape(nb * tb, 2 * lanes)
    y = jnp.concatenate([y_s[0], y_s[1]], axis=-1).reshape(nb * tb, 2 * lanes)
    y = jax.nn.gelu(y + d_ref[...] * u)
    z = jnp.dot(y.astype(BF16), wg_ref[...], preferred_element_type=F32)
    out = _rms(z[:, :GROUP_W] * jax.nn.sigmoid(z[:, GROUP_W:]), mg_ref[...])
    o_ref[...] = out.reshape(nb, tb, GROUP_W)


def _s5(proj3, mats, d_skip, w_glu_bf16, merge_g, tb):
    bsz, t, _ = proj3.shape
    w_in, c_out, d_io, a_grp, a_chunk = mats
    ns = S5_GROUPS * S5_STATE
    jw = S5_STEPS * GROUP_W
    lanes = GROUP_W // 2
    rows = bsz * (tb // S5_CHUNK)
    half = lambda j: pl.BlockSpec((bsz, tb, lanes), lambda i, j=j: (0, i, 2 * COL_S5 + j))
    full = lambda a, b: pl.BlockSpec((a, b), lambda i: (0, 0))
    return pl.pallas_call(
        _s5_kernel,
        grid=(t // tb,),
        in_specs=[half(0), half(1), full(jw, 2 * ns), full(2 * ns, jw), full(jw, jw), full(2, ns),
                  full(2, ns), full(1, GROUP_W), full(GROUP_W, 2 * GROUP_W), full(1, GROUP_W)],
        out_specs=pl.BlockSpec((bsz, tb, GROUP_W), lambda i: (0, i, 0)),
        out_shape=jax.ShapeDtypeStruct((bsz, t, GROUP_W), F32),
        scratch_shapes=[pltpu.VMEM((bsz, 2 * ns), F32), pltpu.VMEM((rows, 2 * ns), F32),
                        pltpu.VMEM((rows, 2 * ns), F32), pltpu.VMEM((2, bsz, tb, lanes), F32),
                        pltpu.VMEM((S5_CHUNK // S5_STEPS, rows, 2 * ns), F32)],
        compiler_params=_cparams("arbitrary"),
        name="s5",
    )(proj3, proj3, w_in, c_out, d_io, a_grp, a_chunk, d_skip.reshape(1, -1), w_glu_bf16,
      merge_g.reshape(1, -1))


def _rwkv_kernel(r_s, k_s, v_s, lw_s, a_s, b_s, g_ref, rk_ref, lnw_ref, lnb_ref, mg_ref,
                 o_ref, h_ref, y_s, m_s, n_s, p_s, z_s, *, reset):
    tb = r_s.shape[0]
    ch = RWKV_CHUNK
    nh = GROUP_W // HEAD_W
    if reset:
        h_ref[...] = jnp.zeros_like(h_ref)
        return

    bd_bf16 = _block_diag_mask(GROUP_W, HEAD_W).astype(BF16)

    n4 = nh * ch
    ri = lax.broadcasted_iota(jnp.int32, (n4, n4), 0)
    ci = lax.broadcasted_iota(jnp.int32, (n4, n4), 1)
    same_head = (ri // ch) == (ci // ch)
    strict = same_head & ((ri % ch) > (ci % ch))
    incl = same_head & ((ri % ch) >= (ci % ch))
    eye = (ri == ci).astype(F32)
    blk0 = RWKV_INV_BLOCK
    sizes = [blk0 << i for i in range((ch // blk0).bit_length())]
    same_blk = {b: (ri // b) == (ci // b) for b in sizes}
    hm = ((lax.broadcasted_iota(jnp.int32, (n4, GROUP_W), 0) // ch)
          == (lax.broadcasted_iota(jnp.int32, (n4, GROUP_W), 1) // HEAD_W))
    tri = (lax.broadcasted_iota(jnp.int32, (ch, ch), 0)
           >= lax.broadcasted_iota(jnp.int32, (ch, ch), 1)).astype(BF16)

    def stack(x):
        return jnp.where(hm, jnp.concatenate([x] * nh, axis=0), 0.0)

    def dot(x, y, dims=_NN):
        return lax.dot_general(x.astype(BF16), y.astype(BF16), (dims, ((), ())),
                               preferred_element_type=F32)

    chunks = range(tb // ch)
    pre = []
    for c in chunks:
        sl = slice(c * ch, (c + 1) * ch)
        lw = lw_s[sl, :]
        cl = _exact_lhs_mm(tri, lw)
        cl_end = cl[ch - 1:ch, :]
        e_in, e_ex = jnp.exp(cl), jnp.exp(cl - lw)
        e_neg, e_end = jnp.exp(-cl), jnp.exp(cl_end - cl)
        av, bv, kv, rv, vv = a_s[sl, :], b_s[sl, :], k_s[sl, :], r_s[sl, :], v_s[sl, :]
        pre.append(dict(
            at4=stack(av * e_ex), rt4=stack(rv * e_in), v4=stack(vv),
            bt4=jnp.concatenate([bv * e_neg] * nh, axis=0),
            kt4=jnp.concatenate([kv * e_neg] * nh, axis=0),
            bh4=stack(bv * e_end), kh4=stack(kv * e_end), g_end=jnp.exp(cl_end)))
        yield
    for d in pre:
        ar = jnp.concatenate([d["at4"], d["rt4"]], axis=0)
        sb = dot(ar, d["bt4"], _NT)
        sk = dot(ar, d["kt4"], _NT)
        d["l_ab"] = jnp.where(strict, sb[:n4], 0.0)
        d["l_ak"] = jnp.where(strict, sk[:n4], 0.0)
        d["l_rb"] = jnp.where(incl, sb[n4:], 0.0)
        d["l_rk"] = jnp.where(incl, sk[n4:], 0.0)
        yield
    for d in pre:
        nb8 = jnp.where(same_blk[blk0], d["l_ab"], 0.0)
        d["tinv"] = eye + nb8
        d["pw"] = dot(nb8, nb8)
        yield
    for d in pre:
        d["tinv"] = d["tinv"] + dot(d["tinv"], d["pw"])
        d["pw"] = dot(d["pw"], d["pw"])
        yield
    for d in pre:
        d["tinv"] = d["tinv"] + dot(d["tinv"], d["pw"])
        yield
    blk = blk0
    while blk < ch:
        for d in pre:
            d["pw"] = dot(jnp.where(same_blk[2 * blk] & ~same_blk[blk], d["l_ab"], 0.0), d["tinv"])
            yield
        for d in pre:
            d["tinv"] = d["tinv"] + dot(d["tinv"], d["pw"])
            yield
        blk *= 2
    for d in pre:
        d["lakv"] = dot(d["l_ak"], d["v4"])
        d["lrkv"] = dot(d["l_rk"], d["v4"])
        d["khv"] = dot(d["kh4"], d["v4"], _TN)
        yield
    for d in pre:
        d["x12"] = dot(d["tinv"], jnp.concatenate([d["at4"], d["lakv"]], axis=1))
        yield
    for c, d in zip(chunks, pre):
        mn = dot(d["bh4"], d["x12"], _TN)
        pz = dot(d["l_rb"], d["x12"])
        m_s[c] = jnp.where(ri == ci, d["g_end"], 0.0) + mn[:, :GROUP_W]
        n_s[c] = mn[:, GROUP_W:] + d["khv"]
        p_s[c] = d["rt4"] + pz[:, :GROUP_W]
        z_s[c] = pz[:, GROUP_W:] + d["lrkv"]
        yield

    h = h_ref[...]
    for c in range(tb // ch):
        y4 = dot(p_s[c], h) + z_s[c]
        y = y4[0:ch]
        for j in range(1, nh):
            y = y + y4[j * ch:(j + 1) * ch]
        y_s[c * ch:(c + 1) * ch, :] = y
        h = dot(m_s[c], h) + n_s[c]
        yield
    h_ref[...] = h

    y = y_s[...]
    r, k, v = r_s[...], k_s[...], v_s[...]
    mean = _head_sum(y, bd_bf16) * (1.0 / HEAD_W)
    d = y - mean
    var = _head_sum(d * d, bd_bf16) * (1.0 / HEAD_W)
    y = d * lax.rsqrt(var + RWKV_GN_EPS) * lnw_ref[...] + lnb_ref[...]
    y = y + _head_sum(r * k * rk_ref[...], bd_bf16) * v
    o_ref[...] = _rms(y * g_ref[...], mg_ref[...])


def _rwkv_part(proj3, abg3, lyr, p, merge_g, tb):
    bsz, t, _ = proj3.shape
    row = lambda x: x.reshape(1, -1)
    abg = lambda j: _col_spec(tb, j)
    mats = pltpu.VMEM((tb // RWKV_CHUNK, GROUP_W, GROUP_W), F32)
    return dict(
        body=_rwkv_kernel, stages=15 * (tb // RWKV_CHUNK) + 1,
        args=[proj3, proj3, proj3, proj3, abg3, abg3, abg3, row(p["rwkv_r_k"][lyr]),
              row(p["rwkv_ln_w"][lyr]), row(p["rwkv_ln_b"][lyr]), row(merge_g)],
        in_specs=[_col_spec(tb, COL_R), _col_spec(tb, COL_K), _col_spec(tb, COL_V),
                  _col_spec(tb, COL_LORA), abg(EXT_A), abg(EXT_B), abg(EXT_G),
                  _row_spec(), _row_spec(), _row_spec(), _row_spec()],
        out_specs=[_out_blk(tb)], out_shape=[jax.ShapeDtypeStruct((bsz, t, GROUP_W), F32)],
        scratch=[pltpu.VMEM((GROUP_W, GROUP_W), F32), pltpu.VMEM((tb, GROUP_W), F32),
                 mats, mats, mats, mats])


def _bf16_bits(x):
    u = lax.bitcast_convert_type(x, jnp.uint32)
    r = u + jnp.uint32(0x7FFF) + ((u >> 16) & jnp.uint32(1))
    return r & jnp.uint32(0xFFFF0000)


def _slab_copies(rows_ref, slab_hbm, row0, sem):
    tm = rows_ref.shape[0]
    return [(rows_ref.at[:, pl.ds(j * LANES, LANES)], slab_hbm.at[pl.ds(row0, tm), j, :], sem)
            for j in range(slab_hbm.shape[1])]


def _merge_router_kernel(oa_ref, ob_ref, oc_ref, od_ref, h_ref, wo_ref, nw_ref, wr_ref, br_ref,
                         h1_ref, hn_hbm, ids_ref, wts_ref, cnt_ref, pk_s, sem):
    i = pl.program_id(0)
    steps = pl.num_programs(0)
    tm = h_ref.shape[0]
    slot = i % 2

    def writes(s, step):
        return [pltpu.make_async_copy(v, h, m) for v, h, m in
                _slab_copies(pk_s.at[s], hn_hbm, step * tm, sem.at[s])]

    mix = jnp.concatenate([oa_ref[...], ob_ref[...], oc_ref[...], od_ref[...]], axis=1)
    h1 = h_ref[...] + jnp.dot(mix.astype(BF16), wo_ref[...], preferred_element_type=F32)
    h1_ref[...] = h1
    hn = _rms(h1, nw_ref[...])
    half = hn.shape[1] // 2

    @pl.when(i >= 2)
    def _():
        for cp in writes(slot, i - 2):
            cp.wait()

    pk_s[slot] = _bf16_bits(hn[:, :half]) | (_bf16_bits(hn[:, half:]) >> 16)
    for cp in writes(slot, i):
        cp.start()

    @pl.when(i == steps - 1)
    def _():
        for cp in writes(slot, i):
            cp.wait()

    @pl.when((i == steps - 1) & (i >= 1))
    def _():
        for cp in writes(1 - slot, i - 1):
            cp.wait()

    logits = _dg3(hn, wr_ref[...], _NN) + br_ref[...]
    lane = lax.broadcasted_iota(jnp.int32, logits.shape, 1)
    big = jnp.int32(ROUTER_LANES)
    is_c = (lane >= N_EXPERTS) & (lane < N_EXPERTS + MOE_GROUPS)
    cm = jnp.max(jnp.where(is_c, logits, NEG_BIG), axis=-1, keepdims=True)
    gsel = jnp.min(jnp.where(is_c & (logits == cm), lane, big), axis=-1, keepdims=True) - N_EXPERTS
    p_g = 1.0 / jnp.sum(jnp.where(is_c, jnp.exp(logits - cm), 0.0), axis=-1, keepdims=True)
    lo = gsel * MOE_PER_GROUP
    in_g = (lane >= lo) & (lane < lo + MOE_PER_GROUP)
    m1 = jnp.max(jnp.where(in_g, logits, NEG_BIG), axis=-1, keepdims=True)
    i1 = jnp.min(jnp.where(in_g & (logits == m1), lane, big), axis=-1, keepdims=True)
    in_g2 = in_g & (lane != i1)
    m2 = jnp.max(jnp.where(in_g2, logits, NEG_BIG), axis=-1, keepdims=True)
    i2 = jnp.min(jnp.where(in_g2 & (logits == m2), lane, big), axis=-1, keepdims=True)
    w1 = p_g / (1.0 + jnp.exp(m2 - m1))
    w2 = p_g - w1
    two = lax.broadcasted_iota(jnp.int32, ids_ref.shape, 1)
    ids_ref[...] = jnp.where(two == 0, i1, i2)
    wts_ref[...] = jnp.where(two == 0, w1, w2)

    @pl.when(pl.program_id(0) == 0)
    def _():
        cnt_ref[...] = jnp.zeros_like(cnt_ref)

    cnt_ref[...] += jnp.sum(((lane == i1) | (lane == i2)).astype(F32), axis=0, keepdims=True)


def _merge_router(outs, h2d, w_out_bf16, norm_w, w_router, b_router, tm):
    n, d = h2d.shape
    grp = pl.BlockSpec((tm, GROUP_W), lambda i: (i, 0))
    full = lambda a, b: pl.BlockSpec((a, b), lambda i: (0, 0))
    rowblk = lambda w: pl.BlockSpec((tm, w), lambda i: (i, 0))
    return pl.pallas_call(
        _merge_router_kernel,
        grid=(n // tm,),
        in_specs=[grp, grp, grp, grp, rowblk(d), full(d, d), full(1, d), full(d, ROUTER_LANES),
                  full(1, ROUTER_LANES)],
        out_specs=[rowblk(d), pl.BlockSpec(memory_space=pl.ANY),
                   rowblk(MOE_TOPK), rowblk(MOE_TOPK), full(1, ROUTER_LANES)],
        out_shape=[jax.ShapeDtypeStruct((n, d), F32),
                   jax.ShapeDtypeStruct((n, d // 2 // LANES, LANES), jnp.uint32),
                   jax.ShapeDtypeStruct((n, MOE_TOPK), jnp.int32),
                   jax.ShapeDtypeStruct((n, MOE_TOPK), F32),
                   jax.ShapeDtypeStruct((1, ROUTER_LANES), F32)],
        scratch_shapes=[pltpu.VMEM((2, tm, d // 2), jnp.uint32), pltpu.SemaphoreType.DMA((2,))],
        compiler_params=_cparams("arbitrary"),
        name="merge_router",
    )(*[o.reshape(n, GROUP_W) for o in outs], h2d, w_out_bf16, norm_w.reshape(1, d), w_router, b_router)


def _row_copy(src_hbm, src_row, dst_ref, dst_row, sem):
    return pltpu.make_async_copy(src_hbm.at[pl.ds(src_row, 1), :], dst_ref.at[pl.ds(dst_row, 1), :], sem)


def _position_kernel(ids_ref, base_ref, pos_ref, run_ref):
    tm = ids_ref.shape[0]

    @pl.when(pl.program_id(0) == 0)
    def _():
        run_ref[...] = jnp.zeros_like(run_ref)

    i1, i2 = ids_ref[:, 0:1], ids_ref[:, 1:2]
    lane = lax.broadcasted_iota(jnp.int32, (tm, ROUTER_LANES), 1)
    hit = ((lane == i1) | (lane == i2)).astype(BF16)
    earlier = (lax.broadcasted_iota(jnp.int32, (tm, tm), 0)
               > lax.broadcasted_iota(jnp.int32, (tm, tm), 1)).astype(BF16)
    rank = jnp.dot(earlier, hit, preferred_element_type=F32) + run_ref[...]
    where_to = rank + base_ref[...]
    p1 = jnp.sum(jnp.where(lane == i1, where_to, 0.0), axis=-1, keepdims=True)
    p2 = jnp.sum(jnp.where(lane == i2, where_to, 0.0), axis=-1, keepdims=True)
    two = lax.broadcasted_iota(jnp.int32, pos_ref.shape, 1)
    pos_ref[...] = jnp.where(two == 0, p1, p2).astype(jnp.int32)
    run_ref[...] += jnp.sum(hit.astype(F32), axis=0, keepdims=True)


def _positions(ids, group_start, tm):
    n = ids.shape[0]
    return pl.pallas_call(
        _position_kernel,
        grid=(n // tm,),
        in_specs=[pl.BlockSpec((tm, MOE_TOPK), lambda i: (i, 0)),
                  pl.BlockSpec((1, ROUTER_LANES), lambda i: (0, 0))],
        out_specs=pl.BlockSpec((tm, MOE_TOPK), lambda i: (i, 0)),
        out_shape=jax.ShapeDtypeStruct((n, MOE_TOPK), jnp.int32),
        scratch_shapes=[pltpu.VMEM((1, ROUTER_LANES), F32)],
        compiler_params=_cparams("arbitrary"),
        name="moe_positions",
    )(ids, group_start)


def _dispatch_kernel(p0_ref, p1_ref, x_ref, xs_in_hbm, xs_hbm, sem):
    del xs_in_hbm
    n = x_ref.shape[0]

    def copies(i):
        return (pltpu.make_async_copy(x_ref.at[i], xs_hbm.at[p0_ref[i]], sem),
                pltpu.make_async_copy(x_ref.at[i], xs_hbm.at[p1_ref[i]], sem))

    def issue(i, c):
        for prio, cp in enumerate(copies(i)):
            cp.start(priority=prio)
        return c

    def drain(i, c):
        for cp in copies(i):
            cp.wait()
        return c

    lax.fori_loop(0, n, issue, 0, unroll=8)
    lax.fori_loop(0, n, drain, 0, unroll=8)


def _dispatch(x, pos0, pos1, n_rows, chunk):
    n, slabs, lanes = x.shape
    smem = lambda: pl.BlockSpec((chunk,), lambda s: (s,), memory_space=pltpu.SMEM)
    return pl.pallas_call(
        _dispatch_kernel,
        grid=(n // chunk,),
        in_specs=[smem(), smem(), pl.BlockSpec((chunk, slabs, lanes), lambda s: (s, 0, 0)),
                  pl.BlockSpec(memory_space=pl.ANY)],
        out_specs=pl.BlockSpec(memory_space=pl.ANY),
        out_shape=jax.ShapeDtypeStruct((n_rows, slabs, lanes), x.dtype),
        input_output_aliases={3: 0},
        scratch_shapes=[pltpu.SemaphoreType.DMA(())],
        compiler_params=_cparams("arbitrary"),
        name="moe_dispatch",
    )(pos0, pos1, x, jnp.zeros((n_rows, slabs, lanes), x.dtype))


def _expert_kernel(te_ref, nt_ref, xs_hbm, wg_ref, wu_ref, wd_ref, o_ref, wg_s, wu_s, wd_s, x_s, sem):
    i = pl.program_id(0)
    tm = o_ref.shape[0]
    slot = i % 2

    def reads(s, tile):
        return [pltpu.make_async_copy(h, v, m) for v, h, m in
                _slab_copies(x_s.at[s], xs_hbm, tile * tm, sem.at[s])]

    @pl.when(i == 0)
    def _():
        for cp in reads(0, 0):
            cp.start()

    @pl.when(i + 1 < pl.num_programs(0))
    def _():
        for cp in reads(1 - slot, i + 1):
            cp.start()

    @pl.when((i == 0) | (te_ref[i] != te_ref[jnp.maximum(i - 1, 0)]))
    def _():
        wg_s[...] = wg_ref[...].astype(BF16)
        wu_s[...] = wu_ref[...].astype(BF16)
        wd_s[...] = wd_ref[...].astype(BF16)

    for cp in reads(slot, i):
        cp.wait()

    @pl.when(i < nt_ref[0])
    def _():
        p = x_s[slot]
        xa = lax.bitcast_convert_type(p & jnp.uint32(0xFFFF0000), F32).astype(BF16)
        xb = lax.bitcast_convert_type(p << 16, F32).astype(BF16)
        x = jnp.concatenate([xa, xb], axis=1)
        he = (_silu(jnp.dot(x, wg_s[...], preferred_element_type=F32))
              * jnp.dot(x, wu_s[...], preferred_element_type=F32))
        o_ref[...] = jnp.dot(he.astype(BF16), wd_s[...], preferred_element_type=F32)

    @pl.when(i >= nt_ref[0])
    def _():
        o_ref[...] = jnp.zeros_like(o_ref)


def _experts(xs, tile_expert, n_tiles_used, wg, wu, wd, layer, tm):
    p, slabs, lanes = xs.shape
    d = 2 * slabs * lanes
    pick = lambda i, te, nt: (layer, te[i], 0, 0)
    grid_spec = pltpu.PrefetchScalarGridSpec(
        num_scalar_prefetch=2,
        grid=(p // tm,),
        in_specs=[pl.BlockSpec(memory_space=pl.ANY),
                  pl.BlockSpec((None, None, d, D_EXPERT), pick),
                  pl.BlockSpec((None, None, d, D_EXPERT), pick),
                  pl.BlockSpec((None, None, D_EXPERT, d), pick)],
        out_specs=pl.BlockSpec((tm, d), lambda i, te, nt: (i, 0)),
        scratch_shapes=[pltpu.VMEM((d, D_EXPERT), BF16), pltpu.VMEM((d, D_EXPERT), BF16),
                        pltpu.VMEM((D_EXPERT, d), BF16), pltpu.VMEM((2, tm, slabs * lanes), xs.dtype),
                        pltpu.SemaphoreType.DMA((2,))],
    )
    return pl.pallas_call(
        _expert_kernel,
        grid_spec=grid_spec,
        out_shape=jax.ShapeDtypeStruct((p, d), F32),
        compiler_params=_cparams("arbitrary"),
        name="moe_experts",
    )(tile_expert, n_tiles_used, xs, wg, wu, wd)


def _combine_kernel(i0_ref, i1_ref, h1_ref, w_ref, fw_ref, ys_hbm, o_ref, buf0, buf1, sem, *, final_norm):
    n = o_ref.shape[0]

    def issue(i, c):
        _row_copy(ys_hbm, i0_ref[i], buf0, i, sem).start(priority=0)
        _row_copy(ys_hbm, i1_ref[i], buf1, i, sem).start(priority=1)
        return c

    def drain(i, c):
        _row_copy(ys_hbm, 0, buf0, i, sem).wait()
        _row_copy(ys_hbm, 0, buf1, i, sem).wait()
        return c

    lax.fori_loop(0, n, issue, 0, unroll=8)
    lax.fori_loop(0, n, drain, 0, unroll=8)
    out = h1_ref[...] + w_ref[:, 0:1] * buf0[...] + w_ref[:, 1:2] * buf1[...]
    if final_norm:
        out = _rms(out, fw_ref[...])
    o_ref[...] = out


def _combine(ys, pos0, pos1, wts, h1, final_w, final_norm, chunk):
    n, d = h1.shape
    smem = lambda: pl.BlockSpec((chunk,), lambda s: (s,), memory_space=pltpu.SMEM)
    return pl.pallas_call(
        functools.partial(_combine_kernel, final_norm=final_norm),
        grid=(n // chunk,),
        in_specs=[smem(), smem(), pl.BlockSpec((chunk, d), lambda s: (s, 0)),
                  pl.BlockSpec((chunk, MOE_TOPK), lambda s: (s, 0)),
                  pl.BlockSpec((1, d), lambda s: (0, 0)), pl.BlockSpec(memory_space=pl.ANY)],
        out_specs=pl.BlockSpec((chunk, d), lambda s: (s, 0)),
        out_shape=jax.ShapeDtypeStruct((n, d), F32),
        scratch_shapes=[pltpu.VMEM((chunk, d), F32), pltpu.VMEM((chunk, d), F32),
                        pltpu.SemaphoreType.DMA(())],
        compiler_params=_cparams("arbitrary"),
        name="moe_combine",
    )(pos0, pos1, h1, wts, final_w.reshape(1, d), ys)


def _moe(hn_packed, ids, wts, counts, h1, wg, wu, wd, layer, final_w, final_norm, tm):
    n = h1.shape[0]
    n_tiles = (n * MOE_TOPK) // tm + N_EXPERTS
    cnt = counts[0, :N_EXPERTS].astype(jnp.int32)
    padded = ((cnt + tm - 1) // tm) * tm
    ends = jnp.cumsum(padded)
    tile_start = jnp.arange(n_tiles, dtype=jnp.int32) * tm
    tile_expert = jnp.minimum(jnp.sum(ends[None, :] <= tile_start[:, None], axis=1),
                              N_EXPERTS - 1).astype(jnp.int32)
    n_tiles_used = (ends[-1] // tm).astype(jnp.int32).reshape(1)
    group_start = jnp.zeros((1, ROUTER_LANES), F32).at[0, :N_EXPERTS].set((ends - padded).astype(F32))
    pos = _positions(ids, group_start, _pick_div(n, 512))
    pos0, pos1 = pos[:, 0], pos[:, 1]
    xs = _dispatch(hn_packed, pos0, pos1, n_tiles * tm, _pick_div(n, 1024))
    ys = _experts(xs, tile_expert, n_tiles_used, wg, wu, wd, layer, tm)
    return _combine(ys, pos0, pos1, wts, h1, final_w, final_norm, _pick_div(n, 1024))


def _pick_div(n, pref):
    while n % pref:
        pref //= 2
    return pref


def _pick(n, pref):
    return pref if n % pref == 0 else n


def kernel(x, norm1_w, w_in, hgrn_lb_logits, hgrn_norm_w, s5_lambda_re, s5_lambda_im, s5_log_dt, s5_b_re, s5_b_im, s5_c_re, s5_c_im, s5_d, s5_w_glu, rwkv_mu, rwkv_w0, rwkv_w2, rwkv_a0, rwkv_a2, rwkv_g2, rwkv_k_k, rwkv_k_a, rwkv_r_k, rwkv_v0, rwkv_v1, rwkv_v2, rwkv_ln_w, rwkv_ln_b, lru_conv_w, lru_conv_b, lru_wa, lru_ba, lru_wx, lru_bx, lru_lambda, merge_gain, w_out, norm2_w, moe_coarse_w, moe_coarse_b, moe_fine_w, moe_fine_b, moe_w_gate, moe_w_up, moe_w_down, final_norm_w):
    bsz, t, d = x.shape
    n = bsz * t
    depth = w_in.shape[0]
    rw = dict(rwkv_mu=rwkv_mu, rwkv_w0=rwkv_w0, rwkv_w2=rwkv_w2, rwkv_a0=rwkv_a0, rwkv_a2=rwkv_a2,
              rwkv_g2=rwkv_g2, rwkv_k_k=rwkv_k_k, rwkv_k_a=rwkv_k_a, rwkv_r_k=rwkv_r_k,
              rwkv_v0=rwkv_v0, rwkv_v1=rwkv_v1, rwkv_v2=rwkv_v2, rwkv_ln_w=rwkv_ln_w,
              rwkv_ln_b=rwkv_ln_b)
    lb_all = jnp.cumsum(jax.nn.softmax(hgrn_lb_logits.astype(F32), axis=0), axis=0)
    lb_all = lb_all - lb_all[:1]

    tm_proj = _pick(n, 512)
    tm_in = _pick_div(t, 512)
    tm_moe = 512
    tb_mix = _pick(t, 256)
    tb_s5 = _pick(t, 512)

    h = x.reshape(n, d)
    v_first = None
    for l in range(depth):
        mg = merge_gain[l].reshape(4, GROUP_W)
        proj, abg, v_first = _in_proj(h, norm1_w[l], w_in[l].astype(BF16), v_first, l, rw, t, tm_in)
        proj3 = proj.reshape(bsz, t, -1)
        abg3 = abg.reshape(bsz, t, -1)
        mats = _s5_matrices(s5_lambda_re[l], s5_lambda_im[l], s5_log_dt[l], s5_b_re[l], s5_b_im[l],
                            s5_c_re[l], s5_c_im[l])
        o_b = _s5(proj3, mats, s5_d[l], s5_w_glu[l].astype(BF16), mg[1], tb_s5)
        parts = [_rwkv_part(proj3, abg3, l, rw, mg[2], tb_mix),
                 _hgrn_part(proj3, lb_all[l], hgrn_norm_w[l], mg[0], tb_mix),
                 _lru_part(proj3, lru_conv_w[l], lru_conv_b[l], _block_diag_weight(lru_wa[l]),
                           lru_ba[l], _block_diag_weight(lru_wx[l]), lru_bx[l], lru_lambda[l],
                           mg[3], tb_mix)]
        outs = _mixers(parts, bsz, t // tb_mix, "mixers")
        o_c, o_a, o_d = outs
        w_router = jnp.concatenate(
            [moe_fine_w[l].transpose(1, 0, 2).reshape(d, N_EXPERTS), moe_coarse_w[l],
             jnp.zeros((d, ROUTER_LANES - N_EXPERTS - MOE_GROUPS), F32)], axis=1)
        b_router = jnp.concatenate(
            [moe_fine_b[l].reshape(N_EXPERTS), moe_coarse_b[l],
             jnp.zeros((ROUTER_LANES - N_EXPERTS - MOE_GROUPS,), F32)]).reshape(1, ROUTER_LANES)
        h1, hn, ids, wts, counts = _merge_router((o_a, o_b, o_c, o_d), h, w_out[l].astype(BF16),
                                                 norm2_w[l], w_router, b_router, tm_proj)
        h = _moe(hn, ids, wts, counts, h1, moe_w_gate, moe_w_up, moe_w_down, l,
                 final_norm_w, l == depth - 1, tm_moe)
    return h.reshape(bsz, t, d)
```

```python
import functools

import jax
import jax.numpy as jnp
from jax import lax
from jax.experimental import pallas as pl
from jax.experimental.pallas import tpu as pltpu

F32 = jnp.float32
BF16 = jnp.bfloat16

GROUP_W = 256
RMS_EPS = 1e-6
HEAD_W = 64
HGRN_CHUNK = 32
S5_CH = 16
S5_GROUPS = GROUP_W // S5_CH
S5_STATE = 64
S5_CHUNK = 8
S5_STEPS = 4
RWKV_CHUNK = 64
RWKV_INV_BLOCK = 8
RWKV_GN_EPS = 64e-5
RWKV_W_LORA = 64
RWKV_V_LORA = 32
LRU_CONV = 4
LRU_C = 8.0
MOE_GROUPS = 4
MOE_PER_GROUP = 8
N_EXPERTS = MOE_GROUPS * MOE_PER_GROUP
MOE_TOPK = 2
D_EXPERT = 512
LANES = 128
SUBLANES = 8
ROUTER_LANES = LANES
NEG_BIG = -1e30
V7X_VMEM_BYTES = 64 * 1024 * 1024
VMEM_LIMIT = V7X_VMEM_BYTES - 8 * 1024 * 1024

COL_HQ, COL_HF, COL_HI, COL_HG, COL_S5, COL_R, COL_K, COL_V, COL_LORA, COL_LG, COL_LX = range(11)
EXT_A, EXT_B, EXT_G = range(3)


def _cparams(*sem):
    return pltpu.CompilerParams(dimension_semantics=sem, vmem_limit_bytes=VMEM_LIMIT)


def _mm(a, b):
    return jnp.dot(a.astype(BF16), b.astype(BF16), preferred_element_type=F32)


def _split2(x):
    hi = x.astype(BF16)
    lo = (x - hi.astype(F32)).astype(BF16)
    return hi, lo


def _dg3(a, b, dims):
    ah, al = _split2(a)
    bh, bl = _split2(b)
    d = lambda x, y: lax.dot_general(x, y, (dims, ((), ())), preferred_element_type=F32)
    return d(ah, bh) + d(ah, bl) + d(al, bh)


_NN = ((1,), (0,))
_NT = ((1,), (1,))
_TN = ((0,), (0,))


def _exact_lhs_mm(m_bf16, x):
    h1 = x.astype(BF16)
    r1 = x - h1.astype(F32)
    h2 = r1.astype(BF16)
    h3 = (r1 - h2.astype(F32)).astype(BF16)
    d = lambda y: jnp.dot(m_bf16, y, preferred_element_type=F32)
    return d(h1) + d(h2) + d(h3)


def _head_sum(x, bd_bf16):
    return jnp.dot(x.astype(BF16), bd_bf16, preferred_element_type=F32)


def _block_diag_mask(n, blk):
    r = lax.broadcasted_iota(jnp.int32, (n, n), 0) // blk
    c = lax.broadcasted_iota(jnp.int32, (n, n), 1) // blk
    return r == c


def _rms(x, w):
    return x * lax.rsqrt(jnp.mean(x * x, axis=-1, keepdims=True) + RMS_EPS) * w


def _silu(x):
    return x * jax.nn.sigmoid(x)


def _softplus(x):
    return jnp.maximum(x, 0.0) + jnp.log(1.0 + jnp.exp(-jnp.abs(x)))


def _in_proj_kernel(*refs, has_vmix, tiles_per_seq):
    if has_vmix:
        (x_ref, nw_ref, w_ref, mu_ref, w0_ref, w2_ref, a0_ref, a2_ref, g2_ref, kk_ref, ka_ref,
         vf_ref, v0_ref, v1_ref, v2_ref, o_ref, o2_ref, prev_s) = refs
    else:
        (x_ref, nw_ref, w_ref, mu_ref, w0_ref, w2_ref, a0_ref, a2_ref, g2_ref, kk_ref, ka_ref,
         o_ref, o2_ref, vf_out_ref, prev_s) = refs
    tm = x_ref.shape[0]
    gw = GROUP_W
    c0, c1 = COL_R * gw, (COL_LORA + 1) * gw
    @pl.when((pl.program_id(0) % tiles_per_seq) == 0)
    def _():
        prev_s[...] = jnp.zeros_like(prev_s)

    y = _rms(x_ref[...], nw_ref[...]).astype(BF16)
    raw = jnp.dot(y, w_ref[:, c0:c1], preferred_element_type=F32)

    row0 = lax.broadcasted_iota(jnp.int32, (tm, 1), 0) == 0
    sh = jnp.where(row0, prev_s[...], pltpu.roll(raw, 1, axis=0))
    prev_s[...] = raw[tm - 1:tm, :]
    pf = raw + mu_ref[...] * (sh - raw)
    r, k, v, lo = (pf[:, j * GROUP_W:(j + 1) * GROUP_W] for j in range(4))

    o_ref[:, :c0] = jnp.dot(y, w_ref[:, :c0], preferred_element_type=F32)
    bd_bf16 = _block_diag_mask(GROUP_W, HEAD_W).astype(BF16)
    lo_wa, lo_g = lo[:, :LANES], lo[:, LANES:]
    w_raw = -_softplus(-(w0_ref[...] + _mm(jnp.tanh(lo_wa), w2_ref[...]))) - 0.5
    alr = jax.nn.sigmoid(a0_ref[...] + _mm(lo_wa, a2_ref[...]))
    o_ref[:, c1:] = jnp.dot(y, w_ref[:, c1:], preferred_element_type=F32)
    kkr = k * kk_ref[...]
    kk = kkr / jnp.maximum(jnp.sqrt(_head_sum(kkr * kkr, bd_bf16)), 1e-12)
    if has_vmix:
        gate = jax.nn.sigmoid(v0_ref[...] + _mm(_mm(v, v1_ref[...]), v2_ref[...]))
        v = v + (vf_ref[...] - v) * gate
    else:
        vf_out_ref[...] = v
    o_ref[:, c0:c0 + GROUP_W] = r
    o_ref[:, c0 + GROUP_W:c0 + 2 * GROUP_W] = k * (1.0 + (alr - 1.0) * ka_ref[...])
    o_ref[:, c0 + 2 * GROUP_W:c0 + 3 * GROUP_W] = v
    o_ref[:, c0 + 3 * GROUP_W:c1] = -jnp.exp(w_raw)
    o2_ref[:, EXT_A * gw:(EXT_A + 1) * gw] = -kk
    o2_ref[:, EXT_B * gw:(EXT_B + 1) * gw] = kk * alr
    o2_ref[:, EXT_G * gw:(EXT_G + 1) * gw] = _mm(jax.nn.sigmoid(lo_g), g2_ref[...])


def _pad_rows(w, start, total=LANES):
    out = jnp.zeros((total, w.shape[1]), F32).at[start:start + w.shape[0]].set(w)
    return out.astype(BF16)


def _in_proj(h2d, norm_w, w_bf16, v_first, lyr, p, t, tm):
    n, d = h2d.shape
    d_in = w_bf16.shape[1]
    has_vmix = v_first is not None
    row = lambda x: x.reshape(1, -1)
    full = lambda a, b: pl.BlockSpec((a, b), lambda i: (0, 0))
    rowblk = lambda w: pl.BlockSpec((tm, w), lambda i: (i, 0))
    args = [h2d, row(norm_w), w_bf16, row(p["rwkv_mu"][lyr]), row(p["rwkv_w0"][lyr]),
            _pad_rows(p["rwkv_w2"][lyr], 0), row(p["rwkv_a0"][lyr]),
            _pad_rows(p["rwkv_a2"][lyr], RWKV_W_LORA), p["rwkv_g2"][lyr].astype(BF16),
            row(p["rwkv_k_k"][lyr]), row(p["rwkv_k_a"][lyr])]
    specs = [rowblk(d), full(1, d), full(d, d_in), full(1, 4 * GROUP_W), full(1, GROUP_W),
             full(LANES, GROUP_W), full(1, GROUP_W), full(LANES, GROUP_W), full(LANES, GROUP_W),
             full(1, GROUP_W), full(1, GROUP_W)]
    out_specs = [rowblk(d_in), rowblk(3 * GROUP_W)]
    out_shape = [jax.ShapeDtypeStruct((n, d_in), F32), jax.ShapeDtypeStruct((n, 3 * GROUP_W), F32)]
    if has_vmix:
        v1 = jnp.zeros((GROUP_W, LANES), F32).at[:, :RWKV_V_LORA].set(p["rwkv_v1"][lyr - 1]).astype(BF16)
        v2 = jnp.zeros((LANES, GROUP_W), F32).at[:RWKV_V_LORA].set(p["rwkv_v2"][lyr - 1]).astype(BF16)
        args += [v_first, row(p["rwkv_v0"][lyr - 1]), v1, v2]
        specs += [rowblk(GROUP_W), full(1, GROUP_W), full(GROUP_W, LANES), full(LANES, GROUP_W)]
    else:
        out_specs.append(rowblk(GROUP_W))
        out_shape.append(jax.ShapeDtypeStruct((n, GROUP_W), F32))
    res = pl.pallas_call(
        functools.partial(_in_proj_kernel, has_vmix=has_vmix, tiles_per_seq=t // tm),
        grid=(n // tm,),
        in_specs=specs,
        out_specs=out_specs,
        out_shape=out_shape,
        scratch_shapes=[pltpu.VMEM((1, 4 * GROUP_W), F32)],
        compiler_params=_cparams("arbitrary"),
        name="in_proj",
    )(*args)
    return res[0], res[1], (v_first if has_vmix else res[2])


def _col_spec(tb, col):
    return pl.BlockSpec((None, tb, GROUP_W), lambda b, t: (b, t, col))


def _row_spec(width=GROUP_W):
    return pl.BlockSpec((1, width), lambda b, t: (0, 0))


def _full_spec(shape):
    return pl.BlockSpec(shape, lambda b, t: (0,) * len(shape))


def _hgrn_kernel(q_ref, f_ref, i_ref, g_ref, lb_ref, nw_ref, mg_ref, o_ref,
                 st_ref, q_s, k_s, v_s, lf_s, o_s, *, reset):
    ch = HGRN_CHUNK
    tb = q_ref.shape[0]
    if reset:
        st_ref[...] = jnp.zeros_like(st_ref)
        return

    lb = lb_ref[...]
    fx = f_ref[...]
    x1 = jnp.log(lb)
    x2 = jnp.log(1.0 - lb) - _softplus(-fx)
    m = jnp.maximum(x1, x2)
    lf_s[...] = m + jnp.log(jnp.exp(x1 - m) + jnp.exp(x2 - m))
    q_s[...] = _silu(q_ref[...])
    k_s[...] = (1.0 - lb) * jax.nn.sigmoid(-fx)
    v_s[...] = _silu(i_ref[...])
    yield

    bd = _block_diag_mask(GROUP_W, HEAD_W)
    bd_bf16 = bd.astype(BF16)
    tri = (lax.broadcasted_iota(jnp.int32, (ch, ch), 0)
           >= lax.broadcasted_iota(jnp.int32, (ch, ch), 1))
    tri_bf16 = tri.astype(BF16)
    tri3 = (lax.broadcasted_iota(jnp.int32, (ch, ch, 1), 0)
            <= lax.broadcasted_iota(jnp.int32, (ch, ch, 1), 1))

    st = st_ref[...]
    for c in range(tb // ch):
        sl = slice(c * ch, (c + 1) * ch)
        qc, kc, vc = q_s[sl, :], k_s[sl, :], v_s[sl, :]
        b = _exact_lhs_mm(tri_bf16, lf_s[sl, :])
        rel = b[None, :, :] - b[:, None, :]
        dec = jnp.exp(jnp.where(tri3, rel, NEG_BIG))
        p = (qc[None, :, :] * kc[:, None, :]) * dec
        sc = jnp.dot(p.reshape(ch * ch, GROUP_W).astype(BF16), bd_bf16,
                     preferred_element_type=F32).reshape(ch, ch, GROUP_W)
        o_intra = jnp.sum(sc * vc[:, None, :], axis=0)
        o_inter = lax.dot_general((qc * jnp.exp(b)).astype(BF16), st.astype(BF16),
                                  (_NT, ((), ())), preferred_element_type=F32)
        b_end = b[ch - 1:ch, :]
        kh = kc * jnp.exp(b_end - b)
        upd = lax.dot_general(vc.astype(BF16), kh.astype(BF16), (_TN, ((), ())),
                              preferred_element_type=F32)
        st = st * jnp.exp(b_end) + jnp.where(bd, upd, 0.0)
        o_s[sl, :] = o_intra + o_inter
        yield
    st_ref[...] = st

    o = o_s[...]
    ms = _head_sum(o * o, bd_bf16) * (1.0 / HEAD_W)
    o = o * lax.rsqrt(ms + RMS_EPS) * nw_ref[...] * _silu(g_ref[...])
    o_ref[...] = _rms(o, mg_ref[...])


def _out_blk(tb):
    return pl.BlockSpec((None, tb, GROUP_W), lambda b, i: (b, i, 0))


def _hgrn_part(proj3, lb, norm_w, merge_g, tb):
    bsz, t, _ = proj3.shape
    blk = pltpu.VMEM((tb, GROUP_W), F32)
    return dict(
        body=_hgrn_kernel, stages=tb // HGRN_CHUNK + 2,
        args=[proj3, proj3, proj3, proj3, lb.reshape(1, -1), norm_w.reshape(1, -1), merge_g.reshape(1, -1)],
        in_specs=[_col_spec(tb, COL_HQ), _col_spec(tb, COL_HF), _col_spec(tb, COL_HI),
                  _col_spec(tb, COL_HG), _row_spec(), _row_spec(), _row_spec()],
        out_specs=[_out_blk(tb)],
        out_shape=[jax.ShapeDtypeStruct((bsz, t, GROUP_W), F32)],
        scratch=[pltpu.VMEM((GROUP_W, GROUP_W), F32), blk, blk, blk, blk, blk])


def _mixer_kernel(*refs, parts):
    groups, i = [], 0
    for kind in range(3):
        for body, counts in parts:
            groups.append(refs[i:i + counts[kind]])
            i += counts[kind]
    k = len(parts)
    per_part = [groups[j] + groups[k + j] + groups[2 * k + j] for j in range(k)]

    @pl.when(pl.program_id(1) == 0)
    def _():
        for (body, _), r in zip(parts, per_part):
            for _step in body(*r, reset=True):
                pass

    runs = [[body(*r, reset=False), 0, counts[3]] for (body, counts), r in zip(parts, per_part)]
    while runs:
        run = min(runs, key=lambda x: x[1] / x[2])
        try:
            next(run[0])
            run[1] += 1
        except StopIteration:
            runs.remove(run)


def _mixers(parts, bsz, n_tblocks, name):
    cat = lambda key: [x for p in parts for x in p[key]]
    light = tuple((p["body"], (len(p["args"]), len(p["out_specs"]), len(p["scratch"]), p["stages"]))
                  for p in parts)
    return pl.pallas_call(
        functools.partial(_mixer_kernel, parts=light),
        grid=(bsz, n_tblocks),
        in_specs=cat("in_specs"),
        out_specs=cat("out_specs"),
        out_shape=cat("out_shape"),
        scratch_shapes=cat("scratch"),
        compiler_params=_cparams("parallel", "arbitrary"),
        name=name,
    )(*cat("args"))


def _lru_kernel(xg_ref, xr_ref, cw_ref, cb_ref, wa_ref, ba_ref, wx_ref, bx_ref, lam_ref, mg_ref,
                o_ref, buf_ref, h_ref, *, reset):
    tb = xr_ref.shape[0]
    pad = SUBLANES
    if reset:
        buf_ref[0:pad, :] = jnp.zeros((pad, GROUP_W), F32)
        h_ref[...] = jnp.zeros_like(h_ref)
        return

    xr = xr_ref[...]
    buf_ref[pad:pad + tb, :] = xr
    xc = cb_ref[...] + jnp.zeros_like(xr)
    for j in range(LRU_CONV):
        xc = xc + cw_ref[j:j + 1, :] * buf_ref[pl.ds(pad - (LRU_CONV - 1) + j, tb), :]
    buf_ref[0:pad, :] = xr[tb - pad:tb, :]

    r = jax.nn.sigmoid(jnp.dot(xc.astype(BF16), wa_ref[...], preferred_element_type=F32) + ba_ref[...])
    gi = jax.nn.sigmoid(jnp.dot(xc.astype(BF16), wx_ref[...], preferred_element_type=F32) + bx_ref[...])
    log_a = -LRU_C * r * _softplus(-lam_ref[...])
    a = jnp.exp(log_a)
    x = jnp.sqrt(1.0 - jnp.exp(2.0 * log_a)) * (gi * xc)
    yield

    rows = lax.broadcasted_iota(jnp.int32, (tb, 1), 0)
    k = 1
    while k < tb:
        keep = rows >= k
        x = x + jnp.where(keep, a * pltpu.roll(x, k, axis=0), 0.0)
        a = jnp.where(keep, a * pltpu.roll(a, k, axis=0), a)
        k *= 2
        yield
    h = x + a * h_ref[...]
    h_ref[...] = h[tb - 1:tb, :]
    o_ref[...] = _rms(jax.nn.gelu(xg_ref[...]) * h, mg_ref[...])


def _lru_part(proj3, conv_w, conv_b, wa_bd, ba, wx_bd, bx, lam, merge_g, tb):
    bsz, t, _ = proj3.shape
    return dict(
        body=_lru_kernel, stages=tb.bit_length() + 1,
        args=[proj3, proj3, conv_w, conv_b.reshape(1, -1), wa_bd, ba.reshape(1, -1), wx_bd,
              bx.reshape(1, -1), lam.reshape(1, -1), merge_g.reshape(1, -1)],
        in_specs=[_col_spec(tb, COL_LG), _col_spec(tb, COL_LX), _full_spec((LRU_CONV, GROUP_W)),
                  _row_spec(), _full_spec((GROUP_W, GROUP_W)), _row_spec(),
                  _full_spec((GROUP_W, GROUP_W)), _row_spec(), _row_spec(), _row_spec()],
        out_specs=[_out_blk(tb)],
        out_shape=[jax.ShapeDtypeStruct((bsz, t, GROUP_W), F32)],
        scratch=[pltpu.VMEM((tb + SUBLANES, GROUP_W), F32), pltpu.VMEM((1, GROUP_W), F32)])


def _block_diag_weight(w):
    h, n, _ = w.shape
    eye = jnp.eye(h, dtype=w.dtype)
    return jnp.einsum('hij,hg->higj', w, eye).reshape(h * n, h * n).astype(BF16)


def _s5_matrices(lam_re, lam_im, log_dt, b_re, b_im, c_re, c_im):
    L, G, P, C = S5_CHUNK, S5_GROUPS, S5_STATE, S5_CH
    lr, li = lam_re.astype(F32), lam_im.astype(F32)
    dt = jnp.exp(log_dt.astype(F32))[:, None]
    mag = jnp.exp(lr * dt)
    a_re, a_im = mag * jnp.cos(li * dt), mag * jnp.sin(li * dt)
    den = lr * lr + li * li
    kap_re = ((a_re - 1.0) * lr + a_im * li) / den
    kap_im = (a_im * lr - (a_re - 1.0) * li) / den
    br, bi = b_re.astype(F32), b_im.astype(F32)
    bb_re = kap_re[..., None] * br - kap_im[..., None] * bi
    bb_im = kap_re[..., None] * bi + kap_im[..., None] * br
    cr, ci = c_re.astype(F32), c_im.astype(F32)
    eye = jnp.eye(G, dtype=F32)
    J = S5_STEPS
    kk = jnp.arange(J + 1, dtype=F32)[:, None, None]
    pmag = jnp.exp(kk * (lr * dt)[None])
    pw_re, pw_im = pmag * jnp.cos(kk * (li * dt)[None]), pmag * jnp.sin(kk * (li * dt)[None])
    ab_re = pw_re[:J, :, :, None] * bb_re[None] - pw_im[:J, :, :, None] * bb_im[None]
    ab_im = pw_re[:J, :, :, None] * bb_im[None] + pw_im[:J, :, :, None] * bb_re[None]
    rev = lambda x: jnp.stack([x[J - 1 - j] for j in range(J)])
    w_in = jnp.concatenate(
        [jnp.einsum('jgpc,gh->jgchp', rev(ab_re), eye).reshape(J * G * C, G * P),
         jnp.einsum('jgpc,gh->jgchp', rev(ab_im), eye).reshape(J * G * C, G * P)], axis=1)
    ca_re = jnp.einsum('gcp,jgp->jgcp', cr, pw_re[1:]) - jnp.einsum('gcp,jgp->jgcp', ci, pw_im[1:])
    ca_im = jnp.einsum('gcp,jgp->jgcp', cr, pw_im[1:]) + jnp.einsum('gcp,jgp->jgcp', ci, pw_re[1:])
    c_out = jnp.concatenate(
        [jnp.einsum('jgcp,gh->gpjhc', ca_re, eye).reshape(G * P, J * G * C),
         -jnp.einsum('jgcp,gh->gpjhc', ca_im, eye).reshape(G * P, J * G * C)], axis=0)
    taps = jnp.einsum('gop,kgpc->kgco', cr, ab_re) - jnp.einsum('gop,kgpc->kgco', ci, ab_im)
    none = jnp.zeros_like(taps[0])
    d_io = jnp.stack([jnp.stack([taps[j - i] if j >= i else none for j in range(J)])
                      for i in range(J)])
    d_io = jnp.einsum('ijgco,gh->igcjho', d_io, eye).reshape(J * G * C, J * G * C)
    a_grp = jnp.stack([pw_re[J].reshape(G * P), pw_im[J].reshape(G * P)])
    mag_l = jnp.exp(L * lr * dt)
    a_chunk = jnp.stack([(mag_l * jnp.cos(L * li * dt)).reshape(G * P),
                         (mag_l * jnp.sin(L * li * dt)).reshape(G * P)])
    return w_in.astype(BF16), c_out.astype(BF16), d_io.astype(BF16), a_grp, a_chunk


def _s5_kernel(u0_ref, u1_ref, w_ref, c_ref, dio_ref, a_ref, al_ref, d_ref, wg_ref, mg_ref, o_ref,
               carry_ref, x_s, st_s, y_s, bu_s):
    nb, tb, lanes = u0_ref.shape
    L, J = S5_CHUNK, S5_STEPS
    r = tb // L
    rows = nb * r
    ns = S5_GROUPS * S5_STATE

    @pl.when(pl.program_id(0) == 0)
    def _():
        carry_ref[...] = jnp.zeros_like(carry_ref)

    a_re, a_im = a_ref[0:1, :], a_ref[1:2, :]
    al_re, al_im = al_ref[0:1, :], al_ref[1:2, :]

    def inputs(q):
        parts = []
        for j in range(J):
            sl = pl.ds(q * J + j, r, stride=L)
            parts += [u0_ref[:, sl, :], u1_ref[:, sl, :]]
        return jnp.concatenate(parts, axis=-1).reshape(rows, J * 2 * lanes).astype(BF16)

    def advance(q):
        bu = bu_s[q]
        xr, xi = x_s[:, :ns], x_s[:, ns:]
        x_s[:, :ns] = a_re * xr - a_im * xi + bu[:, :ns]
        x_s[:, ns:] = a_re * xi + a_im * xr + bu[:, ns:]

    x_s[...] = jnp.zeros_like(x_s)
    for q in range(L // J):
        bu_s[q] = jnp.dot(inputs(q), w_ref[...], preferred_element_type=F32)
        advance(q)

    for b in range(nb):
        def hop(c, carry, b=b):
            xr, xi = carry
            row = pl.ds(b * r + c, 1)
            st_s[row, :] = jnp.concatenate([xr, xi], axis=1)
            p = x_s[row, :]
            return (al_re * xr - al_im * xi + p[:, :ns], al_re * xi + al_im * xr + p[:, ns:])

        xr, xi = lax.fori_loop(0, r, hop, (carry_ref[b:b + 1, :ns], carry_ref[b:b + 1, ns:]))
        carry_ref[b:b + 1, :] = jnp.concatenate([xr, xi], axis=1)

    x_s[...] = st_s[...]
    for q in range(L // J):
        y = (jnp.dot(x_s[...].astype(BF16), c_ref[...], preferred_element_type=F32)
             + jnp.dot(inputs(q), dio_ref[...], preferred_element_type=F32))
        for j in range(J):
            sl = pl.ds(q * J + j, r, stride=L)
            lo = j * 2 * lanes
            y_s[0, :, sl, :] = y[:, lo:lo + lanes].reshape(nb, r, lanes)
            y_s[1, :, sl, :] = y[:, lo + lanes:lo + 2 * lanes].reshape(nb, r, lanes)
        if q + 1 < L // J:
            advance(q)

    u = jnp.concatenate([u0_ref[...], u1_ref[...]], axis=-1).reshape(nb * tb, 2 * lanes)
    y = jnp.concatenate([y_s[0], y_s[1]], axis=-1).reshape(nb * tb, 2 * lanes)
    y = jax.nn.gelu(y + d_ref[...] * u)
    z = jnp.dot(y.astype(BF16), wg_ref[...], preferred_element_type=F32)
    out = _rms(z[:, :GROUP_W] * jax.nn.sigmoid(z[:, GROUP_W:]), mg_ref[...])
    o_ref[...] = out.reshape(nb, tb, GROUP_W)


def _s5(proj3, mats, d_skip, w_glu_bf16, merge_g, tb):
    bsz, t, _ = proj3.shape
    w_in, c_out, d_io, a_grp, a_chunk = mats
    ns = S5_GROUPS * S5_STATE
    jw = S5_STEPS * GROUP_W
    lanes = GROUP_W // 2
    rows = bsz * (tb // S5_CHUNK)
    half = lambda j: pl.BlockSpec((bsz, tb, lanes), lambda i, j=j: (0, i, 2 * COL_S5 + j))
    full = lambda a, b: pl.BlockSpec((a, b), lambda i: (0, 0))
    return pl.pallas_call(
        _s5_kernel,
        grid=(t // tb,),
        in_specs=[half(0), half(1), full(jw, 2 * ns), full(2 * ns, jw), full(jw, jw), full(2, ns),
                  full(2, ns), full(1, GROUP_W), full(GROUP_W, 2 * GROUP_W), full(1, GROUP_W)],
        out_specs=pl.BlockSpec((bsz, tb, GROUP_W), lambda i: (0, i, 0)),
        out_shape=jax.ShapeDtypeStruct((bsz, t, GROUP_W), F32),
        scratch_shapes=[pltpu.VMEM((bsz, 2 * ns), F32), pltpu.VMEM((rows, 2 * ns), F32),
                        pltpu.VMEM((rows, 2 * ns), F32), pltpu.VMEM((2, bsz, tb, lanes), F32),
                        pltpu.VMEM((S5_CHUNK // S5_STEPS, rows, 2 * ns), F32)],
        compiler_params=_cparams("arbitrary"),
        name="s5",
    )(proj3, proj3, w_in, c_out, d_io, a_grp, a_chunk, d_skip.reshape(1, -1), w_glu_bf16,
      merge_g.reshape(1, -1))


def _rwkv_kernel(r_s, k_s, v_s, lw_s, a_s, b_s, g_ref, rk_ref, lnw_ref, lnb_ref, mg_ref,
                 o_ref, h_ref, y_s, m_s, n_s, p_s, z_s, *, reset):
    tb = r_s.shape[0]
    ch = RWKV_CHUNK
    nh = GROUP_W // HEAD_W
    if reset:
        h_ref[...] = jnp.zeros_like(h_ref)
        return

    bd_bf16 = _block_diag_mask(GROUP_W, HEAD_W).astype(BF16)

    n4 = nh * ch
    ri = lax.broadcasted_iota(jnp.int32, (n4, n4), 0)
    ci = lax.broadcasted_iota(jnp.int32, (n4, n4), 1)
    same_head = (ri // ch) == (ci // ch)
    strict = same_head & ((ri % ch) > (ci % ch))
    incl = same_head & ((ri % ch) >= (ci % ch))
    eye = (ri == ci).astype(F32)
    blk0 = RWKV_INV_BLOCK
    sizes = [blk0 << i for i in range((ch // blk0).bit_length())]
    same_blk = {b: (ri // b) == (ci // b) for b in sizes}
    hm = ((lax.broadcasted_iota(jnp.int32, (n4, GROUP_W), 0) // ch)
          == (lax.broadcasted_iota(jnp.int32, (n4, GROUP_W), 1) // HEAD_W))
    tri = (lax.broadcasted_iota(jnp.int32, (ch, ch), 0)
           >= lax.broadcasted_iota(jnp.int32, (ch, ch), 1)).astype(BF16)

    def stack(x):
        return jnp.where(hm, jnp.concatenate([x] * nh, axis=0), 0.0)

    def dot(x, y, dims=_NN):
        return lax.dot_general(x.astype(BF16), y.astype(BF16), (dims, ((), ())),
                               preferred_element_type=F32)

    chunks = range(tb // ch)
    pre = []
    for c in chunks:
        sl = slice(c * ch, (c + 1) * ch)
        lw = lw_s[sl, :]
        cl = _exact_lhs_mm(tri, lw)
        cl_end = cl[ch - 1:ch, :]
        e_in, e_ex = jnp.exp(cl), jnp.exp(cl - lw)
        e_neg, e_end = jnp.exp(-cl), jnp.exp(cl_end - cl)
        av, bv, kv, rv, vv = a_s[sl, :], b_s[sl, :], k_s[sl, :], r_s[sl, :], v_s[sl, :]
        pre.append(dict(
            at4=stack(av * e_ex), rt4=stack(rv * e_in), v4=stack(vv),
            bt4=jnp.concatenate([bv * e_neg] * nh, axis=0),
            kt4=jnp.concatenate([kv * e_neg] * nh, axis=0),
            bh4=stack(bv * e_end), kh4=stack(kv * e_end), g_end=jnp.exp(cl_end)))
        yield
    for d in pre:
        ar = jnp.concatenate([d["at4"], d["rt4"]], axis=0)
        sb = dot(ar, d["bt4"], _NT)
        sk = dot(ar, d["kt4"], _NT)
        d["l_ab"] = jnp.where(strict, sb[:n4], 0.0)
        d["l_ak"] = jnp.where(strict, sk[:n4], 0.0)
        d["l_rb"] = jnp.where(incl, sb[n4:], 0.0)
        d["l_rk"] = jnp.where(incl, sk[n4:], 0.0)
        yield
    for d in pre:
        nb8 = jnp.where(same_blk[blk0], d["l_ab"], 0.0)
        d["tinv"] = eye + nb8
        d["pw"] = dot(nb8, nb8)
        yield
    for d in pre:
        d["tinv"] = d["tinv"] + dot(d["tinv"], d["pw"])
        d["pw"] = dot(d["pw"], d["pw"])
        yield
    for d in pre:
        d["tinv"] = d["tinv"] + dot(d["tinv"], d["pw"])
        yield
    blk = blk0
    while blk < ch:
        for d in pre:
            d["pw"] = dot(jnp.where(same_blk[2 * blk] & ~same_blk[blk], d["l_ab"], 0.0), d["tinv"])
            yield
        for d in pre:
            d["tinv"] = d["tinv"] + dot(d["tinv"], d["pw"])
            yield
        blk *= 2
    for d in pre:
        d["lakv"] = dot(d["l_ak"], d["v4"])
        d["lrkv"] = dot(d["l_rk"], d["v4"])
        d["khv"] = dot(d["kh4"], d["v4"], _TN)
        yield
    for d in pre:
        d["x12"] = dot(d["tinv"], jnp.concatenate([d["at4"], d["lakv"]], axis=1))
        yield
    for c, d in zip(chunks, pre):
        mn = dot(d["bh4"], d["x12"], _TN)
        pz = dot(d["l_rb"], d["x12"])
        m_s[c] = jnp.where(ri == ci, d["g_end"], 0.0) + mn[:, :GROUP_W]
        n_s[c] = mn[:, GROUP_W:] + d["khv"]
        p_s[c] = d["rt4"] + pz[:, :GROUP_W]
        z_s[c] = pz[:, GROUP_W:] + d["lrkv"]
        yield

    h = h_ref[...]
    for c in range(tb // ch):
        y4 = dot(p_s[c], h) + z_s[c]
        y = y4[0:ch]
        for j in range(1, nh):
            y = y + y4[j * ch:(j + 1) * ch]
        y_s[c * ch:(c + 1) * ch, :] = y
        h = dot(m_s[c], h) + n_s[c]
        yield
    h_ref[...] = h

    y = y_s[...]
    r, k, v = r_s[...], k_s[...], v_s[...]
    mean = _head_sum(y, bd_bf16) * (1.0 / HEAD_W)
    d = y - mean
    var = _head_sum(d * d, bd_bf16) * (1.0 / HEAD_W)
    y = d * lax.rsqrt(var + RWKV_GN_EPS) * lnw_ref[...] + lnb_ref[...]
    y = y + _head_sum(r * k * rk_ref[...], bd_bf16) * v
    o_ref[...] = _rms(y * g_ref[...], mg_ref[...])


def _rwkv_part(proj3, abg3, lyr, p, merge_g, tb):
    bsz, t, _ = proj3.shape
    row = lambda x: x.reshape(1, -1)
    abg = lambda j: _col_spec(tb, j)
    mats = pltpu.VMEM((tb // RWKV_CHUNK, GROUP_W, GROUP_W), F32)
    return dict(
        body=_rwkv_kernel, stages=15 * (tb // RWKV_CHUNK) + 1,
        args=[proj3, proj3, proj3, proj3, abg3, abg3, abg3, row(p["rwkv_r_k"][lyr]),
              row(p["rwkv_ln_w"][lyr]), row(p["rwkv_ln_b"][lyr]), row(merge_g)],
        in_specs=[_col_spec(tb, COL_R), _col_spec(tb, COL_K), _col_spec(tb, COL_V),
                  _col_spec(tb, COL_LORA), abg(EXT_A), abg(EXT_B), abg(EXT_G),
                  _row_spec(), _row_spec(), _row_spec(), _row_spec()],
        out_specs=[_out_blk(tb)], out_shape=[jax.ShapeDtypeStruct((bsz, t, GROUP_W), F32)],
        scratch=[pltpu.VMEM((GROUP_W, GROUP_W), F32), pltpu.VMEM((tb, GROUP_W), F32),
                 mats, mats, mats, mats])


def _bf16_bits(x):
    u = lax.bitcast_convert_type(x, jnp.uint32)
    r = u + jnp.uint32(0x7FFF) + ((u >> 16) & jnp.uint32(1))
    return r & jnp.uint32(0xFFFF0000)


def _slab_copies(rows_ref, slab_hbm, row0, sem):
    tm = rows_ref.shape[0]
    return [(rows_ref.at[:, pl.ds(j * LANES, LANES)], slab_hbm.at[pl.ds(row0, tm), j, :], sem)
            for j in range(slab_hbm.shape[1])]


def _merge_router_kernel(oa_ref, ob_ref, oc_ref, od_ref, h_ref, wo_ref, nw_ref, wr_ref, br_ref,
                         h1_ref, hn_hbm, ids_ref, wts_ref, cnt_ref, pk_s, sem):
    i = pl.program_id(0)
    steps = pl.num_programs(0)
    tm = h_ref.shape[0]
    slot = i % 2

    def writes(s, step):
        return [pltpu.make_async_copy(v, h, m) for v, h, m in
                _slab_copies(pk_s.at[s], hn_hbm, step * tm, sem.at[s])]

    mix = jnp.concatenate([oa_ref[...], ob_ref[...], oc_ref[...], od_ref[...]], axis=1)
    h1 = h_ref[...] + jnp.dot(mix.astype(BF16), wo_ref[...], preferred_element_type=F32)
    h1_ref[...] = h1
    hn = _rms(h1, nw_ref[...])
    half = hn.shape[1] // 2

    @pl.when(i >= 2)
    def _():
        for cp in writes(slot, i - 2):
            cp.wait()

    pk_s[slot] = _bf16_bits(hn[:, :half]) | (_bf16_bits(hn[:, half:]) >> 16)
    for cp in writes(slot, i):
        cp.start()

    @pl.when(i == steps - 1)
    def _():
        for cp in writes(slot, i):
            cp.wait()

    @pl.when((i == steps - 1) & (i >= 1))
    def _():
        for cp in writes(1 - slot, i - 1):
            cp.wait()

    logits = _dg3(hn, wr_ref[...], _NN) + br_ref[...]
    lane = lax.broadcasted_iota(jnp.int32, logits.shape, 1)
    big = jnp.int32(ROUTER_LANES)
    is_c = (lane >= N_EXPERTS) & (lane < N_EXPERTS + MOE_GROUPS)
    cm = jnp.max(jnp.where(is_c, logits, NEG_BIG), axis=-1, keepdims=True)
    gsel = jnp.min(jnp.where(is_c & (logits == cm), lane, big), axis=-1, keepdims=True) - N_EXPERTS
    p_g = 1.0 / jnp.sum(jnp.where(is_c, jnp.exp(logits - cm), 0.0), axis=-1, keepdims=True)
    lo = gsel * MOE_PER_GROUP
    in_g = (lane >= lo) & (lane < lo + MOE_PER_GROUP)
    m1 = jnp.max(jnp.where(in_g, logits, NEG_BIG), axis=-1, keepdims=True)
    i1 = jnp.min(jnp.where(in_g & (logits == m1), lane, big), axis=-1, keepdims=True)
    in_g2 = in_g & (lane != i1)
    m2 = jnp.max(jnp.where(in_g2, logits, NEG_BIG), axis=-1, keepdims=True)
    i2 = jnp.min(jnp.where(in_g2 & (logits == m2), lane, big), axis=-1, keepdims=True)
    w1 = p_g / (1.0 + jnp.exp(m2 - m1))
    w2 = p_g - w1
    two = lax.broadcasted_iota(jnp.int32, ids_ref.shape, 1)
    ids_ref[...] = jnp.where(two == 0, i1, i2)
    wts_ref[...] = jnp.where(two == 0, w1, w2)

    @pl.when(pl.program_id(0) == 0)
    def _():
        cnt_ref[...] = jnp.zeros_like(cnt_ref)

    cnt_ref[...] += jnp.sum(((lane == i1) | (lane == i2)).astype(F32), axis=0, keepdims=True)


def _merge_router(outs, h2d, w_out_bf16, norm_w, w_router, b_router, tm):
    n, d = h2d.shape
    grp = pl.BlockSpec((tm, GROUP_W), lambda i: (i, 0))
    full = lambda a, b: pl.BlockSpec((a, b), lambda i: (0, 0))
    rowblk = lambda w: pl.BlockSpec((tm, w), lambda i: (i, 0))
    return pl.pallas_call(
        _merge_router_kernel,
        grid=(n // tm,),
        in_specs=[grp, grp, grp, grp, rowblk(d), full(d, d), full(1, d), full(d, ROUTER_LANES),
                  full(1, ROUTER_LANES)],
        out_specs=[rowblk(d), pl.BlockSpec(memory_space=pl.ANY),
                   rowblk(MOE_TOPK), rowblk(MOE_TOPK), full(1, ROUTER_LANES)],
        out_shape=[jax.ShapeDtypeStruct((n, d), F32),
                   jax.ShapeDtypeStruct((n, d // 2 // LANES, LANES), jnp.uint32),
                   jax.ShapeDtypeStruct((n, MOE_TOPK), jnp.int32),
                   jax.ShapeDtypeStruct((n, MOE_TOPK), F32),
                   jax.ShapeDtypeStruct((1, ROUTER_LANES), F32)],
        scratch_shapes=[pltpu.VMEM((2, tm, d // 2), jnp.uint32), pltpu.SemaphoreType.DMA((2,))],
        compiler_params=_cparams("arbitrary"),
        name="merge_router",
    )(*[o.reshape(n, GROUP_W) for o in outs], h2d, w_out_bf16, norm_w.reshape(1, d), w_router, b_router)


def _row_copy(src_hbm, src_row, dst_ref, dst_row, sem):
    return pltpu.make_async_copy(src_hbm.at[pl.ds(src_row, 1), :], dst_ref.at[pl.ds(dst_row, 1), :], sem)


def _position_kernel(ids_ref, base_ref, pos_ref, run_ref):
    tm = ids_ref.shape[0]

    @pl.when(pl.program_id(0) == 0)
    def _():
        run_ref[...] = jnp.zeros_like(run_ref)

    i1, i2 = ids_ref[:, 0:1], ids_ref[:, 1:2]
    lane = lax.broadcasted_iota(jnp.int32, (tm, ROUTER_LANES), 1)
    hit = ((lane == i1) | (lane == i2)).astype(BF16)
    earlier = (lax.broadcasted_iota(jnp.int32, (tm, tm), 0)
               > lax.broadcasted_iota(jnp.int32, (tm, tm), 1)).astype(BF16)
    rank = jnp.dot(earlier, hit, preferred_element_type=F32) + run_ref[...]
    where_to = rank + base_ref[...]
    p1 = jnp.sum(jnp.where(lane == i1, where_to, 0.0), axis=-1, keepdims=True)
    p2 = jnp.sum(jnp.where(lane == i2, where_to, 0.0), axis=-1, keepdims=True)
    two = lax.broadcasted_iota(jnp.int32, pos_ref.shape, 1)
    pos_ref[...] = jnp.where(two == 0, p1, p2).astype(jnp.int32)
    run_ref[...] += jnp.sum(hit.astype(F32), axis=0, keepdims=True)


def _positions(ids, group_start, tm):
    n = ids.shape[0]
    return pl.pallas_call(
        _position_kernel,
        grid=(n // tm,),
        in_specs=[pl.BlockSpec((tm, MOE_TOPK), lambda i: (i, 0)),
                  pl.BlockSpec((1, ROUTER_LANES), lambda i: (0, 0))],
        out_specs=pl.BlockSpec((tm, MOE_TOPK), lambda i: (i, 0)),
        out_shape=jax.ShapeDtypeStruct((n, MOE_TOPK), jnp.int32),
        scratch_shapes=[pltpu.VMEM((1, ROUTER_LANES), F32)],
        compiler_params=_cparams("arbitrary"),
        name="moe_positions",
    )(ids, group_start)


def _dispatch_kernel(p0_ref, p1_ref, x_ref, xs_in_hbm, xs_hbm, sem):
    del xs_in_hbm
    n = x_ref.shape[0]

    def copies(i):
        return (pltpu.make_async_copy(x_ref.at[i], xs_hbm.at[p0_ref[i]], sem),
                pltpu.make_async_copy(x_ref.at[i], xs_hbm.at[p1_ref[i]], sem))

    def issue(i, c):
        for prio, cp in enumerate(copies(i)):
            cp.start(priority=prio)
        return c

    def drain(i, c):
        for cp in copies(i):
            cp.wait()
        return c

    lax.fori_loop(0, n, issue, 0, unroll=8)
    lax.fori_loop(0, n, drain, 0, unroll=8)


def _dispatch(x, pos0, pos1, n_rows, chunk):
    n, slabs, lanes = x.shape
    smem = lambda: pl.BlockSpec((chunk,), lambda s: (s,), memory_space=pltpu.SMEM)
    return pl.pallas_call(
        _dispatch_kernel,
        grid=(n // chunk,),
        in_specs=[smem(), smem(), pl.BlockSpec((chunk, slabs, lanes), lambda s: (s, 0, 0)),
                  pl.BlockSpec(memory_space=pl.ANY)],
        out_specs=pl.BlockSpec(memory_space=pl.ANY),
        out_shape=jax.ShapeDtypeStruct((n_rows, slabs, lanes), x.dtype),
        input_output_aliases={3: 0},
        scratch_shapes=[pltpu.SemaphoreType.DMA(())],
        compiler_params=_cparams("arbitrary"),
        name="moe_dispatch",
    )(pos0, pos1, x, jnp.zeros((n_rows, slabs, lanes), x.dtype))


def _expert_kernel(te_ref, nt_ref, xs_hbm, wg_ref, wu_ref, wd_ref, o_ref, wg_s, wu_s, wd_s, x_s, sem):
    i = pl.program_id(0)
    tm = o_ref.shape[0]
    slot = i % 2

    def reads(s, tile):
        return [pltpu.make_async_copy(h, v, m) for v, h, m in
                _slab_copies(x_s.at[s], xs_hbm, tile * tm, sem.at[s])]

    @pl.when(i == 0)
    def _():
        for cp in reads(0, 0):
            cp.start()

    @pl.when(i + 1 < pl.num_programs(0))
    def _():
        for cp in reads(1 - slot, i + 1):
            cp.start()

    @pl.when((i == 0) | (te_ref[i] != te_ref[jnp.maximum(i - 1, 0)]))
    def _():
        wg_s[...] = wg_ref[...].astype(BF16)
        wu_s[...] = wu_ref[...].astype(BF16)
        wd_s[...] = wd_ref[...].astype(BF16)

    for cp in reads(slot, i):
        cp.wait()

    @pl.when(i < nt_ref[0])
    def _():
        p = x_s[slot]
        xa = lax.bitcast_convert_type(p & jnp.uint32(0xFFFF0000), F32).astype(BF16)
        xb = lax.bitcast_convert_type(p << 16, F32).astype(BF16)
        x = jnp.concatenate([xa, xb], axis=1)
        he = (_silu(jnp.dot(x, wg_s[...], preferred_element_type=F32))
              * jnp.dot(x, wu_s[...], preferred_element_type=F32))
        o_ref[...] = jnp.dot(he.astype(BF16), wd_s[...], preferred_element_type=F32)

    @pl.when(i >= nt_ref[0])
    def _():
        o_ref[...] = jnp.zeros_like(o_ref)


def _experts(xs, tile_expert, n_tiles_used, wg, wu, wd, layer, tm):
    p, slabs, lanes = xs.shape
    d = 2 * slabs * lanes
    pick = lambda i, te, nt: (layer, te[i], 0, 0)
    grid_spec = pltpu.PrefetchScalarGridSpec(
        num_scalar_prefetch=2,
        grid=(p // tm,),
        in_specs=[pl.BlockSpec(memory_space=pl.ANY),
                  pl.BlockSpec((None, None, d, D_EXPERT), pick),
                  pl.BlockSpec((None, None, d, D_EXPERT), pick),
                  pl.BlockSpec((None, None, D_EXPERT, d), pick)],
        out_specs=pl.BlockSpec((tm, d), lambda i, te, nt: (i, 0)),
        scratch_shapes=[pltpu.VMEM((d, D_EXPERT), BF16), pltpu.VMEM((d, D_EXPERT), BF16),
                        pltpu.VMEM((D_EXPERT, d), BF16), pltpu.VMEM((2, tm, slabs * lanes), xs.dtype),
                        pltpu.SemaphoreType.DMA((2,))],
    )
    return pl.pallas_call(
        _expert_kernel,
        grid_spec=grid_spec,
        out_shape=jax.ShapeDtypeStruct((p, d), F32),
        compiler_params=_cparams("arbitrary"),
        name="moe_experts",
    )(tile_expert, n_tiles_used, xs, wg, wu, wd)


def _combine_kernel(i0_ref, i1_ref, h1_ref, w_ref, fw_ref, ys_hbm, o_ref, buf0, buf1, sem, *, final_norm):
    n = o_ref.shape[0]

    def issue(i, c):
        _row_copy(ys_hbm, i0_ref[i], buf0, i, sem).start(priority=0)
        _row_copy(ys_hbm, i1_ref[i], buf1, i, sem).start(priority=1)
        return c

    def drain(i, c):
        _row_copy(ys_hbm, 0, buf0, i, sem).wait()
        _row_copy(ys_hbm, 0, buf1, i, sem).wait()
        return c

    lax.fori_loop(0, n, issue, 0, unroll=8)
    lax.fori_loop(0, n, drain, 0, unroll=8)
    out = h1_ref[...] + w_ref[:, 0:1] * buf0[...] + w_ref[:, 1:2] * buf1[...]
    if final_norm:
        out = _rms(out, fw_ref[...])
    o_ref[...] = out


def _combine(ys, pos0, pos1, wts, h1, final_w, final_norm, chunk):
    n, d = h1.shape
    smem = lambda: pl.BlockSpec((chunk,), lambda s: (s,), memory_space=pltpu.SMEM)
    return pl.pallas_call(
        functools.partial(_combine_kernel, final_norm=final_norm),
        grid=(n // chunk,),
        in_specs=[smem(), smem(), pl.BlockSpec((chunk, d), lambda s: (s, 0)),
                  pl.BlockSpec((chunk, MOE_TOPK), lambda s: (s, 0)),
                  pl.BlockSpec((1, d), lambda s: (0, 0)), pl.BlockSpec(memory_space=pl.ANY)],
        out_specs=pl.BlockSpec((chunk, d), lambda s: (s, 0)),
        out_shape=jax.ShapeDtypeStruct((n, d), F32),
        scratch_shapes=[pltpu.VMEM((chunk, d), F32), pltpu.VMEM((chunk, d), F32),
                        pltpu.SemaphoreType.DMA(())],
        compiler_params=_cparams("arbitrary"),
        name="moe_combine",
    )(pos0, pos1, h1, wts, final_w.reshape(1, d), ys)


def _moe(hn_packed, ids, wts, counts, h1, wg, wu, wd, layer, final_w, final_norm, tm):
    n = h1.shape[0]
    n_tiles = (n * MOE_TOPK) // tm + N_EXPERTS
    cnt = counts[0, :N_EXPERTS].astype(jnp.int32)
    padded = ((cnt + tm - 1) // tm) * tm
    ends = jnp.cumsum(padded)
    tile_start = jnp.arange(n_tiles, dtype=jnp.int32) * tm
    tile_expert = jnp.minimum(jnp.sum(ends[None, :] <= tile_start[:, None], axis=1),
                              N_EXPERTS - 1).astype(jnp.int32)
    n_tiles_used = (ends[-1] // tm).astype(jnp.int32).reshape(1)
    group_start = jnp.zeros((1, ROUTER_LANES), F32).at[0, :N_EXPERTS].set((ends - padded).astype(F32))
    pos = _positions(ids, group_start, _pick_div(n, 512))
    pos0, pos1 = pos[:, 0], pos[:, 1]
    xs = _dispatch(hn_packed, pos0, pos1, n_tiles * tm, _pick_div(n, 1024))
    ys = _experts(xs, tile_expert, n_tiles_used, wg, wu, wd, layer, tm)
    return _combine(ys, pos0, pos1, wts, h1, final_w, final_norm, _pick_div(n, 1024))


def _pick_div(n, pref):
    while n % pref:
        pref //= 2
    return pref


def _pick(n, pref):
    return pref if n % pref == 0 else n


def kernel(x, norm1_w, w_in, hgrn_lb_logits, hgrn_norm_w, s5_lambda_re, s5_lambda_im, s5_log_dt, s5_b_re, s5_b_im, s5_c_re, s5_c_im, s5_d, s5_w_glu, rwkv_mu, rwkv_w0, rwkv_w2, rwkv_a0, rwkv_a2, rwkv_g2, rwkv_k_k, rwkv_k_a, rwkv_r_k, rwkv_v0, rwkv_v1, rwkv_v2, rwkv_ln_w, rwkv_ln_b, lru_conv_w, lru_conv_b, lru_wa, lru_ba, lru_wx, lru_bx, lru_lambda, merge_gain, w_out, norm2_w, moe_coarse_w, moe_coarse_b, moe_fine_w, moe_fine_b, moe_w_gate, moe_w_up, moe_w_down, final_norm_w):
    bsz, t, d = x.shape
    n = bsz * t
    depth = w_in.shape[0]
    rw = dict(rwkv_mu=rwkv_mu, rwkv_w0=rwkv_w0, rwkv_w2=rwkv_w2, rwkv_a0=rwkv_a0, rwkv_a2=rwkv_a2,
              rwkv_g2=rwkv_g2, rwkv_k_k=rwkv_k_k, rwkv_k_a=rwkv_k_a, rwkv_r_k=rwkv_r_k,
              rwkv_v0=rwkv_v0, rwkv_v1=rwkv_v1, rwkv_v2=rwkv_v2, rwkv_ln_w=rwkv_ln_w,
              rwkv_ln_b=rwkv_ln_b)
    lb_all = jnp.cumsum(jax.nn.softmax(hgrn_lb_logits.astype(F32), axis=0), axis=0)
    lb_all = lb_all - lb_all[:1]

    tm_proj = _pick(n, 1024)
    tm_in = _pick_div(t, 512)
    tm_moe = 512
    tb_mix = _pick(t, 256)
    tb_s5 = _pick(t, 512)

    h = x.reshape(n, d)
    v_first = None
    for l in range(depth):
        mg = merge_gain[l].reshape(4, GROUP_W)
        proj, abg, v_first = _in_proj(h, norm1_w[l], w_in[l].astype(BF16), v_first, l, rw, t, tm_in)
        proj3 = proj.reshape(bsz, t, -1)
        abg3 = abg.reshape(bsz, t, -1)
        mats = _s5_matrices(s5_lambda_re[l], s5_lambda_im[l], s5_log_dt[l], s5_b_re[l], s5_b_im[l],
                            s5_c_re[l], s5_c_im[l])
        o_b = _s5(proj3, mats, s5_d[l], s5_w_glu[l].astype(BF16), mg[1], tb_s5)
        parts = [_rwkv_part(proj3, abg3, l, rw, mg[2], tb_mix),
                 _hgrn_part(proj3, lb_all[l], hgrn_norm_w[l], mg[0], tb_mix),
                 _lru_part(proj3, lru_conv_w[l], lru_conv_b[l], _block_diag_weight(lru_wa[l]),
                           lru_ba[l], _block_diag_weight(lru_wx[l]), lru_bx[l], lru_lambda[l],
                           mg[3], tb_mix)]
        outs = _mixers(parts, bsz, t // tb_mix, "mixers")
        o_c, o_a, o_d = outs
        w_router = jnp.concatenate(
            [moe_fine_w[l].transpose(1, 0, 2).reshape(d, N_EXPERTS), moe_coarse_w[l],
             jnp.zeros((d, ROUTER_LANES - N_EXPERTS - MOE_GROUPS), F32)], axis=1)
        b_router = jnp.concatenate(
            [moe_fine_b[l].reshape(N_EXPERTS), moe_coarse_b[l],
             jnp.zeros((ROUTER_LANES - N_EXPERTS - MOE_GROUPS,), F32)]).reshape(1, ROUTER_LANES)
        h1, hn, ids, wts, counts = _merge_router((o_a, o_b, o_c, o_d), h, w_out[l].astype(BF16),
                                                 norm2_w[l], w_router, b_router, tm_proj)
        h = _moe(hn, ids, wts, counts, h1, moe_w_gate, moe_w_up, moe_w_down, l,
                 final_norm_w, l == depth - 1, tm_moe)
    return h.reshape(bsz, t, d)
```

```python
import functools

import jax
import jax.numpy as jnp
from jax import lax
from jax.experimental import pallas as pl
from jax.experimental.pallas import tpu as pltpu

F32 = jnp.float32
BF16 = jnp.bfloat16

GROUP_W = 256
RMS_EPS = 1e-6
HEAD_W = 64
HGRN_CHUNK = 32
S5_CH = 16
S5_GROUPS = GROUP_W // S5_CH
S5_STATE = 64
S5_CHUNK = 16
S5_STEPS = 4
RWKV_CHUNK = 64
RWKV_INV_BLOCK = 8
RWKV_GN_EPS = 64e-5
RWKV_W_LORA = 64
RWKV_V_LORA = 32
LRU_CONV = 4
LRU_C = 8.0
MOE_GROUPS = 4
MOE_PER_GROUP = 8
N_EXPERTS = MOE_GROUPS * MOE_PER_GROUP
MOE_TOPK = 2
D_EXPERT = 512
LANES = 128
SUBLANES = 8
ROUTER_LANES = LANES
NEG_BIG = -1e30
V7X_VMEM_BYTES = 64 * 1024 * 1024
VMEM_LIMIT = V7X_VMEM_BYTES - 8 * 1024 * 1024

COL_HQ, COL_HF, COL_HI, COL_HG, COL_S5, COL_R, COL_K, COL_V, COL_LORA, COL_LG, COL_LX = range(11)
EXT_A, EXT_B, EXT_G = range(3)


def _cparams(*sem):
    return pltpu.CompilerParams(dimension_semantics=sem, vmem_limit_bytes=VMEM_LIMIT)


def _mm(a, b):
    return jnp.dot(a.astype(BF16), b.astype(BF16), preferred_element_type=F32)


def _split2(x):
    hi = x.astype(BF16)
    lo = (x - hi.astype(F32)).astype(BF16)
    return hi, lo


def _dg3(a, b, dims):
    ah, al = _split2(a)
    bh, bl = _split2(b)
    d = lambda x, y: lax.dot_general(x, y, (dims, ((), ())), preferred_element_type=F32)
    return d(ah, bh) + d(ah, bl) + d(al, bh)


_NN = ((1,), (0,))
_NT = ((1,), (1,))
_TN = ((0,), (0,))


def _exact_lhs_mm(m_bf16, x):
    h1 = x.astype(BF16)
    r1 = x - h1.astype(F32)
    h2 = r1.astype(BF16)
    h3 = (r1 - h2.astype(F32)).astype(BF16)
    d = lambda y: jnp.dot(m_bf16, y, preferred_element_type=F32)
    return d(h1) + d(h2) + d(h3)


def _head_sum(x, bd_bf16):
    return jnp.dot(x.astype(BF16), bd_bf16, preferred_element_type=F32)


def _block_diag_mask(n, blk):
    r = lax.broadcasted_iota(jnp.int32, (n, n), 0) // blk
    c = lax.broadcasted_iota(jnp.int32, (n, n), 1) // blk
    return r == c


def _rms(x, w):
    return x * lax.rsqrt(jnp.mean(x * x, axis=-1, keepdims=True) + RMS_EPS) * w


def _silu(x):
    return x * jax.nn.sigmoid(x)


def _softplus(x):
    return jnp.maximum(x, 0.0) + jnp.log(1.0 + jnp.exp(-jnp.abs(x)))


def _in_proj_kernel(*refs, has_vmix, tiles_per_seq):
    if has_vmix:
        (x_ref, nw_ref, w_ref, mu_ref, w0_ref, w2_ref, a0_ref, a2_ref, g2_ref, kk_ref, ka_ref,
         vf_ref, v0_ref, v1_ref, v2_ref, o_ref, o2_ref, prev_s) = refs
    else:
        (x_ref, nw_ref, w_ref, mu_ref, w0_ref, w2_ref, a0_ref, a2_ref, g2_ref, kk_ref, ka_ref,
         o_ref, o2_ref, vf_out_ref, prev_s) = refs
    tm = x_ref.shape[0]
    gw = GROUP_W
    c0, c1 = COL_R * gw, (COL_LORA + 1) * gw
    @pl.when((pl.program_id(0) % tiles_per_seq) == 0)
    def _():
        prev_s[...] = jnp.zeros_like(prev_s)

    y = _rms(x_ref[...], nw_ref[...]).astype(BF16)
    raw = jnp.dot(y, w_ref[:, c0:c1], preferred_element_type=F32)

    row0 = lax.broadcasted_iota(jnp.int32, (tm, 1), 0) == 0
    sh = jnp.where(row0, prev_s[...], pltpu.roll(raw, 1, axis=0))
    prev_s[...] = raw[tm - 1:tm, :]
    pf = raw + mu_ref[...] * (sh - raw)
    r, k, v, lo = (pf[:, j * GROUP_W:(j + 1) * GROUP_W] for j in range(4))

    o_ref[:, :c0] = jnp.dot(y, w_ref[:, :c0], preferred_element_type=F32)
    bd_bf16 = _block_diag_mask(GROUP_W, HEAD_W).astype(BF16)
    lo_wa, lo_g = lo[:, :LANES], lo[:, LANES:]
    w_raw = -_softplus(-(w0_ref[...] + _mm(jnp.tanh(lo_wa), w2_ref[...]))) - 0.5
    alr = jax.nn.sigmoid(a0_ref[...] + _mm(lo_wa, a2_ref[...]))
    o_ref[:, c1:] = jnp.dot(y, w_ref[:, c1:], preferred_element_type=F32)
    kkr = k * kk_ref[...]
    kk = kkr / jnp.maximum(jnp.sqrt(_head_sum(kkr * kkr, bd_bf16)), 1e-12)
    if has_vmix:
        gate = jax.nn.sigmoid(v0_ref[...] + _mm(_mm(v, v1_ref[...]), v2_ref[...]))
        v = v + (vf_ref[...] - v) * gate
    else:
        vf_out_ref[...] = v
    o_ref[:, c0:c0 + GROUP_W] = r
    o_ref[:, c0 + GROUP_W:c0 + 2 * GROUP_W] = k * (1.0 + (alr - 1.0) * ka_ref[...])
    o_ref[:, c0 + 2 * GROUP_W:c0 + 3 * GROUP_W] = v
    o_ref[:, c0 + 3 * GROUP_W:c1] = -jnp.exp(w_raw)
    o2_ref[:, EXT_A * gw:(EXT_A + 1) * gw] = -kk
    o2_ref[:, EXT_B * gw:(EXT_B + 1) * gw] = kk * alr
    o2_ref[:, EXT_G * gw:(EXT_G + 1) * gw] = _mm(jax.nn.sigmoid(lo_g), g2_ref[...])


def _pad_rows(w, start, total=LANES):
    out = jnp.zeros((total, w.shape[1]), F32).at[start:start + w.shape[0]].set(w)
    return out.astype(BF16)


def _in_proj(h2d, norm_w, w_bf16, v_first, lyr, p, t, tm):
    n, d = h2d.shape
    d_in = w_bf16.shape[1]
    has_vmix = v_first is not None
    row = lambda x: x.reshape(1, -1)
    full = lambda a, b: pl.BlockSpec((a, b), lambda i: (0, 0))
    rowblk = lambda w: pl.BlockSpec((tm, w), lambda i: (i, 0))
    args = [h2d, row(norm_w), w_bf16, row(p["rwkv_mu"][lyr]), row(p["rwkv_w0"][lyr]),
            _pad_rows(p["rwkv_w2"][lyr], 0), row(p["rwkv_a0"][lyr]),
            _pad_rows(p["rwkv_a2"][lyr], RWKV_W_LORA), p["rwkv_g2"][lyr].astype(BF16),
            row(p["rwkv_k_k"][lyr]), row(p["rwkv_k_a"][lyr])]
    specs = [rowblk(d), full(1, d), full(d, d_in), full(1, 4 * GROUP_W), full(1, GROUP_W),
             full(LANES, GROUP_W), full(1, GROUP_W), full(LANES, GROUP_W), full(LANES, GROUP_W),
             full(1, GROUP_W), full(1, GROUP_W)]
    out_specs = [rowblk(d_in), rowblk(3 * GROUP_W)]
    out_shape = [jax.ShapeDtypeStruct((n, d_in), F32), jax.ShapeDtypeStruct((n, 3 * GROUP_W), F32)]
    if has_vmix:
        v1 = jnp.zeros((GROUP_W, LANES), F32).at[:, :RWKV_V_LORA].set(p["rwkv_v1"][lyr - 1]).astype(BF16)
        v2 = jnp.zeros((LANES, GROUP_W), F32).at[:RWKV_V_LORA].set(p["rwkv_v2"][lyr - 1]).astype(BF16)
        args += [v_first, row(p["rwkv_v0"][lyr - 1]), v1, v2]
        specs += [rowblk(GROUP_W), full(1, GROUP_W), full(GROUP_W, LANES), full(LANES, GROUP_W)]
    else:
        out_specs.append(rowblk(GROUP_W))
        out_shape.append(jax.ShapeDtypeStruct((n, GROUP_W), F32))
    res = pl.pallas_call(
        functools.partial(_in_proj_kernel, has_vmix=has_vmix, tiles_per_seq=t // tm),
        grid=(n // tm,),
        in_specs=specs,
        out_specs=out_specs,
        out_shape=out_shape,
        scratch_shapes=[pltpu.VMEM((1, 4 * GROUP_W), F32)],
        compiler_params=_cparams("arbitrary"),
        name="in_proj",
    )(*args)
    return res[0], res[1], (v_first if has_vmix else res[2])


def _col_spec(tb, col):
    return pl.BlockSpec((None, tb, GROUP_W), lambda b, t: (b, t, col))


def _row_spec(width=GROUP_W):
    return pl.BlockSpec((1, width), lambda b, t: (0, 0))


def _full_spec(shape):
    return pl.BlockSpec(shape, lambda b, t: (0,) * len(shape))


def _hgrn_kernel(q_ref, f_ref, i_ref, g_ref, lb_ref, nw_ref, mg_ref, o_ref,
                 st_ref, q_s, k_s, v_s, lf_s, o_s, *, reset):
    ch = HGRN_CHUNK
    tb = q_ref.shape[0]
    if reset:
        st_ref[...] = jnp.zeros_like(st_ref)
        return

    lb = lb_ref[...]
    fx = f_ref[...]
    x1 = jnp.log(lb)
    x2 = jnp.log(1.0 - lb) - _softplus(-fx)
    m = jnp.maximum(x1, x2)
    lf_s[...] = m + jnp.log(jnp.exp(x1 - m) + jnp.exp(x2 - m))
    q_s[...] = _silu(q_ref[...])
    k_s[...] = (1.0 - lb) * jax.nn.sigmoid(-fx)
    v_s[...] = _silu(i_ref[...])
    yield

    bd = _block_diag_mask(GROUP_W, HEAD_W)
    bd_bf16 = bd.astype(BF16)
    tri = (lax.broadcasted_iota(jnp.int32, (ch, ch), 0)
           >= lax.broadcasted_iota(jnp.int32, (ch, ch), 1))
    tri_bf16 = tri.astype(BF16)
    tri3 = (lax.broadcasted_iota(jnp.int32, (ch, ch, 1), 0)
            <= lax.broadcasted_iota(jnp.int32, (ch, ch, 1), 1))

    st = st_ref[...]
    for c in range(tb // ch):
        sl = slice(c * ch, (c + 1) * ch)
        qc, kc, vc = q_s[sl, :], k_s[sl, :], v_s[sl, :]
        b = _exact_lhs_mm(tri_bf16, lf_s[sl, :])
        rel = b[None, :, :] - b[:, None, :]
        dec = jnp.exp(jnp.where(tri3, rel, NEG_BIG))
        p = (qc[None, :, :] * kc[:, None, :]) * dec
        sc = jnp.dot(p.reshape(ch * ch, GROUP_W).astype(BF16), bd_bf16,
                     preferred_element_type=F32).reshape(ch, ch, GROUP_W)
        o_intra = jnp.sum(sc * vc[:, None, :], axis=0)
        o_inter = lax.dot_general((qc * jnp.exp(b)).astype(BF16), st.astype(BF16),
                                  (_NT, ((), ())), preferred_element_type=F32)
        b_end = b[ch - 1:ch, :]
        kh = kc * jnp.exp(b_end - b)
        upd = lax.dot_general(vc.astype(BF16), kh.astype(BF16), (_TN, ((), ())),
                              preferred_element_type=F32)
        st = st * jnp.exp(b_end) + jnp.where(bd, upd, 0.0)
        o_s[sl, :] = o_intra + o_inter
        yield
    st_ref[...] = st

    o = o_s[...]
    ms = _head_sum(o * o, bd_bf16) * (1.0 / HEAD_W)
    o = o * lax.rsqrt(ms + RMS_EPS) * nw_ref[...] * _silu(g_ref[...])
    o_ref[...] = _rms(o, mg_ref[...])


def _out_blk(tb):
    return pl.BlockSpec((None, tb, GROUP_W), lambda b, i: (b, i, 0))


def _hgrn_part(proj3, lb, norm_w, merge_g, tb):
    bsz, t, _ = proj3.shape
    blk = pltpu.VMEM((tb, GROUP_W), F32)
    return dict(
        body=_hgrn_kernel, stages=tb // HGRN_CHUNK + 2,
        args=[proj3, proj3, proj3, proj3, lb.reshape(1, -1), norm_w.reshape(1, -1), merge_g.reshape(1, -1)],
        in_specs=[_col_spec(tb, COL_HQ), _col_spec(tb, COL_HF), _col_spec(tb, COL_HI),
                  _col_spec(tb, COL_HG), _row_spec(), _row_spec(), _row_spec()],
        out_specs=[_out_blk(tb)],
        out_shape=[jax.ShapeDtypeStruct((bsz, t, GROUP_W), F32)],
        scratch=[pltpu.VMEM((GROUP_W, GROUP_W), F32), blk, blk, blk, blk, blk])


def _mixer_kernel(*refs, parts):
    groups, i = [], 0
    for kind in range(3):
        for body, counts in parts:
            groups.append(refs[i:i + counts[kind]])
            i += counts[kind]
    k = len(parts)
    per_part = [groups[j] + groups[k + j] + groups[2 * k + j] for j in range(k)]

    @pl.when(pl.program_id(1) == 0)
    def _():
        for (body, _), r in zip(parts, per_part):
            for _step in body(*r, reset=True):
                pass

    runs = [[body(*r, reset=False), 0, counts[3]] for (body, counts), r in zip(parts, per_part)]
    while runs:
        run = min(runs, key=lambda x: x[1] / x[2])
        try:
            next(run[0])
            run[1] += 1
        except StopIteration:
            runs.remove(run)


def _mixers(parts, bsz, n_tblocks, name):
    cat = lambda key: [x for p in parts for x in p[key]]
    light = tuple((p["body"], (len(p["args"]), len(p["out_specs"]), len(p["scratch"]), p["stages"]))
                  for p in parts)
    return pl.pallas_call(
        functools.partial(_mixer_kernel, parts=light),
        grid=(bsz, n_tblocks),
        in_specs=cat("in_specs"),
        out_specs=cat("out_specs"),
        out_shape=cat("out_shape"),
        scratch_shapes=cat("scratch"),
        compiler_params=_cparams("parallel", "arbitrary"),
        name=name,
    )(*cat("args"))


def _lru_kernel(xg_ref, xr_ref, cw_ref, cb_ref, wa_ref, ba_ref, wx_ref, bx_ref, lam_ref, mg_ref,
                o_ref, buf_ref, h_ref, *, reset):
    tb = xr_ref.shape[0]
    pad = SUBLANES
    if reset:
        buf_ref[0:pad, :] = jnp.zeros((pad, GROUP_W), F32)
        h_ref[...] = jnp.zeros_like(h_ref)
        return

    xr = xr_ref[...]
    buf_ref[pad:pad + tb, :] = xr
    xc = cb_ref[...] + jnp.zeros_like(xr)
    for j in range(LRU_CONV):
        xc = xc + cw_ref[j:j + 1, :] * buf_ref[pl.ds(pad - (LRU_CONV - 1) + j, tb), :]
    buf_ref[0:pad, :] = xr[tb - pad:tb, :]

    r = jax.nn.sigmoid(jnp.dot(xc.astype(BF16), wa_ref[...], preferred_element_type=F32) + ba_ref[...])
    gi = jax.nn.sigmoid(jnp.dot(xc.astype(BF16), wx_ref[...], preferred_element_type=F32) + bx_ref[...])
    log_a = -LRU_C * r * _softplus(-lam_ref[...])
    a = jnp.exp(log_a)
    x = jnp.sqrt(1.0 - jnp.exp(2.0 * log_a)) * (gi * xc)
    yield

    rows = lax.broadcasted_iota(jnp.int32, (tb, 1), 0)
    k = 1
    while k < tb:
        keep = rows >= k
        x = x + jnp.where(keep, a * pltpu.roll(x, k, axis=0), 0.0)
        a = jnp.where(keep, a * pltpu.roll(a, k, axis=0), a)
        k *= 2
        yield
    h = x + a * h_ref[...]
    h_ref[...] = h[tb - 1:tb, :]
    o_ref[...] = _rms(jax.nn.gelu(xg_ref[...]) * h, mg_ref[...])


def _lru_part(proj3, conv_w, conv_b, wa_bd, ba, wx_bd, bx, lam, merge_g, tb):
    bsz, t, _ = proj3.shape
    return dict(
        body=_lru_kernel, stages=tb.bit_length() + 1,
        args=[proj3, proj3, conv_w, conv_b.reshape(1, -1), wa_bd, ba.reshape(1, -1), wx_bd,
              bx.reshape(1, -1), lam.reshape(1, -1), merge_g.reshape(1, -1)],
        in_specs=[_col_spec(tb, COL_LG), _col_spec(tb, COL_LX), _full_spec((LRU_CONV, GROUP_W)),
                  _row_spec(), _full_spec((GROUP_W, GROUP_W)), _row_spec(),
                  _full_spec((GROUP_W, GROUP_W)), _row_spec(), _row_spec(), _row_spec()],
        out_specs=[_out_blk(tb)],
        out_shape=[jax.ShapeDtypeStruct((bsz, t, GROUP_W), F32)],
        scratch=[pltpu.VMEM((tb + SUBLANES, GROUP_W), F32), pltpu.VMEM((1, GROUP_W), F32)])


def _block_diag_weight(w):
    h, n, _ = w.shape
    eye = jnp.eye(h, dtype=w.dtype)
    return jnp.einsum('hij,hg->higj', w, eye).reshape(h * n, h * n).astype(BF16)


def _s5_matrices(lam_re, lam_im, log_dt, b_re, b_im, c_re, c_im):
    L, G, P, C = S5_CHUNK, S5_GROUPS, S5_STATE, S5_CH
    lr, li = lam_re.astype(F32), lam_im.astype(F32)
    dt = jnp.exp(log_dt.astype(F32))[:, None]
    mag = jnp.exp(lr * dt)
    a_re, a_im = mag * jnp.cos(li * dt), mag * jnp.sin(li * dt)
    den = lr * lr + li * li
    kap_re = ((a_re - 1.0) * lr + a_im * li) / den
    kap_im = (a_im * lr - (a_re - 1.0) * li) / den
    br, bi = b_re.astype(F32), b_im.astype(F32)
    bb_re = kap_re[..., None] * br - kap_im[..., None] * bi
    bb_im = kap_re[..., None] * bi + kap_im[..., None] * br
    cr, ci = c_re.astype(F32), c_im.astype(F32)
    eye = jnp.eye(G, dtype=F32)
    J = S5_STEPS
    kk = jnp.arange(J + 1, dtype=F32)[:, None, None]
    pmag = jnp.exp(kk * (lr * dt)[None])
    pw_re, pw_im = pmag * jnp.cos(kk * (li * dt)[None]), pmag * jnp.sin(kk * (li * dt)[None])
    ab_re = pw_re[:J, :, :, None] * bb_re[None] - pw_im[:J, :, :, None] * bb_im[None]
    ab_im = pw_re[:J, :, :, None] * bb_im[None] + pw_im[:J, :, :, None] * bb_re[None]
    rev = lambda x: jnp.stack([x[J - 1 - j] for j in range(J)])
    w_in = jnp.concatenate(
        [jnp.einsum('jgpc,gh->jgchp', rev(ab_re), eye).reshape(J * G * C, G * P),
         jnp.einsum('jgpc,gh->jgchp', rev(ab_im), eye).reshape(J * G * C, G * P)], axis=1)
    ca_re = jnp.einsum('gcp,jgp->jgcp', cr, pw_re[1:]) - jnp.einsum('gcp,jgp->jgcp', ci, pw_im[1:])
    ca_im = jnp.einsum('gcp,jgp->jgcp', cr, pw_im[1:]) + jnp.einsum('gcp,jgp->jgcp', ci, pw_re[1:])
    c_out = jnp.concatenate(
        [jnp.einsum('jgcp,gh->gpjhc', ca_re, eye).reshape(G * P, J * G * C),
         -jnp.einsum('jgcp,gh->gpjhc', ca_im, eye).reshape(G * P, J * G * C)], axis=0)
    taps = jnp.einsum('gop,kgpc->kgco', cr, ab_re) - jnp.einsum('gop,kgpc->kgco', ci, ab_im)
    none = jnp.zeros_like(taps[0])
    d_io = jnp.stack([jnp.stack([taps[j - i] if j >= i else none for j in range(J)])
                      for i in range(J)])
    d_io = jnp.einsum('ijgco,gh->igcjho', d_io, eye).reshape(J * G * C, J * G * C)
    a_grp = jnp.stack([pw_re[J].reshape(G * P), pw_im[J].reshape(G * P)])
    mag_l = jnp.exp(L * lr * dt)
    a_chunk = jnp.stack([(mag_l * jnp.cos(L * li * dt)).reshape(G * P),
                         (mag_l * jnp.sin(L * li * dt)).reshape(G * P)])
    return w_in.astype(BF16), c_out.astype(BF16), d_io.astype(BF16), a_grp, a_chunk


def _s5_kernel(u0_ref, u1_ref, w_ref, c_ref, dio_ref, a_ref, al_ref, d_ref, wg_ref, mg_ref, o_ref,
               carry_ref, x_s, st_s, y_s, bu_s):
    nb, tb, lanes = u0_ref.shape
    L, J = S5_CHUNK, S5_STEPS
    r = tb // L
    rows = nb * r
    ns = S5_GROUPS * S5_STATE

    @pl.when(pl.program_id(0) == 0)
    def _():
        carry_ref[...] = jnp.zeros_like(carry_ref)

    a_re, a_im = a_ref[0:1, :], a_ref[1:2, :]
    al_re, al_im = al_ref[0:1, :], al_ref[1:2, :]

    def inputs(q):
        parts = []
        for j in range(J):
            sl = pl.ds(q * J + j, r, stride=L)
            parts += [u0_ref[:, sl, :], u1_ref[:, sl, :]]
        return jnp.concatenate(parts, axis=-1).reshape(rows, J * 2 * lanes).astype(BF16)

    def advance(q):
        bu = bu_s[q]
        xr, xi = x_s[:, :ns], x_s[:, ns:]
        x_s[:, :ns] = a_re * xr - a_im * xi + bu[:, :ns]
        x_s[:, ns:] = a_re * xi + a_im * xr + bu[:, ns:]

    x_s[...] = jnp.zeros_like(x_s)
    for q in range(L // J):
        bu_s[q] = jnp.dot(inputs(q), w_ref[...], preferred_element_type=F32)
        advance(q)

    for b in range(nb):
        def hop(c, carry, b=b):
            xr, xi = carry
            row = pl.ds(b * r + c, 1)
            st_s[row, :] = jnp.concatenate([xr, xi], axis=1)
            p = x_s[row, :]
            return (al_re * xr - al_im * xi + p[:, :ns], al_re * xi + al_im * xr + p[:, ns:])

        xr, xi = lax.fori_loop(0, r, hop, (carry_ref[b:b + 1, :ns], carry_ref[b:b + 1, ns:]))
        carry_ref[b:b + 1, :] = jnp.concatenate([xr, xi], axis=1)

    x_s[...] = st_s[...]
    for q in range(L // J):
        y = (jnp.dot(x_s[...].astype(BF16), c_ref[...], preferred_element_type=F32)
             + jnp.dot(inputs(q), dio_ref[...], preferred_element_type=F32))
        for j in range(J):
            sl = pl.ds(q * J + j, r, stride=L)
            lo = j * 2 * lanes
            y_s[0, :, sl, :] = y[:, lo:lo + lanes].reshape(nb, r, lanes)
            y_s[1, :, sl, :] = y[:, lo + lanes:lo + 2 * lanes].reshape(nb, r, lanes)
        if q + 1 < L // J:
            advance(q)

    u = jnp.concatenate([u0_ref[...], u1_ref[...]], axis=-1).reshape(nb * tb, 2 * lanes)
    y = jnp.concatenate([y_s[0], y_s[1]], axis=-1).reshape(nb * tb, 2 * lanes)
    y = jax.nn.gelu(y + d_ref[...] * u)
    z = jnp.dot(y.astype(BF16), wg_ref[...], preferred_element_type=F32)
    out = _rms(z[:, :GROUP_W] * jax.nn.sigmoid(z[:, GROUP_W:]), mg_ref[...])
    o_ref[...] = out.reshape(nb, tb, GROUP_W)


def _s5(proj3, mats, d_skip, w_glu_bf16, merge_g, tb):
    bsz, t, _ = proj3.shape
    w_in, c_out, d_io, a_grp, a_chunk = mats
    ns = S5_GROUPS * S5_STATE
    jw = S5_STEPS * GROUP_W
    lanes = GROUP_W // 2
    rows = bsz * (tb // S5_CHUNK)
    half = lambda j: pl.BlockSpec((bsz, tb, lanes), lambda i, j=j: (0, i, 2 * COL_S5 + j))
    full = lambda a, b: pl.BlockSpec((a, b), lambda i: (0, 0))
    return pl.pallas_call(
        _s5_kernel,
        grid=(t // tb,),
        in_specs=[half(0), half(1), full(jw, 2 * ns), full(2 * ns, jw), full(jw, jw), full(2, ns),
                  full(2, ns), full(1, GROUP_W), full(GROUP_W, 2 * GROUP_W), full(1, GROUP_W)],
        out_specs=pl.BlockSpec((bsz, tb, GROUP_W), lambda i: (0, i, 0)),
        out_shape=jax.ShapeDtypeStruct((bsz, t, GROUP_W), F32),
        scratch_shapes=[pltpu.VMEM((bsz, 2 * ns), F32), pltpu.VMEM((rows, 2 * ns), F32),
                        pltpu.VMEM((rows, 2 * ns), F32), pltpu.VMEM((2, bsz, tb, lanes), F32),
                        pltpu.VMEM((S5_CHUNK // S5_STEPS, rows, 2 * ns), F32)],
        compiler_params=_cparams("arbitrary"),
        name="s5",
    )(proj3, proj3, w_in, c_out, d_io, a_grp, a_chunk, d_skip.reshape(1, -1), w_glu_bf16,
      merge_g.reshape(1, -1))


def _rwkv_kernel(r_s, k_s, v_s, lw_s, a_s, b_s, g_ref, rk_ref, lnw_ref, lnb_ref, mg_ref,
                 o_ref, h_ref, y_s, m_s, n_s, p_s, z_s, *, reset):
    tb = r_s.shape[0]
    ch = RWKV_CHUNK
    nh = GROUP_W // HEAD_W
    if reset:
        h_ref[...] = jnp.zeros_like(h_ref)
        return

    bd_bf16 = _block_diag_mask(GROUP_W, HEAD_W).astype(BF16)

    n4 = nh * ch
    ri = lax.broadcasted_iota(jnp.int32, (n4, n4), 0)
    ci = lax.broadcasted_iota(jnp.int32, (n4, n4), 1)
    same_head = (ri // ch) == (ci // ch)
    strict = same_head & ((ri % ch) > (ci % ch))
    incl = same_head & ((ri % ch) >= (ci % ch))
    eye = (ri == ci).astype(F32)
    blk0 = RWKV_INV_BLOCK
    sizes = [blk0 << i for i in range((ch // blk0).bit_length())]
    same_blk = {b: (ri // b) == (ci // b) for b in sizes}
    hm = ((lax.broadcasted_iota(jnp.int32, (n4, GROUP_W), 0) // ch)
          == (lax.broadcasted_iota(jnp.int32, (n4, GROUP_W), 1) // HEAD_W))
    tri = (lax.broadcasted_iota(jnp.int32, (ch, ch), 0)
           >= lax.broadcasted_iota(jnp.int32, (ch, ch), 1)).astype(BF16)

    def stack(x):
        return jnp.where(hm, jnp.concatenate([x] * nh, axis=0), 0.0)

    def dot(x, y, dims=_NN):
        return lax.dot_general(x.astype(BF16), y.astype(BF16), (dims, ((), ())),
                               preferred_element_type=F32)

    chunks = range(tb // ch)
    pre = []
    for c in chunks:
        sl = slice(c * ch, (c + 1) * ch)
        lw = lw_s[sl, :]
        cl = _exact_lhs_mm(tri, lw)
        cl_end = cl[ch - 1:ch, :]
        e_in, e_ex = jnp.exp(cl), jnp.exp(cl - lw)
        e_neg, e_end = jnp.exp(-cl), jnp.exp(cl_end - cl)
        av, bv, kv, rv, vv = a_s[sl, :], b_s[sl, :], k_s[sl, :], r_s[sl, :], v_s[sl, :]
        pre.append(dict(
            at4=stack(av * e_ex), rt4=stack(rv * e_in), v4=stack(vv),
            bt4=jnp.concatenate([bv * e_neg] * nh, axis=0),
            kt4=jnp.concatenate([kv * e_neg] * nh, axis=0),
            bh4=stack(bv * e_end), kh4=stack(kv * e_end), g_end=jnp.exp(cl_end)))
        yield
    for d in pre:
        ar = jnp.concatenate([d["at4"], d["rt4"]], axis=0)
        sc = dot(ar, jnp.concatenate([d["bt4"], d["kt4"]], axis=0), _NT)
        d["l_ab"] = jnp.where(strict, sc[:n4, :n4], 0.0)
        d["l_ak"] = jnp.where(strict, sc[:n4, n4:], 0.0)
        d["l_rb"] = jnp.where(incl, sc[n4:, :n4], 0.0)
        d["l_rk"] = jnp.where(incl, sc[n4:, n4:], 0.0)
        yield
    for d in pre:
        nb8 = jnp.where(same_blk[blk0], d["l_ab"], 0.0)
        d["tinv"] = eye + nb8
        d["pw"] = dot(nb8, nb8)
        yield
    for d in pre:
        d["tinv"] = d["tinv"] + dot(d["tinv"], d["pw"])
        d["pw"] = dot(d["pw"], d["pw"])
        yield
    for d in pre:
        d["tinv"] = d["tinv"] + dot(d["tinv"], d["pw"])
        yield
    blk = blk0
    while blk < ch:
        for d in pre:
            d["pw"] = dot(jnp.where(same_blk[2 * blk] & ~same_blk[blk], d["l_ab"], 0.0), d["tinv"])
            yield
        for d in pre:
            d["tinv"] = d["tinv"] + dot(d["tinv"], d["pw"])
            yield
        blk *= 2
    for d in pre:
        lv = dot(jnp.concatenate([d["l_ak"], d["l_rk"]], axis=0), d["v4"])
        d["lakv"], d["lrkv"] = lv[:n4], lv[n4:]
        d["khv"] = dot(d["kh4"], d["v4"], _TN)
        yield
    for d in pre:
        d["x12"] = dot(d["tinv"], jnp.concatenate([d["at4"], d["lakv"]], axis=1))
        yield
    for c, d in zip(chunks, pre):
        mn = dot(d["bh4"], d["x12"], _TN)
        pz = dot(d["l_rb"], d["x12"])
        m_s[c] = jnp.where(ri == ci, d["g_end"], 0.0) + mn[:, :GROUP_W]
        n_s[c] = mn[:, GROUP_W:] + d["khv"]
        p_s[c] = d["rt4"] + pz[:, :GROUP_W]
        z_s[c] = pz[:, GROUP_W:] + d["lrkv"]
        yield

    h = h_ref[...]
    for c in range(tb // ch):
        y4 = dot(p_s[c], h) + z_s[c]
        y = y4[0:ch]
        for j in range(1, nh):
            y = y + y4[j * ch:(j + 1) * ch]
        y_s[c * ch:(c + 1) * ch, :] = y
        h = dot(m_s[c], h) + n_s[c]
        yield
    h_ref[...] = h

    y = y_s[...]
    r, k, v = r_s[...], k_s[...], v_s[...]
    mean = _head_sum(y, bd_bf16) * (1.0 / HEAD_W)
    d = y - mean
    var = _head_sum(d * d, bd_bf16) * (1.0 / HEAD_W)
    y = d * lax.rsqrt(var + RWKV_GN_EPS) * lnw_ref[...] + lnb_ref[...]
    y = y + _head_sum(r * k * rk_ref[...], bd_bf16) * v
    o_ref[...] = _rms(y * g_ref[...], mg_ref[...])


def _rwkv_part(proj3, abg3, lyr, p, merge_g, tb):
    bsz, t, _ = proj3.shape
    row = lambda x: x.reshape(1, -1)
    abg = lambda j: _col_spec(tb, j)
    mats = pltpu.VMEM((tb // RWKV_CHUNK, GROUP_W, GROUP_W), F32)
    return dict(
        body=_rwkv_kernel, stages=15 * (tb // RWKV_CHUNK) + 1,
        args=[proj3, proj3, proj3, proj3, abg3, abg3, abg3, row(p["rwkv_r_k"][lyr]),
              row(p["rwkv_ln_w"][lyr]), row(p["rwkv_ln_b"][lyr]), row(merge_g)],
        in_specs=[_col_spec(tb, COL_R), _col_spec(tb, COL_K), _col_spec(tb, COL_V),
                  _col_spec(tb, COL_LORA), abg(EXT_A), abg(EXT_B), abg(EXT_G),
                  _row_spec(), _row_spec(), _row_spec(), _row_spec()],
        out_specs=[_out_blk(tb)], out_shape=[jax.ShapeDtypeStruct((bsz, t, GROUP_W), F32)],
        scratch=[pltpu.VMEM((GROUP_W, GROUP_W), F32), pltpu.VMEM((tb, GROUP_W), F32),
                 mats, mats, mats, mats])


def _bf16_bits(x):
    u = lax.bitcast_convert_type(x, jnp.uint32)
    r = u + jnp.uint32(0x7FFF) + ((u >> 16) & jnp.uint32(1))
    return r & jnp.uint32(0xFFFF0000)


def _slab_copies(rows_ref, slab_hbm, row0, sem):
    tm = rows_ref.shape[0]
    return [(rows_ref.at[:, pl.ds(j * LANES, LANES)], slab_hbm.at[pl.ds(row0, tm), j, :], sem)
            for j in range(slab_hbm.shape[1])]


def _merge_router_kernel(oa_ref, ob_ref, oc_ref, od_ref, h_ref, wo_ref, nw_ref, wr_ref, br_ref,
                         h1_ref, hn_hbm, ids_ref, wts_ref, cnt_ref, pk_s, sem):
    i = pl.program_id(0)
    steps = pl.num_programs(0)
    tm = h_ref.shape[0]
    slot = i % 2

    def writes(s, step):
        return [pltpu.make_async_copy(v, h, m) for v, h, m in
                _slab_copies(pk_s.at[s], hn_hbm, step * tm, sem.at[s])]

    mix = jnp.concatenate([oa_ref[...], ob_ref[...], oc_ref[...], od_ref[...]], axis=1)
    h1 = h_ref[...] + jnp.dot(mix.astype(BF16), wo_ref[...], preferred_element_type=F32)
    h1_ref[...] = h1
    hn = _rms(h1, nw_ref[...])
    half = hn.shape[1] // 2

    @pl.when(i >= 2)
    def _():
        for cp in writes(slot, i - 2):
            cp.wait()

    pk_s[slot] = _bf16_bits(hn[:, :half]) | (_bf16_bits(hn[:, half:]) >> 16)
    for cp in writes(slot, i):
        cp.start()

    @pl.when(i == steps - 1)
    def _():
        for cp in writes(slot, i):
            cp.wait()

    @pl.when((i == steps - 1) & (i >= 1))
    def _():
        for cp in writes(1 - slot, i - 1):
            cp.wait()

    logits = _dg3(hn, wr_ref[...], _NN) + br_ref[...]
    lane = lax.broadcasted_iota(jnp.int32, logits.shape, 1)
    big = jnp.int32(ROUTER_LANES)
    is_c = (lane >= N_EXPERTS) & (lane < N_EXPERTS + MOE_GROUPS)
    cm = jnp.max(jnp.where(is_c, logits, NEG_BIG), axis=-1, keepdims=True)
    gsel = jnp.min(jnp.where(is_c & (logits == cm), lane, big), axis=-1, keepdims=True) - N_EXPERTS
    p_g = 1.0 / jnp.sum(jnp.where(is_c, jnp.exp(logits - cm), 0.0), axis=-1, keepdims=True)
    lo = gsel * MOE_PER_GROUP
    in_g = (lane >= lo) & (lane < lo + MOE_PER_GROUP)
    m1 = jnp.max(jnp.where(in_g, logits, NEG_BIG), axis=-1, keepdims=True)
    i1 = jnp.min(jnp.where(in_g & (logits == m1), lane, big), axis=-1, keepdims=True)
    in_g2 = in_g & (lane != i1)
    m2 = jnp.max(jnp.where(in_g2, logits, NEG_BIG), axis=-1, keepdims=True)
    i2 = jnp.min(jnp.where(in_g2 & (logits == m2), lane, big), axis=-1, keepdims=True)
    w1 = p_g / (1.0 + jnp.exp(m2 - m1))
    w2 = p_g - w1
    two = lax.broadcasted_iota(jnp.int32, ids_ref.shape, 1)
    ids_ref[...] = jnp.where(two == 0, i1, i2)
    wts_ref[...] = jnp.where(two == 0, w1, w2)

    @pl.when(pl.program_id(0) == 0)
    def _():
        cnt_ref[...] = jnp.zeros_like(cnt_ref)

    cnt_ref[...] += jnp.sum(((lane == i1) | (lane == i2)).astype(F32), axis=0, keepdims=True)


def _merge_router(outs, h2d, w_out_bf16, norm_w, w_router, b_router, tm):
    n, d = h2d.shape
    grp = pl.BlockSpec((tm, GROUP_W), lambda i: (i, 0))
    full = lambda a, b: pl.BlockSpec((a, b), lambda i: (0, 0))
    rowblk = lambda w: pl.BlockSpec((tm, w), lambda i: (i, 0))
    return pl.pallas_call(
        _merge_router_kernel,
        grid=(n // tm,),
        in_specs=[grp, grp, grp, grp, rowblk(d), full(d, d), full(1, d), full(d, ROUTER_LANES),
                  full(1, ROUTER_LANES)],
        out_specs=[rowblk(d), pl.BlockSpec(memory_space=pl.ANY),
                   rowblk(MOE_TOPK), rowblk(MOE_TOPK), full(1, ROUTER_LANES)],
        out_shape=[jax.ShapeDtypeStruct((n, d), F32),
                   jax.ShapeDtypeStruct((n, d // 2 // LANES, LANES), jnp.uint32),
                   jax.ShapeDtypeStruct((n, MOE_TOPK), jnp.int32),
                   jax.ShapeDtypeStruct((n, MOE_TOPK), F32),
                   jax.ShapeDtypeStruct((1, ROUTER_LANES), F32)],
        scratch_shapes=[pltpu.VMEM((2, tm, d // 2), jnp.uint32), pltpu.SemaphoreType.DMA((2,))],
        compiler_params=_cparams("arbitrary"),
        name="merge_router",
    )(*[o.reshape(n, GROUP_W) for o in outs], h2d, w_out_bf16, norm_w.reshape(1, d), w_router, b_router)


def _row_copy(src_hbm, src_row, dst_ref, dst_row, sem):
    return pltpu.make_async_copy(src_hbm.at[pl.ds(src_row, 1), :], dst_ref.at[pl.ds(dst_row, 1), :], sem)


def _position_kernel(ids_ref, base_ref, pos_ref, run_ref):
    tm = ids_ref.shape[0]

    @pl.when(pl.program_id(0) == 0)
    def _():
        run_ref[...] = jnp.zeros_like(run_ref)

    i1, i2 = ids_ref[:, 0:1], ids_ref[:, 1:2]
    lane = lax.broadcasted_iota(jnp.int32, (tm, ROUTER_LANES), 1)
    hit = ((lane == i1) | (lane == i2)).astype(BF16)
    earlier = (lax.broadcasted_iota(jnp.int32, (tm, tm), 0)
               > lax.broadcasted_iota(jnp.int32, (tm, tm), 1)).astype(BF16)
    rank = jnp.dot(earlier, hit, preferred_element_type=F32) + run_ref[...]
    where_to = rank + base_ref[...]
    p1 = jnp.sum(jnp.where(lane == i1, where_to, 0.0), axis=-1, keepdims=True)
    p2 = jnp.sum(jnp.where(lane == i2, where_to, 0.0), axis=-1, keepdims=True)
    two = lax.broadcasted_iota(jnp.int32, pos_ref.shape, 1)
    pos_ref[...] = jnp.where(two == 0, p1, p2).astype(jnp.int32)
    run_ref[...] += jnp.sum(hit.astype(F32), axis=0, keepdims=True)


def _positions(ids, group_start, tm):
    n = ids.shape[0]
    return pl.pallas_call(
        _position_kernel,
        grid=(n // tm,),
        in_specs=[pl.BlockSpec((tm, MOE_TOPK), lambda i: (i, 0)),
                  pl.BlockSpec((1, ROUTER_LANES), lambda i: (0, 0))],
        out_specs=pl.BlockSpec((tm, MOE_TOPK), lambda i: (i, 0)),
        out_shape=jax.ShapeDtypeStruct((n, MOE_TOPK), jnp.int32),
        scratch_shapes=[pltpu.VMEM((1, ROUTER_LANES), F32)],
        compiler_params=_cparams("arbitrary"),
        name="moe_positions",
    )(ids, group_start)


def _dispatch_kernel(p0_ref, p1_ref, x_ref, xs_in_hbm, xs_hbm, sem):
    del xs_in_hbm
    n = x_ref.shape[0]

    def copies(i):
        return (pltpu.make_async_copy(x_ref.at[i], xs_hbm.at[p0_ref[i]], sem),
                pltpu.make_async_copy(x_ref.at[i], xs_hbm.at[p1_ref[i]], sem))

    def issue(i, c):
        for prio, cp in enumerate(copies(i)):
            cp.start(priority=prio)
        return c

    def drain(i, c):
        for cp in copies(i):
            cp.wait()
        return c

    lax.fori_loop(0, n, issue, 0, unroll=8)
    lax.fori_loop(0, n, drain, 0, unroll=8)


def _dispatch(x, pos0, pos1, n_rows, chunk):
    n, slabs, lanes = x.shape
    smem = lambda: pl.BlockSpec((chunk,), lambda s: (s,), memory_space=pltpu.SMEM)
    return pl.pallas_call(
        _dispatch_kernel,
        grid=(n // chunk,),
        in_specs=[smem(), smem(), pl.BlockSpec((chunk, slabs, lanes), lambda s: (s, 0, 0)),
                  pl.BlockSpec(memory_space=pl.ANY)],
        out_specs=pl.BlockSpec(memory_space=pl.ANY),
        out_shape=jax.ShapeDtypeStruct((n_rows, slabs, lanes), x.dtype),
        input_output_aliases={3: 0},
        scratch_shapes=[pltpu.SemaphoreType.DMA(())],
        compiler_params=_cparams("arbitrary"),
        name="moe_dispatch",
    )(pos0, pos1, x, jnp.zeros((n_rows, slabs, lanes), x.dtype))


def _expert_kernel(te_ref, nt_ref, xs_hbm, wg_ref, wu_ref, wd_ref, o_ref, wg_s, wu_s, wd_s, x_s, sem):
    i = pl.program_id(0)
    tm = o_ref.shape[0]
    slot = i % 2

    def reads(s, tile):
        return [pltpu.make_async_copy(h, v, m) for v, h, m in
                _slab_copies(x_s.at[s], xs_hbm, tile * tm, sem.at[s])]

    @pl.when(i == 0)
    def _():
        for cp in reads(0, 0):
            cp.start()

    @pl.when(i + 1 < pl.num_programs(0))
    def _():
        for cp in reads(1 - slot, i + 1):
            cp.start()

    @pl.when((i == 0) | (te_ref[i] != te_ref[jnp.maximum(i - 1, 0)]))
    def _():
        wg_s[...] = wg_ref[...].astype(BF16)
        wu_s[...] = wu_ref[...].astype(BF16)
        wd_s[...] = wd_ref[...].astype(BF16)

    for cp in reads(slot, i):
        cp.wait()

    @pl.when(i < nt_ref[0])
    def _():
        p = x_s[slot]
        xa = lax.bitcast_convert_type(p & jnp.uint32(0xFFFF0000), F32).astype(BF16)
        xb = lax.bitcast_convert_type(p << 16, F32).astype(BF16)
        x = jnp.concatenate([xa, xb], axis=1)
        he = (_silu(jnp.dot(x, wg_s[...], preferred_element_type=F32))
              * jnp.dot(x, wu_s[...], preferred_element_type=F32))
        o_ref[...] = jnp.dot(he.astype(BF16), wd_s[...], preferred_element_type=F32)

    @pl.when(i >= nt_ref[0])
    def _():
        o_ref[...] = jnp.zeros_like(o_ref)


def _experts(xs, tile_expert, n_tiles_used, wg, wu, wd, layer, tm):
    p, slabs, lanes = xs.shape
    d = 2 * slabs * lanes
    pick = lambda i, te, nt: (layer, te[i], 0, 0)
    grid_spec = pltpu.PrefetchScalarGridSpec(
        num_scalar_prefetch=2,
        grid=(p // tm,),
        in_specs=[pl.BlockSpec(memory_space=pl.ANY),
                  pl.BlockSpec((None, None, d, D_EXPERT), pick),
                  pl.BlockSpec((None, None, d, D_EXPERT), pick),
                  pl.BlockSpec((None, None, D_EXPERT, d), pick)],
        out_specs=pl.BlockSpec((tm, d), lambda i, te, nt: (i, 0)),
        scratch_shapes=[pltpu.VMEM((d, D_EXPERT), BF16), pltpu.VMEM((d, D_EXPERT), BF16),
                        pltpu.VMEM((D_EXPERT, d), BF16), pltpu.VMEM((2, tm, slabs * lanes), xs.dtype),
                        pltpu.SemaphoreType.DMA((2,))],
    )
    return pl.pallas_call(
        _expert_kernel,
        grid_spec=grid_spec,
        out_shape=jax.ShapeDtypeStruct((p, d), F32),
        compiler_params=_cparams("arbitrary"),
        name="moe_experts",
    )(tile_expert, n_tiles_used, xs, wg, wu, wd)


def _combine_kernel(i0_ref, i1_ref, h1_ref, w_ref, fw_ref, ys_hbm, o_ref, buf0, buf1, sem, *, final_norm):
    n = o_ref.shape[0]

    def issue(i, c):
        _row_copy(ys_hbm, i0_ref[i], buf0, i, sem).start(priority=0)
        _row_copy(ys_hbm, i1_ref[i], buf1, i, sem).start(priority=1)
        return c

    def drain(i, c):
        _row_copy(ys_hbm, 0, buf0, i, sem).wait()
        _row_copy(ys_hbm, 0, buf1, i, sem).wait()
        return c

    lax.fori_loop(0, n, issue, 0, unroll=8)
    lax.fori_loop(0, n, drain, 0, unroll=8)
    out = h1_ref[...] + w_ref[:, 0:1] * buf0[...] + w_ref[:, 1:2] * buf1[...]
    if final_norm:
        out = _rms(out, fw_ref[...])
    o_ref[...] = out


def _combine(ys, pos0, pos1, wts, h1, final_w, final_norm, chunk):
    n, d = h1.shape
    smem = lambda: pl.BlockSpec((chunk,), lambda s: (s,), memory_space=pltpu.SMEM)
    return pl.pallas_call(
        functools.partial(_combine_kernel, final_norm=final_norm),
        grid=(n // chunk,),
        in_specs=[smem(), smem(), pl.BlockSpec((chunk, d), lambda s: (s, 0)),
                  pl.BlockSpec((chunk, MOE_TOPK), lambda s: (s, 0)),
                  pl.BlockSpec((1, d), lambda s: (0, 0)), pl.BlockSpec(memory_space=pl.ANY)],
        out_specs=pl.BlockSpec((chunk, d), lambda s: (s, 0)),
        out_shape=jax.ShapeDtypeStruct((n, d), F32),
        scratch_shapes=[pltpu.VMEM((chunk, d), F32), pltpu.VMEM((chunk, d), F32),
                        pltpu.SemaphoreType.DMA(())],
        compiler_params=_cparams("arbitrary"),
        name="moe_combine",
    )(pos0, pos1, h1, wts, final_w.reshape(1, d), ys)


def _moe(hn_packed, ids, wts, counts, h1, wg, wu, wd, layer, final_w, final_norm, tm):
    n = h1.shape[0]
    n_tiles = (n * MOE_TOPK) // tm + N_EXPERTS
    cnt = counts[0, :N_EXPERTS].astype(jnp.int32)
    padded = ((cnt + tm - 1) // tm) * tm
    ends = jnp.cumsum(padded)
    tile_start = jnp.arange(n_tiles, dtype=jnp.int32) * tm
    tile_expert = jnp.minimum(jnp.sum(ends[None, :] <= tile_start[:, None], axis=1),
                              N_EXPERTS - 1).astype(jnp.int32)
    n_tiles_used = (ends[-1] // tm).astype(jnp.int32).reshape(1)
    group_start = jnp.zeros((1, ROUTER_LANES), F32).at[0, :N_EXPERTS].set((ends - padded).astype(F32))
    pos = _positions(ids, group_start, _pick_div(n, 512))
    pos0, pos1 = pos[:, 0], pos[:, 1]
    xs = _dispatch(hn_packed, pos0, pos1, n_tiles * tm, _pick_div(n, 1024))
    ys = _experts(xs, tile_expert, n_tiles_used, wg, wu, wd, layer, tm)
    return _combine(ys, pos0, pos1, wts, h1, final_w, final_norm, _pick_div(n, 1024))


def _pick_div(n, pref):
    while n % pref:
        pref //= 2
    return pref


def _pick(n, pref):
    return pref if n % pref == 0 else n


def kernel(x, norm1_w, w_in, hgrn_lb_logits, hgrn_norm_w, s5_lambda_re, s5_lambda_im, s5_log_dt, s5_b_re, s5_b_im, s5_c_re, s5_c_im, s5_d, s5_w_glu, rwkv_mu, rwkv_w0, rwkv_w2, rwkv_a0, rwkv_a2, rwkv_g2, rwkv_k_k, rwkv_k_a, rwkv_r_k, rwkv_v0, rwkv_v1, rwkv_v2, rwkv_ln_w, rwkv_ln_b, lru_conv_w, lru_conv_b, lru_wa, lru_ba, lru_wx, lru_bx, lru_lambda, merge_gain, w_out, norm2_w, moe_coarse_w, moe_coarse_b, moe_fine_w, moe_fine_b, moe_w_gate, moe_w_up, moe_w_down, final_norm_w):
    bsz, t, d = x.shape
    n = bsz * t
    depth = w_in.shape[0]
    rw = dict(rwkv_mu=rwkv_mu, rwkv_w0=rwkv_w0, rwkv_w2=rwkv_w2, rwkv_a0=rwkv_a0, rwkv_a2=rwkv_a2,
              rwkv_g2=rwkv_g2, rwkv_k_k=rwkv_k_k, rwkv_k_a=rwkv_k_a, rwkv_r_k=rwkv_r_k,
              rwkv_v0=rwkv_v0, rwkv_v1=rwkv_v1, rwkv_v2=rwkv_v2, rwkv_ln_w=rwkv_ln_w,
              rwkv_ln_b=rwkv_ln_b)
    lb_all = jnp.cumsum(jax.nn.softmax(hgrn_lb_logits.astype(F32), axis=0), axis=0)
    lb_all = lb_all - lb_all[:1]

    tm_proj = _pick(n, 1024)
    tm_in = _pick_div(t, 512)
    tm_moe = 512
    tb_mix = _pick(t, 256)
    tb_s5 = _pick(t, 512)

    h = x.reshape(n, d)
    v_first = None
    for l in range(depth):
        mg = merge_gain[l].reshape(4, GROUP_W)
        proj, abg, v_first = _in_proj(h, norm1_w[l], w_in[l].astype(BF16), v_first, l, rw, t, tm_in)
        proj3 = proj.reshape(bsz, t, -1)
        abg3 = abg.reshape(bsz, t, -1)
        mats = _s5_matrices(s5_lambda_re[l], s5_lambda_im[l], s5_log_dt[l], s5_b_re[l], s5_b_im[l],
                            s5_c_re[l], s5_c_im[l])
        o_b = _s5(proj3, mats, s5_d[l], s5_w_glu[l].astype(BF16), mg[1], tb_s5)
        parts = [_rwkv_part(proj3, abg3, l, rw, mg[2], tb_mix),
                 _hgrn_part(proj3, lb_all[l], hgrn_norm_w[l], mg[0], tb_mix),
                 _lru_part(proj3, lru_conv_w[l], lru_conv_b[l], _block_diag_weight(lru_wa[l]),
                           lru_ba[l], _block_diag_weight(lru_wx[l]), lru_bx[l], lru_lambda[l],
                           mg[3], tb_mix)]
        outs = _mixers(parts, bsz, t // tb_mix, "mixers")
        o_c, o_a, o_d = outs
        w_router = jnp.concatenate(
            [moe_fine_w[l].transpose(1, 0, 2).reshape(d, N_EXPERTS), moe_coarse_w[l],
             jnp.zeros((d, ROUTER_LANES - N_EXPERTS - MOE_GROUPS), F32)], axis=1)
        b_router = jnp.concatenate(
            [moe_fine_b[l].reshape(N_EXPERTS), moe_coarse_b[l],
             jnp.zeros((ROUTER_LANES - N_EXPERTS - MOE_GROUPS,), F32)]).reshape(1, ROUTER_LANES)
        h1, hn, ids, wts, counts = _merge_router((o_a, o_b, o_c, o_d), h, w_out[l].astype(BF16),
                                                 norm2_w[l], w_router, b_router, tm_proj)
        h = _moe(hn, ids, wts, counts, h1, moe_w_gate, moe_w_up, moe_w_down, l,
                 final_norm_w, l == depth - 1, tm_moe)
    return h.reshape(bsz, t, d)
```

```python
import functools

import jax
import jax.numpy as jnp
from jax import lax
from jax.experimental import pallas as pl
from jax.experimental.pallas import tpu as pltpu

F32 = jnp.float32
BF16 = jnp.bfloat16

GROUP_W = 256
RMS_EPS = 1e-6
HEAD_W = 64
HGRN_CHUNK = 32
S5_CH = 16
S5_GROUPS = GROUP_W // S5_CH
S5_STATE = 64
S5_CHUNK = 16
S5_STEPS = 4
RWKV_CHUNK = 64
RWKV_INV_BLOCK = 8
RWKV_GN_EPS = 64e-5
RWKV_W_LORA = 64
RWKV_V_LORA = 32
LRU_CONV = 4
LRU_C = 8.0
MOE_GROUPS = 4
MOE_PER_GROUP = 8
N_EXPERTS = MOE_GROUPS * MOE_PER_GROUP
MOE_TOPK = 2
D_EXPERT = 512
LANES = 128
SUBLANES = 8
ROUTER_LANES = LANES
NEG_BIG = -1e30
V7X_VMEM_BYTES = 64 * 1024 * 1024
VMEM_LIMIT = V7X_VMEM_BYTES - 8 * 1024 * 1024

COL_HQ, COL_HF, COL_HI, COL_HG, COL_S5, COL_R, COL_K, COL_V, COL_LORA, COL_LG, COL_LX = range(11)
EXT_A, EXT_B, EXT_G = range(3)


def _cparams(*sem):
    return pltpu.CompilerParams(dimension_semantics=sem, vmem_limit_bytes=VMEM_LIMIT)


def _mm(a, b):
    return jnp.dot(a.astype(BF16), b.astype(BF16), preferred_element_type=F32)


def _split2(x):
    hi = x.astype(BF16)
    lo = (x - hi.astype(F32)).astype(BF16)
    return hi, lo


def _dg3(a, b, dims):
    ah, al = _split2(a)
    bh, bl = _split2(b)
    d = lambda x, y: lax.dot_general(x, y, (dims, ((), ())), preferred_element_type=F32)
    return d(ah, bh) + d(ah, bl) + d(al, bh)


_NN = ((1,), (0,))
_NT = ((1,), (1,))
_TN = ((0,), (0,))


def _exact_lhs_mm(m_bf16, x):
    h1 = x.astype(BF16)
    r1 = x - h1.astype(F32)
    h2 = r1.astype(BF16)
    h3 = (r1 - h2.astype(F32)).astype(BF16)
    d = lambda y: jnp.dot(m_bf16, y, preferred_element_type=F32)
    return d(h1) + d(h2) + d(h3)


def _head_sum(x, bd_bf16):
    return jnp.dot(x.astype(BF16), bd_bf16, preferred_element_type=F32)


def _block_diag_mask(n, blk):
    r = lax.broadcasted_iota(jnp.int32, (n, n), 0) // blk
    c = lax.broadcasted_iota(jnp.int32, (n, n), 1) // blk
    return r == c


def _rms(x, w):
    return x * lax.rsqrt(jnp.mean(x * x, axis=-1, keepdims=True) + RMS_EPS) * w


def _silu(x):
    return x * jax.nn.sigmoid(x)


def _softplus(x):
    return jnp.maximum(x, 0.0) + jnp.log(1.0 + jnp.exp(-jnp.abs(x)))


def _in_proj_kernel(*refs, has_vmix, tiles_per_seq):
    if has_vmix:
        (x_ref, nw_ref, w_ref, mu_ref, w0_ref, w2_ref, a0_ref, a2_ref, g2_ref, kk_ref, ka_ref,
         vf_ref, v0_ref, v1_ref, v2_ref, o_ref, o2_ref, prev_s) = refs
    else:
        (x_ref, nw_ref, w_ref, mu_ref, w0_ref, w2_ref, a0_ref, a2_ref, g2_ref, kk_ref, ka_ref,
         o_ref, o2_ref, vf_out_ref, prev_s) = refs
    tm = x_ref.shape[0]
    gw = GROUP_W
    c0, c1 = COL_R * gw, (COL_LORA + 1) * gw
    @pl.when((pl.program_id(0) % tiles_per_seq) == 0)
    def _():
        prev_s[...] = jnp.zeros_like(prev_s)

    y = _rms(x_ref[...], nw_ref[...]).astype(BF16)
    raw = jnp.dot(y, w_ref[:, c0:c1], preferred_element_type=F32)

    row0 = lax.broadcasted_iota(jnp.int32, (tm, 1), 0) == 0
    sh = jnp.where(row0, prev_s[...], pltpu.roll(raw, 1, axis=0))
    prev_s[...] = raw[tm - 1:tm, :]
    pf = raw + mu_ref[...] * (sh - raw)
    r, k, v, lo = (pf[:, j * GROUP_W:(j + 1) * GROUP_W] for j in range(4))

    o_ref[:, :c0] = jnp.dot(y, w_ref[:, :c0], preferred_element_type=F32)
    bd_bf16 = _block_diag_mask(GROUP_W, HEAD_W).astype(BF16)
    lo_wa, lo_g = lo[:, :LANES], lo[:, LANES:]
    w_raw = -_softplus(-(w0_ref[...] + _mm(jnp.tanh(lo_wa), w2_ref[...]))) - 0.5
    alr = jax.nn.sigmoid(a0_ref[...] + _mm(lo_wa, a2_ref[...]))
    o_ref[:, c1:] = jnp.dot(y, w_ref[:, c1:], preferred_element_type=F32)
    kkr = k * kk_ref[...]
    kk = kkr / jnp.maximum(jnp.sqrt(_head_sum(kkr * kkr, bd_bf16)), 1e-12)
    if has_vmix:
        gate = jax.nn.sigmoid(v0_ref[...] + _mm(_mm(v, v1_ref[...]), v2_ref[...]))
        v = v + (vf_ref[...] - v) * gate
    else:
        vf_out_ref[...] = v
    o_ref[:, c0:c0 + GROUP_W] = r
    o_ref[:, c0 + GROUP_W:c0 + 2 * GROUP_W] = k * (1.0 + (alr - 1.0) * ka_ref[...])
    o_ref[:, c0 + 2 * GROUP_W:c0 + 3 * GROUP_W] = v
    o_ref[:, c0 + 3 * GROUP_W:c1] = -jnp.exp(w_raw)
    o2_ref[:, EXT_A * gw:(EXT_A + 1) * gw] = -kk
    o2_ref[:, EXT_B * gw:(EXT_B + 1) * gw] = kk * alr
    o2_ref[:, EXT_G * gw:(EXT_G + 1) * gw] = _mm(jax.nn.sigmoid(lo_g), g2_ref[...])


def _pad_rows(w, start, total=LANES):
    out = jnp.zeros((total, w.shape[1]), F32).at[start:start + w.shape[0]].set(w)
    return out.astype(BF16)


def _in_proj(h2d, norm_w, w_bf16, v_first, lyr, p, t, tm):
    n, d = h2d.shape
    d_in = w_bf16.shape[1]
    has_vmix = v_first is not None
    row = lambda x: x.reshape(1, -1)
    full = lambda a, b: pl.BlockSpec((a, b), lambda i: (0, 0))
    rowblk = lambda w: pl.BlockSpec((tm, w), lambda i: (i, 0))
    args = [h2d, row(norm_w), w_bf16, row(p["rwkv_mu"][lyr]), row(p["rwkv_w0"][lyr]),
            _pad_rows(p["rwkv_w2"][lyr], 0), row(p["rwkv_a0"][lyr]),
            _pad_rows(p["rwkv_a2"][lyr], RWKV_W_LORA), p["rwkv_g2"][lyr].astype(BF16),
            row(p["rwkv_k_k"][lyr]), row(p["rwkv_k_a"][lyr])]
    specs = [rowblk(d), full(1, d), full(d, d_in), full(1, 4 * GROUP_W), full(1, GROUP_W),
             full(LANES, GROUP_W), full(1, GROUP_W), full(LANES, GROUP_W), full(LANES, GROUP_W),
             full(1, GROUP_W), full(1, GROUP_W)]
    out_specs = [rowblk(d_in), rowblk(3 * GROUP_W)]
    out_shape = [jax.ShapeDtypeStruct((n, d_in), F32), jax.ShapeDtypeStruct((n, 3 * GROUP_W), F32)]
    if has_vmix:
        v1 = jnp.zeros((GROUP_W, LANES), F32).at[:, :RWKV_V_LORA].set(p["rwkv_v1"][lyr - 1]).astype(BF16)
        v2 = jnp.zeros((LANES, GROUP_W), F32).at[:RWKV_V_LORA].set(p["rwkv_v2"][lyr - 1]).astype(BF16)
        args += [v_first, row(p["rwkv_v0"][lyr - 1]), v1, v2]
        specs += [rowblk(GROUP_W), full(1, GROUP_W), full(GROUP_W, LANES), full(LANES, GROUP_W)]
    else:
        out_specs.append(rowblk(GROUP_W))
        out_shape.append(jax.ShapeDtypeStruct((n, GROUP_W), F32))
    res = pl.pallas_call(
        functools.partial(_in_proj_kernel, has_vmix=has_vmix, tiles_per_seq=t // tm),
        grid=(n // tm,),
        in_specs=specs,
        out_specs=out_specs,
        out_shape=out_shape,
        scratch_shapes=[pltpu.VMEM((1, 4 * GROUP_W), F32)],
        compiler_params=_cparams("arbitrary"),
        name="in_proj",
    )(*args)
    return res[0], res[1], (v_first if has_vmix else res[2])


def _col_spec(tb, col):
    return pl.BlockSpec((None, tb, GROUP_W), lambda b, t: (b, t, col))


def _row_spec(width=GROUP_W):
    return pl.BlockSpec((1, width), lambda b, t: (0, 0))


def _full_spec(shape):
    return pl.BlockSpec(shape, lambda b, t: (0,) * len(shape))


def _hgrn_kernel(q_ref, f_ref, i_ref, g_ref, lb_ref, nw_ref, mg_ref, o_ref,
                 st_ref, q_s, k_s, v_s, lf_s, o_s, *, reset):
    ch = HGRN_CHUNK
    tb = q_ref.shape[0]
    if reset:
        st_ref[...] = jnp.zeros_like(st_ref)
        return

    lb = lb_ref[...]
    fx = f_ref[...]
    x1 = jnp.log(lb)
    x2 = jnp.log(1.0 - lb) - _softplus(-fx)
    m = jnp.maximum(x1, x2)
    lf_s[...] = m + jnp.log(jnp.exp(x1 - m) + jnp.exp(x2 - m))
    q_s[...] = _silu(q_ref[...])
    k_s[...] = (1.0 - lb) * jax.nn.sigmoid(-fx)
    v_s[...] = _silu(i_ref[...])
    yield

    bd = _block_diag_mask(GROUP_W, HEAD_W)
    bd_bf16 = bd.astype(BF16)
    tri = (lax.broadcasted_iota(jnp.int32, (ch, ch), 0)
           >= lax.broadcasted_iota(jnp.int32, (ch, ch), 1))
    tri_bf16 = tri.astype(BF16)
    hb = ch // 2
    tri3 = (lax.broadcasted_iota(jnp.int32, (hb, hb, 1), 0)
            <= lax.broadcasted_iota(jnp.int32, (hb, hb, 1), 1))
    first, second = slice(0, hb), slice(hb, ch)

    st = st_ref[...]
    for c in range(tb // ch):
        sl = slice(c * ch, (c + 1) * ch)
        qc, kc, vc = q_s[sl, :], k_s[sl, :], v_s[sl, :]
        b = _exact_lhs_mm(tri_bf16, lf_s[sl, :])
        def pairs(ss, ts, mask):
            rel = b[None, ts, :] - b[ss, None, :]
            dec = jnp.exp(rel if mask is None else jnp.where(mask, rel, NEG_BIG))
            return (qc[None, ts, :] * kc[ss, None, :]) * dec

        p = jnp.concatenate([pairs(first, first, tri3), pairs(first, second, None),
                             pairs(second, second, tri3)], axis=0)
        sc = jnp.dot(p.reshape(3 * hb * hb, GROUP_W).astype(BF16), bd_bf16,
                     preferred_element_type=F32).reshape(3 * hb, hb, GROUP_W)
        v1, v2 = vc[first, None, :], vc[second, None, :]
        o_intra = jnp.concatenate(
            [jnp.sum(sc[:hb] * v1, axis=0),
             jnp.sum(sc[hb:2 * hb] * v1, axis=0) + jnp.sum(sc[2 * hb:] * v2, axis=0)], axis=0)
        o_inter = lax.dot_general((qc * jnp.exp(b)).astype(BF16), st.astype(BF16),
                                  (_NT, ((), ())), preferred_element_type=F32)
        b_end = b[ch - 1:ch, :]
        kh = kc * jnp.exp(b_end - b)
        upd = lax.dot_general(vc.astype(BF16), kh.astype(BF16), (_TN, ((), ())),
                              preferred_element_type=F32)
        st = st * jnp.exp(b_end) + jnp.where(bd, upd, 0.0)
        o_s[sl, :] = o_intra + o_inter
        yield
    st_ref[...] = st

    o = o_s[...]
    ms = _head_sum(o * o, bd_bf16) * (1.0 / HEAD_W)
    o = o * lax.rsqrt(ms + RMS_EPS) * nw_ref[...] * _silu(g_ref[...])
    o_ref[...] = _rms(o, mg_ref[...])


def _out_blk(tb):
    return pl.BlockSpec((None, tb, GROUP_W), lambda b, i: (b, i, 0))


def _hgrn_part(proj3, lb, norm_w, merge_g, tb):
    bsz, t, _ = proj3.shape
    blk = pltpu.VMEM((tb, GROUP_W), F32)
    return dict(
        body=_hgrn_kernel, stages=tb // HGRN_CHUNK + 2,
        args=[proj3, proj3, proj3, proj3, lb.reshape(1, -1), norm_w.reshape(1, -1), merge_g.reshape(1, -1)],
        in_specs=[_col_spec(tb, COL_HQ), _col_spec(tb, COL_HF), _col_spec(tb, COL_HI),
                  _col_spec(tb, COL_HG), _row_spec(), _row_spec(), _row_spec()],
        out_specs=[_out_blk(tb)],
        out_shape=[jax.ShapeDtypeStruct((bsz, t, GROUP_W), F32)],
        scratch=[pltpu.VMEM((GROUP_W, GROUP_W), F32), blk, blk, blk, blk, blk])


def _mixer_kernel(*refs, parts):
    groups, i = [], 0
    for kind in range(3):
        for body, counts in parts:
            groups.append(refs[i:i + counts[kind]])
            i += counts[kind]
    k = len(parts)
    per_part = [groups[j] + groups[k + j] + groups[2 * k + j] for j in range(k)]

    @pl.when(pl.program_id(1) == 0)
    def _():
        for (body, _), r in zip(parts, per_part):
            for _step in body(*r, reset=True):
                pass

    runs = [[body(*r, reset=False), 0, counts[3]] for (body, counts), r in zip(parts, per_part)]
    while runs:
        run = min(runs, key=lambda x: x[1] / x[2])
        try:
            next(run[0])
            run[1] += 1
        except StopIteration:
            runs.remove(run)


def _mixers(parts, bsz, n_tblocks, name):
    cat = lambda key: [x for p in parts for x in p[key]]
    light = tuple((p["body"], (len(p["args"]), len(p["out_specs"]), len(p["scratch"]), p["stages"]))
                  for p in parts)
    return pl.pallas_call(
        functools.partial(_mixer_kernel, parts=light),
        grid=(bsz, n_tblocks),
        in_specs=cat("in_specs"),
        out_specs=cat("out_specs"),
        out_shape=cat("out_shape"),
        scratch_shapes=cat("scratch"),
        compiler_params=_cparams("parallel", "arbitrary"),
        name=name,
    )(*cat("args"))


def _lru_kernel(xg_ref, xr_ref, cw_ref, cb_ref, wa_ref, ba_ref, wx_ref, bx_ref, lam_ref, mg_ref,
                o_ref, buf_ref, h_ref, *, reset):
    tb = xr_ref.shape[0]
    pad = SUBLANES
    if reset:
        buf_ref[0:pad, :] = jnp.zeros((pad, GROUP_W), F32)
        h_ref[...] = jnp.zeros_like(h_ref)
        return

    xr = xr_ref[...]
    buf_ref[pad:pad + tb, :] = xr
    xc = cb_ref[...] + jnp.zeros_like(xr)
    for j in range(LRU_CONV):
        xc = xc + cw_ref[j:j + 1, :] * buf_ref[pl.ds(pad - (LRU_CONV - 1) + j, tb), :]
    buf_ref[0:pad, :] = xr[tb - pad:tb, :]

    r = jax.nn.sigmoid(jnp.dot(xc.astype(BF16), wa_ref[...], preferred_element_type=F32) + ba_ref[...])
    gi = jax.nn.sigmoid(jnp.dot(xc.astype(BF16), wx_ref[...], preferred_element_type=F32) + bx_ref[...])
    log_a = -LRU_C * r * _softplus(-lam_ref[...])
    a = jnp.exp(log_a)
    x = jnp.sqrt(1.0 - jnp.exp(2.0 * log_a)) * (gi * xc)
    yield

    rows = lax.broadcasted_iota(jnp.int32, (tb, 1), 0)
    k = 1
    while k < tb:
        keep = rows >= k
        x = x + jnp.where(keep, a * pltpu.roll(x, k, axis=0), 0.0)
        a = jnp.where(keep, a * pltpu.roll(a, k, axis=0), a)
        k *= 2
        yield
    h = x + a * h_ref[...]
    h_ref[...] = h[tb - 1:tb, :]
    o_ref[...] = _rms(jax.nn.gelu(xg_ref[...]) * h, mg_ref[...])


def _lru_part(proj3, conv_w, conv_b, wa_bd, ba, wx_bd, bx, lam, merge_g, tb):
    bsz, t, _ = proj3.shape
    return dict(
        body=_lru_kernel, stages=tb.bit_length() + 1,
        args=[proj3, proj3, conv_w, conv_b.reshape(1, -1), wa_bd, ba.reshape(1, -1), wx_bd,
              bx.reshape(1, -1), lam.reshape(1, -1), merge_g.reshape(1, -1)],
        in_specs=[_col_spec(tb, COL_LG), _col_spec(tb, COL_LX), _full_spec((LRU_CONV, GROUP_W)),
                  _row_spec(), _full_spec((GROUP_W, GROUP_W)), _row_spec(),
                  _full_spec((GROUP_W, GROUP_W)), _row_spec(), _row_spec(), _row_spec()],
        out_specs=[_out_blk(tb)],
        out_shape=[jax.ShapeDtypeStruct((bsz, t, GROUP_W), F32)],
        scratch=[pltpu.VMEM((tb + SUBLANES, GROUP_W), F32), pltpu.VMEM((1, GROUP_W), F32)])


def _block_diag_weight(w):
    h, n, _ = w.shape
    eye = jnp.eye(h, dtype=w.dtype)
    return jnp.einsum('hij,hg->higj', w, eye).reshape(h * n, h * n).astype(BF16)


def _s5_matrices(lam_re, lam_im, log_dt, b_re, b_im, c_re, c_im):
    L, G, P, C = S5_CHUNK, S5_GROUPS, S5_STATE, S5_CH
    lr, li = lam_re.astype(F32), lam_im.astype(F32)
    dt = jnp.exp(log_dt.astype(F32))[:, None]
    mag = jnp.exp(lr * dt)
    a_re, a_im = mag * jnp.cos(li * dt), mag * jnp.sin(li * dt)
    den = lr * lr + li * li
    kap_re = ((a_re - 1.0) * lr + a_im * li) / den
    kap_im = (a_im * lr - (a_re - 1.0) * li) / den
    br, bi = b_re.astype(F32), b_im.astype(F32)
    bb_re = kap_re[..., None] * br - kap_im[..., None] * bi
    bb_im = kap_re[..., None] * bi + kap_im[..., None] * br
    cr, ci = c_re.astype(F32), c_im.astype(F32)
    eye = jnp.eye(G, dtype=F32)
    J = S5_STEPS
    kk = jnp.arange(J + 1, dtype=F32)[:, None, None]
    pmag = jnp.exp(kk * (lr * dt)[None])
    pw_re, pw_im = pmag * jnp.cos(kk * (li * dt)[None]), pmag * jnp.sin(kk * (li * dt)[None])
    ab_re = pw_re[:J, :, :, None] * bb_re[None] - pw_im[:J, :, :, None] * bb_im[None]
    ab_im = pw_re[:J, :, :, None] * bb_im[None] + pw_im[:J, :, :, None] * bb_re[None]
    rev = lambda x: jnp.stack([x[J - 1 - j] for j in range(J)])
    w_in = jnp.concatenate(
        [jnp.einsum('jgpc,gh->jgchp', rev(ab_re), eye).reshape(J * G * C, G * P),
         jnp.einsum('jgpc,gh->jgchp', rev(ab_im), eye).reshape(J * G * C, G * P)], axis=1)
    ca_re = jnp.einsum('gcp,jgp->jgcp', cr, pw_re[1:]) - jnp.einsum('gcp,jgp->jgcp', ci, pw_im[1:])
    ca_im = jnp.einsum('gcp,jgp->jgcp', cr, pw_im[1:]) + jnp.einsum('gcp,jgp->jgcp', ci, pw_re[1:])
    c_out = jnp.concatenate(
        [jnp.einsum('jgcp,gh->gpjhc', ca_re, eye).reshape(G * P, J * G * C),
         -jnp.einsum('jgcp,gh->gpjhc', ca_im, eye).reshape(G * P, J * G * C)], axis=0)
    taps = jnp.einsum('gop,kgpc->kgco', cr, ab_re) - jnp.einsum('gop,kgpc->kgco', ci, ab_im)
    none = jnp.zeros_like(taps[0])
    d_io = jnp.stack([jnp.stack([taps[j - i] if j >= i else none for j in range(J)])
                      for i in range(J)])
    d_io = jnp.einsum('ijgco,gh->igcjho', d_io, eye).reshape(J * G * C, J * G * C)
    a_grp = jnp.stack([pw_re[J].reshape(G * P), pw_im[J].reshape(G * P)])
    mag_l = jnp.exp(L * lr * dt)
    a_chunk = jnp.stack([(mag_l * jnp.cos(L * li * dt)).reshape(G * P),
                         (mag_l * jnp.sin(L * li * dt)).reshape(G * P)])
    return w_in.astype(BF16), c_out.astype(BF16), d_io.astype(BF16), a_grp, a_chunk


def _s5_kernel(u0_ref, u1_ref, w_ref, c_ref, dio_ref, a_ref, al_ref, d_ref, wg_ref, mg_ref, o_ref,
               carry_ref, x_s, st_s, y_s, bu_s):
    nb, tb, lanes = u0_ref.shape
    L, J = S5_CHUNK, S5_STEPS
    r = tb // L
    rows = nb * r
    ns = S5_GROUPS * S5_STATE

    @pl.when(pl.program_id(0) == 0)
    def _():
        carry_ref[...] = jnp.zeros_like(carry_ref)

    a_re, a_im = a_ref[0:1, :], a_ref[1:2, :]
    al_re, al_im = al_ref[0:1, :], al_ref[1:2, :]

    def inputs(q):
        parts = []
        for j in range(J):
            sl = pl.ds(q * J + j, r, stride=L)
            parts += [u0_ref[:, sl, :], u1_ref[:, sl, :]]
        return jnp.concatenate(parts, axis=-1).reshape(rows, J * 2 * lanes).astype(BF16)

    def advance(q):
        bu = bu_s[q]
        xr, xi = x_s[:, :ns], x_s[:, ns:]
        x_s[:, :ns] = a_re * xr - a_im * xi + bu[:, :ns]
        x_s[:, ns:] = a_re * xi + a_im * xr + bu[:, ns:]

    x_s[...] = jnp.zeros_like(x_s)
    for q in range(L // J):
        bu_s[q] = jnp.dot(inputs(q), w_ref[...], preferred_element_type=F32)
        advance(q)

    for b in range(nb):
        def hop(c, carry, b=b):
            xr, xi = carry
            row = pl.ds(b * r + c, 1)
            st_s[row, :] = jnp.concatenate([xr, xi], axis=1)
            p = x_s[row, :]
            return (al_re * xr - al_im * xi + p[:, :ns], al_re * xi + al_im * xr + p[:, ns:])

        xr, xi = lax.fori_loop(0, r, hop, (carry_ref[b:b + 1, :ns], carry_ref[b:b + 1, ns:]))
        carry_ref[b:b + 1, :] = jnp.concatenate([xr, xi], axis=1)

    x_s[...] = st_s[...]
    for q in range(L // J):
        y = (jnp.dot(x_s[...].astype(BF16), c_ref[...], preferred_element_type=F32)
             + jnp.dot(inputs(q), dio_ref[...], preferred_element_type=F32))
        for j in range(J):
            sl = pl.ds(q * J + j, r, stride=L)
            lo = j * 2 * lanes
            y_s[0, :, sl, :] = y[:, lo:lo + lanes].reshape(nb, r, lanes)
            y_s[1, :, sl, :] = y[:, lo + lanes:lo + 2 * lanes].reshape(nb, r, lanes)
        if q + 1 < L // J:
            advance(q)

    u = jnp.concatenate([u0_ref[...], u1_ref[...]], axis=-1).reshape(nb * tb, 2 * lanes)
    y = jnp.concatenate([y_s[0], y_s[1]], axis=-1).reshape(nb * tb, 2 * lanes)
    y = jax.nn.gelu(y + d_ref[...] * u)
    z = jnp.dot(y.astype(BF16), wg_ref[...], preferred_element_type=F32)
    out = _rms(z[:, :GROUP_W] * jax.nn.sigmoid(z[:, GROUP_W:]), mg_ref[...])
    o_ref[...] = out.reshape(nb, tb, GROUP_W)


def _s5(proj3, mats, d_skip, w_glu_bf16, merge_g, tb):
    bsz, t, _ = proj3.shape
    w_in, c_out, d_io, a_grp, a_chunk = mats
    ns = S5_GROUPS * S5_STATE
    jw = S5_STEPS * GROUP_W
    lanes = GROUP_W // 2
    rows = bsz * (tb // S5_CHUNK)
    half = lambda j: pl.BlockSpec((bsz, tb, lanes), lambda i, j=j: (0, i, 2 * COL_S5 + j))
    full = lambda a, b: pl.BlockSpec((a, b), lambda i: (0, 0))
    return pl.pallas_call(
        _s5_kernel,
        grid=(t // tb,),
        in_specs=[half(0), half(1), full(jw, 2 * ns), full(2 * ns, jw), full(jw, jw), full(2, ns),
                  full(2, ns), full(1, GROUP_W), full(GROUP_W, 2 * GROUP_W), full(1, GROUP_W)],
        out_specs=pl.BlockSpec((bsz, tb, GROUP_W), lambda i: (0, i, 0)),
        out_shape=jax.ShapeDtypeStruct((bsz, t, GROUP_W), F32),
        scratch_shapes=[pltpu.VMEM((bsz, 2 * ns), F32), pltpu.VMEM((rows, 2 * ns), F32),
                        pltpu.VMEM((rows, 2 * ns), F32), pltpu.VMEM((2, bsz, tb, lanes), F32),
                        pltpu.VMEM((S5_CHUNK // S5_STEPS, rows, 2 * ns), F32)],
        compiler_params=_cparams("arbitrary"),
        name="s5",
    )(proj3, proj3, w_in, c_out, d_io, a_grp, a_chunk, d_skip.reshape(1, -1), w_glu_bf16,
      merge_g.reshape(1, -1))


def _rwkv_kernel(r_s, k_s, v_s, lw_s, a_s, b_s, g_ref, rk_ref, lnw_ref, lnb_ref, mg_ref,
                 o_ref, h_ref, y_s, m_s, n_s, p_s, z_s, *, reset):
    tb = r_s.shape[0]
    ch = RWKV_CHUNK
    nh = GROUP_W // HEAD_W
    if reset:
        h_ref[...] = jnp.zeros_like(h_ref)
        return

    bd_bf16 = _block_diag_mask(GROUP_W, HEAD_W).astype(BF16)

    n4 = nh * ch
    ri = lax.broadcasted_iota(jnp.int32, (n4, n4), 0)
    ci = lax.broadcasted_iota(jnp.int32, (n4, n4), 1)
    same_head = (ri // ch) == (ci // ch)
    strict = same_head & ((ri % ch) > (ci % ch))
    incl = same_head & ((ri % ch) >= (ci % ch))
    eye = (ri == ci).astype(F32)
    blk0 = RWKV_INV_BLOCK
    sizes = [blk0 << i for i in range((ch // blk0).bit_length())]
    same_blk = {b: (ri // b) == (ci // b) for b in sizes}
    hm = ((lax.broadcasted_iota(jnp.int32, (n4, GROUP_W), 0) // ch)
          == (lax.broadcasted_iota(jnp.int32, (n4, GROUP_W), 1) // HEAD_W))
    tri = (lax.broadcasted_iota(jnp.int32, (ch, ch), 0)
           >= lax.broadcasted_iota(jnp.int32, (ch, ch), 1)).astype(BF16)

    def stack(x):
        return jnp.where(hm, jnp.concatenate([x] * nh, axis=0), 0.0)

    def dot(x, y, dims=_NN):
        return lax.dot_general(x.astype(BF16), y.astype(BF16), (dims, ((), ())),
                               preferred_element_type=F32)

    chunks = range(tb // ch)
    pre = []
    for c in chunks:
        sl = slice(c * ch, (c + 1) * ch)
        lw = lw_s[sl, :]
        cl = _exact_lhs_mm(tri, lw)
        cl_end = cl[ch - 1:ch, :]
        e_in, e_ex = jnp.exp(cl), jnp.exp(cl - lw)
        e_neg, e_end = jnp.exp(-cl), jnp.exp(cl_end - cl)
        av, bv, kv, rv, vv = a_s[sl, :], b_s[sl, :], k_s[sl, :], r_s[sl, :], v_s[sl, :]
        pre.append(dict(
            at4=stack(av * e_ex), rt4=stack(rv * e_in), v4=stack(vv),
            bt4=jnp.concatenate([bv * e_neg] * nh, axis=0),
            kt4=jnp.concatenate([kv * e_neg] * nh, axis=0),
            bh4=stack(bv * e_end), kh4=stack(kv * e_end), g_end=jnp.exp(cl_end)))
        yield
    for d in pre:
        ar = jnp.concatenate([d["at4"], d["rt4"]], axis=0)
        sc = dot(ar, jnp.concatenate([d["bt4"], d["kt4"]], axis=0), _NT)
        d["l_ab"] = jnp.where(strict, sc[:n4, :n4], 0.0)
        d["l_ak"] = jnp.where(strict, sc[:n4, n4:], 0.0)
        d["l_rb"] = jnp.where(incl, sc[n4:, :n4], 0.0)
        d["l_rk"] = jnp.where(incl, sc[n4:, n4:], 0.0)
        yield
    for d in pre:
        nb8 = jnp.where(same_blk[blk0], d["l_ab"], 0.0)
        d["tinv"] = eye + nb8
        d["pw"] = dot(nb8, nb8)
        yield
    for d in pre:
        d["tinv"] = d["tinv"] + dot(d["tinv"], d["pw"])
        d["pw"] = dot(d["pw"], d["pw"])
        yield
    for d in pre:
        d["tinv"] = d["tinv"] + dot(d["tinv"], d["pw"])
        yield
    blk = blk0
    while blk < ch:
        for d in pre:
            d["pw"] = dot(jnp.where(same_blk[2 * blk] & ~same_blk[blk], d["l_ab"], 0.0), d["tinv"])
            yield
        for d in pre:
            d["tinv"] = d["tinv"] + dot(d["tinv"], d["pw"])
            yield
        blk *= 2
    for d in pre:
        lv = dot(jnp.concatenate([d["l_ak"], d["l_rk"]], axis=0), d["v4"])
        d["lakv"], d["lrkv"] = lv[:n4], lv[n4:]
        d["khv"] = dot(d["kh4"], d["v4"], _TN)
        yield
    for d in pre:
        d["x12"] = dot(d["tinv"], jnp.concatenate([d["at4"], d["lakv"]], axis=1))
        yield
    for c, d in zip(chunks, pre):
        mn = dot(d["bh4"], d["x12"], _TN)
        pz = dot(d["l_rb"], d["x12"])
        m_s[c] = jnp.where(ri == ci, d["g_end"], 0.0) + mn[:, :GROUP_W]
        n_s[c] = mn[:, GROUP_W:] + d["khv"]
        p_s[c] = d["rt4"] + pz[:, :GROUP_W]
        z_s[c] = pz[:, GROUP_W:] + d["lrkv"]
        yield

    h = h_ref[...]
    for c in range(tb // ch):
        y4 = dot(p_s[c], h) + z_s[c]
        y = y4[0:ch]
        for j in range(1, nh):
            y = y + y4[j * ch:(j + 1) * ch]
        y_s[c * ch:(c + 1) * ch, :] = y
        h = dot(m_s[c], h) + n_s[c]
        yield
    h_ref[...] = h

    y = y_s[...]
    r, k, v = r_s[...], k_s[...], v_s[...]
    mean = _head_sum(y, bd_bf16) * (1.0 / HEAD_W)
    d = y - mean
    var = _head_sum(d * d, bd_bf16) * (1.0 / HEAD_W)
    y = d * lax.rsqrt(var + RWKV_GN_EPS) * lnw_ref[...] + lnb_ref[...]
    y = y + _head_sum(r * k * rk_ref[...], bd_bf16) * v
    o_ref[...] = _rms(y * g_ref[...], mg_ref[...])


def _rwkv_part(proj3, abg3, lyr, p, merge_g, tb):
    bsz, t, _ = proj3.shape
    row = lambda x: x.reshape(1, -1)
    abg = lambda j: _col_spec(tb, j)
    mats = pltpu.VMEM((tb // RWKV_CHUNK, GROUP_W, GROUP_W), F32)
    return dict(
        body=_rwkv_kernel, stages=15 * (tb // RWKV_CHUNK) + 1,
        args=[proj3, proj3, proj3, proj3, abg3, abg3, abg3, row(p["rwkv_r_k"][lyr]),
              row(p["rwkv_ln_w"][lyr]), row(p["rwkv_ln_b"][lyr]), row(merge_g)],
        in_specs=[_col_spec(tb, COL_R), _col_spec(tb, COL_K), _col_spec(tb, COL_V),
                  _col_spec(tb, COL_LORA), abg(EXT_A), abg(EXT_B), abg(EXT_G),
                  _row_spec(), _row_spec(), _row_spec(), _row_spec()],
        out_specs=[_out_blk(tb)], out_shape=[jax.ShapeDtypeStruct((bsz, t, GROUP_W), F32)],
        scratch=[pltpu.VMEM((GROUP_W, GROUP_W), F32), pltpu.VMEM((tb, GROUP_W), F32),
                 mats, mats, mats, mats])


def _bf16_bits(x):
    u = lax.bitcast_convert_type(x, jnp.uint32)
    r = u + jnp.uint32(0x7FFF) + ((u >> 16) & jnp.uint32(1))
    return r & jnp.uint32(0xFFFF0000)


def _slab_copies(rows_ref, slab_hbm, row0, sem):
    tm = rows_ref.shape[0]
    return [(rows_ref.at[:, pl.ds(j * LANES, LANES)], slab_hbm.at[pl.ds(row0, tm), j, :], sem)
            for j in range(slab_hbm.shape[1])]


def _merge_router_kernel(oa_ref, ob_ref, oc_ref, od_ref, h_ref, wo_ref, nw_ref, wr_ref, br_ref,
                         h1_ref, hn_hbm, ids_ref, wts_ref, cnt_ref, pk_s, sem):
    i = pl.program_id(0)
    steps = pl.num_programs(0)
    tm = h_ref.shape[0]
    slot = i % 2

    def writes(s, step):
        return [pltpu.make_async_copy(v, h, m) for v, h, m in
                _slab_copies(pk_s.at[s], hn_hbm, step * tm, sem.at[s])]

    mix = jnp.concatenate([oa_ref[...], ob_ref[...], oc_ref[...], od_ref[...]], axis=1)
    h1 = h_ref[...] + jnp.dot(mix.astype(BF16), wo_ref[...], preferred_element_type=F32)
    h1_ref[...] = h1
    hn = _rms(h1, nw_ref[...])
    half = hn.shape[1] // 2

    @pl.when(i >= 2)
    def _():
        for cp in writes(slot, i - 2):
            cp.wait()

    pk_s[slot] = _bf16_bits(hn[:, :half]) | (_bf16_bits(hn[:, half:]) >> 16)
    for cp in writes(slot, i):
        cp.start()

    @pl.when(i == steps - 1)
    def _():
        for cp in writes(slot, i):
            cp.wait()

    @pl.when((i == steps - 1) & (i >= 1))
    def _():
        for cp in writes(1 - slot, i - 1):
            cp.wait()

    logits = _dg3(hn, wr_ref[...], _NN) + br_ref[...]
    lane = lax.broadcasted_iota(jnp.int32, logits.shape, 1)
    big = jnp.int32(ROUTER_LANES)
    is_c = (lane >= N_EXPERTS) & (lane < N_EXPERTS + MOE_GROUPS)
    cm = jnp.max(jnp.where(is_c, logits, NEG_BIG), axis=-1, keepdims=True)
    gsel = jnp.min(jnp.where(is_c & (logits == cm), lane, big), axis=-1, keepdims=True) - N_EXPERTS
    p_g = 1.0 / jnp.sum(jnp.where(is_c, jnp.exp(logits - cm), 0.0), axis=-1, keepdims=True)
    lo = gsel * MOE_PER_GROUP
    in_g = (lane >= lo) & (lane < lo + MOE_PER_GROUP)
    m1 = jnp.max(jnp.where(in_g, logits, NEG_BIG), axis=-1, keepdims=True)
    i1 = jnp.min(jnp.where(in_g & (logits == m1), lane, big), axis=-1, keepdims=True)
    in_g2 = in_g & (lane != i1)
    m2 = jnp.max(jnp.where(in_g2, logits, NEG_BIG), axis=-1, keepdims=True)
    i2 = jnp.min(jnp.where(in_g2 & (logits == m2), lane, big), axis=-1, keepdims=True)
    w1 = p_g / (1.0 + jnp.exp(m2 - m1))
    w2 = p_g - w1
    two = lax.broadcasted_iota(jnp.int32, ids_ref.shape, 1)
    ids_ref[...] = jnp.where(two == 0, i1, i2)
    wts_ref[...] = jnp.where(two == 0, w1, w2)

    @pl.when(pl.program_id(0) == 0)
    def _():
        cnt_ref[...] = jnp.zeros_like(cnt_ref)

    cnt_ref[...] += jnp.sum(((lane == i1) | (lane == i2)).astype(F32), axis=0, keepdims=True)


def _merge_router(outs, h2d, w_out_bf16, norm_w, w_router, b_router, tm):
    n, d = h2d.shape
    grp = pl.BlockSpec((tm, GROUP_W), lambda i: (i, 0))
    full = lambda a, b: pl.BlockSpec((a, b), lambda i: (0, 0))
    rowblk = lambda w: pl.BlockSpec((tm, w), lambda i: (i, 0))
    return pl.pallas_call(
        _merge_router_kernel,
        grid=(n // tm,),
        in_specs=[grp, grp, grp, grp, rowblk(d), full(d, d), full(1, d), full(d, ROUTER_LANES),
                  full(1, ROUTER_LANES)],
        out_specs=[rowblk(d), pl.BlockSpec(memory_space=pl.ANY),
                   rowblk(MOE_TOPK), rowblk(MOE_TOPK), full(1, ROUTER_LANES)],
        out_shape=[jax.ShapeDtypeStruct((n, d), F32),
                   jax.ShapeDtypeStruct((n, d // 2 // LANES, LANES), jnp.uint32),
                   jax.ShapeDtypeStruct((n, MOE_TOPK), jnp.int32),
                   jax.ShapeDtypeStruct((n, MOE_TOPK), F32),
                   jax.ShapeDtypeStruct((1, ROUTER_LANES), F32)],
        scratch_shapes=[pltpu.VMEM((2, tm, d // 2), jnp.uint32), pltpu.SemaphoreType.DMA((2,))],
        compiler_params=_cparams("arbitrary"),
        name="merge_router",
    )(*[o.reshape(n, GROUP_W) for o in outs], h2d, w_out_bf16, norm_w.reshape(1, d), w_router, b_router)


def _row_copy(src_hbm, src_row, dst_ref, dst_row, sem):
    return pltpu.make_async_copy(src_hbm.at[pl.ds(src_row, 1), :], dst_ref.at[pl.ds(dst_row, 1), :], sem)


def _position_kernel(ids_ref, base_ref, pos_ref, run_ref):
    tm = ids_ref.shape[0]

    @pl.when(pl.program_id(0) == 0)
    def _():
        run_ref[...] = jnp.zeros_like(run_ref)

    i1, i2 = ids_ref[:, 0:1], ids_ref[:, 1:2]
    lane = lax.broadcasted_iota(jnp.int32, (tm, ROUTER_LANES), 1)
    hit = ((lane == i1) | (lane == i2)).astype(BF16)
    earlier = (lax.broadcasted_iota(jnp.int32, (tm, tm), 0)
               > lax.broadcasted_iota(jnp.int32, (tm, tm), 1)).astype(BF16)
    rank = jnp.dot(earlier, hit, preferred_element_type=F32) + run_ref[...]
    where_to = rank + base_ref[...]
    p1 = jnp.sum(jnp.where(lane == i1, where_to, 0.0), axis=-1, keepdims=True)
    p2 = jnp.sum(jnp.where(lane == i2, where_to, 0.0), axis=-1, keepdims=True)
    two = lax.broadcasted_iota(jnp.int32, pos_ref.shape, 1)
    pos_ref[...] = jnp.where(two == 0, p1, p2).astype(jnp.int32)
    run_ref[...] += jnp.sum(hit.astype(F32), axis=0, keepdims=True)


def _positions(ids, group_start, tm):
    n = ids.shape[0]
    return pl.pallas_call(
        _position_kernel,
        grid=(n // tm,),
        in_specs=[pl.BlockSpec((tm, MOE_TOPK), lambda i: (i, 0)),
                  pl.BlockSpec((1, ROUTER_LANES), lambda i: (0, 0))],
        out_specs=pl.BlockSpec((tm, MOE_TOPK), lambda i: (i, 0)),
        out_shape=jax.ShapeDtypeStruct((n, MOE_TOPK), jnp.int32),
        scratch_shapes=[pltpu.VMEM((1, ROUTER_LANES), F32)],
        compiler_params=_cparams("arbitrary"),
        name="moe_positions",
    )(ids, group_start)


def _dispatch_kernel(p0_ref, p1_ref, x_ref, xs_in_hbm, xs_hbm, sem):
    del xs_in_hbm
    n = x_ref.shape[0]

    def copies(i):
        return (pltpu.make_async_copy(x_ref.at[i], xs_hbm.at[p0_ref[i]], sem),
                pltpu.make_async_copy(x_ref.at[i], xs_hbm.at[p1_ref[i]], sem))

    def issue(i, c):
        for prio, cp in enumerate(copies(i)):
            cp.start(priority=prio)
        return c

    def drain(i, c):
        for cp in copies(i):
            cp.wait()
        return c

    lax.fori_loop(0, n, issue, 0, unroll=8)
    lax.fori_loop(0, n, drain, 0, unroll=8)


def _dispatch(x, pos0, pos1, n_rows, chunk):
    n, slabs, lanes = x.shape
    smem = lambda: pl.BlockSpec((chunk,), lambda s: (s,), memory_space=pltpu.SMEM)
    return pl.pallas_call(
        _dispatch_kernel,
        grid=(n // chunk,),
        in_specs=[smem(), smem(), pl.BlockSpec((chunk, slabs, lanes), lambda s: (s, 0, 0)),
                  pl.BlockSpec(memory_space=pl.ANY)],
        out_specs=pl.BlockSpec(memory_space=pl.ANY),
        out_shape=jax.ShapeDtypeStruct((n_rows, slabs, lanes), x.dtype),
        input_output_aliases={3: 0},
        scratch_shapes=[pltpu.SemaphoreType.DMA(())],
        compiler_params=_cparams("arbitrary"),
        name="moe_dispatch",
    )(pos0, pos1, x, jnp.zeros((n_rows, slabs, lanes), x.dtype))


def _expert_kernel(te_ref, nt_ref, xs_hbm, wg_ref, wu_ref, wd_ref, o_ref, wg_s, wu_s, wd_s, x_s, sem):
    i = pl.program_id(0)
    tm = o_ref.shape[0]
    slot = i % 2

    def reads(s, tile):
        return [pltpu.make_async_copy(h, v, m) for v, h, m in
                _slab_copies(x_s.at[s], xs_hbm, tile * tm, sem.at[s])]

    @pl.when(i == 0)
    def _():
        for cp in reads(0, 0):
            cp.start()

    @pl.when(i + 1 < pl.num_programs(0))
    def _():
        for cp in reads(1 - slot, i + 1):
            cp.start()

    @pl.when((i == 0) | (te_ref[i] != te_ref[jnp.maximum(i - 1, 0)]))
    def _():
        wg_s[...] = wg_ref[...].astype(BF16)
        wu_s[...] = wu_ref[...].astype(BF16)
        wd_s[...] = wd_ref[...].astype(BF16)

    for cp in reads(slot, i):
        cp.wait()

    @pl.when(i < nt_ref[0])
    def _():
        p = x_s[slot]
        xa = lax.bitcast_convert_type(p & jnp.uint32(0xFFFF0000), F32).astype(BF16)
        xb = lax.bitcast_convert_type(p << 16, F32).astype(BF16)
        x = jnp.concatenate([xa, xb], axis=1)
        he = (_silu(jnp.dot(x, wg_s[...], preferred_element_type=F32))
              * jnp.dot(x, wu_s[...], preferred_element_type=F32))
        o_ref[...] = jnp.dot(he.astype(BF16), wd_s[...], preferred_element_type=F32)

    @pl.when(i >= nt_ref[0])
    def _():
        o_ref[...] = jnp.zeros_like(o_ref)


def _experts(xs, tile_expert, n_tiles_used, wg, wu, wd, layer, tm):
    p, slabs, lanes = xs.shape
    d = 2 * slabs * lanes
    pick = lambda i, te, nt: (layer, te[i], 0, 0)
    grid_spec = pltpu.PrefetchScalarGridSpec(
        num_scalar_prefetch=2,
        grid=(p // tm,),
        in_specs=[pl.BlockSpec(memory_space=pl.ANY),
                  pl.BlockSpec((None, None, d, D_EXPERT), pick),
                  pl.BlockSpec((None, None, d, D_EXPERT), pick),
                  pl.BlockSpec((None, None, D_EXPERT, d), pick)],
        out_specs=pl.BlockSpec((tm, d), lambda i, te, nt: (i, 0)),
        scratch_shapes=[pltpu.VMEM((d, D_EXPERT), BF16), pltpu.VMEM((d, D_EXPERT), BF16),
                        pltpu.VMEM((D_EXPERT, d), BF16), pltpu.VMEM((2, tm, slabs * lanes), xs.dtype),
                        pltpu.SemaphoreType.DMA((2,))],
    )
    return pl.pallas_call(
        _expert_kernel,
        grid_spec=grid_spec,
        out_shape=jax.ShapeDtypeStruct((p, d), F32),
        compiler_params=_cparams("arbitrary"),
        name="moe_experts",
    )(tile_expert, n_tiles_used, xs, wg, wu, wd)


def _combine_kernel(i0_ref, i1_ref, h1_ref, w_ref, fw_ref, ys_hbm, o_ref, buf0, buf1, sem, *, final_norm):
    n = o_ref.shape[0]

    def issue(i, c):
        _row_copy(ys_hbm, i0_ref[i], buf0, i, sem).start(priority=0)
        _row_copy(ys_hbm, i1_ref[i], buf1, i, sem).start(priority=1)
        return c

    def drain(i, c):
        _row_copy(ys_hbm, 0, buf0, i, sem).wait()
        _row_copy(ys_hbm, 0, buf1, i, sem).wait()
        return c

    lax.fori_loop(0, n, issue, 0, unroll=8)
    lax.fori_loop(0, n, drain, 0, unroll=8)
    out = h1_ref[...] + w_ref[:, 0:1] * buf0[...] + w_ref[:, 1:2] * buf1[...]
    if final_norm:
        out = _rms(out, fw_ref[...])
    o_ref[...] = out


def _combine(ys, pos0, pos1, wts, h1, final_w, final_norm, chunk):
    n, d = h1.shape
    smem = lambda: pl.BlockSpec((chunk,), lambda s: (s,), memory_space=pltpu.SMEM)
    return pl.pallas_call(
        functools.partial(_combine_kernel, final_norm=final_norm),
        grid=(n // chunk,),
        in_specs=[smem(), smem(), pl.BlockSpec((chunk, d), lambda s: (s, 0)),
                  pl.BlockSpec((chunk, MOE_TOPK), lambda s: (s, 0)),
                  pl.BlockSpec((1, d), lambda s: (0, 0)), pl.BlockSpec(memory_space=pl.ANY)],
        out_specs=pl.BlockSpec((chunk, d), lambda s: (s, 0)),
        out_shape=jax.ShapeDtypeStruct((n, d), F32),
        scratch_shapes=[pltpu.VMEM((chunk, d), F32), pltpu.VMEM((chunk, d), F32),
                        pltpu.SemaphoreType.DMA(())],
        compiler_params=_cparams("arbitrary"),
        name="moe_combine",
    )(pos0, pos1, h1, wts, final_w.reshape(1, d), ys)


def _moe(hn_packed, ids, wts, counts, h1, wg, wu, wd, layer, final_w, final_norm, tm):
    n = h1.shape[0]
    n_tiles = (n * MOE_TOPK) // tm + N_EXPERTS
    cnt = counts[0, :N_EXPERTS].astype(jnp.int32)
    padded = ((cnt + tm - 1) // tm) * tm
    ends = jnp.cumsum(padded)
    tile_start = jnp.arange(n_tiles, dtype=jnp.int32) * tm
    tile_expert = jnp.minimum(jnp.sum(ends[None, :] <= tile_start[:, None], axis=1),
                              N_EXPERTS - 1).astype(jnp.int32)
    n_tiles_used = (ends[-1] // tm).astype(jnp.int32).reshape(1)
    group_start = jnp.zeros((1, ROUTER_LANES), F32).at[0, :N_EXPERTS].set((ends - padded).astype(F32))
    pos = _positions(ids, group_start, _pick_div(n, 512))
    pos0, pos1 = pos[:, 0], pos[:, 1]
    xs = _dispatch(hn_packed, pos0, pos1, n_tiles * tm, _pick_div(n, 1024))
    ys = _experts(xs, tile_expert, n_tiles_used, wg, wu, wd, layer, tm)
    return _combine(ys, pos0, pos1, wts, h1, final_w, final_norm, _pick_div(n, 1024))


def _pick_div(n, pref):
    while n % pref:
        pref //= 2
    return pref


def _pick(n, pref):
    return pref if n % pref == 0 else n


def kernel(x, norm1_w, w_in, hgrn_lb_logits, hgrn_norm_w, s5_lambda_re, s5_lambda_im, s5_log_dt, s5_b_re, s5_b_im, s5_c_re, s5_c_im, s5_d, s5_w_glu, rwkv_mu, rwkv_w0, rwkv_w2, rwkv_a0, rwkv_a2, rwkv_g2, rwkv_k_k, rwkv_k_a, rwkv_r_k, rwkv_v0, rwkv_v1, rwkv_v2, rwkv_ln_w, rwkv_ln_b, lru_conv_w, lru_conv_b, lru_wa, lru_ba, lru_wx, lru_bx, lru_lambda, merge_gain, w_out, norm2_w, moe_coarse_w, moe_coarse_b, moe_fine_w, moe_fine_b, moe_w_gate, moe_w_up, moe_w_down, final_norm_w):
    bsz, t, d = x.shape
    n = bsz * t
    depth = w_in.shape[0]
    rw = dict(rwkv_mu=rwkv_mu, rwkv_w0=rwkv_w0, rwkv_w2=rwkv_w2, rwkv_a0=rwkv_a0, rwkv_a2=rwkv_a2,
              rwkv_g2=rwkv_g2, rwkv_k_k=rwkv_k_k, rwkv_k_a=rwkv_k_a, rwkv_r_k=rwkv_r_k,
              rwkv_v0=rwkv_v0, rwkv_v1=rwkv_v1, rwkv_v2=rwkv_v2, rwkv_ln_w=rwkv_ln_w,
              rwkv_ln_b=rwkv_ln_b)
    lb_all = jnp.cumsum(jax.nn.softmax(hgrn_lb_logits.astype(F32), axis=0), axis=0)
    lb_all = lb_all - lb_all[:1]

    tm_proj = _pick(n, 1024)
    tm_in = _pick_div(t, 512)
    tm_moe = 512
    tb_mix = _pick(t, 256)
    tb_s5 = _pick(t, 512)

    h = x.reshape(n, d)
    v_first = None
    for l in range(depth):
        mg = merge_gain[l].reshape(4, GROUP_W)
        proj, abg, v_first = _in_proj(h, norm1_w[l], w_in[l].astype(BF16), v_first, l, rw, t, tm_in)
        proj3 = proj.reshape(bsz, t, -1)
        abg3 = abg.reshape(bsz, t, -1)
        mats = _s5_matrices(s5_lambda_re[l], s5_lambda_im[l], s5_log_dt[l], s5_b_re[l], s5_b_im[l],
                            s5_c_re[l], s5_c_im[l])
        o_b = _s5(proj3, mats, s5_d[l], s5_w_glu[l].astype(BF16), mg[1], tb_s5)
        parts = [_rwkv_part(proj3, abg3, l, rw, mg[2], tb_mix),
                 _hgrn_part(proj3, lb_all[l], hgrn_norm_w[l], mg[0], tb_mix),
                 _lru_part(proj3, lru_conv_w[l], lru_conv_b[l], _block_diag_weight(lru_wa[l]),
                           lru_ba[l], _block_diag_weight(lru_wx[l]), lru_bx[l], lru_lambda[l],
                           mg[3], tb_mix)]
        outs = _mixers(parts, bsz, t // tb_mix, "mixers")
        o_c, o_a, o_d = outs
        w_router = jnp.concatenate(
            [moe_fine_w[l].transpose(1, 0, 2).reshape(d, N_EXPERTS), moe_coarse_w[l],
             jnp.zeros((d, ROUTER_LANES - N_EXPERTS - MOE_GROUPS), F32)], axis=1)
        b_router = jnp.concatenate(
            [moe_fine_b[l].reshape(N_EXPERTS), moe_coarse_b[l],
             jnp.zeros((ROUTER_LANES - N_EXPERTS - MOE_GROUPS,), F32)]).reshape(1, ROUTER_LANES)
        h1, hn, ids, wts, counts = _merge_router((o_a, o_b, o_c, o_d), h, w_out[l].astype(BF16),
                                                 norm2_w[l], w_router, b_router, tm_proj)
        h = _moe(hn, ids, wts, counts, h1, moe_w_gate, moe_w_up, moe_w_down, l,
                 final_norm_w, l == depth - 1, tm_moe)
    return h.reshape(bsz, t, d)
```

```python
import functools

import jax
import jax.numpy as jnp
from jax import lax
from jax.experimental import pallas as pl
from jax.experimental.pallas import tpu as pltpu

F32 = jnp.float32
BF16 = jnp.bfloat16

GROUP_W = 256
RMS_EPS = 1e-6
HEAD_W = 64
HGRN_CHUNK = 32
S5_CH = 16
S5_GROUPS = GROUP_W // S5_CH
S5_STATE = 64
S5_CHUNK = 16
S5_STEPS = 4
RWKV_CHUNK = 64
RWKV_INV_BLOCK = 8
RWKV_GN_EPS = 64e-5
RWKV_W_LORA = 64
RWKV_V_LORA = 32
LRU_CONV = 4
LRU_C = 8.0
MOE_GROUPS = 4
MOE_PER_GROUP = 8
N_EXPERTS = MOE_GROUPS * MOE_PER_GROUP
MOE_TOPK = 2
D_EXPERT = 512
LANES = 128
SUBLANES = 8
ROUTER_LANES = LANES
NEG_BIG = -1e30
V7X_VMEM_BYTES = 64 * 1024 * 1024
VMEM_LIMIT = V7X_VMEM_BYTES - 8 * 1024 * 1024

COL_HQ, COL_HF, COL_HI, COL_HG, COL_S5, COL_R, COL_K, COL_V, COL_LORA, COL_LG, COL_LX = range(11)
EXT_A, EXT_B, EXT_G = range(3)


def _cparams(*sem):
    return pltpu.CompilerParams(dimension_semantics=sem, vmem_limit_bytes=VMEM_LIMIT)


def _mm(a, b):
    return jnp.dot(a.astype(BF16), b.astype(BF16), preferred_element_type=F32)


def _split2(x):
    hi = x.astype(BF16)
    lo = (x - hi.astype(F32)).astype(BF16)
    return hi, lo


def _dg3(a, b, dims):
    ah, al = _split2(a)
    bh, bl = _split2(b)
    d = lambda x, y: lax.dot_general(x, y, (dims, ((), ())), preferred_element_type=F32)
    return d(ah, bh) + d(ah, bl) + d(al, bh)


_NN = ((1,), (0,))
_NT = ((1,), (1,))
_TN = ((0,), (0,))


def _exact_lhs_mm(m_bf16, x):
    h1 = x.astype(BF16)
    r1 = x - h1.astype(F32)
    h2 = r1.astype(BF16)
    h3 = (r1 - h2.astype(F32)).astype(BF16)
    d = lambda y: jnp.dot(m_bf16, y, preferred_element_type=F32)
    return d(h1) + d(h2) + d(h3)


def _head_sum(x, bd_bf16):
    return jnp.dot(x.astype(BF16), bd_bf16, preferred_element_type=F32)


def _block_diag_mask(n, blk):
    r = lax.broadcasted_iota(jnp.int32, (n, n), 0) // blk
    c = lax.broadcasted_iota(jnp.int32, (n, n), 1) // blk
    return r == c


def _rms(x, w):
    return x * lax.rsqrt(jnp.mean(x * x, axis=-1, keepdims=True) + RMS_EPS) * w


def _silu(x):
    return x * jax.nn.sigmoid(x)


def _softplus(x):
    return jnp.maximum(x, 0.0) + jnp.log(1.0 + jnp.exp(-jnp.abs(x)))


def _in_proj_kernel(*refs, has_vmix, tiles_per_seq):
    if has_vmix:
        (x_ref, nw_ref, w_ref, mu_ref, w0_ref, w2_ref, a0_ref, a2_ref, g2_ref, kk_ref, ka_ref,
         vf_ref, v0_ref, v1_ref, v2_ref, o_ref, o2_ref, prev_s) = refs
    else:
        (x_ref, nw_ref, w_ref, mu_ref, w0_ref, w2_ref, a0_ref, a2_ref, g2_ref, kk_ref, ka_ref,
         o_ref, o2_ref, vf_out_ref, prev_s) = refs
    tm = x_ref.shape[0]
    gw = GROUP_W
    c0, c1 = COL_R * gw, (COL_LORA + 1) * gw
    @pl.when((pl.program_id(0) % tiles_per_seq) == 0)
    def _():
        prev_s[...] = jnp.zeros_like(prev_s)

    y = _rms(x_ref[...], nw_ref[...]).astype(BF16)
    raw = jnp.dot(y, w_ref[:, c0:c1], preferred_element_type=F32)

    row0 = lax.broadcasted_iota(jnp.int32, (tm, 1), 0) == 0
    sh = jnp.where(row0, prev_s[...], pltpu.roll(raw, 1, axis=0))
    prev_s[...] = raw[tm - 1:tm, :]
    pf = raw + mu_ref[...] * (sh - raw)
    r, k, v, lo = (pf[:, j * GROUP_W:(j + 1) * GROUP_W] for j in range(4))

    o_ref[:, :c0] = jnp.dot(y, w_ref[:, :c0], preferred_element_type=F32)
    bd_bf16 = _block_diag_mask(GROUP_W, HEAD_W).astype(BF16)
    lo_wa, lo_g = lo[:, :LANES], lo[:, LANES:]
    w_raw = -_softplus(-(w0_ref[...] + _mm(jnp.tanh(lo_wa), w2_ref[...]))) - 0.5
    alr = jax.nn.sigmoid(a0_ref[...] + _mm(lo_wa, a2_ref[...]))
    o_ref[:, c1:] = jnp.dot(y, w_ref[:, c1:], preferred_element_type=F32)
    kkr = k * kk_ref[...]
    kk = kkr / jnp.maximum(jnp.sqrt(_head_sum(kkr * kkr, bd_bf16)), 1e-12)
    if has_vmix:
        gate = jax.nn.sigmoid(v0_ref[...] + _mm(_mm(v, v1_ref[...]), v2_ref[...]))
        v = v + (vf_ref[...] - v) * gate
    else:
        vf_out_ref[...] = v
    o_ref[:, c0:c0 + GROUP_W] = r
    o_ref[:, c0 + GROUP_W:c0 + 2 * GROUP_W] = k * (1.0 + (alr - 1.0) * ka_ref[...])
    o_ref[:, c0 + 2 * GROUP_W:c0 + 3 * GROUP_W] = v
    o_ref[:, c0 + 3 * GROUP_W:c1] = -jnp.exp(w_raw)
    o2_ref[:, EXT_A * gw:(EXT_A + 1) * gw] = -kk
    o2_ref[:, EXT_B * gw:(EXT_B + 1) * gw] = kk * alr
    o2_ref[:, EXT_G * gw:(EXT_G + 1) * gw] = _mm(jax.nn.sigmoid(lo_g), g2_ref[...])


def _pad_rows(w, start, total=LANES):
    out = jnp.zeros((total, w.shape[1]), F32).at[start:start + w.shape[0]].set(w)
    return out.astype(BF16)


def _in_proj(h2d, norm_w, w_bf16, v_first, lyr, p, t, tm):
    n, d = h2d.shape
    d_in = w_bf16.shape[1]
    has_vmix = v_first is not None
    row = lambda x: x.reshape(1, -1)
    full = lambda a, b: pl.BlockSpec((a, b), lambda i: (0, 0))
    rowblk = lambda w: pl.BlockSpec((tm, w), lambda i: (i, 0))
    args = [h2d, row(norm_w), w_bf16, row(p["rwkv_mu"][lyr]), row(p["rwkv_w0"][lyr]),
            _pad_rows(p["rwkv_w2"][lyr], 0), row(p["rwkv_a0"][lyr]),
            _pad_rows(p["rwkv_a2"][lyr], RWKV_W_LORA), p["rwkv_g2"][lyr].astype(BF16),
            row(p["rwkv_k_k"][lyr]), row(p["rwkv_k_a"][lyr])]
    specs = [rowblk(d), full(1, d), full(d, d_in), full(1, 4 * GROUP_W), full(1, GROUP_W),
             full(LANES, GROUP_W), full(1, GROUP_W), full(LANES, GROUP_W), full(LANES, GROUP_W),
             full(1, GROUP_W), full(1, GROUP_W)]
    out_specs = [rowblk(d_in), rowblk(3 * GROUP_W)]
    out_shape = [jax.ShapeDtypeStruct((n, d_in), F32), jax.ShapeDtypeStruct((n, 3 * GROUP_W), F32)]
    if has_vmix:
        v1 = jnp.zeros((GROUP_W, LANES), F32).at[:, :RWKV_V_LORA].set(p["rwkv_v1"][lyr - 1]).astype(BF16)
        v2 = jnp.zeros((LANES, GROUP_W), F32).at[:RWKV_V_LORA].set(p["rwkv_v2"][lyr - 1]).astype(BF16)
        args += [v_first, row(p["rwkv_v0"][lyr - 1]), v1, v2]
        specs += [rowblk(GROUP_W), full(1, GROUP_W), full(GROUP_W, LANES), full(LANES, GROUP_W)]
    else:
        out_specs.append(rowblk(GROUP_W))
        out_shape.append(jax.ShapeDtypeStruct((n, GROUP_W), F32))
    res = pl.pallas_call(
        functools.partial(_in_proj_kernel, has_vmix=has_vmix, tiles_per_seq=t // tm),
        grid=(n // tm,),
        in_specs=specs,
        out_specs=out_specs,
        out_shape=out_shape,
        scratch_shapes=[pltpu.VMEM((1, 4 * GROUP_W), F32)],
        compiler_params=_cparams("arbitrary"),
        name="in_proj",
    )(*args)
    return res[0], res[1], (v_first if has_vmix else res[2])


def _col_spec(tb, col):
    return pl.BlockSpec((None, tb, GROUP_W), lambda b, t: (b, t, col))


def _row_spec(width=GROUP_W):
    return pl.BlockSpec((1, width), lambda b, t: (0, 0))


def _full_spec(shape):
    return pl.BlockSpec(shape, lambda b, t: (0,) * len(shape))


def _hgrn_kernel(q_ref, f_ref, i_ref, g_ref, lb_ref, nw_ref, mg_ref, o_ref,
                 st_ref, q_s, k_s, v_s, lf_s, o_s, *, reset):
    ch = HGRN_CHUNK
    tb = q_ref.shape[0]
    if reset:
        st_ref[...] = jnp.zeros_like(st_ref)
        return

    lb = lb_ref[...]
    fx = f_ref[...]
    x1 = jnp.log(lb)
    x2 = jnp.log(1.0 - lb) - _softplus(-fx)
    m = jnp.maximum(x1, x2)
    lf_s[...] = m + jnp.log(jnp.exp(x1 - m) + jnp.exp(x2 - m))
    q_s[...] = _silu(q_ref[...])
    k_s[...] = (1.0 - lb) * jax.nn.sigmoid(-fx)
    v_s[...] = _silu(i_ref[...])
    yield

    bd = _block_diag_mask(GROUP_W, HEAD_W)
    bd_bf16 = bd.astype(BF16)
    tri = (lax.broadcasted_iota(jnp.int32, (ch, ch), 0)
           >= lax.broadcasted_iota(jnp.int32, (ch, ch), 1))
    tri_bf16 = tri.astype(BF16)
    tri3 = (lax.broadcasted_iota(jnp.int32, (ch, ch, 1), 0)
            <= lax.broadcasted_iota(jnp.int32, (ch, ch, 1), 1))

    st = st_ref[...]
    for c in range(tb // ch):
        sl = slice(c * ch, (c + 1) * ch)
        qc, kc, vc = q_s[sl, :], k_s[sl, :], v_s[sl, :]
        b = _exact_lhs_mm(tri_bf16, lf_s[sl, :])
        rel = b[None, :, :] - b[:, None, :]
        dec = jnp.exp(jnp.where(tri3, rel, NEG_BIG))
        p = (qc[None, :, :] * kc[:, None, :]) * dec
        sc = jnp.dot(p.reshape(ch * ch, GROUP_W).astype(BF16), bd_bf16,
                     preferred_element_type=F32).reshape(ch, ch, GROUP_W)
        o_intra = jnp.sum(sc * vc[:, None, :], axis=0)
        o_inter = lax.dot_general((qc * jnp.exp(b)).astype(BF16), st.astype(BF16),
                                  (_NT, ((), ())), preferred_element_type=F32)
        b_end = b[ch - 1:ch, :]
        kh = kc * jnp.exp(b_end - b)
        upd = lax.dot_general(vc.astype(BF16), kh.astype(BF16), (_TN, ((), ())),
                              preferred_element_type=F32)
        st = st * jnp.exp(b_end) + jnp.where(bd, upd, 0.0)
        o_s[sl, :] = o_intra + o_inter
        yield
    st_ref[...] = st

    o = o_s[...]
    ms = _head_sum(o * o, bd_bf16) * (1.0 / HEAD_W)
    o = o * lax.rsqrt(ms + RMS_EPS) * nw_ref[...] * _silu(g_ref[...])
    o_ref[...] = _rms(o, mg_ref[...])


def _out_blk(tb):
    return pl.BlockSpec((None, tb, GROUP_W), lambda b, i: (b, i, 0))


def _hgrn_part(proj3, lb, norm_w, merge_g, tb):
    bsz, t, _ = proj3.shape
    blk = pltpu.VMEM((tb, GROUP_W), F32)
    return dict(
        body=_hgrn_kernel, stages=tb // HGRN_CHUNK + 2,
        args=[proj3, proj3, proj3, proj3, lb.reshape(1, -1), norm_w.reshape(1, -1), merge_g.reshape(1, -1)],
        in_specs=[_col_spec(tb, COL_HQ), _col_spec(tb, COL_HF), _col_spec(tb, COL_HI),
                  _col_spec(tb, COL_HG), _row_spec(), _row_spec(), _row_spec()],
        out_specs=[_out_blk(tb)],
        out_shape=[jax.ShapeDtypeStruct((bsz, t, GROUP_W), F32)],
        scratch=[pltpu.VMEM((GROUP_W, GROUP_W), F32), blk, blk, blk, blk, blk])


def _mixer_kernel(*refs, parts):
    groups, i = [], 0
    for kind in range(3):
        for body, counts in parts:
            groups.append(refs[i:i + counts[kind]])
            i += counts[kind]
    k = len(parts)
    per_part = [groups[j] + groups[k + j] + groups[2 * k + j] for j in range(k)]

    @pl.when(pl.program_id(1) == 0)
    def _():
        for (body, _), r in zip(parts, per_part):
            for _step in body(*r, reset=True):
                pass

    runs = [[body(*r, reset=False), 0, counts[3]] for (body, counts), r in zip(parts, per_part)]
    while runs:
        run = min(runs, key=lambda x: x[1] / x[2])
        try:
            next(run[0])
            run[1] += 1
        except StopIteration:
            runs.remove(run)


def _mixers(parts, bsz, n_tblocks, name):
    cat = lambda key: [x for p in parts for x in p[key]]
    light = tuple((p["body"], (len(p["args"]), len(p["out_specs"]), len(p["scratch"]), p["stages"]))
                  for p in parts)
    return pl.pallas_call(
        functools.partial(_mixer_kernel, parts=light),
        grid=(bsz, n_tblocks),
        in_specs=cat("in_specs"),
        out_specs=cat("out_specs"),
        out_shape=cat("out_shape"),
        scratch_shapes=cat("scratch"),
        compiler_params=_cparams("parallel", "arbitrary"),
        name=name,
    )(*cat("args"))


def _lru_kernel(xg_ref, xr_ref, cw_ref, cb_ref, wa_ref, ba_ref, wx_ref, bx_ref, lam_ref, mg_ref,
                o_ref, buf_ref, h_ref, *, reset):
    tb = xr_ref.shape[0]
    pad = SUBLANES
    if reset:
        buf_ref[0:pad, :] = jnp.zeros((pad, GROUP_W), F32)
        h_ref[...] = jnp.zeros_like(h_ref)
        return

    xr = xr_ref[...]
    buf_ref[pad:pad + tb, :] = xr
    xc = cb_ref[...] + jnp.zeros_like(xr)
    for j in range(LRU_CONV):
        xc = xc + cw_ref[j:j + 1, :] * buf_ref[pl.ds(pad - (LRU_CONV - 1) + j, tb), :]
    buf_ref[0:pad, :] = xr[tb - pad:tb, :]

    r = jax.nn.sigmoid(jnp.dot(xc.astype(BF16), wa_ref[...], preferred_element_type=F32) + ba_ref[...])
    gi = jax.nn.sigmoid(jnp.dot(xc.astype(BF16), wx_ref[...], preferred_element_type=F32) + bx_ref[...])
    log_a = -LRU_C * r * _softplus(-lam_ref[...])
    a = jnp.exp(log_a)
    x = jnp.sqrt(1.0 - jnp.exp(2.0 * log_a)) * (gi * xc)
    yield

    rows = lax.broadcasted_iota(jnp.int32, (tb, 1), 0)
    k = 1
    while k < tb:
        keep = rows >= k
        x = x + jnp.where(keep, a * pltpu.roll(x, k, axis=0), 0.0)
        a = jnp.where(keep, a * pltpu.roll(a, k, axis=0), a)
        k *= 2
        yield
    h = x + a * h_ref[...]
    h_ref[...] = h[tb - 1:tb, :]
    o_ref[...] = _rms(jax.nn.gelu(xg_ref[...]) * h, mg_ref[...])


def _lru_part(proj3, conv_w, conv_b, wa_bd, ba, wx_bd, bx, lam, merge_g, tb):
    bsz, t, _ = proj3.shape
    return dict(
        body=_lru_kernel, stages=tb.bit_length() + 1,
        args=[proj3, proj3, conv_w, conv_b.reshape(1, -1), wa_bd, ba.reshape(1, -1), wx_bd,
              bx.reshape(1, -1), lam.reshape(1, -1), merge_g.reshape(1, -1)],
        in_specs=[_col_spec(tb, COL_LG), _col_spec(tb, COL_LX), _full_spec((LRU_CONV, GROUP_W)),
                  _row_spec(), _full_spec((GROUP_W, GROUP_W)), _row_spec(),
                  _full_spec((GROUP_W, GROUP_W)), _row_spec(), _row_spec(), _row_spec()],
        out_specs=[_out_blk(tb)],
        out_shape=[jax.ShapeDtypeStruct((bsz, t, GROUP_W), F32)],
        scratch=[pltpu.VMEM((tb + SUBLANES, GROUP_W), F32), pltpu.VMEM((1, GROUP_W), F32)])


def _block_diag_weight(w):
    h, n, _ = w.shape
    eye = jnp.eye(h, dtype=w.dtype)
    return jnp.einsum('hij,hg->higj', w, eye).reshape(h * n, h * n).astype(BF16)


def _s5_matrices(lam_re, lam_im, log_dt, b_re, b_im, c_re, c_im):
    L, G, P, C = S5_CHUNK, S5_GROUPS, S5_STATE, S5_CH
    lr, li = lam_re.astype(F32), lam_im.astype(F32)
    dt = jnp.exp(log_dt.astype(F32))[:, None]
    mag = jnp.exp(lr * dt)
    a_re, a_im = mag * jnp.cos(li * dt), mag * jnp.sin(li * dt)
    den = lr * lr + li * li
    kap_re = ((a_re - 1.0) * lr + a_im * li) / den
    kap_im = (a_im * lr - (a_re - 1.0) * li) / den
    br, bi = b_re.astype(F32), b_im.astype(F32)
    bb_re = kap_re[..., None] * br - kap_im[..., None] * bi
    bb_im = kap_re[..., None] * bi + kap_im[..., None] * br
    cr, ci = c_re.astype(F32), c_im.astype(F32)
    eye = jnp.eye(G, dtype=F32)
    J = S5_STEPS
    kk = jnp.arange(J + 1, dtype=F32)[:, None, None]
    pmag = jnp.exp(kk * (lr * dt)[None])
    pw_re, pw_im = pmag * jnp.cos(kk * (li * dt)[None]), pmag * jnp.sin(kk * (li * dt)[None])
    ab_re = pw_re[:J, :, :, None] * bb_re[None] - pw_im[:J, :, :, None] * bb_im[None]
    ab_im = pw_re[:J, :, :, None] * bb_im[None] + pw_im[:J, :, :, None] * bb_re[None]
    rev = lambda x: jnp.stack([x[J - 1 - j] for j in range(J)])
    w_in = jnp.concatenate(
        [jnp.einsum('jgpc,gh->jgchp', rev(ab_re), eye).reshape(J * G * C, G * P),
         jnp.einsum('jgpc,gh->jgchp', rev(ab_im), eye).reshape(J * G * C, G * P)], axis=1)
    ca_re = jnp.einsum('gcp,jgp->jgcp', cr, pw_re[1:]) - jnp.einsum('gcp,jgp->jgcp', ci, pw_im[1:])
    ca_im = jnp.einsum('gcp,jgp->jgcp', cr, pw_im[1:]) + jnp.einsum('gcp,jgp->jgcp', ci, pw_re[1:])
    c_out = jnp.concatenate(
        [jnp.einsum('jgcp,gh->gpjhc', ca_re, eye).reshape(G * P, J * G * C),
         -jnp.einsum('jgcp,gh->gpjhc', ca_im, eye).reshape(G * P, J * G * C)], axis=0)
    taps = jnp.einsum('gop,kgpc->kgco', cr, ab_re) - jnp.einsum('gop,kgpc->kgco', ci, ab_im)
    none = jnp.zeros_like(taps[0])
    d_io = jnp.stack([jnp.stack([taps[j - i] if j >= i else none for j in range(J)])
                      for i in range(J)])
    d_io = jnp.einsum('ijgco,gh->igcjho', d_io, eye).reshape(J * G * C, J * G * C)
    a_grp = jnp.stack([pw_re[J].reshape(G * P), pw_im[J].reshape(G * P)])
    mag_l = jnp.exp(L * lr * dt)
    a_chunk = jnp.stack([(mag_l * jnp.cos(L * li * dt)).reshape(G * P),
                         (mag_l * jnp.sin(L * li * dt)).reshape(G * P)])
    return w_in.astype(BF16), c_out.astype(BF16), d_io.astype(BF16), a_grp, a_chunk


def _s5_kernel(u0_ref, u1_ref, w_ref, c_ref, dio_ref, a_ref, al_ref, d_ref, wg_ref, mg_ref, o_ref,
               carry_ref, x_s, st_s, y_s, bu_s):
    nb, tb, lanes = u0_ref.shape
    L, J = S5_CHUNK, S5_STEPS
    r = tb // L
    rows = nb * r
    ns = S5_GROUPS * S5_STATE

    @pl.when(pl.program_id(0) == 0)
    def _():
        carry_ref[...] = jnp.zeros_like(carry_ref)

    a_re, a_im = a_ref[0:1, :], a_ref[1:2, :]
    al_re, al_im = al_ref[0:1, :], al_ref[1:2, :]

    def inputs(q):
        parts = []
        for j in range(J):
            sl = pl.ds(q * J + j, r, stride=L)
            parts += [u0_ref[:, sl, :], u1_ref[:, sl, :]]
        return jnp.concatenate(parts, axis=-1).reshape(rows, J * 2 * lanes).astype(BF16)

    def advance(q):
        bu = bu_s[q]
        xr, xi = x_s[:, :ns], x_s[:, ns:]
        x_s[:, :ns] = a_re * xr - a_im * xi + bu[:, :ns]
        x_s[:, ns:] = a_re * xi + a_im * xr + bu[:, ns:]

    x_s[...] = jnp.zeros_like(x_s)
    for q in range(L // J):
        bu_s[q] = jnp.dot(inputs(q), w_ref[...], preferred_element_type=F32)
        advance(q)

    for b in range(nb):
        def hop(c, carry, b=b):
            xr, xi = carry
            row = pl.ds(b * r + c, 1)
            st_s[row, :] = jnp.concatenate([xr, xi], axis=1)
            p = x_s[row, :]
            return (al_re * xr - al_im * xi + p[:, :ns], al_re * xi + al_im * xr + p[:, ns:])

        xr, xi = lax.fori_loop(0, r, hop, (carry_ref[b:b + 1, :ns], carry_ref[b:b + 1, ns:]))
        carry_ref[b:b + 1, :] = jnp.concatenate([xr, xi], axis=1)

    x_s[...] = st_s[...]
    for q in range(L // J):
        y = (jnp.dot(x_s[...].astype(BF16), c_ref[...], preferred_element_type=F32)
             + jnp.dot(inputs(q), dio_ref[...], preferred_element_type=F32))
        for j in range(J):
            sl = pl.ds(q * J + j, r, stride=L)
            lo = j * 2 * lanes
            y_s[0, :, sl, :] = y[:, lo:lo + lanes].reshape(nb, r, lanes)
            y_s[1, :, sl, :] = y[:, lo + lanes:lo + 2 * lanes].reshape(nb, r, lanes)
        if q + 1 < L // J:
            advance(q)

    u = jnp.concatenate([u0_ref[...], u1_ref[...]], axis=-1).reshape(nb * tb, 2 * lanes)
    y = jnp.concatenate([y_s[0], y_s[1]], axis=-1).reshape(nb * tb, 2 * lanes)
    y = jax.nn.gelu(y + d_ref[...] * u)
    z = jnp.dot(y.astype(BF16), wg_ref[...], preferred_element_type=F32)
    out = _rms(z[:, :GROUP_W] * jax.nn.sigmoid(z[:, GROUP_W:]), mg_ref[...])
    o_ref[...] = out.reshape(nb, tb, GROUP_W)


def _s5(proj3, mats, d_skip, w_glu_bf16, merge_g, tb):
    bsz, t, _ = proj3.shape
    w_in, c_out, d_io, a_grp, a_chunk = mats
    ns = S5_GROUPS * S5_STATE
    jw = S5_STEPS * GROUP_W
    lanes = GROUP_W // 2
    rows = bsz * (tb // S5_CHUNK)
    half = lambda j: pl.BlockSpec((bsz, tb, lanes), lambda i, j=j: (0, i, 2 * COL_S5 + j))
    full = lambda a, b: pl.BlockSpec((a, b), lambda i: (0, 0))
    return pl.pallas_call(
        _s5_kernel,
        grid=(t // tb,),
        in_specs=[half(0), half(1), full(jw, 2 * ns), full(2 * ns, jw), full(jw, jw), full(2, ns),
                  full(2, ns), full(1, GROUP_W), full(GROUP_W, 2 * GROUP_W), full(1, GROUP_W)],
        out_specs=pl.BlockSpec((bsz, tb, GROUP_W), lambda i: (0, i, 0)),
        out_shape=jax.ShapeDtypeStruct((bsz, t, GROUP_W), F32),
        scratch_shapes=[pltpu.VMEM((bsz, 2 * ns), F32), pltpu.VMEM((rows, 2 * ns), F32),
                        pltpu.VMEM((rows, 2 * ns), F32), pltpu.VMEM((2, bsz, tb, lanes), F32),
                        pltpu.VMEM((S5_CHUNK // S5_STEPS, rows, 2 * ns), F32)],
        compiler_params=_cparams("arbitrary"),
        name="s5",
    )(proj3, proj3, w_in, c_out, d_io, a_grp, a_chunk, d_skip.reshape(1, -1), w_glu_bf16,
      merge_g.reshape(1, -1))


def _rwkv_kernel(r_s, k_s, v_s, lw_s, a_s, b_s, g_ref, rk_ref, lnw_ref, lnb_ref, mg_ref,
                 o_ref, h_ref, y_s, m_s, n_s, p_s, z_s, *, reset):
    tb = r_s.shape[0]
    ch = RWKV_CHUNK
    nh = GROUP_W // HEAD_W
    if reset:
        h_ref[...] = jnp.zeros_like(h_ref)
        return

    bd_bf16 = _block_diag_mask(GROUP_W, HEAD_W).astype(BF16)

    n4 = nh * ch
    ri = lax.broadcasted_iota(jnp.int32, (n4, n4), 0)
    ci = lax.broadcasted_iota(jnp.int32, (n4, n4), 1)
    same_head = (ri // ch) == (ci // ch)
    strict = same_head & ((ri % ch) > (ci % ch))
    incl = same_head & ((ri % ch) >= (ci % ch))
    eye = (ri == ci).astype(F32)
    blk0 = RWKV_INV_BLOCK
    sizes = [blk0 << i for i in range((ch // blk0).bit_length())]
    same_blk = {b: (ri // b) == (ci // b) for b in sizes}
    hm = ((lax.broadcasted_iota(jnp.int32, (n4, GROUP_W), 0) // ch)
          == (lax.broadcasted_iota(jnp.int32, (n4, GROUP_W), 1) // HEAD_W))
    tri = (lax.broadcasted_iota(jnp.int32, (ch, ch), 0)
           >= lax.broadcasted_iota(jnp.int32, (ch, ch), 1)).astype(BF16)

    def stack(x):
        return jnp.where(hm, jnp.concatenate([x] * nh, axis=0), 0.0)

    def dot(x, y, dims=_NN):
        return lax.dot_general(x.astype(BF16), y.astype(BF16), (dims, ((), ())),
                               preferred_element_type=F32)

    chunks = range(tb // ch)
    pre = []
    for c in chunks:
        sl = slice(c * ch, (c + 1) * ch)
        lw = lw_s[sl, :]
        cl = _exact_lhs_mm(tri, lw)
        cl_end = cl[ch - 1:ch, :]
        e_in, e_ex = jnp.exp(cl), jnp.exp(cl - lw)
        e_neg, e_end = jnp.exp(-cl), jnp.exp(cl_end - cl)
        av, bv, kv, rv, vv = a_s[sl, :], b_s[sl, :], k_s[sl, :], r_s[sl, :], v_s[sl, :]
        pre.append(dict(
            at4=stack(av * e_ex), rt4=stack(rv * e_in), v4=stack(vv),
            bt4=jnp.concatenate([bv * e_neg] * nh, axis=0),
            kt4=jnp.concatenate([kv * e_neg] * nh, axis=0),
            bh4=stack(bv * e_end), kh4=stack(kv * e_end), g_end=jnp.exp(cl_end)))
        yield
    for d in pre:
        ar = jnp.concatenate([d["at4"], d["rt4"]], axis=0)
        sc = dot(ar, jnp.concatenate([d["bt4"], d["kt4"]], axis=0), _NT)
        d["l_ab"] = jnp.where(strict, sc[:n4, :n4], 0.0)
        d["l_ak"] = jnp.where(strict, sc[:n4, n4:], 0.0)
        d["l_rb"] = jnp.where(incl, sc[n4:, :n4], 0.0)
        d["l_rk"] = jnp.where(incl, sc[n4:, n4:], 0.0)
        yield
    for d in pre:
        nb8 = jnp.where(same_blk[blk0], d["l_ab"], 0.0)
        d["tinv"] = eye + nb8
        d["pw"] = dot(nb8, nb8)
        yield
    for d in pre:
        d["tinv"] = d["tinv"] + dot(d["tinv"], d["pw"])
        d["pw"] = dot(d["pw"], d["pw"])
        yield
    for d in pre:
        d["tinv"] = d["tinv"] + dot(d["tinv"], d["pw"])
        yield
    blk = blk0
    while blk < ch:
        for d in pre:
            d["pw"] = dot(jnp.where(same_blk[2 * blk] & ~same_blk[blk], d["l_ab"], 0.0), d["tinv"])
            yield
        for d in pre:
            d["tinv"] = d["tinv"] + dot(d["tinv"], d["pw"])
            yield
        blk *= 2
    for d in pre:
        lv = dot(jnp.concatenate([d["l_ak"], d["l_rk"]], axis=0), d["v4"])
        d["lakv"], d["lrkv"] = lv[:n4], lv[n4:]
        d["khv"] = dot(d["kh4"], d["v4"], _TN)
        yield
    for d in pre:
        d["x12"] = dot(d["tinv"], jnp.concatenate([d["at4"], d["lakv"]], axis=1))
        yield
    for c, d in zip(chunks, pre):
        mn = dot(d["bh4"], d["x12"], _TN)
        pz = dot(d["l_rb"], d["x12"])
        m_s[c] = jnp.where(ri == ci, d["g_end"], 0.0) + mn[:, :GROUP_W]
        n_s[c] = mn[:, GROUP_W:] + d["khv"]
        p_s[c] = d["rt4"] + pz[:, :GROUP_W]
        z_s[c] = pz[:, GROUP_W:] + d["lrkv"]
        yield

    h = h_ref[...]
    for c in range(tb // ch):
        y4 = dot(p_s[c], h) + z_s[c]
        y = y4[0:ch]
        for j in range(1, nh):
            y = y + y4[j * ch:(j + 1) * ch]
        y_s[c * ch:(c + 1) * ch, :] = y
        h = dot(m_s[c], h) + n_s[c]
        yield
    h_ref[...] = h

    y = y_s[...]
    r, k, v = r_s[...], k_s[...], v_s[...]
    mean = _head_sum(y, bd_bf16) * (1.0 / HEAD_W)
    d = y - mean
    var = _head_sum(d * d, bd_bf16) * (1.0 / HEAD_W)
    y = d * lax.rsqrt(var + RWKV_GN_EPS) * lnw_ref[...] + lnb_ref[...]
    y = y + _head_sum(r * k * rk_ref[...], bd_bf16) * v
    o_ref[...] = _rms(y * g_ref[...], mg_ref[...])


def _rwkv_part(proj3, abg3, lyr, p, merge_g, tb):
    bsz, t, _ = proj3.shape
    row = lambda x: x.reshape(1, -1)
    abg = lambda j: _col_spec(tb, j)
    mats = pltpu.VMEM((tb // RWKV_CHUNK, GROUP_W, GROUP_W), F32)
    return dict(
        body=_rwkv_kernel, stages=15 * (tb // RWKV_CHUNK) + 1,
        args=[proj3, proj3, proj3, proj3, abg3, abg3, abg3, row(p["rwkv_r_k"][lyr]),
              row(p["rwkv_ln_w"][lyr]), row(p["rwkv_ln_b"][lyr]), row(merge_g)],
        in_specs=[_col_spec(tb, COL_R), _col_spec(tb, COL_K), _col_spec(tb, COL_V),
                  _col_spec(tb, COL_LORA), abg(EXT_A), abg(EXT_B), abg(EXT_G),
                  _row_spec(), _row_spec(), _row_spec(), _row_spec()],
        out_specs=[_out_blk(tb)], out_shape=[jax.ShapeDtypeStruct((bsz, t, GROUP_W), F32)],
        scratch=[pltpu.VMEM((GROUP_W, GROUP_W), F32), pltpu.VMEM((tb, GROUP_W), F32),
                 mats, mats, mats, mats])


def _bf16_bits(x):
    u = lax.bitcast_convert_type(x, jnp.uint32)
    r = u + jnp.uint32(0x7FFF) + ((u >> 16) & jnp.uint32(1))
    return r & jnp.uint32(0xFFFF0000)


def _slab_copies(rows_ref, slab_hbm, row0, sem):
    tm = rows_ref.shape[0]
    return [(rows_ref.at[:, pl.ds(j * LANES, LANES)], slab_hbm.at[pl.ds(row0, tm), j, :], sem)
            for j in range(slab_hbm.shape[1])]


def _merge_router_kernel(oa_ref, ob_ref, oc_ref, od_ref, h_ref, wo_ref, nw_ref, wr_ref, br_ref,
                         h1_ref, hn_hbm, ids_ref, wts_ref, cnt_ref, pk_s, sem):
    i = pl.program_id(0)
    steps = pl.num_programs(0)
    tm = h_ref.shape[0]
    slot = i % 2

    def writes(s, step):
        return [pltpu.make_async_copy(v, h, m) for v, h, m in
                _slab_copies(pk_s.at[s], hn_hbm, step * tm, sem.at[s])]

    mix = jnp.concatenate([oa_ref[...], ob_ref[...], oc_ref[...], od_ref[...]], axis=1)
    h1 = h_ref[...] + jnp.dot(mix.astype(BF16), wo_ref[...], preferred_element_type=F32)
    h1_ref[...] = h1
    hn = _rms(h1, nw_ref[...])
    half = hn.shape[1] // 2

    @pl.when(i >= 2)
    def _():
        for cp in writes(slot, i - 2):
            cp.wait()

    pk_s[slot] = _bf16_bits(hn[:, :half]) | (_bf16_bits(hn[:, half:]) >> 16)
    for cp in writes(slot, i):
        cp.start()

    @pl.when(i == steps - 1)
    def _():
        for cp in writes(slot, i):
            cp.wait()

    @pl.when((i == steps - 1) & (i >= 1))
    def _():
        for cp in writes(1 - slot, i - 1):
            cp.wait()

    logits = _dg3(hn, wr_ref[...], _NN) + br_ref[...]
    lane = lax.broadcasted_iota(jnp.int32, logits.shape, 1)
    big = jnp.int32(ROUTER_LANES)
    is_c = (lane >= N_EXPERTS) & (lane < N_EXPERTS + MOE_GROUPS)
    cm = jnp.max(jnp.where(is_c, logits, NEG_BIG), axis=-1, keepdims=True)
    gsel = jnp.min(jnp.where(is_c & (logits == cm), lane, big), axis=-1, keepdims=True) - N_EXPERTS
    p_g = 1.0 / jnp.sum(jnp.where(is_c, jnp.exp(logits - cm), 0.0), axis=-1, keepdims=True)
    lo = gsel * MOE_PER_GROUP
    in_g = (lane >= lo) & (lane < lo + MOE_PER_GROUP)
    m1 = jnp.max(jnp.where(in_g, logits, NEG_BIG), axis=-1, keepdims=True)
    i1 = jnp.min(jnp.where(in_g & (logits == m1), lane, big), axis=-1, keepdims=True)
    in_g2 = in_g & (lane != i1)
    m2 = jnp.max(jnp.where(in_g2, logits, NEG_BIG), axis=-1, keepdims=True)
    i2 = jnp.min(jnp.where(in_g2 & (logits == m2), lane, big), axis=-1, keepdims=True)
    w1 = p_g / (1.0 + jnp.exp(m2 - m1))
    w2 = p_g - w1
    two = lax.broadcasted_iota(jnp.int32, ids_ref.shape, 1)
    ids_ref[...] = jnp.where(two == 0, i1, i2)
    wts_ref[...] = jnp.where(two == 0, w1, w2)

    @pl.when(pl.program_id(0) == 0)
    def _():
        cnt_ref[...] = jnp.zeros_like(cnt_ref)

    cnt_ref[...] += jnp.sum(((lane == i1) | (lane == i2)).astype(F32), axis=0, keepdims=True)


def _merge_router(outs, h2d, w_out_bf16, norm_w, w_router, b_router, tm):
    n, d = h2d.shape
    grp = pl.BlockSpec((tm, GROUP_W), lambda i: (i, 0))
    full = lambda a, b: pl.BlockSpec((a, b), lambda i: (0, 0))
    rowblk = lambda w: pl.BlockSpec((tm, w), lambda i: (i, 0))
    return pl.pallas_call(
        _merge_router_kernel,
        grid=(n // tm,),
        in_specs=[grp, grp, grp, grp, rowblk(d), full(d, d), full(1, d), full(d, ROUTER_LANES),
                  full(1, ROUTER_LANES)],
        out_specs=[rowblk(d), pl.BlockSpec(memory_space=pl.ANY),
                   rowblk(MOE_TOPK), rowblk(MOE_TOPK), full(1, ROUTER_LANES)],
        out_shape=[jax.ShapeDtypeStruct((n, d), F32),
                   jax.ShapeDtypeStruct((n, d // 2 // LANES, LANES), jnp.uint32),
                   jax.ShapeDtypeStruct((n, MOE_TOPK), jnp.int32),
                   jax.ShapeDtypeStruct((n, MOE_TOPK), F32),
                   jax.ShapeDtypeStruct((1, ROUTER_LANES), F32)],
        scratch_shapes=[pltpu.VMEM((2, tm, d // 2), jnp.uint32), pltpu.SemaphoreType.DMA((2,))],
        compiler_params=_cparams("arbitrary"),
        name="merge_router",
    )(*[o.reshape(n, GROUP_W) for o in outs], h2d, w_out_bf16, norm_w.reshape(1, d), w_router, b_router)


def _row_copy(src_hbm, src_row, dst_ref, dst_row, sem):
    return pltpu.make_async_copy(src_hbm.at[pl.ds(src_row, 1), :], dst_ref.at[pl.ds(dst_row, 1), :], sem)


def _position_kernel(ids_ref, base_ref, pos_ref, run_ref):
    tm = ids_ref.shape[0]

    @pl.when(pl.program_id(0) == 0)
    def _():
        run_ref[...] = jnp.zeros_like(run_ref)

    i1, i2 = ids_ref[:, 0:1], ids_ref[:, 1:2]
    lane = lax.broadcasted_iota(jnp.int32, (tm, ROUTER_LANES), 1)
    hit = ((lane == i1) | (lane == i2)).astype(BF16)
    earlier = (lax.broadcasted_iota(jnp.int32, (tm, tm), 0)
               > lax.broadcasted_iota(jnp.int32, (tm, tm), 1)).astype(BF16)
    rank = jnp.dot(earlier, hit, preferred_element_type=F32) + run_ref[...]
    where_to = rank + base_ref[...]
    p1 = jnp.sum(jnp.where(lane == i1, where_to, 0.0), axis=-1, keepdims=True)
    p2 = jnp.sum(jnp.where(lane == i2, where_to, 0.0), axis=-1, keepdims=True)
    two = lax.broadcasted_iota(jnp.int32, pos_ref.shape, 1)
    pos_ref[...] = jnp.where(two == 0, p1, p2).astype(jnp.int32)
    run_ref[...] += jnp.sum(hit.astype(F32), axis=0, keepdims=True)


def _positions(ids, group_start, tm):
    n = ids.shape[0]
    return pl.pallas_call(
        _position_kernel,
        grid=(n // tm,),
        in_specs=[pl.BlockSpec((tm, MOE_TOPK), lambda i: (i, 0)),
                  pl.BlockSpec((1, ROUTER_LANES), lambda i: (0, 0))],
        out_specs=pl.BlockSpec((tm, MOE_TOPK), lambda i: (i, 0)),
        out_shape=jax.ShapeDtypeStruct((n, MOE_TOPK), jnp.int32),
        scratch_shapes=[pltpu.VMEM((1, ROUTER_LANES), F32)],
        compiler_params=_cparams("arbitrary"),
        name="moe_positions",
    )(ids, group_start)


def _dispatch_kernel(p0_ref, p1_ref, x_ref, xs_in_hbm, xs_hbm, sem):
    del xs_in_hbm
    n = x_ref.shape[0]

    def copies(i):
        return (pltpu.make_async_copy(x_ref.at[i], xs_hbm.at[p0_ref[i]], sem),
                pltpu.make_async_copy(x_ref.at[i], xs_hbm.at[p1_ref[i]], sem))

    def issue(i, c):
        for prio, cp in enumerate(copies(i)):
            cp.start(priority=prio)
        return c

    def drain(i, c):
        for cp in copies(i):
            cp.wait()
        return c

    lax.fori_loop(0, n, issue, 0, unroll=8)
    lax.fori_loop(0, n, drain, 0, unroll=8)


def _dispatch(x, pos0, pos1, n_rows, chunk):
    n, slabs, lanes = x.shape
    smem = lambda: pl.BlockSpec((chunk,), lambda s: (s,), memory_space=pltpu.SMEM)
    return pl.pallas_call(
        _dispatch_kernel,
        grid=(n // chunk,),
        in_specs=[smem(), smem(), pl.BlockSpec((chunk, slabs, lanes), lambda s: (s, 0, 0)),
                  pl.BlockSpec(memory_space=pl.ANY)],
        out_specs=pl.BlockSpec(memory_space=pl.ANY),
        out_shape=jax.ShapeDtypeStruct((n_rows, slabs, lanes), x.dtype),
        input_output_aliases={3: 0},
        scratch_shapes=[pltpu.SemaphoreType.DMA(())],
        compiler_params=_cparams("arbitrary"),
        name="moe_dispatch",
    )(pos0, pos1, x, jnp.zeros((n_rows, slabs, lanes), x.dtype))


def _expert_kernel(te_ref, nt_ref, xs_hbm, wg_ref, wu_ref, wd_ref, o_ref, wg_s, wu_s, wd_s, x_s, sem):
    i = pl.program_id(0)
    tm = o_ref.shape[0]
    slot = i % 2

    def reads(s, tile):
        return [pltpu.make_async_copy(h, v, m) for v, h, m in
                _slab_copies(x_s.at[s], xs_hbm, tile * tm, sem.at[s])]

    @pl.when(i == 0)
    def _():
        for cp in reads(0, 0):
            cp.start()

    @pl.when(i + 1 < pl.num_programs(0))
    def _():
        for cp in reads(1 - slot, i + 1):
            cp.start()

    @pl.when((i == 0) | (te_ref[i] != te_ref[jnp.maximum(i - 1, 0)]))
    def _():
        wg_s[...] = wg_ref[...].astype(BF16)
        wu_s[...] = wu_ref[...].astype(BF16)
        wd_s[...] = wd_ref[...].astype(BF16)

    for cp in reads(slot, i):
        cp.wait()

    @pl.when(i < nt_ref[0])
    def _():
        p = x_s[slot]
        xa = lax.bitcast_convert_type(p & jnp.uint32(0xFFFF0000), F32).astype(BF16)
        xb = lax.bitcast_convert_type(p << 16, F32).astype(BF16)
        x = jnp.concatenate([xa, xb], axis=1)
        he = (_silu(jnp.dot(x, wg_s[...], preferred_element_type=F32))
              * jnp.dot(x, wu_s[...], preferred_element_type=F32))
        o_ref[...] = jnp.dot(he.astype(BF16), wd_s[...], preferred_element_type=F32)

    @pl.when(i >= nt_ref[0])
    def _():
        o_ref[...] = jnp.zeros_like(o_ref)


def _experts(xs, tile_expert, n_tiles_used, wg, wu, wd, layer, tm):
    p, slabs, lanes = xs.shape
    d = 2 * slabs * lanes
    pick = lambda i, te, nt: (layer, te[i], 0, 0)
    grid_spec = pltpu.PrefetchScalarGridSpec(
        num_scalar_prefetch=2,
        grid=(p // tm,),
        in_specs=[pl.BlockSpec(memory_space=pl.ANY),
                  pl.BlockSpec((None, None, d, D_EXPERT), pick),
                  pl.BlockSpec((None, None, d, D_EXPERT), pick),
                  pl.BlockSpec((None, None, D_EXPERT, d), pick)],
        out_specs=pl.BlockSpec((tm, d), lambda i, te, nt: (i, 0)),
        scratch_shapes=[pltpu.VMEM((d, D_EXPERT), BF16), pltpu.VMEM((d, D_EXPERT), BF16),
                        pltpu.VMEM((D_EXPERT, d), BF16), pltpu.VMEM((2, tm, slabs * lanes), xs.dtype),
                        pltpu.SemaphoreType.DMA((2,))],
    )
    return pl.pallas_call(
        _expert_kernel,
        grid_spec=grid_spec,
        out_shape=jax.ShapeDtypeStruct((p, d), F32),
        compiler_params=_cparams("arbitrary"),
        name="moe_experts",
    )(tile_expert, n_tiles_used, xs, wg, wu, wd)


def _combine_kernel(i0_ref, i1_ref, n0_ref, n1_ref, h1_ref, w_ref, fw_ref, ys_hbm, o_ref,
                    buf0, buf1, sem, *, final_norm):
    n = o_ref.shape[0]
    s = pl.program_id(0)
    slot = s % 2

    def issue_into(sl, p0_ref, p1_ref):
        def issue(i, c):
            _row_copy(ys_hbm, p0_ref[i], buf0.at[sl], i, sem.at[sl]).start(priority=0)
            _row_copy(ys_hbm, p1_ref[i], buf1.at[sl], i, sem.at[sl]).start(priority=1)
            return c
        lax.fori_loop(0, n, issue, 0, unroll=8)

    @pl.when(s == 0)
    def _():
        issue_into(0, i0_ref, i1_ref)

    @pl.when(s + 1 < pl.num_programs(0))
    def _():
        issue_into(1 - slot, n0_ref, n1_ref)

    def drain(i, c):
        _row_copy(ys_hbm, 0, buf0.at[slot], i, sem.at[slot]).wait()
        _row_copy(ys_hbm, 0, buf1.at[slot], i, sem.at[slot]).wait()
        return c

    lax.fori_loop(0, n, drain, 0, unroll=8)
    out = h1_ref[...] + w_ref[:, 0:1] * buf0[slot] + w_ref[:, 1:2] * buf1[slot]
    if final_norm:
        out = _rms(out, fw_ref[...])
    o_ref[...] = out


def _combine(ys, pos0, pos1, wts, h1, final_w, final_norm, chunk):
    n, d = h1.shape
    steps = n // chunk
    smem = lambda: pl.BlockSpec((chunk,), lambda s: (s,), memory_space=pltpu.SMEM)
    nxt = lambda: pl.BlockSpec((chunk,), lambda s: (jnp.minimum(s + 1, steps - 1),), memory_space=pltpu.SMEM)
    return pl.pallas_call(
        functools.partial(_combine_kernel, final_norm=final_norm),
        grid=(steps,),
        in_specs=[smem(), smem(), nxt(), nxt(), pl.BlockSpec((chunk, d), lambda s: (s, 0)),
                  pl.BlockSpec((chunk, MOE_TOPK), lambda s: (s, 0)),
                  pl.BlockSpec((1, d), lambda s: (0, 0)), pl.BlockSpec(memory_space=pl.ANY)],
        out_specs=pl.BlockSpec((chunk, d), lambda s: (s, 0)),
        out_shape=jax.ShapeDtypeStruct((n, d), F32),
        scratch_shapes=[pltpu.VMEM((2, chunk, d), F32), pltpu.VMEM((2, chunk, d), F32),
                        pltpu.SemaphoreType.DMA((2,))],
        compiler_params=_cparams("arbitrary"),
        name="moe_combine",
    )(pos0, pos1, pos0, pos1, h1, wts, final_w.reshape(1, d), ys)


def _moe(hn_packed, ids, wts, counts, h1, wg, wu, wd, layer, final_w, final_norm, tm):
    n = h1.shape[0]
    n_tiles = (n * MOE_TOPK) // tm + N_EXPERTS
    cnt = counts[0, :N_EXPERTS].astype(jnp.int32)
    padded = ((cnt + tm - 1) // tm) * tm
    ends = jnp.cumsum(padded)
    tile_start = jnp.arange(n_tiles, dtype=jnp.int32) * tm
    tile_expert = jnp.minimum(jnp.sum(ends[None, :] <= tile_start[:, None], axis=1),
                              N_EXPERTS - 1).astype(jnp.int32)
    n_tiles_used = (ends[-1] // tm).astype(jnp.int32).reshape(1)
    group_start = jnp.zeros((1, ROUTER_LANES), F32).at[0, :N_EXPERTS].set((ends - padded).astype(F32))
    pos = _positions(ids, group_start, _pick_div(n, 512))
    pos0, pos1 = pos[:, 0], pos[:, 1]
    xs = _dispatch(hn_packed, pos0, pos1, n_tiles * tm, _pick_div(n, 1024))
    ys = _experts(xs, tile_expert, n_tiles_used, wg, wu, wd, layer, tm)
    return _combine(ys, pos0, pos1, wts, h1, final_w, final_norm, _pick_div(n, 1024))


def _pick_div(n, pref):
    while n % pref:
        pref //= 2
    return pref


def _pick(n, pref):
    return pref if n % pref == 0 else n


def kernel(x, norm1_w, w_in, hgrn_lb_logits, hgrn_norm_w, s5_lambda_re, s5_lambda_im, s5_log_dt, s5_b_re, s5_b_im, s5_c_re, s5_c_im, s5_d, s5_w_glu, rwkv_mu, rwkv_w0, rwkv_w2, rwkv_a0, rwkv_a2, rwkv_g2, rwkv_k_k, rwkv_k_a, rwkv_r_k, rwkv_v0, rwkv_v1, rwkv_v2, rwkv_ln_w, rwkv_ln_b, lru_conv_w, lru_conv_b, lru_wa, lru_ba, lru_wx, lru_bx, lru_lambda, merge_gain, w_out, norm2_w, moe_coarse_w, moe_coarse_b, moe_fine_w, moe_fine_b, moe_w_gate, moe_w_up, moe_w_down, final_norm_w):
    bsz, t, d = x.shape
    n = bsz * t
    depth = w_in.shape[0]
    rw = dict(rwkv_mu=rwkv_mu, rwkv_w0=rwkv_w0, rwkv_w2=rwkv_w2, rwkv_a0=rwkv_a0, rwkv_a2=rwkv_a2,
              rwkv_g2=rwkv_g2, rwkv_k_k=rwkv_k_k, rwkv_k_a=rwkv_k_a, rwkv_r_k=rwkv_r_k,
              rwkv_v0=rwkv_v0, rwkv_v1=rwkv_v1, rwkv_v2=rwkv_v2, rwkv_ln_w=rwkv_ln_w,
              rwkv_ln_b=rwkv_ln_b)
    lb_all = jnp.cumsum(jax.nn.softmax(hgrn_lb_logits.astype(F32), axis=0), axis=0)
    lb_all = lb_all - lb_all[:1]

    tm_proj = _pick(n, 1024)
    tm_in = _pick_div(t, 512)
    tm_moe = 512
    tb_mix = _pick(t, 256)
    tb_s5 = _pick(t, 512)

    h = x.reshape(n, d)
    v_first = None
    for l in range(depth):
        mg = merge_gain[l].reshape(4, GROUP_W)
        proj, abg, v_first = _in_proj(h, norm1_w[l], w_in[l].astype(BF16), v_first, l, rw, t, tm_in)
        proj3 = proj.reshape(bsz, t, -1)
        abg3 = abg.reshape(bsz, t, -1)
        mats = _s5_matrices(s5_lambda_re[l], s5_lambda_im[l], s5_log_dt[l], s5_b_re[l], s5_b_im[l],
                            s5_c_re[l], s5_c_im[l])
        o_b = _s5(proj3, mats, s5_d[l], s5_w_glu[l].astype(BF16), mg[1], tb_s5)
        parts = [_rwkv_part(proj3, abg3, l, rw, mg[2], tb_mix),
                 _hgrn_part(proj3, lb_all[l], hgrn_norm_w[l], mg[0], tb_mix),
                 _lru_part(proj3, lru_conv_w[l], lru_conv_b[l], _block_diag_weight(lru_wa[l]),
                           lru_ba[l], _block_diag_weight(lru_wx[l]), lru_bx[l], lru_lambda[l],
                           mg[3], tb_mix)]
        outs = _mixers(parts, bsz, t // tb_mix, "mixers")
        o_c, o_a, o_d = outs
        w_router = jnp.concatenate(
            [moe_fine_w[l].transpose(1, 0, 2).reshape(d, N_EXPERTS), moe_coarse_w[l],
             jnp.zeros((d, ROUTER_LANES - N_EXPERTS - MOE_GROUPS), F32)], axis=1)
        b_router = jnp.concatenate(
            [moe_fine_b[l].reshape(N_EXPERTS), moe_coarse_b[l],
             jnp.zeros((ROUTER_LANES - N_EXPERTS - MOE_GROUPS,), F32)]).reshape(1, ROUTER_LANES)
        h1, hn, ids, wts, counts = _merge_router((o_a, o_b, o_c, o_d), h, w_out[l].astype(BF16),
                                                 norm2_w[l], w_router, b_router, tm_proj)
        h = _moe(hn, ids, wts, counts, h1, moe_w_gate, moe_w_up, moe_w_down, l,
                 final_norm_w, l == depth - 1, tm_moe)
    return h.reshape(bsz, t, d)
```
